```python
import jax, jax.numpy as jnp
from jax import lax
import numpy as np

D_MODEL = 1024
BATCH = 8
SEQ = 4096
DEPTH = 2

HEAD_DIM = 64
SGU_WIDTH = 3 * D_MODEL // 8
CONV_WIDTH = 3 * D_MODEL // 8
POOL_WIDTH = D_MODEL - SGU_WIDTH - CONV_WIDTH
SGU_HEADS = SGU_WIDTH // HEAD_DIM
CHUNK = 128
CONV_K = 31
POOL_WINDOWS = (2, 4, 8, 16)
POOL_GROUPS = len(POOL_WINDOWS)
POOL_GDIM = POOL_WIDTH // POOL_GROUPS
IN_WIDTH = 2 * SGU_WIDTH + 2 * CONV_WIDTH + POOL_WIDTH
D_FF = ((8 * D_MODEL // 3 + 127) // 128) * 128
FFN_CONV_K = 3
N_MOD = 6
EPS = 1e-6
MOD_INIT_STD = 0.02

kernel_name = "hybrid_sgu_conformer_pool_convffn_block"


def rms_norm(x, g):
    xf = x.astype(jnp.float32)
    y = xf * lax.rsqrt(jnp.mean(xf * xf, axis=-1, keepdims=True) + EPS)
    return (y * g.astype(jnp.float32)).astype(x.dtype)


def layer_norm(x, g, b):
    xf = x.astype(jnp.float32)
    mu = jnp.mean(xf, axis=-1, keepdims=True)
    var = jnp.mean(jnp.square(xf - mu), axis=-1, keepdims=True)
    y = (xf - mu) * lax.rsqrt(var + EPS)
    return (y * g.astype(jnp.float32) + b.astype(jnp.float32)).astype(x.dtype)


def causal_depthwise_conv(x, w, b):
    k, ch = w.shape
    y = lax.conv_general_dilated(
        x, w[:, None, :].astype(x.dtype), window_strides=(1,), padding=[(k - 1, 0)],
        dimension_numbers=("NWC", "WIO", "NWC"), feature_group_count=ch)
    return y + b


def spatial_gating(z, norm_g, norm_b, w_s, b_s):
    bsz, s, _ = z.shape
    u, v = jnp.split(z, 2, axis=-1)
    v = v.reshape(bsz, s // CHUNK, CHUNK, SGU_HEADS, HEAD_DIM)
    v = layer_norm(v, norm_g.reshape(SGU_HEADS, HEAD_DIM), norm_b.reshape(SGU_HEADS, HEAD_DIM))
    mask = jnp.tril(jnp.ones((CHUNK, CHUNK), dtype=bool))
    w = jnp.where(mask[None], w_s, jnp.zeros_like(w_s))
    f = jnp.einsum("hts,bnshd->bnthd", w, v) + b_s.T[None, None, :, :, None]
    return u * f.reshape(bsz, s, SGU_WIDTH)


def conformer_conv(z, conv_w, conv_b, norm_g, norm_b):
    a, g = jnp.split(z, 2, axis=-1)
    h = a * jax.nn.sigmoid(g)
    h = causal_depthwise_conv(h, conv_w, conv_b)
    return jax.nn.silu(layer_norm(h, norm_g, norm_b))


def multiscale_pool(z, pool_w, pool_scale):
    bsz, s, _ = z.shape
    zf = z.astype(jnp.float32)
    cs = jnp.pad(jnp.cumsum(zf, axis=1), ((0, 0), (1, 0), (0, 0)))
    pos1 = jnp.arange(1, s + 1, dtype=jnp.int32)
    means = []
    for gi, win in enumerate(POOL_WINDOWS):
        sl = slice(gi * POOL_GDIM, (gi + 1) * POOL_GDIM)
        hi = cs[:, 1:, sl]
        lo = jnp.pad(cs[:, : s + 1 - win, sl], ((0, 0), (win - 1, 0), (0, 0)))
        count = jnp.minimum(pos1, win).astype(jnp.float32)[None, :, None]
        means.append((hi - lo) / count)
    d = (jnp.concatenate(means, axis=-1) - zf).astype(z.dtype)
    d = d.reshape(bsz, s, POOL_GROUPS, POOL_GDIM)
    y = jnp.einsum("bsgc,gcd->bsgd", d, pool_w).reshape(bsz, s, POOL_WIDTH)
    return y * pool_scale


def _fwd_setup_inputs(seed: int = 0) -> dict:
    key = jax.random.key(seed)
    ks = jax.random.split(key, 32)
    L, D = DEPTH, D_MODEL
    nrm = lambda k, shape, std: (jax.random.normal(k, shape, jnp.float32) * std)
    gain = lambda k, shape: 1.0 + 0.05 * jax.random.normal(k, shape, jnp.float32)
    return {
        "x": jax.random.normal(ks[0], (BATCH, SEQ, D), jnp.float32),
        "c": jax.random.normal(ks[1], (BATCH, D), jnp.float32),
        "mod_w": nrm(ks[2], (L, D, N_MOD * D), MOD_INIT_STD),
        "mod_b": nrm(ks[3], (L, N_MOD * D), 0.01),
        "mix_pre_g": gain(ks[4], (L, D)),
        "mix_post_g": gain(ks[5], (L, D)),
        "w_in": nrm(ks[6], (L, D, IN_WIDTH), D ** -0.5),
        "sgu_norm_g": gain(ks[7], (L, SGU_WIDTH)),
        "sgu_norm_b": nrm(ks[8], (L, SGU_WIDTH), 0.02),
        "sgu_w": nrm(ks[9], (L, SGU_HEADS, CHUNK, CHUNK), CHUNK ** -0.5),
        "sgu_b": gain(ks[10], (L, SGU_HEADS, CHUNK)),
        "conv_w": nrm(ks[11], (L, CONV_K, CONV_WIDTH), CONV_K ** -0.5),
        "conv_b": nrm(ks[12], (L, CONV_WIDTH), 0.02),
        "conv_norm_g": gain(ks[13], (L, CONV_WIDTH)),
        "conv_norm_b": nrm(ks[14], (L, CONV_WIDTH), 0.02),
        "pool_w": nrm(ks[15], (L, POOL_GROUPS, POOL_GDIM, POOL_GDIM), POOL_GDIM ** -0.5),
        "pool_scale": gain(ks[16], (L, POOL_WIDTH)),
        "branch_g": gain(ks[17], (L, D)),
        "w_out": nrm(ks[18], (L, D, D), D ** -0.5),
        "ffn_pre_g": gain(ks[19], (L, D)),
        "ffn_post_g": gain(ks[20], (L, D)),
        "ffn_up": nrm(ks[21], (L, D, 2 * D_FF), D ** -0.5),
        "ffn_conv_w": nrm(ks[22], (L, FFN_CONV_K, 2 * D_FF), FFN_CONV_K ** -0.5),
        "ffn_conv_b": nrm(ks[23], (L, 2 * D_FF), 0.02),
        "ffn_down": nrm(ks[24], (L, D_FF, D), D_FF ** -0.5),
    }


def _fwd_reference(x, c, mod_w, mod_b, mix_pre_g, mix_post_g, w_in, sgu_norm_g, sgu_norm_b, sgu_w, sgu_b,
              conv_w, conv_b, conv_norm_g, conv_norm_b, pool_w, pool_scale, branch_g, w_out,
              ffn_pre_g, ffn_post_g, ffn_up, ffn_conv_w, ffn_conv_b, ffn_down):
    sc = jax.nn.silu(c)
    for l in range(DEPTH):
        mod = sc @ mod_w[l] + mod_b[l]
        sh1, sc1, g1, sh2, sc2, g2 = [m[:, None, :] for m in jnp.split(mod, N_MOD, axis=-1)]

        h = rms_norm(x, mix_pre_g[l]) * (1.0 + sc1) + sh1
        z = h @ w_in[l]
        z_a, z_b, z_c = jnp.split(z, [2 * SGU_WIDTH, 2 * SGU_WIDTH + 2 * CONV_WIDTH], axis=-1)
        y_a = spatial_gating(jax.nn.gelu(z_a), sgu_norm_g[l], sgu_norm_b[l], sgu_w[l], sgu_b[l])
        y_b = conformer_conv(z_b, conv_w[l], conv_b[l], conv_norm_g[l], conv_norm_b[l])
        y_c = multiscale_pool(z_c, pool_w[l], pool_scale[l])
        ga, gb, gc = jnp.split(branch_g[l], [SGU_WIDTH, SGU_WIDTH + CONV_WIDTH])
        y = jnp.concatenate([rms_norm(y_a, ga), rms_norm(y_b, gb), rms_norm(y_c, gc)], axis=-1)
        y = y @ w_out[l]
        x = x + g1 * rms_norm(y, mix_post_g[l])

        h = rms_norm(x, ffn_pre_g[l]) * (1.0 + sc2) + sh2
        u = causal_depthwise_conv(h @ ffn_up[l], ffn_conv_w[l], ffn_conv_b[l])
        ug, uv = jnp.split(u, 2, axis=-1)
        y = (jax.nn.gelu(ug) * uv) @ ffn_down[l]
        x = x + g2 * rms_norm(y, ffn_post_g[l])
    return x


import jax as _jax
import jax.numpy as _jnp

TWIN_FORMAT = 'train_step'
FWD_PARAMS = ['x', 'c', 'mod_w', 'mod_b', 'mix_pre_g', 'mix_post_g', 'w_in', 'sgu_norm_g', 'sgu_norm_b', 'sgu_w', 'sgu_b', 'conv_w', 'conv_b', 'conv_norm_g', 'conv_norm_b', 'pool_w', 'pool_scale', 'branch_g', 'w_out', 'ffn_pre_g', 'ffn_post_g', 'ffn_up', 'ffn_conv_w', 'ffn_conv_b', 'ffn_down']
TWIN_WEIGHTS = ['mod_w', 'mod_b', 'mix_pre_g', 'mix_post_g', 'w_in', 'sgu_norm_g', 'sgu_norm_b', 'sgu_w', 'sgu_b', 'conv_w', 'conv_b', 'conv_norm_g', 'conv_norm_b', 'pool_w', 'pool_scale', 'branch_g', 'w_out', 'ffn_pre_g', 'ffn_post_g', 'ffn_up', 'ffn_conv_w', 'ffn_conv_b', 'ffn_down']
TWIN_DIFF_INPUT = 'x'
TWIN_INPUTS = ['x', 'c', 'mod_w', 'mod_b', 'mix_pre_g', 'mix_post_g', 'w_in', 'sgu_norm_g', 'sgu_norm_b', 'sgu_w', 'sgu_b', 'conv_w', 'conv_b', 'conv_norm_g', 'conv_norm_b', 'pool_w', 'pool_scale', 'branch_g', 'w_out', 'ffn_pre_g', 'ffn_post_g', 'ffn_up', 'ffn_conv_w', 'ffn_conv_b', 'ffn_down', 'loss_target', 'm_mod_w', 'm_mod_b', 'm_mix_pre_g', 'm_mix_post_g', 'm_w_in', 'm_sgu_norm_g', 'm_sgu_norm_b', 'm_sgu_w', 'm_sgu_b', 'm_conv_w', 'm_conv_b', 'm_conv_norm_g', 'm_conv_norm_b', 'm_pool_w', 'm_pool_scale', 'm_branch_g', 'm_w_out', 'm_ffn_pre_g', 'm_ffn_post_g', 'm_ffn_up', 'm_ffn_conv_w', 'm_ffn_conv_b', 'm_ffn_down', 'v_mod_w', 'v_mod_b', 'v_mix_pre_g', 'v_mix_post_g', 'v_w_in', 'v_sgu_norm_g', 'v_sgu_norm_b', 'v_sgu_w', 'v_sgu_b', 'v_conv_w', 'v_conv_b', 'v_conv_norm_g', 'v_conv_norm_b', 'v_pool_w', 'v_pool_scale', 'v_branch_g', 'v_w_out', 'v_ffn_pre_g', 'v_ffn_post_g', 'v_ffn_up', 'v_ffn_conv_w', 'v_ffn_conv_b', 'v_ffn_down']
TWIN_OUTPUTS = ['loss', 'grad_x', 'grad_mod_w', 'grad_mod_b', 'grad_mix_pre_g', 'grad_mix_post_g', 'grad_w_in', 'grad_sgu_norm_g', 'grad_sgu_norm_b', 'grad_sgu_w', 'grad_sgu_b', 'grad_conv_w', 'grad_conv_b', 'grad_conv_norm_g', 'grad_conv_norm_b', 'grad_pool_w', 'grad_pool_scale', 'grad_branch_g', 'grad_w_out', 'grad_ffn_pre_g', 'grad_ffn_post_g', 'grad_ffn_up', 'grad_ffn_conv_w', 'grad_ffn_conv_b', 'grad_ffn_down', 'delta_mod_w', 'delta_mod_b', 'delta_mix_pre_g', 'delta_mix_post_g', 'delta_w_in', 'delta_sgu_norm_g', 'delta_sgu_norm_b', 'delta_sgu_w', 'delta_sgu_b', 'delta_conv_w', 'delta_conv_b', 'delta_conv_norm_g', 'delta_conv_norm_b', 'delta_pool_w', 'delta_pool_scale', 'delta_branch_g', 'delta_w_out', 'delta_ffn_pre_g', 'delta_ffn_post_g', 'delta_ffn_up', 'delta_ffn_conv_w', 'delta_ffn_conv_b', 'delta_ffn_down', 'new_m_mod_w', 'new_m_mod_b', 'new_m_mix_pre_g', 'new_m_mix_post_g', 'new_m_w_in', 'new_m_sgu_norm_g', 'new_m_sgu_norm_b', 'new_m_sgu_w', 'new_m_sgu_b', 'new_m_conv_w', 'new_m_conv_b', 'new_m_conv_norm_g', 'new_m_conv_norm_b', 'new_m_pool_w', 'new_m_pool_scale', 'new_m_branch_g', 'new_m_w_out', 'new_m_ffn_pre_g', 'new_m_ffn_post_g', 'new_m_ffn_up', 'new_m_ffn_conv_w', 'new_m_ffn_conv_b', 'new_m_ffn_down', 'new_v_mod_w', 'new_v_mod_b', 'new_v_mix_pre_g', 'new_v_mix_post_g', 'new_v_w_in', 'new_v_sgu_norm_g', 'new_v_sgu_norm_b', 'new_v_sgu_w', 'new_v_sgu_b', 'new_v_conv_w', 'new_v_conv_b', 'new_v_conv_norm_g', 'new_v_conv_norm_b', 'new_v_pool_w', 'new_v_pool_scale', 'new_v_branch_g', 'new_v_w_out', 'new_v_ffn_pre_g', 'new_v_ffn_post_g', 'new_v_ffn_up', 'new_v_ffn_conv_w', 'new_v_ffn_conv_b', 'new_v_ffn_down']
TWIN_LEAF_KINDS = {'loss': 'loss', 'grad_x': 'grad_x', 'grad_mod_w': 'grad_w', 'grad_mod_b': 'grad_w', 'grad_mix_pre_g': 'grad_w', 'grad_mix_post_g': 'grad_w', 'grad_w_in': 'grad_w', 'grad_sgu_norm_g': 'grad_w', 'grad_sgu_norm_b': 'grad_w', 'grad_sgu_w': 'grad_w', 'grad_sgu_b': 'grad_w', 'grad_conv_w': 'grad_w', 'grad_conv_b': 'grad_w', 'grad_conv_norm_g': 'grad_w', 'grad_conv_norm_b': 'grad_w', 'grad_pool_w': 'grad_w', 'grad_pool_scale': 'grad_w', 'grad_branch_g': 'grad_w', 'grad_w_out': 'grad_w', 'grad_ffn_pre_g': 'grad_w', 'grad_ffn_post_g': 'grad_w', 'grad_ffn_up': 'grad_w', 'grad_ffn_conv_w': 'grad_w', 'grad_ffn_conv_b': 'grad_w', 'grad_ffn_down': 'grad_w', 'delta_mod_w': 'delta_w', 'delta_mod_b': 'delta_w', 'delta_mix_pre_g': 'delta_w', 'delta_mix_post_g': 'delta_w', 'delta_w_in': 'delta_w', 'delta_sgu_norm_g': 'delta_w', 'delta_sgu_norm_b': 'delta_w', 'delta_sgu_w': 'delta_w', 'delta_sgu_b': 'delta_w', 'delta_conv_w': 'delta_w', 'delta_conv_b': 'delta_w', 'delta_conv_norm_g': 'delta_w', 'delta_conv_norm_b': 'delta_w', 'delta_pool_w': 'delta_w', 'delta_pool_scale': 'delta_w', 'delta_branch_g': 'delta_w', 'delta_w_out': 'delta_w', 'delta_ffn_pre_g': 'delta_w', 'delta_ffn_post_g': 'delta_w', 'delta_ffn_up': 'delta_w', 'delta_ffn_conv_w': 'delta_w', 'delta_ffn_conv_b': 'delta_w', 'delta_ffn_down': 'delta_w', 'new_m_mod_w': 'new_m', 'new_m_mod_b': 'new_m', 'new_m_mix_pre_g': 'new_m', 'new_m_mix_post_g': 'new_m', 'new_m_w_in': 'new_m', 'new_m_sgu_norm_g': 'new_m', 'new_m_sgu_norm_b': 'new_m', 'new_m_sgu_w': 'new_m', 'new_m_sgu_b': 'new_m', 'new_m_conv_w': 'new_m', 'new_m_conv_b': 'new_m', 'new_m_conv_norm_g': 'new_m', 'new_m_conv_norm_b': 'new_m', 'new_m_pool_w': 'new_m', 'new_m_pool_scale': 'new_m', 'new_m_branch_g': 'new_m', 'new_m_w_out': 'new_m', 'new_m_ffn_pre_g': 'new_m', 'new_m_ffn_post_g': 'new_m', 'new_m_ffn_up': 'new_m', 'new_m_ffn_conv_w': 'new_m', 'new_m_ffn_conv_b': 'new_m', 'new_m_ffn_down': 'new_m', 'new_v_mod_w': 'new_v', 'new_v_mod_b': 'new_v', 'new_v_mix_pre_g': 'new_v', 'new_v_mix_post_g': 'new_v', 'new_v_w_in': 'new_v', 'new_v_sgu_norm_g': 'new_v', 'new_v_sgu_norm_b': 'new_v', 'new_v_sgu_w': 'new_v', 'new_v_sgu_b': 'new_v', 'new_v_conv_w': 'new_v', 'new_v_conv_b': 'new_v', 'new_v_conv_norm_g': 'new_v', 'new_v_conv_norm_b': 'new_v', 'new_v_pool_w': 'new_v', 'new_v_pool_scale': 'new_v', 'new_v_branch_g': 'new_v', 'new_v_w_out': 'new_v', 'new_v_ffn_pre_g': 'new_v', 'new_v_ffn_post_g': 'new_v', 'new_v_ffn_up': 'new_v', 'new_v_ffn_conv_w': 'new_v', 'new_v_ffn_conv_b': 'new_v', 'new_v_ffn_down': 'new_v'}


def _forward(args):
    return _fwd_reference(*[args[k] for k in FWD_PARAMS])


def _output_shape():
    out = _jax.eval_shape(lambda: _forward(_fwd_setup_inputs(0)))
    return out.shape, out.dtype

N_MICROBATCH = 1
ADAM_LR = 0.001
ADAM_B1 = 0.9
ADAM_B2 = 0.999
ADAM_EPS = 1e-08
ADAM_WD = 0.01
ADAM_STEP = 10
PER_EXAMPLE_BATCH_AXIS = {'x': 0, 'c': 0, 'loss_target': 0}
SHARED_INPUTS = []
_WEIGHT_DTYPES = {'mod_w': _jnp.float32, 'mod_b': _jnp.float32, 'mix_pre_g': _jnp.float32, 'mix_post_g': _jnp.float32, 'w_in': _jnp.float32, 'sgu_norm_g': _jnp.float32, 'sgu_norm_b': _jnp.float32, 'sgu_w': _jnp.float32, 'sgu_b': _jnp.float32, 'conv_w': _jnp.float32, 'conv_b': _jnp.float32, 'conv_norm_g': _jnp.float32, 'conv_norm_b': _jnp.float32, 'pool_w': _jnp.float32, 'pool_scale': _jnp.float32, 'branch_g': _jnp.float32, 'w_out': _jnp.float32, 'ffn_pre_g': _jnp.float32, 'ffn_post_g': _jnp.float32, 'ffn_up': _jnp.float32, 'ffn_conv_w': _jnp.float32, 'ffn_conv_b': _jnp.float32, 'ffn_down': _jnp.float32}
MOMENT_SCALE = {'mod_w': 2.318746e+00, 'mod_b': 4.295440e+00, 'mix_pre_g': 2.008964e-01, 'mix_post_g': 6.318420e+00, 'w_in': 2.650040e-01, 'sgu_norm_g': 8.426669e-02, 'sgu_norm_b': 8.053279e-02, 'sgu_w': 5.520223e-02, 'sgu_b': 8.202013e-02, 'conv_w': 4.383236e-01, 'conv_b': 2.069762e+00, 'conv_norm_g': 1.104591e+00, 'conv_norm_b': 1.423903e+00, 'pool_w': 2.181944e-01, 'pool_scale': 2.479624e-01, 'branch_g': 6.551154e-01, 'w_out': 5.991619e-01, 'ffn_pre_g': 2.099040e-01, 'ffn_post_g': 6.105813e+00, 'ffn_up': 1.369049e-01, 'ffn_conv_w': 1.598538e-01, 'ffn_conv_b': 3.538696e-01, 'ffn_down': 3.025907e-01}


def _to_microbatches(a, axis):
    t = _jnp.moveaxis(a, axis, 0)
    t = t.reshape((N_MICROBATCH, t.shape[0] // N_MICROBATCH) + t.shape[1:])
    return _jnp.moveaxis(t, 1, axis + 1)


def setup_inputs(seed: int = 0) -> dict:
    inp = _fwd_setup_inputs(seed)
    key = _jax.random.fold_in(_jax.random.key(seed), 7919)
    shape, _ = _output_shape()
    out = dict(inp)
    out["loss_target"] = _jax.random.normal(_jax.random.fold_in(key, 0), shape, _jnp.float32)
    for i, name in enumerate(TWIN_WEIGHTS):
        w = inp[name].astype(_jnp.float32)
        if MOMENT_SCALE is None:
            s = _jnp.sqrt(_jnp.mean(_jnp.square(w)) + 1e-30)
        else:
            s = MOMENT_SCALE[name]
        km, kv = _jax.random.split(_jax.random.fold_in(key, i + 1))
        out[name] = w
        out["m_" + name] = s * _jax.random.normal(km, w.shape, _jnp.float32)
        out["v_" + name] = (s * s) * _jax.random.uniform(kv, w.shape, _jnp.float32, 0.5, 1.5)
    if N_MICROBATCH > 1:
        for name, axis in PER_EXAMPLE_BATCH_AXIS.items():
            out[name] = _to_microbatches(out[name], axis)
    return {'x': out['x'], 'c': out['c'], 'mod_w': out['mod_w'], 'mod_b': out['mod_b'], 'mix_pre_g': out['mix_pre_g'], 'mix_post_g': out['mix_post_g'], 'w_in': out['w_in'], 'sgu_norm_g': out['sgu_norm_g'], 'sgu_norm_b': out['sgu_norm_b'], 'sgu_w': out['sgu_w'], 'sgu_b': out['sgu_b'], 'conv_w': out['conv_w'], 'conv_b': out['conv_b'], 'conv_norm_g': out['conv_norm_g'], 'conv_norm_b': out['conv_norm_b'], 'pool_w': out['pool_w'], 'pool_scale': out['pool_scale'], 'branch_g': out['branch_g'], 'w_out': out['w_out'], 'ffn_pre_g': out['ffn_pre_g'], 'ffn_post_g': out['ffn_post_g'], 'ffn_up': out['ffn_up'], 'ffn_conv_w': out['ffn_conv_w'], 'ffn_conv_b': out['ffn_conv_b'], 'ffn_down': out['ffn_down'], 'loss_target': out['loss_target'], 'm_mod_w': out['m_mod_w'], 'm_mod_b': out['m_mod_b'], 'm_mix_pre_g': out['m_mix_pre_g'], 'm_mix_post_g': out['m_mix_post_g'], 'm_w_in': out['m_w_in'], 'm_sgu_norm_g': out['m_sgu_norm_g'], 'm_sgu_norm_b': out['m_sgu_norm_b'], 'm_sgu_w': out['m_sgu_w'], 'm_sgu_b': out['m_sgu_b'], 'm_conv_w': out['m_conv_w'], 'm_conv_b': out['m_conv_b'], 'm_conv_norm_g': out['m_conv_norm_g'], 'm_conv_norm_b': out['m_conv_norm_b'], 'm_pool_w': out['m_pool_w'], 'm_pool_scale': out['m_pool_scale'], 'm_branch_g': out['m_branch_g'], 'm_w_out': out['m_w_out'], 'm_ffn_pre_g': out['m_ffn_pre_g'], 'm_ffn_post_g': out['m_ffn_post_g'], 'm_ffn_up': out['m_ffn_up'], 'm_ffn_conv_w': out['m_ffn_conv_w'], 'm_ffn_conv_b': out['m_ffn_conv_b'], 'm_ffn_down': out['m_ffn_down'], 'v_mod_w': out['v_mod_w'], 'v_mod_b': out['v_mod_b'], 'v_mix_pre_g': out['v_mix_pre_g'], 'v_mix_post_g': out['v_mix_post_g'], 'v_w_in': out['v_w_in'], 'v_sgu_norm_g': out['v_sgu_norm_g'], 'v_sgu_norm_b': out['v_sgu_norm_b'], 'v_sgu_w': out['v_sgu_w'], 'v_sgu_b': out['v_sgu_b'], 'v_conv_w': out['v_conv_w'], 'v_conv_b': out['v_conv_b'], 'v_conv_norm_g': out['v_conv_norm_g'], 'v_conv_norm_b': out['v_conv_norm_b'], 'v_pool_w': out['v_pool_w'], 'v_pool_scale': out['v_pool_scale'], 'v_branch_g': out['v_branch_g'], 'v_w_out': out['v_w_out'], 'v_ffn_pre_g': out['v_ffn_pre_g'], 'v_ffn_post_g': out['v_ffn_post_g'], 'v_ffn_up': out['v_ffn_up'], 'v_ffn_conv_w': out['v_ffn_conv_w'], 'v_ffn_conv_b': out['v_ffn_conv_b'], 'v_ffn_down': out['v_ffn_down']}


def _loss(weights, diff, rest, loss_target):
    with _jax.named_scope("forward"):
        args = {**rest, TWIN_DIFF_INPUT: diff, **{k: w.astype(_WEIGHT_DTYPES[k]) for k, w in weights.items()}}
        y = _forward(args)
    with _jax.named_scope("loss_head"):
        err = _jnp.square(y.astype(_jnp.float32) - loss_target)
        return 0.5 * _jnp.sum(_jnp.mean(err, axis=-1)) if err.ndim else 0.5 * err


def _adamw(w, g, m, v):
    m = ADAM_B1 * m + (1.0 - ADAM_B1) * g
    v = ADAM_B2 * v + (1.0 - ADAM_B2) * _jnp.square(g)
    m_hat = m / (1.0 - ADAM_B1 ** ADAM_STEP)
    v_hat = v / (1.0 - ADAM_B2 ** ADAM_STEP)
    delta = -ADAM_LR * (m_hat / (_jnp.sqrt(v_hat) + ADAM_EPS) + ADAM_WD * w)
    return delta, m, v


def reference(x, c, mod_w, mod_b, mix_pre_g, mix_post_g, w_in, sgu_norm_g, sgu_norm_b, sgu_w, sgu_b, conv_w, conv_b, conv_norm_g, conv_norm_b, pool_w, pool_scale, branch_g, w_out, ffn_pre_g, ffn_post_g, ffn_up, ffn_conv_w, ffn_conv_b, ffn_down, loss_target, m_mod_w, m_mod_b, m_mix_pre_g, m_mix_post_g, m_w_in, m_sgu_norm_g, m_sgu_norm_b, m_sgu_w, m_sgu_b, m_conv_w, m_conv_b, m_conv_norm_g, m_conv_norm_b, m_pool_w, m_pool_scale, m_branch_g, m_w_out, m_ffn_pre_g, m_ffn_post_g, m_ffn_up, m_ffn_conv_w, m_ffn_conv_b, m_ffn_down, v_mod_w, v_mod_b, v_mix_pre_g, v_mix_post_g, v_w_in, v_sgu_norm_g, v_sgu_norm_b, v_sgu_w, v_sgu_b, v_conv_w, v_conv_b, v_conv_norm_g, v_conv_norm_b, v_pool_w, v_pool_scale, v_branch_g, v_w_out, v_ffn_pre_g, v_ffn_post_g, v_ffn_up, v_ffn_conv_w, v_ffn_conv_b, v_ffn_down):
    given = dict(x=x, c=c, mod_w=mod_w, mod_b=mod_b, mix_pre_g=mix_pre_g, mix_post_g=mix_post_g, w_in=w_in, sgu_norm_g=sgu_norm_g, sgu_norm_b=sgu_norm_b, sgu_w=sgu_w, sgu_b=sgu_b, conv_w=conv_w, conv_b=conv_b, conv_norm_g=conv_norm_g, conv_norm_b=conv_norm_b, pool_w=pool_w, pool_scale=pool_scale, branch_g=branch_g, w_out=w_out, ffn_pre_g=ffn_pre_g, ffn_post_g=ffn_post_g, ffn_up=ffn_up, ffn_conv_w=ffn_conv_w, ffn_conv_b=ffn_conv_b, ffn_down=ffn_down, loss_target=loss_target, m_mod_w=m_mod_w, m_mod_b=m_mod_b, m_mix_pre_g=m_mix_pre_g, m_mix_post_g=m_mix_post_g, m_w_in=m_w_in, m_sgu_norm_g=m_sgu_norm_g, m_sgu_norm_b=m_sgu_norm_b, m_sgu_w=m_sgu_w, m_sgu_b=m_sgu_b, m_conv_w=m_conv_w, m_conv_b=m_conv_b, m_conv_norm_g=m_conv_norm_g, m_conv_norm_b=m_conv_norm_b, m_pool_w=m_pool_w, m_pool_scale=m_pool_scale, m_branch_g=m_branch_g, m_w_out=m_w_out, m_ffn_pre_g=m_ffn_pre_g, m_ffn_post_g=m_ffn_post_g, m_ffn_up=m_ffn_up, m_ffn_conv_w=m_ffn_conv_w, m_ffn_conv_b=m_ffn_conv_b, m_ffn_down=m_ffn_down, v_mod_w=v_mod_w, v_mod_b=v_mod_b, v_mix_pre_g=v_mix_pre_g, v_mix_post_g=v_mix_post_g, v_w_in=v_w_in, v_sgu_norm_g=v_sgu_norm_g, v_sgu_norm_b=v_sgu_norm_b, v_sgu_w=v_sgu_w, v_sgu_b=v_sgu_b, v_conv_w=v_conv_w, v_conv_b=v_conv_b, v_conv_norm_g=v_conv_norm_g, v_conv_norm_b=v_conv_norm_b, v_pool_w=v_pool_w, v_pool_scale=v_pool_scale, v_branch_g=v_branch_g, v_w_out=v_w_out, v_ffn_pre_g=v_ffn_pre_g, v_ffn_post_g=v_ffn_post_g, v_ffn_up=v_ffn_up, v_ffn_conv_w=v_ffn_conv_w, v_ffn_conv_b=v_ffn_conv_b, v_ffn_down=v_ffn_down)
    weights = {n: given[n] for n in TWIN_WEIGHTS}
    shared = {n: given[n] for n in SHARED_INPUTS}
    per_example = {n: given[n] for n in ['x', 'c']}
    grad_fn = _jax.value_and_grad(_loss, argnums=(0, 1))

    def one_microbatch(ex, loss_target):
        ex = dict(ex)
        diff = ex.pop(TWIN_DIFF_INPUT)
        return grad_fn(weights, diff, {**shared, **ex}, loss_target)

    if N_MICROBATCH == 1:
        loss, (grad_w, grad_x) = one_microbatch(per_example, given["loss_target"])
    else:
        def body(carry, xs):
            loss_sum, grad_sum = carry
            l_k, (gw_k, gx_k) = one_microbatch(xs[0], xs[1])
            with _jax.named_scope("update"):
                return (loss_sum + l_k, _jax.tree.map(_jnp.add, grad_sum, gw_k)), gx_k

        init = (_jnp.zeros((), _jnp.float32), _jax.tree.map(_jnp.zeros_like, weights))
        (loss, grad_w), grad_x = _jax.lax.scan(body, init, (per_example, given["loss_target"]))
    with _jax.named_scope("update"):
        delta_w, new_m, new_v = {}, {}, {}
        for n in TWIN_WEIGHTS:
            delta_w[n], new_m[n], new_v[n] = _adamw(weights[n], grad_w[n], given["m_" + n], given["v_" + n])
    return (loss, grad_x, *[grad_w[n] for n in TWIN_WEIGHTS], *[delta_w[n] for n in TWIN_WEIGHTS],
            *[new_m[n] for n in TWIN_WEIGHTS], *[new_v[n] for n in TWIN_WEIGHTS])
```

```python
import functools
import math

import jax
import jax.numpy as jnp
from jax import lax
from jax.experimental import pallas as pl
from jax.experimental.pallas import tpu as pltpu

F32 = jnp.float32
BF16 = jnp.bfloat16

D_MODEL = 1024
N_DEV = 8
SGU_WIDTH = 384
CONV_WIDTH = 384
POOL_WIDTH = 256
HEAD_DIM = 64
SGU_HEADS = 6
CHUNK = 128
CONV_K = 31
POOL_WINDOWS = (2, 4, 8, 16)
IN_WIDTH = 1792
D_FF = 2816
FF_SHARD = 2 * D_FF // N_DEV
FF_PAIRS = N_DEV // 2
FFN_CONV_K = 3
EPS = 1e-6
GELU_C0 = math.sqrt(2.0 / math.pi)
GELU_C1 = 0.044715

ADAM_LR = 0.001
ADAM_B1 = 0.9
ADAM_B2 = 0.999
ADAM_EPS = 1e-08
ADAM_WD = 0.01
ADAM_STEP = 10

VMEM_LIMIT_BYTES = 56 * 1024 * 1024
HALO = 32
MIX_TILE = 256
FFN_TILE = 256
WGRAD_TK = 512


def _cparams(n_axes):
    return pltpu.CompilerParams(dimension_semantics=("arbitrary",) * n_axes, vmem_limit_bytes=VMEM_LIMIT_BYTES)


def _whole(shape):
    nd = len(shape)
    return pl.BlockSpec(shape, lambda *_: (0,) * nd, pipeline_mode=pl.Buffered(1))


def _dot(a, b):
    return jnp.dot(a, b, preferred_element_type=F32)


def _dot_nt(a, b):
    return lax.dot_general(a, b, (((1,), (1,)), ((), ())), preferred_element_type=F32)


def _dot_tn(a, b):
    return lax.dot_general(a, b, (((0,), (0,)), ((), ())), preferred_element_type=F32)


def _gelu(x):
    t = jnp.tanh(GELU_C0 * (x + GELU_C1 * x * x * x))
    return 0.5 * x * (1.0 + t), t


def _gelu_grad(x, t):
    return 0.5 * (1.0 + t) + 0.5 * x * (1.0 - t * t) * (GELU_C0 * (1.0 + 3.0 * GELU_C1 * x * x))


def _rowmean(x):
    return jnp.mean(x, axis=-1, keepdims=True)


def _colsum(x):
    return jnp.sum(x, axis=0, keepdims=True)


def _rms_fwd(x):
    r = lax.rsqrt(_rowmean(x * x) + EPS)
    return x * r, r


def _rms_bwd(dxhat, xhat, r):
    return r * (dxhat - xhat * _rowmean(dxhat * xhat))


def _seg_mean(x, segp):
    hi = x.astype(BF16)
    lo = (x - hi.astype(F32)).astype(BF16)
    return _dot(hi, segp) + _dot(lo, segp)


def _ffn_fwd(x1, modv, g1024, wup, wdn, cw, cb, name="ffn_fwd"):
    s_len = x1.shape[0]
    t = FFN_TILE
    n_tiles = s_len // t

    def body(x1_ref, mod_ref, g_ref, wup_ref, wdn_ref, cw_ref, cb_ref, x2_ref, y2_ref, p_ref, ext_ref, carry_ref):
        i = pl.program_id(0)

        @pl.when(i == 0)
        def _():
            carry_ref[...] = jnp.zeros_like(carry_ref)

        x1v = x1_ref[...]
        pre_g, post_g = g_ref[3:4, :], g_ref[4:5, :]
        sh2, sc2, g2 = mod_ref[3:4, :], mod_ref[4:5, :], mod_ref[5:6, :]
        xhat, _ = _rms_fwd(x1v)
        h2b = (xhat * pre_g * (1.0 + sc2) + sh2).astype(BF16)

        def conv_shard(s):
            p = _dot(h2b, wup_ref[s])
            p_ref[s] = p.astype(BF16)
            ext_ref[0:8, :] = carry_ref[s]
            ext_ref[8:8 + t, :] = p
            carry_ref[s] = p[t - 8:t, :]
            w = cw_ref[s]
            return w[0:1, :] * ext_ref[6:6 + t, :] + w[1:2, :] * ext_ref[7:7 + t, :] + w[2:3, :] * p + cb_ref[s]

        y2 = jnp.zeros((t, D_MODEL), F32)
        for j in range(FF_PAIRS):
            ug = conv_shard(j)
            uv = conv_shard(j + FF_PAIRS)
            ge, _ = _gelu(ug)
            y2 = y2 + _dot((ge * uv).astype(BF16), wdn_ref[j])
        y2_ref[...] = y2
        yhat, _ = _rms_fwd(y2)
        x2_ref[...] = x1v + g2 * (yhat * post_g)

    tile = pl.BlockSpec((t, D_MODEL), lambda i: (i, 0))
    return pl.pallas_call(
        body,
        grid=(n_tiles,),
        in_specs=[tile, _whole(modv.shape), _whole(g1024.shape), _whole(wup.shape), _whole(wdn.shape),
                  _whole(cw.shape), _whole(cb.shape)],
        out_specs=[tile, tile, pl.BlockSpec((N_DEV, t, FF_SHARD), lambda i: (0, i, 0))],
        out_shape=[jax.ShapeDtypeStruct((s_len, D_MODEL), F32), jax.ShapeDtypeStruct((s_len, D_MODEL), F32),
                   jax.ShapeDtypeStruct((N_DEV, s_len, FF_SHARD), BF16)],
        scratch_shapes=[pltpu.VMEM((8 + t, FF_SHARD), F32), pltpu.VMEM((N_DEV, 8, FF_SHARD), F32)],
        compiler_params=_cparams(1),
        name=name,
    )(x1, modv, g1024, wup, wdn, cw, cb)


def _ffn_bwd(dx2, x1, y2, p, modv, g1024, wup, wdn, cw, cb, name="ffn_bwd"):
    s_len = x1.shape[0]
    t = FFN_TILE
    n_tiles = s_len // t
    hb = 16

    def body(dx2_ref, x1_ref, y2_ref, p_ref, ph_ref, mod_ref, g_ref, wup_ref, wdn_ref, cw_ref, cb_ref,
             dx1_ref, dp_ref, a_ref, dy2_ref, h2_ref, vec_ref, cgrad_ref, ext_ref, dext_ref, dcarry_ref):
        i = pl.program_id(0)
        tile_idx = n_tiles - 1 - i

        @pl.when(i == 0)
        def _():
            vec_ref[...] = jnp.zeros_like(vec_ref)
            cgrad_ref[...] = jnp.zeros_like(cgrad_ref)
            dcarry_ref[...] = jnp.zeros_like(dcarry_ref)

        dx2v, x1v, y2v = dx2_ref[...], x1_ref[...], y2_ref[...]
        pre_g, post_g = g_ref[3:4, :], g_ref[4:5, :]
        sh2, sc2, g2 = mod_ref[3:4, :], mod_ref[4:5, :], mod_ref[5:6, :]

        yhat, ry = _rms_fwd(y2v)
        vec_ref[1:2, :] += _colsum(dx2v * (yhat * post_g))
        dyn = dx2v * g2
        vec_ref[0:1, :] += _colsum(dyn * yhat)
        dy2b = _rms_bwd(dyn * post_g, yhat, ry).astype(BF16)
        dy2_ref[...] = dy2b

        xhat, rx = _rms_fwd(x1v)
        xn = xhat * pre_g
        h2_ref[...] = (xn * (1.0 + sc2) + sh2).astype(BF16)

        not_first = (tile_idx > 0).astype(F32)

        def recompute(s, slot):
            pf = p_ref[s].astype(F32)
            ext_ref[slot, 0:8, :] = ph_ref[s][hb - 8:hb, :].astype(F32) * not_first
            ext_ref[slot, 8:8 + t, :] = pf
            w = cw_ref[s]
            u = (w[0:1, :] * ext_ref[slot, 6:6 + t, :] + w[1:2, :] * ext_ref[slot, 7:7 + t, :]
                 + w[2:3, :] * pf + cb_ref[s])
            return u

        def conv_bwd(s, slot, du):
            w = cw_ref[s]
            cgrad_ref[s, 0:1, :] += _colsum(du * ext_ref[slot, 6:6 + t, :])
            cgrad_ref[s, 1:2, :] += _colsum(du * ext_ref[slot, 7:7 + t, :])
            cgrad_ref[s, 2:3, :] += _colsum(du * ext_ref[slot, 8:8 + t, :])
            cgrad_ref[s, 3:4, :] += _colsum(du)
            dext_ref[0:t, :] = du
            dext_ref[t:t + 8, :] = dcarry_ref[s]
            dcarry_ref[s] = du[0:8, :]
            dp = w[2:3, :] * du + w[1:2, :] * dext_ref[1:1 + t, :] + w[0:1, :] * dext_ref[2:2 + t, :]
            dpb = dp.astype(BF16)
            dp_ref[s] = dpb
            return _dot_nt(dpb, wup_ref[s])

        dh2 = jnp.zeros((t, D_MODEL), F32)
        for j in range(FF_PAIRS):
            ug = recompute(j, 0)
            uv = recompute(j + FF_PAIRS, 1)
            ge, th = _gelu(ug)
            a_ref[j] = (ge * uv).astype(BF16)
            da = _dot_nt(dy2b, wdn_ref[j])
            dh2 = dh2 + conv_bwd(j, 0, da * uv * _gelu_grad(ug, th))
            dh2 = dh2 + conv_bwd(j + FF_PAIRS, 1, da * ge)

        vec_ref[2:3, :] += _colsum(dh2)
        vec_ref[3:4, :] += _colsum(dh2 * xn)
        dxn = dh2 * (1.0 + sc2)
        vec_ref[4:5, :] += _colsum(dxn * xhat)
        dx1_ref[...] = dx2v + _rms_bwd(dxn * pre_g, xhat, rx)

    rev = lambda i: (n_tiles - 1 - i, 0)
    tile = pl.BlockSpec((t, D_MODEL), rev)
    halo_idx = lambda i: (0, jnp.maximum((n_tiles - 1 - i) * (t // hb) - 1, 0), 0)
    return pl.pallas_call(
        body,
        grid=(n_tiles,),
        in_specs=[tile, tile, tile,
                  pl.BlockSpec((N_DEV, t, FF_SHARD), lambda i: (0, n_tiles - 1 - i, 0)),
                  pl.BlockSpec((N_DEV, hb, FF_SHARD), halo_idx),
                  _whole(modv.shape), _whole(g1024.shape), _whole(wup.shape), _whole(wdn.shape),
                  _whole(cw.shape), _whole(cb.shape)],
        out_specs=[tile,
                   pl.BlockSpec((N_DEV, t, FF_SHARD), lambda i: (0, n_tiles - 1 - i, 0)),
                   pl.BlockSpec((FF_PAIRS, t, FF_SHARD), lambda i: (0, n_tiles - 1 - i, 0)),
                   tile, tile,
                   pl.BlockSpec((8, D_MODEL), lambda i: (0, 0)),
                   pl.BlockSpec((N_DEV, 8, FF_SHARD), lambda i: (0, 0, 0))],
        out_shape=[jax.ShapeDtypeStruct((s_len, D_MODEL), F32),
                   jax.ShapeDtypeStruct((N_DEV, s_len, FF_SHARD), BF16),
                   jax.ShapeDtypeStruct((FF_PAIRS, s_len, FF_SHARD), BF16),
                   jax.ShapeDtypeStruct((s_len, D_MODEL), BF16),
                   jax.ShapeDtypeStruct((s_len, D_MODEL), BF16),
                   jax.ShapeDtypeStruct((8, D_MODEL), F32),
                   jax.ShapeDtypeStruct((N_DEV, 8, FF_SHARD), F32)],
        scratch_shapes=[pltpu.VMEM((2, 8 + t, FF_SHARD), F32), pltpu.VMEM((t + 8, FF_SHARD), F32),
                        pltpu.VMEM((N_DEV, 8, FF_SHARD), F32)],
        compiler_params=_cparams(1),
        name=name,
    )(dx2, x1, y2, p, p, modv, g1024, wup, wdn, cw, cb)


def _wgrad(a, b, name):
    a_grouped, b_grouped = a.ndim == 3, b.ndim == 3
    groups = a.shape[0] if a_grouped else b.shape[0]
    s_len, m, n = a.shape[-2], a.shape[-1], b.shape[-1]
    tk = min(WGRAD_TK, s_len)
    n_k = s_len // tk

    def body(a_ref, b_ref, o_ref, acc_ref):
        k = pl.program_id(1)
        av = a_ref[0] if a_grouped else a_ref[...]
        bv = b_ref[0] if b_grouped else b_ref[...]
        part = _dot_tn(av, bv)

        @pl.when(k == 0)
        def _():
            acc_ref[...] = part

        @pl.when(k > 0)
        def _():
            acc_ref[...] += part

        @pl.when(k == n_k - 1)
        def _():
            o_ref[0] = acc_ref[...].astype(BF16)

    a_spec = pl.BlockSpec((1, tk, m), lambda g, k: (g, k, 0)) if a_grouped else pl.BlockSpec((tk, m), lambda g, k: (k, 0))
    b_spec = pl.BlockSpec((1, tk, n), lambda g, k: (g, k, 0)) if b_grouped else pl.BlockSpec((tk, n), lambda g, k: (k, 0))
    return pl.pallas_call(
        body,
        grid=(groups, n_k),
        in_specs=[a_spec, b_spec],
        out_specs=pl.BlockSpec((1, m, n), lambda g, k: (g, 0, 0)),
        out_shape=jax.ShapeDtypeStruct((groups, m, n), BF16),
        scratch_shapes=[pltpu.VMEM((m, n), F32)],
        compiler_params=_cparams(2),
        name=name,
    )(a, b)


def _lane(shape):
    return lax.broadcasted_iota(jnp.int32, shape, 1)


def _by_pool_group(shape, vals):
    lane = _lane(shape)
    return jnp.where(lane < 64, vals[0], jnp.where(lane < 128, vals[1], jnp.where(lane < 192, vals[2], vals[3])))


def _pool_inv_counts(t, tile_idx):
    pos1 = lax.broadcasted_iota(jnp.int32, (t, 1), 0) + tile_idx * t + 1
    return [1.0 / jnp.minimum(pos1, w).astype(F32) for w in POOL_WINDOWS]


def _masked_sgu_w(sguw_ref):
    row = lax.broadcasted_iota(jnp.int32, (CHUNK, CHUNK), 0)
    col = lax.broadcasted_iota(jnp.int32, (CHUNK, CHUNK), 1)
    return [jnp.where(row >= col, sguw_ref[h], 0.0).astype(BF16) for h in range(SGU_HEADS)]


def _branches_fwd(z, tile_idx, p384_ref, cw_ref, wm, bmat_ref, pwbd_ref, psc_ref, segp_ref, g_ref, hext_ref, zext_ref):
    t = z.shape[0]
    segp = segp_ref[...]
    r = {}
    u, _ = _gelu(z[:, 0:SGU_WIDTH])
    vraw, _ = _gelu(z[:, SGU_WIDTH:2 * SGU_WIDTH])
    xc = vraw - _seg_mean(vraw, segp)
    rstd_v = lax.rsqrt(_seg_mean(xc * xc, segp) + EPS)
    xh_v = xc * rstd_v
    vnb = (xh_v * p384_ref[0:1, :] + p384_ref[1:2, :]).astype(BF16)
    first_head = _lane((CHUNK, CHUNK)) < HEAD_DIM
    f_rows = []
    for c in range(t // CHUNK):
        f_pairs = []
        for pr in range(SGU_HEADS // 2):
            vp = vnb[c * CHUNK:(c + 1) * CHUNK, pr * 128:(pr + 1) * 128]
            f_pairs.append(jnp.where(first_head, _dot(wm[2 * pr], vp), _dot(wm[2 * pr + 1], vp)))
        f_rows.append(jnp.concatenate(f_pairs, axis=1) + bmat_ref[...])
    f = jnp.concatenate(f_rows, axis=0)
    ya = u * f
    r.update(u=u, xh_v=xh_v, rstd_v=rstd_v, vnb=vnb, f=f)
    o_b = 2 * SGU_WIDTH
    a_in = z[:, o_b:o_b + CONV_WIDTH]
    sig_g = jax.nn.sigmoid(z[:, o_b + CONV_WIDTH:o_b + 2 * CONV_WIDTH])
    hext_ref[HALO:HALO + t, :] = a_in * sig_g
    conv = jnp.zeros((t, CONV_WIDTH), F32) + p384_ref[2:3, :]
    for k in range(CONV_K):
        conv = conv + cw_ref[k:k + 1, :] * hext_ref[pl.ds(HALO - (CONV_K - 1) + k, t), :]
    cc = conv - _rowmean(conv)
    rstd_c = lax.rsqrt(_rowmean(cc * cc) + EPS)
    xh_c = cc * rstd_c
    cn = xh_c * p384_ref[3:4, :] + p384_ref[4:5, :]
    sig_c = jax.nn.sigmoid(cn)
    yb = cn * sig_c
    r.update(a_in=a_in, sig_g=sig_g, xh_c=xh_c, rstd_c=rstd_c, cn=cn, sig_c=sig_c)
    o_c = o_b + 2 * CONV_WIDTH
    zc = z[:, o_c:o_c + POOL_WIDTH]
    zext_ref[HALO:HALO + t, :] = zc
    sums, acc = [], zc
    for j in range(1, POOL_WINDOWS[-1]):
        acc = acc + zext_ref[pl.ds(HALO - j, t), :]
        if j + 1 in POOL_WINDOWS:
            sums.append(acc)
    inv = _pool_inv_counts(t, tile_idx)
    dpool = _by_pool_group((t, POOL_WIDTH), [s * iv for s, iv in zip(sums, inv)]) - zc
    ycp = _dot(dpool.astype(BF16), pwbd_ref[...])
    yc = ycp * psc_ref[0:1, :]
    r.update(dpool=dpool, ycp=ycp)
    yha, ra = _rms_fwd(ya)
    yhb, rb = _rms_fwd(yb)
    yhc, rc = _rms_fwd(yc)
    bg = g_ref[2:3, :]
    ycat = jnp.concatenate([yha * bg[:, 0:384], yhb * bg[:, 384:768], yhc * bg[:, 768:1024]], axis=1)
    r.update(yha=yha, ra=ra, yhb=yhb, rb=rb, yhc=yhc, rc=rc, ycat=ycat)
    return r


def _mixer_fwd(x, modv, g1024, p384, cw, sguw, bmat, pwbd, psc, segp, win, wout, name="mixer_fwd"):
    s_len = x.shape[0]
    t = MIX_TILE

    def body(x_ref, mod_ref, g_ref, p384_ref, cw_ref, sguw_ref, bmat_ref, pwbd_ref, psc_ref, segp_ref, win_ref, wout_ref,
             x1_ref, z_ref, o_ref, hext_ref, zext_ref):
        i = pl.program_id(0)

        @pl.when(i == 0)
        def _():
            hext_ref[0:HALO, :] = jnp.zeros((HALO, CONV_WIDTH), F32)
            zext_ref[0:HALO, :] = jnp.zeros((HALO, POOL_WIDTH), F32)

        xv = x_ref[...]
        sh1, sc1, g1 = mod_ref[0:1, :], mod_ref[1:2, :], mod_ref[2:3, :]
        xhat, _ = _rms_fwd(xv)
        h1 = xhat * g_ref[0:1, :] * (1.0 + sc1) + sh1
        z = _dot(h1.astype(BF16), win_ref[...])
        z_ref[...] = z
        r = _branches_fwd(z, i, p384_ref, cw_ref, _masked_sgu_w(sguw_ref), bmat_ref, pwbd_ref, psc_ref, segp_ref,
                          g_ref, hext_ref, zext_ref)
        o = _dot(r["ycat"].astype(BF16), wout_ref[...])
        o_ref[...] = o
        ohat, _ = _rms_fwd(o)
        x1_ref[...] = xv + g1 * (ohat * g_ref[1:2, :])
        hext_ref[0:HALO, :] = hext_ref[t:t + HALO, :]
        zext_ref[0:HALO, :] = zext_ref[t:t + HALO, :]

    tile = pl.BlockSpec((t, D_MODEL), lambda i: (i, 0))
    consts = (modv, g1024, p384, cw, sguw, bmat, pwbd, psc, segp, win, wout)
    return pl.pallas_call(
        body,
        grid=(s_len // t,),
        in_specs=[tile] + [_whole(c.shape) for c in consts],
        out_specs=[tile, pl.BlockSpec((t, IN_WIDTH), lambda i: (i, 0)), tile],
        out_shape=[jax.ShapeDtypeStruct((s_len, D_MODEL), F32), jax.ShapeDtypeStruct((s_len, IN_WIDTH), F32),
                   jax.ShapeDtypeStruct((s_len, D_MODEL), F32)],
        scratch_shapes=[pltpu.VMEM((HALO + t, CONV_WIDTH), F32), pltpu.VMEM((HALO + t, POOL_WIDTH), F32)],
        compiler_params=_cparams(1),
        name=name,
    )(x, *consts)


def _mixer_bwd(dx1, x, o, z, modv, g1024, p384, cw, sguw, bmat, pwbd, psc, segp, win, wout, name="mixer_bwd"):
    s_len = x.shape[0]
    t = MIX_TILE
    n_tiles = s_len // t

    def body(dx1_ref, x_ref, o_ref, z_ref, zh_ref, mod_ref, g_ref, p384_ref, cw_ref, sguw_ref, bmat_ref, pwbd_ref,
             psc_ref, segp_ref, win_ref, wout_ref,
             dx_ref, dz_ref, do_ref, ycat_ref, h1_ref, vec_ref, v384_ref, dcw_ref, dsguw_ref, dbmat_ref, dpw_ref,
             dpsc_ref, hext_ref, zext_ref, gext_ref, qext_ref):
        i = pl.program_id(0)
        tile_idx = n_tiles - 1 - i

        @pl.when(i == 0)
        def _():
            for ref in (vec_ref, v384_ref, dcw_ref, dsguw_ref, dbmat_ref, dpw_ref, dpsc_ref):
                ref[...] = jnp.zeros_like(ref)
            gext_ref[t:t + HALO, :] = jnp.zeros((HALO, CONV_WIDTH), F32)
            qext_ref[t:t + HALO, :] = jnp.zeros((HALO, POOL_WIDTH), F32)

        dx1v, xv, ov, z = dx1_ref[...], x_ref[...], o_ref[...], z_ref[...]
        sh1, sc1, g1 = mod_ref[0:1, :], mod_ref[1:2, :], mod_ref[2:3, :]
        pre_g, post_g, bg = g_ref[0:1, :], g_ref[1:2, :], g_ref[2:3, :]
        segp = segp_ref[...]

        ohat, ro = _rms_fwd(ov)
        vec_ref[1:2, :] += _colsum(dx1v * (ohat * post_g))
        don = dx1v * g1
        vec_ref[0:1, :] += _colsum(don * ohat)
        dob = _rms_bwd(don * post_g, ohat, ro).astype(BF16)
        do_ref[...] = dob
        dycat = _dot_nt(dob, wout_ref[...])

        not_first = (tile_idx > 0).astype(F32)
        zh = zh_ref[...] * not_first
        o_b = 2 * SGU_WIDTH
        o_c = o_b + 2 * CONV_WIDTH
        hext_ref[0:HALO, :] = zh[:, o_b:o_b + CONV_WIDTH] * jax.nn.sigmoid(zh[:, o_b + CONV_WIDTH:o_c])
        zext_ref[0:HALO, :] = zh[:, o_c:o_c + POOL_WIDTH]
        wm = _masked_sgu_w(sguw_ref)
        r = _branches_fwd(z, tile_idx, p384_ref, cw_ref, wm, bmat_ref, pwbd_ref, psc_ref, segp_ref, g_ref,
                          hext_ref, zext_ref)
        ycat_ref[...] = r["ycat"].astype(BF16)

        def branch_norm_bwd(dyn, yhat, rr, gain):
            return _colsum(dyn * yhat), _rms_bwd(dyn * gain, yhat, rr)

        dga, dya = branch_norm_bwd(dycat[:, 0:384], r["yha"], r["ra"], bg[:, 0:384])
        dgb, dyb = branch_norm_bwd(dycat[:, 384:768], r["yhb"], r["rb"], bg[:, 384:768])
        dgc, dyc = branch_norm_bwd(dycat[:, 768:1024], r["yhc"], r["rc"], bg[:, 768:1024])
        vec_ref[5:6, :] += jnp.concatenate([dga, dgb, dgc], axis=1)

        du_act = dya * r["f"]
        df = dya * r["u"]
        first_head = _lane((CHUNK, CHUNK)) < HEAD_DIM
        wmt = [w.T for w in wm]
        dvn_rows = []
        for c in range(t // CHUNK):
            dfc = df[c * CHUNK:(c + 1) * CHUNK, :]
            dbmat_ref[...] += dfc
            dvn_pairs = []
            for pr in range(SGU_HEADS // 2):
                dfp = dfc[:, pr * 128:(pr + 1) * 128]
                df0 = jnp.where(first_head, dfp, 0.0).astype(BF16)
                df1 = jnp.where(first_head, 0.0, dfp).astype(BF16)
                vp = r["vnb"][c * CHUNK:(c + 1) * CHUNK, pr * 128:(pr + 1) * 128]
                dvn_pairs.append(_dot(wmt[2 * pr], df0) + _dot(wmt[2 * pr + 1], df1))
                dsguw_ref[2 * pr] += _dot_nt(df0, vp)
                dsguw_ref[2 * pr + 1] += _dot_nt(df1, vp)
            dvn_rows.append(jnp.concatenate(dvn_pairs, axis=1))
        dvn = jnp.concatenate(dvn_rows, axis=0)
        v384_ref[0:1, :] += _colsum(dvn * r["xh_v"])
        v384_ref[1:2, :] += _colsum(dvn)
        dxh = dvn * p384_ref[0:1, :]
        dvraw = r["rstd_v"] * (dxh - _seg_mean(dxh, segp) - r["xh_v"] * _seg_mean(dxh * r["xh_v"], segp))
        zu, zv = z[:, 0:SGU_WIDTH], z[:, SGU_WIDTH:o_b]
        _, tu = _gelu(zu)
        _, tv = _gelu(zv)
        dz_u = du_act * _gelu_grad(zu, tu)
        dz_v = dvraw * _gelu_grad(zv, tv)

        cn, sig_c = r["cn"], r["sig_c"]
        dcn = dyb * (sig_c * (1.0 + cn * (1.0 - sig_c)))
        v384_ref[3:4, :] += _colsum(dcn * r["xh_c"])
        v384_ref[4:5, :] += _colsum(dcn)
        dxc = dcn * p384_ref[3:4, :]
        gconv = r["rstd_c"] * (dxc - _rowmean(dxc) - r["xh_c"] * _rowmean(dxc * r["xh_c"]))
        v384_ref[2:3, :] += _colsum(gconv)
        gext_ref[0:t, :] = gconv
        dhh = jnp.zeros((t, CONV_WIDTH), F32)
        for k in range(CONV_K):
            dcw_ref[k:k + 1, :] += _colsum(gconv * hext_ref[pl.ds(HALO - (CONV_K - 1) + k, t), :])
            dhh = dhh + cw_ref[k:k + 1, :] * gext_ref[pl.ds(CONV_K - 1 - k, t), :]
        gext_ref[t:t + HALO, :] = gconv[0:HALO, :]
        sig_g = r["sig_g"]
        dz_a = dhh * sig_g
        dz_g = dhh * r["a_in"] * sig_g * (1.0 - sig_g)

        dpsc_ref[0:1, :] += _colsum(dyc * r["ycp"])
        dycp = (dyc * psc_ref[0:1, :]).astype(BF16)
        dpw_ref[...] += _dot_tn(r["dpool"].astype(BF16), dycp)
        ddp = _dot_nt(dycp, pwbd_ref[...])
        inv = _pool_inv_counts(t, tile_idx)
        q = ddp * _by_pool_group((t, POOL_WIDTH), inv)
        qext_ref[0:t, :] = q
        sums, acc = [], q
        for j in range(1, POOL_WINDOWS[-1]):
            acc = acc + qext_ref[pl.ds(j, t), :]
            if j + 1 in POOL_WINDOWS:
                sums.append(acc)
        qext_ref[t:t + HALO, :] = q[0:HALO, :]
        dz_c = _by_pool_group((t, POOL_WIDTH), sums) - ddp

        dzb = jnp.concatenate([dz_u, dz_v, dz_a, dz_g, dz_c], axis=1).astype(BF16)
        dz_ref[...] = dzb
        dh1 = _dot_nt(dzb, win_ref[...])

        xhat, rx = _rms_fwd(xv)
        xn = xhat * pre_g
        h1_ref[...] = (xn * (1.0 + sc1) + sh1).astype(BF16)
        vec_ref[2:3, :] += _colsum(dh1)
        vec_ref[3:4, :] += _colsum(dh1 * xn)
        dxn = dh1 * (1.0 + sc1)
        vec_ref[4:5, :] += _colsum(dxn * xhat)
        dx_ref[...] = dx1v + _rms_bwd(dxn * pre_g, xhat, rx)

        @pl.when(i == n_tiles - 1)
        def _():
            row = lax.broadcasted_iota(jnp.int32, (CHUNK, CHUNK), 0)
            col = lax.broadcasted_iota(jnp.int32, (CHUNK, CHUNK), 1)
            for h in range(SGU_HEADS):
                dsguw_ref[h] = jnp.where(row >= col, dsguw_ref[h], 0.0)
            dbmat_ref[...] = float(HEAD_DIM) * _seg_mean(dbmat_ref[...], segp)

    rev = lambda i: (n_tiles - 1 - i, 0)
    tile = pl.BlockSpec((t, D_MODEL), rev)
    ztile = pl.BlockSpec((t, IN_WIDTH), rev)
    zhalo = pl.BlockSpec((HALO, IN_WIDTH), lambda i: (jnp.maximum((n_tiles - 1 - i) * (t // HALO) - 1, 0), 0))
    consts = (modv, g1024, p384, cw, sguw, bmat, pwbd, psc, segp, win, wout)
    acc = lambda shape: pl.BlockSpec(shape, lambda i: (0,) * len(shape))
    acc_shapes = [(8, D_MODEL), (8, SGU_WIDTH), (32, CONV_WIDTH), (SGU_HEADS, CHUNK, CHUNK), (CHUNK, SGU_WIDTH),
                  (POOL_WIDTH, POOL_WIDTH), (8, POOL_WIDTH)]
    return pl.pallas_call(
        body,
        grid=(n_tiles,),
        in_specs=[tile, tile, tile, ztile, zhalo] + [_whole(c.shape) for c in consts],
        out_specs=[tile, ztile, tile, tile, tile] + [acc(s) for s in acc_shapes],
        out_shape=[jax.ShapeDtypeStruct((s_len, D_MODEL), F32), jax.ShapeDtypeStruct((s_len, IN_WIDTH), BF16),
                   jax.ShapeDtypeStruct((s_len, D_MODEL), BF16), jax.ShapeDtypeStruct((s_len, D_MODEL), BF16),
                   jax.ShapeDtypeStruct((s_len, D_MODEL), BF16)] + [jax.ShapeDtypeStruct(s, F32) for s in acc_shapes],
        scratch_shapes=[pltpu.VMEM((HALO + t, CONV_WIDTH), F32), pltpu.VMEM((HALO + t, POOL_WIDTH), F32),
                        pltpu.VMEM((t + HALO, CONV_WIDTH), F32), pltpu.VMEM((t + HALO, POOL_WIDTH), F32)],
        compiler_params=_cparams(1),
        name=name,
    )(dx1, x, o, z, z, *consts)


def _loss_grad(xo, target):
    s_len = xo.shape[0]
    t = 512

    def body(x_ref, t_ref, dx_ref, loss_ref):
        i = pl.program_id(0)

        @pl.when(i == 0)
        def _():
            loss_ref[...] = jnp.zeros_like(loss_ref)

        diff = x_ref[...] - t_ref[...]
        dx_ref[...] = diff * (1.0 / D_MODEL)
        loss_ref[...] += (0.5 / D_MODEL) * jnp.sum(diff * diff)

    tile = pl.BlockSpec((t, D_MODEL), lambda i: (i, 0))
    return pl.pallas_call(
        body,
        grid=(s_len // t,),
        in_specs=[tile, tile],
        out_specs=[tile, pl.BlockSpec((8, 128), lambda i: (0, 0))],
        out_shape=[jax.ShapeDtypeStruct((s_len, D_MODEL), F32), jax.ShapeDtypeStruct((8, 128), F32)],
        compiler_params=_cparams(1),
        name="loss_grad",
    )(xo, target)


N_PEERS = N_DEV - 1
MOD_SHARD = 6 * D_MODEL // N_DEV
ANY = pl.BlockSpec(memory_space=pl.ANY)
VMEM = pl.BlockSpec(memory_space=pltpu.VMEM)

ROW_DMOD = 0
ROW_G1024 = 8
ROW_V384 = 13
ROW_SGU_B = 18
ROW_POOL_SCALE = 19
ROW_CONV_W = 24
ROW_FFN_CONV = 56
ROW_POOL_W = 88
ROW_SGU_W = 104
ROWS_PER_LAYER = 200
N_LAYERS = 2
SMALL_ROWS = N_LAYERS * ROWS_PER_LAYER


def _my_pos():
    return lax.axis_index("x"), lax.axis_index("y"), lax.axis_index("c")


def _peer(pos, k):
    x, y, c = pos
    return (1 - x if k & 4 else x, 1 - y if k & 2 else y, 1 - c if k & 1 else c)


def _flat(pos):
    return 4 * pos[0] + 2 * pos[1] + pos[2]


def _remote_copy(src, dst, send_sem, recv_sem, peer):
    return pltpu.make_async_remote_copy(src_ref=src, dst_ref=dst, send_sem=send_sem, recv_sem=recv_sem,
                                        device_id=peer, device_id_type=pl.DeviceIdType.MESH)


def _gather_weights(c8, mod_w, mod_b8, shards):
    n = len(shards)

    def body(c_ref, modw_ref, modb_ref, *rest):
        shard_refs = rest[:n]
        sc_all_ref, modrows_ref = rest[n], rest[n + 1]
        full_refs = rest[n + 2:2 * n + 2]
        send_buf, mod_recv, w_send, w_recv, w_local, sc_send, sc_recv, mod_send, mod_recv_sem = rest[2 * n + 2:]
        pos = _my_pos()
        me = _flat(pos)
        peers = [_peer(pos, k) for k in range(1, N_DEV)]

        own = [pltpu.make_async_copy(shard_refs[a], full_refs[a].at[me], w_local.at[a]) for a in range(n)]
        for cp in own:
            cp.start()
        w_copies = [_remote_copy(shard_refs[a], full_refs[a].at[me], w_send.at[a, k], w_recv.at[a, k], peers[k])
                    for a in range(n) for k in range(N_PEERS)]
        for cp in w_copies:
            cp.start()

        cv = c_ref[...]
        sc_all_ref[me] = cv * jax.nn.sigmoid(cv)
        sc_copies = [_remote_copy(sc_all_ref.at[me], sc_all_ref.at[me], sc_send.at[k], sc_recv.at[k], peers[k])
                     for k in range(N_PEERS)]
        for cp in sc_copies:
            cp.start()
        for cp in sc_copies:
            cp.wait()

        sc = jnp.concatenate([sc_all_ref[j, 0:1, :] for j in range(N_DEV)], axis=0)
        send_buf[...] = jnp.zeros_like(send_buf)
        for l in range(N_LAYERS):
            part = jnp.dot(sc, modw_ref[l], precision=lax.Precision.HIGHEST, preferred_element_type=F32)
            for j in range(N_DEV):
                send_buf[j, l:l + 1, :] = part[j:j + 1, :]
        mod_recv[me] = send_buf[me]
        mod_copies = [_remote_copy(send_buf.at[_flat(peers[k])], mod_recv.at[me], mod_send.at[k], mod_recv_sem.at[k],
                                   peers[k]) for k in range(N_PEERS)]
        for cp in mod_copies:
            cp.start()
        for cp in mod_copies:
            cp.wait()
        modrows_ref[...] = jnp.zeros_like(modrows_ref)
        for l in range(N_LAYERS):
            row = jnp.concatenate([mod_recv[j, l:l + 1, :] for j in range(N_DEV)], axis=1)
            modrows_ref[l:l + 1, :] = row + modb_ref[l:l + 1, :]

        for cp in w_copies:
            cp.wait()
        for cp in own:
            cp.wait()

    out_shape = ([jax.ShapeDtypeStruct((N_DEV, 8, D_MODEL), F32), jax.ShapeDtypeStruct((8, 6 * D_MODEL), F32)]
                 + [jax.ShapeDtypeStruct((N_DEV,) + s.shape, s.dtype) for s in shards])
    return pl.pallas_call(
        body,
        in_specs=[VMEM, VMEM, VMEM] + [ANY] * n,
        out_specs=[VMEM, VMEM] + [ANY] * n,
        out_shape=out_shape,
        scratch_shapes=[pltpu.VMEM((N_DEV, 8, MOD_SHARD), F32), pltpu.VMEM((N_DEV, 8, MOD_SHARD), F32),
                        pltpu.SemaphoreType.DMA((n, N_PEERS)), pltpu.SemaphoreType.DMA((n, N_PEERS)),
                        pltpu.SemaphoreType.DMA((n,)),
                        pltpu.SemaphoreType.DMA((N_PEERS,)), pltpu.SemaphoreType.DMA((N_PEERS,)),
                        pltpu.SemaphoreType.DMA((N_PEERS,)), pltpu.SemaphoreType.DMA((N_PEERS,))],
        compiler_params=pltpu.CompilerParams(vmem_limit_bytes=VMEM_LIMIT_BYTES),
        name="gather_weights",
    )(c8, mod_w, mod_b8, *shards)


def _reduce_grads(sc_all, small, bigs):
    n = len(bigs)

    def body(sc_all_ref, small_ref, *rest):
        big_refs = rest[:n]
        small_sum_ref, gmodw_ref = rest[n], rest[n + 1]
        recv_refs = rest[n + 2:2 * n + 2]
        small_all, b_send, b_recv, b_local, s_send, s_recv = rest[2 * n + 2:]
        pos = _my_pos()
        me = _flat(pos)
        peers = [_peer(pos, k) for k in range(1, N_DEV)]

        own = [pltpu.make_async_copy(big_refs[a].at[me], recv_refs[a].at[me], b_local.at[a]) for a in range(n)]
        for cp in own:
            cp.start()
        small_all[me] = small_ref[...]
        s_copies = [_remote_copy(small_ref, small_all.at[me], s_send.at[k], s_recv.at[k], peers[k])
                    for k in range(N_PEERS)]
        for cp in s_copies:
            cp.start()
        b_copies = [_remote_copy(big_refs[a].at[_flat(peers[k])], recv_refs[a].at[me], b_send.at[a, k], b_recv.at[a, k],
                                 peers[k]) for a in range(n) for k in range(N_PEERS)]
        for cp in b_copies:
            cp.start()

        for cp in s_copies:
            cp.wait()
        total = small_all[0]
        for j in range(1, N_DEV):
            total = total + small_all[j]
        small_sum_ref[...] = total
        sc = jnp.concatenate([sc_all_ref[j, 0:1, :] for j in range(N_DEV)], axis=0)
        for l in range(N_LAYERS):
            row = l * ROWS_PER_LAYER + ROW_DMOD + me
            dm = jnp.concatenate([small_all[j, pl.ds(row, 1), 0:MOD_SHARD] for j in range(N_DEV)], axis=0)
            gmodw_ref[l] = lax.dot_general(sc, dm, (((0,), (0,)), ((), ())), precision=lax.Precision.HIGHEST,
                                           preferred_element_type=F32)

        for cp in b_copies:
            cp.wait()
        for cp in own:
            cp.wait()

    out_shape = ([jax.ShapeDtypeStruct((SMALL_ROWS, D_MODEL), F32),
                  jax.ShapeDtypeStruct((N_LAYERS, D_MODEL, MOD_SHARD), F32)]
                 + [jax.ShapeDtypeStruct(b.shape, b.dtype) for b in bigs])
    return pl.pallas_call(
        body,
        in_specs=[VMEM, VMEM] + [ANY] * n,
        out_specs=[VMEM, VMEM] + [ANY] * n,
        out_shape=out_shape,
        scratch_shapes=[pltpu.VMEM((N_DEV, SMALL_ROWS, D_MODEL), F32),
                        pltpu.SemaphoreType.DMA((n, N_PEERS)), pltpu.SemaphoreType.DMA((n, N_PEERS)),
                        pltpu.SemaphoreType.DMA((n,)),
                        pltpu.SemaphoreType.DMA((N_PEERS,)), pltpu.SemaphoreType.DMA((N_PEERS,))],
        compiler_params=pltpu.CompilerParams(vmem_limit_bytes=VMEM_LIMIT_BYTES),
        name="reduce_grads",
    )(sc_all, small, *bigs)


def _adam_update(g, w, m, v):
    m2 = ADAM_B1 * m + (1.0 - ADAM_B1) * g
    v2 = ADAM_B2 * v + (1.0 - ADAM_B2) * (g * g)
    m_hat = m2 / (1.0 - ADAM_B1 ** ADAM_STEP)
    v_hat = v2 / (1.0 - ADAM_B2 ** ADAM_STEP)
    delta = -ADAM_LR * (m_hat / (jnp.sqrt(v_hat) + ADAM_EPS) + ADAM_WD * w)
    return delta, m2, v2


def _adam_sharded(recv0, recv1, w, m, v, row_chunk, name):
    _, rows, cols = w.shape
    n_chunks = rows // row_chunk

    def body(r0_ref, r1_ref, w_ref, m_ref, v_ref, g_ref, d_ref, m2_ref, v2_ref):
        layer = pl.program_id(0)

        def run(r_ref):
            g = r_ref[0].astype(F32)
            for j in range(1, N_DEV):
                g = g + r_ref[j].astype(F32)
            delta, m2, v2 = _adam_update(g, w_ref[0], m_ref[0], v_ref[0])
            g_ref[0], d_ref[0], m2_ref[0], v2_ref[0] = g, delta, m2, v2

        @pl.when(layer == 0)
        def _():
            run(r0_ref)

        @pl.when(layer == 1)
        def _():
            run(r1_ref)

    r0_spec = pl.BlockSpec((N_DEV, row_chunk, cols), lambda l, i: (0, i * (1 - l) + (n_chunks - 1) * l, 0))
    r1_spec = pl.BlockSpec((N_DEV, row_chunk, cols), lambda l, i: (0, i * l, 0))
    blk = pl.BlockSpec((1, row_chunk, cols), lambda l, i: (l, i, 0))
    out = jax.ShapeDtypeStruct(w.shape, F32)
    return pl.pallas_call(
        body,
        grid=(N_LAYERS, n_chunks),
        in_specs=[r0_spec, r1_spec, blk, blk, blk],
        out_specs=[blk] * 4,
        out_shape=[out] * 4,
        compiler_params=_cparams(2),
        name=name,
    )(recv0, recv1, w, m, v)


def _adam_dense(g, w, m, v, row_chunk, name):
    rows, cols = w.shape

    def body(g_ref, w_ref, m_ref, v_ref, d_ref, m2_ref, v2_ref):
        d_ref[...], m2_ref[...], v2_ref[...] = _adam_update(g_ref[...], w_ref[...], m_ref[...], v_ref[...])

    blk = pl.BlockSpec((row_chunk, cols), lambda i: (i, 0))
    out = jax.ShapeDtypeStruct(w.shape, F32)
    return pl.pallas_call(
        body,
        grid=(rows // row_chunk,),
        in_specs=[blk] * 4,
        out_specs=[blk] * 3,
        out_shape=[out] * 3,
        compiler_params=_cparams(1),
        name=name,
    )(g, w, m, v)


WEIGHT_NAMES = ("mod_w", "mod_b", "mix_pre_g", "mix_post_g", "w_in", "sgu_norm_g", "sgu_norm_b", "sgu_w", "sgu_b",
                "conv_w", "conv_b", "conv_norm_g", "conv_norm_b", "pool_w", "pool_scale", "branch_g", "w_out",
                "ffn_pre_g", "ffn_post_g", "ffn_up", "ffn_conv_w", "ffn_conv_b", "ffn_down")
SHARDED_BIG = ("w_in", "w_out", "ffn_up", "ffn_down")
SMALL_PACKED = tuple(n for n in WEIGHT_NAMES if n not in SHARDED_BIG + ("mod_w",))


def _rows8(rows, width=D_MODEL):
    out = [jnp.pad(r.astype(F32), (0, width - r.shape[0]))[None] for r in rows]
    out.append(jnp.zeros((8 - len(rows), width), F32))
    return jnp.concatenate(out, axis=0)


def _as_rows(a, width=D_MODEL):
    flat = a.astype(F32).reshape(-1)
    pad = (-flat.shape[0]) % width
    return jnp.pad(flat, (0, pad)).reshape(-1, width)


def _pad_cols(a, width=D_MODEL):
    return jnp.pad(a.astype(F32), ((0, 0), (0, width - a.shape[1])))


def _pack_rows(arrays):
    rows = jnp.concatenate([_as_rows(a) for a in arrays], axis=0)
    return jnp.pad(rows, ((0, (-rows.shape[0]) % 8), (0, 0)))


def _unpack_rows(packed, shapes):
    out, r = [], 0
    for shape in shapes:
        size = math.prod(shape)
        n_rows = -(-size // D_MODEL)
        out.append(packed[r:r + n_rows].reshape(-1)[:size].reshape(shape))
        r += n_rows
    return out


def _layer_consts(l, w, mod_rows, win, wout, conv_w_full):
    modv = _rows8(list(mod_rows[l].reshape(6, D_MODEL)))
    g1024 = _rows8([w["mix_pre_g"][l], w["mix_post_g"][l], w["branch_g"][l], w["ffn_pre_g"][l], w["ffn_post_g"][l]])
    p384 = _rows8([w["sgu_norm_g"][l], w["sgu_norm_b"][l], w["conv_b"][l], w["conv_norm_g"][l], w["conv_norm_b"][l]],
                  SGU_WIDTH)
    cw = jnp.pad(conv_w_full[l], ((0, 32 - CONV_K), (0, 0)))
    bmat = jnp.repeat(w["sgu_b"][l].T, HEAD_DIM, axis=1)
    groups = len(POOL_WINDOWS)
    eye = jnp.eye(groups, dtype=F32)
    pwbd = (eye[:, None, :, None] * w["pool_w"][l][:, :, None, :]).reshape(POOL_WIDTH, POOL_WIDTH).astype(BF16)
    psc = _rows8([w["pool_scale"][l]], POOL_WIDTH)
    seg = jnp.arange(SGU_WIDTH) // HEAD_DIM
    segp = jnp.where(seg[:, None] == seg[None, :], 1.0 / HEAD_DIM, 0.0).astype(BF16)
    return modv, g1024, (modv, g1024, p384, cw, w["sgu_w"][l], bmat, pwbd, psc, segp, win, wout)


def _small_grad_rows(mix, ffn):
    _, _, _, _, _, mvec, v384, dcw, dsguw, dbmat, dpw, dpsc = mix
    fvec, cgrad = ffn[5], ffn[6]
    dmod = jnp.stack([mvec[2], mvec[3], mvec[1], fvec[2], fvec[3], fvec[1]]).reshape(N_DEV, MOD_SHARD)
    g_rows = jnp.stack([mvec[4], mvec[0], mvec[5], fvec[4], fvec[0]])
    dsgu_b = dbmat[:, ::HEAD_DIM].T.reshape(1, SGU_HEADS * CHUNK)
    groups = len(POOL_WINDOWS)
    gdim = POOL_WIDTH // groups
    dpw4 = dpw.reshape(groups, gdim, groups, gdim)
    dpool = jnp.stack([dpw4[g, :, g, :] for g in range(groups)])
    blocks = [_pad_cols(dmod), g_rows, _pad_cols(v384[0:5]), _pad_cols(dsgu_b), _pad_cols(dpsc[0:1]),
              jnp.zeros((ROW_CONV_W - ROW_POOL_SCALE - 1, D_MODEL), F32), _pad_cols(dcw),
              _pad_cols(cgrad[:, 0:4, :].reshape(4 * N_DEV, FF_SHARD)), _as_rows(dpool), _as_rows(dsguw)]
    return jnp.concatenate(blocks, axis=0)


def _small_grads_from_rows(total):
    per = {n: [] for n in SMALL_PACKED}
    for l in range(N_LAYERS):
        s = total[l * ROWS_PER_LAYER:(l + 1) * ROWS_PER_LAYER]
        per["mod_b"].append(s[ROW_DMOD:ROW_DMOD + N_DEV, :MOD_SHARD].reshape(6 * D_MODEL))
        for j, name in enumerate(("mix_pre_g", "mix_post_g", "branch_g", "ffn_pre_g", "ffn_post_g")):
            per[name].append(s[ROW_G1024 + j])
        for j, name in enumerate(("sgu_norm_g", "sgu_norm_b", "conv_b", "conv_norm_g", "conv_norm_b")):
            per[name].append(s[ROW_V384 + j, :SGU_WIDTH])
        per["sgu_b"].append(s[ROW_SGU_B, :SGU_HEADS * CHUNK].reshape(SGU_HEADS, CHUNK))
        per["pool_scale"].append(s[ROW_POOL_SCALE, :POOL_WIDTH])
        per["conv_w"].append(s[ROW_CONV_W:ROW_CONV_W + CONV_K, :CONV_WIDTH])
        fc = s[ROW_FFN_CONV:ROW_FFN_CONV + 4 * N_DEV, :FF_SHARD].reshape(N_DEV, 4, FF_SHARD)
        per["ffn_conv_w"].append(fc[:, 0:3, :].transpose(1, 0, 2).reshape(FFN_CONV_K, 2 * D_FF))
        per["ffn_conv_b"].append(fc[:, 3, :].reshape(2 * D_FF))
        per["pool_w"].append(s[ROW_POOL_W:ROW_POOL_W + 16].reshape(len(POOL_WINDOWS), HEAD_DIM, HEAD_DIM))
        per["sgu_w"].append(s[ROW_SGU_W:ROW_SGU_W + 96].reshape(SGU_HEADS, CHUNK, CHUNK))
    return {n: jnp.stack(v) for n, v in per.items()}


def kernel(x, c, mod_w, mod_b, mix_pre_g, mix_post_g, w_in, sgu_norm_g, sgu_norm_b, sgu_w, sgu_b, conv_w, conv_b, conv_norm_g, conv_norm_b, pool_w, pool_scale, branch_g, w_out, ffn_pre_g, ffn_post_g, ffn_up, ffn_conv_w, ffn_conv_b, ffn_down, loss_target, m_mod_w, m_mod_b, m_mix_pre_g, m_mix_post_g, m_w_in, m_sgu_norm_g, m_sgu_norm_b, m_sgu_w, m_sgu_b, m_conv_w, m_conv_b, m_conv_norm_g, m_conv_norm_b, m_pool_w, m_pool_scale, m_branch_g, m_w_out, m_ffn_pre_g, m_ffn_post_g, m_ffn_up, m_ffn_conv_w, m_ffn_conv_b, m_ffn_down, v_mod_w, v_mod_b, v_mix_pre_g, v_mix_post_g, v_w_in, v_sgu_norm_g, v_sgu_norm_b, v_sgu_w, v_sgu_b, v_conv_w, v_conv_b, v_conv_norm_g, v_conv_norm_b, v_pool_w, v_pool_scale, v_branch_g, v_w_out, v_ffn_pre_g, v_ffn_post_g, v_ffn_up, v_ffn_conv_w, v_ffn_conv_b, v_ffn_down):
    w = dict(zip(WEIGHT_NAMES, (mod_w, mod_b, mix_pre_g, mix_post_g, w_in, sgu_norm_g, sgu_norm_b, sgu_w, sgu_b, conv_w,
                                conv_b, conv_norm_g, conv_norm_b, pool_w, pool_scale, branch_g, w_out, ffn_pre_g,
                                ffn_post_g, ffn_up, ffn_conv_w, ffn_conv_b, ffn_down)))
    m = dict(zip(WEIGHT_NAMES, (m_mod_w, m_mod_b, m_mix_pre_g, m_mix_post_g, m_w_in, m_sgu_norm_g, m_sgu_norm_b, m_sgu_w,
                                m_sgu_b, m_conv_w, m_conv_b, m_conv_norm_g, m_conv_norm_b, m_pool_w, m_pool_scale,
                                m_branch_g, m_w_out, m_ffn_pre_g, m_ffn_post_g, m_ffn_up, m_ffn_conv_w, m_ffn_conv_b,
                                m_ffn_down)))
    v = dict(zip(WEIGHT_NAMES, (v_mod_w, v_mod_b, v_mix_pre_g, v_mix_post_g, v_w_in, v_sgu_norm_g, v_sgu_norm_b, v_sgu_w,
                                v_sgu_b, v_conv_w, v_conv_b, v_conv_norm_g, v_conv_norm_b, v_pool_w, v_pool_scale,
                                v_branch_g, v_w_out, v_ffn_pre_g, v_ffn_post_g, v_ffn_up, v_ffn_conv_w, v_ffn_conv_b,
                                v_ffn_down)))
    me = _flat(_my_pos())
    xs = x[0]
    s_len = xs.shape[0]

    shards = []
    for l in range(N_LAYERS):
        shards += [w_in[l].astype(BF16), w_out[l].astype(BF16), ffn_up[l].astype(BF16), ffn_down[l].astype(BF16)]
    shards += [conv_w, ffn_conv_w]
    c8 = jnp.broadcast_to(c, (8, D_MODEL))
    mod_b8 = jnp.pad(mod_b, ((0, 8 - N_LAYERS), (0, 0)))
    gathered = _gather_weights(c8, mod_w, mod_b8, shards)
    sc_all, mod_rows = gathered[0], gathered[1]
    full = gathered[2:]
    conv_w_full = full[8].transpose(1, 2, 0, 3).reshape(N_LAYERS, CONV_K, CONV_WIDTH)
    ffn_cw_full = full[9]

    layers = []
    for l in range(N_LAYERS):
        win_g, wout_g, wup_g, wdn_g = full[4 * l:4 * l + 4]
        win = win_g.transpose(1, 0, 2).reshape(D_MODEL, IN_WIDTH)
        wout = wout_g.reshape(D_MODEL, D_MODEL)
        wdn = wdn_g.reshape(FF_PAIRS, FF_SHARD, D_MODEL)
        modv, g1024, mix_consts = _layer_consts(l, w, mod_rows, win, wout, conv_w_full)
        ffn_consts = (modv, g1024, wup_g, wdn, ffn_cw_full[:, l], ffn_conv_b[l].reshape(N_DEV, 1, FF_SHARD))
        layers.append((mix_consts, ffn_consts))

    saved = []
    h = xs
    for l, (mix_consts, ffn_consts) in enumerate(layers):
        x1, z, o = _mixer_fwd(h, *mix_consts, name=f"mixer_fwd_l{l}")
        x2, y2, p = _ffn_fwd(x1, *ffn_consts, name=f"ffn_fwd_l{l}")
        saved.append((h, z, o, x1, y2, p))
        h = x2
    dh, loss_tile = _loss_grad(h, loss_target[0])
    loss = lax.psum(loss_tile[0, 0], ("x", "y", "c"))

    small_blocks, bigs = [None] * N_LAYERS, [None] * N_LAYERS
    for l in reversed(range(N_LAYERS)):
        mix_consts, ffn_consts = layers[l]
        x_in, z, o, x1, y2, p = saved[l]
        ffn = _ffn_bwd(dh, x1, y2, p, *ffn_consts, name=f"ffn_bwd_l{l}")
        dx1, dp, a, dy2, h2 = ffn[0:5]
        d_up = _wgrad(h2, dp, f"wgrad_ffn_up_l{l}")
        d_dn = _wgrad(a, dy2, f"wgrad_ffn_down_l{l}").reshape(N_DEV, D_FF // N_DEV, D_MODEL)
        mix = _mixer_bwd(dx1, x_in, o, z, *mix_consts, name=f"mixer_bwd_l{l}")
        dh, dz, do, ycat, h1 = mix[0:5]
        d_in = _wgrad(h1, dz[None], f"wgrad_w_in_l{l}")[0]
        d_in = d_in.reshape(D_MODEL, N_DEV, IN_WIDTH // N_DEV).transpose(1, 0, 2)
        d_out = _wgrad(ycat, do[None], f"wgrad_w_out_l{l}").reshape(N_DEV, D_MODEL // N_DEV, D_MODEL)
        small_blocks[l] = _small_grad_rows(mix, ffn)
        bigs[l] = [d_in, d_out, d_up, d_dn]
    grad_x = dh[None]

    reduced = _reduce_grads(sc_all, jnp.concatenate(small_blocks, axis=0), bigs[0] + bigs[1])
    small_total, g_mod_w = reduced[0], reduced[1]
    recv = reduced[2:]

    grads, deltas, new_m, new_v = {}, {}, {}, {}
    for j, (name, chunk) in enumerate((("w_in", 256), ("w_out", 128), ("ffn_up", 256), ("ffn_down", 176))):
        grads[name], deltas[name], new_m[name], new_v[name] = _adam_sharded(
            recv[j], recv[4 + j], w[name], m[name], v[name], chunk, "adam_" + name)
    flat2 = lambda a: a.reshape(N_LAYERS * D_MODEL, MOD_SHARD)
    grads["mod_w"] = g_mod_w
    d, m2, v2 = _adam_dense(flat2(g_mod_w), flat2(mod_w), flat2(m_mod_w), flat2(v_mod_w), 256, "adam_mod_w")
    deltas["mod_w"], new_m["mod_w"], new_v["mod_w"] = (t.reshape(mod_w.shape) for t in (d, m2, v2))

    small_g = _small_grads_from_rows(small_total)
    small_g["conv_w"] = lax.dynamic_slice_in_dim(small_g["conv_w"], me * conv_w.shape[2], conv_w.shape[2], axis=2)
    small_g["ffn_conv_w"] = lax.dynamic_slice_in_dim(small_g["ffn_conv_w"], me * FF_SHARD, FF_SHARD, axis=2)
    shapes = [w[n].shape for n in SMALL_PACKED]
    packs = [_pack_rows([src[n] for n in SMALL_PACKED]) for src in (small_g, w, m, v)]
    d, m2, v2 = _adam_dense(*packs, packs[0].shape[0], "adam_small")
    for name, dd, mm, vv in zip(SMALL_PACKED, _unpack_rows(d, shapes), _unpack_rows(m2, shapes), _unpack_rows(v2, shapes)):
        grads[name], deltas[name], new_m[name], new_v[name] = small_g[name], dd, mm, vv

    return (loss, grad_x, *[grads[n] for n in WEIGHT_NAMES], *[deltas[n] for n in WEIGHT_NAMES],
            *[new_m[n] for n in WEIGHT_NAMES], *[new_v[n] for n in WEIGHT_NAMES])
```

```python
import functools
import math

import jax
import jax.numpy as jnp
from jax import lax
from jax.experimental import pallas as pl
from jax.experimental.pallas import tpu as pltpu

F32 = jnp.float32
BF16 = jnp.bfloat16

D_MODEL = 1024
N_DEV = 8
SGU_WIDTH = 384
CONV_WIDTH = 384
POOL_WIDTH = 256
HEAD_DIM = 64
SGU_HEADS = 6
CHUNK = 128
CONV_K = 31
POOL_WINDOWS = (2, 4, 8, 16)
IN_WIDTH = 1792
D_FF = 2816
FF_SHARD = 2 * D_FF // N_DEV
FF_PAIRS = N_DEV // 2
FFN_CONV_K = 3
EPS = 1e-6
GELU_C0 = math.sqrt(2.0 / math.pi)
GELU_C1 = 0.044715

ADAM_LR = 0.001
ADAM_B1 = 0.9
ADAM_B2 = 0.999
ADAM_EPS = 1e-08
ADAM_WD = 0.01
ADAM_STEP = 10

VMEM_LIMIT_BYTES = 56 * 1024 * 1024
HALO = 32
MIX_TILE = 256
FFN_TILE = 256
WGRAD_TK = 512


def _cparams(n_axes):
    return pltpu.CompilerParams(dimension_semantics=("arbitrary",) * n_axes, vmem_limit_bytes=VMEM_LIMIT_BYTES)


def _whole(shape):
    nd = len(shape)
    return pl.BlockSpec(shape, lambda *_: (0,) * nd, pipeline_mode=pl.Buffered(1))


def _dot(a, b):
    return jnp.dot(a, b, preferred_element_type=F32)


def _dot_nt(a, b):
    return lax.dot_general(a, b, (((1,), (1,)), ((), ())), preferred_element_type=F32)


def _dot_tn(a, b):
    return lax.dot_general(a, b, (((0,), (0,)), ((), ())), preferred_element_type=F32)


def _gelu(x):
    t = jnp.tanh(GELU_C0 * (x + GELU_C1 * x * x * x))
    return 0.5 * x * (1.0 + t), t


def _gelu_grad(x, t):
    return 0.5 * (1.0 + t) + 0.5 * x * (1.0 - t * t) * (GELU_C0 * (1.0 + 3.0 * GELU_C1 * x * x))


def _rowmean(x):
    return jnp.mean(x, axis=-1, keepdims=True)


def _colsum(x):
    return jnp.sum(x, axis=0, keepdims=True)


def _rms_fwd(x):
    r = lax.rsqrt(_rowmean(x * x) + EPS)
    return x * r, r


def _rms_bwd(dxhat, xhat, r):
    return r * (dxhat - xhat * _rowmean(dxhat * xhat))


N_PEERS = N_DEV - 1
ANY = pl.BlockSpec(memory_space=pl.ANY)
VMEM = pl.BlockSpec(memory_space=pltpu.VMEM)


def _my_pos():
    return lax.axis_index("x"), lax.axis_index("y"), lax.axis_index("c")


def _peer(pos, k):
    x, y, c = pos
    return (1 - x if k & 4 else x, 1 - y if k & 2 else y, 1 - c if k & 1 else c)


def _flat(pos):
    return 4 * pos[0] + 2 * pos[1] + pos[2]


def _remote_copy(src, dst, send_sem, recv_sem, peer):
    return pltpu.make_async_remote_copy(src_ref=src, dst_ref=dst, send_sem=send_sem, recv_sem=recv_sem,
                                        device_id=peer, device_id_type=pl.DeviceIdType.MESH)


def _exchange_out_shapes(job):
    return [jax.ShapeDtypeStruct(((N_DEV,) + a.shape) if kind == "gather" else a.shape, a.dtype) for kind, a in job]


def _exchange_sems(n):
    return [pltpu.SemaphoreType.DMA((n, N_PEERS)), pltpu.SemaphoreType.DMA((n, N_PEERS)), pltpu.SemaphoreType.DMA((n,))]


def _exchange_copies(kinds, src_refs, dst_refs, send_sems, recv_sems, local_sems):
    pos = _my_pos()
    me = _flat(pos)
    copies = []
    for a, kind in enumerate(kinds):
        mine = src_refs[a] if kind == "gather" else src_refs[a].at[me]
        copies.append(pltpu.make_async_copy(mine, dst_refs[a].at[me], local_sems.at[a]))
        for k in range(N_PEERS):
            peer = _peer(pos, k + 1)
            src = src_refs[a] if kind == "gather" else src_refs[a].at[_flat(peer)]
            copies.append(_remote_copy(src, dst_refs[a].at[me], send_sems.at[a, k], recv_sems.at[a, k], peer))
    return copies


def _pallas_call_with_exchange(body, *, grid, in_specs, out_specs, out_shape, scratch_shapes, operands, name, job):
    params = _cparams(len(grid))
    if not job:
        outs = pl.pallas_call(body, grid=grid, in_specs=in_specs, out_specs=out_specs, out_shape=out_shape,
                              scratch_shapes=scratch_shapes, compiler_params=params, name=name)(*operands)
        return outs, []
    kinds = [kind for kind, _ in job]
    n, n_in, n_out, n_scr = len(job), len(in_specs), len(out_specs), len(scratch_shapes)

    def wrapped(*refs):
        ins, jin = refs[:n_in], refs[n_in:n_in + n]
        outs, jout = refs[n_in + n:n_in + n + n_out], refs[n_in + n + n_out:n_in + 2 * n + n_out]
        scr = refs[n_in + 2 * n + n_out:n_in + 2 * n + n_out + n_scr]
        sems = refs[n_in + 2 * n + n_out + n_scr:]
        first = functools.reduce(jnp.logical_and, [pl.program_id(d) == 0 for d in range(len(grid))])
        last = functools.reduce(jnp.logical_and, [pl.program_id(d) == grid[d] - 1 for d in range(len(grid))])

        @pl.when(first)
        def _():
            for cp in _exchange_copies(kinds, jin, jout, *sems):
                cp.start()

        body(*ins, *outs, *scr)

        @pl.when(last)
        def _():
            for cp in _exchange_copies(kinds, jin, jout, *sems):
                cp.wait()

    res = pl.pallas_call(
        wrapped, grid=grid,
        in_specs=list(in_specs) + [ANY] * n,
        out_specs=list(out_specs) + [ANY] * n,
        out_shape=list(out_shape) + _exchange_out_shapes(job),
        scratch_shapes=list(scratch_shapes) + _exchange_sems(n),
        compiler_params=params, name=name,
    )(*operands, *[a for _, a in job])
    return res[:n_out], res[n_out:]


def _seg_mean(x, segp):
    hi = x.astype(BF16)
    lo = (x - hi.astype(F32)).astype(BF16)
    return _dot(hi, segp) + _dot(lo, segp)


def _ffn_fwd(x1, modv, g1024, wup, wdn, cw, cb, name="ffn_fwd", job=None):
    s_len = x1.shape[0]
    t = FFN_TILE
    n_tiles = s_len // t

    def body(x1_ref, mod_ref, g_ref, wup_ref, wdn_ref, cw_ref, cb_ref, x2_ref, y2_ref, p_ref, ext_ref, carry_ref):
        i = pl.program_id(0)

        @pl.when(i == 0)
        def _():
            carry_ref[...] = jnp.zeros_like(carry_ref)

        x1v = x1_ref[...]
        pre_g, post_g = g_ref[3:4, :], g_ref[4:5, :]
        sh2, sc2, g2 = mod_ref[3:4, :], mod_ref[4:5, :], mod_ref[5:6, :]
        xhat, _ = _rms_fwd(x1v)
        h2b = (xhat * pre_g * (1.0 + sc2) + sh2).astype(BF16)

        def conv_shard(s):
            p = _dot(h2b, wup_ref[s])
            p_ref[s] = p.astype(BF16)
            ext_ref[0:8, :] = carry_ref[s]
            ext_ref[8:8 + t, :] = p
            carry_ref[s] = p[t - 8:t, :]
            w = cw_ref[s]
            return w[0:1, :] * ext_ref[6:6 + t, :] + w[1:2, :] * ext_ref[7:7 + t, :] + w[2:3, :] * p + cb_ref[s]

        y2 = jnp.zeros((t, D_MODEL), F32)
        for j in range(FF_PAIRS):
            ug = conv_shard(j)
            uv = conv_shard(j + FF_PAIRS)
            ge, _ = _gelu(ug)
            y2 = y2 + _dot((ge * uv).astype(BF16), wdn_ref[j])
        y2_ref[...] = y2
        yhat, _ = _rms_fwd(y2)
        x2_ref[...] = x1v + g2 * (yhat * post_g)

    tile = pl.BlockSpec((t, D_MODEL), lambda i: (i, 0))
    return _pallas_call_with_exchange(
        body,
        grid=(n_tiles,),
        in_specs=[tile, _whole(modv.shape), _whole(g1024.shape), _whole(wup.shape), _whole(wdn.shape),
                  _whole(cw.shape), _whole(cb.shape)],
        out_specs=[tile, tile, pl.BlockSpec((N_DEV, t, FF_SHARD), lambda i: (0, i, 0))],
        out_shape=[jax.ShapeDtypeStruct((s_len, D_MODEL), F32), jax.ShapeDtypeStruct((s_len, D_MODEL), F32),
                   jax.ShapeDtypeStruct((N_DEV, s_len, FF_SHARD), BF16)],
        scratch_shapes=[pltpu.VMEM((8 + t, FF_SHARD), F32), pltpu.VMEM((N_DEV, 8, FF_SHARD), F32)],
        operands=(x1, modv, g1024, wup, wdn, cw, cb),
        name=name, job=job)


def _ffn_bwd(dx2, x1, y2, p, modv, g1024, wup, wdn, cw, cb, name="ffn_bwd", job=None):
    s_len = x1.shape[0]
    t = FFN_TILE
    n_tiles = s_len // t
    hb = 16

    def body(dx2_ref, x1_ref, y2_ref, p_ref, ph_ref, mod_ref, g_ref, wup_ref, wdn_ref, cw_ref, cb_ref,
             dx1_ref, dp_ref, a_ref, dy2_ref, h2_ref, vec_ref, cgrad_ref, ext_ref, dext_ref, dcarry_ref):
        i = pl.program_id(0)
        tile_idx = n_tiles - 1 - i

        @pl.when(i == 0)
        def _():
            vec_ref[...] = jnp.zeros_like(vec_ref)
            cgrad_ref[...] = jnp.zeros_like(cgrad_ref)
            dcarry_ref[...] = jnp.zeros_like(dcarry_ref)

        dx2v, x1v, y2v = dx2_ref[...], x1_ref[...], y2_ref[...]
        pre_g, post_g = g_ref[3:4, :], g_ref[4:5, :]
        sh2, sc2, g2 = mod_ref[3:4, :], mod_ref[4:5, :], mod_ref[5:6, :]

        yhat, ry = _rms_fwd(y2v)
        vec_ref[1:2, :] += _colsum(dx2v * (yhat * post_g))
        dyn = dx2v * g2
        vec_ref[0:1, :] += _colsum(dyn * yhat)
        dy2b = _rms_bwd(dyn * post_g, yhat, ry).astype(BF16)
        dy2_ref[...] = dy2b

        xhat, rx = _rms_fwd(x1v)
        xn = xhat * pre_g
        h2_ref[...] = (xn * (1.0 + sc2) + sh2).astype(BF16)

        not_first = (tile_idx > 0).astype(F32)

        def recompute(s, slot):
            pf = p_ref[s].astype(F32)
            ext_ref[slot, 0:8, :] = ph_ref[s][hb - 8:hb, :].astype(F32) * not_first
            ext_ref[slot, 8:8 + t, :] = pf
            w = cw_ref[s]
            u = (w[0:1, :] * ext_ref[slot, 6:6 + t, :] + w[1:2, :] * ext_ref[slot, 7:7 + t, :]
                 + w[2:3, :] * pf + cb_ref[s])
            return u

        def conv_bwd(s, slot, du):
            w = cw_ref[s]
            cgrad_ref[s, 0:1, :] += _colsum(du * ext_ref[slot, 6:6 + t, :])
            cgrad_ref[s, 1:2, :] += _colsum(du * ext_ref[slot, 7:7 + t, :])
            cgrad_ref[s, 2:3, :] += _colsum(du * ext_ref[slot, 8:8 + t, :])
            cgrad_ref[s, 3:4, :] += _colsum(du)
            dext_ref[0:t, :] = du
            dext_ref[t:t + 8, :] = dcarry_ref[s]
            dcarry_ref[s] = du[0:8, :]
            dp = w[2:3, :] * du + w[1:2, :] * dext_ref[1:1 + t, :] + w[0:1, :] * dext_ref[2:2 + t, :]
            dpb = dp.astype(BF16)
            dp_ref[s] = dpb
            return _dot_nt(dpb, wup_ref[s])

        dh2 = jnp.zeros((t, D_MODEL), F32)
        for j in range(FF_PAIRS):
            ug = recompute(j, 0)
            uv = recompute(j + FF_PAIRS, 1)
            ge, th = _gelu(ug)
            a_ref[j] = (ge * uv).astype(BF16)
            da = _dot_nt(dy2b, wdn_ref[j])
            dh2 = dh2 + conv_bwd(j, 0, da * uv * _gelu_grad(ug, th))
            dh2 = dh2 + conv_bwd(j + FF_PAIRS, 1, da * ge)

        vec_ref[2:3, :] += _colsum(dh2)
        vec_ref[3:4, :] += _colsum(dh2 * xn)
        dxn = dh2 * (1.0 + sc2)
        vec_ref[4:5, :] += _colsum(dxn * xhat)
        dx1_ref[...] = dx2v + _rms_bwd(dxn * pre_g, xhat, rx)

    rev = lambda i: (n_tiles - 1 - i, 0)
    tile = pl.BlockSpec((t, D_MODEL), rev)
    halo_idx = lambda i: (0, jnp.maximum((n_tiles - 1 - i) * (t // hb) - 1, 0), 0)
    return _pallas_call_with_exchange(
        body,
        grid=(n_tiles,),
        in_specs=[tile, tile, tile,
                  pl.BlockSpec((N_DEV, t, FF_SHARD), lambda i: (0, n_tiles - 1 - i, 0)),
                  pl.BlockSpec((N_DEV, hb, FF_SHARD), halo_idx),
                  _whole(modv.shape), _whole(g1024.shape), _whole(wup.shape), _whole(wdn.shape),
                  _whole(cw.shape), _whole(cb.shape)],
        out_specs=[tile,
                   pl.BlockSpec((N_DEV, t, FF_SHARD), lambda i: (0, n_tiles - 1 - i, 0)),
                   pl.BlockSpec((FF_PAIRS, t, FF_SHARD), lambda i: (0, n_tiles - 1 - i, 0)),
                   tile, tile,
                   pl.BlockSpec((8, D_MODEL), lambda i: (0, 0)),
                   pl.BlockSpec((N_DEV, 8, FF_SHARD), lambda i: (0, 0, 0))],
        out_shape=[jax.ShapeDtypeStruct((s_len, D_MODEL), F32),
                   jax.ShapeDtypeStruct((N_DEV, s_len, FF_SHARD), BF16),
                   jax.ShapeDtypeStruct((FF_PAIRS, s_len, FF_SHARD), BF16),
                   jax.ShapeDtypeStruct((s_len, D_MODEL), BF16),
                   jax.ShapeDtypeStruct((s_len, D_MODEL), BF16),
                   jax.ShapeDtypeStruct((8, D_MODEL), F32),
                   jax.ShapeDtypeStruct((N_DEV, 8, FF_SHARD), F32)],
        scratch_shapes=[pltpu.VMEM((2, 8 + t, FF_SHARD), F32), pltpu.VMEM((t + 8, FF_SHARD), F32),
                        pltpu.VMEM((N_DEV, 8, FF_SHARD), F32)],
        operands=(dx2, x1, y2, p, p, modv, g1024, wup, wdn, cw, cb),
        name=name, job=job)


def _wgrad(a, b, name):
    a_grouped, b_grouped = a.ndim == 3, b.ndim == 3
    groups = a.shape[0] if a_grouped else b.shape[0]
    s_len, m, n = a.shape[-2], a.shape[-1], b.shape[-1]
    tk = min(WGRAD_TK, s_len)
    n_k = s_len // tk

    def body(a_ref, b_ref, o_ref, acc_ref):
        k = pl.program_id(1)
        av = a_ref[0] if a_grouped else a_ref[...]
        bv = b_ref[0] if b_grouped else b_ref[...]
        part = _dot_tn(av, bv)

        @pl.when(k == 0)
        def _():
            acc_ref[...] = part

        @pl.when(k > 0)
        def _():
            acc_ref[...] += part

        @pl.when(k == n_k - 1)
        def _():
            o_ref[0] = acc_ref[...].astype(BF16)

    a_spec = pl.BlockSpec((1, tk, m), lambda g, k: (g, k, 0)) if a_grouped else pl.BlockSpec((tk, m), lambda g, k: (k, 0))
    b_spec = pl.BlockSpec((1, tk, n), lambda g, k: (g, k, 0)) if b_grouped else pl.BlockSpec((tk, n), lambda g, k: (k, 0))
    return pl.pallas_call(
        body,
        grid=(groups, n_k),
        in_specs=[a_spec, b_spec],
        out_specs=pl.BlockSpec((1, m, n), lambda g, k: (g, 0, 0)),
        out_shape=jax.ShapeDtypeStruct((groups, m, n), BF16),
        scratch_shapes=[pltpu.VMEM((m, n), F32)],
        compiler_params=_cparams(2),
        name=name,
    )(a, b)


def _lane(shape):
    return lax.broadcasted_iota(jnp.int32, shape, 1)


def _by_pool_group(shape, vals):
    lane = _lane(shape)
    return jnp.where(lane < 64, vals[0], jnp.where(lane < 128, vals[1], jnp.where(lane < 192, vals[2], vals[3])))


def _pool_inv_counts(t, tile_idx):
    pos1 = lax.broadcasted_iota(jnp.int32, (t, 1), 0) + tile_idx * t + 1
    return [1.0 / jnp.minimum(pos1, w).astype(F32) for w in POOL_WINDOWS]


def _masked_sgu_w(sguw_ref):
    row = lax.broadcasted_iota(jnp.int32, (CHUNK, CHUNK), 0)
    col = lax.broadcasted_iota(jnp.int32, (CHUNK, CHUNK), 1)
    return [jnp.where(row >= col, sguw_ref[h], 0.0).astype(BF16) for h in range(SGU_HEADS)]


def _branches_fwd(z, tile_idx, p384_ref, cw_ref, wm, bmat_ref, pwbd_ref, psc_ref, segp_ref, g_ref, hext_ref, zext_ref):
    t = z.shape[0]
    segp = segp_ref[...]
    r = {}
    u, _ = _gelu(z[:, 0:SGU_WIDTH])
    vraw, _ = _gelu(z[:, SGU_WIDTH:2 * SGU_WIDTH])
    xc = vraw - _seg_mean(vraw, segp)
    rstd_v = lax.rsqrt(_seg_mean(xc * xc, segp) + EPS)
    xh_v = xc * rstd_v
    vnb = (xh_v * p384_ref[0:1, :] + p384_ref[1:2, :]).astype(BF16)
    first_head = _lane((CHUNK, CHUNK)) < HEAD_DIM
    f_rows = []
    for c in range(t // CHUNK):
        f_pairs = []
        for pr in range(SGU_HEADS // 2):
            vp = vnb[c * CHUNK:(c + 1) * CHUNK, pr * 128:(pr + 1) * 128]
            f_pairs.append(jnp.where(first_head, _dot(wm[2 * pr], vp), _dot(wm[2 * pr + 1], vp)))
        f_rows.append(jnp.concatenate(f_pairs, axis=1) + bmat_ref[...])
    f = jnp.concatenate(f_rows, axis=0)
    ya = u * f
    r.update(u=u, xh_v=xh_v, rstd_v=rstd_v, vnb=vnb, f=f)
    o_b = 2 * SGU_WIDTH
    a_in = z[:, o_b:o_b + CONV_WIDTH]
    sig_g = jax.nn.sigmoid(z[:, o_b + CONV_WIDTH:o_b + 2 * CONV_WIDTH])
    hext_ref[HALO:HALO + t, :] = a_in * sig_g
    conv = jnp.zeros((t, CONV_WIDTH), F32) + p384_ref[2:3, :]
    for k in range(CONV_K):
        conv = conv + cw_ref[k:k + 1, :] * hext_ref[pl.ds(HALO - (CONV_K - 1) + k, t), :]
    cc = conv - _rowmean(conv)
    rstd_c = lax.rsqrt(_rowmean(cc * cc) + EPS)
    xh_c = cc * rstd_c
    cn = xh_c * p384_ref[3:4, :] + p384_ref[4:5, :]
    sig_c = jax.nn.sigmoid(cn)
    yb = cn * sig_c
    r.update(a_in=a_in, sig_g=sig_g, xh_c=xh_c, rstd_c=rstd_c, cn=cn, sig_c=sig_c)
    o_c = o_b + 2 * CONV_WIDTH
    zc = z[:, o_c:o_c + POOL_WIDTH]
    zext_ref[HALO:HALO + t, :] = zc
    sums, acc = [], zc
    for j in range(1, POOL_WINDOWS[-1]):
        acc = acc + zext_ref[pl.ds(HALO - j, t), :]
        if j + 1 in POOL_WINDOWS:
            sums.append(acc)
    inv = _pool_inv_counts(t, tile_idx)
    dpool = _by_pool_group((t, POOL_WIDTH), [s * iv for s, iv in zip(sums, inv)]) - zc
    ycp = _dot(dpool.astype(BF16), pwbd_ref[...])
    yc = ycp * psc_ref[0:1, :]
    r.update(dpool=dpool, ycp=ycp)
    yha, ra = _rms_fwd(ya)
    yhb, rb = _rms_fwd(yb)
    yhc, rc = _rms_fwd(yc)
    bg = g_ref[2:3, :]
    ycat = jnp.concatenate([yha * bg[:, 0:384], yhb * bg[:, 384:768], yhc * bg[:, 768:1024]], axis=1)
    r.update(yha=yha, ra=ra, yhb=yhb, rb=rb, yhc=yhc, rc=rc, ycat=ycat)
    return r


def _mixer_fwd(x, modv, g1024, p384, cw, sguw, bmat, pwbd, psc, segp, win, wout, name="mixer_fwd", job=None):
    s_len = x.shape[0]
    t = MIX_TILE

    def body(x_ref, mod_ref, g_ref, p384_ref, cw_ref, sguw_ref, bmat_ref, pwbd_ref, psc_ref, segp_ref, win_ref, wout_ref,
             x1_ref, z_ref, o_ref, hext_ref, zext_ref):
        i = pl.program_id(0)

        @pl.when(i == 0)
        def _():
            hext_ref[0:HALO, :] = jnp.zeros((HALO, CONV_WIDTH), F32)
            zext_ref[0:HALO, :] = jnp.zeros((HALO, POOL_WIDTH), F32)

        xv = x_ref[...]
        sh1, sc1, g1 = mod_ref[0:1, :], mod_ref[1:2, :], mod_ref[2:3, :]
        xhat, _ = _rms_fwd(xv)
        h1 = xhat * g_ref[0:1, :] * (1.0 + sc1) + sh1
        z = _dot(h1.astype(BF16), win_ref[...])
        z_ref[...] = z
        r = _branches_fwd(z, i, p384_ref, cw_ref, _masked_sgu_w(sguw_ref), bmat_ref, pwbd_ref, psc_ref, segp_ref,
                          g_ref, hext_ref, zext_ref)
        o = _dot(r["ycat"].astype(BF16), wout_ref[...])
        o_ref[...] = o
        ohat, _ = _rms_fwd(o)
        x1_ref[...] = xv + g1 * (ohat * g_ref[1:2, :])
        hext_ref[0:HALO, :] = hext_ref[t:t + HALO, :]
        zext_ref[0:HALO, :] = zext_ref[t:t + HALO, :]

    tile = pl.BlockSpec((t, D_MODEL), lambda i: (i, 0))
    consts = (modv, g1024, p384, cw, sguw, bmat, pwbd, psc, segp, win, wout)
    return _pallas_call_with_exchange(
        body,
        grid=(s_len // t,),
        in_specs=[tile] + [_whole(c.shape) for c in consts],
        out_specs=[tile, pl.BlockSpec((t, IN_WIDTH), lambda i: (i, 0)), tile],
        out_shape=[jax.ShapeDtypeStruct((s_len, D_MODEL), F32), jax.ShapeDtypeStruct((s_len, IN_WIDTH), F32),
                   jax.ShapeDtypeStruct((s_len, D_MODEL), F32)],
        scratch_shapes=[pltpu.VMEM((HALO + t, CONV_WIDTH), F32), pltpu.VMEM((HALO + t, POOL_WIDTH), F32)],
        operands=(x, *consts),
        name=name, job=job)


def _mixer_bwd(dx1, x, o, z, modv, g1024, p384, cw, sguw, bmat, pwbd, psc, segp, win, wout, name="mixer_bwd", job=None):
    s_len = x.shape[0]
    t = MIX_TILE
    n_tiles = s_len // t

    def body(dx1_ref, x_ref, o_ref, z_ref, zh_ref, mod_ref, g_ref, p384_ref, cw_ref, sguw_ref, bmat_ref, pwbd_ref,
             psc_ref, segp_ref, win_ref, wout_ref,
             dx_ref, dz_ref, do_ref, ycat_ref, h1_ref, vec_ref, v384_ref, dcw_ref, dsguw_ref, dbmat_ref, dpw_ref,
             dpsc_ref, hext_ref, zext_ref, gext_ref, qext_ref):
        i = pl.program_id(0)
        tile_idx = n_tiles - 1 - i

        @pl.when(i == 0)
        def _():
            for ref in (vec_ref, v384_ref, dcw_ref, dsguw_ref, dbmat_ref, dpw_ref, dpsc_ref):
                ref[...] = jnp.zeros_like(ref)
            gext_ref[t:t + HALO, :] = jnp.zeros((HALO, CONV_WIDTH), F32)
            qext_ref[t:t + HALO, :] = jnp.zeros((HALO, POOL_WIDTH), F32)

        dx1v, xv, ov, z = dx1_ref[...], x_ref[...], o_ref[...], z_ref[...]
        sh1, sc1, g1 = mod_ref[0:1, :], mod_ref[1:2, :], mod_ref[2:3, :]
        pre_g, post_g, bg = g_ref[0:1, :], g_ref[1:2, :], g_ref[2:3, :]
        segp = segp_ref[...]

        ohat, ro = _rms_fwd(ov)
        vec_ref[1:2, :] += _colsum(dx1v * (ohat * post_g))
        don = dx1v * g1
        vec_ref[0:1, :] += _colsum(don * ohat)
        dob = _rms_bwd(don * post_g, ohat, ro).astype(BF16)
        do_ref[...] = dob
        dycat = _dot_nt(dob, wout_ref[...])

        not_first = (tile_idx > 0).astype(F32)
        zh = zh_ref[...] * not_first
        o_b = 2 * SGU_WIDTH
        o_c = o_b + 2 * CONV_WIDTH
        hext_ref[0:HALO, :] = zh[:, o_b:o_b + CONV_WIDTH] * jax.nn.sigmoid(zh[:, o_b + CONV_WIDTH:o_c])
        zext_ref[0:HALO, :] = zh[:, o_c:o_c + POOL_WIDTH]
        wm = _masked_sgu_w(sguw_ref)
        r = _branches_fwd(z, tile_idx, p384_ref, cw_ref, wm, bmat_ref, pwbd_ref, psc_ref, segp_ref, g_ref,
                          hext_ref, zext_ref)
        ycat_ref[...] = r["ycat"].astype(BF16)

        def branch_norm_bwd(dyn, yhat, rr, gain):
            return _colsum(dyn * yhat), _rms_bwd(dyn * gain, yhat, rr)

        dga, dya = branch_norm_bwd(dycat[:, 0:384], r["yha"], r["ra"], bg[:, 0:384])
        dgb, dyb = branch_norm_bwd(dycat[:, 384:768], r["yhb"], r["rb"], bg[:, 384:768])
        dgc, dyc = branch_norm_bwd(dycat[:, 768:1024], r["yhc"], r["rc"], bg[:, 768:1024])
        vec_ref[5:6, :] += jnp.concatenate([dga, dgb, dgc], axis=1)

        du_act = dya * r["f"]
        df = dya * r["u"]
        first_head = _lane((CHUNK, CHUNK)) < HEAD_DIM
        wmt = [w.T for w in wm]
        dvn_rows = []
        for c in range(t // CHUNK):
            dfc = df[c * CHUNK:(c + 1) * CHUNK, :]
            dbmat_ref[...] += dfc
            dvn_pairs = []
            for pr in range(SGU_HEADS // 2):
                dfp = dfc[:, pr * 128:(pr + 1) * 128]
                df0 = jnp.where(first_head, dfp, 0.0).astype(BF16)
                df1 = jnp.where(first_head, 0.0, dfp).astype(BF16)
                vp = r["vnb"][c * CHUNK:(c + 1) * CHUNK, pr * 128:(pr + 1) * 128]
                dvn_pairs.append(_dot(wmt[2 * pr], df0) + _dot(wmt[2 * pr + 1], df1))
                dsguw_ref[2 * pr] += _dot_nt(df0, vp)
                dsguw_ref[2 * pr + 1] += _dot_nt(df1, vp)
            dvn_rows.append(jnp.concatenate(dvn_pairs, axis=1))
        dvn = jnp.concatenate(dvn_rows, axis=0)
        v384_ref[0:1, :] += _colsum(dvn * r["xh_v"])
        v384_ref[1:2, :] += _colsum(dvn)
        dxh = dvn * p384_ref[0:1, :]
        dvraw = r["rstd_v"] * (dxh - _seg_mean(dxh, segp) - r["xh_v"] * _seg_mean(dxh * r["xh_v"], segp))
        zu, zv = z[:, 0:SGU_WIDTH], z[:, SGU_WIDTH:o_b]
        _, tu = _gelu(zu)
        _, tv = _gelu(zv)
        dz_u = du_act * _gelu_grad(zu, tu)
        dz_v = dvraw * _gelu_grad(zv, tv)

        cn, sig_c = r["cn"], r["sig_c"]
        dcn = dyb * (sig_c * (1.0 + cn * (1.0 - sig_c)))
        v384_ref[3:4, :] += _colsum(dcn * r["xh_c"])
        v384_ref[4:5, :] += _colsum(dcn)
        dxc = dcn * p384_ref[3:4, :]
        gconv = r["rstd_c"] * (dxc - _rowmean(dxc) - r["xh_c"] * _rowmean(dxc * r["xh_c"]))
        v384_ref[2:3, :] += _colsum(gconv)
        gext_ref[0:t, :] = gconv
        dhh = jnp.zeros((t, CONV_WIDTH), F32)
        for k in range(CONV_K):
            dcw_ref[k:k + 1, :] += _colsum(gconv * hext_ref[pl.ds(HALO - (CONV_K - 1) + k, t), :])
            dhh = dhh + cw_ref[k:k + 1, :] * gext_ref[pl.ds(CONV_K - 1 - k, t), :]
        gext_ref[t:t + HALO, :] = gconv[0:HALO, :]
        sig_g = r["sig_g"]
        dz_a = dhh * sig_g
        dz_g = dhh * r["a_in"] * sig_g * (1.0 - sig_g)

        dpsc_ref[0:1, :] += _colsum(dyc * r["ycp"])
        dycp = (dyc * psc_ref[0:1, :]).astype(BF16)
        dpw_ref[...] += _dot_tn(r["dpool"].astype(BF16), dycp)
        ddp = _dot_nt(dycp, pwbd_ref[...])
        inv = _pool_inv_counts(t, tile_idx)
        q = ddp * _by_pool_group((t, POOL_WIDTH), inv)
        qext_ref[0:t, :] = q
        sums, acc = [], q
        for j in range(1, POOL_WINDOWS[-1]):
            acc = acc + qext_ref[pl.ds(j, t), :]
            if j + 1 in POOL_WINDOWS:
                sums.append(acc)
        qext_ref[t:t + HALO, :] = q[0:HALO, :]
        dz_c = _by_pool_group((t, POOL_WIDTH), sums) - ddp

        dzb = jnp.concatenate([dz_u, dz_v, dz_a, dz_g, dz_c], axis=1).astype(BF16)
        dz_ref[...] = dzb
        dh1 = _dot_nt(dzb, win_ref[...])

        xhat, rx = _rms_fwd(xv)
        xn = xhat * pre_g
        h1_ref[...] = (xn * (1.0 + sc1) + sh1).astype(BF16)
        vec_ref[2:3, :] += _colsum(dh1)
        vec_ref[3:4, :] += _colsum(dh1 * xn)
        dxn = dh1 * (1.0 + sc1)
        vec_ref[4:5, :] += _colsum(dxn * xhat)
        dx_ref[...] = dx1v + _rms_bwd(dxn * pre_g, xhat, rx)

        @pl.when(i == n_tiles - 1)
        def _():
            row = lax.broadcasted_iota(jnp.int32, (CHUNK, CHUNK), 0)
            col = lax.broadcasted_iota(jnp.int32, (CHUNK, CHUNK), 1)
            for h in range(SGU_HEADS):
                dsguw_ref[h] = jnp.where(row >= col, dsguw_ref[h], 0.0)
            dbmat_ref[...] = float(HEAD_DIM) * _seg_mean(dbmat_ref[...], segp)

    rev = lambda i: (n_tiles - 1 - i, 0)
    tile = pl.BlockSpec((t, D_MODEL), rev)
    ztile = pl.BlockSpec((t, IN_WIDTH), rev)
    zhalo = pl.BlockSpec((HALO, IN_WIDTH), lambda i: (jnp.maximum((n_tiles - 1 - i) * (t // HALO) - 1, 0), 0))
    consts = (modv, g1024, p384, cw, sguw, bmat, pwbd, psc, segp, win, wout)
    acc = lambda shape: pl.BlockSpec(shape, lambda i: (0,) * len(shape))
    acc_shapes = [(8, D_MODEL), (8, SGU_WIDTH), (32, CONV_WIDTH), (SGU_HEADS, CHUNK, CHUNK), (CHUNK, SGU_WIDTH),
                  (POOL_WIDTH, POOL_WIDTH), (8, POOL_WIDTH)]
    return _pallas_call_with_exchange(
        body,
        grid=(n_tiles,),
        in_specs=[tile, tile, tile, ztile, zhalo] + [_whole(c.shape) for c in consts],
        out_specs=[tile, ztile, tile, tile, tile] + [acc(s) for s in acc_shapes],
        out_shape=[jax.ShapeDtypeStruct((s_len, D_MODEL), F32), jax.ShapeDtypeStruct((s_len, IN_WIDTH), BF16),
                   jax.ShapeDtypeStruct((s_len, D_MODEL), BF16), jax.ShapeDtypeStruct((s_len, D_MODEL), BF16),
                   jax.ShapeDtypeStruct((s_len, D_MODEL), BF16)] + [jax.ShapeDtypeStruct(s, F32) for s in acc_shapes],
        scratch_shapes=[pltpu.VMEM((HALO + t, CONV_WIDTH), F32), pltpu.VMEM((HALO + t, POOL_WIDTH), F32),
                        pltpu.VMEM((t + HALO, CONV_WIDTH), F32), pltpu.VMEM((t + HALO, POOL_WIDTH), F32)],
        operands=(dx1, x, o, z, z, *consts),
        name=name, job=job)


def _loss_grad(xo, target):
    s_len = xo.shape[0]
    t = 512

    def body(x_ref, t_ref, dx_ref, loss_ref):
        i = pl.program_id(0)

        @pl.when(i == 0)
        def _():
            loss_ref[...] = jnp.zeros_like(loss_ref)

        diff = x_ref[...] - t_ref[...]
        dx_ref[...] = diff * (1.0 / D_MODEL)
        loss_ref[...] += (0.5 / D_MODEL) * jnp.sum(diff * diff)

    tile = pl.BlockSpec((t, D_MODEL), lambda i: (i, 0))
    return pl.pallas_call(
        body,
        grid=(s_len // t,),
        in_specs=[tile, tile],
        out_specs=[tile, pl.BlockSpec((8, 128), lambda i: (0, 0))],
        out_shape=[jax.ShapeDtypeStruct((s_len, D_MODEL), F32), jax.ShapeDtypeStruct((8, 128), F32)],
        compiler_params=_cparams(1),
        name="loss_grad",
    )(xo, target)


MOD_SHARD = 6 * D_MODEL // N_DEV

ROW_DMOD = 0
ROW_G1024 = 8
ROW_V384 = 16
ROW_SGU_B = 24
ROW_POOL_SCALE = 25
ROW_CONV_W = 32
ROW_FFN_CONV = 64
ROW_POOL_W = 96
ROW_SGU_W = 112
ROWS_PER_LAYER = 208
N_LAYERS = 2


def _gather_weights(c8, mod_w, mod_b8, shards):
    n = len(shards)

    def body(c_ref, modw_ref, modb_ref, *rest):
        shard_refs = rest[:n]
        sc_all_ref, modrows_ref = rest[n], rest[n + 1]
        full_refs = rest[n + 2:2 * n + 2]
        send_buf, mod_recv, w_send, w_recv, w_local, sc_send, sc_recv, mod_send, mod_recv_sem = rest[2 * n + 2:]
        pos = _my_pos()
        me = _flat(pos)
        peers = [_peer(pos, k) for k in range(1, N_DEV)]

        own = [pltpu.make_async_copy(shard_refs[a], full_refs[a].at[me], w_local.at[a]) for a in range(n)]
        for cp in own:
            cp.start()
        w_copies = [_remote_copy(shard_refs[a], full_refs[a].at[me], w_send.at[a, k], w_recv.at[a, k], peers[k])
                    for a in range(n) for k in range(N_PEERS)]
        for cp in w_copies:
            cp.start()

        cv = c_ref[...]
        sc_all_ref[me] = cv * jax.nn.sigmoid(cv)
        sc_copies = [_remote_copy(sc_all_ref.at[me], sc_all_ref.at[me], sc_send.at[k], sc_recv.at[k], peers[k])
                     for k in range(N_PEERS)]
        for cp in sc_copies:
            cp.start()
        for cp in sc_copies:
            cp.wait()

        sc = jnp.concatenate([sc_all_ref[j, 0:1, :] for j in range(N_DEV)], axis=0)
        send_buf[...] = jnp.zeros_like(send_buf)
        for l in range(N_LAYERS):
            part = jnp.dot(sc, modw_ref[l], precision=lax.Precision.HIGHEST, preferred_element_type=F32)
            for j in range(N_DEV):
                send_buf[j, l:l + 1, :] = part[j:j + 1, :]
        mod_recv[me] = send_buf[me]
        mod_copies = [_remote_copy(send_buf.at[_flat(peers[k])], mod_recv.at[me], mod_send.at[k], mod_recv_sem.at[k],
                                   peers[k]) for k in range(N_PEERS)]
        for cp in mod_copies:
            cp.start()
        for cp in mod_copies:
            cp.wait()
        modrows_ref[...] = jnp.zeros_like(modrows_ref)
        for l in range(N_LAYERS):
            row = jnp.concatenate([mod_recv[j, l:l + 1, :] for j in range(N_DEV)], axis=1)
            modrows_ref[l:l + 1, :] = row + modb_ref[l:l + 1, :]

        for cp in w_copies:
            cp.wait()
        for cp in own:
            cp.wait()

    out_shape = ([jax.ShapeDtypeStruct((N_DEV, 8, D_MODEL), F32), jax.ShapeDtypeStruct((8, 6 * D_MODEL), F32)]
                 + [jax.ShapeDtypeStruct((N_DEV,) + s.shape, s.dtype) for s in shards])
    return pl.pallas_call(
        body,
        in_specs=[VMEM, VMEM, VMEM] + [ANY] * n,
        out_specs=[VMEM, VMEM] + [ANY] * n,
        out_shape=out_shape,
        scratch_shapes=[pltpu.VMEM((N_DEV, 8, MOD_SHARD), F32), pltpu.VMEM((N_DEV, 8, MOD_SHARD), F32),
                        pltpu.SemaphoreType.DMA((n, N_PEERS)), pltpu.SemaphoreType.DMA((n, N_PEERS)),
                        pltpu.SemaphoreType.DMA((n,)),
                        pltpu.SemaphoreType.DMA((N_PEERS,)), pltpu.SemaphoreType.DMA((N_PEERS,)),
                        pltpu.SemaphoreType.DMA((N_PEERS,)), pltpu.SemaphoreType.DMA((N_PEERS,))],
        compiler_params=pltpu.CompilerParams(vmem_limit_bytes=VMEM_LIMIT_BYTES),
        name="gather_weights",
    )(c8, mod_w, mod_b8, *shards)


def _reduce_grads(sc_all, small0, small_all1, bigs):
    n = len(bigs)
    kinds = ["scatter"] * n

    def body(sc_all_ref, small0_ref, small_all1_ref, *rest):
        big_refs = rest[:n]
        small_sum_ref, gmodw_ref = rest[n], rest[n + 1]
        recv_refs = rest[n + 2:2 * n + 2]
        small_all0, b_send, b_recv, b_local, s_send, s_recv = rest[2 * n + 2:]
        pos = _my_pos()
        me = _flat(pos)

        small_all0[me] = small0_ref[...]
        s_copies = [_remote_copy(small0_ref, small_all0.at[me], s_send.at[k], s_recv.at[k], _peer(pos, k + 1))
                    for k in range(N_PEERS)]
        for cp in s_copies:
            cp.start()
        b_copies = _exchange_copies(kinds, big_refs, recv_refs, b_send, b_recv, b_local)
        for cp in b_copies:
            cp.start()

        for cp in s_copies:
            cp.wait()
        sc = jnp.concatenate([sc_all_ref[j, 0:1, :] for j in range(N_DEV)], axis=0)
        for l, parts in enumerate((small_all0, small_all1_ref)):
            total = parts[0]
            for j in range(1, N_DEV):
                total = total + parts[j]
            small_sum_ref[l] = total
            dm = jnp.concatenate([parts[j, pl.ds(ROW_DMOD + me, 1), 0:MOD_SHARD] for j in range(N_DEV)], axis=0)
            gmodw_ref[l] = lax.dot_general(sc, dm, (((0,), (0,)), ((), ())), precision=lax.Precision.HIGHEST,
                                           preferred_element_type=F32)

        for cp in b_copies:
            cp.wait()

    out_shape = ([jax.ShapeDtypeStruct((N_LAYERS, ROWS_PER_LAYER, D_MODEL), F32),
                  jax.ShapeDtypeStruct((N_LAYERS, D_MODEL, MOD_SHARD), F32)]
                 + [jax.ShapeDtypeStruct(b.shape, b.dtype) for b in bigs])
    return pl.pallas_call(
        body,
        in_specs=[VMEM, VMEM, VMEM] + [ANY] * n,
        out_specs=[VMEM, VMEM] + [ANY] * n,
        out_shape=out_shape,
        scratch_shapes=[pltpu.VMEM((N_DEV, ROWS_PER_LAYER, D_MODEL), F32)] + _exchange_sems(n)
                       + [pltpu.SemaphoreType.DMA((N_PEERS,)), pltpu.SemaphoreType.DMA((N_PEERS,))],
        compiler_params=pltpu.CompilerParams(vmem_limit_bytes=VMEM_LIMIT_BYTES),
        name="reduce_grads",
    )(sc_all, small0, small_all1, *bigs)


def _adam_update(g, w, m, v):
    m2 = ADAM_B1 * m + (1.0 - ADAM_B1) * g
    v2 = ADAM_B2 * v + (1.0 - ADAM_B2) * (g * g)
    m_hat = m2 / (1.0 - ADAM_B1 ** ADAM_STEP)
    v_hat = v2 / (1.0 - ADAM_B2 ** ADAM_STEP)
    delta = -ADAM_LR * (m_hat / (jnp.sqrt(v_hat) + ADAM_EPS) + ADAM_WD * w)
    return delta, m2, v2


def _adam_sharded(recv0, recv1, w, m, v, row_chunk, name):
    _, rows, cols = w.shape
    n_chunks = rows // row_chunk

    def body(r0_ref, r1_ref, w_ref, m_ref, v_ref, g_ref, d_ref, m2_ref, v2_ref):
        layer = pl.program_id(0)

        def run(r_ref):
            g = r_ref[0].astype(F32)
            for j in range(1, N_DEV):
                g = g + r_ref[j].astype(F32)
            delta, m2, v2 = _adam_update(g, w_ref[0], m_ref[0], v_ref[0])
            g_ref[0], d_ref[0], m2_ref[0], v2_ref[0] = g, delta, m2, v2

        @pl.when(layer == 0)
        def _():
            run(r0_ref)

        @pl.when(layer == 1)
        def _():
            run(r1_ref)

    r0_spec = pl.BlockSpec((N_DEV, row_chunk, cols), lambda l, i: (0, i * (1 - l) + (n_chunks - 1) * l, 0))
    r1_spec = pl.BlockSpec((N_DEV, row_chunk, cols), lambda l, i: (0, i * l, 0))
    blk = pl.BlockSpec((1, row_chunk, cols), lambda l, i: (l, i, 0))
    out = jax.ShapeDtypeStruct(w.shape, F32)
    return pl.pallas_call(
        body,
        grid=(N_LAYERS, n_chunks),
        in_specs=[r0_spec, r1_spec, blk, blk, blk],
        out_specs=[blk] * 4,
        out_shape=[out] * 4,
        compiler_params=_cparams(2),
        name=name,
    )(recv0, recv1, w, m, v)


def _adam_dense(g, w, m, v, row_chunk, name):
    n_lead, rows, cols = w.shape

    def body(g_ref, w_ref, m_ref, v_ref, go_ref, d_ref, m2_ref, v2_ref):
        gv = g_ref[...]
        go_ref[...] = gv
        d_ref[...], m2_ref[...], v2_ref[...] = _adam_update(gv, w_ref[...], m_ref[...], v_ref[...])

    blk = pl.BlockSpec((1, row_chunk, cols), lambda l, i: (l, i, 0))
    out = jax.ShapeDtypeStruct(w.shape, F32)
    return pl.pallas_call(
        body,
        grid=(n_lead, rows // row_chunk),
        in_specs=[blk] * 4,
        out_specs=[blk] * 4,
        out_shape=[out] * 4,
        compiler_params=_cparams(2),
        name=name,
    )(g, w, m, v)


WEIGHT_NAMES = ("mod_w", "mod_b", "mix_pre_g", "mix_post_g", "w_in", "sgu_norm_g", "sgu_norm_b", "sgu_w", "sgu_b",
                "conv_w", "conv_b", "conv_norm_g", "conv_norm_b", "pool_w", "pool_scale", "branch_g", "w_out",
                "ffn_pre_g", "ffn_post_g", "ffn_up", "ffn_conv_w", "ffn_conv_b", "ffn_down")
SHARDED_BIG = ("w_in", "w_out", "ffn_up", "ffn_down")
SMALL_PACKED = tuple(n for n in WEIGHT_NAMES if n not in SHARDED_BIG + ("mod_w",))


def _rows8(rows, width=D_MODEL):
    out = [jnp.pad(r.astype(F32), (0, width - r.shape[0]))[None] for r in rows]
    out.append(jnp.zeros((8 - len(rows), width), F32))
    return jnp.concatenate(out, axis=0)


def _as_rows(a, width=D_MODEL):
    flat = a.astype(F32).reshape(-1)
    pad = (-flat.shape[0]) % width
    return jnp.pad(flat, (0, pad)).reshape(-1, width)


def _pad_cols(a, width=D_MODEL):
    return jnp.pad(a.astype(F32), ((0, 0), (0, width - a.shape[1])))


def _pack_rows(arrays):
    rows = jnp.concatenate([_as_rows(a) for a in arrays], axis=0)
    return jnp.pad(rows, ((0, (-rows.shape[0]) % 8), (0, 0)))


def _unpack_rows(packed, shapes):
    out, r = [], 0
    for shape in shapes:
        size = math.prod(shape)
        n_rows = -(-size // D_MODEL)
        out.append(packed[r:r + n_rows].reshape(-1)[:size].reshape(shape))
        r += n_rows
    return out


def _layer_consts(l, w, mod_rows, win, wout, conv_w_full):
    modv = _rows8(list(mod_rows[l].reshape(6, D_MODEL)))
    g1024 = _rows8([w["mix_pre_g"][l], w["mix_post_g"][l], w["branch_g"][l], w["ffn_pre_g"][l], w["ffn_post_g"][l]])
    p384 = _rows8([w["sgu_norm_g"][l], w["sgu_norm_b"][l], w["conv_b"][l], w["conv_norm_g"][l], w["conv_norm_b"][l]],
                  SGU_WIDTH)
    cw = jnp.pad(conv_w_full[l], ((0, 32 - CONV_K), (0, 0)))
    bmat = jnp.repeat(w["sgu_b"][l].T, HEAD_DIM, axis=1)
    groups = len(POOL_WINDOWS)
    eye = jnp.eye(groups, dtype=F32)
    pwbd = (eye[:, None, :, None] * w["pool_w"][l][:, :, None, :]).reshape(POOL_WIDTH, POOL_WIDTH).astype(BF16)
    psc = _rows8([w["pool_scale"][l]], POOL_WIDTH)
    seg = jnp.arange(SGU_WIDTH) // HEAD_DIM
    segp = jnp.where(seg[:, None] == seg[None, :], 1.0 / HEAD_DIM, 0.0).astype(BF16)
    return modv, g1024, (modv, g1024, p384, cw, w["sgu_w"][l], bmat, pwbd, psc, segp, win, wout)


def _small_grad_rows(mix, ffn):
    _, _, _, _, _, mvec, v384, dcw, dsguw, dbmat, dpw, dpsc = mix
    fvec, cgrad = ffn[5], ffn[6]
    dmod = jnp.stack([mvec[2], mvec[3], mvec[1], fvec[2], fvec[3], fvec[1]]).reshape(N_DEV, MOD_SHARD)
    g_rows = jnp.stack([mvec[4], mvec[0], mvec[5], fvec[4], fvec[0]])
    dsgu_b = dbmat[:, ::HEAD_DIM].T.reshape(1, SGU_HEADS * CHUNK)
    groups = len(POOL_WINDOWS)
    gdim = POOL_WIDTH // groups
    dpw4 = dpw.reshape(groups, gdim, groups, gdim)
    dpool = jnp.stack([dpw4[g, :, g, :] for g in range(groups)])
    blocks = [_pad_cols(dmod), _rows8(list(g_rows)), _pad_cols(v384), _rows8([dsgu_b[0], dpsc[0]]), _pad_cols(dcw),
              _pad_cols(cgrad[:, 0:4, :].reshape(4 * N_DEV, FF_SHARD)), _as_rows(dpool), _as_rows(dsguw)]
    return jnp.concatenate(blocks, axis=0)


def _small_grads_from_rows(total):
    per = {n: [] for n in SMALL_PACKED}
    for l in range(N_LAYERS):
        s = total[l]
        per["mod_b"].append(s[ROW_DMOD:ROW_DMOD + N_DEV, :MOD_SHARD].reshape(6 * D_MODEL))
        for j, name in enumerate(("mix_pre_g", "mix_post_g", "branch_g", "ffn_pre_g", "ffn_post_g")):
            per[name].append(s[ROW_G1024 + j])
        for j, name in enumerate(("sgu_norm_g", "sgu_norm_b", "conv_b", "conv_norm_g", "conv_norm_b")):
            per[name].append(s[ROW_V384 + j, :SGU_WIDTH])
        per["sgu_b"].append(s[ROW_SGU_B, :SGU_HEADS * CHUNK].reshape(SGU_HEADS, CHUNK))
        per["pool_scale"].append(s[ROW_POOL_SCALE, :POOL_WIDTH])
        per["conv_w"].append(s[ROW_CONV_W:ROW_CONV_W + CONV_K, :CONV_WIDTH])
        fc = s[ROW_FFN_CONV:ROW_FFN_CONV + 4 * N_DEV, :FF_SHARD].reshape(N_DEV, 4, FF_SHARD)
        per["ffn_conv_w"].append(fc[:, 0:3, :].transpose(1, 0, 2).reshape(FFN_CONV_K, 2 * D_FF))
        per["ffn_conv_b"].append(fc[:, 3, :].reshape(2 * D_FF))
        per["pool_w"].append(s[ROW_POOL_W:ROW_POOL_W + 16].reshape(len(POOL_WINDOWS), HEAD_DIM, HEAD_DIM))
        per["sgu_w"].append(s[ROW_SGU_W:ROW_SGU_W + 96].reshape(SGU_HEADS, CHUNK, CHUNK))
    return {n: jnp.stack(v) for n, v in per.items()}


def kernel(x, c, mod_w, mod_b, mix_pre_g, mix_post_g, w_in, sgu_norm_g, sgu_norm_b, sgu_w, sgu_b, conv_w, conv_b, conv_norm_g, conv_norm_b, pool_w, pool_scale, branch_g, w_out, ffn_pre_g, ffn_post_g, ffn_up, ffn_conv_w, ffn_conv_b, ffn_down, loss_target, m_mod_w, m_mod_b, m_mix_pre_g, m_mix_post_g, m_w_in, m_sgu_norm_g, m_sgu_norm_b, m_sgu_w, m_sgu_b, m_conv_w, m_conv_b, m_conv_norm_g, m_conv_norm_b, m_pool_w, m_pool_scale, m_branch_g, m_w_out, m_ffn_pre_g, m_ffn_post_g, m_ffn_up, m_ffn_conv_w, m_ffn_conv_b, m_ffn_down, v_mod_w, v_mod_b, v_mix_pre_g, v_mix_post_g, v_w_in, v_sgu_norm_g, v_sgu_norm_b, v_sgu_w, v_sgu_b, v_conv_w, v_conv_b, v_conv_norm_g, v_conv_norm_b, v_pool_w, v_pool_scale, v_branch_g, v_w_out, v_ffn_pre_g, v_ffn_post_g, v_ffn_up, v_ffn_conv_w, v_ffn_conv_b, v_ffn_down):
    w = dict(zip(WEIGHT_NAMES, (mod_w, mod_b, mix_pre_g, mix_post_g, w_in, sgu_norm_g, sgu_norm_b, sgu_w, sgu_b, conv_w,
                                conv_b, conv_norm_g, conv_norm_b, pool_w, pool_scale, branch_g, w_out, ffn_pre_g,
                                ffn_post_g, ffn_up, ffn_conv_w, ffn_conv_b, ffn_down)))
    m = dict(zip(WEIGHT_NAMES, (m_mod_w, m_mod_b, m_mix_pre_g, m_mix_post_g, m_w_in, m_sgu_norm_g, m_sgu_norm_b, m_sgu_w,
                                m_sgu_b, m_conv_w, m_conv_b, m_conv_norm_g, m_conv_norm_b, m_pool_w, m_pool_scale,
                                m_branch_g, m_w_out, m_ffn_pre_g, m_ffn_post_g, m_ffn_up, m_ffn_conv_w, m_ffn_conv_b,
                                m_ffn_down)))
    v = dict(zip(WEIGHT_NAMES, (v_mod_w, v_mod_b, v_mix_pre_g, v_mix_post_g, v_w_in, v_sgu_norm_g, v_sgu_norm_b, v_sgu_w,
                                v_sgu_b, v_conv_w, v_conv_b, v_conv_norm_g, v_conv_norm_b, v_pool_w, v_pool_scale,
                                v_branch_g, v_w_out, v_ffn_pre_g, v_ffn_post_g, v_ffn_up, v_ffn_conv_w, v_ffn_conv_b,
                                v_ffn_down)))
    me = _flat(_my_pos())
    xs = x[0]
    s_len = xs.shape[0]

    bf16_shards = [[w_in[l].astype(BF16), w_out[l].astype(BF16), ffn_up[l].astype(BF16), ffn_down[l].astype(BF16)]
                   for l in range(N_LAYERS)]

    def layer_operands(l, full, mod_rows, conv_w_full, ffn_cw_full):
        win_g, wout_g, wup_g, wdn_g = full
        win = win_g.transpose(1, 0, 2).reshape(D_MODEL, IN_WIDTH)
        wout = wout_g.reshape(D_MODEL, D_MODEL)
        wdn = wdn_g.reshape(FF_PAIRS, FF_SHARD, D_MODEL)
        modv, g1024, mix_consts = _layer_consts(l, w, mod_rows, win, wout, conv_w_full)
        ffn_consts = (modv, g1024, wup_g, wdn, ffn_cw_full[:, l], ffn_conv_b[l].reshape(N_DEV, 1, FF_SHARD))
        return mix_consts, ffn_consts

    c8 = jnp.broadcast_to(c, (8, D_MODEL))
    mod_b8 = jnp.pad(mod_b, ((0, 8 - N_LAYERS), (0, 0)))
    gathered = _gather_weights(c8, mod_w, mod_b8, bf16_shards[0] + [conv_w, ffn_conv_w])
    sc_all, mod_rows = gathered[0], gathered[1]
    conv_w_full = gathered[6].transpose(1, 2, 0, 3).reshape(N_LAYERS, CONV_K, CONV_WIDTH)
    ffn_cw_full = gathered[7]
    layers = [layer_operands(0, gathered[2:6], mod_rows, conv_w_full, ffn_cw_full), None]

    w1 = bf16_shards[1]
    (x1, z, o), (win1_g, wdn1_g) = _mixer_fwd(xs, *layers[0][0], name="mixer_fwd_l0",
                                              job=[("gather", w1[0]), ("gather", w1[3])])
    (x2, y2, p), (wout1_g, wup1_g) = _ffn_fwd(x1, *layers[0][1], name="ffn_fwd_l0",
                                              job=[("gather", w1[1]), ("gather", w1[2])])
    saved = [(xs, z, o, x1, y2, p)]
    layers[1] = layer_operands(1, (win1_g, wout1_g, wup1_g, wdn1_g), mod_rows, conv_w_full, ffn_cw_full)
    (x1, z, o), _ = _mixer_fwd(x2, *layers[1][0], name="mixer_fwd_l1")
    (x3, y2, p), _ = _ffn_fwd(x1, *layers[1][1], name="ffn_fwd_l1")
    saved.append((x2, z, o, x1, y2, p))
    dh, loss_tile = _loss_grad(x3, loss_target[0])
    loss = lax.psum(loss_tile[0, 0], ("x", "y", "c"))

    def weight_grads(l, ffn, mix):
        dp, a, dy2, h2 = ffn[1:5]
        dz, do, ycat, h1 = mix[1:5]
        d_up = _wgrad(h2, dp, f"wgrad_ffn_up_l{l}")
        d_dn = _wgrad(a, dy2, f"wgrad_ffn_down_l{l}").reshape(N_DEV, D_FF // N_DEV, D_MODEL)
        d_in = _wgrad(h1, dz[None], f"wgrad_w_in_l{l}")[0]
        d_in = d_in.reshape(D_MODEL, N_DEV, IN_WIDTH // N_DEV).transpose(1, 0, 2)
        d_out = _wgrad(ycat, do[None], f"wgrad_w_out_l{l}").reshape(N_DEV, D_MODEL // N_DEV, D_MODEL)
        return d_in, d_out, d_up, d_dn

    x_in, z, o, x1, y2, p = saved[1]
    ffn1, _ = _ffn_bwd(dh, x1, y2, p, *layers[1][1], name="ffn_bwd_l1")
    mix1, _ = _mixer_bwd(ffn1[0], x_in, o, z, *layers[1][0], name="mixer_bwd_l1")
    big1 = weight_grads(1, ffn1, mix1)
    small1 = _small_grad_rows(mix1, ffn1)

    x_in, z, o, x1, y2, p = saved[0]
    ffn0, job_out = _ffn_bwd(mix1[0], x1, y2, p, *layers[0][1], name="ffn_bwd_l0",
                             job=[("scatter", g) for g in big1] + [("gather", small1)])
    recv1, small_all1 = job_out[0:4], job_out[4]
    d_up0 = _wgrad(ffn0[4], ffn0[1], "wgrad_ffn_up_l0")
    d_dn0 = _wgrad(ffn0[2], ffn0[3], "wgrad_ffn_down_l0").reshape(N_DEV, D_FF // N_DEV, D_MODEL)
    mix0, recv_ffn0 = _mixer_bwd(ffn0[0], x_in, o, z, *layers[0][0], name="mixer_bwd_l0",
                                 job=[("scatter", d_up0), ("scatter", d_dn0)])
    d_in0 = _wgrad(mix0[4], mix0[1][None], "wgrad_w_in_l0")[0]
    d_in0 = d_in0.reshape(D_MODEL, N_DEV, IN_WIDTH // N_DEV).transpose(1, 0, 2)
    d_out0 = _wgrad(mix0[3], mix0[2][None], "wgrad_w_out_l0").reshape(N_DEV, D_MODEL // N_DEV, D_MODEL)
    grad_x = mix0[0][None]

    reduced = _reduce_grads(sc_all, _small_grad_rows(mix0, ffn0), small_all1, [d_in0, d_out0])
    small_total, g_mod_w = reduced[0], reduced[1]
    recv0 = [reduced[2], reduced[3], recv_ffn0[0], recv_ffn0[1]]

    grads, deltas, new_m, new_v = {}, {}, {}, {}
    for j, (name, chunk) in enumerate((("w_in", 256), ("w_out", 128), ("ffn_up", 256), ("ffn_down", 176))):
        grads[name], deltas[name], new_m[name], new_v[name] = _adam_sharded(
            recv0[j], recv1[j], w[name], m[name], v[name], chunk, "adam_" + name)
    grads["mod_w"], deltas["mod_w"], new_m["mod_w"], new_v["mod_w"] = _adam_dense(
        g_mod_w, mod_w, m_mod_w, v_mod_w, 256, "adam_mod_w")

    small_g = _small_grads_from_rows(small_total)
    small_g["conv_w"] = lax.dynamic_slice_in_dim(small_g["conv_w"], me * conv_w.shape[2], conv_w.shape[2], axis=2)
    small_g["ffn_conv_w"] = lax.dynamic_slice_in_dim(small_g["ffn_conv_w"], me * FF_SHARD, FF_SHARD, axis=2)
    shapes = [w[n].shape for n in SMALL_PACKED]
    packs = [_pack_rows([src[n] for n in SMALL_PACKED])[None] for src in (small_g, w, m, v)]
    _, d, m2, v2 = _adam_dense(*packs, packs[0].shape[1], "adam_small")
    for name, dd, mm, vv in zip(SMALL_PACKED, _unpack_rows(d[0], shapes), _unpack_rows(m2[0], shapes),
                                _unpack_rows(v2[0], shapes)):
        grads[name], deltas[name], new_m[name], new_v[name] = small_g[name], dd, mm, vv

    return (loss, grad_x, *[grads[n] for n in WEIGHT_NAMES], *[deltas[n] for n in WEIGHT_NAMES],
            *[new_m[n] for n in WEIGHT_NAMES], *[new_v[n] for n in WEIGHT_NAMES])
```

```python
import functools
import math

import jax
import jax.numpy as jnp
from jax import lax
from jax.experimental import pallas as pl
from jax.experimental.pallas import tpu as pltpu

F32 = jnp.float32
BF16 = jnp.bfloat16

D_MODEL = 1024
N_DEV = 8
SGU_WIDTH = 384
CONV_WIDTH = 384
POOL_WIDTH = 256
HEAD_DIM = 64
SGU_HEADS = 6
CHUNK = 128
CONV_K = 31
POOL_WINDOWS = (2, 4, 8, 16)
IN_WIDTH = 1792
D_FF = 2816
FF_SHARD = 2 * D_FF // N_DEV
FF_PAIRS = N_DEV // 2
FFN_CONV_K = 3
EPS = 1e-6
GELU_C0 = math.sqrt(2.0 / math.pi)
GELU_C1 = 0.044715

ADAM_LR = 0.001
ADAM_B1 = 0.9
ADAM_B2 = 0.999
ADAM_EPS = 1e-08
ADAM_WD = 0.01
ADAM_STEP = 10

VMEM_LIMIT_BYTES = 56 * 1024 * 1024
HALO = 32
MIX_TILE = 256
FFN_TILE = 256
WGRAD_TK = 2048
WGRAD_TK_FFN = 4096


def _cparams(n_axes):
    return pltpu.CompilerParams(dimension_semantics=("arbitrary",) * n_axes, vmem_limit_bytes=VMEM_LIMIT_BYTES)


def _whole(shape):
    nd = len(shape)
    return pl.BlockSpec(shape, lambda *_: (0,) * nd, pipeline_mode=pl.Buffered(1))


def _dot(a, b):
    return jnp.dot(a, b, preferred_element_type=F32)


def _dot_nt(a, b):
    return lax.dot_general(a, b, (((1,), (1,)), ((), ())), preferred_element_type=F32)


def _dot_tn(a, b):
    return lax.dot_general(a, b, (((0,), (0,)), ((), ())), preferred_element_type=F32)


def _gelu(x):
    t = jnp.tanh(GELU_C0 * (x + GELU_C1 * x * x * x))
    return 0.5 * x * (1.0 + t), t


def _gelu_grad(x, t):
    return 0.5 * (1.0 + t) + 0.5 * x * (1.0 - t * t) * (GELU_C0 * (1.0 + 3.0 * GELU_C1 * x * x))


def _rowmean(x):
    return jnp.mean(x, axis=-1, keepdims=True)


def _colsum(x):
    return jnp.sum(x, axis=0, keepdims=True)


def _rms_fwd(x):
    r = lax.rsqrt(_rowmean(x * x) + EPS)
    return x * r, r


def _rms_bwd(dxhat, xhat, r):
    return r * (dxhat - xhat * _rowmean(dxhat * xhat))


N_PEERS = N_DEV - 1
ANY = pl.BlockSpec(memory_space=pl.ANY)
VMEM = pl.BlockSpec(memory_space=pltpu.VMEM)


def _my_pos():
    return lax.axis_index("x"), lax.axis_index("y"), lax.axis_index("c")


def _peer(pos, k):
    x, y, c = pos
    return (1 - x if k & 4 else x, 1 - y if k & 2 else y, 1 - c if k & 1 else c)


def _flat(pos):
    return 4 * pos[0] + 2 * pos[1] + pos[2]


def _remote_copy(src, dst, send_sem, recv_sem, peer):
    return pltpu.make_async_remote_copy(src_ref=src, dst_ref=dst, send_sem=send_sem, recv_sem=recv_sem,
                                        device_id=peer, device_id_type=pl.DeviceIdType.MESH)


def _exchange_out_shapes(job):
    return [jax.ShapeDtypeStruct(((N_DEV,) + a.shape) if kind == "gather" else a.shape, a.dtype) for kind, a in job]


def _exchange_sems(n):
    return [pltpu.SemaphoreType.DMA((n, N_PEERS)), pltpu.SemaphoreType.DMA((n, N_PEERS)), pltpu.SemaphoreType.DMA((n,))]


def _exchange_copies(kinds, src_refs, dst_refs, send_sems, recv_sems, local_sems):
    pos = _my_pos()
    me = _flat(pos)
    copies = []
    for a, kind in enumerate(kinds):
        mine = src_refs[a] if kind == "gather" else src_refs[a].at[me]
        copies.append(pltpu.make_async_copy(mine, dst_refs[a].at[me], local_sems.at[a]))
        for k in range(N_PEERS):
            peer = _peer(pos, k + 1)
            src = src_refs[a] if kind == "gather" else src_refs[a].at[_flat(peer)]
            copies.append(_remote_copy(src, dst_refs[a].at[me], send_sems.at[a, k], recv_sems.at[a, k], peer))
    return copies


def _pallas_call_with_exchange(body, *, grid, in_specs, out_specs, out_shape, scratch_shapes, operands, name, job):
    params = _cparams(len(grid))
    if not job:
        outs = pl.pallas_call(body, grid=grid, in_specs=in_specs, out_specs=out_specs, out_shape=out_shape,
                              scratch_shapes=scratch_shapes, compiler_params=params, name=name)(*operands)
        return outs, []
    kinds = [kind for kind, _ in job]
    n, n_in, n_out, n_scr = len(job), len(in_specs), len(out_specs), len(scratch_shapes)

    def wrapped(*refs):
        ins, jin = refs[:n_in], refs[n_in:n_in + n]
        outs, jout = refs[n_in + n:n_in + n + n_out], refs[n_in + n + n_out:n_in + 2 * n + n_out]
        scr = refs[n_in + 2 * n + n_out:n_in + 2 * n + n_out + n_scr]
        sems = refs[n_in + 2 * n + n_out + n_scr:]
        first = functools.reduce(jnp.logical_and, [pl.program_id(d) == 0 for d in range(len(grid))])
        last = functools.reduce(jnp.logical_and, [pl.program_id(d) == grid[d] - 1 for d in range(len(grid))])

        @pl.when(first)
        def _():
            for cp in _exchange_copies(kinds, jin, jout, *sems):
                cp.start()

        body(*ins, *outs, *scr)

        @pl.when(last)
        def _():
            for cp in _exchange_copies(kinds, jin, jout, *sems):
                cp.wait()

    res = pl.pallas_call(
        wrapped, grid=grid,
        in_specs=list(in_specs) + [ANY] * n,
        out_specs=list(out_specs) + [ANY] * n,
        out_shape=list(out_shape) + _exchange_out_shapes(job),
        scratch_shapes=list(scratch_shapes) + _exchange_sems(n),
        compiler_params=params, name=name,
    )(*operands, *[a for _, a in job])
    return res[:n_out], res[n_out:]


def _seg_mean(x, segp):
    hi = x.astype(BF16)
    lo = (x - hi.astype(F32)).astype(BF16)
    return _dot(hi, segp) + _dot(lo, segp)


def _ffn_fwd(x1, modv, g1024, wup, wdn, cw, cb, name="ffn_fwd", job=None):
    s_len = x1.shape[0]
    t = FFN_TILE
    n_tiles = s_len // t

    def body(x1_ref, mod_ref, g_ref, wup_ref, wdn_ref, cw_ref, cb_ref, x2_ref, y2_ref, p_ref, ext_ref, carry_ref):
        i = pl.program_id(0)

        @pl.when(i == 0)
        def _():
            carry_ref[...] = jnp.zeros_like(carry_ref)

        x1v = x1_ref[...]
        pre_g, post_g = g_ref[3:4, :], g_ref[4:5, :]
        sh2, sc2, g2 = mod_ref[3:4, :], mod_ref[4:5, :], mod_ref[5:6, :]
        xhat, _ = _rms_fwd(x1v)
        h2b = (xhat * pre_g * (1.0 + sc2) + sh2).astype(BF16)

        def conv_shard(s):
            p = _dot_nt(h2b, wup_ref[s])
            p_ref[s] = p.astype(BF16)
            ext_ref[0:8, :] = carry_ref[s]
            ext_ref[8:8 + t, :] = p
            carry_ref[s] = p[t - 8:t, :]
            w = cw_ref[s]
            return w[0:1, :] * ext_ref[6:6 + t, :] + w[1:2, :] * ext_ref[7:7 + t, :] + w[2:3, :] * p + cb_ref[s]

        y2 = jnp.zeros((t, D_MODEL), F32)
        for j in range(FF_PAIRS):
            ug = conv_shard(j)
            uv = conv_shard(j + FF_PAIRS)
            ge, _ = _gelu(ug)
            y2 = y2 + _dot((ge * uv).astype(BF16), wdn_ref[j])
        y2_ref[...] = y2
        yhat, _ = _rms_fwd(y2)
        x2_ref[...] = x1v + g2 * (yhat * post_g)

    tile = pl.BlockSpec((t, D_MODEL), lambda i: (i, 0))
    return _pallas_call_with_exchange(
        body,
        grid=(n_tiles,),
        in_specs=[tile, _whole(modv.shape), _whole(g1024.shape), _whole(wup.shape), _whole(wdn.shape),
                  _whole(cw.shape), _whole(cb.shape)],
        out_specs=[tile, tile, pl.BlockSpec((N_DEV, t, FF_SHARD), lambda i: (0, i, 0))],
        out_shape=[jax.ShapeDtypeStruct((s_len, D_MODEL), F32), jax.ShapeDtypeStruct((s_len, D_MODEL), F32),
                   jax.ShapeDtypeStruct((N_DEV, s_len, FF_SHARD), BF16)],
        scratch_shapes=[pltpu.VMEM((8 + t, FF_SHARD), F32), pltpu.VMEM((N_DEV, 8, FF_SHARD), F32)],
        operands=(x1, modv, g1024, wup, wdn, cw, cb),
        name=name, job=job)


def _ffn_bwd(dx2, x1, y2, p, modv, g1024, wup, wdn, cw, cb, name="ffn_bwd", job=None):
    s_len = x1.shape[0]
    t = FFN_TILE
    n_tiles = s_len // t
    hb = 16

    def body(dx2_ref, x1_ref, y2_ref, p_ref, ph_ref, mod_ref, g_ref, wup_ref, wdn_ref, cw_ref, cb_ref,
             dx1_ref, dp_ref, a_ref, dy2_ref, h2_ref, vec_ref, cgrad_ref, ext_ref, dext_ref, dcarry_ref):
        i = pl.program_id(0)
        tile_idx = n_tiles - 1 - i

        @pl.when(i == 0)
        def _():
            vec_ref[...] = jnp.zeros_like(vec_ref)
            cgrad_ref[...] = jnp.zeros_like(cgrad_ref)
            dcarry_ref[...] = jnp.zeros_like(dcarry_ref)

        dx2v, x1v, y2v = dx2_ref[...], x1_ref[...], y2_ref[...]
        pre_g, post_g = g_ref[3:4, :], g_ref[4:5, :]
        sh2, sc2, g2 = mod_ref[3:4, :], mod_ref[4:5, :], mod_ref[5:6, :]

        yhat, ry = _rms_fwd(y2v)
        vec_ref[1:2, :] += _colsum(dx2v * (yhat * post_g))
        dyn = dx2v * g2
        vec_ref[0:1, :] += _colsum(dyn * yhat)
        dy2b = _rms_bwd(dyn * post_g, yhat, ry).astype(BF16)
        dy2_ref[...] = dy2b

        xhat, rx = _rms_fwd(x1v)
        xn = xhat * pre_g
        h2_ref[...] = (xn * (1.0 + sc2) + sh2).astype(BF16)

        not_first = (tile_idx > 0).astype(F32)

        def recompute(s, slot):
            pf = p_ref[s].astype(F32)
            ext_ref[slot, 0:8, :] = ph_ref[s][hb - 8:hb, :].astype(F32) * not_first
            ext_ref[slot, 8:8 + t, :] = pf
            w = cw_ref[s]
            u = (w[0:1, :] * ext_ref[slot, 6:6 + t, :] + w[1:2, :] * ext_ref[slot, 7:7 + t, :]
                 + w[2:3, :] * pf + cb_ref[s])
            return u

        def conv_bwd(s, slot, du):
            w = cw_ref[s]
            cgrad_ref[s, 0:1, :] += _colsum(du * ext_ref[slot, 6:6 + t, :])
            cgrad_ref[s, 1:2, :] += _colsum(du * ext_ref[slot, 7:7 + t, :])
            cgrad_ref[s, 2:3, :] += _colsum(du * ext_ref[slot, 8:8 + t, :])
            cgrad_ref[s, 3:4, :] += _colsum(du)
            dext_ref[0:t, :] = du
            dext_ref[t:t + 8, :] = dcarry_ref[s]
            dcarry_ref[s] = du[0:8, :]
            dp = w[2:3, :] * du + w[1:2, :] * dext_ref[1:1 + t, :] + w[0:1, :] * dext_ref[2:2 + t, :]
            dpb = dp.astype(BF16)
            dp_ref[s] = dpb
            return _dot(dpb, wup_ref[s])

        dh2 = jnp.zeros((t, D_MODEL), F32)
        for j in range(FF_PAIRS):
            ug = recompute(j, 0)
            uv = recompute(j + FF_PAIRS, 1)
            ge, th = _gelu(ug)
            a_ref[j] = (ge * uv).astype(BF16)
            da = _dot_nt(dy2b, wdn_ref[j])
            dh2 = dh2 + conv_bwd(j, 0, da * uv * _gelu_grad(ug, th))
            dh2 = dh2 + conv_bwd(j + FF_PAIRS, 1, da * ge)

        vec_ref[2:3, :] += _colsum(dh2)
        vec_ref[3:4, :] += _colsum(dh2 * xn)
        dxn = dh2 * (1.0 + sc2)
        vec_ref[4:5, :] += _colsum(dxn * xhat)
        dx1_ref[...] = dx2v + _rms_bwd(dxn * pre_g, xhat, rx)

    rev = lambda i: (n_tiles - 1 - i, 0)
    tile = pl.BlockSpec((t, D_MODEL), rev)
    halo_idx = lambda i: (0, jnp.maximum((n_tiles - 1 - i) * (t // hb) - 1, 0), 0)
    return _pallas_call_with_exchange(
        body,
        grid=(n_tiles,),
        in_specs=[tile, tile, tile,
                  pl.BlockSpec((N_DEV, t, FF_SHARD), lambda i: (0, n_tiles - 1 - i, 0)),
                  pl.BlockSpec((N_DEV, hb, FF_SHARD), halo_idx),
                  _whole(modv.shape), _whole(g1024.shape), _whole(wup.shape), _whole(wdn.shape),
                  _whole(cw.shape), _whole(cb.shape)],
        out_specs=[tile,
                   pl.BlockSpec((N_DEV, t, FF_SHARD), lambda i: (0, n_tiles - 1 - i, 0)),
                   pl.BlockSpec((FF_PAIRS, t, FF_SHARD), lambda i: (0, n_tiles - 1 - i, 0)),
                   tile, tile,
                   pl.BlockSpec((8, D_MODEL), lambda i: (0, 0)),
                   pl.BlockSpec((N_DEV, 8, FF_SHARD), lambda i: (0, 0, 0))],
        out_shape=[jax.ShapeDtypeStruct((s_len, D_MODEL), F32),
                   jax.ShapeDtypeStruct((N_DEV, s_len, FF_SHARD), BF16),
                   jax.ShapeDtypeStruct((FF_PAIRS, s_len, FF_SHARD), BF16),
                   jax.ShapeDtypeStruct((s_len, D_MODEL), BF16),
                   jax.ShapeDtypeStruct((s_len, D_MODEL), BF16),
                   jax.ShapeDtypeStruct((8, D_MODEL), F32),
                   jax.ShapeDtypeStruct((N_DEV, 8, FF_SHARD), F32)],
        scratch_shapes=[pltpu.VMEM((2, 8 + t, FF_SHARD), F32), pltpu.VMEM((t + 8, FF_SHARD), F32),
                        pltpu.VMEM((N_DEV, 8, FF_SHARD), F32)],
        operands=(dx2, x1, y2, p, p, modv, g1024, wup, wdn, cw, cb),
        name=name, job=job)


def _wgrad(a, b, name, tk=WGRAD_TK):
    a_grouped, b_grouped = a.ndim == 3, b.ndim == 3
    groups = a.shape[0] if a_grouped else b.shape[0]
    s_len, m, n = a.shape[-2], a.shape[-1], b.shape[-1]
    tk = min(tk, s_len)
    n_k = s_len // tk

    def body(a_ref, b_ref, o_ref, acc_ref):
        k = pl.program_id(1)
        av = a_ref[0] if a_grouped else a_ref[...]
        bv = b_ref[0] if b_grouped else b_ref[...]
        part = _dot_tn(av, bv)
        if n_k == 1:
            o_ref[0] = part.astype(BF16)
            return

        @pl.when(k == 0)
        def _():
            acc_ref[...] = part

        @pl.when(jnp.logical_and(k > 0, k < n_k - 1))
        def _():
            acc_ref[...] += part

        @pl.when(k == n_k - 1)
        def _():
            o_ref[0] = (acc_ref[...] + part).astype(BF16)

    a_spec = pl.BlockSpec((1, tk, m), lambda g, k: (g, k, 0)) if a_grouped else pl.BlockSpec((tk, m), lambda g, k: (k, 0))
    b_spec = pl.BlockSpec((1, tk, n), lambda g, k: (g, k, 0)) if b_grouped else pl.BlockSpec((tk, n), lambda g, k: (k, 0))
    return pl.pallas_call(
        body,
        grid=(groups, n_k),
        in_specs=[a_spec, b_spec],
        out_specs=pl.BlockSpec((1, m, n), lambda g, k: (g, 0, 0)),
        out_shape=jax.ShapeDtypeStruct((groups, m, n), BF16),
        scratch_shapes=[pltpu.VMEM((m, n), F32)],
        compiler_params=_cparams(2),
        name=name,
    )(a, b)


def _lane(shape):
    return lax.broadcasted_iota(jnp.int32, shape, 1)


def _by_pool_group(shape, vals):
    lane = _lane(shape)
    return jnp.where(lane < 64, vals[0], jnp.where(lane < 128, vals[1], jnp.where(lane < 192, vals[2], vals[3])))


def _pool_inv_counts(t, tile_idx):
    pos1 = lax.broadcasted_iota(jnp.int32, (t, 1), 0) + tile_idx * t + 1
    return [1.0 / jnp.minimum(pos1, w).astype(F32) for w in POOL_WINDOWS]


def _masked_sgu_w(sguw_ref):
    row = lax.broadcasted_iota(jnp.int32, (CHUNK, CHUNK), 0)
    col = lax.broadcasted_iota(jnp.int32, (CHUNK, CHUNK), 1)
    return [jnp.where(row >= col, sguw_ref[h], 0.0).astype(BF16) for h in range(SGU_HEADS)]


def _branches_fwd(z, tile_idx, p384_ref, cw_ref, wm, bmat_ref, pwbd_ref, psc_ref, segp_ref, g_ref, hext_ref, zext_ref):
    t = z.shape[0]
    segp = segp_ref[...]
    r = {}
    u, _ = _gelu(z[:, 0:SGU_WIDTH])
    vraw, _ = _gelu(z[:, SGU_WIDTH:2 * SGU_WIDTH])
    xc = vraw - _seg_mean(vraw, segp)
    rstd_v = lax.rsqrt(_seg_mean(xc * xc, segp) + EPS)
    xh_v = xc * rstd_v
    vnb = (xh_v * p384_ref[0:1, :] + p384_ref[1:2, :]).astype(BF16)
    first_head = _lane((CHUNK, CHUNK)) < HEAD_DIM
    f_rows = []
    for c in range(t // CHUNK):
        f_pairs = []
        for pr in range(SGU_HEADS // 2):
            vp = vnb[c * CHUNK:(c + 1) * CHUNK, pr * 128:(pr + 1) * 128]
            f_pairs.append(jnp.where(first_head, _dot(wm[2 * pr], vp), _dot(wm[2 * pr + 1], vp)))
        f_rows.append(jnp.concatenate(f_pairs, axis=1) + bmat_ref[...])
    f = jnp.concatenate(f_rows, axis=0)
    ya = u * f
    r.update(u=u, xh_v=xh_v, rstd_v=rstd_v, vnb=vnb, f=f)
    o_b = 2 * SGU_WIDTH
    a_in = z[:, o_b:o_b + CONV_WIDTH]
    sig_g = jax.nn.sigmoid(z[:, o_b + CONV_WIDTH:o_b + 2 * CONV_WIDTH])
    hext_ref[HALO:HALO + t, :] = a_in * sig_g
    conv = jnp.zeros((t, CONV_WIDTH), F32) + p384_ref[2:3, :]
    for k in range(CONV_K):
        conv = conv + cw_ref[k:k + 1, :] * hext_ref[pl.ds(HALO - (CONV_K - 1) + k, t), :]
    cc = conv - _rowmean(conv)
    rstd_c = lax.rsqrt(_rowmean(cc * cc) + EPS)
    xh_c = cc * rstd_c
    cn = xh_c * p384_ref[3:4, :] + p384_ref[4:5, :]
    sig_c = jax.nn.sigmoid(cn)
    yb = cn * sig_c
    r.update(a_in=a_in, sig_g=sig_g, xh_c=xh_c, rstd_c=rstd_c, cn=cn, sig_c=sig_c)
    o_c = o_b + 2 * CONV_WIDTH
    zc = z[:, o_c:o_c + POOL_WIDTH]
    zext_ref[HALO:HALO + t, :] = zc
    sums, acc = [], zc
    for j in range(1, POOL_WINDOWS[-1]):
        acc = acc + zext_ref[pl.ds(HALO - j, t), :]
        if j + 1 in POOL_WINDOWS:
            sums.append(acc)
    inv = _pool_inv_counts(t, tile_idx)
    dpool = _by_pool_group((t, POOL_WIDTH), [s * iv for s, iv in zip(sums, inv)]) - zc
    ycp = _dot(dpool.astype(BF16), pwbd_ref[...])
    yc = ycp * psc_ref[0:1, :]
    r.update(dpool=dpool, ycp=ycp)
    yha, ra = _rms_fwd(ya)
    yhb, rb = _rms_fwd(yb)
    yhc, rc = _rms_fwd(yc)
    bg = g_ref[2:3, :]
    ycat = jnp.concatenate([yha * bg[:, 0:384], yhb * bg[:, 384:768], yhc * bg[:, 768:1024]], axis=1)
    r.update(yha=yha, ra=ra, yhb=yhb, rb=rb, yhc=yhc, rc=rc, ycat=ycat)
    return r


def _mixer_fwd(x, modv, g1024, p384, cw, sguw, bmat, pwbd, psc, segp, win, wout, name="mixer_fwd", job=None):
    s_len = x.shape[0]
    t = MIX_TILE

    def body(x_ref, mod_ref, g_ref, p384_ref, cw_ref, sguw_ref, bmat_ref, pwbd_ref, psc_ref, segp_ref, win_ref, wout_ref,
             x1_ref, z_ref, o_ref, hext_ref, zext_ref):
        i = pl.program_id(0)

        @pl.when(i == 0)
        def _():
            hext_ref[0:HALO, :] = jnp.zeros((HALO, CONV_WIDTH), F32)
            zext_ref[0:HALO, :] = jnp.zeros((HALO, POOL_WIDTH), F32)

        xv = x_ref[...]
        sh1, sc1, g1 = mod_ref[0:1, :], mod_ref[1:2, :], mod_ref[2:3, :]
        xhat, _ = _rms_fwd(xv)
        h1 = xhat * g_ref[0:1, :] * (1.0 + sc1) + sh1
        z = _dot_nt(h1.astype(BF16), win_ref[...])
        z_ref[...] = z
        r = _branches_fwd(z, i, p384_ref, cw_ref, _masked_sgu_w(sguw_ref), bmat_ref, pwbd_ref, psc_ref, segp_ref,
                          g_ref, hext_ref, zext_ref)
        o = _dot(r["ycat"].astype(BF16), wout_ref[...])
        o_ref[...] = o
        ohat, _ = _rms_fwd(o)
        x1_ref[...] = xv + g1 * (ohat * g_ref[1:2, :])
        hext_ref[0:HALO, :] = hext_ref[t:t + HALO, :]
        zext_ref[0:HALO, :] = zext_ref[t:t + HALO, :]

    tile = pl.BlockSpec((t, D_MODEL), lambda i: (i, 0))
    consts = (modv, g1024, p384, cw, sguw, bmat, pwbd, psc, segp, win, wout)
    return _pallas_call_with_exchange(
        body,
        grid=(s_len // t,),
        in_specs=[tile] + [_whole(c.shape) for c in consts],
        out_specs=[tile, pl.BlockSpec((t, IN_WIDTH), lambda i: (i, 0)), tile],
        out_shape=[jax.ShapeDtypeStruct((s_len, D_MODEL), F32), jax.ShapeDtypeStruct((s_len, IN_WIDTH), F32),
                   jax.ShapeDtypeStruct((s_len, D_MODEL), F32)],
        scratch_shapes=[pltpu.VMEM((HALO + t, CONV_WIDTH), F32), pltpu.VMEM((HALO + t, POOL_WIDTH), F32)],
        operands=(x, *consts),
        name=name, job=job)


def _mixer_bwd(dx1, x, o, z, modv, g1024, p384, cw, sguw, bmat, pwbd, psc, segp, win, wout, name="mixer_bwd", job=None):
    s_len = x.shape[0]
    t = MIX_TILE
    n_tiles = s_len // t

    def body(dx1_ref, x_ref, o_ref, z_ref, zh_ref, mod_ref, g_ref, p384_ref, cw_ref, sguw_ref, bmat_ref, pwbd_ref,
             psc_ref, segp_ref, win_ref, wout_ref,
             dx_ref, dz_ref, do_ref, ycat_ref, h1_ref, vec_ref, v384_ref, dcw_ref, dsguw_ref, dbmat_ref, dpw_ref,
             dpsc_ref, hext_ref, zext_ref, gext_ref, qext_ref):
        i = pl.program_id(0)
        tile_idx = n_tiles - 1 - i

        @pl.when(i == 0)
        def _():
            for ref in (vec_ref, v384_ref, dcw_ref, dsguw_ref, dbmat_ref, dpw_ref, dpsc_ref):
                ref[...] = jnp.zeros_like(ref)
            gext_ref[t:t + HALO, :] = jnp.zeros((HALO, CONV_WIDTH), F32)
            qext_ref[t:t + HALO, :] = jnp.zeros((HALO, POOL_WIDTH), F32)

        dx1v, xv, ov, z = dx1_ref[...], x_ref[...], o_ref[...], z_ref[...]
        sh1, sc1, g1 = mod_ref[0:1, :], mod_ref[1:2, :], mod_ref[2:3, :]
        pre_g, post_g, bg = g_ref[0:1, :], g_ref[1:2, :], g_ref[2:3, :]
        segp = segp_ref[...]

        ohat, ro = _rms_fwd(ov)
        vec_ref[1:2, :] += _colsum(dx1v * (ohat * post_g))
        don = dx1v * g1
        vec_ref[0:1, :] += _colsum(don * ohat)
        dob = _rms_bwd(don * post_g, ohat, ro).astype(BF16)
        do_ref[...] = dob
        dycat = _dot_nt(dob, wout_ref[...])

        not_first = (tile_idx > 0).astype(F32)
        zh = zh_ref[...] * not_first
        o_b = 2 * SGU_WIDTH
        o_c = o_b + 2 * CONV_WIDTH
        hext_ref[0:HALO, :] = zh[:, o_b:o_b + CONV_WIDTH] * jax.nn.sigmoid(zh[:, o_b + CONV_WIDTH:o_c])
        zext_ref[0:HALO, :] = zh[:, o_c:o_c + POOL_WIDTH]
        wm = _masked_sgu_w(sguw_ref)
        r = _branches_fwd(z, tile_idx, p384_ref, cw_ref, wm, bmat_ref, pwbd_ref, psc_ref, segp_ref, g_ref,
                          hext_ref, zext_ref)
        ycat_ref[...] = r["ycat"].astype(BF16)

        def branch_norm_bwd(dyn, yhat, rr, gain):
            return _colsum(dyn * yhat), _rms_bwd(dyn * gain, yhat, rr)

        dga, dya = branch_norm_bwd(dycat[:, 0:384], r["yha"], r["ra"], bg[:, 0:384])
        dgb, dyb = branch_norm_bwd(dycat[:, 384:768], r["yhb"], r["rb"], bg[:, 384:768])
        dgc, dyc = branch_norm_bwd(dycat[:, 768:1024], r["yhc"], r["rc"], bg[:, 768:1024])
        vec_ref[5:6, :] += jnp.concatenate([dga, dgb, dgc], axis=1)

        du_act = dya * r["f"]
        df = dya * r["u"]
        first_head = _lane((CHUNK, CHUNK)) < HEAD_DIM
        wmt = [w.T for w in wm]
        dvn_rows = []
        for c in range(t // CHUNK):
            dfc = df[c * CHUNK:(c + 1) * CHUNK, :]
            dbmat_ref[...] += dfc
            dvn_pairs = []
            for pr in range(SGU_HEADS // 2):
                dfp = dfc[:, pr * 128:(pr + 1) * 128]
                df0 = jnp.where(first_head, dfp, 0.0).astype(BF16)
                df1 = jnp.where(first_head, 0.0, dfp).astype(BF16)
                vp = r["vnb"][c * CHUNK:(c + 1) * CHUNK, pr * 128:(pr + 1) * 128]
                dvn_pairs.append(_dot(wmt[2 * pr], df0) + _dot(wmt[2 * pr + 1], df1))
                dsguw_ref[2 * pr] += _dot_nt(df0, vp)
                dsguw_ref[2 * pr + 1] += _dot_nt(df1, vp)
            dvn_rows.append(jnp.concatenate(dvn_pairs, axis=1))
        dvn = jnp.concatenate(dvn_rows, axis=0)
        v384_ref[0:1, :] += _colsum(dvn * r["xh_v"])
        v384_ref[1:2, :] += _colsum(dvn)
        dxh = dvn * p384_ref[0:1, :]
        dvraw = r["rstd_v"] * (dxh - _seg_mean(dxh, segp) - r["xh_v"] * _seg_mean(dxh * r["xh_v"], segp))
        zu, zv = z[:, 0:SGU_WIDTH], z[:, SGU_WIDTH:o_b]
        _, tu = _gelu(zu)
        _, tv = _gelu(zv)
        dz_u = du_act * _gelu_grad(zu, tu)
        dz_v = dvraw * _gelu_grad(zv, tv)

        cn, sig_c = r["cn"], r["sig_c"]
        dcn = dyb * (sig_c * (1.0 + cn * (1.0 - sig_c)))
        v384_ref[3:4, :] += _colsum(dcn * r["xh_c"])
        v384_ref[4:5, :] += _colsum(dcn)
        dxc = dcn * p384_ref[3:4, :]
        gconv = r["rstd_c"] * (dxc - _rowmean(dxc) - r["xh_c"] * _rowmean(dxc * r["xh_c"]))
        v384_ref[2:3, :] += _colsum(gconv)
        gext_ref[0:t, :] = gconv
        dhh = jnp.zeros((t, CONV_WIDTH), F32)
        for k in range(CONV_K):
            dcw_ref[k:k + 1, :] += _colsum(gconv * hext_ref[pl.ds(HALO - (CONV_K - 1) + k, t), :])
            dhh = dhh + cw_ref[k:k + 1, :] * gext_ref[pl.ds(CONV_K - 1 - k, t), :]
        gext_ref[t:t + HALO, :] = gconv[0:HALO, :]
        sig_g = r["sig_g"]
        dz_a = dhh * sig_g
        dz_g = dhh * r["a_in"] * sig_g * (1.0 - sig_g)

        dpsc_ref[0:1, :] += _colsum(dyc * r["ycp"])
        dycp = (dyc * psc_ref[0:1, :]).astype(BF16)
        dpw_ref[...] += _dot_tn(r["dpool"].astype(BF16), dycp)
        ddp = _dot_nt(dycp, pwbd_ref[...])
        inv = _pool_inv_counts(t, tile_idx)
        q = ddp * _by_pool_group((t, POOL_WIDTH), inv)
        qext_ref[0:t, :] = q
        sums, acc = [], q
        for j in range(1, POOL_WINDOWS[-1]):
            acc = acc + qext_ref[pl.ds(j, t), :]
            if j + 1 in POOL_WINDOWS:
                sums.append(acc)
        qext_ref[t:t + HALO, :] = q[0:HALO, :]
        dz_c = _by_pool_group((t, POOL_WIDTH), sums) - ddp

        dzb = jnp.concatenate([dz_u, dz_v, dz_a, dz_g, dz_c], axis=1).astype(BF16)
        dz_ref[...] = dzb
        dh1 = _dot(dzb, win_ref[...])

        xhat, rx = _rms_fwd(xv)
        xn = xhat * pre_g
        h1_ref[...] = (xn * (1.0 + sc1) + sh1).astype(BF16)
        vec_ref[2:3, :] += _colsum(dh1)
        vec_ref[3:4, :] += _colsum(dh1 * xn)
        dxn = dh1 * (1.0 + sc1)
        vec_ref[4:5, :] += _colsum(dxn * xhat)
        dx_ref[...] = dx1v + _rms_bwd(dxn * pre_g, xhat, rx)

        @pl.when(i == n_tiles - 1)
        def _():
            row = lax.broadcasted_iota(jnp.int32, (CHUNK, CHUNK), 0)
            col = lax.broadcasted_iota(jnp.int32, (CHUNK, CHUNK), 1)
            for h in range(SGU_HEADS):
                dsguw_ref[h] = jnp.where(row >= col, dsguw_ref[h], 0.0)
            dbmat_ref[...] = float(HEAD_DIM) * _seg_mean(dbmat_ref[...], segp)

    rev = lambda i: (n_tiles - 1 - i, 0)
    tile = pl.BlockSpec((t, D_MODEL), rev)
    ztile = pl.BlockSpec((t, IN_WIDTH), rev)
    zhalo = pl.BlockSpec((HALO, IN_WIDTH), lambda i: (jnp.maximum((n_tiles - 1 - i) * (t // HALO) - 1, 0), 0))
    consts = (modv, g1024, p384, cw, sguw, bmat, pwbd, psc, segp, win, wout)
    acc = lambda shape: pl.BlockSpec(shape, lambda i: (0,) * len(shape))
    acc_shapes = [(8, D_MODEL), (8, SGU_WIDTH), (32, CONV_WIDTH), (SGU_HEADS, CHUNK, CHUNK), (CHUNK, SGU_WIDTH),
                  (POOL_WIDTH, POOL_WIDTH), (8, POOL_WIDTH)]
    return _pallas_call_with_exchange(
        body,
        grid=(n_tiles,),
        in_specs=[tile, tile, tile, ztile, zhalo] + [_whole(c.shape) for c in consts],
        out_specs=[tile, ztile, tile, tile, tile] + [acc(s) for s in acc_shapes],
        out_shape=[jax.ShapeDtypeStruct((s_len, D_MODEL), F32), jax.ShapeDtypeStruct((s_len, IN_WIDTH), BF16),
                   jax.ShapeDtypeStruct((s_len, D_MODEL), BF16), jax.ShapeDtypeStruct((s_len, D_MODEL), BF16),
                   jax.ShapeDtypeStruct((s_len, D_MODEL), BF16)] + [jax.ShapeDtypeStruct(s, F32) for s in acc_shapes],
        scratch_shapes=[pltpu.VMEM((HALO + t, CONV_WIDTH), F32), pltpu.VMEM((HALO + t, POOL_WIDTH), F32),
                        pltpu.VMEM((t + HALO, CONV_WIDTH), F32), pltpu.VMEM((t + HALO, POOL_WIDTH), F32)],
        operands=(dx1, x, o, z, z, *consts),
        name=name, job=job)


def _loss_grad(xo, target):
    s_len = xo.shape[0]
    t = 512

    def body(x_ref, t_ref, dx_ref, loss_ref):
        i = pl.program_id(0)

        @pl.when(i == 0)
        def _():
            loss_ref[...] = jnp.zeros_like(loss_ref)

        diff = x_ref[...] - t_ref[...]
        dx_ref[...] = diff * (1.0 / D_MODEL)
        loss_ref[...] += (0.5 / D_MODEL) * jnp.sum(diff * diff)

    tile = pl.BlockSpec((t, D_MODEL), lambda i: (i, 0))
    return pl.pallas_call(
        body,
        grid=(s_len // t,),
        in_specs=[tile, tile],
        out_specs=[tile, pl.BlockSpec((8, 128), lambda i: (0, 0))],
        out_shape=[jax.ShapeDtypeStruct((s_len, D_MODEL), F32), jax.ShapeDtypeStruct((8, 128), F32)],
        compiler_params=_cparams(1),
        name="loss_grad",
    )(xo, target)


MOD_SHARD = 6 * D_MODEL // N_DEV

ROW_DMOD = 0
ROW_G1024 = 8
ROW_V384 = 16
ROW_SGU_B = 24
ROW_POOL_SCALE = 25
ROW_CONV_W = 32
ROW_FFN_CONV = 64
ROW_POOL_W = 96
ROW_SGU_W = 112
ROWS_PER_LAYER = 208
N_LAYERS = 2


def _gather_weights(c8, mod_w, mod_b8, shards):
    n = len(shards)

    def body(c_ref, modw_ref, modb_ref, *rest):
        shard_refs = rest[:n]
        sc_all_ref, modrows_ref = rest[n], rest[n + 1]
        full_refs = rest[n + 2:2 * n + 2]
        send_buf, mod_recv, w_send, w_recv, w_local, sc_send, sc_recv, mod_send, mod_recv_sem = rest[2 * n + 2:]
        pos = _my_pos()
        me = _flat(pos)
        peers = [_peer(pos, k) for k in range(1, N_DEV)]

        own = [pltpu.make_async_copy(shard_refs[a], full_refs[a].at[me], w_local.at[a]) for a in range(n)]
        for cp in own:
            cp.start()
        w_copies = [_remote_copy(shard_refs[a], full_refs[a].at[me], w_send.at[a, k], w_recv.at[a, k], peers[k])
                    for a in range(n) for k in range(N_PEERS)]
        for cp in w_copies:
            cp.start()

        cv = c_ref[...]
        sc_all_ref[me] = cv * jax.nn.sigmoid(cv)
        sc_copies = [_remote_copy(sc_all_ref.at[me], sc_all_ref.at[me], sc_send.at[k], sc_recv.at[k], peers[k])
                     for k in range(N_PEERS)]
        for cp in sc_copies:
            cp.start()
        for cp in sc_copies:
            cp.wait()

        sc = jnp.concatenate([sc_all_ref[j, 0:1, :] for j in range(N_DEV)], axis=0)
        send_buf[...] = jnp.zeros_like(send_buf)
        for l in range(N_LAYERS):
            part = jnp.dot(sc, modw_ref[l], precision=lax.Precision.HIGHEST, preferred_element_type=F32)
            for j in range(N_DEV):
                send_buf[j, l:l + 1, :] = part[j:j + 1, :]
        mod_recv[me] = send_buf[me]
        mod_copies = [_remote_copy(send_buf.at[_flat(peers[k])], mod_recv.at[me], mod_send.at[k], mod_recv_sem.at[k],
                                   peers[k]) for k in range(N_PEERS)]
        for cp in mod_copies:
            cp.start()
        for cp in mod_copies:
            cp.wait()
        modrows_ref[...] = jnp.zeros_like(modrows_ref)
        for l in range(N_LAYERS):
            row = jnp.concatenate([mod_recv[j, l:l + 1, :] for j in range(N_DEV)], axis=1)
            modrows_ref[l:l + 1, :] = row + modb_ref[l:l + 1, :]

        for cp in w_copies:
            cp.wait()
        for cp in own:
            cp.wait()

    out_shape = ([jax.ShapeDtypeStruct((N_DEV, 8, D_MODEL), F32), jax.ShapeDtypeStruct((8, 6 * D_MODEL), F32)]
                 + [jax.ShapeDtypeStruct((N_DEV,) + s.shape, s.dtype) for s in shards])
    return pl.pallas_call(
        body,
        in_specs=[VMEM, VMEM, VMEM] + [ANY] * n,
        out_specs=[VMEM, VMEM] + [ANY] * n,
        out_shape=out_shape,
        scratch_shapes=[pltpu.VMEM((N_DEV, 8, MOD_SHARD), F32), pltpu.VMEM((N_DEV, 8, MOD_SHARD), F32),
                        pltpu.SemaphoreType.DMA((n, N_PEERS)), pltpu.SemaphoreType.DMA((n, N_PEERS)),
                        pltpu.SemaphoreType.DMA((n,)),
                        pltpu.SemaphoreType.DMA((N_PEERS,)), pltpu.SemaphoreType.DMA((N_PEERS,)),
                        pltpu.SemaphoreType.DMA((N_PEERS,)), pltpu.SemaphoreType.DMA((N_PEERS,))],
        compiler_params=pltpu.CompilerParams(vmem_limit_bytes=VMEM_LIMIT_BYTES),
        name="gather_weights",
    )(c8, mod_w, mod_b8, *shards)


def _reduce_grads(sc_all, small0, small_all1, bigs):
    n = len(bigs)
    kinds = ["scatter"] * n

    def body(sc_all_ref, small0_ref, small_all1_ref, *rest):
        big_refs = rest[:n]
        small_sum_ref, gmodw_ref = rest[n], rest[n + 1]
        recv_refs = rest[n + 2:2 * n + 2]
        small_all0, b_send, b_recv, b_local, s_send, s_recv = rest[2 * n + 2:]
        pos = _my_pos()
        me = _flat(pos)

        small_all0[me] = small0_ref[...]
        s_copies = [_remote_copy(small0_ref, small_all0.at[me], s_send.at[k], s_recv.at[k], _peer(pos, k + 1))
                    for k in range(N_PEERS)]
        for cp in s_copies:
            cp.start()
        b_copies = _exchange_copies(kinds, big_refs, recv_refs, b_send, b_recv, b_local)
        for cp in b_copies:
            cp.start()

        for cp in s_copies:
            cp.wait()
        sc = jnp.concatenate([sc_all_ref[j, 0:1, :] for j in range(N_DEV)], axis=0)
        for l, parts in enumerate((small_all0, small_all1_ref)):
            total = parts[0]
            for j in range(1, N_DEV):
                total = total + parts[j]
            small_sum_ref[l] = total
            dm = jnp.concatenate([parts[j, pl.ds(ROW_DMOD + me, 1), 0:MOD_SHARD] for j in range(N_DEV)], axis=0)
            gmodw_ref[l] = lax.dot_general(sc, dm, (((0,), (0,)), ((), ())), precision=lax.Precision.HIGHEST,
                                           preferred_element_type=F32)

        for cp in b_copies:
            cp.wait()

    out_shape = ([jax.ShapeDtypeStruct((N_LAYERS, ROWS_PER_LAYER, D_MODEL), F32),
                  jax.ShapeDtypeStruct((N_LAYERS, D_MODEL, MOD_SHARD), F32)]
                 + [jax.ShapeDtypeStruct(b.shape, b.dtype) for b in bigs])
    return pl.pallas_call(
        body,
        in_specs=[VMEM, VMEM, VMEM] + [ANY] * n,
        out_specs=[VMEM, VMEM] + [ANY] * n,
        out_shape=out_shape,
        scratch_shapes=[pltpu.VMEM((N_DEV, ROWS_PER_LAYER, D_MODEL), F32)] + _exchange_sems(n)
                       + [pltpu.SemaphoreType.DMA((N_PEERS,)), pltpu.SemaphoreType.DMA((N_PEERS,))],
        compiler_params=pltpu.CompilerParams(vmem_limit_bytes=VMEM_LIMIT_BYTES),
        name="reduce_grads",
    )(sc_all, small0, small_all1, *bigs)


def _adam_update(g, w, m, v):
    m2 = ADAM_B1 * m + (1.0 - ADAM_B1) * g
    v2 = ADAM_B2 * v + (1.0 - ADAM_B2) * (g * g)
    m_hat = m2 / (1.0 - ADAM_B1 ** ADAM_STEP)
    v_hat = v2 / (1.0 - ADAM_B2 ** ADAM_STEP)
    delta = -ADAM_LR * (m_hat / (jnp.sqrt(v_hat) + ADAM_EPS) + ADAM_WD * w)
    return delta, m2, v2


def _adam_sharded(recv0, recv1, w, m, v, row_chunk, name):
    _, rows, cols = w.shape
    n_chunks = rows // row_chunk

    def body(r0_ref, r1_ref, w_ref, m_ref, v_ref, g_ref, d_ref, m2_ref, v2_ref):
        layer = pl.program_id(0)

        def run(r_ref):
            g = r_ref[0].astype(F32)
            for j in range(1, N_DEV):
                g = g + r_ref[j].astype(F32)
            delta, m2, v2 = _adam_update(g, w_ref[0], m_ref[0], v_ref[0])
            g_ref[0], d_ref[0], m2_ref[0], v2_ref[0] = g, delta, m2, v2

        @pl.when(layer == 0)
        def _():
            run(r0_ref)

        @pl.when(layer == 1)
        def _():
            run(r1_ref)

    r0_spec = pl.BlockSpec((N_DEV, row_chunk, cols), lambda l, i: (0, i * (1 - l) + (n_chunks - 1) * l, 0))
    r1_spec = pl.BlockSpec((N_DEV, row_chunk, cols), lambda l, i: (0, i * l, 0))
    blk = pl.BlockSpec((1, row_chunk, cols), lambda l, i: (l, i, 0))
    out = jax.ShapeDtypeStruct(w.shape, F32)
    return pl.pallas_call(
        body,
        grid=(N_LAYERS, n_chunks),
        in_specs=[r0_spec, r1_spec, blk, blk, blk],
        out_specs=[blk] * 4,
        out_shape=[out] * 4,
        compiler_params=_cparams(2),
        name=name,
    )(recv0, recv1, w, m, v)


def _adam_dense(g, w, m, v, row_chunk, name):
    n_lead, rows, cols = w.shape

    def body(g_ref, w_ref, m_ref, v_ref, go_ref, d_ref, m2_ref, v2_ref):
        gv = g_ref[...]
        go_ref[...] = gv
        d_ref[...], m2_ref[...], v2_ref[...] = _adam_update(gv, w_ref[...], m_ref[...], v_ref[...])

    blk = pl.BlockSpec((1, row_chunk, cols), lambda l, i: (l, i, 0))
    out = jax.ShapeDtypeStruct(w.shape, F32)
    return pl.pallas_call(
        body,
        grid=(n_lead, rows // row_chunk),
        in_specs=[blk] * 4,
        out_specs=[blk] * 4,
        out_shape=[out] * 4,
        compiler_params=_cparams(2),
        name=name,
    )(g, w, m, v)


WEIGHT_NAMES = ("mod_w", "mod_b", "mix_pre_g", "mix_post_g", "w_in", "sgu_norm_g", "sgu_norm_b", "sgu_w", "sgu_b",
                "conv_w", "conv_b", "conv_norm_g", "conv_norm_b", "pool_w", "pool_scale", "branch_g", "w_out",
                "ffn_pre_g", "ffn_post_g", "ffn_up", "ffn_conv_w", "ffn_conv_b", "ffn_down")
SHARDED_BIG = ("w_in", "w_out", "ffn_up", "ffn_down")
SMALL_PACKED = tuple(n for n in WEIGHT_NAMES if n not in SHARDED_BIG + ("mod_w",))


def _rows8(rows, width=D_MODEL):
    out = [jnp.pad(r.astype(F32), (0, width - r.shape[0]))[None] for r in rows]
    out.append(jnp.zeros((8 - len(rows), width), F32))
    return jnp.concatenate(out, axis=0)


def _as_rows(a, width=D_MODEL):
    flat = a.astype(F32).reshape(-1)
    pad = (-flat.shape[0]) % width
    return jnp.pad(flat, (0, pad)).reshape(-1, width)


def _pad_cols(a, width=D_MODEL):
    return jnp.pad(a.astype(F32), ((0, 0), (0, width - a.shape[1])))


def _pack_rows(arrays):
    rows = jnp.concatenate([_as_rows(a) for a in arrays], axis=0)
    return jnp.pad(rows, ((0, (-rows.shape[0]) % 8), (0, 0)))


def _unpack_rows(packed, shapes):
    out, r = [], 0
    for shape in shapes:
        size = math.prod(shape)
        n_rows = -(-size // D_MODEL)
        out.append(packed[r:r + n_rows].reshape(-1)[:size].reshape(shape))
        r += n_rows
    return out


def _layer_consts(l, w, mod_rows, win, wout, conv_w_full):
    modv = _rows8(list(mod_rows[l].reshape(6, D_MODEL)))
    g1024 = _rows8([w["mix_pre_g"][l], w["mix_post_g"][l], w["branch_g"][l], w["ffn_pre_g"][l], w["ffn_post_g"][l]])
    p384 = _rows8([w["sgu_norm_g"][l], w["sgu_norm_b"][l], w["conv_b"][l], w["conv_norm_g"][l], w["conv_norm_b"][l]],
                  SGU_WIDTH)
    cw = jnp.pad(conv_w_full[l], ((0, 32 - CONV_K), (0, 0)))
    bmat = jnp.repeat(w["sgu_b"][l].T, HEAD_DIM, axis=1)
    groups = len(POOL_WINDOWS)
    eye = jnp.eye(groups, dtype=F32)
    pwbd = (eye[:, None, :, None] * w["pool_w"][l][:, :, None, :]).reshape(POOL_WIDTH, POOL_WIDTH).astype(BF16)
    psc = _rows8([w["pool_scale"][l]], POOL_WIDTH)
    seg = jnp.arange(SGU_WIDTH) // HEAD_DIM
    segp = jnp.where(seg[:, None] == seg[None, :], 1.0 / HEAD_DIM, 0.0).astype(BF16)
    return modv, g1024, (modv, g1024, p384, cw, w["sgu_w"][l], bmat, pwbd, psc, segp, win, wout)


def _small_grad_rows(mix, ffn):
    _, _, _, _, _, mvec, v384, dcw, dsguw, dbmat, dpw, dpsc = mix
    fvec, cgrad = ffn[5], ffn[6]
    dmod = jnp.stack([mvec[2], mvec[3], mvec[1], fvec[2], fvec[3], fvec[1]]).reshape(N_DEV, MOD_SHARD)
    g_rows = jnp.stack([mvec[4], mvec[0], mvec[5], fvec[4], fvec[0]])
    dsgu_b = dbmat[:, ::HEAD_DIM].T.reshape(1, SGU_HEADS * CHUNK)
    groups = len(POOL_WINDOWS)
    gdim = POOL_WIDTH // groups
    dpw4 = dpw.reshape(groups, gdim, groups, gdim)
    dpool = jnp.stack([dpw4[g, :, g, :] for g in range(groups)])
    blocks = [_pad_cols(dmod), _rows8(list(g_rows)), _pad_cols(v384), _rows8([dsgu_b[0], dpsc[0]]), _pad_cols(dcw),
              _pad_cols(cgrad[:, 0:4, :].reshape(4 * N_DEV, FF_SHARD)), _as_rows(dpool), _as_rows(dsguw)]
    return jnp.concatenate(blocks, axis=0)


def _small_grads_from_rows(total):
    per = {n: [] for n in SMALL_PACKED}
    for l in range(N_LAYERS):
        s = total[l]
        per["mod_b"].append(s[ROW_DMOD:ROW_DMOD + N_DEV, :MOD_SHARD].reshape(6 * D_MODEL))
        for j, name in enumerate(("mix_pre_g", "mix_post_g", "branch_g", "ffn_pre_g", "ffn_post_g")):
            per[name].append(s[ROW_G1024 + j])
        for j, name in enumerate(("sgu_norm_g", "sgu_norm_b", "conv_b", "conv_norm_g", "conv_norm_b")):
            per[name].append(s[ROW_V384 + j, :SGU_WIDTH])
        per["sgu_b"].append(s[ROW_SGU_B, :SGU_HEADS * CHUNK].reshape(SGU_HEADS, CHUNK))
        per["pool_scale"].append(s[ROW_POOL_SCALE, :POOL_WIDTH])
        per["conv_w"].append(s[ROW_CONV_W:ROW_CONV_W + CONV_K, :CONV_WIDTH])
        fc = s[ROW_FFN_CONV:ROW_FFN_CONV + 4 * N_DEV, :FF_SHARD].reshape(N_DEV, 4, FF_SHARD)
        per["ffn_conv_w"].append(fc[:, 0:3, :].transpose(1, 0, 2).reshape(FFN_CONV_K, 2 * D_FF))
        per["ffn_conv_b"].append(fc[:, 3, :].reshape(2 * D_FF))
        per["pool_w"].append(s[ROW_POOL_W:ROW_POOL_W + 16].reshape(len(POOL_WINDOWS), HEAD_DIM, HEAD_DIM))
        per["sgu_w"].append(s[ROW_SGU_W:ROW_SGU_W + 96].reshape(SGU_HEADS, CHUNK, CHUNK))
    return {n: jnp.stack(v) for n, v in per.items()}


def kernel(x, c, mod_w, mod_b, mix_pre_g, mix_post_g, w_in, sgu_norm_g, sgu_norm_b, sgu_w, sgu_b, conv_w, conv_b, conv_norm_g, conv_norm_b, pool_w, pool_scale, branch_g, w_out, ffn_pre_g, ffn_post_g, ffn_up, ffn_conv_w, ffn_conv_b, ffn_down, loss_target, m_mod_w, m_mod_b, m_mix_pre_g, m_mix_post_g, m_w_in, m_sgu_norm_g, m_sgu_norm_b, m_sgu_w, m_sgu_b, m_conv_w, m_conv_b, m_conv_norm_g, m_conv_norm_b, m_pool_w, m_pool_scale, m_branch_g, m_w_out, m_ffn_pre_g, m_ffn_post_g, m_ffn_up, m_ffn_conv_w, m_ffn_conv_b, m_ffn_down, v_mod_w, v_mod_b, v_mix_pre_g, v_mix_post_g, v_w_in, v_sgu_norm_g, v_sgu_norm_b, v_sgu_w, v_sgu_b, v_conv_w, v_conv_b, v_conv_norm_g, v_conv_norm_b, v_pool_w, v_pool_scale, v_branch_g, v_w_out, v_ffn_pre_g, v_ffn_post_g, v_ffn_up, v_ffn_conv_w, v_ffn_conv_b, v_ffn_down):
    w = dict(zip(WEIGHT_NAMES, (mod_w, mod_b, mix_pre_g, mix_post_g, w_in, sgu_norm_g, sgu_norm_b, sgu_w, sgu_b, conv_w,
                                conv_b, conv_norm_g, conv_norm_b, pool_w, pool_scale, branch_g, w_out, ffn_pre_g,
                                ffn_post_g, ffn_up, ffn_conv_w, ffn_conv_b, ffn_down)))
    m = dict(zip(WEIGHT_NAMES, (m_mod_w, m_mod_b, m_mix_pre_g, m_mix_post_g, m_w_in, m_sgu_norm_g, m_sgu_norm_b, m_sgu_w,
                                m_sgu_b, m_conv_w, m_conv_b, m_conv_norm_g, m_conv_norm_b, m_pool_w, m_pool_scale,
                                m_branch_g, m_w_out, m_ffn_pre_g, m_ffn_post_g, m_ffn_up, m_ffn_conv_w, m_ffn_conv_b,
                                m_ffn_down)))
    v = dict(zip(WEIGHT_NAMES, (v_mod_w, v_mod_b, v_mix_pre_g, v_mix_post_g, v_w_in, v_sgu_norm_g, v_sgu_norm_b, v_sgu_w,
                                v_sgu_b, v_conv_w, v_conv_b, v_conv_norm_g, v_conv_norm_b, v_pool_w, v_pool_scale,
                                v_branch_g, v_w_out, v_ffn_pre_g, v_ffn_post_g, v_ffn_up, v_ffn_conv_w, v_ffn_conv_b,
                                v_ffn_down)))
    me = _flat(_my_pos())
    xs = x[0]
    s_len = xs.shape[0]

    transposed = ("w_in", "ffn_up")
    wt = {n: jnp.swapaxes(w[n], 1, 2) if n in transposed else w[n] for n in SHARDED_BIG}
    mt = {n: jnp.swapaxes(m[n], 1, 2) if n in transposed else m[n] for n in SHARDED_BIG}
    vt = {n: jnp.swapaxes(v[n], 1, 2) if n in transposed else v[n] for n in SHARDED_BIG}
    bf16_shards = [[wt[n][l].astype(BF16) for n in SHARDED_BIG] for l in range(N_LAYERS)]

    def layer_operands(l, full, mod_rows, conv_w_full, ffn_cw_full):
        win_g, wout_g, wup_g, wdn_g = full
        win = win_g.reshape(IN_WIDTH, D_MODEL)
        wout = wout_g.reshape(D_MODEL, D_MODEL)
        wdn = wdn_g.reshape(FF_PAIRS, FF_SHARD, D_MODEL)
        modv, g1024, mix_consts = _layer_consts(l, w, mod_rows, win, wout, conv_w_full)
        ffn_consts = (modv, g1024, wup_g, wdn, ffn_cw_full[:, l], ffn_conv_b[l].reshape(N_DEV, 1, FF_SHARD))
        return mix_consts, ffn_consts

    c8 = jnp.broadcast_to(c, (8, D_MODEL))
    mod_b8 = jnp.pad(mod_b, ((0, 8 - N_LAYERS), (0, 0)))
    gathered = _gather_weights(c8, mod_w, mod_b8, bf16_shards[0] + [conv_w, ffn_conv_w])
    sc_all, mod_rows = gathered[0], gathered[1]
    conv_w_full = gathered[6].transpose(1, 2, 0, 3).reshape(N_LAYERS, CONV_K, CONV_WIDTH)
    ffn_cw_full = gathered[7]
    layers = [layer_operands(0, gathered[2:6], mod_rows, conv_w_full, ffn_cw_full), None]

    w1 = bf16_shards[1]
    (x1, z, o), (win1_g, wdn1_g) = _mixer_fwd(xs, *layers[0][0], name="mixer_fwd_l0",
                                              job=[("gather", w1[0]), ("gather", w1[3])])
    (x2, y2, p), (wout1_g, wup1_g) = _ffn_fwd(x1, *layers[0][1], name="ffn_fwd_l0",
                                              job=[("gather", w1[1]), ("gather", w1[2])])
    saved = [(xs, z, o, x1, y2, p)]
    layers[1] = layer_operands(1, (win1_g, wout1_g, wup1_g, wdn1_g), mod_rows, conv_w_full, ffn_cw_full)
    (x1, z, o), _ = _mixer_fwd(x2, *layers[1][0], name="mixer_fwd_l1")
    (x3, y2, p), _ = _ffn_fwd(x1, *layers[1][1], name="ffn_fwd_l1")
    saved.append((x2, z, o, x1, y2, p))
    dh, loss_tile = _loss_grad(x3, loss_target[0])
    loss = lax.psum(loss_tile[0, 0], ("x", "y", "c"))

    def weight_grads(l, ffn, mix):
        dp, a, dy2, h2 = ffn[1:5]
        dz, do, ycat, h1 = mix[1:5]
        d_up = _wgrad(dp, h2, f"wgrad_ffn_up_l{l}", tk=WGRAD_TK_FFN)
        d_dn = _wgrad(a, dy2, f"wgrad_ffn_down_l{l}", tk=WGRAD_TK_FFN).reshape(N_DEV, D_FF // N_DEV, D_MODEL)
        d_in = _wgrad(dz[None], h1, f"wgrad_w_in_l{l}").reshape(N_DEV, IN_WIDTH // N_DEV, D_MODEL)
        d_out = _wgrad(ycat, do[None], f"wgrad_w_out_l{l}").reshape(N_DEV, D_MODEL // N_DEV, D_MODEL)
        return d_in, d_out, d_up, d_dn

    x_in, z, o, x1, y2, p = saved[1]
    ffn1, _ = _ffn_bwd(dh, x1, y2, p, *layers[1][1], name="ffn_bwd_l1")
    mix1, _ = _mixer_bwd(ffn1[0], x_in, o, z, *layers[1][0], name="mixer_bwd_l1")
    big1 = weight_grads(1, ffn1, mix1)
    small1 = _small_grad_rows(mix1, ffn1)

    x_in, z, o, x1, y2, p = saved[0]
    ffn0, job_out = _ffn_bwd(mix1[0], x1, y2, p, *layers[0][1], name="ffn_bwd_l0",
                             job=[("scatter", g) for g in big1] + [("gather", small1)])
    recv1, small_all1 = job_out[0:4], job_out[4]
    d_up0 = _wgrad(ffn0[1], ffn0[4], "wgrad_ffn_up_l0", tk=WGRAD_TK_FFN)
    d_dn0 = _wgrad(ffn0[2], ffn0[3], "wgrad_ffn_down_l0", tk=WGRAD_TK_FFN).reshape(N_DEV, D_FF // N_DEV, D_MODEL)
    mix0, recv_ffn0 = _mixer_bwd(ffn0[0], x_in, o, z, *layers[0][0], name="mixer_bwd_l0",
                                 job=[("scatter", d_up0), ("scatter", d_dn0)])
    d_in0 = _wgrad(mix0[1][None], mix0[4], "wgrad_w_in_l0").reshape(N_DEV, IN_WIDTH // N_DEV, D_MODEL)
    d_out0 = _wgrad(mix0[3], mix0[2][None], "wgrad_w_out_l0").reshape(N_DEV, D_MODEL // N_DEV, D_MODEL)
    grad_x = mix0[0][None]

    reduced = _reduce_grads(sc_all, _small_grad_rows(mix0, ffn0), small_all1, [d_in0, d_out0])
    small_total, g_mod_w = reduced[0], reduced[1]
    recv0 = [reduced[2], reduced[3], recv_ffn0[0], recv_ffn0[1]]

    grads, deltas, new_m, new_v = {}, {}, {}, {}
    for j, (name, chunk) in enumerate((("w_in", 224), ("w_out", 128), ("ffn_up", 176), ("ffn_down", 176))):
        outs = _adam_sharded(recv0[j], recv1[j], wt[name], mt[name], vt[name], chunk, "adam_" + name)
        if name in transposed:
            outs = [jnp.swapaxes(t, 1, 2) for t in outs]
        grads[name], deltas[name], new_m[name], new_v[name] = outs
    grads["mod_w"], deltas["mod_w"], new_m["mod_w"], new_v["mod_w"] = _adam_dense(
        g_mod_w, mod_w, m_mod_w, v_mod_w, 256, "adam_mod_w")

    small_g = _small_grads_from_rows(small_total)
    small_g["conv_w"] = lax.dynamic_slice_in_dim(small_g["conv_w"], me * conv_w.shape[2], conv_w.shape[2], axis=2)
    small_g["ffn_conv_w"] = lax.dynamic_slice_in_dim(small_g["ffn_conv_w"], me * FF_SHARD, FF_SHARD, axis=2)
    shapes = [w[n].shape for n in SMALL_PACKED]
    packs = [_pack_rows([src[n] for n in SMALL_PACKED])[None] for src in (small_g, w, m, v)]
    _, d, m2, v2 = _adam_dense(*packs, packs[0].shape[1], "adam_small")
    for name, dd, mm, vv in zip(SMALL_PACKED, _unpack_rows(d[0], shapes), _unpack_rows(m2[0], shapes),
                                _unpack_rows(v2[0], shapes)):
        grads[name], deltas[name], new_m[name], new_v[name] = small_g[name], dd, mm, vv

    return (loss, grad_x, *[grads[n] for n in WEIGHT_NAMES], *[deltas[n] for n in WEIGHT_NAMES],
            *[new_m[n] for n in WEIGHT_NAMES], *[new_v[n] for n in WEIGHT_NAMES])
```

```python
import functools
import math

import jax
import jax.numpy as jnp
from jax import lax
from jax.experimental import pallas as pl
from jax.experimental.pallas import tpu as pltpu

F32 = jnp.float32
BF16 = jnp.bfloat16

D_MODEL = 1024
N_DEV = 8
SGU_WIDTH = 384
CONV_WIDTH = 384
POOL_WIDTH = 256
HEAD_DIM = 64
SGU_HEADS = 6
CHUNK = 128
CONV_K = 31
POOL_WINDOWS = (2, 4, 8, 16)
IN_WIDTH = 1792
D_FF = 2816
FF_SHARD = 2 * D_FF // N_DEV
FF_PAIRS = N_DEV // 2
FFN_CONV_K = 3
EPS = 1e-6
GELU_C0 = math.sqrt(2.0 / math.pi)
GELU_C1 = 0.044715

ADAM_LR = 0.001
ADAM_B1 = 0.9
ADAM_B2 = 0.999
ADAM_EPS = 1e-08
ADAM_WD = 0.01
ADAM_STEP = 10

VMEM_LIMIT_BYTES = 56 * 1024 * 1024
HALO = 32
MIX_TILE = 256
FFN_TILE = 256
WGRAD_TK = 2048
WGRAD_TK_FFN = 4096


def _cparams(n_axes):
    return pltpu.CompilerParams(dimension_semantics=("arbitrary",) * n_axes, vmem_limit_bytes=VMEM_LIMIT_BYTES)


def _whole(shape):
    nd = len(shape)
    return pl.BlockSpec(shape, lambda *_: (0,) * nd, pipeline_mode=pl.Buffered(1))


def _dot(a, b):
    return jnp.dot(a, b, preferred_element_type=F32)


def _dot_nt(a, b):
    return lax.dot_general(a, b, (((1,), (1,)), ((), ())), preferred_element_type=F32)


def _dot_tn(a, b):
    return lax.dot_general(a, b, (((0,), (0,)), ((), ())), preferred_element_type=F32)


def _gelu(x):
    t = jnp.tanh(GELU_C0 * (x + GELU_C1 * x * x * x))
    return 0.5 * x * (1.0 + t), t


def _gelu_grad(x, t):
    return 0.5 * (1.0 + t) + 0.5 * x * (1.0 - t * t) * (GELU_C0 * (1.0 + 3.0 * GELU_C1 * x * x))


def _rowmean(x):
    return jnp.mean(x, axis=-1, keepdims=True)


def _colsum(x):
    return jnp.sum(x, axis=0, keepdims=True)


def _rms_fwd(x):
    r = lax.rsqrt(_rowmean(x * x) + EPS)
    return x * r, r


def _rms_bwd(dxhat, xhat, r):
    return r * (dxhat - xhat * _rowmean(dxhat * xhat))


N_PEERS = N_DEV - 1
ANY = pl.BlockSpec(memory_space=pl.ANY)
VMEM = pl.BlockSpec(memory_space=pltpu.VMEM)


def _my_pos():
    return lax.axis_index("x"), lax.axis_index("y"), lax.axis_index("c")


def _peer(pos, k):
    x, y, c = pos
    return (1 - x if k & 4 else x, 1 - y if k & 2 else y, 1 - c if k & 1 else c)


def _flat(pos):
    return 4 * pos[0] + 2 * pos[1] + pos[2]


def _remote_copy(src, dst, send_sem, recv_sem, peer):
    return pltpu.make_async_remote_copy(src_ref=src, dst_ref=dst, send_sem=send_sem, recv_sem=recv_sem,
                                        device_id=peer, device_id_type=pl.DeviceIdType.MESH)


N_CHIPS = N_DEV // 2
SIBLING = 1
SAME_CORE_PEERS = (2, 4, 6)


def _exchange_out_shapes(job):
    def shape(kind, a):
        if kind in ("gather", "gather2"):
            return (N_DEV,) + a.shape
        if kind == "scatter_p1":
            return (N_CHIPS,) + a.shape[1:]
        return a.shape
    return [jax.ShapeDtypeStruct(shape(kind, a), a.dtype) for kind, a in job]


def _exchange_sems(n):
    return [pltpu.SemaphoreType.DMA((n, N_PEERS)), pltpu.SemaphoreType.DMA((n, N_PEERS)), pltpu.SemaphoreType.DMA((n,))]


def _exchange_copies(kinds, src_refs, dst_refs, send_sems, recv_sems, local_sems, phase):
    pos = _my_pos()
    me = _flat(pos)
    chip, core = 2 * pos[0] + pos[1], pos[2]
    copies = []

    def remote(a, src, dst, k, sem=None):
        sem = k - 1 if sem is None else sem
        copies.append(_remote_copy(src, dst, send_sems.at[a, sem], recv_sems.at[a, sem], _peer(pos, k)))

    for a, kind in enumerate(kinds):
        src, dst = src_refs[a], dst_refs[a]
        if phase == 1:
            if kind == "gather2":
                for k in SAME_CORE_PEERS:
                    remote(a, dst.at[me ^ k], dst.at[me ^ k], SIBLING, sem=k)
        elif kind in ("gather", "gather2"):
            copies.append(pltpu.make_async_copy(src, dst.at[me], local_sems.at[a]))
            for k in (range(1, N_DEV) if kind == "gather" else (SIBLING,) + SAME_CORE_PEERS):
                remote(a, src, dst.at[me], k)
        elif kind == "scatter":
            copies.append(pltpu.make_async_copy(src.at[me], dst.at[me], local_sems.at[a]))
            for k in range(1, N_DEV):
                remote(a, src.at[me ^ k], dst.at[me], k)
        elif kind == "scatter_p1":
            for q in range(N_CHIPS):
                remote(a, src.at[2 * q + 1 - core], dst.at[q], SIBLING, sem=q)
        elif kind == "scatter_p2":
            copies.append(pltpu.make_async_copy(src.at[chip], dst.at[chip], local_sems.at[a]))
            for k in SAME_CORE_PEERS:
                remote(a, src.at[chip ^ (k >> 1)], dst.at[chip], k)
    return copies


def _pallas_call_with_exchange(body, *, grid, in_specs, out_specs, out_shape, scratch_shapes, operands, name, job):
    params = _cparams(len(grid))
    if not job:
        outs = pl.pallas_call(body, grid=grid, in_specs=in_specs, out_specs=out_specs, out_shape=out_shape,
                              scratch_shapes=scratch_shapes, compiler_params=params, name=name)(*operands)
        return outs, []
    kinds = [kind for kind, _ in job]
    relayed = [kind if kind == "gather2" else None for kind in kinds]
    unrelayed = [None if kind == "gather2" else kind for kind in kinds]
    n, n_in, n_out, n_scr = len(job), len(in_specs), len(out_specs), len(scratch_shapes)
    (n_steps,) = grid
    relay_step = max(n_steps - 2, 0)

    def wrapped(*refs):
        ins, jin = refs[:n_in], refs[n_in:n_in + n]
        outs, jout = refs[n_in + n:n_in + n + n_out], refs[n_in + n + n_out:n_in + 2 * n + n_out]
        scr = refs[n_in + 2 * n + n_out:n_in + 2 * n + n_out + n_scr]
        sems = refs[n_in + 2 * n + n_out + n_scr:]
        step = pl.program_id(0)

        def copies(which, phase):
            return _exchange_copies(which, jin, jout, *sems, phase=phase)

        @pl.when(step == 0)
        def _():
            for cp in copies(kinds, 0):
                cp.start()

        body(*ins, *outs, *scr)

        @pl.when(step == relay_step)
        def _():
            for cp in copies(relayed, 0):
                cp.wait()
            for cp in copies(relayed, 1):
                cp.start()

        @pl.when(step == n_steps - 1)
        def _():
            for cp in copies(unrelayed, 0) + copies(relayed, 1):
                cp.wait()

    res = pl.pallas_call(
        wrapped, grid=grid,
        in_specs=list(in_specs) + [ANY] * n,
        out_specs=list(out_specs) + [ANY] * n,
        out_shape=list(out_shape) + _exchange_out_shapes(job),
        scratch_shapes=list(scratch_shapes) + _exchange_sems(n),
        compiler_params=params, name=name,
    )(*operands, *[a for _, a in job])
    return res[:n_out], res[n_out:]


def _seg_mean(x, segp):
    hi = x.astype(BF16)
    lo = (x - hi.astype(F32)).astype(BF16)
    return _dot(hi, segp) + _dot(lo, segp)


def _ffn_fwd(x1, modv, g1024, wup, wdn, cw, cb, name="ffn_fwd", job=None):
    s_len = x1.shape[0]
    t = FFN_TILE
    n_tiles = s_len // t

    def body(x1_ref, mod_ref, g_ref, wup_ref, wdn_ref, cw_ref, cb_ref, x2_ref, y2_ref, p_ref, ext_ref, carry_ref):
        i = pl.program_id(0)

        @pl.when(i == 0)
        def _():
            carry_ref[...] = jnp.zeros_like(carry_ref)

        x1v = x1_ref[...]
        pre_g, post_g = g_ref[3:4, :], g_ref[4:5, :]
        sh2, sc2, g2 = mod_ref[3:4, :], mod_ref[4:5, :], mod_ref[5:6, :]
        xhat, _ = _rms_fwd(x1v)
        h2b = (xhat * pre_g * (1.0 + sc2) + sh2).astype(BF16)

        def conv_shard(s):
            p = _dot_nt(h2b, wup_ref[s])
            p_ref[s] = p.astype(BF16)
            ext_ref[0:8, :] = carry_ref[s]
            ext_ref[8:8 + t, :] = p
            carry_ref[s] = p[t - 8:t, :]
            w = cw_ref[s]
            return w[0:1, :] * ext_ref[6:6 + t, :] + w[1:2, :] * ext_ref[7:7 + t, :] + w[2:3, :] * p + cb_ref[s]

        y2 = jnp.zeros((t, D_MODEL), F32)
        for j in range(FF_PAIRS):
            ug = conv_shard(j)
            uv = conv_shard(j + FF_PAIRS)
            ge, _ = _gelu(ug)
            y2 = y2 + _dot((ge * uv).astype(BF16), wdn_ref[j])
        y2_ref[...] = y2
        yhat, _ = _rms_fwd(y2)
        x2_ref[...] = x1v + g2 * (yhat * post_g)

    tile = pl.BlockSpec((t, D_MODEL), lambda i: (i, 0))
    return _pallas_call_with_exchange(
        body,
        grid=(n_tiles,),
        in_specs=[tile, _whole(modv.shape), _whole(g1024.shape), _whole(wup.shape), _whole(wdn.shape),
                  _whole(cw.shape), _whole(cb.shape)],
        out_specs=[tile, tile, pl.BlockSpec((N_DEV, t, FF_SHARD), lambda i: (0, i, 0))],
        out_shape=[jax.ShapeDtypeStruct((s_len, D_MODEL), F32), jax.ShapeDtypeStruct((s_len, D_MODEL), F32),
                   jax.ShapeDtypeStruct((N_DEV, s_len, FF_SHARD), BF16)],
        scratch_shapes=[pltpu.VMEM((8 + t, FF_SHARD), F32), pltpu.VMEM((N_DEV, 8, FF_SHARD), F32)],
        operands=(x1, modv, g1024, wup, wdn, cw, cb),
        name=name, job=job)


def _ffn_bwd(dx2, x1, y2, p, modv, g1024, wup, wdn, cw, cb, name="ffn_bwd", job=None):
    s_len = x1.shape[0]
    t = FFN_TILE
    n_tiles = s_len // t
    hb = 16

    def body(dx2_ref, x1_ref, y2_ref, p_ref, ph_ref, mod_ref, g_ref, wup_ref, wdn_ref, cw_ref, cb_ref,
             dx1_ref, dp_ref, a_ref, dy2_ref, h2_ref, vec_ref, cgrad_ref, ext_ref, dext_ref, dcarry_ref):
        i = pl.program_id(0)
        tile_idx = n_tiles - 1 - i

        @pl.when(i == 0)
        def _():
            vec_ref[...] = jnp.zeros_like(vec_ref)
            cgrad_ref[...] = jnp.zeros_like(cgrad_ref)
            dcarry_ref[...] = jnp.zeros_like(dcarry_ref)

        dx2v, x1v, y2v = dx2_ref[...], x1_ref[...], y2_ref[...]
        pre_g, post_g = g_ref[3:4, :], g_ref[4:5, :]
        sh2, sc2, g2 = mod_ref[3:4, :], mod_ref[4:5, :], mod_ref[5:6, :]

        yhat, ry = _rms_fwd(y2v)
        vec_ref[1:2, :] += _colsum(dx2v * (yhat * post_g))
        dyn = dx2v * g2
        vec_ref[0:1, :] += _colsum(dyn * yhat)
        dy2b = _rms_bwd(dyn * post_g, yhat, ry).astype(BF16)
        dy2_ref[...] = dy2b

        xhat, rx = _rms_fwd(x1v)
        xn = xhat * pre_g
        h2_ref[...] = (xn * (1.0 + sc2) + sh2).astype(BF16)

        not_first = (tile_idx > 0).astype(F32)

        def recompute(s, slot):
            pf = p_ref[s].astype(F32)
            ext_ref[slot, 0:8, :] = ph_ref[s][hb - 8:hb, :].astype(F32) * not_first
            ext_ref[slot, 8:8 + t, :] = pf
            w = cw_ref[s]
            u = (w[0:1, :] * ext_ref[slot, 6:6 + t, :] + w[1:2, :] * ext_ref[slot, 7:7 + t, :]
                 + w[2:3, :] * pf + cb_ref[s])
            return u

        def conv_bwd(s, slot, du):
            w = cw_ref[s]
            cgrad_ref[s, 0:1, :] += _colsum(du * ext_ref[slot, 6:6 + t, :])
            cgrad_ref[s, 1:2, :] += _colsum(du * ext_ref[slot, 7:7 + t, :])
            cgrad_ref[s, 2:3, :] += _colsum(du * ext_ref[slot, 8:8 + t, :])
            cgrad_ref[s, 3:4, :] += _colsum(du)
            dext_ref[0:t, :] = du
            dext_ref[t:t + 8, :] = dcarry_ref[s]
            dcarry_ref[s] = du[0:8, :]
            dp = w[2:3, :] * du + w[1:2, :] * dext_ref[1:1 + t, :] + w[0:1, :] * dext_ref[2:2 + t, :]
            dpb = dp.astype(BF16)
            dp_ref[s] = dpb
            return _dot(dpb, wup_ref[s])

        dh2 = jnp.zeros((t, D_MODEL), F32)
        for j in range(FF_PAIRS):
            ug = recompute(j, 0)
            uv = recompute(j + FF_PAIRS, 1)
            ge, th = _gelu(ug)
            a_ref[j] = (ge * uv).astype(BF16)
            da = _dot_nt(dy2b, wdn_ref[j])
            dh2 = dh2 + conv_bwd(j, 0, da * uv * _gelu_grad(ug, th))
            dh2 = dh2 + conv_bwd(j + FF_PAIRS, 1, da * ge)

        vec_ref[2:3, :] += _colsum(dh2)
        vec_ref[3:4, :] += _colsum(dh2 * xn)
        dxn = dh2 * (1.0 + sc2)
        vec_ref[4:5, :] += _colsum(dxn * xhat)
        dx1_ref[...] = dx2v + _rms_bwd(dxn * pre_g, xhat, rx)

    rev = lambda i: (n_tiles - 1 - i, 0)
    tile = pl.BlockSpec((t, D_MODEL), rev)
    halo_idx = lambda i: (0, jnp.maximum((n_tiles - 1 - i) * (t // hb) - 1, 0), 0)
    return _pallas_call_with_exchange(
        body,
        grid=(n_tiles,),
        in_specs=[tile, tile, tile,
                  pl.BlockSpec((N_DEV, t, FF_SHARD), lambda i: (0, n_tiles - 1 - i, 0)),
                  pl.BlockSpec((N_DEV, hb, FF_SHARD), halo_idx),
                  _whole(modv.shape), _whole(g1024.shape), _whole(wup.shape), _whole(wdn.shape),
                  _whole(cw.shape), _whole(cb.shape)],
        out_specs=[tile,
                   pl.BlockSpec((N_DEV, t, FF_SHARD), lambda i: (0, n_tiles - 1 - i, 0)),
                   pl.BlockSpec((FF_PAIRS, t, FF_SHARD), lambda i: (0, n_tiles - 1 - i, 0)),
                   tile, tile,
                   pl.BlockSpec((8, D_MODEL), lambda i: (0, 0)),
                   pl.BlockSpec((N_DEV, 8, FF_SHARD), lambda i: (0, 0, 0))],
        out_shape=[jax.ShapeDtypeStruct((s_len, D_MODEL), F32),
                   jax.ShapeDtypeStruct((N_DEV, s_len, FF_SHARD), BF16),
                   jax.ShapeDtypeStruct((FF_PAIRS, s_len, FF_SHARD), BF16),
                   jax.ShapeDtypeStruct((s_len, D_MODEL), BF16),
                   jax.ShapeDtypeStruct((s_len, D_MODEL), BF16),
                   jax.ShapeDtypeStruct((8, D_MODEL), F32),
                   jax.ShapeDtypeStruct((N_DEV, 8, FF_SHARD), F32)],
        scratch_shapes=[pltpu.VMEM((2, 8 + t, FF_SHARD), F32), pltpu.VMEM((t + 8, FF_SHARD), F32),
                        pltpu.VMEM((N_DEV, 8, FF_SHARD), F32)],
        operands=(dx2, x1, y2, p, p, modv, g1024, wup, wdn, cw, cb),
        name=name, job=job)


def _wgrad(a, b, name, tk=WGRAD_TK):
    a_grouped, b_grouped = a.ndim == 3, b.ndim == 3
    groups = a.shape[0] if a_grouped else b.shape[0]
    s_len, m, n = a.shape[-2], a.shape[-1], b.shape[-1]
    tk = min(tk, s_len)
    n_k = s_len // tk

    def body(a_ref, b_ref, o_ref, acc_ref):
        k = pl.program_id(1)
        av = a_ref[0] if a_grouped else a_ref[...]
        bv = b_ref[0] if b_grouped else b_ref[...]
        part = _dot_tn(av, bv)
        if n_k == 1:
            o_ref[0] = part.astype(BF16)
            return

        @pl.when(k == 0)
        def _():
            acc_ref[...] = part

        @pl.when(jnp.logical_and(k > 0, k < n_k - 1))
        def _():
            acc_ref[...] += part

        @pl.when(k == n_k - 1)
        def _():
            o_ref[0] = (acc_ref[...] + part).astype(BF16)

    a_spec = pl.BlockSpec((1, tk, m), lambda g, k: (g, k, 0)) if a_grouped else pl.BlockSpec((tk, m), lambda g, k: (k, 0))
    b_spec = pl.BlockSpec((1, tk, n), lambda g, k: (g, k, 0)) if b_grouped else pl.BlockSpec((tk, n), lambda g, k: (k, 0))
    return pl.pallas_call(
        body,
        grid=(groups, n_k),
        in_specs=[a_spec, b_spec],
        out_specs=pl.BlockSpec((1, m, n), lambda g, k: (g, 0, 0)),
        out_shape=jax.ShapeDtypeStruct((groups, m, n), BF16),
        scratch_shapes=[pltpu.VMEM((m, n), F32)],
        compiler_params=_cparams(2),
        name=name,
    )(a, b)


def _lane(shape):
    return lax.broadcasted_iota(jnp.int32, shape, 1)


def _by_pool_group(shape, vals):
    lane = _lane(shape)
    return jnp.where(lane < 64, vals[0], jnp.where(lane < 128, vals[1], jnp.where(lane < 192, vals[2], vals[3])))


def _pool_inv_counts(t, tile_idx):
    pos1 = lax.broadcasted_iota(jnp.int32, (t, 1), 0) + tile_idx * t + 1
    return [1.0 / jnp.minimum(pos1, w).astype(F32) for w in POOL_WINDOWS]


def _masked_sgu_w(sguw_ref):
    row = lax.broadcasted_iota(jnp.int32, (CHUNK, CHUNK), 0)
    col = lax.broadcasted_iota(jnp.int32, (CHUNK, CHUNK), 1)
    return [jnp.where(row >= col, sguw_ref[h], 0.0).astype(BF16) for h in range(SGU_HEADS)]


def _branches_fwd(z, tile_idx, p384_ref, cw_ref, wm, bmat_ref, pwbd_ref, psc_ref, segp_ref, g_ref, hext_ref, zext_ref):
    t = z.shape[0]
    segp = segp_ref[...]
    r = {}
    u, _ = _gelu(z[:, 0:SGU_WIDTH])
    vraw, _ = _gelu(z[:, SGU_WIDTH:2 * SGU_WIDTH])
    xc = vraw - _seg_mean(vraw, segp)
    rstd_v = lax.rsqrt(_seg_mean(xc * xc, segp) + EPS)
    xh_v = xc * rstd_v
    vnb = (xh_v * p384_ref[0:1, :] + p384_ref[1:2, :]).astype(BF16)
    first_head = _lane((CHUNK, CHUNK)) < HEAD_DIM
    f_rows = []
    for c in range(t // CHUNK):
        f_pairs = []
        for pr in range(SGU_HEADS // 2):
            vp = vnb[c * CHUNK:(c + 1) * CHUNK, pr * 128:(pr + 1) * 128]
            f_pairs.append(jnp.where(first_head, _dot(wm[2 * pr], vp), _dot(wm[2 * pr + 1], vp)))
        f_rows.append(jnp.concatenate(f_pairs, axis=1) + bmat_ref[...])
    f = jnp.concatenate(f_rows, axis=0)
    ya = u * f
    r.update(u=u, xh_v=xh_v, rstd_v=rstd_v, vnb=vnb, f=f)
    o_b = 2 * SGU_WIDTH
    a_in = z[:, o_b:o_b + CONV_WIDTH]
    sig_g = jax.nn.sigmoid(z[:, o_b + CONV_WIDTH:o_b + 2 * CONV_WIDTH])
    hext_ref[HALO:HALO + t, :] = a_in * sig_g
    conv = jnp.zeros((t, CONV_WIDTH), F32) + p384_ref[2:3, :]
    for k in range(CONV_K):
        conv = conv + cw_ref[k:k + 1, :] * hext_ref[pl.ds(HALO - (CONV_K - 1) + k, t), :]
    cc = conv - _rowmean(conv)
    rstd_c = lax.rsqrt(_rowmean(cc * cc) + EPS)
    xh_c = cc * rstd_c
    cn = xh_c * p384_ref[3:4, :] + p384_ref[4:5, :]
    sig_c = jax.nn.sigmoid(cn)
    yb = cn * sig_c
    r.update(a_in=a_in, sig_g=sig_g, xh_c=xh_c, rstd_c=rstd_c, cn=cn, sig_c=sig_c)
    o_c = o_b + 2 * CONV_WIDTH
    zc = z[:, o_c:o_c + POOL_WIDTH]
    zext_ref[HALO:HALO + t, :] = zc
    sums, acc = [], zc
    for j in range(1, POOL_WINDOWS[-1]):
        acc = acc + zext_ref[pl.ds(HALO - j, t), :]
        if j + 1 in POOL_WINDOWS:
            sums.append(acc)
    inv = _pool_inv_counts(t, tile_idx)
    dpool = _by_pool_group((t, POOL_WIDTH), [s * iv for s, iv in zip(sums, inv)]) - zc
    ycp = _dot(dpool.astype(BF16), pwbd_ref[...])
    yc = ycp * psc_ref[0:1, :]
    r.update(dpool=dpool, ycp=ycp)
    yha, ra = _rms_fwd(ya)
    yhb, rb = _rms_fwd(yb)
    yhc, rc = _rms_fwd(yc)
    bg = g_ref[2:3, :]
    ycat = jnp.concatenate([yha * bg[:, 0:384], yhb * bg[:, 384:768], yhc * bg[:, 768:1024]], axis=1)
    r.update(yha=yha, ra=ra, yhb=yhb, rb=rb, yhc=yhc, rc=rc, ycat=ycat)
    return r


def _mixer_fwd(x, modv, g1024, p384, cw, sguw, bmat, pwbd, psc, segp, win, wout, name="mixer_fwd", job=None):
    s_len = x.shape[0]
    t = MIX_TILE

    def body(x_ref, mod_ref, g_ref, p384_ref, cw_ref, sguw_ref, bmat_ref, pwbd_ref, psc_ref, segp_ref, win_ref, wout_ref,
             x1_ref, z_ref, o_ref, hext_ref, zext_ref):
        i = pl.program_id(0)

        @pl.when(i == 0)
        def _():
            hext_ref[0:HALO, :] = jnp.zeros((HALO, CONV_WIDTH), F32)
            zext_ref[0:HALO, :] = jnp.zeros((HALO, POOL_WIDTH), F32)

        xv = x_ref[...]
        sh1, sc1, g1 = mod_ref[0:1, :], mod_ref[1:2, :], mod_ref[2:3, :]
        xhat, _ = _rms_fwd(xv)
        h1 = xhat * g_ref[0:1, :] * (1.0 + sc1) + sh1
        z = _dot_nt(h1.astype(BF16), win_ref[...])
        z_ref[...] = z
        r = _branches_fwd(z, i, p384_ref, cw_ref, _masked_sgu_w(sguw_ref), bmat_ref, pwbd_ref, psc_ref, segp_ref,
                          g_ref, hext_ref, zext_ref)
        o = _dot(r["ycat"].astype(BF16), wout_ref[...])
        o_ref[...] = o
        ohat, _ = _rms_fwd(o)
        x1_ref[...] = xv + g1 * (ohat * g_ref[1:2, :])
        hext_ref[0:HALO, :] = hext_ref[t:t + HALO, :]
        zext_ref[0:HALO, :] = zext_ref[t:t + HALO, :]

    tile = pl.BlockSpec((t, D_MODEL), lambda i: (i, 0))
    consts = (modv, g1024, p384, cw, sguw, bmat, pwbd, psc, segp, win, wout)
    return _pallas_call_with_exchange(
        body,
        grid=(s_len // t,),
        in_specs=[tile] + [_whole(c.shape) for c in consts],
        out_specs=[tile, pl.BlockSpec((t, IN_WIDTH), lambda i: (i, 0)), tile],
        out_shape=[jax.ShapeDtypeStruct((s_len, D_MODEL), F32), jax.ShapeDtypeStruct((s_len, IN_WIDTH), F32),
                   jax.ShapeDtypeStruct((s_len, D_MODEL), F32)],
        scratch_shapes=[pltpu.VMEM((HALO + t, CONV_WIDTH), F32), pltpu.VMEM((HALO + t, POOL_WIDTH), F32)],
        operands=(x, *consts),
        name=name, job=job)


def _mixer_bwd(dx1, x, o, z, modv, g1024, p384, cw, sguw, bmat, pwbd, psc, segp, win, wout, name="mixer_bwd", job=None):
    s_len = x.shape[0]
    t = MIX_TILE
    n_tiles = s_len // t

    def body(dx1_ref, x_ref, o_ref, z_ref, zh_ref, mod_ref, g_ref, p384_ref, cw_ref, sguw_ref, bmat_ref, pwbd_ref,
             psc_ref, segp_ref, win_ref, wout_ref,
             dx_ref, dz_ref, do_ref, ycat_ref, h1_ref, vec_ref, v384_ref, dcw_ref, dsguw_ref, dbmat_ref, dpw_ref,
             dpsc_ref, hext_ref, zext_ref, gext_ref, qext_ref):
        i = pl.program_id(0)
        tile_idx = n_tiles - 1 - i

        @pl.when(i == 0)
        def _():
            for ref in (vec_ref, v384_ref, dcw_ref, dsguw_ref, dbmat_ref, dpw_ref, dpsc_ref):
                ref[...] = jnp.zeros_like(ref)
            gext_ref[t:t + HALO, :] = jnp.zeros((HALO, CONV_WIDTH), F32)
            qext_ref[t:t + HALO, :] = jnp.zeros((HALO, POOL_WIDTH), F32)

        dx1v, xv, ov, z = dx1_ref[...], x_ref[...], o_ref[...], z_ref[...]
        sh1, sc1, g1 = mod_ref[0:1, :], mod_ref[1:2, :], mod_ref[2:3, :]
        pre_g, post_g, bg = g_ref[0:1, :], g_ref[1:2, :], g_ref[2:3, :]
        segp = segp_ref[...]

        ohat, ro = _rms_fwd(ov)
        vec_ref[1:2, :] += _colsum(dx1v * (ohat * post_g))
        don = dx1v * g1
        vec_ref[0:1, :] += _colsum(don * ohat)
        dob = _rms_bwd(don * post_g, ohat, ro).astype(BF16)
        do_ref[...] = dob
        dycat = _dot_nt(dob, wout_ref[...])

        not_first = (tile_idx > 0).astype(F32)
        zh = zh_ref[...] * not_first
        o_b = 2 * SGU_WIDTH
        o_c = o_b + 2 * CONV_WIDTH
        hext_ref[0:HALO, :] = zh[:, o_b:o_b + CONV_WIDTH] * jax.nn.sigmoid(zh[:, o_b + CONV_WIDTH:o_c])
        zext_ref[0:HALO, :] = zh[:, o_c:o_c + POOL_WIDTH]
        wm = _masked_sgu_w(sguw_ref)
        r = _branches_fwd(z, tile_idx, p384_ref, cw_ref, wm, bmat_ref, pwbd_ref, psc_ref, segp_ref, g_ref,
                          hext_ref, zext_ref)
        ycat_ref[...] = r["ycat"].astype(BF16)

        def branch_norm_bwd(dyn, yhat, rr, gain):
            return _colsum(dyn * yhat), _rms_bwd(dyn * gain, yhat, rr)

        dga, dya = branch_norm_bwd(dycat[:, 0:384], r["yha"], r["ra"], bg[:, 0:384])
        dgb, dyb = branch_norm_bwd(dycat[:, 384:768], r["yhb"], r["rb"], bg[:, 384:768])
        dgc, dyc = branch_norm_bwd(dycat[:, 768:1024], r["yhc"], r["rc"], bg[:, 768:1024])
        vec_ref[5:6, :] += jnp.concatenate([dga, dgb, dgc], axis=1)

        du_act = dya * r["f"]
        df = dya * r["u"]
        first_head = _lane((CHUNK, CHUNK)) < HEAD_DIM
        wmt = [w.T for w in wm]
        dvn_rows = []
        for c in range(t // CHUNK):
            dfc = df[c * CHUNK:(c + 1) * CHUNK, :]
            dbmat_ref[...] += dfc
            dvn_pairs = []
            for pr in range(SGU_HEADS // 2):
                dfp = dfc[:, pr * 128:(pr + 1) * 128]
                df0 = jnp.where(first_head, dfp, 0.0).astype(BF16)
                df1 = jnp.where(first_head, 0.0, dfp).astype(BF16)
                vp = r["vnb"][c * CHUNK:(c + 1) * CHUNK, pr * 128:(pr + 1) * 128]
                dvn_pairs.append(_dot(wmt[2 * pr], df0) + _dot(wmt[2 * pr + 1], df1))
                dsguw_ref[2 * pr] += _dot_nt(df0, vp)
                dsguw_ref[2 * pr + 1] += _dot_nt(df1, vp)
            dvn_rows.append(jnp.concatenate(dvn_pairs, axis=1))
        dvn = jnp.concatenate(dvn_rows, axis=0)
        v384_ref[0:1, :] += _colsum(dvn * r["xh_v"])
        v384_ref[1:2, :] += _colsum(dvn)
        dxh = dvn * p384_ref[0:1, :]
        dvraw = r["rstd_v"] * (dxh - _seg_mean(dxh, segp) - r["xh_v"] * _seg_mean(dxh * r["xh_v"], segp))
        zu, zv = z[:, 0:SGU_WIDTH], z[:, SGU_WIDTH:o_b]
        _, tu = _gelu(zu)
        _, tv = _gelu(zv)
        dz_u = du_act * _gelu_grad(zu, tu)
        dz_v = dvraw * _gelu_grad(zv, tv)

        cn, sig_c = r["cn"], r["sig_c"]
        dcn = dyb * (sig_c * (1.0 + cn * (1.0 - sig_c)))
        v384_ref[3:4, :] += _colsum(dcn * r["xh_c"])
        v384_ref[4:5, :] += _colsum(dcn)
        dxc = dcn * p384_ref[3:4, :]
        gconv = r["rstd_c"] * (dxc - _rowmean(dxc) - r["xh_c"] * _rowmean(dxc * r["xh_c"]))
        v384_ref[2:3, :] += _colsum(gconv)
        gext_ref[0:t, :] = gconv
        dhh = jnp.zeros((t, CONV_WIDTH), F32)
        for k in range(CONV_K):
            dcw_ref[k:k + 1, :] += _colsum(gconv * hext_ref[pl.ds(HALO - (CONV_K - 1) + k, t), :])
            dhh = dhh + cw_ref[k:k + 1, :] * gext_ref[pl.ds(CONV_K - 1 - k, t), :]
        gext_ref[t:t + HALO, :] = gconv[0:HALO, :]
        sig_g = r["sig_g"]
        dz_a = dhh * sig_g
        dz_g = dhh * r["a_in"] * sig_g * (1.0 - sig_g)

        dpsc_ref[0:1, :] += _colsum(dyc * r["ycp"])
        dycp = (dyc * psc_ref[0:1, :]).astype(BF16)
        dpw_ref[...] += _dot_tn(r["dpool"].astype(BF16), dycp)
        ddp = _dot_nt(dycp, pwbd_ref[...])
        inv = _pool_inv_counts(t, tile_idx)
        q = ddp * _by_pool_group((t, POOL_WIDTH), inv)
        qext_ref[0:t, :] = q
        sums, acc = [], q
        for j in range(1, POOL_WINDOWS[-1]):
            acc = acc + qext_ref[pl.ds(j, t), :]
            if j + 1 in POOL_WINDOWS:
                sums.append(acc)
        qext_ref[t:t + HALO, :] = q[0:HALO, :]
        dz_c = _by_pool_group((t, POOL_WIDTH), sums) - ddp

        dzb = jnp.concatenate([dz_u, dz_v, dz_a, dz_g, dz_c], axis=1).astype(BF16)
        dz_ref[...] = dzb
        dh1 = _dot(dzb, win_ref[...])

        xhat, rx = _rms_fwd(xv)
        xn = xhat * pre_g
        h1_ref[...] = (xn * (1.0 + sc1) + sh1).astype(BF16)
        vec_ref[2:3, :] += _colsum(dh1)
        vec_ref[3:4, :] += _colsum(dh1 * xn)
        dxn = dh1 * (1.0 + sc1)
        vec_ref[4:5, :] += _colsum(dxn * xhat)
        dx_ref[...] = dx1v + _rms_bwd(dxn * pre_g, xhat, rx)

        @pl.when(i == n_tiles - 1)
        def _():
            row = lax.broadcasted_iota(jnp.int32, (CHUNK, CHUNK), 0)
            col = lax.broadcasted_iota(jnp.int32, (CHUNK, CHUNK), 1)
            for h in range(SGU_HEADS):
                dsguw_ref[h] = jnp.where(row >= col, dsguw_ref[h], 0.0)
            dbmat_ref[...] = float(HEAD_DIM) * _seg_mean(dbmat_ref[...], segp)

    rev = lambda i: (n_tiles - 1 - i, 0)
    tile = pl.BlockSpec((t, D_MODEL), rev)
    ztile = pl.BlockSpec((t, IN_WIDTH), rev)
    zhalo = pl.BlockSpec((HALO, IN_WIDTH), lambda i: (jnp.maximum((n_tiles - 1 - i) * (t // HALO) - 1, 0), 0))
    consts = (modv, g1024, p384, cw, sguw, bmat, pwbd, psc, segp, win, wout)
    acc = lambda shape: pl.BlockSpec(shape, lambda i: (0,) * len(shape))
    acc_shapes = [(8, D_MODEL), (8, SGU_WIDTH), (32, CONV_WIDTH), (SGU_HEADS, CHUNK, CHUNK), (CHUNK, SGU_WIDTH),
                  (POOL_WIDTH, POOL_WIDTH), (8, POOL_WIDTH)]
    return _pallas_call_with_exchange(
        body,
        grid=(n_tiles,),
        in_specs=[tile, tile, tile, ztile, zhalo] + [_whole(c.shape) for c in consts],
        out_specs=[tile, ztile, tile, tile, tile] + [acc(s) for s in acc_shapes],
        out_shape=[jax.ShapeDtypeStruct((s_len, D_MODEL), F32), jax.ShapeDtypeStruct((s_len, IN_WIDTH), BF16),
                   jax.ShapeDtypeStruct((s_len, D_MODEL), BF16), jax.ShapeDtypeStruct((s_len, D_MODEL), BF16),
                   jax.ShapeDtypeStruct((s_len, D_MODEL), BF16)] + [jax.ShapeDtypeStruct(s, F32) for s in acc_shapes],
        scratch_shapes=[pltpu.VMEM((HALO + t, CONV_WIDTH), F32), pltpu.VMEM((HALO + t, POOL_WIDTH), F32),
                        pltpu.VMEM((t + HALO, CONV_WIDTH), F32), pltpu.VMEM((t + HALO, POOL_WIDTH), F32)],
        operands=(dx1, x, o, z, z, *consts),
        name=name, job=job)


def _loss_grad(xo, target):
    s_len = xo.shape[0]
    t = 512

    def body(x_ref, t_ref, dx_ref, loss_ref):
        i = pl.program_id(0)

        @pl.when(i == 0)
        def _():
            loss_ref[...] = jnp.zeros_like(loss_ref)

        diff = x_ref[...] - t_ref[...]
        dx_ref[...] = diff * (1.0 / D_MODEL)
        loss_ref[...] += (0.5 / D_MODEL) * jnp.sum(diff * diff)

    tile = pl.BlockSpec((t, D_MODEL), lambda i: (i, 0))
    return pl.pallas_call(
        body,
        grid=(s_len // t,),
        in_specs=[tile, tile],
        out_specs=[tile, pl.BlockSpec((8, 128), lambda i: (0, 0))],
        out_shape=[jax.ShapeDtypeStruct((s_len, D_MODEL), F32), jax.ShapeDtypeStruct((8, 128), F32)],
        compiler_params=_cparams(1),
        name="loss_grad",
    )(xo, target)


MOD_SHARD = 6 * D_MODEL // N_DEV

ROW_DMOD = 0
ROW_G1024 = 8
ROW_V384 = 16
ROW_SGU_B = 24
ROW_POOL_SCALE = 25
ROW_CONV_W = 32
ROW_FFN_CONV = 64
ROW_POOL_W = 96
ROW_SGU_W = 112
ROWS_PER_LAYER = 208
N_LAYERS = 2


def _gather_weights(c8, mod_w, mod_b8, job):
    kinds = [kind for kind, _ in job]
    shards = [a for _, a in job]
    n = len(shards)

    def body(c_ref, modw_ref, modb_ref, *rest):
        shard_refs = rest[:n]
        sc_all_ref, modrows_ref = rest[n], rest[n + 1]
        full_refs = rest[n + 2:2 * n + 2]
        send_buf, mod_recv, w_send, w_recv, w_local, sc_send, sc_recv, mod_send, mod_recv_sem = rest[2 * n + 2:]
        pos = _my_pos()
        me = _flat(pos)
        peers = [_peer(pos, k) for k in range(1, N_DEV)]

        w_copies = _exchange_copies(kinds, shard_refs, full_refs, w_send, w_recv, w_local, phase=0)
        for cp in w_copies:
            cp.start()

        cv = c_ref[...]
        sc_all_ref[me] = cv * jax.nn.sigmoid(cv)
        sc_copies = [_remote_copy(sc_all_ref.at[me], sc_all_ref.at[me], sc_send.at[k], sc_recv.at[k], peers[k])
                     for k in range(N_PEERS)]
        for cp in sc_copies:
            cp.start()
        for cp in sc_copies:
            cp.wait()

        sc = jnp.concatenate([sc_all_ref[j, 0:1, :] for j in range(N_DEV)], axis=0)
        send_buf[...] = jnp.zeros_like(send_buf)
        for l in range(N_LAYERS):
            part = jnp.dot(sc, modw_ref[l], precision=lax.Precision.HIGHEST, preferred_element_type=F32)
            for j in range(N_DEV):
                send_buf[j, l:l + 1, :] = part[j:j + 1, :]
        mod_recv[me] = send_buf[me]
        mod_copies = [_remote_copy(send_buf.at[_flat(peers[k])], mod_recv.at[me], mod_send.at[k], mod_recv_sem.at[k],
                                   peers[k]) for k in range(N_PEERS)]
        for cp in mod_copies:
            cp.start()
        for cp in mod_copies:
            cp.wait()
        modrows_ref[...] = jnp.zeros_like(modrows_ref)
        for l in range(N_LAYERS):
            row = jnp.concatenate([mod_recv[j, l:l + 1, :] for j in range(N_DEV)], axis=1)
            modrows_ref[l:l + 1, :] = row + modb_ref[l:l + 1, :]

        for cp in w_copies:
            cp.wait()
        relays = _exchange_copies(kinds, shard_refs, full_refs, w_send, w_recv, w_local, phase=1)
        for cp in relays:
            cp.start()
        for cp in relays:
            cp.wait()

    out_shape = ([jax.ShapeDtypeStruct((N_DEV, 8, D_MODEL), F32), jax.ShapeDtypeStruct((8, 6 * D_MODEL), F32)]
                 + [jax.ShapeDtypeStruct((N_DEV,) + s.shape, s.dtype) for s in shards])
    return pl.pallas_call(
        body,
        in_specs=[VMEM, VMEM, VMEM] + [ANY] * n,
        out_specs=[VMEM, VMEM] + [ANY] * n,
        out_shape=out_shape,
        scratch_shapes=[pltpu.VMEM((N_DEV, 8, MOD_SHARD), F32), pltpu.VMEM((N_DEV, 8, MOD_SHARD), F32),
                        pltpu.SemaphoreType.DMA((n, N_PEERS)), pltpu.SemaphoreType.DMA((n, N_PEERS)),
                        pltpu.SemaphoreType.DMA((n,)),
                        pltpu.SemaphoreType.DMA((N_PEERS,)), pltpu.SemaphoreType.DMA((N_PEERS,)),
                        pltpu.SemaphoreType.DMA((N_PEERS,)), pltpu.SemaphoreType.DMA((N_PEERS,))],
        compiler_params=pltpu.CompilerParams(vmem_limit_bytes=VMEM_LIMIT_BYTES),
        name="gather_weights",
    )(c8, mod_w, mod_b8, *shards)


def _reduce_grads(sc_all, small0, small_all1, bigs):
    n = len(bigs)
    kinds = ["scatter"] * n

    def body(sc_all_ref, small0_ref, small_all1_ref, *rest):
        big_refs = rest[:n]
        small_sum_ref, gmodw_ref = rest[n], rest[n + 1]
        recv_refs = rest[n + 2:2 * n + 2]
        small_all0, b_send, b_recv, b_local, s_send, s_recv = rest[2 * n + 2:]
        pos = _my_pos()
        me = _flat(pos)

        small_all0[me] = small0_ref[...]
        s_copies = [_remote_copy(small0_ref, small_all0.at[me], s_send.at[k], s_recv.at[k], _peer(pos, k + 1))
                    for k in range(N_PEERS)]
        for cp in s_copies:
            cp.start()
        b_copies = _exchange_copies(kinds, big_refs, recv_refs, b_send, b_recv, b_local, phase=0)
        for cp in b_copies:
            cp.start()

        for cp in s_copies:
            cp.wait()
        sc = jnp.concatenate([sc_all_ref[j, 0:1, :] for j in range(N_DEV)], axis=0)
        for l, parts in enumerate((small_all0, small_all1_ref)):
            total = parts[0]
            for j in range(1, N_DEV):
                total = total + parts[j]
            small_sum_ref[l] = total
            dm = jnp.concatenate([parts[j, pl.ds(ROW_DMOD + me, 1), 0:MOD_SHARD] for j in range(N_DEV)], axis=0)
            gmodw_ref[l] = lax.dot_general(sc, dm, (((0,), (0,)), ((), ())), precision=lax.Precision.HIGHEST,
                                           preferred_element_type=F32)

        for cp in b_copies:
            cp.wait()

    out_shape = ([jax.ShapeDtypeStruct((N_LAYERS, ROWS_PER_LAYER, D_MODEL), F32),
                  jax.ShapeDtypeStruct((N_LAYERS, D_MODEL, MOD_SHARD), F32)]
                 + [jax.ShapeDtypeStruct(b.shape, b.dtype) for b in bigs])
    return pl.pallas_call(
        body,
        in_specs=[VMEM, VMEM, VMEM] + [ANY] * n,
        out_specs=[VMEM, VMEM] + [ANY] * n,
        out_shape=out_shape,
        scratch_shapes=[pltpu.VMEM((N_DEV, ROWS_PER_LAYER, D_MODEL), F32)] + _exchange_sems(n)
                       + [pltpu.SemaphoreType.DMA((N_PEERS,)), pltpu.SemaphoreType.DMA((N_PEERS,))],
        compiler_params=pltpu.CompilerParams(vmem_limit_bytes=VMEM_LIMIT_BYTES),
        name="reduce_grads",
    )(sc_all, small0, small_all1, *bigs)


def _adam_update(g, w, m, v):
    m2 = ADAM_B1 * m + (1.0 - ADAM_B1) * g
    v2 = ADAM_B2 * v + (1.0 - ADAM_B2) * (g * g)
    m_hat = m2 / (1.0 - ADAM_B1 ** ADAM_STEP)
    v_hat = v2 / (1.0 - ADAM_B2 ** ADAM_STEP)
    delta = -ADAM_LR * (m_hat / (jnp.sqrt(v_hat) + ADAM_EPS) + ADAM_WD * w)
    return delta, m2, v2


def _pair_add(g, r1, row_chunk, name):
    _, rows, cols = g.shape
    core = lax.axis_index("c").astype(jnp.int32).reshape(1)

    def body(core_ref, g_ref, r_ref, o_ref):
        o_ref[0] = (g_ref[0, 0].astype(F32) + r_ref[0].astype(F32)).astype(BF16)

    blk = pl.BlockSpec((1, row_chunk, cols), lambda q, i, core_ref: (q, i, 0))
    grid_spec = pltpu.PrefetchScalarGridSpec(
        num_scalar_prefetch=1, grid=(N_CHIPS, rows // row_chunk),
        in_specs=[pl.BlockSpec((1, 1, row_chunk, cols), lambda q, i, core_ref: (q, core_ref[0], i, 0)), blk],
        out_specs=blk)
    return pl.pallas_call(
        body, grid_spec=grid_spec, out_shape=jax.ShapeDtypeStruct((N_CHIPS, rows, cols), BF16),
        compiler_params=_cparams(2), name=name,
    )(core, g.reshape(N_CHIPS, 2, rows, cols), r1)


def _adam_sharded(recv0, recv1, w, m, v, row_chunk, name):
    _, rows, cols = w.shape
    n_chunks = rows // row_chunk

    def body(r0_ref, r1_ref, w_ref, m_ref, v_ref, g_ref, d_ref, m2_ref, v2_ref):
        layer = pl.program_id(0)

        def run(r_ref):
            g = r_ref[0].astype(F32)
            for j in range(1, r_ref.shape[0]):
                g = g + r_ref[j].astype(F32)
            delta, m2, v2 = _adam_update(g, w_ref[0], m_ref[0], v_ref[0])
            g_ref[0], d_ref[0], m2_ref[0], v2_ref[0] = g, delta, m2, v2

        @pl.when(layer == 0)
        def _():
            run(r0_ref)

        @pl.when(layer == 1)
        def _():
            run(r1_ref)

    r0_spec = pl.BlockSpec((recv0.shape[0], row_chunk, cols), lambda l, i: (0, i * (1 - l) + (n_chunks - 1) * l, 0))
    r1_spec = pl.BlockSpec((recv1.shape[0], row_chunk, cols), lambda l, i: (0, i * l, 0))
    blk = pl.BlockSpec((1, row_chunk, cols), lambda l, i: (l, i, 0))
    out = jax.ShapeDtypeStruct(w.shape, F32)
    return pl.pallas_call(
        body,
        grid=(N_LAYERS, n_chunks),
        in_specs=[r0_spec, r1_spec, blk, blk, blk],
        out_specs=[blk] * 4,
        out_shape=[out] * 4,
        compiler_params=_cparams(2),
        name=name,
    )(recv0, recv1, w, m, v)


def _adam_dense(g, w, m, v, row_chunk, name):
    n_lead, rows, cols = w.shape

    def body(g_ref, w_ref, m_ref, v_ref, go_ref, d_ref, m2_ref, v2_ref):
        gv = g_ref[...]
        go_ref[...] = gv
        d_ref[...], m2_ref[...], v2_ref[...] = _adam_update(gv, w_ref[...], m_ref[...], v_ref[...])

    blk = pl.BlockSpec((1, row_chunk, cols), lambda l, i: (l, i, 0))
    out = jax.ShapeDtypeStruct(w.shape, F32)
    return pl.pallas_call(
        body,
        grid=(n_lead, rows // row_chunk),
        in_specs=[blk] * 4,
        out_specs=[blk] * 4,
        out_shape=[out] * 4,
        compiler_params=_cparams(2),
        name=name,
    )(g, w, m, v)


WEIGHT_NAMES = ("mod_w", "mod_b", "mix_pre_g", "mix_post_g", "w_in", "sgu_norm_g", "sgu_norm_b", "sgu_w", "sgu_b",
                "conv_w", "conv_b", "conv_norm_g", "conv_norm_b", "pool_w", "pool_scale", "branch_g", "w_out",
                "ffn_pre_g", "ffn_post_g", "ffn_up", "ffn_conv_w", "ffn_conv_b", "ffn_down")
SHARDED_BIG = ("w_in", "w_out", "ffn_up", "ffn_down")
SMALL_PACKED = tuple(n for n in WEIGHT_NAMES if n not in SHARDED_BIG + ("mod_w",))


def _rows8(rows, width=D_MODEL):
    out = [jnp.pad(r.astype(F32), (0, width - r.shape[0]))[None] for r in rows]
    out.append(jnp.zeros((8 - len(rows), width), F32))
    return jnp.concatenate(out, axis=0)


def _as_rows(a, width=D_MODEL):
    flat = a.astype(F32).reshape(-1)
    pad = (-flat.shape[0]) % width
    return jnp.pad(flat, (0, pad)).reshape(-1, width)


def _pad_cols(a, width=D_MODEL):
    return jnp.pad(a.astype(F32), ((0, 0), (0, width - a.shape[1])))


def _pack_rows(arrays):
    rows = jnp.concatenate([_as_rows(a) for a in arrays], axis=0)
    return jnp.pad(rows, ((0, (-rows.shape[0]) % 8), (0, 0)))


def _unpack_rows(packed, shapes):
    out, r = [], 0
    for shape in shapes:
        size = math.prod(shape)
        n_rows = -(-size // D_MODEL)
        out.append(packed[r:r + n_rows].reshape(-1)[:size].reshape(shape))
        r += n_rows
    return out


def _layer_consts(l, w, mod_rows, win, wout, conv_w_full):
    modv = _rows8(list(mod_rows[l].reshape(6, D_MODEL)))
    g1024 = _rows8([w["mix_pre_g"][l], w["mix_post_g"][l], w["branch_g"][l], w["ffn_pre_g"][l], w["ffn_post_g"][l]])
    p384 = _rows8([w["sgu_norm_g"][l], w["sgu_norm_b"][l], w["conv_b"][l], w["conv_norm_g"][l], w["conv_norm_b"][l]],
                  SGU_WIDTH)
    cw = jnp.pad(conv_w_full[l], ((0, 32 - CONV_K), (0, 0)))
    bmat = jnp.repeat(w["sgu_b"][l].T, HEAD_DIM, axis=1)
    groups = len(POOL_WINDOWS)
    eye = jnp.eye(groups, dtype=F32)
    pwbd = (eye[:, None, :, None] * w["pool_w"][l][:, :, None, :]).reshape(POOL_WIDTH, POOL_WIDTH).astype(BF16)
    psc = _rows8([w["pool_scale"][l]], POOL_WIDTH)
    seg = jnp.arange(SGU_WIDTH) // HEAD_DIM
    segp = jnp.where(seg[:, None] == seg[None, :], 1.0 / HEAD_DIM, 0.0).astype(BF16)
    return modv, g1024, (modv, g1024, p384, cw, w["sgu_w"][l], bmat, pwbd, psc, segp, win, wout)


def _small_grad_rows(mix, ffn):
    _, _, _, _, _, mvec, v384, dcw, dsguw, dbmat, dpw, dpsc = mix
    fvec, cgrad = ffn[5], ffn[6]
    dmod = jnp.stack([mvec[2], mvec[3], mvec[1], fvec[2], fvec[3], fvec[1]]).reshape(N_DEV, MOD_SHARD)
    g_rows = jnp.stack([mvec[4], mvec[0], mvec[5], fvec[4], fvec[0]])
    dsgu_b = dbmat[:, ::HEAD_DIM].T.reshape(1, SGU_HEADS * CHUNK)
    groups = len(POOL_WINDOWS)
    gdim = POOL_WIDTH // groups
    dpw4 = dpw.reshape(groups, gdim, groups, gdim)
    dpool = jnp.stack([dpw4[g, :, g, :] for g in range(groups)])
    blocks = [_pad_cols(dmod), _rows8(list(g_rows)), _pad_cols(v384), _rows8([dsgu_b[0], dpsc[0]]), _pad_cols(dcw),
              _pad_cols(cgrad[:, 0:4, :].reshape(4 * N_DEV, FF_SHARD)), _as_rows(dpool), _as_rows(dsguw)]
    return jnp.concatenate(blocks, axis=0)


def _small_grads_from_rows(total):
    per = {n: [] for n in SMALL_PACKED}
    for l in range(N_LAYERS):
        s = total[l]
        per["mod_b"].append(s[ROW_DMOD:ROW_DMOD + N_DEV, :MOD_SHARD].reshape(6 * D_MODEL))
        for j, name in enumerate(("mix_pre_g", "mix_post_g", "branch_g", "ffn_pre_g", "ffn_post_g")):
            per[name].append(s[ROW_G1024 + j])
        for j, name in enumerate(("sgu_norm_g", "sgu_norm_b", "conv_b", "conv_norm_g", "conv_norm_b")):
            per[name].append(s[ROW_V384 + j, :SGU_WIDTH])
        per["sgu_b"].append(s[ROW_SGU_B, :SGU_HEADS * CHUNK].reshape(SGU_HEADS, CHUNK))
        per["pool_scale"].append(s[ROW_POOL_SCALE, :POOL_WIDTH])
        per["conv_w"].append(s[ROW_CONV_W:ROW_CONV_W + CONV_K, :CONV_WIDTH])
        fc = s[ROW_FFN_CONV:ROW_FFN_CONV + 4 * N_DEV, :FF_SHARD].reshape(N_DEV, 4, FF_SHARD)
        per["ffn_conv_w"].append(fc[:, 0:3, :].transpose(1, 0, 2).reshape(FFN_CONV_K, 2 * D_FF))
        per["ffn_conv_b"].append(fc[:, 3, :].reshape(2 * D_FF))
        per["pool_w"].append(s[ROW_POOL_W:ROW_POOL_W + 16].reshape(len(POOL_WINDOWS), HEAD_DIM, HEAD_DIM))
        per["sgu_w"].append(s[ROW_SGU_W:ROW_SGU_W + 96].reshape(SGU_HEADS, CHUNK, CHUNK))
    return {n: jnp.stack(v) for n, v in per.items()}


def kernel(x, c, mod_w, mod_b, mix_pre_g, mix_post_g, w_in, sgu_norm_g, sgu_norm_b, sgu_w, sgu_b, conv_w, conv_b, conv_norm_g, conv_norm_b, pool_w, pool_scale, branch_g, w_out, ffn_pre_g, ffn_post_g, ffn_up, ffn_conv_w, ffn_conv_b, ffn_down, loss_target, m_mod_w, m_mod_b, m_mix_pre_g, m_mix_post_g, m_w_in, m_sgu_norm_g, m_sgu_norm_b, m_sgu_w, m_sgu_b, m_conv_w, m_conv_b, m_conv_norm_g, m_conv_norm_b, m_pool_w, m_pool_scale, m_branch_g, m_w_out, m_ffn_pre_g, m_ffn_post_g, m_ffn_up, m_ffn_conv_w, m_ffn_conv_b, m_ffn_down, v_mod_w, v_mod_b, v_mix_pre_g, v_mix_post_g, v_w_in, v_sgu_norm_g, v_sgu_norm_b, v_sgu_w, v_sgu_b, v_conv_w, v_conv_b, v_conv_norm_g, v_conv_norm_b, v_pool_w, v_pool_scale, v_branch_g, v_w_out, v_ffn_pre_g, v_ffn_post_g, v_ffn_up, v_ffn_conv_w, v_ffn_conv_b, v_ffn_down):
    w = dict(zip(WEIGHT_NAMES, (mod_w, mod_b, mix_pre_g, mix_post_g, w_in, sgu_norm_g, sgu_norm_b, sgu_w, sgu_b, conv_w,
                                conv_b, conv_norm_g, conv_norm_b, pool_w, pool_scale, branch_g, w_out, ffn_pre_g,
                                ffn_post_g, ffn_up, ffn_conv_w, ffn_conv_b, ffn_down)))
    m = dict(zip(WEIGHT_NAMES, (m_mod_w, m_mod_b, m_mix_pre_g, m_mix_post_g, m_w_in, m_sgu_norm_g, m_sgu_norm_b, m_sgu_w,
                                m_sgu_b, m_conv_w, m_conv_b, m_conv_norm_g, m_conv_norm_b, m_pool_w, m_pool_scale,
                                m_branch_g, m_w_out, m_ffn_pre_g, m_ffn_post_g, m_ffn_up, m_ffn_conv_w, m_ffn_conv_b,
                                m_ffn_down)))
    v = dict(zip(WEIGHT_NAMES, (v_mod_w, v_mod_b, v_mix_pre_g, v_mix_post_g, v_w_in, v_sgu_norm_g, v_sgu_norm_b, v_sgu_w,
                                v_sgu_b, v_conv_w, v_conv_b, v_conv_norm_g, v_conv_norm_b, v_pool_w, v_pool_scale,
                                v_branch_g, v_w_out, v_ffn_pre_g, v_ffn_post_g, v_ffn_up, v_ffn_conv_w, v_ffn_conv_b,
                                v_ffn_down)))
    me = _flat(_my_pos())
    xs = x[0]
    s_len = xs.shape[0]

    transposed = ("w_in", "ffn_up")
    wt = {n: jnp.swapaxes(w[n], 1, 2) if n in transposed else w[n] for n in SHARDED_BIG}
    mt = {n: jnp.swapaxes(m[n], 1, 2) if n in transposed else m[n] for n in SHARDED_BIG}
    vt = {n: jnp.swapaxes(v[n], 1, 2) if n in transposed else v[n] for n in SHARDED_BIG}
    bf16_shards = [[wt[n][l].astype(BF16) for n in SHARDED_BIG] for l in range(N_LAYERS)]

    def mixer_operands(l, win_g, wout_g):
        win = win_g.reshape(IN_WIDTH, D_MODEL)
        return _layer_consts(l, w, mod_rows, win, wout_g.reshape(D_MODEL, D_MODEL), conv_w_full)

    def ffn_operands(l, modv, g1024, wup_g, wdn_g):
        wdn = wdn_g.reshape(FF_PAIRS, FF_SHARD, D_MODEL)
        return modv, g1024, wup_g, wdn, ffn_cw_full[:, l], ffn_conv_b[l].reshape(N_DEV, 1, FF_SHARD)

    w0, w1 = bf16_shards
    c8 = jnp.broadcast_to(c, (8, D_MODEL))
    mod_b8 = jnp.pad(mod_b, ((0, 8 - N_LAYERS), (0, 0)))
    sc_all, mod_rows, win0_g, wout0_g, conv_w_g, ffn_cw_full = _gather_weights(
        c8, mod_w, mod_b8, [("gather2", w0[0]), ("gather2", w0[1]), ("gather", conv_w), ("gather", ffn_conv_w)])
    conv_w_full = conv_w_g.transpose(1, 2, 0, 3).reshape(N_LAYERS, CONV_K, CONV_WIDTH)

    modv0, g0, mix_consts0 = mixer_operands(0, win0_g, wout0_g)
    (x1, z, o), (wup0_g, wdn0_g) = _mixer_fwd(xs, *mix_consts0, name="mixer_fwd_l0",
                                              job=[("gather2", w0[2]), ("gather2", w0[3])])
    ffn_consts0 = ffn_operands(0, modv0, g0, wup0_g, wdn0_g)
    (x2, y2, p), (win1_g, wout1_g, wdn1_g) = _ffn_fwd(x1, *ffn_consts0, name="ffn_fwd_l0",
                                                      job=[("gather2", w1[0]), ("gather2", w1[1]), ("gather2", w1[3])])
    saved = [(xs, z, o, x1, y2, p)]
    modv1, g1, mix_consts1 = mixer_operands(1, win1_g, wout1_g)
    (x1, z, o), (wup1_g,) = _mixer_fwd(x2, *mix_consts1, name="mixer_fwd_l1", job=[("gather2", w1[2])])
    ffn_consts1 = ffn_operands(1, modv1, g1, wup1_g, wdn1_g)
    (x3, y2, p), _ = _ffn_fwd(x1, *ffn_consts1, name="ffn_fwd_l1")
    saved.append((x2, z, o, x1, y2, p))
    dh, loss_tile = _loss_grad(x3, loss_target[0])
    loss = lax.psum(loss_tile[0, 0], ("x", "y", "c"))

    def ffn_weight_grads(l, ffn):
        dp, a, dy2, h2 = ffn[1:5]
        d_up = _wgrad(dp, h2, f"wgrad_ffn_up_l{l}", tk=WGRAD_TK_FFN)
        d_dn = _wgrad(a, dy2, f"wgrad_ffn_down_l{l}", tk=WGRAD_TK_FFN).reshape(N_DEV, D_FF // N_DEV, D_MODEL)
        return d_up, d_dn

    def mixer_weight_grads(l, mix):
        dz, do, ycat, h1 = mix[1:5]
        d_in = _wgrad(dz[None], h1, f"wgrad_w_in_l{l}").reshape(N_DEV, IN_WIDTH // N_DEV, D_MODEL)
        d_out = _wgrad(ycat, do[None], f"wgrad_w_out_l{l}").reshape(N_DEV, D_MODEL // N_DEV, D_MODEL)
        return d_in, d_out

    x_in, z, o, x1, y2, p = saved[1]
    ffn1, _ = _ffn_bwd(dh, x1, y2, p, *ffn_consts1, name="ffn_bwd_l1")
    d_up1, d_dn1 = ffn_weight_grads(1, ffn1)
    mix1, (sib_up1, sib_dn1) = _mixer_bwd(ffn1[0], x_in, o, z, *mix_consts1, name="mixer_bwd_l1",
                                          job=[("scatter_p1", d_up1), ("scatter_p1", d_dn1)])
    chip_up1 = _pair_add(d_up1, sib_up1, 176, "pair_add_ffn_up_l1")
    chip_dn1 = _pair_add(d_dn1, sib_dn1, 176, "pair_add_ffn_down_l1")
    d_in1, d_out1 = mixer_weight_grads(1, mix1)
    small1 = _small_grad_rows(mix1, ffn1)

    x_in, z, o, x1, y2, p = saved[0]
    ffn0, job_out = _ffn_bwd(mix1[0], x1, y2, p, *ffn_consts0, name="ffn_bwd_l0",
                             job=[("scatter", d_in1), ("scatter", d_out1), ("scatter_p2", chip_up1),
                                  ("scatter_p2", chip_dn1), ("gather", small1)])
    recv1, small_all1 = job_out[0:4], job_out[4]
    d_up0, d_dn0 = ffn_weight_grads(0, ffn0)
    mix0, recv_ffn0 = _mixer_bwd(ffn0[0], x_in, o, z, *mix_consts0, name="mixer_bwd_l0",
                                 job=[("scatter", d_up0), ("scatter", d_dn0)])
    d_in0, d_out0 = mixer_weight_grads(0, mix0)
    grad_x = mix0[0][None]

    reduced = _reduce_grads(sc_all, _small_grad_rows(mix0, ffn0), small_all1, [d_in0, d_out0])
    small_total, g_mod_w = reduced[0], reduced[1]
    recv0 = [reduced[2], reduced[3], recv_ffn0[0], recv_ffn0[1]]

    grads, deltas, new_m, new_v = {}, {}, {}, {}
    for j, (name, chunk) in enumerate((("w_in", 224), ("w_out", 128), ("ffn_up", 176), ("ffn_down", 176))):
        outs = _adam_sharded(recv0[j], recv1[j], wt[name], mt[name], vt[name], chunk, "adam_" + name)
        if name in transposed:
            outs = [jnp.swapaxes(t, 1, 2) for t in outs]
        grads[name], deltas[name], new_m[name], new_v[name] = outs
    grads["mod_w"], deltas["mod_w"], new_m["mod_w"], new_v["mod_w"] = _adam_dense(
        g_mod_w, mod_w, m_mod_w, v_mod_w, 256, "adam_mod_w")

    small_g = _small_grads_from_rows(small_total)
    small_g["conv_w"] = lax.dynamic_slice_in_dim(small_g["conv_w"], me * conv_w.shape[2], conv_w.shape[2], axis=2)
    small_g["ffn_conv_w"] = lax.dynamic_slice_in_dim(small_g["ffn_conv_w"], me * FF_SHARD, FF_SHARD, axis=2)
    shapes = [w[n].shape for n in SMALL_PACKED]
    packs = [_pack_rows([src[n] for n in SMALL_PACKED])[None] for src in (small_g, w, m, v)]
    _, d, m2, v2 = _adam_dense(*packs, packs[0].shape[1], "adam_small")
    for name, dd, mm, vv in zip(SMALL_PACKED, _unpack_rows(d[0], shapes), _unpack_rows(m2[0], shapes),
                                _unpack_rows(v2[0], shapes)):
        grads[name], deltas[name], new_m[name], new_v[name] = small_g[name], dd, mm, vv

    return (loss, grad_x, *[grads[n] for n in WEIGHT_NAMES], *[deltas[n] for n in WEIGHT_NAMES],
            *[new_m[n] for n in WEIGHT_NAMES], *[new_v[n] for n in WEIGHT_NAMES])
```

```python
import functools
import math

import jax
import jax.numpy as jnp
from jax import lax
from jax.experimental import pallas as pl
from jax.experimental.pallas import tpu as pltpu

F32 = jnp.float32
BF16 = jnp.bfloat16

D_MODEL = 1024
N_DEV = 8
SGU_WIDTH = 384
CONV_WIDTH = 384
POOL_WIDTH = 256
HEAD_DIM = 64
SGU_HEADS = 6
CHUNK = 128
CONV_K = 31
POOL_WINDOWS = (2, 4, 8, 16)
IN_WIDTH = 1792
D_FF = 2816
FF_SHARD = 2 * D_FF // N_DEV
FF_PAIRS = N_DEV // 2
FFN_CONV_K = 3
EPS = 1e-6
GELU_C0 = math.sqrt(2.0 / math.pi)
GELU_C1 = 0.044715

ADAM_LR = 0.001
ADAM_B1 = 0.9
ADAM_B2 = 0.999
ADAM_EPS = 1e-08
ADAM_WD = 0.01
ADAM_STEP = 10

VMEM_LIMIT_BYTES = 56 * 1024 * 1024
TILE = 256
MIX_TILE = TILE
FFN_TILE = TILE
FFN_HALO = 8 * (FFN_CONV_K - 1)
POOL_HALO = 8 * POOL_WINDOWS[-1]
WGRAD_TK = 2048
WGRAD_TK_FFN = 4096


def _cparams(n_axes):
    return pltpu.CompilerParams(dimension_semantics=("arbitrary",) * n_axes, vmem_limit_bytes=VMEM_LIMIT_BYTES)


def _whole(shape):
    nd = len(shape)
    return pl.BlockSpec(shape, lambda *_: (0,) * nd, pipeline_mode=pl.Buffered(1))


def _dot(a, b):
    return jnp.dot(a, b, preferred_element_type=F32)


def _dot_nt(a, b):
    return lax.dot_general(a, b, (((1,), (1,)), ((), ())), preferred_element_type=F32)


def _dot_tn(a, b):
    return lax.dot_general(a, b, (((0,), (0,)), ((), ())), preferred_element_type=F32)


def _gelu(x):
    t = jnp.tanh(GELU_C0 * (x + GELU_C1 * x * x * x))
    return 0.5 * x * (1.0 + t), t


def _gelu_grad(x, t):
    return 0.5 * (1.0 + t) + 0.5 * x * (1.0 - t * t) * (GELU_C0 * (1.0 + 3.0 * GELU_C1 * x * x))


def _rowmean(x):
    return jnp.mean(x, axis=-1, keepdims=True)


def _colsum(x):
    return jnp.sum(x, axis=0, keepdims=True)


def _rms_fwd(x):
    r = lax.rsqrt(_rowmean(x * x) + EPS)
    return x * r, r


def _rms_bwd(dxhat, xhat, r):
    return r * (dxhat - xhat * _rowmean(dxhat * xhat))


N_PEERS = N_DEV - 1
ANY = pl.BlockSpec(memory_space=pl.ANY)
VMEM = pl.BlockSpec(memory_space=pltpu.VMEM)


def _my_pos():
    return lax.axis_index("x"), lax.axis_index("y"), lax.axis_index("c")


def _peer(pos, k):
    x, y, c = pos
    return (1 - x if k & 4 else x, 1 - y if k & 2 else y, 1 - c if k & 1 else c)


def _flat(pos):
    return 4 * pos[0] + 2 * pos[1] + pos[2]


def _remote_copy(src, dst, send_sem, recv_sem, peer):
    return pltpu.make_async_remote_copy(src_ref=src, dst_ref=dst, send_sem=send_sem, recv_sem=recv_sem,
                                        device_id=peer, device_id_type=pl.DeviceIdType.MESH)


N_CHIPS = N_DEV // 2
SIBLING = 1
SAME_CORE_PEERS = (2, 4, 6)


def _exchange_out_shapes(job):
    def shape(kind, a):
        if kind in ("gather", "gather2"):
            return (N_DEV,) + a.shape
        if kind == "scatter_p1":
            return (N_CHIPS,) + a.shape[1:]
        return a.shape
    return [jax.ShapeDtypeStruct(shape(kind, a), a.dtype) for kind, a in job]


def _exchange_sems(n):
    return [pltpu.SemaphoreType.DMA((n, N_PEERS)), pltpu.SemaphoreType.DMA((n, N_PEERS)), pltpu.SemaphoreType.DMA((n,))]


def _exchange_copies(kinds, src_refs, dst_refs, send_sems, recv_sems, local_sems, phase):
    pos = _my_pos()
    me = _flat(pos)
    chip, core = 2 * pos[0] + pos[1], pos[2]
    copies = []

    def remote(a, src, dst, k, sem=None):
        sem = k - 1 if sem is None else sem
        copies.append(_remote_copy(src, dst, send_sems.at[a, sem], recv_sems.at[a, sem], _peer(pos, k)))

    for a, kind in enumerate(kinds):
        src, dst = src_refs[a], dst_refs[a]
        if phase == 1:
            if kind == "gather2":
                for k in SAME_CORE_PEERS:
                    remote(a, dst.at[me ^ k], dst.at[me ^ k], SIBLING, sem=k)
        elif kind in ("gather", "gather2"):
            copies.append(pltpu.make_async_copy(src, dst.at[me], local_sems.at[a]))
            for k in (range(1, N_DEV) if kind == "gather" else (SIBLING,) + SAME_CORE_PEERS):
                remote(a, src, dst.at[me], k)
        elif kind == "scatter":
            copies.append(pltpu.make_async_copy(src.at[me], dst.at[me], local_sems.at[a]))
            for k in range(1, N_DEV):
                remote(a, src.at[me ^ k], dst.at[me], k)
        elif kind == "scatter_p1":
            for q in range(N_CHIPS):
                remote(a, src.at[2 * q + 1 - core], dst.at[q], SIBLING, sem=q)
        elif kind == "scatter_p2":
            copies.append(pltpu.make_async_copy(src.at[chip], dst.at[chip], local_sems.at[a]))
            for k in SAME_CORE_PEERS:
                remote(a, src.at[chip ^ (k >> 1)], dst.at[chip], k)
    return copies


def _pallas_call_with_exchange(body, *, grid, in_specs, out_specs, out_shape, scratch_shapes, operands, name, job):
    params = _cparams(len(grid))
    if not job:
        outs = pl.pallas_call(body, grid=grid, in_specs=in_specs, out_specs=out_specs, out_shape=out_shape,
                              scratch_shapes=scratch_shapes, compiler_params=params, name=name)(*operands)
        return outs, []
    kinds = [kind for kind, _ in job]
    relayed = [kind if kind == "gather2" else None for kind in kinds]
    unrelayed = [None if kind == "gather2" else kind for kind in kinds]
    n, n_in, n_out, n_scr = len(job), len(in_specs), len(out_specs), len(scratch_shapes)
    (n_steps,) = grid
    relay_step = max(n_steps - 2, 0)

    def wrapped(*refs):
        ins, jin = refs[:n_in], refs[n_in:n_in + n]
        outs, jout = refs[n_in + n:n_in + n + n_out], refs[n_in + n + n_out:n_in + 2 * n + n_out]
        scr = refs[n_in + 2 * n + n_out:n_in + 2 * n + n_out + n_scr]
        sems = refs[n_in + 2 * n + n_out + n_scr:]
        step = pl.program_id(0)

        def copies(which, phase):
            return _exchange_copies(which, jin, jout, *sems, phase=phase)

        @pl.when(step == 0)
        def _():
            for cp in copies(kinds, 0):
                cp.start()

        body(*ins, *outs, *scr)

        @pl.when(step == relay_step)
        def _():
            for cp in copies(relayed, 0):
                cp.wait()
            for cp in copies(relayed, 1):
                cp.start()

        @pl.when(step == n_steps - 1)
        def _():
            for cp in copies(unrelayed, 0) + copies(relayed, 1):
                cp.wait()

    res = pl.pallas_call(
        wrapped, grid=grid,
        in_specs=list(in_specs) + [ANY] * n,
        out_specs=list(out_specs) + [ANY] * n,
        out_shape=list(out_shape) + _exchange_out_shapes(job),
        scratch_shapes=list(scratch_shapes) + _exchange_sems(n),
        compiler_params=params, name=name,
    )(*operands, *[a for _, a in job])
    return res[:n_out], res[n_out:]


def _seg_mean(x, segp):
    hi = x.astype(BF16)
    lo = (x - hi.astype(F32)).astype(BF16)
    return _dot(hi, segp) + _dot(lo, segp)


def _rot_rows(x, shift):
    m, c = x.shape
    return pltpu.roll(x.reshape(m // 8, 8, c), shift, 1).reshape(m, c)


def _sublane_is(shape, s):
    return lax.broadcasted_iota(jnp.int32, shape, 0) % 8 == s


def _causal_tail(tail, prev_rot):
    rot = _rot_rows(tail, 1)
    return jnp.where(_sublane_is(tail.shape, 0), prev_rot, rot), rot


def _lookahead_head(head, next_rot):
    rot = _rot_rows(head, 7)
    return jnp.where(_sublane_is(head.shape, 7), next_rot, rot), rot


def _tile_token_index(t, tile_idx):
    r = lax.broadcasted_iota(jnp.int32, (t, 1), 0)
    return tile_idx * t + (r % 8) * (t // 8) + r // 8


def _interleave(x):
    t, c = x.shape
    return jnp.swapaxes(x.reshape(8, t // 8, c), 0, 1).reshape(t, c)


def _deinterleave(x):
    t, c = x.shape
    return jnp.swapaxes(x.reshape(t // 8, 8, c), 0, 1).reshape(t, c)


def _ffn_fwd(x1, modv, g1024, wup, wdn, cw, cb, name="ffn_fwd", job=None):
    s_len = x1.shape[0]
    t = FFN_TILE
    n_tiles = s_len // t

    def body(x1_ref, mod_ref, g_ref, wup_ref, wdn_ref, cw_ref, cb_ref, x2_ref, y2_ref, p_ref, ext_ref, carry_ref):
        i = pl.program_id(0)

        @pl.when(i == 0)
        def _():
            carry_ref[...] = jnp.zeros_like(carry_ref)

        x1v = x1_ref[...]
        pre_g, post_g = g_ref[3:4, :], g_ref[4:5, :]
        sh2, sc2, g2 = mod_ref[3:4, :], mod_ref[4:5, :], mod_ref[5:6, :]
        xhat, _ = _rms_fwd(x1v)
        h2b = (xhat * pre_g * (1.0 + sc2) + sh2).astype(BF16)

        def conv_shard(s):
            p = _dot_nt(h2b, wup_ref[s])
            p_ref[s] = p.astype(BF16)
            ext_ref[0:FFN_HALO, :], carry_ref[s] = _causal_tail(p[t - FFN_HALO:t, :], carry_ref[s])
            ext_ref[FFN_HALO:FFN_HALO + t, :] = p
            w = cw_ref[s]
            return w[0:1, :] * ext_ref[0:t, :] + w[1:2, :] * ext_ref[8:8 + t, :] + w[2:3, :] * p + cb_ref[s]

        y2 = jnp.zeros((t, D_MODEL), F32)
        for j in range(FF_PAIRS):
            ug = conv_shard(j)
            uv = conv_shard(j + FF_PAIRS)
            ge, _ = _gelu(ug)
            y2 = y2 + _dot((ge * uv).astype(BF16), wdn_ref[j])
        y2_ref[...] = y2
        yhat, _ = _rms_fwd(y2)
        x2_ref[...] = x1v + g2 * (yhat * post_g)

    tile = pl.BlockSpec((t, D_MODEL), lambda i: (i, 0))
    return _pallas_call_with_exchange(
        body,
        grid=(n_tiles,),
        in_specs=[tile, _whole(modv.shape), _whole(g1024.shape), _whole(wup.shape), _whole(wdn.shape),
                  _whole(cw.shape), _whole(cb.shape)],
        out_specs=[tile, tile, pl.BlockSpec((N_DEV, t, FF_SHARD), lambda i: (0, i, 0))],
        out_shape=[jax.ShapeDtypeStruct((s_len, D_MODEL), F32), jax.ShapeDtypeStruct((s_len, D_MODEL), F32),
                   jax.ShapeDtypeStruct((N_DEV, s_len, FF_SHARD), BF16)],
        scratch_shapes=[pltpu.VMEM((FFN_HALO + t, FF_SHARD), F32), pltpu.VMEM((N_DEV, FFN_HALO, FF_SHARD), F32)],
        operands=(x1, modv, g1024, wup, wdn, cw, cb),
        name=name, job=job)


def _ffn_bwd(dx2, x1, y2, p, modv, g1024, wup, wdn, cw, cb, name="ffn_bwd", job=None):
    s_len = x1.shape[0]
    t = FFN_TILE
    n_tiles = s_len // t
    hb = FFN_HALO

    def body(dx2_ref, x1_ref, y2_ref, p_ref, ph_ref, mod_ref, g_ref, wup_ref, wdn_ref, cw_ref, cb_ref,
             dx1_ref, dp_ref, a_ref, dy2_ref, h2_ref, vec_ref, cgrad_ref, ext_ref, dext_ref, dcarry_ref):
        i = pl.program_id(0)
        tile_idx = n_tiles - 1 - i

        @pl.when(i == 0)
        def _():
            vec_ref[...] = jnp.zeros_like(vec_ref)
            cgrad_ref[...] = jnp.zeros_like(cgrad_ref)
            dcarry_ref[...] = jnp.zeros_like(dcarry_ref)

        dx2v, x1v, y2v = dx2_ref[...], x1_ref[...], y2_ref[...]
        pre_g, post_g = g_ref[3:4, :], g_ref[4:5, :]
        sh2, sc2, g2 = mod_ref[3:4, :], mod_ref[4:5, :], mod_ref[5:6, :]

        yhat, ry = _rms_fwd(y2v)
        vec_ref[1:2, :] += _colsum(dx2v * (yhat * post_g))
        dyn = dx2v * g2
        vec_ref[0:1, :] += _colsum(dyn * yhat)
        dy2b = _rms_bwd(dyn * post_g, yhat, ry).astype(BF16)
        dy2_ref[...] = dy2b

        xhat, rx = _rms_fwd(x1v)
        xn = xhat * pre_g
        h2_ref[...] = (xn * (1.0 + sc2) + sh2).astype(BF16)

        not_first = (tile_idx > 0).astype(F32)

        def recompute(s, slot):
            pf = p_ref[s].astype(F32)
            prev_rot = _rot_rows(ph_ref[s].astype(F32), 1) * not_first
            ext_ref[slot, 0:hb, :], _ = _causal_tail(pf[t - hb:t, :], prev_rot)
            ext_ref[slot, hb:hb + t, :] = pf
            w = cw_ref[s]
            u = (w[0:1, :] * ext_ref[slot, 0:t, :] + w[1:2, :] * ext_ref[slot, 8:8 + t, :]
                 + w[2:3, :] * pf + cb_ref[s])
            return u

        def conv_bwd(s, slot, du):
            w = cw_ref[s]
            cgrad_ref[s, 0:1, :] += _colsum(du * ext_ref[slot, 0:t, :])
            cgrad_ref[s, 1:2, :] += _colsum(du * ext_ref[slot, 8:8 + t, :])
            cgrad_ref[s, 2:3, :] += _colsum(du * ext_ref[slot, 16:16 + t, :])
            cgrad_ref[s, 3:4, :] += _colsum(du)
            dext_ref[0:t, :] = du
            dext_ref[t:t + hb, :], dcarry_ref[s] = _lookahead_head(du[0:hb, :], dcarry_ref[s])
            dp = w[2:3, :] * du + w[1:2, :] * dext_ref[8:8 + t, :] + w[0:1, :] * dext_ref[16:16 + t, :]
            dpb = dp.astype(BF16)
            dp_ref[s] = dpb
            return _dot(dpb, wup_ref[s])

        dh2 = jnp.zeros((t, D_MODEL), F32)
        for j in range(FF_PAIRS):
            ug = recompute(j, 0)
            uv = recompute(j + FF_PAIRS, 1)
            ge, th = _gelu(ug)
            a_ref[j] = (ge * uv).astype(BF16)
            da = _dot_nt(dy2b, wdn_ref[j])
            dh2 = dh2 + conv_bwd(j, 0, da * uv * _gelu_grad(ug, th))
            dh2 = dh2 + conv_bwd(j + FF_PAIRS, 1, da * ge)

        vec_ref[2:3, :] += _colsum(dh2)
        vec_ref[3:4, :] += _colsum(dh2 * xn)
        dxn = dh2 * (1.0 + sc2)
        vec_ref[4:5, :] += _colsum(dxn * xhat)
        dx1_ref[...] = dx2v + _rms_bwd(dxn * pre_g, xhat, rx)

    rev = lambda i: (n_tiles - 1 - i, 0)
    tile = pl.BlockSpec((t, D_MODEL), rev)
    halo_idx = lambda i: (0, jnp.maximum((n_tiles - 1 - i) * (t // hb) - 1, 0), 0)
    return _pallas_call_with_exchange(
        body,
        grid=(n_tiles,),
        in_specs=[tile, tile, tile,
                  pl.BlockSpec((N_DEV, t, FF_SHARD), lambda i: (0, n_tiles - 1 - i, 0)),
                  pl.BlockSpec((N_DEV, hb, FF_SHARD), halo_idx),
                  _whole(modv.shape), _whole(g1024.shape), _whole(wup.shape), _whole(wdn.shape),
                  _whole(cw.shape), _whole(cb.shape)],
        out_specs=[tile,
                   pl.BlockSpec((N_DEV, t, FF_SHARD), lambda i: (0, n_tiles - 1 - i, 0)),
                   pl.BlockSpec((FF_PAIRS, t, FF_SHARD), lambda i: (0, n_tiles - 1 - i, 0)),
                   tile, tile,
                   pl.BlockSpec((8, D_MODEL), lambda i: (0, 0)),
                   pl.BlockSpec((N_DEV, 8, FF_SHARD), lambda i: (0, 0, 0))],
        out_shape=[jax.ShapeDtypeStruct((s_len, D_MODEL), F32),
                   jax.ShapeDtypeStruct((N_DEV, s_len, FF_SHARD), BF16),
                   jax.ShapeDtypeStruct((FF_PAIRS, s_len, FF_SHARD), BF16),
                   jax.ShapeDtypeStruct((s_len, D_MODEL), BF16),
                   jax.ShapeDtypeStruct((s_len, D_MODEL), BF16),
                   jax.ShapeDtypeStruct((8, D_MODEL), F32),
                   jax.ShapeDtypeStruct((N_DEV, 8, FF_SHARD), F32)],
        scratch_shapes=[pltpu.VMEM((2, hb + t, FF_SHARD), F32), pltpu.VMEM((t + hb, FF_SHARD), F32),
                        pltpu.VMEM((N_DEV, hb, FF_SHARD), F32)],
        operands=(dx2, x1, y2, p, p, modv, g1024, wup, wdn, cw, cb),
        name=name, job=job)


def _wgrad(a, b, name, tk=WGRAD_TK):
    a_grouped, b_grouped = a.ndim == 3, b.ndim == 3
    groups = a.shape[0] if a_grouped else b.shape[0]
    s_len, m, n = a.shape[-2], a.shape[-1], b.shape[-1]
    tk = min(tk, s_len)
    n_k = s_len // tk

    def body(a_ref, b_ref, o_ref, acc_ref):
        k = pl.program_id(1)
        av = a_ref[0] if a_grouped else a_ref[...]
        bv = b_ref[0] if b_grouped else b_ref[...]
        part = _dot_tn(av, bv)
        if n_k == 1:
            o_ref[0] = part.astype(BF16)
            return

        @pl.when(k == 0)
        def _():
            acc_ref[...] = part

        @pl.when(jnp.logical_and(k > 0, k < n_k - 1))
        def _():
            acc_ref[...] += part

        @pl.when(k == n_k - 1)
        def _():
            o_ref[0] = (acc_ref[...] + part).astype(BF16)

    a_spec = pl.BlockSpec((1, tk, m), lambda g, k: (g, k, 0)) if a_grouped else pl.BlockSpec((tk, m), lambda g, k: (k, 0))
    b_spec = pl.BlockSpec((1, tk, n), lambda g, k: (g, k, 0)) if b_grouped else pl.BlockSpec((tk, n), lambda g, k: (k, 0))
    return pl.pallas_call(
        body,
        grid=(groups, n_k),
        in_specs=[a_spec, b_spec],
        out_specs=pl.BlockSpec((1, m, n), lambda g, k: (g, 0, 0)),
        out_shape=jax.ShapeDtypeStruct((groups, m, n), BF16),
        scratch_shapes=[pltpu.VMEM((m, n), F32)],
        compiler_params=_cparams(2),
        name=name,
    )(a, b)


def _lane(shape):
    return lax.broadcasted_iota(jnp.int32, shape, 1)


def _by_pool_group(shape, vals):
    lane = _lane(shape)
    return jnp.where(lane < 64, vals[0], jnp.where(lane < 128, vals[1], jnp.where(lane < 192, vals[2], vals[3])))


def _pool_inv_counts(t, tile_idx):
    pos1 = _tile_token_index(t, tile_idx) + 1
    return [1.0 / jnp.minimum(pos1, w).astype(F32) for w in POOL_WINDOWS]


def _sgu_keep_mask(t):
    tok_r = _tile_token_index(t, 0)
    c = lax.broadcasted_iota(jnp.int32, (1, t), 1)
    tok_c = (c % 8) * (t // 8) + c // 8
    return jnp.logical_and(tok_r // CHUNK == tok_c // CHUNK, tok_r >= tok_c)


def _masked_sgu_w(sguw_ref):
    keep = _sgu_keep_mask(sguw_ref.shape[1])
    return [jnp.where(keep, sguw_ref[h], 0.0).astype(BF16) for h in range(SGU_HEADS)]


def _branches_fwd(z, tile_idx, p384_ref, cw_ref, wm, bmat_ref, pwbd_ref, psc_ref, segp_ref, g_ref, hext_ref, zext_ref,
                  h_prev_rot, z_prev_rot):
    t = z.shape[0]
    segp = segp_ref[...]
    r = {}
    u, _ = _gelu(z[:, 0:SGU_WIDTH])
    vraw, _ = _gelu(z[:, SGU_WIDTH:2 * SGU_WIDTH])
    xc = vraw - _seg_mean(vraw, segp)
    rstd_v = lax.rsqrt(_seg_mean(xc * xc, segp) + EPS)
    xh_v = xc * rstd_v
    vnb = (xh_v * p384_ref[0:1, :] + p384_ref[1:2, :]).astype(BF16)
    first_head = _lane((t, 128)) < HEAD_DIM
    f_pairs = []
    for pr in range(SGU_HEADS // 2):
        vp = vnb[:, pr * 128:(pr + 1) * 128]
        f_pairs.append(jnp.where(first_head, _dot(wm[2 * pr], vp), _dot(wm[2 * pr + 1], vp)))
    f = jnp.concatenate(f_pairs, axis=1) + bmat_ref[...]
    ya = u * f
    r.update(u=u, xh_v=xh_v, rstd_v=rstd_v, vnb=vnb, f=f)
    o_b = 2 * SGU_WIDTH
    a_in = z[:, o_b:o_b + CONV_WIDTH]
    sig_g = jax.nn.sigmoid(z[:, o_b + CONV_WIDTH:o_b + 2 * CONV_WIDTH])
    hh = a_in * sig_g
    hext_ref[0:t, :], r["h_rot"] = _causal_tail(hh, h_prev_rot)
    hext_ref[t:2 * t, :] = hh
    conv = jnp.zeros((t, CONV_WIDTH), F32) + p384_ref[2:3, :]
    for k in range(CONV_K):
        conv = conv + cw_ref[k:k + 1, :] * hext_ref[pl.ds(t - 8 * (CONV_K - 1 - k), t), :]
    cc = conv - _rowmean(conv)
    rstd_c = lax.rsqrt(_rowmean(cc * cc) + EPS)
    xh_c = cc * rstd_c
    cn = xh_c * p384_ref[3:4, :] + p384_ref[4:5, :]
    sig_c = jax.nn.sigmoid(cn)
    yb = cn * sig_c
    r.update(a_in=a_in, sig_g=sig_g, xh_c=xh_c, rstd_c=rstd_c, cn=cn, sig_c=sig_c)
    o_c = o_b + 2 * CONV_WIDTH
    zc = z[:, o_c:o_c + POOL_WIDTH]
    zext_ref[0:POOL_HALO, :], r["z_rot"] = _causal_tail(zc[t - POOL_HALO:t, :], z_prev_rot)
    zext_ref[POOL_HALO:POOL_HALO + t, :] = zc
    sums, acc = [], zc
    for j in range(1, POOL_WINDOWS[-1]):
        acc = acc + zext_ref[pl.ds(POOL_HALO - 8 * j, t), :]
        if j + 1 in POOL_WINDOWS:
            sums.append(acc)
    inv = _pool_inv_counts(t, tile_idx)
    dpool = _by_pool_group((t, POOL_WIDTH), [s * iv for s, iv in zip(sums, inv)]) - zc
    ycp = _dot(dpool.astype(BF16), pwbd_ref[...])
    yc = ycp * psc_ref[0:1, :]
    r.update(dpool=dpool, ycp=ycp)
    yha, ra = _rms_fwd(ya)
    yhb, rb = _rms_fwd(yb)
    yhc, rc = _rms_fwd(yc)
    bg = g_ref[2:3, :]
    ycat = jnp.concatenate([yha * bg[:, 0:384], yhb * bg[:, 384:768], yhc * bg[:, 768:1024]], axis=1)
    r.update(yha=yha, ra=ra, yhb=yhb, rb=rb, yhc=yhc, rc=rc, ycat=ycat)
    return r


def _mixer_fwd(x, modv, g1024, p384, cw, sguw, bmat, pwbd, psc, segp, win, wout, name="mixer_fwd", job=None,
               natural_x=False):
    s_len = x.shape[0]
    t = MIX_TILE

    def body(x_ref, mod_ref, g_ref, p384_ref, cw_ref, sguw_ref, bmat_ref, pwbd_ref, psc_ref, segp_ref, win_ref, wout_ref,
             x1_ref, z_ref, o_ref, hext_ref, zext_ref, hrot_ref, zrot_ref):
        i = pl.program_id(0)

        @pl.when(i == 0)
        def _():
            hrot_ref[...] = jnp.zeros_like(hrot_ref)
            zrot_ref[...] = jnp.zeros_like(zrot_ref)

        xv = _interleave(x_ref[...]) if natural_x else x_ref[...]
        sh1, sc1, g1 = mod_ref[0:1, :], mod_ref[1:2, :], mod_ref[2:3, :]
        xhat, _ = _rms_fwd(xv)
        h1 = xhat * g_ref[0:1, :] * (1.0 + sc1) + sh1
        z = _dot_nt(h1.astype(BF16), win_ref[...])
        z_ref[...] = z
        r = _branches_fwd(z, i, p384_ref, cw_ref, _masked_sgu_w(sguw_ref), bmat_ref, pwbd_ref, psc_ref, segp_ref,
                          g_ref, hext_ref, zext_ref, hrot_ref[...], zrot_ref[...])
        hrot_ref[...] = r["h_rot"]
        zrot_ref[...] = r["z_rot"]
        o = _dot(r["ycat"].astype(BF16), wout_ref[...])
        o_ref[...] = o
        ohat, _ = _rms_fwd(o)
        x1_ref[...] = xv + g1 * (ohat * g_ref[1:2, :])

    tile = pl.BlockSpec((t, D_MODEL), lambda i: (i, 0))
    consts = (modv, g1024, p384, cw, sguw, bmat, pwbd, psc, segp, win, wout)
    return _pallas_call_with_exchange(
        body,
        grid=(s_len // t,),
        in_specs=[tile] + [_whole(c.shape) for c in consts],
        out_specs=[tile, pl.BlockSpec((t, IN_WIDTH), lambda i: (i, 0)), tile],
        out_shape=[jax.ShapeDtypeStruct((s_len, D_MODEL), F32), jax.ShapeDtypeStruct((s_len, IN_WIDTH), F32),
                   jax.ShapeDtypeStruct((s_len, D_MODEL), F32)],
        scratch_shapes=[pltpu.VMEM((2 * t, CONV_WIDTH), F32), pltpu.VMEM((POOL_HALO + t, POOL_WIDTH), F32),
                        pltpu.VMEM((t, CONV_WIDTH), F32), pltpu.VMEM((POOL_HALO, POOL_WIDTH), F32)],
        operands=(x, *consts),
        name=name, job=job)


def _mixer_bwd(dx1, x, o, z, modv, g1024, p384, cw, sguw, bmat, pwbd, psc, segp, win, wout, name="mixer_bwd", job=None,
               natural_x=False):
    s_len = x.shape[0]
    t = MIX_TILE
    n_tiles = s_len // t

    def body(dx1_ref, x_ref, o_ref, z_ref, zh_ref, mod_ref, g_ref, p384_ref, cw_ref, sguw_ref, bmat_ref, pwbd_ref,
             psc_ref, segp_ref, win_ref, wout_ref,
             dx_ref, dz_ref, do_ref, ycat_ref, h1_ref, vec_ref, v384_ref, dcw_ref, dsguw_ref, dbmat_ref, dpw_ref,
             dpsc_ref, hext_ref, zext_ref, gext_ref, qext_ref, grot_ref, qrot_ref):
        i = pl.program_id(0)
        tile_idx = n_tiles - 1 - i

        @pl.when(i == 0)
        def _():
            for ref in (vec_ref, v384_ref, dcw_ref, dsguw_ref, dbmat_ref, dpw_ref, dpsc_ref, grot_ref, qrot_ref):
                ref[...] = jnp.zeros_like(ref)

        dx1v, ov, z = dx1_ref[...], o_ref[...], z_ref[...]
        xv = _interleave(x_ref[...]) if natural_x else x_ref[...]
        sh1, sc1, g1 = mod_ref[0:1, :], mod_ref[1:2, :], mod_ref[2:3, :]
        pre_g, post_g, bg = g_ref[0:1, :], g_ref[1:2, :], g_ref[2:3, :]
        segp = segp_ref[...]

        ohat, ro = _rms_fwd(ov)
        vec_ref[1:2, :] += _colsum(dx1v * (ohat * post_g))
        don = dx1v * g1
        vec_ref[0:1, :] += _colsum(don * ohat)
        dob = _rms_bwd(don * post_g, ohat, ro).astype(BF16)
        do_ref[...] = dob
        dycat = _dot_nt(dob, wout_ref[...])

        not_first = (tile_idx > 0).astype(F32)
        o_b = 2 * SGU_WIDTH
        o_c = o_b + 2 * CONV_WIDTH
        h_prev = zh_ref[:, o_b:o_b + CONV_WIDTH] * jax.nn.sigmoid(zh_ref[:, o_b + CONV_WIDTH:o_c])
        h_prev_rot = _rot_rows(h_prev, 1) * not_first
        z_prev_rot = _rot_rows(zh_ref[t - POOL_HALO:t, o_c:o_c + POOL_WIDTH], 1) * not_first
        wm = _masked_sgu_w(sguw_ref)
        r = _branches_fwd(z, tile_idx, p384_ref, cw_ref, wm, bmat_ref, pwbd_ref, psc_ref, segp_ref, g_ref,
                          hext_ref, zext_ref, h_prev_rot, z_prev_rot)
        ycat_ref[...] = r["ycat"].astype(BF16)

        def branch_norm_bwd(dyn, yhat, rr, gain):
            return _colsum(dyn * yhat), _rms_bwd(dyn * gain, yhat, rr)

        dga, dya = branch_norm_bwd(dycat[:, 0:384], r["yha"], r["ra"], bg[:, 0:384])
        dgb, dyb = branch_norm_bwd(dycat[:, 384:768], r["yhb"], r["rb"], bg[:, 384:768])
        dgc, dyc = branch_norm_bwd(dycat[:, 768:1024], r["yhc"], r["rc"], bg[:, 768:1024])
        vec_ref[5:6, :] += jnp.concatenate([dga, dgb, dgc], axis=1)

        du_act = dya * r["f"]
        df = dya * r["u"]
        first_head = _lane((t, 128)) < HEAD_DIM
        dbmat_ref[...] += df
        dvn_pairs = []
        for pr in range(SGU_HEADS // 2):
            dfp = df[:, pr * 128:(pr + 1) * 128]
            df0 = jnp.where(first_head, dfp, 0.0).astype(BF16)
            df1 = jnp.where(first_head, 0.0, dfp).astype(BF16)
            vp = r["vnb"][:, pr * 128:(pr + 1) * 128]
            dvn_pairs.append(_dot_tn(wm[2 * pr], df0) + _dot_tn(wm[2 * pr + 1], df1))
            dsguw_ref[2 * pr] += _dot_nt(df0, vp)
            dsguw_ref[2 * pr + 1] += _dot_nt(df1, vp)
        dvn = jnp.concatenate(dvn_pairs, axis=1)
        v384_ref[0:1, :] += _colsum(dvn * r["xh_v"])
        v384_ref[1:2, :] += _colsum(dvn)
        dxh = dvn * p384_ref[0:1, :]
        dvraw = r["rstd_v"] * (dxh - _seg_mean(dxh, segp) - r["xh_v"] * _seg_mean(dxh * r["xh_v"], segp))
        zu, zv = z[:, 0:SGU_WIDTH], z[:, SGU_WIDTH:o_b]
        _, tu = _gelu(zu)
        _, tv = _gelu(zv)
        dz_u = du_act * _gelu_grad(zu, tu)
        dz_v = dvraw * _gelu_grad(zv, tv)

        cn, sig_c = r["cn"], r["sig_c"]
        dcn = dyb * (sig_c * (1.0 + cn * (1.0 - sig_c)))
        v384_ref[3:4, :] += _colsum(dcn * r["xh_c"])
        v384_ref[4:5, :] += _colsum(dcn)
        dxc = dcn * p384_ref[3:4, :]
        gconv = r["rstd_c"] * (dxc - _rowmean(dxc) - r["xh_c"] * _rowmean(dxc * r["xh_c"]))
        v384_ref[2:3, :] += _colsum(gconv)
        gext_ref[0:t, :] = gconv
        gext_ref[t:2 * t, :], grot_ref[...] = _lookahead_head(gconv, grot_ref[...])
        dhh = jnp.zeros((t, CONV_WIDTH), F32)
        for k in range(CONV_K):
            shift = CONV_K - 1 - k
            dcw_ref[k:k + 1, :] += _colsum(gconv * hext_ref[pl.ds(t - 8 * shift, t), :])
            dhh = dhh + cw_ref[k:k + 1, :] * gext_ref[pl.ds(8 * shift, t), :]
        sig_g = r["sig_g"]
        dz_a = dhh * sig_g
        dz_g = dhh * r["a_in"] * sig_g * (1.0 - sig_g)

        dpsc_ref[0:1, :] += _colsum(dyc * r["ycp"])
        dycp = (dyc * psc_ref[0:1, :]).astype(BF16)
        dpw_ref[...] += _dot_tn(r["dpool"].astype(BF16), dycp)
        ddp = _dot_nt(dycp, pwbd_ref[...])
        inv = _pool_inv_counts(t, tile_idx)
        q = ddp * _by_pool_group((t, POOL_WIDTH), inv)
        qext_ref[0:t, :] = q
        qext_ref[t:t + POOL_HALO, :], qrot_ref[...] = _lookahead_head(q[0:POOL_HALO, :], qrot_ref[...])
        sums, acc = [], q
        for j in range(1, POOL_WINDOWS[-1]):
            acc = acc + qext_ref[pl.ds(8 * j, t), :]
            if j + 1 in POOL_WINDOWS:
                sums.append(acc)
        dz_c = _by_pool_group((t, POOL_WIDTH), sums) - ddp

        dzb = jnp.concatenate([dz_u, dz_v, dz_a, dz_g, dz_c], axis=1).astype(BF16)
        dz_ref[...] = dzb
        dh1 = _dot(dzb, win_ref[...])

        xhat, rx = _rms_fwd(xv)
        xn = xhat * pre_g
        h1_ref[...] = (xn * (1.0 + sc1) + sh1).astype(BF16)
        vec_ref[2:3, :] += _colsum(dh1)
        vec_ref[3:4, :] += _colsum(dh1 * xn)
        dxn = dh1 * (1.0 + sc1)
        vec_ref[4:5, :] += _colsum(dxn * xhat)
        dx = dx1v + _rms_bwd(dxn * pre_g, xhat, rx)
        dx_ref[...] = _deinterleave(dx) if natural_x else dx

        @pl.when(i == n_tiles - 1)
        def _():
            keep = _sgu_keep_mask(t)
            for h in range(SGU_HEADS):
                dsguw_ref[h] = jnp.where(keep, dsguw_ref[h], 0.0)
            dbmat_ref[...] = float(HEAD_DIM) * _seg_mean(dbmat_ref[...], segp)

    rev = lambda i: (n_tiles - 1 - i, 0)
    tile = pl.BlockSpec((t, D_MODEL), rev)
    ztile = pl.BlockSpec((t, IN_WIDTH), rev)
    zhalo = pl.BlockSpec((t, IN_WIDTH), lambda i: (jnp.maximum(n_tiles - 2 - i, 0), 0))
    consts = (modv, g1024, p384, cw, sguw, bmat, pwbd, psc, segp, win, wout)
    acc = lambda shape: pl.BlockSpec(shape, lambda i: (0,) * len(shape))
    acc_shapes = [(8, D_MODEL), (8, SGU_WIDTH), (32, CONV_WIDTH), (SGU_HEADS, t, t), (t, SGU_WIDTH),
                  (POOL_WIDTH, POOL_WIDTH), (8, POOL_WIDTH)]
    return _pallas_call_with_exchange(
        body,
        grid=(n_tiles,),
        in_specs=[tile, tile, tile, ztile, zhalo] + [_whole(c.shape) for c in consts],
        out_specs=[tile, ztile, tile, tile, tile] + [acc(s) for s in acc_shapes],
        out_shape=[jax.ShapeDtypeStruct((s_len, D_MODEL), F32), jax.ShapeDtypeStruct((s_len, IN_WIDTH), BF16),
                   jax.ShapeDtypeStruct((s_len, D_MODEL), BF16), jax.ShapeDtypeStruct((s_len, D_MODEL), BF16),
                   jax.ShapeDtypeStruct((s_len, D_MODEL), BF16)] + [jax.ShapeDtypeStruct(s, F32) for s in acc_shapes],
        scratch_shapes=[pltpu.VMEM((2 * t, CONV_WIDTH), F32), pltpu.VMEM((POOL_HALO + t, POOL_WIDTH), F32),
                        pltpu.VMEM((2 * t, CONV_WIDTH), F32), pltpu.VMEM((t + POOL_HALO, POOL_WIDTH), F32),
                        pltpu.VMEM((t, CONV_WIDTH), F32), pltpu.VMEM((POOL_HALO, POOL_WIDTH), F32)],
        operands=(dx1, x, o, z, z, *consts),
        name=name, job=job)


def _loss_grad(xo, target):
    s_len = xo.shape[0]
    t = TILE

    def body(x_ref, t_ref, dx_ref, loss_ref):
        i = pl.program_id(0)

        @pl.when(i == 0)
        def _():
            loss_ref[...] = jnp.zeros_like(loss_ref)

        diff = x_ref[...] - _interleave(t_ref[...])
        dx_ref[...] = diff * (1.0 / D_MODEL)
        loss_ref[...] += (0.5 / D_MODEL) * jnp.sum(diff * diff)

    tile = pl.BlockSpec((t, D_MODEL), lambda i: (i, 0))
    return pl.pallas_call(
        body,
        grid=(s_len // t,),
        in_specs=[tile, tile],
        out_specs=[tile, pl.BlockSpec((8, 128), lambda i: (0, 0))],
        out_shape=[jax.ShapeDtypeStruct((s_len, D_MODEL), F32), jax.ShapeDtypeStruct((8, 128), F32)],
        compiler_params=_cparams(1),
        name="loss_grad",
    )(xo, target)


MOD_SHARD = 6 * D_MODEL // N_DEV

ROW_DMOD = 0
ROW_G1024 = 8
ROW_V384 = 16
ROW_SGU_B = 24
ROW_POOL_SCALE = 25
ROW_CONV_W = 32
ROW_FFN_CONV = 64
ROW_POOL_W = 96
ROW_SGU_W = 112
ROWS_PER_LAYER = 208
N_LAYERS = 2


def _gather_weights(c8, mod_w, mod_b8, job):
    kinds = [kind for kind, _ in job]
    shards = [a for _, a in job]
    n = len(shards)

    def body(c_ref, modw_ref, modb_ref, *rest):
        shard_refs = rest[:n]
        sc_all_ref, modrows_ref = rest[n], rest[n + 1]
        full_refs = rest[n + 2:2 * n + 2]
        send_buf, mod_recv, w_send, w_recv, w_local, sc_send, sc_recv, mod_send, mod_recv_sem = rest[2 * n + 2:]
        pos = _my_pos()
        me = _flat(pos)
        peers = [_peer(pos, k) for k in range(1, N_DEV)]

        w_copies = _exchange_copies(kinds, shard_refs, full_refs, w_send, w_recv, w_local, phase=0)
        for cp in w_copies:
            cp.start()

        cv = c_ref[...]
        sc_all_ref[me] = cv * jax.nn.sigmoid(cv)
        sc_copies = [_remote_copy(sc_all_ref.at[me], sc_all_ref.at[me], sc_send.at[k], sc_recv.at[k], peers[k])
                     for k in range(N_PEERS)]
        for cp in sc_copies:
            cp.start()
        for cp in sc_copies:
            cp.wait()

        sc = jnp.concatenate([sc_all_ref[j, 0:1, :] for j in range(N_DEV)], axis=0)
        send_buf[...] = jnp.zeros_like(send_buf)
        for l in range(N_LAYERS):
            part = jnp.dot(sc, modw_ref[l], precision=lax.Precision.HIGHEST, preferred_element_type=F32)
            for j in range(N_DEV):
                send_buf[j, l:l + 1, :] = part[j:j + 1, :]
        mod_recv[me] = send_buf[me]
        mod_copies = [_remote_copy(send_buf.at[_flat(peers[k])], mod_recv.at[me], mod_send.at[k], mod_recv_sem.at[k],
                                   peers[k]) for k in range(N_PEERS)]
        for cp in mod_copies:
            cp.start()
        for cp in mod_copies:
            cp.wait()
        modrows_ref[...] = jnp.zeros_like(modrows_ref)
        for l in range(N_LAYERS):
            row = jnp.concatenate([mod_recv[j, l:l + 1, :] for j in range(N_DEV)], axis=1)
            modrows_ref[l:l + 1, :] = row + modb_ref[l:l + 1, :]

        for cp in w_copies:
            cp.wait()
        relays = _exchange_copies(kinds, shard_refs, full_refs, w_send, w_recv, w_local, phase=1)
        for cp in relays:
            cp.start()
        for cp in relays:
            cp.wait()

    out_shape = ([jax.ShapeDtypeStruct((N_DEV, 8, D_MODEL), F32), jax.ShapeDtypeStruct((8, 6 * D_MODEL), F32)]
                 + [jax.ShapeDtypeStruct((N_DEV,) + s.shape, s.dtype) for s in shards])
    return pl.pallas_call(
        body,
        in_specs=[VMEM, VMEM, VMEM] + [ANY] * n,
        out_specs=[VMEM, VMEM] + [ANY] * n,
        out_shape=out_shape,
        scratch_shapes=[pltpu.VMEM((N_DEV, 8, MOD_SHARD), F32), pltpu.VMEM((N_DEV, 8, MOD_SHARD), F32),
                        pltpu.SemaphoreType.DMA((n, N_PEERS)), pltpu.SemaphoreType.DMA((n, N_PEERS)),
                        pltpu.SemaphoreType.DMA((n,)),
                        pltpu.SemaphoreType.DMA((N_PEERS,)), pltpu.SemaphoreType.DMA((N_PEERS,)),
                        pltpu.SemaphoreType.DMA((N_PEERS,)), pltpu.SemaphoreType.DMA((N_PEERS,))],
        compiler_params=pltpu.CompilerParams(vmem_limit_bytes=VMEM_LIMIT_BYTES),
        name="gather_weights",
    )(c8, mod_w, mod_b8, *shards)


def _reduce_grads(sc_all, small0, small_all1, bigs):
    n = len(bigs)
    kinds = ["scatter"] * n

    def body(sc_all_ref, small0_ref, small_all1_ref, *rest):
        big_refs = rest[:n]
        small_sum_ref, gmodw_ref = rest[n], rest[n + 1]
        recv_refs = rest[n + 2:2 * n + 2]
        small_all0, b_send, b_recv, b_local, s_send, s_recv = rest[2 * n + 2:]
        pos = _my_pos()
        me = _flat(pos)

        small_all0[me] = small0_ref[...]
        s_copies = [_remote_copy(small0_ref, small_all0.at[me], s_send.at[k], s_recv.at[k], _peer(pos, k + 1))
                    for k in range(N_PEERS)]
        for cp in s_copies:
            cp.start()
        b_copies = _exchange_copies(kinds, big_refs, recv_refs, b_send, b_recv, b_local, phase=0)
        for cp in b_copies:
            cp.start()

        for cp in s_copies:
            cp.wait()
        sc = jnp.concatenate([sc_all_ref[j, 0:1, :] for j in range(N_DEV)], axis=0)
        for l, parts in enumerate((small_all0, small_all1_ref)):
            total = parts[0]
            for j in range(1, N_DEV):
                total = total + parts[j]
            small_sum_ref[l] = total
            dm = jnp.concatenate([parts[j, pl.ds(ROW_DMOD + me, 1), 0:MOD_SHARD] for j in range(N_DEV)], axis=0)
            gmodw_ref[l] = lax.dot_general(sc, dm, (((0,), (0,)), ((), ())), precision=lax.Precision.HIGHEST,
                                           preferred_element_type=F32)

        for cp in b_copies:
            cp.wait()

    out_shape = ([jax.ShapeDtypeStruct((N_LAYERS, ROWS_PER_LAYER, D_MODEL), F32),
                  jax.ShapeDtypeStruct((N_LAYERS, D_MODEL, MOD_SHARD), F32)]
                 + [jax.ShapeDtypeStruct(b.shape, b.dtype) for b in bigs])
    return pl.pallas_call(
        body,
        in_specs=[VMEM, VMEM, VMEM] + [ANY] * n,
        out_specs=[VMEM, VMEM] + [ANY] * n,
        out_shape=out_shape,
        scratch_shapes=[pltpu.VMEM((N_DEV, ROWS_PER_LAYER, D_MODEL), F32)] + _exchange_sems(n)
                       + [pltpu.SemaphoreType.DMA((N_PEERS,)), pltpu.SemaphoreType.DMA((N_PEERS,))],
        compiler_params=pltpu.CompilerParams(vmem_limit_bytes=VMEM_LIMIT_BYTES),
        name="reduce_grads",
    )(sc_all, small0, small_all1, *bigs)


def _adam_update(g, w, m, v):
    m2 = ADAM_B1 * m + (1.0 - ADAM_B1) * g
    v2 = ADAM_B2 * v + (1.0 - ADAM_B2) * (g * g)
    m_hat = m2 / (1.0 - ADAM_B1 ** ADAM_STEP)
    v_hat = v2 / (1.0 - ADAM_B2 ** ADAM_STEP)
    delta = -ADAM_LR * (m_hat / (jnp.sqrt(v_hat) + ADAM_EPS) + ADAM_WD * w)
    return delta, m2, v2


def _pair_add(g, r1, row_chunk, name):
    _, rows, cols = g.shape
    core = lax.axis_index("c").astype(jnp.int32).reshape(1)

    def body(core_ref, g_ref, r_ref, o_ref):
        o_ref[0] = (g_ref[0, 0].astype(F32) + r_ref[0].astype(F32)).astype(BF16)

    blk = pl.BlockSpec((1, row_chunk, cols), lambda q, i, core_ref: (q, i, 0))
    grid_spec = pltpu.PrefetchScalarGridSpec(
        num_scalar_prefetch=1, grid=(N_CHIPS, rows // row_chunk),
        in_specs=[pl.BlockSpec((1, 1, row_chunk, cols), lambda q, i, core_ref: (q, core_ref[0], i, 0)), blk],
        out_specs=blk)
    return pl.pallas_call(
        body, grid_spec=grid_spec, out_shape=jax.ShapeDtypeStruct((N_CHIPS, rows, cols), BF16),
        compiler_params=_cparams(2), name=name,
    )(core, g.reshape(N_CHIPS, 2, rows, cols), r1)


def _adam_sharded(recv0, recv1, w, m, v, row_chunk, name):
    _, rows, cols = w.shape
    n_chunks = rows // row_chunk

    def body(r0_ref, r1_ref, w_ref, m_ref, v_ref, g_ref, d_ref, m2_ref, v2_ref):
        layer = pl.program_id(0)

        def run(r_ref):
            g = r_ref[0].astype(F32)
            for j in range(1, r_ref.shape[0]):
                g = g + r_ref[j].astype(F32)
            delta, m2, v2 = _adam_update(g, w_ref[0], m_ref[0], v_ref[0])
            g_ref[0], d_ref[0], m2_ref[0], v2_ref[0] = g, delta, m2, v2

        @pl.when(layer == 0)
        def _():
            run(r0_ref)

        @pl.when(layer == 1)
        def _():
            run(r1_ref)

    r0_spec = pl.BlockSpec((recv0.shape[0], row_chunk, cols), lambda l, i: (0, i * (1 - l) + (n_chunks - 1) * l, 0))
    r1_spec = pl.BlockSpec((recv1.shape[0], row_chunk, cols), lambda l, i: (0, i * l, 0))
    blk = pl.BlockSpec((1, row_chunk, cols), lambda l, i: (l, i, 0))
    out = jax.ShapeDtypeStruct(w.shape, F32)
    return pl.pallas_call(
        body,
        grid=(N_LAYERS, n_chunks),
        in_specs=[r0_spec, r1_spec, blk, blk, blk],
        out_specs=[blk] * 4,
        out_shape=[out] * 4,
        compiler_params=_cparams(2),
        name=name,
    )(recv0, recv1, w, m, v)


def _adam_dense(g, w, m, v, row_chunk, name):
    n_lead, rows, cols = w.shape

    def body(g_ref, w_ref, m_ref, v_ref, go_ref, d_ref, m2_ref, v2_ref):
        gv = g_ref[...]
        go_ref[...] = gv
        d_ref[...], m2_ref[...], v2_ref[...] = _adam_update(gv, w_ref[...], m_ref[...], v_ref[...])

    blk = pl.BlockSpec((1, row_chunk, cols), lambda l, i: (l, i, 0))
    out = jax.ShapeDtypeStruct(w.shape, F32)
    return pl.pallas_call(
        body,
        grid=(n_lead, rows // row_chunk),
        in_specs=[blk] * 4,
        out_specs=[blk] * 4,
        out_shape=[out] * 4,
        compiler_params=_cparams(2),
        name=name,
    )(g, w, m, v)


WEIGHT_NAMES = ("mod_w", "mod_b", "mix_pre_g", "mix_post_g", "w_in", "sgu_norm_g", "sgu_norm_b", "sgu_w", "sgu_b",
                "conv_w", "conv_b", "conv_norm_g", "conv_norm_b", "pool_w", "pool_scale", "branch_g", "w_out",
                "ffn_pre_g", "ffn_post_g", "ffn_up", "ffn_conv_w", "ffn_conv_b", "ffn_down")
SHARDED_BIG = ("w_in", "w_out", "ffn_up", "ffn_down")
SMALL_PACKED = tuple(n for n in WEIGHT_NAMES if n not in SHARDED_BIG + ("mod_w",))


def _rows8(rows, width=D_MODEL):
    out = [jnp.pad(r.astype(F32), (0, width - r.shape[0]))[None] for r in rows]
    out.append(jnp.zeros((8 - len(rows), width), F32))
    return jnp.concatenate(out, axis=0)


def _as_rows(a, width=D_MODEL):
    flat = a.astype(F32).reshape(-1)
    pad = (-flat.shape[0]) % width
    return jnp.pad(flat, (0, pad)).reshape(-1, width)


def _pad_cols(a, width=D_MODEL):
    return jnp.pad(a.astype(F32), ((0, 0), (0, width - a.shape[1])))


def _pack_rows(arrays):
    rows = jnp.concatenate([_as_rows(a) for a in arrays], axis=0)
    return jnp.pad(rows, ((0, (-rows.shape[0]) % 8), (0, 0)))


def _unpack_rows(packed, shapes):
    out, r = [], 0
    for shape in shapes:
        size = math.prod(shape)
        n_rows = -(-size // D_MODEL)
        out.append(packed[r:r + n_rows].reshape(-1)[:size].reshape(shape))
        r += n_rows
    return out


def _interleave_axis(a, axis, inverse=False):
    shape = a.shape
    t = shape[axis]
    split = (t // 8, 8) if inverse else (8, t // 8)
    a = a.reshape(shape[:axis] + split + shape[axis + 1:])
    return jnp.swapaxes(a, axis, axis + 1).reshape(shape)


def _layer_consts(l, w, mod_rows, win, wout, conv_w_full):
    modv = _rows8(list(mod_rows[l].reshape(6, D_MODEL)))
    g1024 = _rows8([w["mix_pre_g"][l], w["mix_post_g"][l], w["branch_g"][l], w["ffn_pre_g"][l], w["ffn_post_g"][l]])
    p384 = _rows8([w["sgu_norm_g"][l], w["sgu_norm_b"][l], w["conv_b"][l], w["conv_norm_g"][l], w["conv_norm_b"][l]],
                  SGU_WIDTH)
    cw = jnp.pad(conv_w_full[l], ((0, 32 - CONV_K), (0, 0)))
    reps = MIX_TILE // CHUNK
    sguw = _interleave_axis(_interleave_axis(jnp.tile(w["sgu_w"][l], (1, reps, reps)), 1), 2)
    bmat = _interleave_axis(jnp.tile(jnp.repeat(w["sgu_b"][l].T, HEAD_DIM, axis=1), (reps, 1)), 0)
    groups = len(POOL_WINDOWS)
    eye = jnp.eye(groups, dtype=F32)
    pwbd = (eye[:, None, :, None] * w["pool_w"][l][:, :, None, :]).reshape(POOL_WIDTH, POOL_WIDTH).astype(BF16)
    psc = _rows8([w["pool_scale"][l]], POOL_WIDTH)
    seg = jnp.arange(SGU_WIDTH) // HEAD_DIM
    segp = jnp.where(seg[:, None] == seg[None, :], 1.0 / HEAD_DIM, 0.0).astype(BF16)
    return modv, g1024, (modv, g1024, p384, cw, sguw, bmat, pwbd, psc, segp, win, wout)


def _small_grad_rows(mix, ffn):
    _, _, _, _, _, mvec, v384, dcw, dsguw, dbmat, dpw, dpsc = mix
    fvec, cgrad = ffn[5], ffn[6]
    dmod = jnp.stack([mvec[2], mvec[3], mvec[1], fvec[2], fvec[3], fvec[1]]).reshape(N_DEV, MOD_SHARD)
    g_rows = jnp.stack([mvec[4], mvec[0], mvec[5], fvec[4], fvec[0]])
    reps = MIX_TILE // CHUNK
    dbmat = _interleave_axis(dbmat, 0, inverse=True).reshape(reps, CHUNK, SGU_WIDTH).sum(axis=0)
    dsguw = _interleave_axis(_interleave_axis(dsguw, 1, inverse=True), 2, inverse=True)
    dsguw = dsguw.reshape(SGU_HEADS, reps, CHUNK, reps, CHUNK)
    dsguw = sum(dsguw[:, b, :, b, :] for b in range(reps))
    dsgu_b = dbmat[:, ::HEAD_DIM].T.reshape(1, SGU_HEADS * CHUNK)
    groups = len(POOL_WINDOWS)
    gdim = POOL_WIDTH // groups
    dpw4 = dpw.reshape(groups, gdim, groups, gdim)
    dpool = jnp.stack([dpw4[g, :, g, :] for g in range(groups)])
    blocks = [_pad_cols(dmod), _rows8(list(g_rows)), _pad_cols(v384), _rows8([dsgu_b[0], dpsc[0]]), _pad_cols(dcw),
              _pad_cols(cgrad[:, 0:4, :].reshape(4 * N_DEV, FF_SHARD)), _as_rows(dpool), _as_rows(dsguw)]
    return jnp.concatenate(blocks, axis=0)


def _small_grads_from_rows(total):
    per = {n: [] for n in SMALL_PACKED}
    for l in range(N_LAYERS):
        s = total[l]
        per["mod_b"].append(s[ROW_DMOD:ROW_DMOD + N_DEV, :MOD_SHARD].reshape(6 * D_MODEL))
        for j, name in enumerate(("mix_pre_g", "mix_post_g", "branch_g", "ffn_pre_g", "ffn_post_g")):
            per[name].append(s[ROW_G1024 + j])
        for j, name in enumerate(("sgu_norm_g", "sgu_norm_b", "conv_b", "conv_norm_g", "conv_norm_b")):
            per[name].append(s[ROW_V384 + j, :SGU_WIDTH])
        per["sgu_b"].append(s[ROW_SGU_B, :SGU_HEADS * CHUNK].reshape(SGU_HEADS, CHUNK))
        per["pool_scale"].append(s[ROW_POOL_SCALE, :POOL_WIDTH])
        per["conv_w"].append(s[ROW_CONV_W:ROW_CONV_W + CONV_K, :CONV_WIDTH])
        fc = s[ROW_FFN_CONV:ROW_FFN_CONV + 4 * N_DEV, :FF_SHARD].reshape(N_DEV, 4, FF_SHARD)
        per["ffn_conv_w"].append(fc[:, 0:3, :].transpose(1, 0, 2).reshape(FFN_CONV_K, 2 * D_FF))
        per["ffn_conv_b"].append(fc[:, 3, :].reshape(2 * D_FF))
        per["pool_w"].append(s[ROW_POOL_W:ROW_POOL_W + 16].reshape(len(POOL_WINDOWS), HEAD_DIM, HEAD_DIM))
        per["sgu_w"].append(s[ROW_SGU_W:ROW_SGU_W + 96].reshape(SGU_HEADS, CHUNK, CHUNK))
    return {n: jnp.stack(v) for n, v in per.items()}


def kernel(x, c, mod_w, mod_b, mix_pre_g, mix_post_g, w_in, sgu_norm_g, sgu_norm_b, sgu_w, sgu_b, conv_w, conv_b, conv_norm_g, conv_norm_b, pool_w, pool_scale, branch_g, w_out, ffn_pre_g, ffn_post_g, ffn_up, ffn_conv_w, ffn_conv_b, ffn_down, loss_target, m_mod_w, m_mod_b, m_mix_pre_g, m_mix_post_g, m_w_in, m_sgu_norm_g, m_sgu_norm_b, m_sgu_w, m_sgu_b, m_conv_w, m_conv_b, m_conv_norm_g, m_conv_norm_b, m_pool_w, m_pool_scale, m_branch_g, m_w_out, m_ffn_pre_g, m_ffn_post_g, m_ffn_up, m_ffn_conv_w, m_ffn_conv_b, m_ffn_down, v_mod_w, v_mod_b, v_mix_pre_g, v_mix_post_g, v_w_in, v_sgu_norm_g, v_sgu_norm_b, v_sgu_w, v_sgu_b, v_conv_w, v_conv_b, v_conv_norm_g, v_conv_norm_b, v_pool_w, v_pool_scale, v_branch_g, v_w_out, v_ffn_pre_g, v_ffn_post_g, v_ffn_up, v_ffn_conv_w, v_ffn_conv_b, v_ffn_down):
    w = dict(zip(WEIGHT_NAMES, (mod_w, mod_b, mix_pre_g, mix_post_g, w_in, sgu_norm_g, sgu_norm_b, sgu_w, sgu_b, conv_w,
                                conv_b, conv_norm_g, conv_norm_b, pool_w, pool_scale, branch_g, w_out, ffn_pre_g,
                                ffn_post_g, ffn_up, ffn_conv_w, ffn_conv_b, ffn_down)))
    m = dict(zip(WEIGHT_NAMES, (m_mod_w, m_mod_b, m_mix_pre_g, m_mix_post_g, m_w_in, m_sgu_norm_g, m_sgu_norm_b, m_sgu_w,
                                m_sgu_b, m_conv_w, m_conv_b, m_conv_norm_g, m_conv_norm_b, m_pool_w, m_pool_scale,
                                m_branch_g, m_w_out, m_ffn_pre_g, m_ffn_post_g, m_ffn_up, m_ffn_conv_w, m_ffn_conv_b,
                                m_ffn_down)))
    v = dict(zip(WEIGHT_NAMES, (v_mod_w, v_mod_b, v_mix_pre_g, v_mix_post_g, v_w_in, v_sgu_norm_g, v_sgu_norm_b, v_sgu_w,
                                v_sgu_b, v_conv_w, v_conv_b, v_conv_norm_g, v_conv_norm_b, v_pool_w, v_pool_scale,
                                v_branch_g, v_w_out, v_ffn_pre_g, v_ffn_post_g, v_ffn_up, v_ffn_conv_w, v_ffn_conv_b,
                                v_ffn_down)))
    me = _flat(_my_pos())
    xs = x[0]
    s_len = xs.shape[0]

    transposed = ("w_in", "ffn_up")
    wt = {n: jnp.swapaxes(w[n], 1, 2) if n in transposed else w[n] for n in SHARDED_BIG}
    mt = {n: jnp.swapaxes(m[n], 1, 2) if n in transposed else m[n] for n in SHARDED_BIG}
    vt = {n: jnp.swapaxes(v[n], 1, 2) if n in transposed else v[n] for n in SHARDED_BIG}
    bf16_shards = [[wt[n][l].astype(BF16) for n in SHARDED_BIG] for l in range(N_LAYERS)]

    def mixer_operands(l, win_g, wout_g):
        win = win_g.reshape(IN_WIDTH, D_MODEL)
        return _layer_consts(l, w, mod_rows, win, wout_g.reshape(D_MODEL, D_MODEL), conv_w_full)

    def ffn_operands(l, modv, g1024, wup_g, wdn_g):
        wdn = wdn_g.reshape(FF_PAIRS, FF_SHARD, D_MODEL)
        return modv, g1024, wup_g, wdn, ffn_cw_full[:, l], ffn_conv_b[l].reshape(N_DEV, 1, FF_SHARD)

    w0, w1 = bf16_shards
    c8 = jnp.broadcast_to(c, (8, D_MODEL))
    mod_b8 = jnp.pad(mod_b, ((0, 8 - N_LAYERS), (0, 0)))
    sc_all, mod_rows, win0_g, wout0_g, conv_w_g, ffn_cw_full = _gather_weights(
        c8, mod_w, mod_b8, [("gather2", w0[0]), ("gather2", w0[1]), ("gather", conv_w), ("gather", ffn_conv_w)])
    conv_w_full = conv_w_g.transpose(1, 2, 0, 3).reshape(N_LAYERS, CONV_K, CONV_WIDTH)

    modv0, g0, mix_consts0 = mixer_operands(0, win0_g, wout0_g)
    (x1, z, o), (wup0_g, wdn0_g) = _mixer_fwd(xs, *mix_consts0, name="mixer_fwd_l0", natural_x=True,
                                              job=[("gather2", w0[2]), ("gather2", w0[3])])
    ffn_consts0 = ffn_operands(0, modv0, g0, wup0_g, wdn0_g)
    (x2, y2, p), (win1_g, wout1_g, wdn1_g) = _ffn_fwd(x1, *ffn_consts0, name="ffn_fwd_l0",
                                                      job=[("gather2", w1[0]), ("gather2", w1[1]), ("gather2", w1[3])])
    saved = [(xs, z, o, x1, y2, p)]
    modv1, g1, mix_consts1 = mixer_operands(1, win1_g, wout1_g)
    (x1, z, o), (wup1_g,) = _mixer_fwd(x2, *mix_consts1, name="mixer_fwd_l1", job=[("gather2", w1[2])])
    ffn_consts1 = ffn_operands(1, modv1, g1, wup1_g, wdn1_g)
    (x3, y2, p), _ = _ffn_fwd(x1, *ffn_consts1, name="ffn_fwd_l1")
    saved.append((x2, z, o, x1, y2, p))
    dh, loss_tile = _loss_grad(x3, loss_target[0])
    loss = lax.psum(loss_tile[0, 0], ("x", "y", "c"))

    def ffn_weight_grads(l, ffn):
        dp, a, dy2, h2 = ffn[1:5]
        d_up = _wgrad(dp, h2, f"wgrad_ffn_up_l{l}", tk=WGRAD_TK_FFN)
        d_dn = _wgrad(a, dy2, f"wgrad_ffn_down_l{l}", tk=WGRAD_TK_FFN).reshape(N_DEV, D_FF // N_DEV, D_MODEL)
        return d_up, d_dn

    def mixer_weight_grads(l, mix):
        dz, do, ycat, h1 = mix[1:5]
        d_in = _wgrad(dz[None], h1, f"wgrad_w_in_l{l}").reshape(N_DEV, IN_WIDTH // N_DEV, D_MODEL)
        d_out = _wgrad(ycat, do[None], f"wgrad_w_out_l{l}").reshape(N_DEV, D_MODEL // N_DEV, D_MODEL)
        return d_in, d_out

    x_in, z, o, x1, y2, p = saved[1]
    ffn1, _ = _ffn_bwd(dh, x1, y2, p, *ffn_consts1, name="ffn_bwd_l1")
    d_up1, d_dn1 = ffn_weight_grads(1, ffn1)
    mix1, (sib_up1, sib_dn1) = _mixer_bwd(ffn1[0], x_in, o, z, *mix_consts1, name="mixer_bwd_l1",
                                          job=[("scatter_p1", d_up1), ("scatter_p1", d_dn1)])
    chip_up1 = _pair_add(d_up1, sib_up1, 176, "pair_add_ffn_up_l1")
    chip_dn1 = _pair_add(d_dn1, sib_dn1, 176, "pair_add_ffn_down_l1")
    d_in1, d_out1 = mixer_weight_grads(1, mix1)
    small1 = _small_grad_rows(mix1, ffn1)

    x_in, z, o, x1, y2, p = saved[0]
    ffn0, job_out = _ffn_bwd(mix1[0], x1, y2, p, *ffn_consts0, name="ffn_bwd_l0",
                             job=[("scatter", d_in1), ("scatter", d_out1), ("scatter_p2", chip_up1),
                                  ("scatter_p2", chip_dn1), ("gather", small1)])
    recv1, small_all1 = job_out[0:4], job_out[4]
    d_up0, d_dn0 = ffn_weight_grads(0, ffn0)
    mix0, recv_ffn0 = _mixer_bwd(ffn0[0], x_in, o, z, *mix_consts0, name="mixer_bwd_l0", natural_x=True,
                                 job=[("scatter", d_up0), ("scatter", d_dn0)])
    d_in0, d_out0 = mixer_weight_grads(0, mix0)
    grad_x = mix0[0][None]

    reduced = _reduce_grads(sc_all, _small_grad_rows(mix0, ffn0), small_all1, [d_in0, d_out0])
    small_total, g_mod_w = reduced[0], reduced[1]
    recv0 = [reduced[2], reduced[3], recv_ffn0[0], recv_ffn0[1]]

    grads, deltas, new_m, new_v = {}, {}, {}, {}
    for j, (name, chunk) in enumerate((("w_in", 224), ("w_out", 128), ("ffn_up", 176), ("ffn_down", 176))):
        outs = _adam_sharded(recv0[j], recv1[j], wt[name], mt[name], vt[name], chunk, "adam_" + name)
        if name in transposed:
            outs = [jnp.swapaxes(t, 1, 2) for t in outs]
        grads[name], deltas[name], new_m[name], new_v[name] = outs
    grads["mod_w"], deltas["mod_w"], new_m["mod_w"], new_v["mod_w"] = _adam_dense(
        g_mod_w, mod_w, m_mod_w, v_mod_w, 256, "adam_mod_w")

    small_g = _small_grads_from_rows(small_total)
    small_g["conv_w"] = lax.dynamic_slice_in_dim(small_g["conv_w"], me * conv_w.shape[2], conv_w.shape[2], axis=2)
    small_g["ffn_conv_w"] = lax.dynamic_slice_in_dim(small_g["ffn_conv_w"], me * FF_SHARD, FF_SHARD, axis=2)
    shapes = [w[n].shape for n in SMALL_PACKED]
    packs = [_pack_rows([src[n] for n in SMALL_PACKED])[None] for src in (small_g, w, m, v)]
    _, d, m2, v2 = _adam_dense(*packs, packs[0].shape[1], "adam_small")
    for name, dd, mm, vv in zip(SMALL_PACKED, _unpack_rows(d[0], shapes), _unpack_rows(m2[0], shapes),
                                _unpack_rows(v2[0], shapes)):
        grads[name], deltas[name], new_m[name], new_v[name] = small_g[name], dd, mm, vv

    return (loss, grad_x, *[grads[n] for n in WEIGHT_NAMES], *[deltas[n] for n in WEIGHT_NAMES],
            *[new_m[n] for n in WEIGHT_NAMES], *[new_v[n] for n in WEIGHT_NAMES])
```

```python
import functools
import math

import jax
import jax.numpy as jnp
from jax import lax
from jax.experimental import pallas as pl
from jax.experimental.pallas import tpu as pltpu

F32 = jnp.float32
BF16 = jnp.bfloat16

D_MODEL = 1024
N_DEV = 8
SGU_WIDTH = 384
CONV_WIDTH = 384
POOL_WIDTH = 256
HEAD_DIM = 64
SGU_HEADS = 6
CHUNK = 128
CONV_K = 31
POOL_WINDOWS = (2, 4, 8, 16)
IN_WIDTH = 1792
D_FF = 2816
FF_SHARD = 2 * D_FF // N_DEV
FF_PAIRS = N_DEV // 2
FFN_CONV_K = 3
EPS = 1e-6
GELU_C0 = math.sqrt(2.0 / math.pi)
GELU_C1 = 0.044715

ADAM_LR = 0.001
ADAM_B1 = 0.9
ADAM_B2 = 0.999
ADAM_EPS = 1e-08
ADAM_WD = 0.01
ADAM_STEP = 10

VMEM_LIMIT_BYTES = 56 * 1024 * 1024
TILE = 256
MIX_TILE = TILE
FFN_TILE = TILE
FFN_HALO = 8 * (FFN_CONV_K - 1)
POOL_HALO = 8 * POOL_WINDOWS[-1]
WGRAD_TK = 2048
WGRAD_TK_FFN = 4096


def _cparams(n_axes):
    return pltpu.CompilerParams(dimension_semantics=("arbitrary",) * n_axes, vmem_limit_bytes=VMEM_LIMIT_BYTES)


def _whole(shape):
    nd = len(shape)
    return pl.BlockSpec(shape, lambda *_: (0,) * nd, pipeline_mode=pl.Buffered(1))


def _dot(a, b):
    return jnp.dot(a, b, preferred_element_type=F32)


def _dot_nt(a, b):
    return lax.dot_general(a, b, (((1,), (1,)), ((), ())), preferred_element_type=F32)


def _dot_tn(a, b):
    return lax.dot_general(a, b, (((0,), (0,)), ((), ())), preferred_element_type=F32)


def _gelu(x):
    t = jnp.tanh(GELU_C0 * (x + GELU_C1 * x * x * x))
    return 0.5 * x * (1.0 + t), t


def _gelu_grad(x, t):
    return 0.5 * (1.0 + t) + 0.5 * x * (1.0 - t * t) * (GELU_C0 * (1.0 + 3.0 * GELU_C1 * x * x))


def _rowmean(x):
    return jnp.mean(x, axis=-1, keepdims=True)


def _colsum(x):
    return jnp.sum(x, axis=0, keepdims=True)


def _rms_fwd(x):
    r = lax.rsqrt(_rowmean(x * x) + EPS)
    return x * r, r


def _rms_bwd(dxhat, xhat, r):
    return r * (dxhat - xhat * _rowmean(dxhat * xhat))


N_PEERS = N_DEV - 1
ANY = pl.BlockSpec(memory_space=pl.ANY)
VMEM = pl.BlockSpec(memory_space=pltpu.VMEM)


def _my_pos():
    return lax.axis_index("x"), lax.axis_index("y"), lax.axis_index("c")


def _peer(pos, k):
    x, y, c = pos
    return (1 - x if k & 4 else x, 1 - y if k & 2 else y, 1 - c if k & 1 else c)


def _flat(pos):
    return 4 * pos[0] + 2 * pos[1] + pos[2]


def _remote_copy(src, dst, send_sem, recv_sem, peer):
    return pltpu.make_async_remote_copy(src_ref=src, dst_ref=dst, send_sem=send_sem, recv_sem=recv_sem,
                                        device_id=peer, device_id_type=pl.DeviceIdType.MESH)


N_CHIPS = N_DEV // 2
SIBLING = 1
SAME_CORE_PEERS = (2, 4, 6)


def _exchange_out_shapes(job):
    def shape(kind, a):
        if kind in ("gather", "gather2"):
            return (N_DEV,) + a.shape
        if kind == "scatter_p1":
            return (N_CHIPS,) + a.shape[1:]
        return a.shape
    return [jax.ShapeDtypeStruct(shape(kind, a), a.dtype) for kind, a in job]


def _exchange_sems(n):
    return [pltpu.SemaphoreType.DMA((n, N_PEERS)), pltpu.SemaphoreType.DMA((n, N_PEERS)), pltpu.SemaphoreType.DMA((n,))]


def _exchange_copies(kinds, src_refs, dst_refs, send_sems, recv_sems, local_sems, phase):
    pos = _my_pos()
    me = _flat(pos)
    chip, core = 2 * pos[0] + pos[1], pos[2]
    copies = []

    def remote(a, src, dst, k, sem=None):
        sem = k - 1 if sem is None else sem
        copies.append(_remote_copy(src, dst, send_sems.at[a, sem], recv_sems.at[a, sem], _peer(pos, k)))

    for a, kind in enumerate(kinds):
        src, dst = src_refs[a], dst_refs[a]
        if phase == 1:
            if kind == "gather2":
                for k in SAME_CORE_PEERS:
                    remote(a, dst.at[me ^ k], dst.at[me ^ k], SIBLING, sem=k)
        elif kind in ("gather", "gather2"):
            copies.append(pltpu.make_async_copy(src, dst.at[me], local_sems.at[a]))
            for k in (range(1, N_DEV) if kind == "gather" else (SIBLING,) + SAME_CORE_PEERS):
                remote(a, src, dst.at[me], k)
        elif kind == "scatter":
            copies.append(pltpu.make_async_copy(src.at[me], dst.at[me], local_sems.at[a]))
            for k in range(1, N_DEV):
                remote(a, src.at[me ^ k], dst.at[me], k)
        elif kind == "scatter_p1":
            for q in range(N_CHIPS):
                remote(a, src.at[2 * q + 1 - core], dst.at[q], SIBLING, sem=q)
        elif kind == "scatter_p2":
            copies.append(pltpu.make_async_copy(src.at[chip], dst.at[chip], local_sems.at[a]))
            for k in SAME_CORE_PEERS:
                remote(a, src.at[chip ^ (k >> 1)], dst.at[chip], k)
    return copies


def _pallas_call_with_exchange(body, *, grid, in_specs, out_specs, out_shape, scratch_shapes, operands, name, job):
    params = _cparams(len(grid))
    if not job:
        outs = pl.pallas_call(body, grid=grid, in_specs=in_specs, out_specs=out_specs, out_shape=out_shape,
                              scratch_shapes=scratch_shapes, compiler_params=params, name=name)(*operands)
        return outs, []
    kinds = [kind for kind, _ in job]
    relayed = [kind if kind == "gather2" else None for kind in kinds]
    unrelayed = [None if kind == "gather2" else kind for kind in kinds]
    n, n_in, n_out, n_scr = len(job), len(in_specs), len(out_specs), len(scratch_shapes)
    n_steps = math.prod(grid)
    relay_step = max(n_steps - 2, 0)

    def wrapped(*refs):
        ins, jin = refs[:n_in], refs[n_in:n_in + n]
        outs, jout = refs[n_in + n:n_in + n + n_out], refs[n_in + n + n_out:n_in + 2 * n + n_out]
        scr = refs[n_in + 2 * n + n_out:n_in + 2 * n + n_out + n_scr]
        sems = refs[n_in + 2 * n + n_out + n_scr:]
        step = pl.program_id(0)
        for d in range(1, len(grid)):
            step = step * grid[d] + pl.program_id(d)

        def copies(which, phase):
            return _exchange_copies(which, jin, jout, *sems, phase=phase)

        @pl.when(step == 0)
        def _():
            for cp in copies(kinds, 0):
                cp.start()

        body(*ins, *outs, *scr)

        @pl.when(step == relay_step)
        def _():
            for cp in copies(relayed, 0):
                cp.wait()
            for cp in copies(relayed, 1):
                cp.start()

        @pl.when(step == n_steps - 1)
        def _():
            for cp in copies(unrelayed, 0) + copies(relayed, 1):
                cp.wait()

    res = pl.pallas_call(
        wrapped, grid=grid,
        in_specs=list(in_specs) + [ANY] * n,
        out_specs=list(out_specs) + [ANY] * n,
        out_shape=list(out_shape) + _exchange_out_shapes(job),
        scratch_shapes=list(scratch_shapes) + _exchange_sems(n),
        compiler_params=params, name=name,
    )(*operands, *[a for _, a in job])
    return res[:n_out], res[n_out:]


def _seg_mean(x, segp):
    hi = x.astype(BF16)
    lo = (x - hi.astype(F32)).astype(BF16)
    return _dot(hi, segp) + _dot(lo, segp)


def _rot_rows(x, shift):
    m, c = x.shape
    return pltpu.roll(x.reshape(m // 8, 8, c), shift, 1).reshape(m, c)


def _sublane_is(shape, s):
    return lax.broadcasted_iota(jnp.int32, shape, 0) % 8 == s


def _causal_tail(tail, prev_rot):
    rot = _rot_rows(tail, 1)
    return jnp.where(_sublane_is(tail.shape, 0), prev_rot, rot), rot


def _lookahead_head(head, next_rot):
    rot = _rot_rows(head, 7)
    return jnp.where(_sublane_is(head.shape, 7), next_rot, rot), rot


def _tile_token_index(t, tile_idx):
    r = lax.broadcasted_iota(jnp.int32, (t, 1), 0)
    return tile_idx * t + (r % 8) * (t // 8) + r // 8


def _interleave(x):
    t, c = x.shape
    return jnp.swapaxes(x.reshape(8, t // 8, c), 0, 1).reshape(t, c)


def _deinterleave(x):
    t, c = x.shape
    return jnp.swapaxes(x.reshape(t // 8, 8, c), 0, 1).reshape(t, c)


def _ffn_fwd(x1, modv, g1024, wup, wdn, cw, cb, name="ffn_fwd", job=None):
    s_len = x1.shape[0]
    t = FFN_TILE
    n_tiles = s_len // t

    def body(x1_ref, mod_ref, g_ref, wup_ref, wdn_ref, cw_ref, cb_ref, x2_ref, y2_ref, p_ref, ext_ref, carry_ref):
        i = pl.program_id(0)

        @pl.when(i == 0)
        def _():
            carry_ref[...] = jnp.zeros_like(carry_ref)

        x1v = x1_ref[...]
        pre_g, post_g = g_ref[3:4, :], g_ref[4:5, :]
        sh2, sc2, g2 = mod_ref[3:4, :], mod_ref[4:5, :], mod_ref[5:6, :]
        xhat, _ = _rms_fwd(x1v)
        h2b = (xhat * pre_g * (1.0 + sc2) + sh2).astype(BF16)

        def conv_shard(s):
            p = _dot_nt(h2b, wup_ref[s])
            p_ref[s] = p.astype(BF16)
            ext_ref[0:FFN_HALO, :], carry_ref[s] = _causal_tail(p[t - FFN_HALO:t, :], carry_ref[s])
            ext_ref[FFN_HALO:FFN_HALO + t, :] = p
            w = cw_ref[s]
            return w[0:1, :] * ext_ref[0:t, :] + w[1:2, :] * ext_ref[8:8 + t, :] + w[2:3, :] * p + cb_ref[s]

        y2 = jnp.zeros((t, D_MODEL), F32)
        for j in range(FF_PAIRS):
            ug = conv_shard(j)
            uv = conv_shard(j + FF_PAIRS)
            ge, _ = _gelu(ug)
            y2 = y2 + _dot((ge * uv).astype(BF16), wdn_ref[j])
        y2_ref[...] = y2
        yhat, _ = _rms_fwd(y2)
        x2_ref[...] = x1v + g2 * (yhat * post_g)

    tile = pl.BlockSpec((t, D_MODEL), lambda i: (i, 0))
    return _pallas_call_with_exchange(
        body,
        grid=(n_tiles,),
        in_specs=[tile, _whole(modv.shape), _whole(g1024.shape), _whole(wup.shape), _whole(wdn.shape),
                  _whole(cw.shape), _whole(cb.shape)],
        out_specs=[tile, tile, pl.BlockSpec((N_DEV, t, FF_SHARD), lambda i: (0, i, 0))],
        out_shape=[jax.ShapeDtypeStruct((s_len, D_MODEL), F32), jax.ShapeDtypeStruct((s_len, D_MODEL), F32),
                   jax.ShapeDtypeStruct((N_DEV, s_len, FF_SHARD), BF16)],
        scratch_shapes=[pltpu.VMEM((FFN_HALO + t, FF_SHARD), F32), pltpu.VMEM((N_DEV, FFN_HALO, FF_SHARD), F32)],
        operands=(x1, modv, g1024, wup, wdn, cw, cb),
        name=name, job=job)


def _ffn_bwd(dx2, x1, y2, p, modv, g1024, wup, wdn, cw, cb, name="ffn_bwd", job=None):
    s_len = x1.shape[0]
    t = FFN_TILE
    n_tiles = s_len // t
    hb = FFN_HALO

    def body(dx2_ref, x1_ref, y2_ref, p_ref, ph_ref, mod_ref, g_ref, wup_ref, wdn_ref, cw_ref, cb_ref,
             dx1_ref, dp_ref, a_ref, dy2_ref, h2_ref, vec_ref, cgrad_ref, ext_ref, dext_ref, dcarry_ref):
        i = pl.program_id(0)
        tile_idx = n_tiles - 1 - i

        @pl.when(i == 0)
        def _():
            vec_ref[...] = jnp.zeros_like(vec_ref)
            cgrad_ref[...] = jnp.zeros_like(cgrad_ref)
            dcarry_ref[...] = jnp.zeros_like(dcarry_ref)

        dx2v, x1v, y2v = dx2_ref[...], x1_ref[...], y2_ref[...]
        pre_g, post_g = g_ref[3:4, :], g_ref[4:5, :]
        sh2, sc2, g2 = mod_ref[3:4, :], mod_ref[4:5, :], mod_ref[5:6, :]

        yhat, ry = _rms_fwd(y2v)
        vec_ref[1:2, :] += _colsum(dx2v * (yhat * post_g))
        dyn = dx2v * g2
        vec_ref[0:1, :] += _colsum(dyn * yhat)
        dy2b = _rms_bwd(dyn * post_g, yhat, ry).astype(BF16)
        dy2_ref[...] = dy2b

        xhat, rx = _rms_fwd(x1v)
        xn = xhat * pre_g
        h2_ref[...] = (xn * (1.0 + sc2) + sh2).astype(BF16)

        not_first = (tile_idx > 0).astype(F32)

        def recompute(s, slot):
            pf = p_ref[s].astype(F32)
            prev_rot = _rot_rows(ph_ref[s].astype(F32), 1) * not_first
            ext_ref[slot, 0:hb, :], _ = _causal_tail(pf[t - hb:t, :], prev_rot)
            ext_ref[slot, hb:hb + t, :] = pf
            w = cw_ref[s]
            u = (w[0:1, :] * ext_ref[slot, 0:t, :] + w[1:2, :] * ext_ref[slot, 8:8 + t, :]
                 + w[2:3, :] * pf + cb_ref[s])
            return u

        def conv_bwd(s, slot, du):
            w = cw_ref[s]
            cgrad_ref[s, 0:1, :] += _colsum(du * ext_ref[slot, 0:t, :])
            cgrad_ref[s, 1:2, :] += _colsum(du * ext_ref[slot, 8:8 + t, :])
            cgrad_ref[s, 2:3, :] += _colsum(du * ext_ref[slot, 16:16 + t, :])
            cgrad_ref[s, 3:4, :] += _colsum(du)
            dext_ref[0:t, :] = du
            dext_ref[t:t + hb, :], dcarry_ref[s] = _lookahead_head(du[0:hb, :], dcarry_ref[s])
            dp = w[2:3, :] * du + w[1:2, :] * dext_ref[8:8 + t, :] + w[0:1, :] * dext_ref[16:16 + t, :]
            dpb = dp.astype(BF16)
            dp_ref[s] = dpb
            return _dot(dpb, wup_ref[s])

        dh2 = jnp.zeros((t, D_MODEL), F32)
        for j in range(FF_PAIRS):
            ug = recompute(j, 0)
            uv = recompute(j + FF_PAIRS, 1)
            ge, th = _gelu(ug)
            a_ref[j] = (ge * uv).astype(BF16)
            da = _dot_nt(dy2b, wdn_ref[j])
            dh2 = dh2 + conv_bwd(j, 0, da * uv * _gelu_grad(ug, th))
            dh2 = dh2 + conv_bwd(j + FF_PAIRS, 1, da * ge)

        vec_ref[2:3, :] += _colsum(dh2)
        vec_ref[3:4, :] += _colsum(dh2 * xn)
        dxn = dh2 * (1.0 + sc2)
        vec_ref[4:5, :] += _colsum(dxn * xhat)
        dx1_ref[...] = dx2v + _rms_bwd(dxn * pre_g, xhat, rx)

    rev = lambda i: (n_tiles - 1 - i, 0)
    tile = pl.BlockSpec((t, D_MODEL), rev)
    halo_idx = lambda i: (0, jnp.maximum((n_tiles - 1 - i) * (t // hb) - 1, 0), 0)
    return _pallas_call_with_exchange(
        body,
        grid=(n_tiles,),
        in_specs=[tile, tile, tile,
                  pl.BlockSpec((N_DEV, t, FF_SHARD), lambda i: (0, n_tiles - 1 - i, 0)),
                  pl.BlockSpec((N_DEV, hb, FF_SHARD), halo_idx),
                  _whole(modv.shape), _whole(g1024.shape), _whole(wup.shape), _whole(wdn.shape),
                  _whole(cw.shape), _whole(cb.shape)],
        out_specs=[tile,
                   pl.BlockSpec((N_DEV, t, FF_SHARD), lambda i: (0, n_tiles - 1 - i, 0)),
                   pl.BlockSpec((FF_PAIRS, t, FF_SHARD), lambda i: (0, n_tiles - 1 - i, 0)),
                   tile, tile,
                   pl.BlockSpec((8, D_MODEL), lambda i: (0, 0)),
                   pl.BlockSpec((N_DEV, 8, FF_SHARD), lambda i: (0, 0, 0))],
        out_shape=[jax.ShapeDtypeStruct((s_len, D_MODEL), F32),
                   jax.ShapeDtypeStruct((N_DEV, s_len, FF_SHARD), BF16),
                   jax.ShapeDtypeStruct((FF_PAIRS, s_len, FF_SHARD), BF16),
                   jax.ShapeDtypeStruct((s_len, D_MODEL), BF16),
                   jax.ShapeDtypeStruct((s_len, D_MODEL), BF16),
                   jax.ShapeDtypeStruct((8, D_MODEL), F32),
                   jax.ShapeDtypeStruct((N_DEV, 8, FF_SHARD), F32)],
        scratch_shapes=[pltpu.VMEM((2, hb + t, FF_SHARD), F32), pltpu.VMEM((t + hb, FF_SHARD), F32),
                        pltpu.VMEM((N_DEV, hb, FF_SHARD), F32)],
        operands=(dx2, x1, y2, p, p, modv, g1024, wup, wdn, cw, cb),
        name=name, job=job)


def _wgrad(a, b, name, tk=WGRAD_TK, job=None):
    a_grouped, b_grouped = a.ndim == 3, b.ndim == 3
    groups = a.shape[0] if a_grouped else b.shape[0]
    s_len, m, n = a.shape[-2], a.shape[-1], b.shape[-1]
    tk = min(tk, s_len)
    n_k = s_len // tk

    def body(a_ref, b_ref, o_ref, acc_ref):
        k = pl.program_id(1)
        av = a_ref[0] if a_grouped else a_ref[...]
        bv = b_ref[0] if b_grouped else b_ref[...]
        part = _dot_tn(av, bv)
        if n_k == 1:
            o_ref[0] = part.astype(BF16)
            return

        @pl.when(k == 0)
        def _():
            acc_ref[...] = part

        @pl.when(jnp.logical_and(k > 0, k < n_k - 1))
        def _():
            acc_ref[...] += part

        @pl.when(k == n_k - 1)
        def _():
            o_ref[0] = (acc_ref[...] + part).astype(BF16)

    a_spec = pl.BlockSpec((1, tk, m), lambda g, k: (g, k, 0)) if a_grouped else pl.BlockSpec((tk, m), lambda g, k: (k, 0))
    b_spec = pl.BlockSpec((1, tk, n), lambda g, k: (g, k, 0)) if b_grouped else pl.BlockSpec((tk, n), lambda g, k: (k, 0))
    (out,), exchanged = _pallas_call_with_exchange(
        body,
        grid=(groups, n_k),
        in_specs=[a_spec, b_spec],
        out_specs=[pl.BlockSpec((1, m, n), lambda g, k: (g, 0, 0))],
        out_shape=[jax.ShapeDtypeStruct((groups, m, n), BF16)],
        scratch_shapes=[pltpu.VMEM((m, n), F32)],
        operands=(a, b),
        name=name, job=job)
    return (out, exchanged) if job else out


def _lane(shape):
    return lax.broadcasted_iota(jnp.int32, shape, 1)


def _by_pool_group(shape, vals):
    lane = _lane(shape)
    return jnp.where(lane < 64, vals[0], jnp.where(lane < 128, vals[1], jnp.where(lane < 192, vals[2], vals[3])))


def _pool_inv_counts(t, tile_idx):
    pos1 = _tile_token_index(t, tile_idx) + 1
    return [1.0 / jnp.minimum(pos1, w).astype(F32) for w in POOL_WINDOWS]


def _sgu_keep_mask(t):
    tok_r = _tile_token_index(t, 0)
    c = lax.broadcasted_iota(jnp.int32, (1, t), 1)
    tok_c = (c % 8) * (t // 8) + c // 8
    return jnp.logical_and(tok_r // CHUNK == tok_c // CHUNK, tok_r >= tok_c)


def _masked_sgu_w(sguw_ref):
    keep = _sgu_keep_mask(sguw_ref.shape[1])
    return [jnp.where(keep, sguw_ref[h], 0.0).astype(BF16) for h in range(SGU_HEADS)]


def _branches_fwd(z, tile_idx, p384_ref, cw_ref, wm, bmat_ref, pwbd_ref, psc_ref, segp_ref, g_ref, hext_ref, zext_ref,
                  h_prev_rot, z_prev_rot):
    t = z.shape[0]
    segp = segp_ref[...]
    r = {}
    u, _ = _gelu(z[:, 0:SGU_WIDTH])
    vraw, _ = _gelu(z[:, SGU_WIDTH:2 * SGU_WIDTH])
    xc = vraw - _seg_mean(vraw, segp)
    rstd_v = lax.rsqrt(_seg_mean(xc * xc, segp) + EPS)
    xh_v = xc * rstd_v
    vnb = (xh_v * p384_ref[0:1, :] + p384_ref[1:2, :]).astype(BF16)
    first_head = _lane((t, 128)) < HEAD_DIM
    f_pairs = []
    for pr in range(SGU_HEADS // 2):
        vp = vnb[:, pr * 128:(pr + 1) * 128]
        f_pairs.append(jnp.where(first_head, _dot(wm[2 * pr], vp), _dot(wm[2 * pr + 1], vp)))
    f = jnp.concatenate(f_pairs, axis=1) + bmat_ref[...]
    ya = u * f
    r.update(u=u, xh_v=xh_v, rstd_v=rstd_v, vnb=vnb, f=f)
    o_b = 2 * SGU_WIDTH
    a_in = z[:, o_b:o_b + CONV_WIDTH]
    sig_g = jax.nn.sigmoid(z[:, o_b + CONV_WIDTH:o_b + 2 * CONV_WIDTH])
    hh = a_in * sig_g
    hext_ref[0:t, :], r["h_rot"] = _causal_tail(hh, h_prev_rot)
    hext_ref[t:2 * t, :] = hh
    conv = jnp.zeros((t, CONV_WIDTH), F32) + p384_ref[2:3, :]
    for k in range(CONV_K):
        conv = conv + cw_ref[k:k + 1, :] * hext_ref[pl.ds(t - 8 * (CONV_K - 1 - k), t), :]
    cc = conv - _rowmean(conv)
    rstd_c = lax.rsqrt(_rowmean(cc * cc) + EPS)
    xh_c = cc * rstd_c
    cn = xh_c * p384_ref[3:4, :] + p384_ref[4:5, :]
    sig_c = jax.nn.sigmoid(cn)
    yb = cn * sig_c
    r.update(a_in=a_in, sig_g=sig_g, xh_c=xh_c, rstd_c=rstd_c, cn=cn, sig_c=sig_c)
    o_c = o_b + 2 * CONV_WIDTH
    zc = z[:, o_c:o_c + POOL_WIDTH]
    zext_ref[0:POOL_HALO, :], r["z_rot"] = _causal_tail(zc[t - POOL_HALO:t, :], z_prev_rot)
    zext_ref[POOL_HALO:POOL_HALO + t, :] = zc
    sums, acc = [], zc
    for j in range(1, POOL_WINDOWS[-1]):
        acc = acc + zext_ref[pl.ds(POOL_HALO - 8 * j, t), :]
        if j + 1 in POOL_WINDOWS:
            sums.append(acc)
    inv = _pool_inv_counts(t, tile_idx)
    dpool = _by_pool_group((t, POOL_WIDTH), [s * iv for s, iv in zip(sums, inv)]) - zc
    ycp = _dot(dpool.astype(BF16), pwbd_ref[...])
    yc = ycp * psc_ref[0:1, :]
    r.update(dpool=dpool, ycp=ycp)
    yha, ra = _rms_fwd(ya)
    yhb, rb = _rms_fwd(yb)
    yhc, rc = _rms_fwd(yc)
    bg = g_ref[2:3, :]
    ycat = jnp.concatenate([yha * bg[:, 0:384], yhb * bg[:, 384:768], yhc * bg[:, 768:1024]], axis=1)
    r.update(yha=yha, ra=ra, yhb=yhb, rb=rb, yhc=yhc, rc=rc, ycat=ycat)
    return r


def _mixer_fwd(x, modv, g1024, p384, cw, sguw, bmat, pwbd, psc, segp, win, wout, name="mixer_fwd", job=None,
               natural_x=False):
    s_len = x.shape[0]
    t = MIX_TILE

    def body(x_ref, mod_ref, g_ref, p384_ref, cw_ref, sguw_ref, bmat_ref, pwbd_ref, psc_ref, segp_ref, win_ref, wout_ref,
             x1_ref, z_ref, o_ref, hext_ref, zext_ref, hrot_ref, zrot_ref):
        i = pl.program_id(0)

        @pl.when(i == 0)
        def _():
            hrot_ref[...] = jnp.zeros_like(hrot_ref)
            zrot_ref[...] = jnp.zeros_like(zrot_ref)

        xv = _interleave(x_ref[...]) if natural_x else x_ref[...]
        sh1, sc1, g1 = mod_ref[0:1, :], mod_ref[1:2, :], mod_ref[2:3, :]
        xhat, _ = _rms_fwd(xv)
        h1 = xhat * g_ref[0:1, :] * (1.0 + sc1) + sh1
        z = _dot_nt(h1.astype(BF16), win_ref[...])
        z_ref[...] = z
        r = _branches_fwd(z, i, p384_ref, cw_ref, _masked_sgu_w(sguw_ref), bmat_ref, pwbd_ref, psc_ref, segp_ref,
                          g_ref, hext_ref, zext_ref, hrot_ref[...], zrot_ref[...])
        hrot_ref[...] = r["h_rot"]
        zrot_ref[...] = r["z_rot"]
        o = _dot(r["ycat"].astype(BF16), wout_ref[...])
        o_ref[...] = o
        ohat, _ = _rms_fwd(o)
        x1_ref[...] = xv + g1 * (ohat * g_ref[1:2, :])

    tile = pl.BlockSpec((t, D_MODEL), lambda i: (i, 0))
    consts = (modv, g1024, p384, cw, sguw, bmat, pwbd, psc, segp, win, wout)
    return _pallas_call_with_exchange(
        body,
        grid=(s_len // t,),
        in_specs=[tile] + [_whole(c.shape) for c in consts],
        out_specs=[tile, pl.BlockSpec((t, IN_WIDTH), lambda i: (i, 0)), tile],
        out_shape=[jax.ShapeDtypeStruct((s_len, D_MODEL), F32), jax.ShapeDtypeStruct((s_len, IN_WIDTH), F32),
                   jax.ShapeDtypeStruct((s_len, D_MODEL), F32)],
        scratch_shapes=[pltpu.VMEM((2 * t, CONV_WIDTH), F32), pltpu.VMEM((POOL_HALO + t, POOL_WIDTH), F32),
                        pltpu.VMEM((t, CONV_WIDTH), F32), pltpu.VMEM((POOL_HALO, POOL_WIDTH), F32)],
        operands=(x, *consts),
        name=name, job=job)


def _mixer_bwd(dx1, x, o, z, modv, g1024, p384, cw, sguw, bmat, pwbd, psc, segp, win, wout, name="mixer_bwd", job=None,
               natural_x=False):
    s_len = x.shape[0]
    t = MIX_TILE
    n_tiles = s_len // t

    def body(dx1_ref, x_ref, o_ref, z_ref, zh_ref, mod_ref, g_ref, p384_ref, cw_ref, sguw_ref, bmat_ref, pwbd_ref,
             psc_ref, segp_ref, win_ref, wout_ref,
             dx_ref, dz_ref, do_ref, ycat_ref, h1_ref, vec_ref, v384_ref, dcw_ref, dsguw_ref, dbmat_ref, dpw_ref,
             dpsc_ref, hext_ref, zext_ref, gext_ref, qext_ref, grot_ref, qrot_ref):
        i = pl.program_id(0)
        tile_idx = n_tiles - 1 - i

        @pl.when(i == 0)
        def _():
            for ref in (vec_ref, v384_ref, dcw_ref, dsguw_ref, dbmat_ref, dpw_ref, dpsc_ref, grot_ref, qrot_ref):
                ref[...] = jnp.zeros_like(ref)

        dx1v, ov, z = dx1_ref[...], o_ref[...], z_ref[...]
        xv = _interleave(x_ref[...]) if natural_x else x_ref[...]
        sh1, sc1, g1 = mod_ref[0:1, :], mod_ref[1:2, :], mod_ref[2:3, :]
        pre_g, post_g, bg = g_ref[0:1, :], g_ref[1:2, :], g_ref[2:3, :]
        segp = segp_ref[...]

        ohat, ro = _rms_fwd(ov)
        vec_ref[1:2, :] += _colsum(dx1v * (ohat * post_g))
        don = dx1v * g1
        vec_ref[0:1, :] += _colsum(don * ohat)
        dob = _rms_bwd(don * post_g, ohat, ro).astype(BF16)
        do_ref[...] = dob
        dycat = _dot_nt(dob, wout_ref[...])

        not_first = (tile_idx > 0).astype(F32)
        o_b = 2 * SGU_WIDTH
        o_c = o_b + 2 * CONV_WIDTH
        h_prev = zh_ref[:, o_b:o_b + CONV_WIDTH] * jax.nn.sigmoid(zh_ref[:, o_b + CONV_WIDTH:o_c])
        h_prev_rot = _rot_rows(h_prev, 1) * not_first
        z_prev_rot = _rot_rows(zh_ref[t - POOL_HALO:t, o_c:o_c + POOL_WIDTH], 1) * not_first
        wm = _masked_sgu_w(sguw_ref)
        r = _branches_fwd(z, tile_idx, p384_ref, cw_ref, wm, bmat_ref, pwbd_ref, psc_ref, segp_ref, g_ref,
                          hext_ref, zext_ref, h_prev_rot, z_prev_rot)
        ycat_ref[...] = r["ycat"].astype(BF16)

        def branch_norm_bwd(dyn, yhat, rr, gain):
            return _colsum(dyn * yhat), _rms_bwd(dyn * gain, yhat, rr)

        dga, dya = branch_norm_bwd(dycat[:, 0:384], r["yha"], r["ra"], bg[:, 0:384])
        dgb, dyb = branch_norm_bwd(dycat[:, 384:768], r["yhb"], r["rb"], bg[:, 384:768])
        dgc, dyc = branch_norm_bwd(dycat[:, 768:1024], r["yhc"], r["rc"], bg[:, 768:1024])
        vec_ref[5:6, :] += jnp.concatenate([dga, dgb, dgc], axis=1)

        du_act = dya * r["f"]
        df = dya * r["u"]
        first_head = _lane((t, 128)) < HEAD_DIM
        dbmat_ref[...] += df
        dvn_pairs = []
        for pr in range(SGU_HEADS // 2):
            dfp = df[:, pr * 128:(pr + 1) * 128]
            df0 = jnp.where(first_head, dfp, 0.0).astype(BF16)
            df1 = jnp.where(first_head, 0.0, dfp).astype(BF16)
            vp = r["vnb"][:, pr * 128:(pr + 1) * 128]
            dvn_pairs.append(_dot_tn(wm[2 * pr], df0) + _dot_tn(wm[2 * pr + 1], df1))
            dsguw_ref[2 * pr] += _dot_nt(df0, vp)
            dsguw_ref[2 * pr + 1] += _dot_nt(df1, vp)
        dvn = jnp.concatenate(dvn_pairs, axis=1)
        v384_ref[0:1, :] += _colsum(dvn * r["xh_v"])
        v384_ref[1:2, :] += _colsum(dvn)
        dxh = dvn * p384_ref[0:1, :]
        dvraw = r["rstd_v"] * (dxh - _seg_mean(dxh, segp) - r["xh_v"] * _seg_mean(dxh * r["xh_v"], segp))
        zu, zv = z[:, 0:SGU_WIDTH], z[:, SGU_WIDTH:o_b]
        _, tu = _gelu(zu)
        _, tv = _gelu(zv)
        dz_u = du_act * _gelu_grad(zu, tu)
        dz_v = dvraw * _gelu_grad(zv, tv)

        cn, sig_c = r["cn"], r["sig_c"]
        dcn = dyb * (sig_c * (1.0 + cn * (1.0 - sig_c)))
        v384_ref[3:4, :] += _colsum(dcn * r["xh_c"])
        v384_ref[4:5, :] += _colsum(dcn)
        dxc = dcn * p384_ref[3:4, :]
        gconv = r["rstd_c"] * (dxc - _rowmean(dxc) - r["xh_c"] * _rowmean(dxc * r["xh_c"]))
        v384_ref[2:3, :] += _colsum(gconv)
        gext_ref[0:t, :] = gconv
        gext_ref[t:2 * t, :], grot_ref[...] = _lookahead_head(gconv, grot_ref[...])
        dhh = jnp.zeros((t, CONV_WIDTH), F32)
        for k in range(CONV_K):
            shift = CONV_K - 1 - k
            dcw_ref[k:k + 1, :] += _colsum(gconv * hext_ref[pl.ds(t - 8 * shift, t), :])
            dhh = dhh + cw_ref[k:k + 1, :] * gext_ref[pl.ds(8 * shift, t), :]
        sig_g = r["sig_g"]
        dz_a = dhh * sig_g
        dz_g = dhh * r["a_in"] * sig_g * (1.0 - sig_g)

        dpsc_ref[0:1, :] += _colsum(dyc * r["ycp"])
        dycp = (dyc * psc_ref[0:1, :]).astype(BF16)
        dpw_ref[...] += _dot_tn(r["dpool"].astype(BF16), dycp)
        ddp = _dot_nt(dycp, pwbd_ref[...])
        inv = _pool_inv_counts(t, tile_idx)
        q = ddp * _by_pool_group((t, POOL_WIDTH), inv)
        qext_ref[0:t, :] = q
        qext_ref[t:t + POOL_HALO, :], qrot_ref[...] = _lookahead_head(q[0:POOL_HALO, :], qrot_ref[...])
        sums, acc = [], q
        for j in range(1, POOL_WINDOWS[-1]):
            acc = acc + qext_ref[pl.ds(8 * j, t), :]
            if j + 1 in POOL_WINDOWS:
                sums.append(acc)
        dz_c = _by_pool_group((t, POOL_WIDTH), sums) - ddp

        dzb = jnp.concatenate([dz_u, dz_v, dz_a, dz_g, dz_c], axis=1).astype(BF16)
        dz_ref[...] = dzb
        dh1 = _dot(dzb, win_ref[...])

        xhat, rx = _rms_fwd(xv)
        xn = xhat * pre_g
        h1_ref[...] = (xn * (1.0 + sc1) + sh1).astype(BF16)
        vec_ref[2:3, :] += _colsum(dh1)
        vec_ref[3:4, :] += _colsum(dh1 * xn)
        dxn = dh1 * (1.0 + sc1)
        vec_ref[4:5, :] += _colsum(dxn * xhat)
        dx = dx1v + _rms_bwd(dxn * pre_g, xhat, rx)
        dx_ref[...] = _deinterleave(dx) if natural_x else dx

        @pl.when(i == n_tiles - 1)
        def _():
            keep = _sgu_keep_mask(t)
            for h in range(SGU_HEADS):
                dsguw_ref[h] = jnp.where(keep, dsguw_ref[h], 0.0)
            dbmat_ref[...] = float(HEAD_DIM) * _seg_mean(dbmat_ref[...], segp)

    rev = lambda i: (n_tiles - 1 - i, 0)
    tile = pl.BlockSpec((t, D_MODEL), rev)
    ztile = pl.BlockSpec((t, IN_WIDTH), rev)
    zhalo = pl.BlockSpec((t, IN_WIDTH), lambda i: (jnp.maximum(n_tiles - 2 - i, 0), 0))
    consts = (modv, g1024, p384, cw, sguw, bmat, pwbd, psc, segp, win, wout)
    acc = lambda shape: pl.BlockSpec(shape, lambda i: (0,) * len(shape))
    acc_shapes = [(8, D_MODEL), (8, SGU_WIDTH), (32, CONV_WIDTH), (SGU_HEADS, t, t), (t, SGU_WIDTH),
                  (POOL_WIDTH, POOL_WIDTH), (8, POOL_WIDTH)]
    return _pallas_call_with_exchange(
        body,
        grid=(n_tiles,),
        in_specs=[tile, tile, tile, ztile, zhalo] + [_whole(c.shape) for c in consts],
        out_specs=[tile, ztile, tile, tile, tile] + [acc(s) for s in acc_shapes],
        out_shape=[jax.ShapeDtypeStruct((s_len, D_MODEL), F32), jax.ShapeDtypeStruct((s_len, IN_WIDTH), BF16),
                   jax.ShapeDtypeStruct((s_len, D_MODEL), BF16), jax.ShapeDtypeStruct((s_len, D_MODEL), BF16),
                   jax.ShapeDtypeStruct((s_len, D_MODEL), BF16)] + [jax.ShapeDtypeStruct(s, F32) for s in acc_shapes],
        scratch_shapes=[pltpu.VMEM((2 * t, CONV_WIDTH), F32), pltpu.VMEM((POOL_HALO + t, POOL_WIDTH), F32),
                        pltpu.VMEM((2 * t, CONV_WIDTH), F32), pltpu.VMEM((t + POOL_HALO, POOL_WIDTH), F32),
                        pltpu.VMEM((t, CONV_WIDTH), F32), pltpu.VMEM((POOL_HALO, POOL_WIDTH), F32)],
        operands=(dx1, x, o, z, z, *consts),
        name=name, job=job)


def _loss_grad(xo, target):
    s_len = xo.shape[0]
    t = TILE

    def body(x_ref, t_ref, dx_ref, loss_ref):
        i = pl.program_id(0)

        @pl.when(i == 0)
        def _():
            loss_ref[...] = jnp.zeros_like(loss_ref)

        diff = x_ref[...] - _interleave(t_ref[...])
        dx_ref[...] = diff * (1.0 / D_MODEL)
        loss_ref[...] += (0.5 / D_MODEL) * jnp.sum(diff * diff)

    tile = pl.BlockSpec((t, D_MODEL), lambda i: (i, 0))
    return pl.pallas_call(
        body,
        grid=(s_len // t,),
        in_specs=[tile, tile],
        out_specs=[tile, pl.BlockSpec((8, 128), lambda i: (0, 0))],
        out_shape=[jax.ShapeDtypeStruct((s_len, D_MODEL), F32), jax.ShapeDtypeStruct((8, 128), F32)],
        compiler_params=_cparams(1),
        name="loss_grad",
    )(xo, target)


MOD_SHARD = 6 * D_MODEL // N_DEV

ROW_DMOD = 0
ROW_G1024 = 8
ROW_V384 = 16
ROW_SGU_B = 24
ROW_POOL_SCALE = 25
ROW_CONV_W = 32
ROW_FFN_CONV = 64
ROW_POOL_W = 96
ROW_SGU_W = 112
ROWS_PER_LAYER = 208
N_LAYERS = 2


def _gather_weights(c8, mod_w, mod_b8, job):
    kinds = [kind for kind, _ in job]
    shards = [a for _, a in job]
    n = len(shards)

    def body(c_ref, modw_ref, modb_ref, *rest):
        shard_refs = rest[:n]
        sc_all_ref, modrows_ref = rest[n], rest[n + 1]
        full_refs = rest[n + 2:2 * n + 2]
        send_buf, mod_recv, w_send, w_recv, w_local, sc_send, sc_recv, mod_send, mod_recv_sem = rest[2 * n + 2:]
        pos = _my_pos()
        me = _flat(pos)
        peers = [_peer(pos, k) for k in range(1, N_DEV)]

        w_copies = _exchange_copies(kinds, shard_refs, full_refs, w_send, w_recv, w_local, phase=0)
        for cp in w_copies:
            cp.start()

        cv = c_ref[...]
        sc_all_ref[me] = cv * jax.nn.sigmoid(cv)
        sc_copies = [_remote_copy(sc_all_ref.at[me], sc_all_ref.at[me], sc_send.at[k], sc_recv.at[k], peers[k])
                     for k in range(N_PEERS)]
        for cp in sc_copies:
            cp.start()
        for cp in sc_copies:
            cp.wait()

        sc = jnp.concatenate([sc_all_ref[j, 0:1, :] for j in range(N_DEV)], axis=0)
        send_buf[...] = jnp.zeros_like(send_buf)
        for l in range(N_LAYERS):
            part = jnp.dot(sc, modw_ref[l], precision=lax.Precision.HIGHEST, preferred_element_type=F32)
            for j in range(N_DEV):
                send_buf[j, l:l + 1, :] = part[j:j + 1, :]
        mod_recv[me] = send_buf[me]
        mod_copies = [_remote_copy(send_buf.at[_flat(peers[k])], mod_recv.at[me], mod_send.at[k], mod_recv_sem.at[k],
                                   peers[k]) for k in range(N_PEERS)]
        for cp in mod_copies:
            cp.start()
        for cp in mod_copies:
            cp.wait()
        modrows_ref[...] = jnp.zeros_like(modrows_ref)
        for l in range(N_LAYERS):
            row = jnp.concatenate([mod_recv[j, l:l + 1, :] for j in range(N_DEV)], axis=1)
            modrows_ref[l:l + 1, :] = row + modb_ref[l:l + 1, :]

        for cp in w_copies:
            cp.wait()
        relays = _exchange_copies(kinds, shard_refs, full_refs, w_send, w_recv, w_local, phase=1)
        for cp in relays:
            cp.start()
        for cp in relays:
            cp.wait()

    out_shape = ([jax.ShapeDtypeStruct((N_DEV, 8, D_MODEL), F32), jax.ShapeDtypeStruct((8, 6 * D_MODEL), F32)]
                 + [jax.ShapeDtypeStruct((N_DEV,) + s.shape, s.dtype) for s in shards])
    return pl.pallas_call(
        body,
        in_specs=[VMEM, VMEM, VMEM] + [ANY] * n,
        out_specs=[VMEM, VMEM] + [ANY] * n,
        out_shape=out_shape,
        scratch_shapes=[pltpu.VMEM((N_DEV, 8, MOD_SHARD), F32), pltpu.VMEM((N_DEV, 8, MOD_SHARD), F32),
                        pltpu.SemaphoreType.DMA((n, N_PEERS)), pltpu.SemaphoreType.DMA((n, N_PEERS)),
                        pltpu.SemaphoreType.DMA((n,)),
                        pltpu.SemaphoreType.DMA((N_PEERS,)), pltpu.SemaphoreType.DMA((N_PEERS,)),
                        pltpu.SemaphoreType.DMA((N_PEERS,)), pltpu.SemaphoreType.DMA((N_PEERS,))],
        compiler_params=pltpu.CompilerParams(vmem_limit_bytes=VMEM_LIMIT_BYTES),
        name="gather_weights",
    )(c8, mod_w, mod_b8, *shards)


def _small_sums(sc_all, small_all0, small_all1, job):
    kinds = [kind for kind, _ in job]
    n = len(job)

    def body(sc_all_ref, small_all0_ref, small_all1_ref, *rest):
        small_sum_ref, gmodw_ref = rest[n], rest[n + 1]
        copies = _exchange_copies(kinds, rest[:n], rest[n + 2:2 * n + 2], *rest[2 * n + 2:], phase=0)
        for cp in copies:
            cp.start()
        me = _flat(_my_pos())
        sc = jnp.concatenate([sc_all_ref[j, 0:1, :] for j in range(N_DEV)], axis=0)
        mine = lax.broadcasted_iota(jnp.int32, (2 * N_DEV, MOD_SHARD), 0) == me
        for l, parts in enumerate((small_all0_ref, small_all1_ref)):
            total = parts[0].astype(F32)
            for j in range(1, N_DEV):
                total = total + parts[j].astype(F32)
            small_sum_ref[l] = total
            dm = jnp.concatenate(
                [jnp.sum(jnp.where(mine, parts[j, ROW_DMOD:ROW_DMOD + 2 * N_DEV, 0:MOD_SHARD].astype(F32), 0.0),
                         axis=0, keepdims=True) for j in range(N_DEV)], axis=0)
            gmodw_ref[l] = lax.dot_general(sc, dm, (((0,), (0,)), ((), ())), precision=lax.Precision.HIGHEST,
                                           preferred_element_type=F32)
        for cp in copies:
            cp.wait()

    res = pl.pallas_call(
        body,
        in_specs=[VMEM, VMEM, VMEM] + [ANY] * n,
        out_specs=[VMEM, VMEM] + [ANY] * n,
        out_shape=[jax.ShapeDtypeStruct((N_LAYERS, ROWS_PER_LAYER, D_MODEL), F32),
                   jax.ShapeDtypeStruct((N_LAYERS, D_MODEL, MOD_SHARD), F32)] + _exchange_out_shapes(job),
        scratch_shapes=_exchange_sems(n),
        compiler_params=pltpu.CompilerParams(vmem_limit_bytes=VMEM_LIMIT_BYTES),
        name="small_sums",
    )(sc_all, small_all0, small_all1, *[a for _, a in job])
    return res[0], res[1], res[2:]


def _adam_update(g, w, m, v):
    m2 = ADAM_B1 * m + (1.0 - ADAM_B1) * g
    v2 = ADAM_B2 * v + (1.0 - ADAM_B2) * (g * g)
    m_hat = m2 / (1.0 - ADAM_B1 ** ADAM_STEP)
    v_hat = v2 / (1.0 - ADAM_B2 ** ADAM_STEP)
    delta = -ADAM_LR * (m_hat / (jnp.sqrt(v_hat) + ADAM_EPS) + ADAM_WD * w)
    return delta, m2, v2


def _pair_add(g, r1, row_chunk, name):
    _, rows, cols = g.shape
    core = lax.axis_index("c").astype(jnp.int32).reshape(1)

    def body(core_ref, g_ref, r_ref, o_ref):
        o_ref[0] = (g_ref[0, 0].astype(F32) + r_ref[0].astype(F32)).astype(BF16)

    blk = pl.BlockSpec((1, row_chunk, cols), lambda q, i, core_ref: (q, i, 0))
    grid_spec = pltpu.PrefetchScalarGridSpec(
        num_scalar_prefetch=1, grid=(N_CHIPS, rows // row_chunk),
        in_specs=[pl.BlockSpec((1, 1, row_chunk, cols), lambda q, i, core_ref: (q, core_ref[0], i, 0)), blk],
        out_specs=blk)
    return pl.pallas_call(
        body, grid_spec=grid_spec, out_shape=jax.ShapeDtypeStruct((N_CHIPS, rows, cols), BF16),
        compiler_params=_cparams(2), name=name,
    )(core, g.reshape(N_CHIPS, 2, rows, cols), r1)


def _adam_sharded(recv0, recv1, w, m, v, row_chunk, name):
    _, rows, cols = w.shape
    n_chunks = rows // row_chunk

    def body(r0_ref, r1_ref, w_ref, m_ref, v_ref, g_ref, d_ref, m2_ref, v2_ref):
        layer = pl.program_id(0)

        def run(r_ref):
            g = r_ref[0].astype(F32)
            for j in range(1, r_ref.shape[0]):
                g = g + r_ref[j].astype(F32)
            delta, m2, v2 = _adam_update(g, w_ref[0], m_ref[0], v_ref[0])
            g_ref[0], d_ref[0], m2_ref[0], v2_ref[0] = g, delta, m2, v2

        @pl.when(layer == 0)
        def _():
            run(r0_ref)

        @pl.when(layer == 1)
        def _():
            run(r1_ref)

    r0_spec = pl.BlockSpec((recv0.shape[0], row_chunk, cols), lambda l, i: (0, i * (1 - l) + (n_chunks - 1) * l, 0))
    r1_spec = pl.BlockSpec((recv1.shape[0], row_chunk, cols), lambda l, i: (0, i * l, 0))
    blk = pl.BlockSpec((1, row_chunk, cols), lambda l, i: (l, i, 0))
    out = jax.ShapeDtypeStruct(w.shape, F32)
    return pl.pallas_call(
        body,
        grid=(N_LAYERS, n_chunks),
        in_specs=[r0_spec, r1_spec, blk, blk, blk],
        out_specs=[blk] * 4,
        out_shape=[out] * 4,
        compiler_params=_cparams(2),
        name=name,
    )(recv0, recv1, w, m, v)


def _adam_dense(g, w, m, v, row_chunk, name):
    n_lead, rows, cols = w.shape

    def body(g_ref, w_ref, m_ref, v_ref, go_ref, d_ref, m2_ref, v2_ref):
        gv = g_ref[...]
        go_ref[...] = gv
        d_ref[...], m2_ref[...], v2_ref[...] = _adam_update(gv, w_ref[...], m_ref[...], v_ref[...])

    blk = pl.BlockSpec((1, row_chunk, cols), lambda l, i: (l, i, 0))
    out = jax.ShapeDtypeStruct(w.shape, F32)
    return pl.pallas_call(
        body,
        grid=(n_lead, rows // row_chunk),
        in_specs=[blk] * 4,
        out_specs=[blk] * 4,
        out_shape=[out] * 4,
        compiler_params=_cparams(2),
        name=name,
    )(g, w, m, v)


WEIGHT_NAMES = ("mod_w", "mod_b", "mix_pre_g", "mix_post_g", "w_in", "sgu_norm_g", "sgu_norm_b", "sgu_w", "sgu_b",
                "conv_w", "conv_b", "conv_norm_g", "conv_norm_b", "pool_w", "pool_scale", "branch_g", "w_out",
                "ffn_pre_g", "ffn_post_g", "ffn_up", "ffn_conv_w", "ffn_conv_b", "ffn_down")
SHARDED_BIG = ("w_in", "w_out", "ffn_up", "ffn_down")
SMALL_PACKED = tuple(n for n in WEIGHT_NAMES if n not in SHARDED_BIG + ("mod_w",))


def _rows8(rows, width=D_MODEL):
    out = [jnp.pad(r.astype(F32), (0, width - r.shape[0]))[None] for r in rows]
    out.append(jnp.zeros((8 - len(rows), width), F32))
    return jnp.concatenate(out, axis=0)


def _as_rows(a, width=D_MODEL):
    flat = a.astype(F32).reshape(-1)
    pad = (-flat.shape[0]) % width
    return jnp.pad(flat, (0, pad)).reshape(-1, width)


def _pad_cols(a, width=D_MODEL):
    return jnp.pad(a.astype(F32), ((0, 0), (0, width - a.shape[1])))


def _pack_rows(arrays):
    rows = jnp.concatenate([_as_rows(a) for a in arrays], axis=0)
    return jnp.pad(rows, ((0, (-rows.shape[0]) % 8), (0, 0)))


def _unpack_rows(packed, shapes):
    out, r = [], 0
    for shape in shapes:
        size = math.prod(shape)
        n_rows = -(-size // D_MODEL)
        out.append(packed[r:r + n_rows].reshape(-1)[:size].reshape(shape))
        r += n_rows
    return out


def _interleave_axis(a, axis, inverse=False):
    shape = a.shape
    t = shape[axis]
    split = (t // 8, 8) if inverse else (8, t // 8)
    a = a.reshape(shape[:axis] + split + shape[axis + 1:])
    return jnp.swapaxes(a, axis, axis + 1).reshape(shape)


def _layer_consts(l, w, mod_rows, win, wout, conv_w_full):
    modv = _rows8(list(mod_rows[l].reshape(6, D_MODEL)))
    g1024 = _rows8([w["mix_pre_g"][l], w["mix_post_g"][l], w["branch_g"][l], w["ffn_pre_g"][l], w["ffn_post_g"][l]])
    p384 = _rows8([w["sgu_norm_g"][l], w["sgu_norm_b"][l], w["conv_b"][l], w["conv_norm_g"][l], w["conv_norm_b"][l]],
                  SGU_WIDTH)
    cw = jnp.pad(conv_w_full[l], ((0, 32 - CONV_K), (0, 0)))
    reps = MIX_TILE // CHUNK
    sguw = _interleave_axis(_interleave_axis(jnp.tile(w["sgu_w"][l], (1, reps, reps)), 1), 2)
    bmat = _interleave_axis(jnp.tile(jnp.repeat(w["sgu_b"][l].T, HEAD_DIM, axis=1), (reps, 1)), 0)
    groups = len(POOL_WINDOWS)
    eye = jnp.eye(groups, dtype=F32)
    pwbd = (eye[:, None, :, None] * w["pool_w"][l][:, :, None, :]).reshape(POOL_WIDTH, POOL_WIDTH).astype(BF16)
    psc = _rows8([w["pool_scale"][l]], POOL_WIDTH)
    seg = jnp.arange(SGU_WIDTH) // HEAD_DIM
    segp = jnp.where(seg[:, None] == seg[None, :], 1.0 / HEAD_DIM, 0.0).astype(BF16)
    return modv, g1024, (modv, g1024, p384, cw, sguw, bmat, pwbd, psc, segp, win, wout)


def _small_grad_rows(mix, ffn):
    _, _, _, _, _, mvec, v384, dcw, dsguw, dbmat, dpw, dpsc = mix
    fvec, cgrad = ffn[5], ffn[6]
    dmod = jnp.stack([mvec[2], mvec[3], mvec[1], fvec[2], fvec[3], fvec[1]]).reshape(N_DEV, MOD_SHARD)
    g_rows = jnp.stack([mvec[4], mvec[0], mvec[5], fvec[4], fvec[0]])
    reps = MIX_TILE // CHUNK
    dbmat = _interleave_axis(dbmat, 0, inverse=True).reshape(reps, CHUNK, SGU_WIDTH).sum(axis=0)
    dsguw = _interleave_axis(_interleave_axis(dsguw, 1, inverse=True), 2, inverse=True)
    dsguw = dsguw.reshape(SGU_HEADS, reps, CHUNK, reps, CHUNK)
    dsguw = sum(dsguw[:, b, :, b, :] for b in range(reps))
    dsgu_b = dbmat[:, ::HEAD_DIM].T.reshape(1, SGU_HEADS * CHUNK)
    groups = len(POOL_WINDOWS)
    gdim = POOL_WIDTH // groups
    dpw4 = dpw.reshape(groups, gdim, groups, gdim)
    dpool = jnp.stack([dpw4[g, :, g, :] for g in range(groups)])
    blocks = [_pad_cols(dmod), _rows8(list(g_rows)), _pad_cols(v384), _rows8([dsgu_b[0], dpsc[0]]), _pad_cols(dcw),
              _pad_cols(cgrad[:, 0:4, :].reshape(4 * N_DEV, FF_SHARD)), _as_rows(dpool), _as_rows(dsguw)]
    return jnp.concatenate(blocks, axis=0)


def _small_grads_from_rows(total):
    per = {n: [] for n in SMALL_PACKED}
    for l in range(N_LAYERS):
        s = total[l]
        per["mod_b"].append(s[ROW_DMOD:ROW_DMOD + N_DEV, :MOD_SHARD].reshape(6 * D_MODEL))
        for j, name in enumerate(("mix_pre_g", "mix_post_g", "branch_g", "ffn_pre_g", "ffn_post_g")):
            per[name].append(s[ROW_G1024 + j])
        for j, name in enumerate(("sgu_norm_g", "sgu_norm_b", "conv_b", "conv_norm_g", "conv_norm_b")):
            per[name].append(s[ROW_V384 + j, :SGU_WIDTH])
        per["sgu_b"].append(s[ROW_SGU_B, :SGU_HEADS * CHUNK].reshape(SGU_HEADS, CHUNK))
        per["pool_scale"].append(s[ROW_POOL_SCALE, :POOL_WIDTH])
        per["conv_w"].append(s[ROW_CONV_W:ROW_CONV_W + CONV_K, :CONV_WIDTH])
        fc = s[ROW_FFN_CONV:ROW_FFN_CONV + 4 * N_DEV, :FF_SHARD].reshape(N_DEV, 4, FF_SHARD)
        per["ffn_conv_w"].append(fc[:, 0:3, :].transpose(1, 0, 2).reshape(FFN_CONV_K, 2 * D_FF))
        per["ffn_conv_b"].append(fc[:, 3, :].reshape(2 * D_FF))
        per["pool_w"].append(s[ROW_POOL_W:ROW_POOL_W + 16].reshape(len(POOL_WINDOWS), HEAD_DIM, HEAD_DIM))
        per["sgu_w"].append(s[ROW_SGU_W:ROW_SGU_W + 96].reshape(SGU_HEADS, CHUNK, CHUNK))
    return {n: jnp.stack(v) for n, v in per.items()}


def kernel(x, c, mod_w, mod_b, mix_pre_g, mix_post_g, w_in, sgu_norm_g, sgu_norm_b, sgu_w, sgu_b, conv_w, conv_b, conv_norm_g, conv_norm_b, pool_w, pool_scale, branch_g, w_out, ffn_pre_g, ffn_post_g, ffn_up, ffn_conv_w, ffn_conv_b, ffn_down, loss_target, m_mod_w, m_mod_b, m_mix_pre_g, m_mix_post_g, m_w_in, m_sgu_norm_g, m_sgu_norm_b, m_sgu_w, m_sgu_b, m_conv_w, m_conv_b, m_conv_norm_g, m_conv_norm_b, m_pool_w, m_pool_scale, m_branch_g, m_w_out, m_ffn_pre_g, m_ffn_post_g, m_ffn_up, m_ffn_conv_w, m_ffn_conv_b, m_ffn_down, v_mod_w, v_mod_b, v_mix_pre_g, v_mix_post_g, v_w_in, v_sgu_norm_g, v_sgu_norm_b, v_sgu_w, v_sgu_b, v_conv_w, v_conv_b, v_conv_norm_g, v_conv_norm_b, v_pool_w, v_pool_scale, v_branch_g, v_w_out, v_ffn_pre_g, v_ffn_post_g, v_ffn_up, v_ffn_conv_w, v_ffn_conv_b, v_ffn_down):
    w = dict(zip(WEIGHT_NAMES, (mod_w, mod_b, mix_pre_g, mix_post_g, w_in, sgu_norm_g, sgu_norm_b, sgu_w, sgu_b, conv_w,
                                conv_b, conv_norm_g, conv_norm_b, pool_w, pool_scale, branch_g, w_out, ffn_pre_g,
                                ffn_post_g, ffn_up, ffn_conv_w, ffn_conv_b, ffn_down)))
    m = dict(zip(WEIGHT_NAMES, (m_mod_w, m_mod_b, m_mix_pre_g, m_mix_post_g, m_w_in, m_sgu_norm_g, m_sgu_norm_b, m_sgu_w,
                                m_sgu_b, m_conv_w, m_conv_b, m_conv_norm_g, m_conv_norm_b, m_pool_w, m_pool_scale,
                                m_branch_g, m_w_out, m_ffn_pre_g, m_ffn_post_g, m_ffn_up, m_ffn_conv_w, m_ffn_conv_b,
                                m_ffn_down)))
    v = dict(zip(WEIGHT_NAMES, (v_mod_w, v_mod_b, v_mix_pre_g, v_mix_post_g, v_w_in, v_sgu_norm_g, v_sgu_norm_b, v_sgu_w,
                                v_sgu_b, v_conv_w, v_conv_b, v_conv_norm_g, v_conv_norm_b, v_pool_w, v_pool_scale,
                                v_branch_g, v_w_out, v_ffn_pre_g, v_ffn_post_g, v_ffn_up, v_ffn_conv_w, v_ffn_conv_b,
                                v_ffn_down)))
    me = _flat(_my_pos())
    xs = x[0]
    s_len = xs.shape[0]

    transposed = ("w_in", "ffn_up")
    wt = {n: jnp.swapaxes(w[n], 1, 2) if n in transposed else w[n] for n in SHARDED_BIG}
    mt = {n: jnp.swapaxes(m[n], 1, 2) if n in transposed else m[n] for n in SHARDED_BIG}
    vt = {n: jnp.swapaxes(v[n], 1, 2) if n in transposed else v[n] for n in SHARDED_BIG}
    bf16_shards = [[wt[n][l].astype(BF16) for n in SHARDED_BIG] for l in range(N_LAYERS)]

    def mixer_operands(l, win_g, wout_g):
        win = win_g.reshape(IN_WIDTH, D_MODEL)
        return _layer_consts(l, w, mod_rows, win, wout_g.reshape(D_MODEL, D_MODEL), conv_w_full)

    def ffn_operands(l, modv, g1024, wup_g, wdn_g):
        wdn = wdn_g.reshape(FF_PAIRS, FF_SHARD, D_MODEL)
        return modv, g1024, wup_g, wdn, ffn_cw_full[:, l], ffn_conv_b[l].reshape(N_DEV, 1, FF_SHARD)

    w0, w1 = bf16_shards
    c8 = jnp.broadcast_to(c, (8, D_MODEL))
    mod_b8 = jnp.pad(mod_b, ((0, 8 - N_LAYERS), (0, 0)))
    sc_all, mod_rows, win0_g, wout0_g, conv_w_g, ffn_cw_full = _gather_weights(
        c8, mod_w, mod_b8, [("gather2", w0[0]), ("gather2", w0[1]), ("gather", conv_w), ("gather", ffn_conv_w)])
    conv_w_full = conv_w_g.transpose(1, 2, 0, 3).reshape(N_LAYERS, CONV_K, CONV_WIDTH)

    modv0, g0, mix_consts0 = mixer_operands(0, win0_g, wout0_g)
    (x1, z, o), (wup0_g, wdn0_g) = _mixer_fwd(xs, *mix_consts0, name="mixer_fwd_l0", natural_x=True,
                                              job=[("gather2", w0[2]), ("gather2", w0[3])])
    ffn_consts0 = ffn_operands(0, modv0, g0, wup0_g, wdn0_g)
    (x2, y2, p), (win1_g, wout1_g, wdn1_g) = _ffn_fwd(x1, *ffn_consts0, name="ffn_fwd_l0",
                                                      job=[("gather2", w1[0]), ("gather2", w1[1]), ("gather2", w1[3])])
    saved = [(xs, z, o, x1, y2, p)]
    modv1, g1, mix_consts1 = mixer_operands(1, win1_g, wout1_g)
    (x1, z, o), (wup1_g,) = _mixer_fwd(x2, *mix_consts1, name="mixer_fwd_l1", job=[("gather2", w1[2])])
    ffn_consts1 = ffn_operands(1, modv1, g1, wup1_g, wdn1_g)
    (x3, y2, p), _ = _ffn_fwd(x1, *ffn_consts1, name="ffn_fwd_l1")
    saved.append((x2, z, o, x1, y2, p))
    dh, loss_tile = _loss_grad(x3, loss_target[0])
    loss = lax.psum(loss_tile[0, 0], ("x", "y", "c"))

    def ffn_weight_grads(l, ffn):
        dp, a, dy2, h2 = ffn[1:5]
        d_up = _wgrad(dp, h2, f"wgrad_ffn_up_l{l}", tk=WGRAD_TK_FFN)
        d_dn = _wgrad(a, dy2, f"wgrad_ffn_down_l{l}", tk=WGRAD_TK_FFN).reshape(N_DEV, D_FF // N_DEV, D_MODEL)
        return d_up, d_dn

    def mixer_weight_grads(l, mix):
        dz, do, ycat, h1 = mix[1:5]
        d_in = _wgrad(dz[None], h1, f"wgrad_w_in_l{l}").reshape(N_DEV, IN_WIDTH // N_DEV, D_MODEL)
        d_out = _wgrad(ycat, do[None], f"wgrad_w_out_l{l}").reshape(N_DEV, D_MODEL // N_DEV, D_MODEL)
        return d_in, d_out

    x_in, z, o, x1, y2, p = saved[1]
    ffn1, _ = _ffn_bwd(dh, x1, y2, p, *ffn_consts1, name="ffn_bwd_l1")
    d_up1, d_dn1 = ffn_weight_grads(1, ffn1)
    mix1, (sib_up1, sib_dn1) = _mixer_bwd(ffn1[0], x_in, o, z, *mix_consts1, name="mixer_bwd_l1",
                                          job=[("scatter_p1", d_up1), ("scatter_p1", d_dn1)])
    chip_up1 = _pair_add(d_up1, sib_up1, 176, "pair_add_ffn_up_l1")
    chip_dn1 = _pair_add(d_dn1, sib_dn1, 176, "pair_add_ffn_down_l1")
    d_in1, d_out1 = mixer_weight_grads(1, mix1)
    small1 = _small_grad_rows(mix1, ffn1).astype(BF16)

    x_in, z, o, x1, y2, p = saved[0]
    ffn0, job_out = _ffn_bwd(mix1[0], x1, y2, p, *ffn_consts0, name="ffn_bwd_l0",
                             job=[("scatter", d_in1), ("scatter", d_out1), ("scatter_p2", chip_up1),
                                  ("scatter_p2", chip_dn1), ("gather", small1)])
    recv1, small_all1 = job_out[0:4], job_out[4]
    d_up0, d_dn0 = ffn_weight_grads(0, ffn0)
    mix0, recv_ffn0 = _mixer_bwd(ffn0[0], x_in, o, z, *mix_consts0, name="mixer_bwd_l0", natural_x=True,
                                 job=[("scatter", d_up0), ("scatter", d_dn0)])
    grad_x = mix0[0][None]
    dz, do, ycat, h1 = mix0[1:5]
    small0 = _small_grad_rows(mix0, ffn0).astype(BF16)
    d_in0, (small_all0,) = _wgrad(dz[None], h1, "wgrad_w_in_l0", job=[("gather", small0)])
    d_in0 = d_in0.reshape(N_DEV, IN_WIDTH // N_DEV, D_MODEL)
    d_out0, (recv_in0,) = _wgrad(ycat, do[None], "wgrad_w_out_l0", job=[("scatter", d_in0)])
    d_out0 = d_out0.reshape(N_DEV, D_MODEL // N_DEV, D_MODEL)
    small_total, g_mod_w, (recv_out0,) = _small_sums(sc_all, small_all0, small_all1, [("scatter", d_out0)])
    recv0 = [recv_in0, recv_out0, recv_ffn0[0], recv_ffn0[1]]

    grads, deltas, new_m, new_v = {}, {}, {}, {}
    for j, (name, chunk) in enumerate((("w_in", 224), ("w_out", 128), ("ffn_up", 176), ("ffn_down", 176))):
        outs = _adam_sharded(recv0[j], recv1[j], wt[name], mt[name], vt[name], chunk, "adam_" + name)
        if name in transposed:
            outs = [jnp.swapaxes(t, 1, 2) for t in outs]
        grads[name], deltas[name], new_m[name], new_v[name] = outs
    grads["mod_w"], deltas["mod_w"], new_m["mod_w"], new_v["mod_w"] = _adam_dense(
        g_mod_w, mod_w, m_mod_w, v_mod_w, 256, "adam_mod_w")

    small_g = _small_grads_from_rows(small_total)
    small_g["conv_w"] = lax.dynamic_slice_in_dim(small_g["conv_w"], me * conv_w.shape[2], conv_w.shape[2], axis=2)
    small_g["ffn_conv_w"] = lax.dynamic_slice_in_dim(small_g["ffn_conv_w"], me * FF_SHARD, FF_SHARD, axis=2)
    shapes = [w[n].shape for n in SMALL_PACKED]
    packs = [_pack_rows([src[n] for n in SMALL_PACKED])[None] for src in (small_g, w, m, v)]
    _, d, m2, v2 = _adam_dense(*packs, packs[0].shape[1], "adam_small")
    for name, dd, mm, vv in zip(SMALL_PACKED, _unpack_rows(d[0], shapes), _unpack_rows(m2[0], shapes),
                                _unpack_rows(v2[0], shapes)):
        grads[name], deltas[name], new_m[name], new_v[name] = small_g[name], dd, mm, vv

    return (loss, grad_x, *[grads[n] for n in WEIGHT_NAMES], *[deltas[n] for n in WEIGHT_NAMES],
            *[new_m[n] for n in WEIGHT_NAMES], *[new_v[n] for n in WEIGHT_NAMES])
```

```python
import functools
import math

import jax
import jax.numpy as jnp
from jax import lax
from jax.experimental import pallas as pl
from jax.experimental.pallas import tpu as pltpu

F32 = jnp.float32
BF16 = jnp.bfloat16

D_MODEL = 1024
N_DEV = 8
SGU_WIDTH = 384
CONV_WIDTH = 384
POOL_WIDTH = 256
HEAD_DIM = 64
SGU_HEADS = 6
CHUNK = 128
CONV_K = 31
POOL_WINDOWS = (2, 4, 8, 16)
IN_WIDTH = 1792
D_FF = 2816
FF_SHARD = 2 * D_FF // N_DEV
FF_PAIRS = N_DEV // 2
FFN_CONV_K = 3
EPS = 1e-6
GELU_C0 = math.sqrt(2.0 / math.pi)
GELU_C1 = 0.044715

ADAM_LR = 0.001
ADAM_B1 = 0.9
ADAM_B2 = 0.999
ADAM_EPS = 1e-08
ADAM_WD = 0.01
ADAM_STEP = 10

VMEM_LIMIT_BYTES = 56 * 1024 * 1024
TILE = 256
MIX_TILE = TILE
FFN_TILE = TILE
FFN_HALO = 8 * (FFN_CONV_K - 1)
POOL_HALO = 8 * POOL_WINDOWS[-1]
WGRAD_TK = 2048
WGRAD_TK_FFN = 4096


def _cparams(n_axes):
    return pltpu.CompilerParams(dimension_semantics=("arbitrary",) * n_axes, vmem_limit_bytes=VMEM_LIMIT_BYTES)


def _whole(shape):
    nd = len(shape)
    return pl.BlockSpec(shape, lambda *_: (0,) * nd, pipeline_mode=pl.Buffered(1))


def _dot(a, b):
    return jnp.dot(a, b, preferred_element_type=F32)


def _dot_nt(a, b):
    return lax.dot_general(a, b, (((1,), (1,)), ((), ())), preferred_element_type=F32)


def _dot_tn(a, b):
    return lax.dot_general(a, b, (((0,), (0,)), ((), ())), preferred_element_type=F32)


def _gelu(x):
    t = jnp.tanh(GELU_C0 * (x + GELU_C1 * x * x * x))
    return 0.5 * x * (1.0 + t), t


def _gelu_grad(x, t):
    return 0.5 * (1.0 + t) + 0.5 * x * (1.0 - t * t) * (GELU_C0 * (1.0 + 3.0 * GELU_C1 * x * x))


def _rowmean(x):
    return jnp.mean(x, axis=-1, keepdims=True)


def _colsum(x):
    return jnp.sum(x, axis=0, keepdims=True)


def _rms_fwd(x):
    r = lax.rsqrt(_rowmean(x * x) + EPS)
    return x * r, r


def _rms_bwd(dxhat, xhat, r):
    return r * (dxhat - xhat * _rowmean(dxhat * xhat))


N_PEERS = N_DEV - 1
ANY = pl.BlockSpec(memory_space=pl.ANY)
VMEM = pl.BlockSpec(memory_space=pltpu.VMEM)


def _my_pos():
    return lax.axis_index("x"), lax.axis_index("y"), lax.axis_index("c")


def _peer(pos, k):
    x, y, c = pos
    return (1 - x if k & 4 else x, 1 - y if k & 2 else y, 1 - c if k & 1 else c)


def _flat(pos):
    return 4 * pos[0] + 2 * pos[1] + pos[2]


def _remote_copy(src, dst, send_sem, recv_sem, peer):
    return pltpu.make_async_remote_copy(src_ref=src, dst_ref=dst, send_sem=send_sem, recv_sem=recv_sem,
                                        device_id=peer, device_id_type=pl.DeviceIdType.MESH)


N_CHIPS = N_DEV // 2
SIBLING = 1
SAME_CORE_PEERS = (2, 4, 6)


def _exchange_out_shapes(job):
    def shape(kind, a):
        if kind in ("gather", "gather2"):
            return (N_DEV,) + a.shape
        if kind == "scatter_p1":
            return (N_CHIPS,) + a.shape[1:]
        return a.shape
    return [jax.ShapeDtypeStruct(shape(kind, a), a.dtype) for kind, a in job]


def _exchange_sems(n):
    return [pltpu.SemaphoreType.DMA((n, N_PEERS)), pltpu.SemaphoreType.DMA((n, N_PEERS)), pltpu.SemaphoreType.DMA((n,))]


def _exchange_copies(kinds, src_refs, dst_refs, send_sems, recv_sems, local_sems, phase):
    pos = _my_pos()
    me = _flat(pos)
    chip, core = 2 * pos[0] + pos[1], pos[2]
    copies = []

    def remote(a, src, dst, k, sem=None):
        sem = k - 1 if sem is None else sem
        copies.append(_remote_copy(src, dst, send_sems.at[a, sem], recv_sems.at[a, sem], _peer(pos, k)))

    for a, kind in enumerate(kinds):
        src, dst = src_refs[a], dst_refs[a]
        if phase == 1:
            if kind == "gather2":
                for k in SAME_CORE_PEERS:
                    remote(a, dst.at[me ^ k], dst.at[me ^ k], SIBLING, sem=k)
        elif kind in ("gather", "gather2"):
            copies.append(pltpu.make_async_copy(src, dst.at[me], local_sems.at[a]))
            for k in (range(1, N_DEV) if kind == "gather" else (SIBLING,) + SAME_CORE_PEERS):
                remote(a, src, dst.at[me], k)
        elif kind == "scatter":
            copies.append(pltpu.make_async_copy(src.at[me], dst.at[me], local_sems.at[a]))
            for k in range(1, N_DEV):
                remote(a, src.at[me ^ k], dst.at[me], k)
        elif kind == "scatter_p1":
            for q in range(N_CHIPS):
                remote(a, src.at[2 * q + 1 - core], dst.at[q], SIBLING, sem=q)
        elif kind == "scatter_p2":
            copies.append(pltpu.make_async_copy(src.at[chip], dst.at[chip], local_sems.at[a]))
            for k in SAME_CORE_PEERS:
                remote(a, src.at[chip ^ (k >> 1)], dst.at[chip], k)
    return copies


def _pallas_call_with_exchange(body, *, grid, in_specs, out_specs, out_shape, scratch_shapes, operands, name, job):
    params = _cparams(len(grid))
    if not job:
        outs = pl.pallas_call(body, grid=grid, in_specs=in_specs, out_specs=out_specs, out_shape=out_shape,
                              scratch_shapes=scratch_shapes, compiler_params=params, name=name)(*operands)
        return outs, []
    kinds = [kind for kind, _ in job]
    relayed = [kind if kind == "gather2" else None for kind in kinds]
    unrelayed = [None if kind == "gather2" else kind for kind in kinds]
    n, n_in, n_out, n_scr = len(job), len(in_specs), len(out_specs), len(scratch_shapes)
    n_steps = math.prod(grid)
    relay_step = max(n_steps - 2, 0)

    def wrapped(*refs):
        ins, jin = refs[:n_in], refs[n_in:n_in + n]
        outs, jout = refs[n_in + n:n_in + n + n_out], refs[n_in + n + n_out:n_in + 2 * n + n_out]
        scr = refs[n_in + 2 * n + n_out:n_in + 2 * n + n_out + n_scr]
        sems = refs[n_in + 2 * n + n_out + n_scr:]
        step = pl.program_id(0)
        for d in range(1, len(grid)):
            step = step * grid[d] + pl.program_id(d)

        def copies(which, phase):
            return _exchange_copies(which, jin, jout, *sems, phase=phase)

        @pl.when(step == 0)
        def _():
            for cp in copies(kinds, 0):
                cp.start()

        body(*ins, *outs, *scr)

        @pl.when(step == relay_step)
        def _():
            for cp in copies(relayed, 0):
                cp.wait()
            for cp in copies(relayed, 1):
                cp.start()

        @pl.when(step == n_steps - 1)
        def _():
            for cp in copies(unrelayed, 0) + copies(relayed, 1):
                cp.wait()

    res = pl.pallas_call(
        wrapped, grid=grid,
        in_specs=list(in_specs) + [ANY] * n,
        out_specs=list(out_specs) + [ANY] * n,
        out_shape=list(out_shape) + _exchange_out_shapes(job),
        scratch_shapes=list(scratch_shapes) + _exchange_sems(n),
        compiler_params=params, name=name,
    )(*operands, *[a for _, a in job])
    return res[:n_out], res[n_out:]


def _seg_mean(x, segp):
    hi = x.astype(BF16)
    lo = (x - hi.astype(F32)).astype(BF16)
    return _dot(hi, segp) + _dot(lo, segp)


def _rot_rows(x, shift):
    m, c = x.shape
    return pltpu.roll(x.reshape(m // 8, 8, c), shift, 1).reshape(m, c)


def _sublane_is(shape, s):
    return lax.broadcasted_iota(jnp.int32, shape, 0) % 8 == s


def _causal_tail(tail, prev_rot):
    rot = _rot_rows(tail, 1)
    return jnp.where(_sublane_is(tail.shape, 0), prev_rot, rot), rot


def _lookahead_head(head, next_rot):
    rot = _rot_rows(head, 7)
    return jnp.where(_sublane_is(head.shape, 7), next_rot, rot), rot


def _tile_token_index(t, tile_idx):
    r = lax.broadcasted_iota(jnp.int32, (t, 1), 0)
    return tile_idx * t + (r % 8) * (t // 8) + r // 8


def _interleave(x):
    t, c = x.shape
    return jnp.swapaxes(x.reshape(8, t // 8, c), 0, 1).reshape(t, c)


def _deinterleave(x):
    t, c = x.shape
    return jnp.swapaxes(x.reshape(t // 8, 8, c), 0, 1).reshape(t, c)


def _ffn_fwd(x1, modv, g1024, wup, wdn, cw, cb, name="ffn_fwd", job=None, loss_target=None):
    s_len = x1.shape[0]
    t = FFN_TILE
    n_tiles = s_len // t
    with_loss = loss_target is not None

    def body(x1_ref, *rest):
        if with_loss:
            tgt_ref, rest = rest[0], rest[1:]
            loss_ref, rest = rest[10], rest[:10] + rest[11:]
        mod_ref, g_ref, wup_ref, wdn_ref, cw_ref, cb_ref, x2_ref, y2_ref, p_ref, u_ref, ext_ref, carry_ref = rest
        i = pl.program_id(0)

        @pl.when(i == 0)
        def _():
            carry_ref[...] = jnp.zeros_like(carry_ref)
            if with_loss:
                loss_ref[...] = jnp.zeros_like(loss_ref)

        x1v = x1_ref[...]
        pre_g, post_g = g_ref[3:4, :], g_ref[4:5, :]
        sh2, sc2, g2 = mod_ref[3:4, :], mod_ref[4:5, :], mod_ref[5:6, :]
        xhat, _ = _rms_fwd(x1v)
        h2b = (xhat * pre_g * (1.0 + sc2) + sh2).astype(BF16)

        def conv_shard(s):
            p = _dot_nt(h2b, wup_ref[s])
            p_ref[s] = p.astype(BF16)
            ext_ref[0:FFN_HALO, :], carry_ref[s] = _causal_tail(p[t - FFN_HALO:t, :], carry_ref[s])
            ext_ref[FFN_HALO:FFN_HALO + t, :] = p
            w = cw_ref[s]
            u = w[0:1, :] * ext_ref[0:t, :] + w[1:2, :] * ext_ref[8:8 + t, :] + w[2:3, :] * p + cb_ref[s]
            u_ref[s] = u.astype(BF16)
            return u

        y2 = jnp.zeros((t, D_MODEL), F32)
        for j in range(FF_PAIRS):
            ug = conv_shard(j)
            uv = conv_shard(j + FF_PAIRS)
            ge, _ = _gelu(ug)
            y2 = y2 + _dot((ge * uv).astype(BF16), wdn_ref[j])
        y2_ref[...] = y2
        yhat, _ = _rms_fwd(y2)
        x2 = x1v + g2 * (yhat * post_g)
        if with_loss:
            diff = x2 - _interleave(tgt_ref[...])
            x2_ref[...] = diff * (1.0 / D_MODEL)
            loss_ref[...] += (0.5 / D_MODEL) * jnp.sum(diff * diff)
        else:
            x2_ref[...] = x2

    tile = pl.BlockSpec((t, D_MODEL), lambda i: (i, 0))
    consts = (modv, g1024, wup, wdn, cw, cb)
    shards = pl.BlockSpec((N_DEV, t, FF_SHARD), lambda i: (0, i, 0))
    out_specs = [tile, tile, shards, shards]
    out_shape = [jax.ShapeDtypeStruct((s_len, D_MODEL), F32), jax.ShapeDtypeStruct((s_len, D_MODEL), F32),
                 jax.ShapeDtypeStruct((N_DEV, s_len, FF_SHARD), BF16),
                 jax.ShapeDtypeStruct((N_DEV, s_len, FF_SHARD), BF16)]
    if with_loss:
        out_specs.append(pl.BlockSpec((8, 128), lambda i: (0, 0)))
        out_shape.append(jax.ShapeDtypeStruct((8, 128), F32))
    return _pallas_call_with_exchange(
        body,
        grid=(n_tiles,),
        in_specs=[tile] * (2 if with_loss else 1) + [_whole(c.shape) for c in consts],
        out_specs=out_specs,
        out_shape=out_shape,
        scratch_shapes=[pltpu.VMEM((FFN_HALO + t, FF_SHARD), F32), pltpu.VMEM((N_DEV, FFN_HALO, FF_SHARD), F32)],
        operands=(x1,) + ((loss_target,) if with_loss else ()) + consts,
        name=name, job=job)


def _ffn_bwd(dx2, x1, y2, p, u, modv, g1024, wup, wdn, cw, cb, name="ffn_bwd", job=None):
    s_len = x1.shape[0]
    t = FFN_TILE
    n_tiles = s_len // t
    hb = FFN_HALO

    def body(dx2_ref, x1_ref, y2_ref, p_ref, ph_ref, u_ref, mod_ref, g_ref, wup_ref, wdn_ref, cw_ref, cb_ref,
             dx1_ref, dp_ref, a_ref, dy2_ref, h2_ref, vec_ref, cgrad_ref, ext_ref, dext_ref, dcarry_ref):
        i = pl.program_id(0)
        tile_idx = n_tiles - 1 - i

        @pl.when(i == 0)
        def _():
            vec_ref[...] = jnp.zeros_like(vec_ref)
            cgrad_ref[...] = jnp.zeros_like(cgrad_ref)
            dcarry_ref[...] = jnp.zeros_like(dcarry_ref)

        dx2v, x1v, y2v = dx2_ref[...], x1_ref[...], y2_ref[...]
        pre_g, post_g = g_ref[3:4, :], g_ref[4:5, :]
        sh2, sc2, g2 = mod_ref[3:4, :], mod_ref[4:5, :], mod_ref[5:6, :]

        yhat, ry = _rms_fwd(y2v)
        vec_ref[1:2, :] += _colsum(dx2v * (yhat * post_g))
        dyn = dx2v * g2
        vec_ref[0:1, :] += _colsum(dyn * yhat)
        dy2b = _rms_bwd(dyn * post_g, yhat, ry).astype(BF16)
        dy2_ref[...] = dy2b

        xhat, rx = _rms_fwd(x1v)
        xn = xhat * pre_g
        h2_ref[...] = (xn * (1.0 + sc2) + sh2).astype(BF16)

        not_first = (tile_idx > 0).astype(F32)

        def recompute(s, slot):
            pf = p_ref[s].astype(F32)
            prev_rot = _rot_rows(ph_ref[s].astype(F32), 1) * not_first
            ext_ref[slot, 0:hb, :], _ = _causal_tail(pf[t - hb:t, :], prev_rot)
            ext_ref[slot, hb:hb + t, :] = pf
            return u_ref[s].astype(F32)

        def conv_bwd(s, slot, du):
            w = cw_ref[s]
            cgrad_ref[s, 0:1, :] += _colsum(du * ext_ref[slot, 0:t, :])
            cgrad_ref[s, 1:2, :] += _colsum(du * ext_ref[slot, 8:8 + t, :])
            cgrad_ref[s, 2:3, :] += _colsum(du * ext_ref[slot, 16:16 + t, :])
            cgrad_ref[s, 3:4, :] += _colsum(du)
            dext_ref[0:t, :] = du
            dext_ref[t:t + hb, :], dcarry_ref[s] = _lookahead_head(du[0:hb, :], dcarry_ref[s])
            dp = w[2:3, :] * du + w[1:2, :] * dext_ref[8:8 + t, :] + w[0:1, :] * dext_ref[16:16 + t, :]
            dpb = dp.astype(BF16)
            dp_ref[s] = dpb
            return _dot(dpb, wup_ref[s])

        dh2 = jnp.zeros((t, D_MODEL), F32)
        for j in range(FF_PAIRS):
            ug = recompute(j, 0)
            uv = recompute(j + FF_PAIRS, 1)
            ge, th = _gelu(ug)
            a_ref[j] = (ge * uv).astype(BF16)
            da = _dot_nt(dy2b, wdn_ref[j])
            dh2 = dh2 + conv_bwd(j, 0, da * uv * _gelu_grad(ug, th))
            dh2 = dh2 + conv_bwd(j + FF_PAIRS, 1, da * ge)

        vec_ref[2:3, :] += _colsum(dh2)
        vec_ref[3:4, :] += _colsum(dh2 * xn)
        dxn = dh2 * (1.0 + sc2)
        vec_ref[4:5, :] += _colsum(dxn * xhat)
        dx1_ref[...] = dx2v + _rms_bwd(dxn * pre_g, xhat, rx)

    rev = lambda i: (n_tiles - 1 - i, 0)
    tile = pl.BlockSpec((t, D_MODEL), rev)
    halo_idx = lambda i: (0, jnp.maximum((n_tiles - 1 - i) * (t // hb) - 1, 0), 0)
    return _pallas_call_with_exchange(
        body,
        grid=(n_tiles,),
        in_specs=[tile, tile, tile,
                  pl.BlockSpec((N_DEV, t, FF_SHARD), lambda i: (0, n_tiles - 1 - i, 0)),
                  pl.BlockSpec((N_DEV, hb, FF_SHARD), halo_idx),
                  pl.BlockSpec((N_DEV, t, FF_SHARD), lambda i: (0, n_tiles - 1 - i, 0)),
                  _whole(modv.shape), _whole(g1024.shape), _whole(wup.shape), _whole(wdn.shape),
                  _whole(cw.shape), _whole(cb.shape)],
        out_specs=[tile,
                   pl.BlockSpec((N_DEV, t, FF_SHARD), lambda i: (0, n_tiles - 1 - i, 0)),
                   pl.BlockSpec((FF_PAIRS, t, FF_SHARD), lambda i: (0, n_tiles - 1 - i, 0)),
                   tile, tile,
                   pl.BlockSpec((8, D_MODEL), lambda i: (0, 0)),
                   pl.BlockSpec((N_DEV, 8, FF_SHARD), lambda i: (0, 0, 0))],
        out_shape=[jax.ShapeDtypeStruct((s_len, D_MODEL), F32),
                   jax.ShapeDtypeStruct((N_DEV, s_len, FF_SHARD), BF16),
                   jax.ShapeDtypeStruct((FF_PAIRS, s_len, FF_SHARD), BF16),
                   jax.ShapeDtypeStruct((s_len, D_MODEL), BF16),
                   jax.ShapeDtypeStruct((s_len, D_MODEL), BF16),
                   jax.ShapeDtypeStruct((8, D_MODEL), F32),
                   jax.ShapeDtypeStruct((N_DEV, 8, FF_SHARD), F32)],
        scratch_shapes=[pltpu.VMEM((2, hb + t, FF_SHARD), F32), pltpu.VMEM((t + hb, FF_SHARD), F32),
                        pltpu.VMEM((N_DEV, hb, FF_SHARD), F32)],
        operands=(dx2, x1, y2, p, p, u, modv, g1024, wup, wdn, cw, cb),
        name=name, job=job)


def _wgrad(a, b, name, tk=WGRAD_TK, job=None):
    a_grouped, b_grouped = a.ndim == 3, b.ndim == 3
    groups = a.shape[0] if a_grouped else b.shape[0]
    s_len, m, n = a.shape[-2], a.shape[-1], b.shape[-1]
    tk = min(tk, s_len)
    n_k = s_len // tk

    def body(a_ref, b_ref, o_ref, acc_ref):
        k = pl.program_id(1)
        av = a_ref[0] if a_grouped else a_ref[...]
        bv = b_ref[0] if b_grouped else b_ref[...]
        part = _dot_tn(av, bv)
        if n_k == 1:
            o_ref[0] = part.astype(BF16)
            return

        @pl.when(k == 0)
        def _():
            acc_ref[...] = part

        @pl.when(jnp.logical_and(k > 0, k < n_k - 1))
        def _():
            acc_ref[...] += part

        @pl.when(k == n_k - 1)
        def _():
            o_ref[0] = (acc_ref[...] + part).astype(BF16)

    a_spec = pl.BlockSpec((1, tk, m), lambda g, k: (g, k, 0)) if a_grouped else pl.BlockSpec((tk, m), lambda g, k: (k, 0))
    b_spec = pl.BlockSpec((1, tk, n), lambda g, k: (g, k, 0)) if b_grouped else pl.BlockSpec((tk, n), lambda g, k: (k, 0))
    (out,), exchanged = _pallas_call_with_exchange(
        body,
        grid=(groups, n_k),
        in_specs=[a_spec, b_spec],
        out_specs=[pl.BlockSpec((1, m, n), lambda g, k: (g, 0, 0))],
        out_shape=[jax.ShapeDtypeStruct((groups, m, n), BF16)],
        scratch_shapes=[pltpu.VMEM((m, n), F32)],
        operands=(a, b),
        name=name, job=job)
    return (out, exchanged) if job else out


def _lane(shape):
    return lax.broadcasted_iota(jnp.int32, shape, 1)


def _by_pool_group(shape, vals):
    lane = _lane(shape)
    return jnp.where(lane < 64, vals[0], jnp.where(lane < 128, vals[1], jnp.where(lane < 192, vals[2], vals[3])))


def _pool_inv_counts(t, tile_idx):
    pos1 = _tile_token_index(t, tile_idx) + 1
    return [1.0 / jnp.minimum(pos1, w).astype(F32) for w in POOL_WINDOWS]


def _sgu_keep_mask(t):
    tok_r = _tile_token_index(t, 0)
    c = lax.broadcasted_iota(jnp.int32, (1, t), 1)
    tok_c = (c % 8) * (t // 8) + c // 8
    return jnp.logical_and(tok_r // CHUNK == tok_c // CHUNK, tok_r >= tok_c)


def _masked_sgu_w(sguw_ref):
    keep = _sgu_keep_mask(sguw_ref.shape[1])
    return [jnp.where(keep, sguw_ref[h], 0.0).astype(BF16) for h in range(SGU_HEADS)]


def _branches_fwd(z, tile_idx, p384_ref, cw_ref, wm, bmat_ref, pwbd_ref, psc_ref, segp_ref, g_ref, hext_ref, zext_ref,
                  h_prev_rot, z_prev_rot, conv_saved=None):
    t = z.shape[0]
    segp = segp_ref[...]
    r = {}
    u, _ = _gelu(z[:, 0:SGU_WIDTH])
    vraw, _ = _gelu(z[:, SGU_WIDTH:2 * SGU_WIDTH])
    xc = vraw - _seg_mean(vraw, segp)
    rstd_v = lax.rsqrt(_seg_mean(xc * xc, segp) + EPS)
    xh_v = xc * rstd_v
    vnb = (xh_v * p384_ref[0:1, :] + p384_ref[1:2, :]).astype(BF16)
    first_head = _lane((t, 128)) < HEAD_DIM
    f_pairs = []
    for pr in range(SGU_HEADS // 2):
        vp = vnb[:, pr * 128:(pr + 1) * 128]
        f_pairs.append(jnp.where(first_head, _dot(wm[2 * pr], vp), _dot(wm[2 * pr + 1], vp)))
    f = jnp.concatenate(f_pairs, axis=1) + bmat_ref[...]
    ya = u * f
    r.update(u=u, xh_v=xh_v, rstd_v=rstd_v, vnb=vnb, f=f)
    o_b = 2 * SGU_WIDTH
    a_in = z[:, o_b:o_b + CONV_WIDTH]
    sig_g = jax.nn.sigmoid(z[:, o_b + CONV_WIDTH:o_b + 2 * CONV_WIDTH])
    hh = a_in * sig_g
    hext_ref[0:t, :], r["h_rot"] = _causal_tail(hh, h_prev_rot)
    hext_ref[t:2 * t, :] = hh
    if conv_saved is None:
        conv = jnp.zeros((t, CONV_WIDTH), F32) + p384_ref[2:3, :]
        for k in range(CONV_K):
            conv = conv + cw_ref[k:k + 1, :] * hext_ref[pl.ds(t - 8 * (CONV_K - 1 - k), t), :]
        r["conv"] = conv
    else:
        conv = conv_saved
    cc = conv - _rowmean(conv)
    rstd_c = lax.rsqrt(_rowmean(cc * cc) + EPS)
    xh_c = cc * rstd_c
    cn = xh_c * p384_ref[3:4, :] + p384_ref[4:5, :]
    sig_c = jax.nn.sigmoid(cn)
    yb = cn * sig_c
    r.update(a_in=a_in, sig_g=sig_g, xh_c=xh_c, rstd_c=rstd_c, cn=cn, sig_c=sig_c)
    o_c = o_b + 2 * CONV_WIDTH
    zc = z[:, o_c:o_c + POOL_WIDTH]
    zext_ref[0:POOL_HALO, :], r["z_rot"] = _causal_tail(zc[t - POOL_HALO:t, :], z_prev_rot)
    zext_ref[POOL_HALO:POOL_HALO + t, :] = zc
    sums, acc = [], zc
    for j in range(1, POOL_WINDOWS[-1]):
        acc = acc + zext_ref[pl.ds(POOL_HALO - 8 * j, t), :]
        if j + 1 in POOL_WINDOWS:
            sums.append(acc)
    inv = _pool_inv_counts(t, tile_idx)
    dpool = _by_pool_group((t, POOL_WIDTH), [s * iv for s, iv in zip(sums, inv)]) - zc
    ycp = _dot(dpool.astype(BF16), pwbd_ref[...])
    yc = ycp * psc_ref[0:1, :]
    r.update(dpool=dpool, ycp=ycp)
    yha, ra = _rms_fwd(ya)
    yhb, rb = _rms_fwd(yb)
    yhc, rc = _rms_fwd(yc)
    bg = g_ref[2:3, :]
    ycat = jnp.concatenate([yha * bg[:, 0:384], yhb * bg[:, 384:768], yhc * bg[:, 768:1024]], axis=1)
    r.update(yha=yha, ra=ra, yhb=yhb, rb=rb, yhc=yhc, rc=rc, ycat=ycat)
    return r


def _mixer_fwd(x, modv, g1024, p384, cw, sguw, bmat, pwbd, psc, segp, win, wout, name="mixer_fwd", job=None,
               natural_x=False):
    s_len = x.shape[0]
    t = MIX_TILE

    def body(x_ref, mod_ref, g_ref, p384_ref, cw_ref, sguw_ref, bmat_ref, pwbd_ref, psc_ref, segp_ref, win_ref, wout_ref,
             x1_ref, z_ref, o_ref, conv_ref, hext_ref, zext_ref, hrot_ref, zrot_ref, wm_ref):
        i = pl.program_id(0)

        @pl.when(i == 0)
        def _():
            hrot_ref[...] = jnp.zeros_like(hrot_ref)
            zrot_ref[...] = jnp.zeros_like(zrot_ref)
            for h, wmh in enumerate(_masked_sgu_w(sguw_ref)):
                wm_ref[h] = wmh

        xv = _interleave(x_ref[...]) if natural_x else x_ref[...]
        sh1, sc1, g1 = mod_ref[0:1, :], mod_ref[1:2, :], mod_ref[2:3, :]
        xhat, _ = _rms_fwd(xv)
        h1 = xhat * g_ref[0:1, :] * (1.0 + sc1) + sh1
        z = _dot_nt(h1.astype(BF16), win_ref[...])
        z_ref[...] = z
        r = _branches_fwd(z, i, p384_ref, cw_ref, [wm_ref[h] for h in range(SGU_HEADS)], bmat_ref, pwbd_ref, psc_ref,
                          segp_ref, g_ref, hext_ref, zext_ref, hrot_ref[...], zrot_ref[...])
        hrot_ref[...] = r["h_rot"]
        zrot_ref[...] = r["z_rot"]
        conv_ref[...] = r["conv"]
        o = _dot(r["ycat"].astype(BF16), wout_ref[...])
        o_ref[...] = o
        ohat, _ = _rms_fwd(o)
        x1_ref[...] = xv + g1 * (ohat * g_ref[1:2, :])

    tile = pl.BlockSpec((t, D_MODEL), lambda i: (i, 0))
    consts = (modv, g1024, p384, cw, sguw, bmat, pwbd, psc, segp, win, wout)
    return _pallas_call_with_exchange(
        body,
        grid=(s_len // t,),
        in_specs=[tile] + [_whole(c.shape) for c in consts],
        out_specs=[tile, pl.BlockSpec((t, IN_WIDTH), lambda i: (i, 0)), tile,
                   pl.BlockSpec((t, CONV_WIDTH), lambda i: (i, 0))],
        out_shape=[jax.ShapeDtypeStruct((s_len, D_MODEL), F32), jax.ShapeDtypeStruct((s_len, IN_WIDTH), F32),
                   jax.ShapeDtypeStruct((s_len, D_MODEL), F32), jax.ShapeDtypeStruct((s_len, CONV_WIDTH), F32)],
        scratch_shapes=[pltpu.VMEM((2 * t, CONV_WIDTH), F32), pltpu.VMEM((POOL_HALO + t, POOL_WIDTH), F32),
                        pltpu.VMEM((t, CONV_WIDTH), F32), pltpu.VMEM((POOL_HALO, POOL_WIDTH), F32),
                        pltpu.VMEM((SGU_HEADS, t, t), BF16)],
        operands=(x, *consts),
        name=name, job=job)


def _mixer_bwd(dx1, x, o, z, conv, modv, g1024, p384, cw, sguw, bmat, pwbd, psc, segp, win, wout, name="mixer_bwd",
               job=None, natural_x=False):
    s_len = x.shape[0]
    t = MIX_TILE
    n_tiles = s_len // t

    def body(dx1_ref, x_ref, o_ref, z_ref, zh_ref, conv_ref, mod_ref, g_ref, p384_ref, cw_ref, sguw_ref, bmat_ref, pwbd_ref,
             psc_ref, segp_ref, win_ref, wout_ref,
             dx_ref, dz_ref, do_ref, ycat_ref, h1_ref, vec_ref, v384_ref, dcw_ref, dsguw_ref, dbmat_ref, dpw_ref,
             dpsc_ref, hext_ref, zext_ref, gext_ref, qext_ref, grot_ref, qrot_ref, wm_ref):
        i = pl.program_id(0)
        tile_idx = n_tiles - 1 - i

        @pl.when(i == 0)
        def _():
            for ref in (vec_ref, v384_ref, dcw_ref, dsguw_ref, dbmat_ref, dpw_ref, dpsc_ref, grot_ref, qrot_ref):
                ref[...] = jnp.zeros_like(ref)
            for h, wmh in enumerate(_masked_sgu_w(sguw_ref)):
                wm_ref[h] = wmh

        dx1v, ov, z = dx1_ref[...], o_ref[...], z_ref[...]
        xv = _interleave(x_ref[...]) if natural_x else x_ref[...]
        sh1, sc1, g1 = mod_ref[0:1, :], mod_ref[1:2, :], mod_ref[2:3, :]
        pre_g, post_g, bg = g_ref[0:1, :], g_ref[1:2, :], g_ref[2:3, :]
        segp = segp_ref[...]

        ohat, ro = _rms_fwd(ov)
        vec_ref[1:2, :] += _colsum(dx1v * (ohat * post_g))
        don = dx1v * g1
        vec_ref[0:1, :] += _colsum(don * ohat)
        dob = _rms_bwd(don * post_g, ohat, ro).astype(BF16)
        do_ref[...] = dob
        dycat = _dot_nt(dob, wout_ref[...])

        not_first = (tile_idx > 0).astype(F32)
        o_b = 2 * SGU_WIDTH
        o_c = o_b + 2 * CONV_WIDTH
        h_prev = zh_ref[:, o_b:o_b + CONV_WIDTH] * jax.nn.sigmoid(zh_ref[:, o_b + CONV_WIDTH:o_c])
        h_prev_rot = _rot_rows(h_prev, 1) * not_first
        z_prev_rot = _rot_rows(zh_ref[t - POOL_HALO:t, o_c:o_c + POOL_WIDTH], 1) * not_first
        wm = [wm_ref[h] for h in range(SGU_HEADS)]
        r = _branches_fwd(z, tile_idx, p384_ref, cw_ref, wm, bmat_ref, pwbd_ref, psc_ref, segp_ref, g_ref,
                          hext_ref, zext_ref, h_prev_rot, z_prev_rot, conv_saved=conv_ref[...])
        ycat_ref[...] = r["ycat"].astype(BF16)

        def branch_norm_bwd(dyn, yhat, rr, gain):
            return _colsum(dyn * yhat), _rms_bwd(dyn * gain, yhat, rr)

        dga, dya = branch_norm_bwd(dycat[:, 0:384], r["yha"], r["ra"], bg[:, 0:384])
        dgb, dyb = branch_norm_bwd(dycat[:, 384:768], r["yhb"], r["rb"], bg[:, 384:768])
        dgc, dyc = branch_norm_bwd(dycat[:, 768:1024], r["yhc"], r["rc"], bg[:, 768:1024])
        vec_ref[5:6, :] += jnp.concatenate([dga, dgb, dgc], axis=1)

        du_act = dya * r["f"]
        df = dya * r["u"]
        first_head = _lane((t, 128)) < HEAD_DIM
        dbmat_ref[...] += df
        dvn_pairs = []
        for pr in range(SGU_HEADS // 2):
            dfp = df[:, pr * 128:(pr + 1) * 128]
            df0 = jnp.where(first_head, dfp, 0.0).astype(BF16)
            df1 = jnp.where(first_head, 0.0, dfp).astype(BF16)
            vp = r["vnb"][:, pr * 128:(pr + 1) * 128]
            dvn_pairs.append(_dot_tn(wm[2 * pr], df0) + _dot_tn(wm[2 * pr + 1], df1))
            dsguw_ref[2 * pr] += _dot_nt(df0, vp)
            dsguw_ref[2 * pr + 1] += _dot_nt(df1, vp)
        dvn = jnp.concatenate(dvn_pairs, axis=1)
        v384_ref[0:1, :] += _colsum(dvn * r["xh_v"])
        v384_ref[1:2, :] += _colsum(dvn)
        dxh = dvn * p384_ref[0:1, :]
        dvraw = r["rstd_v"] * (dxh - _seg_mean(dxh, segp) - r["xh_v"] * _seg_mean(dxh * r["xh_v"], segp))
        zu, zv = z[:, 0:SGU_WIDTH], z[:, SGU_WIDTH:o_b]
        _, tu = _gelu(zu)
        _, tv = _gelu(zv)
        dz_u = du_act * _gelu_grad(zu, tu)
        dz_v = dvraw * _gelu_grad(zv, tv)

        cn, sig_c = r["cn"], r["sig_c"]
        dcn = dyb * (sig_c * (1.0 + cn * (1.0 - sig_c)))
        v384_ref[3:4, :] += _colsum(dcn * r["xh_c"])
        v384_ref[4:5, :] += _colsum(dcn)
        dxc = dcn * p384_ref[3:4, :]
        gconv = r["rstd_c"] * (dxc - _rowmean(dxc) - r["xh_c"] * _rowmean(dxc * r["xh_c"]))
        v384_ref[2:3, :] += _colsum(gconv)
        gext_ref[0:t, :] = gconv
        gext_ref[t:2 * t, :], grot_ref[...] = _lookahead_head(gconv, grot_ref[...])
        dhh = jnp.zeros((t, CONV_WIDTH), F32)
        for k in range(CONV_K):
            shift = CONV_K - 1 - k
            dcw_ref[k:k + 1, :] += _colsum(gconv * hext_ref[pl.ds(t - 8 * shift, t), :])
            dhh = dhh + cw_ref[k:k + 1, :] * gext_ref[pl.ds(8 * shift, t), :]
        sig_g = r["sig_g"]
        dz_a = dhh * sig_g
        dz_g = dhh * r["a_in"] * sig_g * (1.0 - sig_g)

        dpsc_ref[0:1, :] += _colsum(dyc * r["ycp"])
        dycp = (dyc * psc_ref[0:1, :]).astype(BF16)
        dpw_ref[...] += _dot_tn(r["dpool"].astype(BF16), dycp)
        ddp = _dot_nt(dycp, pwbd_ref[...])
        inv = _pool_inv_counts(t, tile_idx)
        q = ddp * _by_pool_group((t, POOL_WIDTH), inv)
        qext_ref[0:t, :] = q
        qext_ref[t:t + POOL_HALO, :], qrot_ref[...] = _lookahead_head(q[0:POOL_HALO, :], qrot_ref[...])
        sums, acc = [], q
        for j in range(1, POOL_WINDOWS[-1]):
            acc = acc + qext_ref[pl.ds(8 * j, t), :]
            if j + 1 in POOL_WINDOWS:
                sums.append(acc)
        dz_c = _by_pool_group((t, POOL_WIDTH), sums) - ddp

        dzb = jnp.concatenate([dz_u, dz_v, dz_a, dz_g, dz_c], axis=1).astype(BF16)
        dz_ref[...] = dzb
        dh1 = _dot(dzb, win_ref[...])

        xhat, rx = _rms_fwd(xv)
        xn = xhat * pre_g
        h1_ref[...] = (xn * (1.0 + sc1) + sh1).astype(BF16)
        vec_ref[2:3, :] += _colsum(dh1)
        vec_ref[3:4, :] += _colsum(dh1 * xn)
        dxn = dh1 * (1.0 + sc1)
        vec_ref[4:5, :] += _colsum(dxn * xhat)
        dx = dx1v + _rms_bwd(dxn * pre_g, xhat, rx)
        dx_ref[...] = _deinterleave(dx) if natural_x else dx

        @pl.when(i == n_tiles - 1)
        def _():
            keep = _sgu_keep_mask(t)
            for h in range(SGU_HEADS):
                dsguw_ref[h] = jnp.where(keep, dsguw_ref[h], 0.0)
            dbmat_ref[...] = float(HEAD_DIM) * _seg_mean(dbmat_ref[...], segp)

    rev = lambda i: (n_tiles - 1 - i, 0)
    tile = pl.BlockSpec((t, D_MODEL), rev)
    ztile = pl.BlockSpec((t, IN_WIDTH), rev)
    zhalo = pl.BlockSpec((t, IN_WIDTH), lambda i: (jnp.maximum(n_tiles - 2 - i, 0), 0))
    consts = (modv, g1024, p384, cw, sguw, bmat, pwbd, psc, segp, win, wout)
    acc = lambda shape: pl.BlockSpec(shape, lambda i: (0,) * len(shape))
    acc_shapes = [(8, D_MODEL), (8, SGU_WIDTH), (32, CONV_WIDTH), (SGU_HEADS, t, t), (t, SGU_WIDTH),
                  (POOL_WIDTH, POOL_WIDTH), (8, POOL_WIDTH)]
    return _pallas_call_with_exchange(
        body,
        grid=(n_tiles,),
        in_specs=[tile, tile, tile, ztile, zhalo, pl.BlockSpec((t, CONV_WIDTH), rev)] + [_whole(c.shape) for c in consts],
        out_specs=[tile, ztile, tile, tile, tile] + [acc(s) for s in acc_shapes],
        out_shape=[jax.ShapeDtypeStruct((s_len, D_MODEL), F32), jax.ShapeDtypeStruct((s_len, IN_WIDTH), BF16),
                   jax.ShapeDtypeStruct((s_len, D_MODEL), BF16), jax.ShapeDtypeStruct((s_len, D_MODEL), BF16),
                   jax.ShapeDtypeStruct((s_len, D_MODEL), BF16)] + [jax.ShapeDtypeStruct(s, F32) for s in acc_shapes],
        scratch_shapes=[pltpu.VMEM((2 * t, CONV_WIDTH), F32), pltpu.VMEM((POOL_HALO + t, POOL_WIDTH), F32),
                        pltpu.VMEM((2 * t, CONV_WIDTH), F32), pltpu.VMEM((t + POOL_HALO, POOL_WIDTH), F32),
                        pltpu.VMEM((t, CONV_WIDTH), F32), pltpu.VMEM((POOL_HALO, POOL_WIDTH), F32),
                        pltpu.VMEM((SGU_HEADS, t, t), BF16)],
        operands=(dx1, x, o, z, z, conv, *consts),
        name=name, job=job)


MOD_SHARD = 6 * D_MODEL // N_DEV

ROW_DMOD = 0
ROW_G1024 = 8
ROW_V384 = 16
ROW_SGU_B = 24
ROW_POOL_SCALE = 25
ROW_CONV_W = 32
ROW_FFN_CONV = 64
ROW_POOL_W = 96
ROW_SGU_W = 112
ROWS_PER_LAYER = 208
N_LAYERS = 2


def _gather_weights(c8, mod_w, mod_b8, job):
    kinds = [kind for kind, _ in job]
    shards = [a for _, a in job]
    n = len(shards)

    def body(c_ref, modw_ref, modb_ref, *rest):
        shard_refs = rest[:n]
        sc_all_ref, modrows_ref = rest[n], rest[n + 1]
        full_refs = rest[n + 2:2 * n + 2]
        send_buf, mod_recv, w_send, w_recv, w_local, sc_send, sc_recv, mod_send, mod_recv_sem = rest[2 * n + 2:]
        pos = _my_pos()
        me = _flat(pos)
        peers = [_peer(pos, k) for k in range(1, N_DEV)]

        w_copies = _exchange_copies(kinds, shard_refs, full_refs, w_send, w_recv, w_local, phase=0)
        for cp in w_copies:
            cp.start()

        cv = c_ref[...]
        sc_all_ref[me] = cv * jax.nn.sigmoid(cv)
        sc_copies = [_remote_copy(sc_all_ref.at[me], sc_all_ref.at[me], sc_send.at[k], sc_recv.at[k], peers[k])
                     for k in range(N_PEERS)]
        for cp in sc_copies:
            cp.start()
        for cp in sc_copies:
            cp.wait()

        sc = jnp.concatenate([sc_all_ref[j, 0:1, :] for j in range(N_DEV)], axis=0)
        send_buf[...] = jnp.zeros_like(send_buf)
        for l in range(N_LAYERS):
            part = jnp.dot(sc, modw_ref[l], precision=lax.Precision.HIGHEST, preferred_element_type=F32)
            for j in range(N_DEV):
                send_buf[j, l:l + 1, :] = part[j:j + 1, :]
        mod_recv[me] = send_buf[me]
        mod_copies = [_remote_copy(send_buf.at[_flat(peers[k])], mod_recv.at[me], mod_send.at[k], mod_recv_sem.at[k],
                                   peers[k]) for k in range(N_PEERS)]
        for cp in mod_copies:
            cp.start()
        for cp in mod_copies:
            cp.wait()
        modrows_ref[...] = jnp.zeros_like(modrows_ref)
        for l in range(N_LAYERS):
            row = jnp.concatenate([mod_recv[j, l:l + 1, :] for j in range(N_DEV)], axis=1)
            modrows_ref[l:l + 1, :] = row + modb_ref[l:l + 1, :]

        for cp in w_copies:
            cp.wait()
        relays = _exchange_copies(kinds, shard_refs, full_refs, w_send, w_recv, w_local, phase=1)
        for cp in relays:
            cp.start()
        for cp in relays:
            cp.wait()

    out_shape = ([jax.ShapeDtypeStruct((N_DEV, 8, D_MODEL), F32), jax.ShapeDtypeStruct((8, 6 * D_MODEL), F32)]
                 + [jax.ShapeDtypeStruct((N_DEV,) + s.shape, s.dtype) for s in shards])
    return pl.pallas_call(
        body,
        in_specs=[VMEM, VMEM, VMEM] + [ANY] * n,
        out_specs=[VMEM, VMEM] + [ANY] * n,
        out_shape=out_shape,
        scratch_shapes=[pltpu.VMEM((N_DEV, 8, MOD_SHARD), F32), pltpu.VMEM((N_DEV, 8, MOD_SHARD), F32),
                        pltpu.SemaphoreType.DMA((n, N_PEERS)), pltpu.SemaphoreType.DMA((n, N_PEERS)),
                        pltpu.SemaphoreType.DMA((n,)),
                        pltpu.SemaphoreType.DMA((N_PEERS,)), pltpu.SemaphoreType.DMA((N_PEERS,)),
                        pltpu.SemaphoreType.DMA((N_PEERS,)), pltpu.SemaphoreType.DMA((N_PEERS,))],
        compiler_params=pltpu.CompilerParams(vmem_limit_bytes=VMEM_LIMIT_BYTES),
        name="gather_weights",
    )(c8, mod_w, mod_b8, *shards)


def _small_sums(sc_all, small_all0, small_all1, job):
    kinds = [kind for kind, _ in job]
    n = len(job)

    def body(sc_all_ref, small_all0_ref, small_all1_ref, *rest):
        small_sum_ref, gmodw_ref = rest[n], rest[n + 1]
        copies = _exchange_copies(kinds, rest[:n], rest[n + 2:2 * n + 2], *rest[2 * n + 2:], phase=0)
        for cp in copies:
            cp.start()
        me = _flat(_my_pos())
        sc = jnp.concatenate([sc_all_ref[j, 0:1, :] for j in range(N_DEV)], axis=0)
        mine = lax.broadcasted_iota(jnp.int32, (2 * N_DEV, MOD_SHARD), 0) == me
        for l, parts in enumerate((small_all0_ref, small_all1_ref)):
            total = parts[0].astype(F32)
            for j in range(1, N_DEV):
                total = total + parts[j].astype(F32)
            small_sum_ref[l] = total
            dm = jnp.concatenate(
                [jnp.sum(jnp.where(mine, parts[j, ROW_DMOD:ROW_DMOD + 2 * N_DEV, 0:MOD_SHARD].astype(F32), 0.0),
                         axis=0, keepdims=True) for j in range(N_DEV)], axis=0)
            gmodw_ref[l] = lax.dot_general(sc, dm, (((0,), (0,)), ((), ())), precision=lax.Precision.HIGHEST,
                                           preferred_element_type=F32)
        for cp in copies:
            cp.wait()

    res = pl.pallas_call(
        body,
        in_specs=[VMEM, VMEM, VMEM] + [ANY] * n,
        out_specs=[VMEM, VMEM] + [ANY] * n,
        out_shape=[jax.ShapeDtypeStruct((N_LAYERS, ROWS_PER_LAYER, D_MODEL), F32),
                   jax.ShapeDtypeStruct((N_LAYERS, D_MODEL, MOD_SHARD), F32)] + _exchange_out_shapes(job),
        scratch_shapes=_exchange_sems(n),
        compiler_params=pltpu.CompilerParams(vmem_limit_bytes=VMEM_LIMIT_BYTES),
        name="small_sums",
    )(sc_all, small_all0, small_all1, *[a for _, a in job])
    return res[0], res[1], res[2:]


def _adam_update(g, w, m, v):
    m2 = ADAM_B1 * m + (1.0 - ADAM_B1) * g
    v2 = ADAM_B2 * v + (1.0 - ADAM_B2) * (g * g)
    m_hat = m2 / (1.0 - ADAM_B1 ** ADAM_STEP)
    v_hat = v2 / (1.0 - ADAM_B2 ** ADAM_STEP)
    delta = -ADAM_LR * (m_hat / (jnp.sqrt(v_hat) + ADAM_EPS) + ADAM_WD * w)
    return delta, m2, v2


def _pair_add(g, r1, row_chunk, name):
    _, rows, cols = g.shape
    core = lax.axis_index("c").astype(jnp.int32).reshape(1)

    def body(core_ref, g_ref, r_ref, o_ref):
        o_ref[0] = (g_ref[0, 0].astype(F32) + r_ref[0].astype(F32)).astype(BF16)

    blk = pl.BlockSpec((1, row_chunk, cols), lambda q, i, core_ref: (q, i, 0))
    grid_spec = pltpu.PrefetchScalarGridSpec(
        num_scalar_prefetch=1, grid=(N_CHIPS, rows // row_chunk),
        in_specs=[pl.BlockSpec((1, 1, row_chunk, cols), lambda q, i, core_ref: (q, core_ref[0], i, 0)), blk],
        out_specs=blk)
    return pl.pallas_call(
        body, grid_spec=grid_spec, out_shape=jax.ShapeDtypeStruct((N_CHIPS, rows, cols), BF16),
        compiler_params=_cparams(2), name=name,
    )(core, g.reshape(N_CHIPS, 2, rows, cols), r1)


def _adam_sharded(recv0, recv1, w, m, v, row_chunk, name):
    _, rows, cols = w.shape
    n_chunks = rows // row_chunk

    def body(r0_ref, r1_ref, w_ref, m_ref, v_ref, g_ref, d_ref, m2_ref, v2_ref):
        layer = pl.program_id(0)

        def run(r_ref):
            g = r_ref[0].astype(F32)
            for j in range(1, r_ref.shape[0]):
                g = g + r_ref[j].astype(F32)
            delta, m2, v2 = _adam_update(g, w_ref[0], m_ref[0], v_ref[0])
            g_ref[0], d_ref[0], m2_ref[0], v2_ref[0] = g, delta, m2, v2

        @pl.when(layer == 0)
        def _():
            run(r0_ref)

        @pl.when(layer == 1)
        def _():
            run(r1_ref)

    r0_spec = pl.BlockSpec((recv0.shape[0], row_chunk, cols), lambda l, i: (0, i * (1 - l) + (n_chunks - 1) * l, 0))
    r1_spec = pl.BlockSpec((recv1.shape[0], row_chunk, cols), lambda l, i: (0, i * l, 0))
    blk = pl.BlockSpec((1, row_chunk, cols), lambda l, i: (l, i, 0))
    out = jax.ShapeDtypeStruct(w.shape, F32)
    return pl.pallas_call(
        body,
        grid=(N_LAYERS, n_chunks),
        in_specs=[r0_spec, r1_spec, blk, blk, blk],
        out_specs=[blk] * 4,
        out_shape=[out] * 4,
        compiler_params=_cparams(2),
        name=name,
    )(recv0, recv1, w, m, v)


def _adam_dense(g, w, m, v, row_chunk, name):
    n_lead, rows, cols = w.shape

    def body(g_ref, w_ref, m_ref, v_ref, go_ref, d_ref, m2_ref, v2_ref):
        gv = g_ref[...]
        go_ref[...] = gv
        d_ref[...], m2_ref[...], v2_ref[...] = _adam_update(gv, w_ref[...], m_ref[...], v_ref[...])

    blk = pl.BlockSpec((1, row_chunk, cols), lambda l, i: (l, i, 0))
    out = jax.ShapeDtypeStruct(w.shape, F32)
    return pl.pallas_call(
        body,
        grid=(n_lead, rows // row_chunk),
        in_specs=[blk] * 4,
        out_specs=[blk] * 4,
        out_shape=[out] * 4,
        compiler_params=_cparams(2),
        name=name,
    )(g, w, m, v)


WEIGHT_NAMES = ("mod_w", "mod_b", "mix_pre_g", "mix_post_g", "w_in", "sgu_norm_g", "sgu_norm_b", "sgu_w", "sgu_b",
                "conv_w", "conv_b", "conv_norm_g", "conv_norm_b", "pool_w", "pool_scale", "branch_g", "w_out",
                "ffn_pre_g", "ffn_post_g", "ffn_up", "ffn_conv_w", "ffn_conv_b", "ffn_down")
SHARDED_BIG = ("w_in", "w_out", "ffn_up", "ffn_down")
SMALL_PACKED = tuple(n for n in WEIGHT_NAMES if n not in SHARDED_BIG + ("mod_w",))


def _rows8(rows, width=D_MODEL):
    out = [jnp.pad(r.astype(F32), (0, width - r.shape[0]))[None] for r in rows]
    out.append(jnp.zeros((8 - len(rows), width), F32))
    return jnp.concatenate(out, axis=0)


def _as_rows(a, width=D_MODEL):
    flat = a.astype(F32).reshape(-1)
    pad = (-flat.shape[0]) % width
    return jnp.pad(flat, (0, pad)).reshape(-1, width)


def _pad_cols(a, width=D_MODEL):
    return jnp.pad(a.astype(F32), ((0, 0), (0, width - a.shape[1])))


def _pack_rows(arrays):
    rows = jnp.concatenate([_as_rows(a) for a in arrays], axis=0)
    return jnp.pad(rows, ((0, (-rows.shape[0]) % 8), (0, 0)))


def _unpack_rows(packed, shapes):
    out, r = [], 0
    for shape in shapes:
        size = math.prod(shape)
        n_rows = -(-size // D_MODEL)
        out.append(packed[r:r + n_rows].reshape(-1)[:size].reshape(shape))
        r += n_rows
    return out


def _interleave_axis(a, axis, inverse=False):
    shape = a.shape
    t = shape[axis]
    split = (t // 8, 8) if inverse else (8, t // 8)
    a = a.reshape(shape[:axis] + split + shape[axis + 1:])
    return jnp.swapaxes(a, axis, axis + 1).reshape(shape)


def _layer_consts(l, w, mod_rows, win, wout, conv_w_full):
    modv = _rows8(list(mod_rows[l].reshape(6, D_MODEL)))
    g1024 = _rows8([w["mix_pre_g"][l], w["mix_post_g"][l], w["branch_g"][l], w["ffn_pre_g"][l], w["ffn_post_g"][l]])
    p384 = _rows8([w["sgu_norm_g"][l], w["sgu_norm_b"][l], w["conv_b"][l], w["conv_norm_g"][l], w["conv_norm_b"][l]],
                  SGU_WIDTH)
    cw = jnp.pad(conv_w_full[l], ((0, 32 - CONV_K), (0, 0)))
    reps = MIX_TILE // CHUNK
    sguw = _interleave_axis(_interleave_axis(jnp.tile(w["sgu_w"][l], (1, reps, reps)), 1), 2)
    bmat = _interleave_axis(jnp.tile(jnp.repeat(w["sgu_b"][l].T, HEAD_DIM, axis=1), (reps, 1)), 0)
    groups = len(POOL_WINDOWS)
    eye = jnp.eye(groups, dtype=F32)
    pwbd = (eye[:, None, :, None] * w["pool_w"][l][:, :, None, :]).reshape(POOL_WIDTH, POOL_WIDTH).astype(BF16)
    psc = _rows8([w["pool_scale"][l]], POOL_WIDTH)
    seg = jnp.arange(SGU_WIDTH) // HEAD_DIM
    segp = jnp.where(seg[:, None] == seg[None, :], 1.0 / HEAD_DIM, 0.0).astype(BF16)
    return modv, g1024, (modv, g1024, p384, cw, sguw, bmat, pwbd, psc, segp, win, wout)


def _small_grad_rows(mix, ffn):
    _, _, _, _, _, mvec, v384, dcw, dsguw, dbmat, dpw, dpsc = mix
    fvec, cgrad = ffn[5], ffn[6]
    dmod = jnp.stack([mvec[2], mvec[3], mvec[1], fvec[2], fvec[3], fvec[1]]).reshape(N_DEV, MOD_SHARD)
    g_rows = jnp.stack([mvec[4], mvec[0], mvec[5], fvec[4], fvec[0]])
    reps = MIX_TILE // CHUNK
    dbmat = _interleave_axis(dbmat, 0, inverse=True).reshape(reps, CHUNK, SGU_WIDTH).sum(axis=0)
    dsguw = _interleave_axis(_interleave_axis(dsguw, 1, inverse=True), 2, inverse=True)
    dsguw = dsguw.reshape(SGU_HEADS, reps, CHUNK, reps, CHUNK)
    dsguw = sum(dsguw[:, b, :, b, :] for b in range(reps))
    dsgu_b = dbmat[:, ::HEAD_DIM].T.reshape(1, SGU_HEADS * CHUNK)
    groups = len(POOL_WINDOWS)
    gdim = POOL_WIDTH // groups
    dpw4 = dpw.reshape(groups, gdim, groups, gdim)
    dpool = jnp.stack([dpw4[g, :, g, :] for g in range(groups)])
    blocks = [_pad_cols(dmod), _rows8(list(g_rows)), _pad_cols(v384), _rows8([dsgu_b[0], dpsc[0]]), _pad_cols(dcw),
              _pad_cols(cgrad[:, 0:4, :].reshape(4 * N_DEV, FF_SHARD)), _as_rows(dpool), _as_rows(dsguw)]
    return jnp.concatenate(blocks, axis=0)


def _small_grads_from_rows(total):
    per = {n: [] for n in SMALL_PACKED}
    for l in range(N_LAYERS):
        s = total[l]
        per["mod_b"].append(s[ROW_DMOD:ROW_DMOD + N_DEV, :MOD_SHARD].reshape(6 * D_MODEL))
        for j, name in enumerate(("mix_pre_g", "mix_post_g", "branch_g", "ffn_pre_g", "ffn_post_g")):
            per[name].append(s[ROW_G1024 + j])
        for j, name in enumerate(("sgu_norm_g", "sgu_norm_b", "conv_b", "conv_norm_g", "conv_norm_b")):
            per[name].append(s[ROW_V384 + j, :SGU_WIDTH])
        per["sgu_b"].append(s[ROW_SGU_B, :SGU_HEADS * CHUNK].reshape(SGU_HEADS, CHUNK))
        per["pool_scale"].append(s[ROW_POOL_SCALE, :POOL_WIDTH])
        per["conv_w"].append(s[ROW_CONV_W:ROW_CONV_W + CONV_K, :CONV_WIDTH])
        fc = s[ROW_FFN_CONV:ROW_FFN_CONV + 4 * N_DEV, :FF_SHARD].reshape(N_DEV, 4, FF_SHARD)
        per["ffn_conv_w"].append(fc[:, 0:3, :].transpose(1, 0, 2).reshape(FFN_CONV_K, 2 * D_FF))
        per["ffn_conv_b"].append(fc[:, 3, :].reshape(2 * D_FF))
        per["pool_w"].append(s[ROW_POOL_W:ROW_POOL_W + 16].reshape(len(POOL_WINDOWS), HEAD_DIM, HEAD_DIM))
        per["sgu_w"].append(s[ROW_SGU_W:ROW_SGU_W + 96].reshape(SGU_HEADS, CHUNK, CHUNK))
    return {n: jnp.stack(v) for n, v in per.items()}


def kernel(x, c, mod_w, mod_b, mix_pre_g, mix_post_g, w_in, sgu_norm_g, sgu_norm_b, sgu_w, sgu_b, conv_w, conv_b, conv_norm_g, conv_norm_b, pool_w, pool_scale, branch_g, w_out, ffn_pre_g, ffn_post_g, ffn_up, ffn_conv_w, ffn_conv_b, ffn_down, loss_target, m_mod_w, m_mod_b, m_mix_pre_g, m_mix_post_g, m_w_in, m_sgu_norm_g, m_sgu_norm_b, m_sgu_w, m_sgu_b, m_conv_w, m_conv_b, m_conv_norm_g, m_conv_norm_b, m_pool_w, m_pool_scale, m_branch_g, m_w_out, m_ffn_pre_g, m_ffn_post_g, m_ffn_up, m_ffn_conv_w, m_ffn_conv_b, m_ffn_down, v_mod_w, v_mod_b, v_mix_pre_g, v_mix_post_g, v_w_in, v_sgu_norm_g, v_sgu_norm_b, v_sgu_w, v_sgu_b, v_conv_w, v_conv_b, v_conv_norm_g, v_conv_norm_b, v_pool_w, v_pool_scale, v_branch_g, v_w_out, v_ffn_pre_g, v_ffn_post_g, v_ffn_up, v_ffn_conv_w, v_ffn_conv_b, v_ffn_down):
    w = dict(zip(WEIGHT_NAMES, (mod_w, mod_b, mix_pre_g, mix_post_g, w_in, sgu_norm_g, sgu_norm_b, sgu_w, sgu_b, conv_w,
                                conv_b, conv_norm_g, conv_norm_b, pool_w, pool_scale, branch_g, w_out, ffn_pre_g,
                                ffn_post_g, ffn_up, ffn_conv_w, ffn_conv_b, ffn_down)))
    m = dict(zip(WEIGHT_NAMES, (m_mod_w, m_mod_b, m_mix_pre_g, m_mix_post_g, m_w_in, m_sgu_norm_g, m_sgu_norm_b, m_sgu_w,
                                m_sgu_b, m_conv_w, m_conv_b, m_conv_norm_g, m_conv_norm_b, m_pool_w, m_pool_scale,
                                m_branch_g, m_w_out, m_ffn_pre_g, m_ffn_post_g, m_ffn_up, m_ffn_conv_w, m_ffn_conv_b,
                                m_ffn_down)))
    v = dict(zip(WEIGHT_NAMES, (v_mod_w, v_mod_b, v_mix_pre_g, v_mix_post_g, v_w_in, v_sgu_norm_g, v_sgu_norm_b, v_sgu_w,
                                v_sgu_b, v_conv_w, v_conv_b, v_conv_norm_g, v_conv_norm_b, v_pool_w, v_pool_scale,
                                v_branch_g, v_w_out, v_ffn_pre_g, v_ffn_post_g, v_ffn_up, v_ffn_conv_w, v_ffn_conv_b,
                                v_ffn_down)))
    me = _flat(_my_pos())
    xs = x[0]
    s_len = xs.shape[0]

    transposed = ("w_in", "ffn_up")
    wt = {n: jnp.swapaxes(w[n], 1, 2) if n in transposed else w[n] for n in SHARDED_BIG}
    mt = {n: jnp.swapaxes(m[n], 1, 2) if n in transposed else m[n] for n in SHARDED_BIG}
    vt = {n: jnp.swapaxes(v[n], 1, 2) if n in transposed else v[n] for n in SHARDED_BIG}
    bf16_shards = [[wt[n][l].astype(BF16) for n in SHARDED_BIG] for l in range(N_LAYERS)]

    def mixer_operands(l, win_g, wout_g):
        win = win_g.reshape(IN_WIDTH, D_MODEL)
        return _layer_consts(l, w, mod_rows, win, wout_g.reshape(D_MODEL, D_MODEL), conv_w_full)

    def ffn_operands(l, modv, g1024, wup_g, wdn_g):
        wdn = wdn_g.reshape(FF_PAIRS, FF_SHARD, D_MODEL)
        return modv, g1024, wup_g, wdn, ffn_cw_full[:, l], ffn_conv_b[l].reshape(N_DEV, 1, FF_SHARD)

    w0, w1 = bf16_shards
    c8 = jnp.broadcast_to(c, (8, D_MODEL))
    mod_b8 = jnp.pad(mod_b, ((0, 8 - N_LAYERS), (0, 0)))
    sc_all, mod_rows, win0_g, wout0_g, conv_w_g, ffn_cw_full = _gather_weights(
        c8, mod_w, mod_b8, [("gather2", w0[0]), ("gather2", w0[1]), ("gather", conv_w), ("gather", ffn_conv_w)])
    conv_w_full = conv_w_g.transpose(1, 2, 0, 3).reshape(N_LAYERS, CONV_K, CONV_WIDTH)

    modv0, g0, mix_consts0 = mixer_operands(0, win0_g, wout0_g)
    (x1, z, o, cv), (wup0_g, wdn0_g) = _mixer_fwd(xs, *mix_consts0, name="mixer_fwd_l0", natural_x=True,
                                                  job=[("gather2", w0[2]), ("gather2", w0[3])])
    ffn_consts0 = ffn_operands(0, modv0, g0, wup0_g, wdn0_g)
    (x2, y2, p, u), (win1_g, wout1_g, wdn1_g) = _ffn_fwd(
        x1, *ffn_consts0, name="ffn_fwd_l0", job=[("gather2", w1[0]), ("gather2", w1[1]), ("gather2", w1[3])])
    saved = [(xs, z, o, cv, x1, y2, p, u)]
    modv1, g1, mix_consts1 = mixer_operands(1, win1_g, wout1_g)
    (x1, z, o, cv), (wup1_g,) = _mixer_fwd(x2, *mix_consts1, name="mixer_fwd_l1", job=[("gather2", w1[2])])
    ffn_consts1 = ffn_operands(1, modv1, g1, wup1_g, wdn1_g)
    (dh, y2, p, u, loss_tile), _ = _ffn_fwd(x1, *ffn_consts1, name="ffn_fwd_l1", loss_target=loss_target[0])
    saved.append((x2, z, o, cv, x1, y2, p, u))
    loss = lax.psum(loss_tile[0, 0], ("x", "y", "c"))

    def ffn_weight_grads(l, ffn):
        dp, a, dy2, h2 = ffn[1:5]
        d_up = _wgrad(dp, h2, f"wgrad_ffn_up_l{l}", tk=WGRAD_TK_FFN)
        d_dn = _wgrad(a, dy2, f"wgrad_ffn_down_l{l}", tk=WGRAD_TK_FFN).reshape(N_DEV, D_FF // N_DEV, D_MODEL)
        return d_up, d_dn

    def mixer_weight_grads(l, mix):
        dz, do, ycat, h1 = mix[1:5]
        d_in = _wgrad(dz[None], h1, f"wgrad_w_in_l{l}").reshape(N_DEV, IN_WIDTH // N_DEV, D_MODEL)
        d_out = _wgrad(ycat, do[None], f"wgrad_w_out_l{l}").reshape(N_DEV, D_MODEL // N_DEV, D_MODEL)
        return d_in, d_out

    x_in, z, o, cv, x1, y2, p, u = saved[1]
    ffn1, _ = _ffn_bwd(dh, x1, y2, p, u, *ffn_consts1, name="ffn_bwd_l1")
    d_up1, d_dn1 = ffn_weight_grads(1, ffn1)
    mix1, (sib_up1, sib_dn1) = _mixer_bwd(ffn1[0], x_in, o, z, cv, *mix_consts1, name="mixer_bwd_l1",
                                          job=[("scatter_p1", d_up1), ("scatter_p1", d_dn1)])
    chip_up1 = _pair_add(d_up1, sib_up1, 176, "pair_add_ffn_up_l1")
    chip_dn1 = _pair_add(d_dn1, sib_dn1, 176, "pair_add_ffn_down_l1")
    d_in1, d_out1 = mixer_weight_grads(1, mix1)
    small1 = _small_grad_rows(mix1, ffn1).astype(BF16)

    x_in, z, o, cv, x1, y2, p, u = saved[0]
    ffn0, job_out = _ffn_bwd(mix1[0], x1, y2, p, u, *ffn_consts0, name="ffn_bwd_l0",
                             job=[("scatter", d_in1), ("scatter", d_out1), ("scatter_p2", chip_up1),
                                  ("scatter_p2", chip_dn1), ("gather", small1)])
    recv1, small_all1 = job_out[0:4], job_out[4]
    d_up0, d_dn0 = ffn_weight_grads(0, ffn0)
    mix0, recv_ffn0 = _mixer_bwd(ffn0[0], x_in, o, z, cv, *mix_consts0, name="mixer_bwd_l0", natural_x=True,
                                 job=[("scatter", d_up0), ("scatter", d_dn0)])
    grad_x = mix0[0][None]
    dz, do, ycat, h1 = mix0[1:5]
    small0 = _small_grad_rows(mix0, ffn0).astype(BF16)
    d_in0, (small_all0,) = _wgrad(dz[None], h1, "wgrad_w_in_l0", job=[("gather", small0)])
    d_in0 = d_in0.reshape(N_DEV, IN_WIDTH // N_DEV, D_MODEL)
    d_out0, (recv_in0,) = _wgrad(ycat, do[None], "wgrad_w_out_l0", job=[("scatter", d_in0)])
    d_out0 = d_out0.reshape(N_DEV, D_MODEL // N_DEV, D_MODEL)
    small_total, g_mod_w, (recv_out0,) = _small_sums(sc_all, small_all0, small_all1, [("scatter", d_out0)])
    recv0 = [recv_in0, recv_out0, recv_ffn0[0], recv_ffn0[1]]

    grads, deltas, new_m, new_v = {}, {}, {}, {}
    for j, (name, chunk) in enumerate((("w_in", 224), ("w_out", 128), ("ffn_up", 176), ("ffn_down", 176))):
        outs = _adam_sharded(recv0[j], recv1[j], wt[name], mt[name], vt[name], chunk, "adam_" + name)
        if name in transposed:
            outs = [jnp.swapaxes(t, 1, 2) for t in outs]
        grads[name], deltas[name], new_m[name], new_v[name] = outs
    grads["mod_w"], deltas["mod_w"], new_m["mod_w"], new_v["mod_w"] = _adam_dense(
        g_mod_w, mod_w, m_mod_w, v_mod_w, 256, "adam_mod_w")

    small_g = _small_grads_from_rows(small_total)
    small_g["conv_w"] = lax.dynamic_slice_in_dim(small_g["conv_w"], me * conv_w.shape[2], conv_w.shape[2], axis=2)
    small_g["ffn_conv_w"] = lax.dynamic_slice_in_dim(small_g["ffn_conv_w"], me * FF_SHARD, FF_SHARD, axis=2)
    shapes = [w[n].shape for n in SMALL_PACKED]
    packs = [_pack_rows([src[n] for n in SMALL_PACKED])[None] for src in (small_g, w, m, v)]
    _, d, m2, v2 = _adam_dense(*packs, packs[0].shape[1], "adam_small")
    for name, dd, mm, vv in zip(SMALL_PACKED, _unpack_rows(d[0], shapes), _unpack_rows(m2[0], shapes),
                                _unpack_rows(v2[0], shapes)):
        grads[name], deltas[name], new_m[name], new_v[name] = small_g[name], dd, mm, vv

    return (loss, grad_x, *[grads[n] for n in WEIGHT_NAMES], *[deltas[n] for n in WEIGHT_NAMES],
            *[new_m[n] for n in WEIGHT_NAMES], *[new_v[n] for n in WEIGHT_NAMES])
```

```python
import functools
import math

import jax
import jax.numpy as jnp
from jax import lax
from jax.experimental import pallas as pl
from jax.experimental.pallas import tpu as pltpu

F32 = jnp.float32
BF16 = jnp.bfloat16

D_MODEL = 1024
N_DEV = 8
SGU_WIDTH = 384
CONV_WIDTH = 384
POOL_WIDTH = 256
HEAD_DIM = 64
SGU_HEADS = 6
CHUNK = 128
CONV_K = 31
POOL_WINDOWS = (2, 4, 8, 16)
IN_WIDTH = 1792
D_FF = 2816
FF_SHARD = 2 * D_FF // N_DEV
FF_PAIRS = N_DEV // 2
FFN_CONV_K = 3
EPS = 1e-6
GELU_C0 = math.sqrt(2.0 / math.pi)
GELU_C1 = 0.044715

ADAM_LR = 0.001
ADAM_B1 = 0.9
ADAM_B2 = 0.999
ADAM_EPS = 1e-08
ADAM_WD = 0.01
ADAM_STEP = 10

VMEM_LIMIT_BYTES = 56 * 1024 * 1024
TILE = 256
MIX_TILE = TILE
FFN_TILE = TILE
FFN_HALO = 8 * (FFN_CONV_K - 1)
POOL_HALO = 8 * POOL_WINDOWS[-1]
WGRAD_TK = 2048
WGRAD_TK_FFN = 4096


def _cparams(n_axes):
    return pltpu.CompilerParams(dimension_semantics=("arbitrary",) * n_axes, vmem_limit_bytes=VMEM_LIMIT_BYTES)


def _whole(shape):
    nd = len(shape)
    return pl.BlockSpec(shape, lambda *_: (0,) * nd, pipeline_mode=pl.Buffered(1))


def _dot(a, b):
    return jnp.dot(a, b, preferred_element_type=F32)


def _dot_nt(a, b):
    return lax.dot_general(a, b, (((1,), (1,)), ((), ())), preferred_element_type=F32)


def _dot_tn(a, b):
    return lax.dot_general(a, b, (((0,), (0,)), ((), ())), preferred_element_type=F32)


def _gelu(x):
    t = jnp.tanh(GELU_C0 * (x + GELU_C1 * x * x * x))
    return 0.5 * x * (1.0 + t), t


def _gelu_grad(x, t):
    return 0.5 * (1.0 + t) + 0.5 * x * (1.0 - t * t) * (GELU_C0 * (1.0 + 3.0 * GELU_C1 * x * x))


def _rowmean(x):
    return jnp.mean(x, axis=-1, keepdims=True)


def _colsum(x):
    return jnp.sum(x, axis=0, keepdims=True)


def _rms_fwd(x):
    r = lax.rsqrt(_rowmean(x * x) + EPS)
    return x * r, r


def _rms_bwd(dxhat, xhat, r):
    return r * (dxhat - xhat * _rowmean(dxhat * xhat))


N_PEERS = N_DEV - 1
ANY = pl.BlockSpec(memory_space=pl.ANY)
VMEM = pl.BlockSpec(memory_space=pltpu.VMEM)


def _my_pos():
    return lax.axis_index("x"), lax.axis_index("y"), lax.axis_index("c")


def _peer(pos, k):
    x, y, c = pos
    return (1 - x if k & 4 else x, 1 - y if k & 2 else y, 1 - c if k & 1 else c)


def _flat(pos):
    return 4 * pos[0] + 2 * pos[1] + pos[2]


def _remote_copy(src, dst, send_sem, recv_sem, peer):
    return pltpu.make_async_remote_copy(src_ref=src, dst_ref=dst, send_sem=send_sem, recv_sem=recv_sem,
                                        device_id=peer, device_id_type=pl.DeviceIdType.MESH)


N_CHIPS = N_DEV // 2
SIBLING = 1
SAME_CORE_PEERS = (2, 4, 6)


def _exchange_out_shapes(job):
    def shape(kind, a):
        if kind in ("gather", "gather2"):
            return (N_DEV,) + a.shape
        if kind == "scatter_p1":
            return (N_CHIPS,) + a.shape[1:]
        return a.shape
    return [jax.ShapeDtypeStruct(shape(kind, a), a.dtype) for kind, a in job]


def _exchange_sems(n):
    return [pltpu.SemaphoreType.DMA((n, N_PEERS)), pltpu.SemaphoreType.DMA((n, N_PEERS)), pltpu.SemaphoreType.DMA((n,))]


def _exchange_copies(kinds, src_refs, dst_refs, send_sems, recv_sems, local_sems, phase):
    pos = _my_pos()
    me = _flat(pos)
    chip, core = 2 * pos[0] + pos[1], pos[2]
    copies = []

    def remote(a, src, dst, k, sem=None):
        sem = k - 1 if sem is None else sem
        copies.append(_remote_copy(src, dst, send_sems.at[a, sem], recv_sems.at[a, sem], _peer(pos, k)))

    for a, kind in enumerate(kinds):
        src, dst = src_refs[a], dst_refs[a]
        if phase == 1:
            if kind == "gather2":
                for k in SAME_CORE_PEERS:
                    remote(a, dst.at[me ^ k], dst.at[me ^ k], SIBLING, sem=k)
        elif kind in ("gather", "gather2"):
            copies.append(pltpu.make_async_copy(src, dst.at[me], local_sems.at[a]))
            for k in (range(1, N_DEV) if kind == "gather" else (SIBLING,) + SAME_CORE_PEERS):
                remote(a, src, dst.at[me], k)
        elif kind == "scatter":
            copies.append(pltpu.make_async_copy(src.at[me], dst.at[me], local_sems.at[a]))
            for k in range(1, N_DEV):
                remote(a, src.at[me ^ k], dst.at[me], k)
        elif kind == "scatter_p1":
            for q in range(N_CHIPS):
                remote(a, src.at[2 * q + 1 - core], dst.at[q], SIBLING, sem=q)
        elif kind == "scatter_p2":
            copies.append(pltpu.make_async_copy(src.at[chip], dst.at[chip], local_sems.at[a]))
            for k in SAME_CORE_PEERS:
                remote(a, src.at[chip ^ (k >> 1)], dst.at[chip], k)
    return copies


def _pallas_call_with_exchange(body, *, grid, in_specs, out_specs, out_shape, scratch_shapes, operands, name, job):
    params = _cparams(len(grid))
    if not job:
        outs = pl.pallas_call(body, grid=grid, in_specs=in_specs, out_specs=out_specs, out_shape=out_shape,
                              scratch_shapes=scratch_shapes, compiler_params=params, name=name)(*operands)
        return outs, []
    kinds = [kind for kind, _ in job]
    relayed = [kind if kind == "gather2" else None for kind in kinds]
    unrelayed = [None if kind == "gather2" else kind for kind in kinds]
    n, n_in, n_out, n_scr = len(job), len(in_specs), len(out_specs), len(scratch_shapes)
    n_steps = math.prod(grid)
    relay_step = max(n_steps - 2, 0)

    def wrapped(*refs):
        ins, jin = refs[:n_in], refs[n_in:n_in + n]
        outs, jout = refs[n_in + n:n_in + n + n_out], refs[n_in + n + n_out:n_in + 2 * n + n_out]
        scr = refs[n_in + 2 * n + n_out:n_in + 2 * n + n_out + n_scr]
        sems = refs[n_in + 2 * n + n_out + n_scr:]
        step = pl.program_id(0)
        for d in range(1, len(grid)):
            step = step * grid[d] + pl.program_id(d)

        def copies(which, phase):
            return _exchange_copies(which, jin, jout, *sems, phase=phase)

        @pl.when(step == 0)
        def _():
            for cp in copies(kinds, 0):
                cp.start()

        body(*ins, *outs, *scr)

        @pl.when(step == relay_step)
        def _():
            for cp in copies(relayed, 0):
                cp.wait()
            for cp in copies(relayed, 1):
                cp.start()

        @pl.when(step == n_steps - 1)
        def _():
            for cp in copies(unrelayed, 0) + copies(relayed, 1):
                cp.wait()

    res = pl.pallas_call(
        wrapped, grid=grid,
        in_specs=list(in_specs) + [ANY] * n,
        out_specs=list(out_specs) + [ANY] * n,
        out_shape=list(out_shape) + _exchange_out_shapes(job),
        scratch_shapes=list(scratch_shapes) + _exchange_sems(n),
        compiler_params=params, name=name,
    )(*operands, *[a for _, a in job])
    return res[:n_out], res[n_out:]


def _seg_mean(x, segp):
    hi = x.astype(BF16)
    lo = (x - hi.astype(F32)).astype(BF16)
    return _dot(hi, segp) + _dot(lo, segp)


def _rot_rows(x, shift):
    m, c = x.shape
    return pltpu.roll(x.reshape(m // 8, 8, c), shift, 1).reshape(m, c)


def _sublane_is(shape, s):
    return lax.broadcasted_iota(jnp.int32, shape, 0) % 8 == s


def _causal_tail(tail, prev_rot):
    rot = _rot_rows(tail, 1)
    return jnp.where(_sublane_is(tail.shape, 0), prev_rot, rot), rot


def _lookahead_head(head, next_rot):
    rot = _rot_rows(head, 7)
    return jnp.where(_sublane_is(head.shape, 7), next_rot, rot), rot


def _tile_token_index(t, tile_idx):
    r = lax.broadcasted_iota(jnp.int32, (t, 1), 0)
    return tile_idx * t + (r % 8) * (t // 8) + r // 8


def _interleave(x):
    t, c = x.shape
    return jnp.swapaxes(x.reshape(8, t // 8, c), 0, 1).reshape(t, c)


def _deinterleave(x):
    t, c = x.shape
    return jnp.swapaxes(x.reshape(t // 8, 8, c), 0, 1).reshape(t, c)


def _ffn_fwd(x1, modv, g1024, wup, wdn, cw, cb, name="ffn_fwd", job=None, loss_target=None):
    s_len = x1.shape[0]
    t = FFN_TILE
    n_tiles = s_len // t
    with_loss = loss_target is not None

    def body(x1_ref, *rest):
        if with_loss:
            tgt_ref, rest = rest[0], rest[1:]
            loss_ref, rest = rest[10], rest[:10] + rest[11:]
        mod_ref, g_ref, wup_ref, wdn_ref, cw_ref, cb_ref, x2_ref, y2_ref, p_ref, u_ref, ext_ref, carry_ref = rest
        i = pl.program_id(0)

        @pl.when(i == 0)
        def _():
            carry_ref[...] = jnp.zeros_like(carry_ref)
            if with_loss:
                loss_ref[...] = jnp.zeros_like(loss_ref)

        x1v = x1_ref[...]
        pre_g, post_g = g_ref[3:4, :], g_ref[4:5, :]
        sh2, sc2, g2 = mod_ref[3:4, :], mod_ref[4:5, :], mod_ref[5:6, :]
        xhat, _ = _rms_fwd(x1v)
        h2b = (xhat * pre_g * (1.0 + sc2) + sh2).astype(BF16)

        def conv_shard(s):
            p = _dot_nt(h2b, wup_ref[s])
            p_ref[s] = p.astype(BF16)
            ext_ref[0:FFN_HALO, :], carry_ref[s] = _causal_tail(p[t - FFN_HALO:t, :], carry_ref[s])
            ext_ref[FFN_HALO:FFN_HALO + t, :] = p
            w = cw_ref[s]
            u = w[0:1, :] * ext_ref[0:t, :] + w[1:2, :] * ext_ref[8:8 + t, :] + w[2:3, :] * p + cb_ref[s]
            u_ref[s] = u.astype(BF16)
            return u

        y2 = jnp.zeros((t, D_MODEL), F32)
        for j in range(FF_PAIRS):
            ug = conv_shard(j)
            uv = conv_shard(j + FF_PAIRS)
            ge, _ = _gelu(ug)
            y2 = y2 + _dot((ge * uv).astype(BF16), wdn_ref[j])
        y2_ref[...] = y2
        yhat, _ = _rms_fwd(y2)
        x2 = x1v + g2 * (yhat * post_g)
        if with_loss:
            diff = x2 - _interleave(tgt_ref[...])
            x2_ref[...] = diff * (1.0 / D_MODEL)
            loss_ref[...] += (0.5 / D_MODEL) * jnp.sum(diff * diff)
        else:
            x2_ref[...] = x2

    tile = pl.BlockSpec((t, D_MODEL), lambda i: (i, 0))
    consts = (modv, g1024, wup, wdn, cw, cb)
    shards = pl.BlockSpec((N_DEV, t, FF_SHARD), lambda i: (0, i, 0))
    out_specs = [tile, tile, shards, shards]
    out_shape = [jax.ShapeDtypeStruct((s_len, D_MODEL), F32), jax.ShapeDtypeStruct((s_len, D_MODEL), F32),
                 jax.ShapeDtypeStruct((N_DEV, s_len, FF_SHARD), BF16),
                 jax.ShapeDtypeStruct((N_DEV, s_len, FF_SHARD), BF16)]
    if with_loss:
        out_specs.append(pl.BlockSpec((8, 128), lambda i: (0, 0)))
        out_shape.append(jax.ShapeDtypeStruct((8, 128), F32))
    return _pallas_call_with_exchange(
        body,
        grid=(n_tiles,),
        in_specs=[tile] * (2 if with_loss else 1) + [_whole(c.shape) for c in consts],
        out_specs=out_specs,
        out_shape=out_shape,
        scratch_shapes=[pltpu.VMEM((FFN_HALO + t, FF_SHARD), F32), pltpu.VMEM((N_DEV, FFN_HALO, FF_SHARD), F32)],
        operands=(x1,) + ((loss_target,) if with_loss else ()) + consts,
        name=name, job=job)


def _ffn_bwd(dx2, x1, y2, p, u, modv, g1024, wup, wdn, cw, cb, name="ffn_bwd", job=None):
    s_len = x1.shape[0]
    t = FFN_TILE
    n_tiles = s_len // t
    hb = FFN_HALO

    def body(dx2_ref, x1_ref, y2_ref, p_ref, ph_ref, u_ref, mod_ref, g_ref, wup_ref, wdn_ref, cw_ref, cb_ref,
             dx1_ref, dp_ref, a_ref, dy2_ref, h2_ref, vec_ref, cgrad_ref, ext_ref, dext_ref, dcarry_ref):
        i = pl.program_id(0)
        tile_idx = n_tiles - 1 - i

        @pl.when(i == 0)
        def _():
            vec_ref[...] = jnp.zeros_like(vec_ref)
            cgrad_ref[...] = jnp.zeros_like(cgrad_ref)
            dcarry_ref[...] = jnp.zeros_like(dcarry_ref)

        dx2v, x1v, y2v = dx2_ref[...], x1_ref[...], y2_ref[...]
        pre_g, post_g = g_ref[3:4, :], g_ref[4:5, :]
        sh2, sc2, g2 = mod_ref[3:4, :], mod_ref[4:5, :], mod_ref[5:6, :]

        yhat, ry = _rms_fwd(y2v)
        vec_ref[1:2, :] += _colsum(dx2v * (yhat * post_g))
        dyn = dx2v * g2
        vec_ref[0:1, :] += _colsum(dyn * yhat)
        dy2b = _rms_bwd(dyn * post_g, yhat, ry).astype(BF16)
        dy2_ref[...] = dy2b

        xhat, rx = _rms_fwd(x1v)
        xn = xhat * pre_g
        h2_ref[...] = (xn * (1.0 + sc2) + sh2).astype(BF16)

        not_first = (tile_idx > 0).astype(F32)

        def recompute(s, slot):
            pf = p_ref[s].astype(F32)
            prev_rot = _rot_rows(ph_ref[s].astype(F32), 1) * not_first
            ext_ref[slot, 0:hb, :], _ = _causal_tail(pf[t - hb:t, :], prev_rot)
            ext_ref[slot, hb:hb + t, :] = pf
            return u_ref[s].astype(F32)

        def conv_bwd(s, slot, du):
            w = cw_ref[s]
            cgrad_ref[s, 0:1, :] += _colsum(du * ext_ref[slot, 0:t, :])
            cgrad_ref[s, 1:2, :] += _colsum(du * ext_ref[slot, 8:8 + t, :])
            cgrad_ref[s, 2:3, :] += _colsum(du * ext_ref[slot, 16:16 + t, :])
            cgrad_ref[s, 3:4, :] += _colsum(du)
            dext_ref[0:t, :] = du
            dext_ref[t:t + hb, :], dcarry_ref[s] = _lookahead_head(du[0:hb, :], dcarry_ref[s])
            dp = w[2:3, :] * du + w[1:2, :] * dext_ref[8:8 + t, :] + w[0:1, :] * dext_ref[16:16 + t, :]
            dpb = dp.astype(BF16)
            dp_ref[s] = dpb
            return _dot(dpb, wup_ref[s])

        dh2 = jnp.zeros((t, D_MODEL), F32)
        for j in range(FF_PAIRS):
            ug = recompute(j, 0)
            uv = recompute(j + FF_PAIRS, 1)
            ge, th = _gelu(ug)
            a_ref[j] = (ge * uv).astype(BF16)
            da = _dot_nt(dy2b, wdn_ref[j])
            dh2 = dh2 + conv_bwd(j, 0, da * uv * _gelu_grad(ug, th))
            dh2 = dh2 + conv_bwd(j + FF_PAIRS, 1, da * ge)

        vec_ref[2:3, :] += _colsum(dh2)
        vec_ref[3:4, :] += _colsum(dh2 * xn)
        dxn = dh2 * (1.0 + sc2)
        vec_ref[4:5, :] += _colsum(dxn * xhat)
        dx1_ref[...] = dx2v + _rms_bwd(dxn * pre_g, xhat, rx)

    rev = lambda i: (n_tiles - 1 - i, 0)
    tile = pl.BlockSpec((t, D_MODEL), rev)
    halo_idx = lambda i: (0, jnp.maximum((n_tiles - 1 - i) * (t // hb) - 1, 0), 0)
    return _pallas_call_with_exchange(
        body,
        grid=(n_tiles,),
        in_specs=[tile, tile, tile,
                  pl.BlockSpec((N_DEV, t, FF_SHARD), lambda i: (0, n_tiles - 1 - i, 0)),
                  pl.BlockSpec((N_DEV, hb, FF_SHARD), halo_idx),
                  pl.BlockSpec((N_DEV, t, FF_SHARD), lambda i: (0, n_tiles - 1 - i, 0)),
                  _whole(modv.shape), _whole(g1024.shape), _whole(wup.shape), _whole(wdn.shape),
                  _whole(cw.shape), _whole(cb.shape)],
        out_specs=[tile,
                   pl.BlockSpec((N_DEV, t, FF_SHARD), lambda i: (0, n_tiles - 1 - i, 0)),
                   pl.BlockSpec((FF_PAIRS, t, FF_SHARD), lambda i: (0, n_tiles - 1 - i, 0)),
                   tile, tile,
                   pl.BlockSpec((8, D_MODEL), lambda i: (0, 0)),
                   pl.BlockSpec((N_DEV, 8, FF_SHARD), lambda i: (0, 0, 0))],
        out_shape=[jax.ShapeDtypeStruct((s_len, D_MODEL), F32),
                   jax.ShapeDtypeStruct((N_DEV, s_len, FF_SHARD), BF16),
                   jax.ShapeDtypeStruct((FF_PAIRS, s_len, FF_SHARD), BF16),
                   jax.ShapeDtypeStruct((s_len, D_MODEL), BF16),
                   jax.ShapeDtypeStruct((s_len, D_MODEL), BF16),
                   jax.ShapeDtypeStruct((8, D_MODEL), F32),
                   jax.ShapeDtypeStruct((N_DEV, 8, FF_SHARD), F32)],
        scratch_shapes=[pltpu.VMEM((2, hb + t, FF_SHARD), F32), pltpu.VMEM((t + hb, FF_SHARD), F32),
                        pltpu.VMEM((N_DEV, hb, FF_SHARD), F32)],
        operands=(dx2, x1, y2, p, p, u, modv, g1024, wup, wdn, cw, cb),
        name=name, job=job)


def _wgrad(a, b, name, tk=WGRAD_TK, job=None):
    a_grouped, b_grouped = a.ndim == 3, b.ndim == 3
    groups = a.shape[0] if a_grouped else b.shape[0]
    s_len, m, n = a.shape[-2], a.shape[-1], b.shape[-1]
    tk = min(tk, s_len)
    n_k = s_len // tk

    def body(a_ref, b_ref, o_ref, acc_ref):
        k = pl.program_id(1)
        av = a_ref[0] if a_grouped else a_ref[...]
        bv = b_ref[0] if b_grouped else b_ref[...]
        part = _dot_tn(av, bv)
        if n_k == 1:
            o_ref[0] = part.astype(BF16)
            return

        @pl.when(k == 0)
        def _():
            acc_ref[...] = part

        @pl.when(jnp.logical_and(k > 0, k < n_k - 1))
        def _():
            acc_ref[...] += part

        @pl.when(k == n_k - 1)
        def _():
            o_ref[0] = (acc_ref[...] + part).astype(BF16)

    a_spec = pl.BlockSpec((1, tk, m), lambda g, k: (g, k, 0)) if a_grouped else pl.BlockSpec((tk, m), lambda g, k: (k, 0))
    b_spec = pl.BlockSpec((1, tk, n), lambda g, k: (g, k, 0)) if b_grouped else pl.BlockSpec((tk, n), lambda g, k: (k, 0))
    (out,), exchanged = _pallas_call_with_exchange(
        body,
        grid=(groups, n_k),
        in_specs=[a_spec, b_spec],
        out_specs=[pl.BlockSpec((1, m, n), lambda g, k: (g, 0, 0))],
        out_shape=[jax.ShapeDtypeStruct((groups, m, n), BF16)],
        scratch_shapes=[pltpu.VMEM((m, n), F32)],
        operands=(a, b),
        name=name, job=job)
    return (out, exchanged) if job else out


def _lane(shape):
    return lax.broadcasted_iota(jnp.int32, shape, 1)


def _by_pool_group(shape, vals):
    lane = _lane(shape)
    return jnp.where(lane < 64, vals[0], jnp.where(lane < 128, vals[1], jnp.where(lane < 192, vals[2], vals[3])))


def _pool_inv_counts(t, tile_idx):
    pos1 = _tile_token_index(t, tile_idx) + 1
    return [1.0 / jnp.minimum(pos1, w).astype(F32) for w in POOL_WINDOWS]


def _sgu_keep_mask(t):
    tok_r = _tile_token_index(t, 0)
    c = lax.broadcasted_iota(jnp.int32, (1, t), 1)
    tok_c = (c % 8) * (t // 8) + c // 8
    return jnp.logical_and(tok_r // CHUNK == tok_c // CHUNK, tok_r >= tok_c)


def _masked_sgu_w(sguw_ref):
    keep = _sgu_keep_mask(sguw_ref.shape[1])
    return [jnp.where(keep, sguw_ref[h], 0.0).astype(BF16) for h in range(SGU_HEADS)]


def _branches_fwd(z, tile_idx, p384_ref, cw_ref, wm, bmat_ref, pwbd_ref, psc_ref, segp_ref, g_ref, hext_ref, zext_ref,
                  h_prev_rot, z_prev_rot, conv_saved=None):
    t = z.shape[0]
    segp = segp_ref[...]
    r = {}
    u, _ = _gelu(z[:, 0:SGU_WIDTH])
    vraw, _ = _gelu(z[:, SGU_WIDTH:2 * SGU_WIDTH])
    xc = vraw - _seg_mean(vraw, segp)
    rstd_v = lax.rsqrt(_seg_mean(xc * xc, segp) + EPS)
    xh_v = xc * rstd_v
    vnb = (xh_v * p384_ref[0:1, :] + p384_ref[1:2, :]).astype(BF16)
    first_head = _lane((t, 128)) < HEAD_DIM
    f_pairs = []
    for pr in range(SGU_HEADS // 2):
        vp = vnb[:, pr * 128:(pr + 1) * 128]
        f_pairs.append(jnp.where(first_head, _dot(wm[2 * pr], vp), _dot(wm[2 * pr + 1], vp)))
    f = jnp.concatenate(f_pairs, axis=1) + bmat_ref[...]
    ya = u * f
    r.update(u=u, xh_v=xh_v, rstd_v=rstd_v, vnb=vnb, f=f)
    o_b = 2 * SGU_WIDTH
    a_in = z[:, o_b:o_b + CONV_WIDTH]
    sig_g = jax.nn.sigmoid(z[:, o_b + CONV_WIDTH:o_b + 2 * CONV_WIDTH])
    hh = a_in * sig_g
    hext_ref[0:t, :], r["h_rot"] = _causal_tail(hh, h_prev_rot)
    hext_ref[t:2 * t, :] = hh
    if conv_saved is None:
        conv = jnp.zeros((t, CONV_WIDTH), F32) + p384_ref[2:3, :]
        for k in range(CONV_K):
            conv = conv + cw_ref[k:k + 1, :] * hext_ref[pl.ds(t - 8 * (CONV_K - 1 - k), t), :]
        r["conv"] = conv
    else:
        conv = conv_saved
    cc = conv - _rowmean(conv)
    rstd_c = lax.rsqrt(_rowmean(cc * cc) + EPS)
    xh_c = cc * rstd_c
    cn = xh_c * p384_ref[3:4, :] + p384_ref[4:5, :]
    sig_c = jax.nn.sigmoid(cn)
    yb = cn * sig_c
    r.update(a_in=a_in, sig_g=sig_g, xh_c=xh_c, rstd_c=rstd_c, cn=cn, sig_c=sig_c)
    o_c = o_b + 2 * CONV_WIDTH
    zc = z[:, o_c:o_c + POOL_WIDTH]
    zext_ref[0:POOL_HALO, :], r["z_rot"] = _causal_tail(zc[t - POOL_HALO:t, :], z_prev_rot)
    zext_ref[POOL_HALO:POOL_HALO + t, :] = zc
    sums, acc = [], zc
    for j in range(1, POOL_WINDOWS[-1]):
        acc = acc + zext_ref[pl.ds(POOL_HALO - 8 * j, t), :]
        if j + 1 in POOL_WINDOWS:
            sums.append(acc)
    inv = _pool_inv_counts(t, tile_idx)
    dpool = _by_pool_group((t, POOL_WIDTH), [s * iv for s, iv in zip(sums, inv)]) - zc
    ycp = _dot(dpool.astype(BF16), pwbd_ref[...])
    yc = ycp * psc_ref[0:1, :]
    r.update(dpool=dpool, ycp=ycp)
    yha, ra = _rms_fwd(ya)
    yhb, rb = _rms_fwd(yb)
    yhc, rc = _rms_fwd(yc)
    bg = g_ref[2:3, :]
    ycat = jnp.concatenate([yha * bg[:, 0:384], yhb * bg[:, 384:768], yhc * bg[:, 768:1024]], axis=1)
    r.update(yha=yha, ra=ra, yhb=yhb, rb=rb, yhc=yhc, rc=rc, ycat=ycat)
    return r


def _mixer_fwd(x, modv, g1024, p384, cw, sguw, bmat, pwbd, psc, segp, win, wout, name="mixer_fwd", job=None,
               natural_x=False):
    s_len = x.shape[0]
    t = MIX_TILE

    def body(x_ref, mod_ref, g_ref, p384_ref, cw_ref, sguw_ref, bmat_ref, pwbd_ref, psc_ref, segp_ref, win_ref, wout_ref,
             x1_ref, z_ref, o_ref, conv_ref, hext_ref, zext_ref, hrot_ref, zrot_ref, wm_ref):
        i = pl.program_id(0)

        @pl.when(i == 0)
        def _():
            hrot_ref[...] = jnp.zeros_like(hrot_ref)
            zrot_ref[...] = jnp.zeros_like(zrot_ref)
            for h, wmh in enumerate(_masked_sgu_w(sguw_ref)):
                wm_ref[h] = wmh

        xv = _interleave(x_ref[...]) if natural_x else x_ref[...]
        sh1, sc1, g1 = mod_ref[0:1, :], mod_ref[1:2, :], mod_ref[2:3, :]
        xhat, _ = _rms_fwd(xv)
        h1 = xhat * g_ref[0:1, :] * (1.0 + sc1) + sh1
        z = _dot_nt(h1.astype(BF16), win_ref[...])
        z_ref[...] = z
        r = _branches_fwd(z, i, p384_ref, cw_ref, [wm_ref[h] for h in range(SGU_HEADS)], bmat_ref, pwbd_ref, psc_ref,
                          segp_ref, g_ref, hext_ref, zext_ref, hrot_ref[...], zrot_ref[...])
        hrot_ref[...] = r["h_rot"]
        zrot_ref[...] = r["z_rot"]
        conv_ref[...] = r["conv"]
        o = _dot(r["ycat"].astype(BF16), wout_ref[...])
        o_ref[...] = o
        ohat, _ = _rms_fwd(o)
        x1_ref[...] = xv + g1 * (ohat * g_ref[1:2, :])

    tile = pl.BlockSpec((t, D_MODEL), lambda i: (i, 0))
    consts = (modv, g1024, p384, cw, sguw, bmat, pwbd, psc, segp, win, wout)
    return _pallas_call_with_exchange(
        body,
        grid=(s_len // t,),
        in_specs=[tile] + [_whole(c.shape) for c in consts],
        out_specs=[tile, pl.BlockSpec((t, IN_WIDTH), lambda i: (i, 0)), tile,
                   pl.BlockSpec((t, CONV_WIDTH), lambda i: (i, 0))],
        out_shape=[jax.ShapeDtypeStruct((s_len, D_MODEL), F32), jax.ShapeDtypeStruct((s_len, IN_WIDTH), F32),
                   jax.ShapeDtypeStruct((s_len, D_MODEL), F32), jax.ShapeDtypeStruct((s_len, CONV_WIDTH), F32)],
        scratch_shapes=[pltpu.VMEM((2 * t, CONV_WIDTH), F32), pltpu.VMEM((POOL_HALO + t, POOL_WIDTH), F32),
                        pltpu.VMEM((t, CONV_WIDTH), F32), pltpu.VMEM((POOL_HALO, POOL_WIDTH), F32),
                        pltpu.VMEM((SGU_HEADS, t, t), BF16)],
        operands=(x, *consts),
        name=name, job=job)


def _mixer_bwd(dx1, x, o, z, conv, modv, g1024, p384, cw, sguw, bmat, pwbd, psc, segp, win, wout, name="mixer_bwd",
               job=None, natural_x=False):
    s_len = x.shape[0]
    t = MIX_TILE
    n_tiles = s_len // t

    def body(dx1_ref, x_ref, o_ref, z_ref, zh_ref, conv_ref, mod_ref, g_ref, p384_ref, cw_ref, sguw_ref, bmat_ref, pwbd_ref,
             psc_ref, segp_ref, win_ref, wout_ref,
             dx_ref, dz_ref, do_ref, ycat_ref, h1_ref, vec_ref, v384_ref, dcw_ref, dsguw_ref, dbmat_ref, dpw_ref,
             dpsc_ref, hext_ref, zext_ref, gext_ref, qext_ref, grot_ref, qrot_ref, wm_ref):
        i = pl.program_id(0)
        tile_idx = n_tiles - 1 - i

        @pl.when(i == 0)
        def _():
            for ref in (vec_ref, v384_ref, dcw_ref, dsguw_ref, dbmat_ref, dpw_ref, dpsc_ref, grot_ref, qrot_ref):
                ref[...] = jnp.zeros_like(ref)
            for h, wmh in enumerate(_masked_sgu_w(sguw_ref)):
                wm_ref[h] = wmh

        dx1v, ov, z = dx1_ref[...], o_ref[...], z_ref[...]
        xv = _interleave(x_ref[...]) if natural_x else x_ref[...]
        sh1, sc1, g1 = mod_ref[0:1, :], mod_ref[1:2, :], mod_ref[2:3, :]
        pre_g, post_g, bg = g_ref[0:1, :], g_ref[1:2, :], g_ref[2:3, :]
        segp = segp_ref[...]

        ohat, ro = _rms_fwd(ov)
        vec_ref[1:2, :] += _colsum(dx1v * (ohat * post_g))
        don = dx1v * g1
        vec_ref[0:1, :] += _colsum(don * ohat)
        dob = _rms_bwd(don * post_g, ohat, ro).astype(BF16)
        do_ref[...] = dob
        dycat = _dot_nt(dob, wout_ref[...])

        not_first = (tile_idx > 0).astype(F32)
        o_b = 2 * SGU_WIDTH
        o_c = o_b + 2 * CONV_WIDTH
        h_prev = zh_ref[:, o_b:o_b + CONV_WIDTH] * jax.nn.sigmoid(zh_ref[:, o_b + CONV_WIDTH:o_c])
        h_prev_rot = _rot_rows(h_prev, 1) * not_first
        z_prev_rot = _rot_rows(zh_ref[t - POOL_HALO:t, o_c:o_c + POOL_WIDTH], 1) * not_first
        wm = [wm_ref[h] for h in range(SGU_HEADS)]
        r = _branches_fwd(z, tile_idx, p384_ref, cw_ref, wm, bmat_ref, pwbd_ref, psc_ref, segp_ref, g_ref,
                          hext_ref, zext_ref, h_prev_rot, z_prev_rot, conv_saved=conv_ref[...])
        ycat_ref[...] = r["ycat"].astype(BF16)

        def branch_norm_bwd(dyn, yhat, rr, gain):
            return _colsum(dyn * yhat), _rms_bwd(dyn * gain, yhat, rr)

        dga, dya = branch_norm_bwd(dycat[:, 0:384], r["yha"], r["ra"], bg[:, 0:384])
        dgb, dyb = branch_norm_bwd(dycat[:, 384:768], r["yhb"], r["rb"], bg[:, 384:768])
        dgc, dyc = branch_norm_bwd(dycat[:, 768:1024], r["yhc"], r["rc"], bg[:, 768:1024])
        vec_ref[5:6, :] += jnp.concatenate([dga, dgb, dgc], axis=1)

        du_act = dya * r["f"]
        df = dya * r["u"]
        first_head = _lane((t, 128)) < HEAD_DIM
        dbmat_ref[...] += df
        dvn_pairs = []
        for pr in range(SGU_HEADS // 2):
            dfp = df[:, pr * 128:(pr + 1) * 128]
            df0 = jnp.where(first_head, dfp, 0.0).astype(BF16)
            df1 = jnp.where(first_head, 0.0, dfp).astype(BF16)
            vp = r["vnb"][:, pr * 128:(pr + 1) * 128]
            dvn_pairs.append(_dot_tn(wm[2 * pr], df0) + _dot_tn(wm[2 * pr + 1], df1))
            dsguw_ref[2 * pr] += _dot_nt(df0, vp)
            dsguw_ref[2 * pr + 1] += _dot_nt(df1, vp)
        dvn = jnp.concatenate(dvn_pairs, axis=1)
        v384_ref[0:1, :] += _colsum(dvn * r["xh_v"])
        v384_ref[1:2, :] += _colsum(dvn)
        dxh = dvn * p384_ref[0:1, :]
        dvraw = r["rstd_v"] * (dxh - _seg_mean(dxh, segp) - r["xh_v"] * _seg_mean(dxh * r["xh_v"], segp))
        zu, zv = z[:, 0:SGU_WIDTH], z[:, SGU_WIDTH:o_b]
        _, tu = _gelu(zu)
        _, tv = _gelu(zv)
        dz_u = du_act * _gelu_grad(zu, tu)
        dz_v = dvraw * _gelu_grad(zv, tv)

        cn, sig_c = r["cn"], r["sig_c"]
        dcn = dyb * (sig_c * (1.0 + cn * (1.0 - sig_c)))
        v384_ref[3:4, :] += _colsum(dcn * r["xh_c"])
        v384_ref[4:5, :] += _colsum(dcn)
        dxc = dcn * p384_ref[3:4, :]
        gconv = r["rstd_c"] * (dxc - _rowmean(dxc) - r["xh_c"] * _rowmean(dxc * r["xh_c"]))
        v384_ref[2:3, :] += _colsum(gconv)
        gext_ref[0:t, :] = gconv
        gext_ref[t:2 * t, :], grot_ref[...] = _lookahead_head(gconv, grot_ref[...])
        dhh = jnp.zeros((t, CONV_WIDTH), F32)
        for k in range(CONV_K):
            shift = CONV_K - 1 - k
            dcw_ref[k:k + 1, :] += _colsum(gconv * hext_ref[pl.ds(t - 8 * shift, t), :])
            dhh = dhh + cw_ref[k:k + 1, :] * gext_ref[pl.ds(8 * shift, t), :]
        sig_g = r["sig_g"]
        dz_a = dhh * sig_g
        dz_g = dhh * r["a_in"] * sig_g * (1.0 - sig_g)

        dpsc_ref[0:1, :] += _colsum(dyc * r["ycp"])
        dycp = (dyc * psc_ref[0:1, :]).astype(BF16)
        dpw_ref[...] += _dot_tn(r["dpool"].astype(BF16), dycp)
        ddp = _dot_nt(dycp, pwbd_ref[...])
        inv = _pool_inv_counts(t, tile_idx)
        q = ddp * _by_pool_group((t, POOL_WIDTH), inv)
        qext_ref[0:t, :] = q
        qext_ref[t:t + POOL_HALO, :], qrot_ref[...] = _lookahead_head(q[0:POOL_HALO, :], qrot_ref[...])
        sums, acc = [], q
        for j in range(1, POOL_WINDOWS[-1]):
            acc = acc + qext_ref[pl.ds(8 * j, t), :]
            if j + 1 in POOL_WINDOWS:
                sums.append(acc)
        dz_c = _by_pool_group((t, POOL_WIDTH), sums) - ddp

        dzb = jnp.concatenate([dz_u, dz_v, dz_a, dz_g, dz_c], axis=1).astype(BF16)
        dz_ref[...] = dzb
        dh1 = _dot(dzb, win_ref[...])

        xhat, rx = _rms_fwd(xv)
        xn = xhat * pre_g
        h1_ref[...] = (xn * (1.0 + sc1) + sh1).astype(BF16)
        vec_ref[2:3, :] += _colsum(dh1)
        vec_ref[3:4, :] += _colsum(dh1 * xn)
        dxn = dh1 * (1.0 + sc1)
        vec_ref[4:5, :] += _colsum(dxn * xhat)
        dx = dx1v + _rms_bwd(dxn * pre_g, xhat, rx)
        dx_ref[...] = _deinterleave(dx) if natural_x else dx

        @pl.when(i == n_tiles - 1)
        def _():
            keep = _sgu_keep_mask(t)
            for h in range(SGU_HEADS):
                dsguw_ref[h] = jnp.where(keep, dsguw_ref[h], 0.0)
            dbmat_ref[...] = float(HEAD_DIM) * _seg_mean(dbmat_ref[...], segp)

    rev = lambda i: (n_tiles - 1 - i, 0)
    tile = pl.BlockSpec((t, D_MODEL), rev)
    ztile = pl.BlockSpec((t, IN_WIDTH), rev)
    zhalo = pl.BlockSpec((t, IN_WIDTH), lambda i: (jnp.maximum(n_tiles - 2 - i, 0), 0))
    consts = (modv, g1024, p384, cw, sguw, bmat, pwbd, psc, segp, win, wout)
    acc = lambda shape: pl.BlockSpec(shape, lambda i: (0,) * len(shape))
    acc_shapes = [(8, D_MODEL), (8, SGU_WIDTH), (32, CONV_WIDTH), (SGU_HEADS, t, t), (t, SGU_WIDTH),
                  (POOL_WIDTH, POOL_WIDTH), (8, POOL_WIDTH)]
    return _pallas_call_with_exchange(
        body,
        grid=(n_tiles,),
        in_specs=[tile, tile, tile, ztile, zhalo, pl.BlockSpec((t, CONV_WIDTH), rev)] + [_whole(c.shape) for c in consts],
        out_specs=[tile, ztile, tile, tile, tile] + [acc(s) for s in acc_shapes],
        out_shape=[jax.ShapeDtypeStruct((s_len, D_MODEL), F32), jax.ShapeDtypeStruct((s_len, IN_WIDTH), BF16),
                   jax.ShapeDtypeStruct((s_len, D_MODEL), BF16), jax.ShapeDtypeStruct((s_len, D_MODEL), BF16),
                   jax.ShapeDtypeStruct((s_len, D_MODEL), BF16)] + [jax.ShapeDtypeStruct(s, F32) for s in acc_shapes],
        scratch_shapes=[pltpu.VMEM((2 * t, CONV_WIDTH), F32), pltpu.VMEM((POOL_HALO + t, POOL_WIDTH), F32),
                        pltpu.VMEM((2 * t, CONV_WIDTH), F32), pltpu.VMEM((t + POOL_HALO, POOL_WIDTH), F32),
                        pltpu.VMEM((t, CONV_WIDTH), F32), pltpu.VMEM((POOL_HALO, POOL_WIDTH), F32),
                        pltpu.VMEM((SGU_HEADS, t, t), BF16)],
        operands=(dx1, x, o, z, z, conv, *consts),
        name=name, job=job)


MOD_SHARD = 6 * D_MODEL // N_DEV

ROW_DMOD = 0
ROW_G1024 = 8
ROW_V384 = 16
ROW_SGU_B = 24
ROW_POOL_SCALE = 25
ROW_CONV_W = 32
ROW_FFN_CONV = 64
ROW_POOL_W = 96
ROW_SGU_W = 112
ROWS_PER_LAYER = 208
N_LAYERS = 2


def _gather_weights(c8, mod_w, mod_b8, job):
    kinds = [kind for kind, _ in job]
    shards = [a for _, a in job]
    n = len(shards)

    def body(c_ref, modw_ref, modb_ref, *rest):
        shard_refs = rest[:n]
        sc_all_ref, modrows_ref = rest[n], rest[n + 1]
        full_refs = rest[n + 2:2 * n + 2]
        send_buf, mod_recv, w_send, w_recv, w_local, sc_send, sc_recv, mod_send, mod_recv_sem = rest[2 * n + 2:]
        pos = _my_pos()
        me = _flat(pos)
        peers = [_peer(pos, k) for k in range(1, N_DEV)]

        w_copies = _exchange_copies(kinds, shard_refs, full_refs, w_send, w_recv, w_local, phase=0)
        for cp in w_copies:
            cp.start()

        cv = c_ref[...]
        sc_all_ref[me] = cv * jax.nn.sigmoid(cv)
        sc_copies = [_remote_copy(sc_all_ref.at[me], sc_all_ref.at[me], sc_send.at[k], sc_recv.at[k], peers[k])
                     for k in range(N_PEERS)]
        for cp in sc_copies:
            cp.start()
        for cp in sc_copies:
            cp.wait()

        sc = jnp.concatenate([sc_all_ref[j, 0:1, :] for j in range(N_DEV)], axis=0)
        send_buf[...] = jnp.zeros_like(send_buf)
        for l in range(N_LAYERS):
            part = jnp.dot(sc, modw_ref[l], precision=lax.Precision.HIGHEST, preferred_element_type=F32)
            for j in range(N_DEV):
                send_buf[j, l:l + 1, :] = part[j:j + 1, :]
        mod_recv[me] = send_buf[me]
        mod_copies = [_remote_copy(send_buf.at[_flat(peers[k])], mod_recv.at[me], mod_send.at[k], mod_recv_sem.at[k],
                                   peers[k]) for k in range(N_PEERS)]
        for cp in mod_copies:
            cp.start()
        for cp in mod_copies:
            cp.wait()
        modrows_ref[...] = jnp.zeros_like(modrows_ref)
        for l in range(N_LAYERS):
            row = jnp.concatenate([mod_recv[j, l:l + 1, :] for j in range(N_DEV)], axis=1)
            modrows_ref[l:l + 1, :] = row + modb_ref[l:l + 1, :]

        for cp in w_copies:
            cp.wait()
        relays = _exchange_copies(kinds, shard_refs, full_refs, w_send, w_recv, w_local, phase=1)
        for cp in relays:
            cp.start()
        for cp in relays:
            cp.wait()

    out_shape = ([jax.ShapeDtypeStruct((N_DEV, 8, D_MODEL), F32), jax.ShapeDtypeStruct((8, 6 * D_MODEL), F32)]
                 + [jax.ShapeDtypeStruct((N_DEV,) + s.shape, s.dtype) for s in shards])
    return pl.pallas_call(
        body,
        in_specs=[VMEM, VMEM, VMEM] + [ANY] * n,
        out_specs=[VMEM, VMEM] + [ANY] * n,
        out_shape=out_shape,
        scratch_shapes=[pltpu.VMEM((N_DEV, 8, MOD_SHARD), F32), pltpu.VMEM((N_DEV, 8, MOD_SHARD), F32),
                        pltpu.SemaphoreType.DMA((n, N_PEERS)), pltpu.SemaphoreType.DMA((n, N_PEERS)),
                        pltpu.SemaphoreType.DMA((n,)),
                        pltpu.SemaphoreType.DMA((N_PEERS,)), pltpu.SemaphoreType.DMA((N_PEERS,)),
                        pltpu.SemaphoreType.DMA((N_PEERS,)), pltpu.SemaphoreType.DMA((N_PEERS,))],
        compiler_params=pltpu.CompilerParams(vmem_limit_bytes=VMEM_LIMIT_BYTES),
        name="gather_weights",
    )(c8, mod_w, mod_b8, *shards)


def _small_sums(sc_all, small_all0, small_all1, job):
    kinds = [kind for kind, _ in job]
    n = len(job)

    def body(sc_all_ref, small_all0_ref, small_all1_ref, *rest):
        small_sum_ref, gmodw_ref = rest[n], rest[n + 1]
        copies = _exchange_copies(kinds, rest[:n], rest[n + 2:2 * n + 2], *rest[2 * n + 2:], phase=0)
        for cp in copies:
            cp.start()
        me = _flat(_my_pos())
        sc = jnp.concatenate([sc_all_ref[j, 0:1, :] for j in range(N_DEV)], axis=0)
        mine = lax.broadcasted_iota(jnp.int32, (2 * N_DEV, MOD_SHARD), 0) == me
        for l, parts in enumerate((small_all0_ref, small_all1_ref)):
            total = parts[0].astype(F32)
            for j in range(1, N_DEV):
                total = total + parts[j].astype(F32)
            small_sum_ref[l] = total
            dm = jnp.concatenate(
                [jnp.sum(jnp.where(mine, parts[j, ROW_DMOD:ROW_DMOD + 2 * N_DEV, 0:MOD_SHARD].astype(F32), 0.0),
                         axis=0, keepdims=True) for j in range(N_DEV)], axis=0)
            gmodw_ref[l] = lax.dot_general(sc, dm, (((0,), (0,)), ((), ())), precision=lax.Precision.HIGHEST,
                                           preferred_element_type=F32)
        for cp in copies:
            cp.wait()

    res = pl.pallas_call(
        body,
        in_specs=[VMEM, VMEM, VMEM] + [ANY] * n,
        out_specs=[VMEM, VMEM] + [ANY] * n,
        out_shape=[jax.ShapeDtypeStruct((N_LAYERS, ROWS_PER_LAYER, D_MODEL), F32),
                   jax.ShapeDtypeStruct((N_LAYERS, D_MODEL, MOD_SHARD), F32)] + _exchange_out_shapes(job),
        scratch_shapes=_exchange_sems(n),
        compiler_params=pltpu.CompilerParams(vmem_limit_bytes=VMEM_LIMIT_BYTES),
        name="small_sums",
    )(sc_all, small_all0, small_all1, *[a for _, a in job])
    return res[0], res[1], res[2:]


def _adam_update(g, w, m, v):
    m2 = ADAM_B1 * m + (1.0 - ADAM_B1) * g
    v2 = ADAM_B2 * v + (1.0 - ADAM_B2) * (g * g)
    m_hat = m2 / (1.0 - ADAM_B1 ** ADAM_STEP)
    v_hat = v2 / (1.0 - ADAM_B2 ** ADAM_STEP)
    delta = -ADAM_LR * (m_hat / (jnp.sqrt(v_hat) + ADAM_EPS) + ADAM_WD * w)
    return delta, m2, v2


def _pair_add(g, r1, row_chunk, name):
    _, rows, cols = g.shape
    core = lax.axis_index("c").astype(jnp.int32).reshape(1)

    def body(core_ref, g_ref, r_ref, o_ref):
        o_ref[0] = (g_ref[0, 0].astype(F32) + r_ref[0].astype(F32)).astype(BF16)

    blk = pl.BlockSpec((1, row_chunk, cols), lambda q, i, core_ref: (q, i, 0))
    grid_spec = pltpu.PrefetchScalarGridSpec(
        num_scalar_prefetch=1, grid=(N_CHIPS, rows // row_chunk),
        in_specs=[pl.BlockSpec((1, 1, row_chunk, cols), lambda q, i, core_ref: (q, core_ref[0], i, 0)), blk],
        out_specs=blk)
    return pl.pallas_call(
        body, grid_spec=grid_spec, out_shape=jax.ShapeDtypeStruct((N_CHIPS, rows, cols), BF16),
        compiler_params=_cparams(2), name=name,
    )(core, g.reshape(N_CHIPS, 2, rows, cols), r1)


def _adam_sharded(recv0, recv1, w, m, v, row_chunk, name):
    _, rows, cols = w.shape
    n_chunks = rows // row_chunk

    def body(r0_ref, r1_ref, w_ref, m_ref, v_ref, g_ref, d_ref, m2_ref, v2_ref):
        layer = pl.program_id(0)

        def run(r_ref):
            g = r_ref[0].astype(F32)
            for j in range(1, r_ref.shape[0]):
                g = g + r_ref[j].astype(F32)
            delta, m2, v2 = _adam_update(g, w_ref[0], m_ref[0], v_ref[0])
            g_ref[0], d_ref[0], m2_ref[0], v2_ref[0] = g, delta, m2, v2

        @pl.when(layer == 0)
        def _():
            run(r0_ref)

        @pl.when(layer == 1)
        def _():
            run(r1_ref)

    r0_spec = pl.BlockSpec((recv0.shape[0], row_chunk, cols), lambda l, i: (0, i * (1 - l) + (n_chunks - 1) * l, 0))
    r1_spec = pl.BlockSpec((recv1.shape[0], row_chunk, cols), lambda l, i: (0, i * l, 0))
    blk = pl.BlockSpec((1, row_chunk, cols), lambda l, i: (l, i, 0))
    out = jax.ShapeDtypeStruct(w.shape, F32)
    return pl.pallas_call(
        body,
        grid=(N_LAYERS, n_chunks),
        in_specs=[r0_spec, r1_spec, blk, blk, blk],
        out_specs=[blk] * 4,
        out_shape=[out] * 4,
        compiler_params=_cparams(2),
        name=name,
    )(recv0, recv1, w, m, v)


def _adam_dense(g, w, m, v, row_chunk, name):
    n_lead, rows, cols = w.shape

    def body(g_ref, w_ref, m_ref, v_ref, go_ref, d_ref, m2_ref, v2_ref):
        gv = g_ref[...]
        go_ref[...] = gv
        d_ref[...], m2_ref[...], v2_ref[...] = _adam_update(gv, w_ref[...], m_ref[...], v_ref[...])

    blk = pl.BlockSpec((1, row_chunk, cols), lambda l, i: (l, i, 0))
    out = jax.ShapeDtypeStruct(w.shape, F32)
    return pl.pallas_call(
        body,
        grid=(n_lead, rows // row_chunk),
        in_specs=[blk] * 4,
        out_specs=[blk] * 4,
        out_shape=[out] * 4,
        compiler_params=_cparams(2),
        name=name,
    )(g, w, m, v)


WEIGHT_NAMES = ("mod_w", "mod_b", "mix_pre_g", "mix_post_g", "w_in", "sgu_norm_g", "sgu_norm_b", "sgu_w", "sgu_b",
                "conv_w", "conv_b", "conv_norm_g", "conv_norm_b", "pool_w", "pool_scale", "branch_g", "w_out",
                "ffn_pre_g", "ffn_post_g", "ffn_up", "ffn_conv_w", "ffn_conv_b", "ffn_down")
SHARDED_BIG = ("w_in", "w_out", "ffn_up", "ffn_down")
SMALL_PACKED = tuple(n for n in WEIGHT_NAMES if n not in SHARDED_BIG + ("mod_w",))


def _rows8(rows, width=D_MODEL):
    out = [jnp.pad(r.astype(F32), (0, width - r.shape[0]))[None] for r in rows]
    out.append(jnp.zeros((8 - len(rows), width), F32))
    return jnp.concatenate(out, axis=0)


def _as_rows(a, width=D_MODEL):
    flat = a.astype(F32).reshape(-1)
    pad = (-flat.shape[0]) % width
    return jnp.pad(flat, (0, pad)).reshape(-1, width)


def _pad_cols(a, width=D_MODEL):
    return jnp.pad(a.astype(F32), ((0, 0), (0, width - a.shape[1])))


def _pack_rows(arrays):
    rows = jnp.concatenate([_as_rows(a) for a in arrays], axis=0)
    return jnp.pad(rows, ((0, (-rows.shape[0]) % 8), (0, 0)))


def _unpack_rows(packed, shapes):
    out, r = [], 0
    for shape in shapes:
        size = math.prod(shape)
        n_rows = -(-size // D_MODEL)
        out.append(packed[r:r + n_rows].reshape(-1)[:size].reshape(shape))
        r += n_rows
    return out


def _interleave_axis(a, axis, inverse=False):
    shape = a.shape
    t = shape[axis]
    split = (t // 8, 8) if inverse else (8, t // 8)
    a = a.reshape(shape[:axis] + split + shape[axis + 1:])
    return jnp.swapaxes(a, axis, axis + 1).reshape(shape)


def _layer_consts(l, w, mod_rows, win, wout, conv_w_full):
    modv = _rows8(list(mod_rows[l].reshape(6, D_MODEL)))
    g1024 = _rows8([w["mix_pre_g"][l], w["mix_post_g"][l], w["branch_g"][l], w["ffn_pre_g"][l], w["ffn_post_g"][l]])
    p384 = _rows8([w["sgu_norm_g"][l], w["sgu_norm_b"][l], w["conv_b"][l], w["conv_norm_g"][l], w["conv_norm_b"][l]],
                  SGU_WIDTH)
    cw = jnp.pad(conv_w_full[l], ((0, 32 - CONV_K), (0, 0)))
    reps = MIX_TILE // CHUNK
    sguw = _interleave_axis(_interleave_axis(jnp.tile(w["sgu_w"][l], (1, reps, reps)), 1), 2)
    bmat = _interleave_axis(jnp.tile(jnp.repeat(w["sgu_b"][l].T, HEAD_DIM, axis=1), (reps, 1)), 0)
    groups = len(POOL_WINDOWS)
    eye = jnp.eye(groups, dtype=F32)
    pwbd = (eye[:, None, :, None] * w["pool_w"][l][:, :, None, :]).reshape(POOL_WIDTH, POOL_WIDTH).astype(BF16)
    psc = _rows8([w["pool_scale"][l]], POOL_WIDTH)
    seg = jnp.arange(SGU_WIDTH) // HEAD_DIM
    segp = jnp.where(seg[:, None] == seg[None, :], 1.0 / HEAD_DIM, 0.0).astype(BF16)
    return modv, g1024, (modv, g1024, p384, cw, sguw, bmat, pwbd, psc, segp, win, wout)


def _small_grad_rows(mix, ffn):
    _, _, _, _, _, mvec, v384, dcw, dsguw, dbmat, dpw, dpsc = mix
    fvec, cgrad = ffn[5], ffn[6]
    dmod = jnp.stack([mvec[2], mvec[3], mvec[1], fvec[2], fvec[3], fvec[1]]).reshape(N_DEV, MOD_SHARD)
    g_rows = jnp.stack([mvec[4], mvec[0], mvec[5], fvec[4], fvec[0]])
    reps = MIX_TILE // CHUNK
    dbmat = _interleave_axis(dbmat, 0, inverse=True).reshape(reps, CHUNK, SGU_WIDTH).sum(axis=0)
    dsguw = _interleave_axis(_interleave_axis(dsguw, 1, inverse=True), 2, inverse=True)
    dsguw = dsguw.reshape(SGU_HEADS, reps, CHUNK, reps, CHUNK)
    dsguw = sum(dsguw[:, b, :, b, :] for b in range(reps))
    dsgu_b = dbmat[:, ::HEAD_DIM].T.reshape(1, SGU_HEADS * CHUNK)
    groups = len(POOL_WINDOWS)
    gdim = POOL_WIDTH // groups
    dpw4 = dpw.reshape(groups, gdim, groups, gdim)
    dpool = jnp.stack([dpw4[g, :, g, :] for g in range(groups)])
    blocks = [_pad_cols(dmod), _rows8(list(g_rows)), _pad_cols(v384), _rows8([dsgu_b[0], dpsc[0]]), _pad_cols(dcw),
              _pad_cols(cgrad[:, 0:4, :].reshape(4 * N_DEV, FF_SHARD)), _as_rows(dpool), _as_rows(dsguw)]
    return jnp.concatenate(blocks, axis=0)


def _small_grads_from_rows(total):
    per = {n: [] for n in SMALL_PACKED}
    for l in range(N_LAYERS):
        s = total[l]
        per["mod_b"].append(s[ROW_DMOD:ROW_DMOD + N_DEV, :MOD_SHARD].reshape(6 * D_MODEL))
        for j, name in enumerate(("mix_pre_g", "mix_post_g", "branch_g", "ffn_pre_g", "ffn_post_g")):
            per[name].append(s[ROW_G1024 + j])
        for j, name in enumerate(("sgu_norm_g", "sgu_norm_b", "conv_b", "conv_norm_g", "conv_norm_b")):
            per[name].append(s[ROW_V384 + j, :SGU_WIDTH])
        per["sgu_b"].append(s[ROW_SGU_B, :SGU_HEADS * CHUNK].reshape(SGU_HEADS, CHUNK))
        per["pool_scale"].append(s[ROW_POOL_SCALE, :POOL_WIDTH])
        per["conv_w"].append(s[ROW_CONV_W:ROW_CONV_W + CONV_K, :CONV_WIDTH])
        fc = s[ROW_FFN_CONV:ROW_FFN_CONV + 4 * N_DEV, :FF_SHARD].reshape(N_DEV, 4, FF_SHARD)
        per["ffn_conv_w"].append(fc[:, 0:3, :].transpose(1, 0, 2).reshape(FFN_CONV_K, 2 * D_FF))
        per["ffn_conv_b"].append(fc[:, 3, :].reshape(2 * D_FF))
        per["pool_w"].append(s[ROW_POOL_W:ROW_POOL_W + 16].reshape(len(POOL_WINDOWS), HEAD_DIM, HEAD_DIM))
        per["sgu_w"].append(s[ROW_SGU_W:ROW_SGU_W + 96].reshape(SGU_HEADS, CHUNK, CHUNK))
    return {n: jnp.stack(v) for n, v in per.items()}


def kernel(x, c, mod_w, mod_b, mix_pre_g, mix_post_g, w_in, sgu_norm_g, sgu_norm_b, sgu_w, sgu_b, conv_w, conv_b, conv_norm_g, conv_norm_b, pool_w, pool_scale, branch_g, w_out, ffn_pre_g, ffn_post_g, ffn_up, ffn_conv_w, ffn_conv_b, ffn_down, loss_target, m_mod_w, m_mod_b, m_mix_pre_g, m_mix_post_g, m_w_in, m_sgu_norm_g, m_sgu_norm_b, m_sgu_w, m_sgu_b, m_conv_w, m_conv_b, m_conv_norm_g, m_conv_norm_b, m_pool_w, m_pool_scale, m_branch_g, m_w_out, m_ffn_pre_g, m_ffn_post_g, m_ffn_up, m_ffn_conv_w, m_ffn_conv_b, m_ffn_down, v_mod_w, v_mod_b, v_mix_pre_g, v_mix_post_g, v_w_in, v_sgu_norm_g, v_sgu_norm_b, v_sgu_w, v_sgu_b, v_conv_w, v_conv_b, v_conv_norm_g, v_conv_norm_b, v_pool_w, v_pool_scale, v_branch_g, v_w_out, v_ffn_pre_g, v_ffn_post_g, v_ffn_up, v_ffn_conv_w, v_ffn_conv_b, v_ffn_down):
    w = dict(zip(WEIGHT_NAMES, (mod_w, mod_b, mix_pre_g, mix_post_g, w_in, sgu_norm_g, sgu_norm_b, sgu_w, sgu_b, conv_w,
                                conv_b, conv_norm_g, conv_norm_b, pool_w, pool_scale, branch_g, w_out, ffn_pre_g,
                                ffn_post_g, ffn_up, ffn_conv_w, ffn_conv_b, ffn_down)))
    m = dict(zip(WEIGHT_NAMES, (m_mod_w, m_mod_b, m_mix_pre_g, m_mix_post_g, m_w_in, m_sgu_norm_g, m_sgu_norm_b, m_sgu_w,
                                m_sgu_b, m_conv_w, m_conv_b, m_conv_norm_g, m_conv_norm_b, m_pool_w, m_pool_scale,
                                m_branch_g, m_w_out, m_ffn_pre_g, m_ffn_post_g, m_ffn_up, m_ffn_conv_w, m_ffn_conv_b,
                                m_ffn_down)))
    v = dict(zip(WEIGHT_NAMES, (v_mod_w, v_mod_b, v_mix_pre_g, v_mix_post_g, v_w_in, v_sgu_norm_g, v_sgu_norm_b, v_sgu_w,
                                v_sgu_b, v_conv_w, v_conv_b, v_conv_norm_g, v_conv_norm_b, v_pool_w, v_pool_scale,
                                v_branch_g, v_w_out, v_ffn_pre_g, v_ffn_post_g, v_ffn_up, v_ffn_conv_w, v_ffn_conv_b,
                                v_ffn_down)))
    me = _flat(_my_pos())
    xs = x[0]
    s_len = xs.shape[0]

    transposed = ("w_in", "ffn_up")
    wt = {n: jnp.swapaxes(w[n], 1, 2) if n in transposed else w[n] for n in SHARDED_BIG}
    mt = {n: jnp.swapaxes(m[n], 1, 2) if n in transposed else m[n] for n in SHARDED_BIG}
    vt = {n: jnp.swapaxes(v[n], 1, 2) if n in transposed else v[n] for n in SHARDED_BIG}
    bf16_shards = [[wt[n][l].astype(BF16) for n in SHARDED_BIG] for l in range(N_LAYERS)]

    def mixer_operands(l, win_g, wout_g):
        win = win_g.reshape(IN_WIDTH, D_MODEL)
        return _layer_consts(l, w, mod_rows, win, wout_g.reshape(D_MODEL, D_MODEL), conv_w_full)

    def ffn_operands(l, modv, g1024, wup_g, wdn_g):
        wdn = wdn_g.reshape(FF_PAIRS, FF_SHARD, D_MODEL)
        return modv, g1024, wup_g, wdn, ffn_cw_full[:, l], ffn_conv_b[l].reshape(N_DEV, 1, FF_SHARD)

    w0, w1 = bf16_shards
    c8 = jnp.broadcast_to(c, (8, D_MODEL))
    mod_b8 = jnp.pad(mod_b, ((0, 8 - N_LAYERS), (0, 0)))
    sc_all, mod_rows, win0_g, wout0_g, conv_w_g, ffn_cw_full = _gather_weights(
        c8, mod_w, mod_b8, [("gather2", w0[0]), ("gather2", w0[1]), ("gather", conv_w), ("gather", ffn_conv_w)])
    conv_w_full = conv_w_g.transpose(1, 2, 0, 3).reshape(N_LAYERS, CONV_K, CONV_WIDTH)

    modv0, g0, mix_consts0 = mixer_operands(0, win0_g, wout0_g)
    (x1, z, o, cv), (wup0_g, wdn0_g) = _mixer_fwd(xs, *mix_consts0, name="mixer_fwd_l0", natural_x=True,
                                                  job=[("gather2", w0[2]), ("gather2", w0[3])])
    ffn_consts0 = ffn_operands(0, modv0, g0, wup0_g, wdn0_g)
    (x2, y2, p, u), (win1_g, wout1_g, wdn1_g) = _ffn_fwd(
        x1, *ffn_consts0, name="ffn_fwd_l0", job=[("gather2", w1[0]), ("gather2", w1[1]), ("gather2", w1[3])])
    saved = [(xs, z, o, cv, x1, y2, p, u)]
    modv1, g1, mix_consts1 = mixer_operands(1, win1_g, wout1_g)
    (x1, z, o, cv), (wup1_g,) = _mixer_fwd(x2, *mix_consts1, name="mixer_fwd_l1", job=[("gather2", w1[2])])
    ffn_consts1 = ffn_operands(1, modv1, g1, wup1_g, wdn1_g)
    (dh, y2, p, u, loss_tile), _ = _ffn_fwd(x1, *ffn_consts1, name="ffn_fwd_l1", loss_target=loss_target[0])
    saved.append((x2, z, o, cv, x1, y2, p, u))
    loss = lax.psum(loss_tile[0, 0], ("x", "y", "c"))

    def ffn_weight_grads(l, ffn):
        dp, a, dy2, h2 = ffn[1:5]
        d_up = _wgrad(dp, h2, f"wgrad_ffn_up_l{l}", tk=WGRAD_TK_FFN)
        d_dn = _wgrad(a, dy2, f"wgrad_ffn_down_l{l}", tk=WGRAD_TK_FFN).reshape(N_DEV, D_FF // N_DEV, D_MODEL)
        return d_up, d_dn

    def mixer_weight_grads(l, mix):
        dz, do, ycat, h1 = mix[1:5]
        d_in = _wgrad(dz[None], h1, f"wgrad_w_in_l{l}").reshape(N_DEV, IN_WIDTH // N_DEV, D_MODEL)
        d_out = _wgrad(ycat, do[None], f"wgrad_w_out_l{l}").reshape(N_DEV, D_MODEL // N_DEV, D_MODEL)
        return d_in, d_out

    x_in, z, o, cv, x1, y2, p, u = saved[1]
    ffn1, _ = _ffn_bwd(dh, x1, y2, p, u, *ffn_consts1, name="ffn_bwd_l1")
    d_up1, d_dn1 = ffn_weight_grads(1, ffn1)
    mix1, (sib_up1, sib_dn1) = _mixer_bwd(ffn1[0], x_in, o, z, cv, *mix_consts1, name="mixer_bwd_l1",
                                          job=[("scatter_p1", d_up1), ("scatter_p1", d_dn1)])
    chip_up1 = _pair_add(d_up1, sib_up1, 176, "pair_add_ffn_up_l1")
    chip_dn1 = _pair_add(d_dn1, sib_dn1, 176, "pair_add_ffn_down_l1")
    d_in1, d_out1 = mixer_weight_grads(1, mix1)
    small1 = _small_grad_rows(mix1, ffn1).astype(BF16)

    x_in, z, o, cv, x1, y2, p, u = saved[0]
    ffn0, job_out = _ffn_bwd(mix1[0], x1, y2, p, u, *ffn_consts0, name="ffn_bwd_l0",
                             job=[("scatter", d_in1), ("scatter", d_out1), ("scatter_p2", chip_up1),
                                  ("scatter_p2", chip_dn1), ("gather", small1)])
    recv1, small_all1 = job_out[0:4], job_out[4]
    dp, a, dy2, h2 = ffn0[1:5]
    d_dn0 = _wgrad(a, dy2, "wgrad_ffn_down_l0", tk=WGRAD_TK_FFN).reshape(N_DEV, D_FF // N_DEV, D_MODEL)
    d_up0, (recv_dn0,) = _wgrad(dp, h2, "wgrad_ffn_up_l0", tk=WGRAD_TK_FFN, job=[("scatter", d_dn0)])
    mix0, (recv_up0,) = _mixer_bwd(ffn0[0], x_in, o, z, cv, *mix_consts0, name="mixer_bwd_l0", natural_x=True,
                                   job=[("scatter", d_up0)])
    recv_ffn0 = (recv_up0, recv_dn0)
    grad_x = mix0[0][None]
    dz, do, ycat, h1 = mix0[1:5]
    small0 = _small_grad_rows(mix0, ffn0).astype(BF16)
    d_in0, (small_all0,) = _wgrad(dz[None], h1, "wgrad_w_in_l0", job=[("gather", small0)])
    d_in0 = d_in0.reshape(N_DEV, IN_WIDTH // N_DEV, D_MODEL)
    d_out0, (recv_in0,) = _wgrad(ycat, do[None], "wgrad_w_out_l0", job=[("scatter", d_in0)])
    d_out0 = d_out0.reshape(N_DEV, D_MODEL // N_DEV, D_MODEL)
    small_total, g_mod_w, (recv_out0,) = _small_sums(sc_all, small_all0, small_all1, [("scatter", d_out0)])
    recv0 = [recv_in0, recv_out0, recv_ffn0[0], recv_ffn0[1]]

    grads, deltas, new_m, new_v = {}, {}, {}, {}
    for j, (name, chunk) in enumerate((("w_in", 224), ("w_out", 128), ("ffn_up", 176), ("ffn_down", 176))):
        outs = _adam_sharded(recv0[j], recv1[j], wt[name], mt[name], vt[name], chunk, "adam_" + name)
        if name in transposed:
            outs = [jnp.swapaxes(t, 1, 2) for t in outs]
        grads[name], deltas[name], new_m[name], new_v[name] = outs
    grads["mod_w"], deltas["mod_w"], new_m["mod_w"], new_v["mod_w"] = _adam_dense(
        g_mod_w, mod_w, m_mod_w, v_mod_w, 256, "adam_mod_w")

    small_g = _small_grads_from_rows(small_total)
    small_g["conv_w"] = lax.dynamic_slice_in_dim(small_g["conv_w"], me * conv_w.shape[2], conv_w.shape[2], axis=2)
    small_g["ffn_conv_w"] = lax.dynamic_slice_in_dim(small_g["ffn_conv_w"], me * FF_SHARD, FF_SHARD, axis=2)
    shapes = [w[n].shape for n in SMALL_PACKED]
    packs = [_pack_rows([src[n] for n in SMALL_PACKED])[None] for src in (small_g, w, m, v)]
    _, d, m2, v2 = _adam_dense(*packs, packs[0].shape[1], "adam_small")
    for name, dd, mm, vv in zip(SMALL_PACKED, _unpack_rows(d[0], shapes), _unpack_rows(m2[0], shapes),
                                _unpack_rows(v2[0], shapes)):
        grads[name], deltas[name], new_m[name], new_v[name] = small_g[name], dd, mm, vv

    return (loss, grad_x, *[grads[n] for n in WEIGHT_NAMES], *[deltas[n] for n in WEIGHT_NAMES],
            *[new_m[n] for n in WEIGHT_NAMES], *[new_v[n] for n in WEIGHT_NAMES])
```

```python
import functools
import math

import jax
import jax.numpy as jnp
from jax import lax
from jax.experimental import pallas as pl
from jax.experimental.pallas import tpu as pltpu

F32 = jnp.float32
BF16 = jnp.bfloat16

D_MODEL = 1024
N_DEV = 8
SGU_WIDTH = 384
CONV_WIDTH = 384
POOL_WIDTH = 256
HEAD_DIM = 64
SGU_HEADS = 6
CHUNK = 128
CONV_K = 31
POOL_WINDOWS = (2, 4, 8, 16)
IN_WIDTH = 1792
D_FF = 2816
FF_SHARD = 2 * D_FF // N_DEV
FF_PAIRS = N_DEV // 2
FFN_CONV_K = 3
EPS = 1e-6
GELU_C0 = math.sqrt(2.0 / math.pi)
GELU_C1 = 0.044715

ADAM_LR = 0.001
ADAM_B1 = 0.9
ADAM_B2 = 0.999
ADAM_EPS = 1e-08
ADAM_WD = 0.01
ADAM_STEP = 10

VMEM_LIMIT_BYTES = 56 * 1024 * 1024
TILE = 256
MIX_TILE = TILE
FFN_TILE = TILE
FFN_HALO = 8 * (FFN_CONV_K - 1)
POOL_HALO = 8 * POOL_WINDOWS[-1]
WGRAD_TK = 2048
WGRAD_TK_FFN = 4096


def _cparams(n_axes):
    return pltpu.CompilerParams(dimension_semantics=("arbitrary",) * n_axes, vmem_limit_bytes=VMEM_LIMIT_BYTES)


def _whole(shape):
    nd = len(shape)
    return pl.BlockSpec(shape, lambda *_: (0,) * nd, pipeline_mode=pl.Buffered(1))


def _dot(a, b):
    return jnp.dot(a, b, preferred_element_type=F32)


def _dot_nt(a, b):
    return lax.dot_general(a, b, (((1,), (1,)), ((), ())), preferred_element_type=F32)


def _dot_tn(a, b):
    return lax.dot_general(a, b, (((0,), (0,)), ((), ())), preferred_element_type=F32)


def _gelu(x):
    t = jnp.tanh(GELU_C0 * (x + GELU_C1 * x * x * x))
    return 0.5 * x * (1.0 + t), t


def _gelu_grad(x, t):
    return 0.5 * (1.0 + t) + 0.5 * x * (1.0 - t * t) * (GELU_C0 * (1.0 + 3.0 * GELU_C1 * x * x))


def _rowmean(x):
    return jnp.mean(x, axis=-1, keepdims=True)


def _colsum(x):
    return jnp.sum(x, axis=0, keepdims=True)


def _rms_fwd(x):
    r = lax.rsqrt(_rowmean(x * x) + EPS)
    return x * r, r


def _rms_bwd(dxhat, xhat, r):
    return r * (dxhat - xhat * _rowmean(dxhat * xhat))


N_PEERS = N_DEV - 1
ANY = pl.BlockSpec(memory_space=pl.ANY)
VMEM = pl.BlockSpec(memory_space=pltpu.VMEM)


def _my_pos():
    return lax.axis_index("x"), lax.axis_index("y"), lax.axis_index("c")


def _peer(pos, k):
    x, y, c = pos
    return (1 - x if k & 4 else x, 1 - y if k & 2 else y, 1 - c if k & 1 else c)


def _flat(pos):
    return 4 * pos[0] + 2 * pos[1] + pos[2]


def _remote_copy(src, dst, send_sem, recv_sem, peer):
    return pltpu.make_async_remote_copy(src_ref=src, dst_ref=dst, send_sem=send_sem, recv_sem=recv_sem,
                                        device_id=peer, device_id_type=pl.DeviceIdType.MESH)


N_CHIPS = N_DEV // 2
SIBLING = 1
SAME_CORE_PEERS = (2, 4, 6)


def _exchange_out_shapes(job):
    def shape(kind, a):
        if kind in ("gather", "gather2"):
            return (N_DEV,) + a.shape
        if kind == "scatter_p1":
            return (N_CHIPS,) + a.shape[1:]
        return a.shape
    return [jax.ShapeDtypeStruct(shape(kind, a), a.dtype) for kind, a in job]


def _exchange_sems(n):
    return [pltpu.SemaphoreType.DMA((n, N_PEERS)), pltpu.SemaphoreType.DMA((n, N_PEERS)), pltpu.SemaphoreType.DMA((n,))]


def _exchange_copies(kinds, src_refs, dst_refs, send_sems, recv_sems, local_sems, phase):
    pos = _my_pos()
    me = _flat(pos)
    chip, core = 2 * pos[0] + pos[1], pos[2]
    copies = []

    def remote(a, src, dst, k, sem=None):
        sem = k - 1 if sem is None else sem
        copies.append(_remote_copy(src, dst, send_sems.at[a, sem], recv_sems.at[a, sem], _peer(pos, k)))

    for a, kind in enumerate(kinds):
        src, dst = src_refs[a], dst_refs[a]
        if phase == 1:
            if kind == "gather2":
                for k in SAME_CORE_PEERS:
                    remote(a, dst.at[me ^ k], dst.at[me ^ k], SIBLING, sem=k)
        elif kind in ("gather", "gather2"):
            copies.append(pltpu.make_async_copy(src, dst.at[me], local_sems.at[a]))
            for k in (range(1, N_DEV) if kind == "gather" else (SIBLING,) + SAME_CORE_PEERS):
                remote(a, src, dst.at[me], k)
        elif kind == "scatter":
            copies.append(pltpu.make_async_copy(src.at[me], dst.at[me], local_sems.at[a]))
            for k in range(1, N_DEV):
                remote(a, src.at[me ^ k], dst.at[me], k)
        elif kind == "scatter_p1":
            for q in range(N_CHIPS):
                remote(a, src.at[2 * q + 1 - core], dst.at[q], SIBLING, sem=q)
        elif kind == "scatter_p2":
            copies.append(pltpu.make_async_copy(src.at[chip], dst.at[chip], local_sems.at[a]))
            for k in SAME_CORE_PEERS:
                remote(a, src.at[chip ^ (k >> 1)], dst.at[chip], k)
    return copies


def _pallas_call_with_exchange(body, *, grid, in_specs, out_specs, out_shape, scratch_shapes, operands, name, job):
    params = _cparams(len(grid))
    if not job:
        outs = pl.pallas_call(body, grid=grid, in_specs=in_specs, out_specs=out_specs, out_shape=out_shape,
                              scratch_shapes=scratch_shapes, compiler_params=params, name=name)(*operands)
        return outs, []
    kinds = [kind for kind, _ in job]
    relayed = [kind if kind == "gather2" else None for kind in kinds]
    unrelayed = [None if kind == "gather2" else kind for kind in kinds]
    n, n_in, n_out, n_scr = len(job), len(in_specs), len(out_specs), len(scratch_shapes)
    n_steps = math.prod(grid)
    relay_step = max(n_steps - 2, 0)

    def wrapped(*refs):
        ins, jin = refs[:n_in], refs[n_in:n_in + n]
        outs, jout = refs[n_in + n:n_in + n + n_out], refs[n_in + n + n_out:n_in + 2 * n + n_out]
        scr = refs[n_in + 2 * n + n_out:n_in + 2 * n + n_out + n_scr]
        sems = refs[n_in + 2 * n + n_out + n_scr:]
        step = pl.program_id(0)
        for d in range(1, len(grid)):
            step = step * grid[d] + pl.program_id(d)

        def copies(which, phase):
            return _exchange_copies(which, jin, jout, *sems, phase=phase)

        @pl.when(step == 0)
        def _():
            for cp in copies(kinds, 0):
                cp.start()

        body(*ins, *outs, *scr)

        @pl.when(step == relay_step)
        def _():
            for cp in copies(relayed, 0):
                cp.wait()
            for cp in copies(relayed, 1):
                cp.start()

        @pl.when(step == n_steps - 1)
        def _():
            for cp in copies(unrelayed, 0) + copies(relayed, 1):
                cp.wait()

    res = pl.pallas_call(
        wrapped, grid=grid,
        in_specs=list(in_specs) + [ANY] * n,
        out_specs=list(out_specs) + [ANY] * n,
        out_shape=list(out_shape) + _exchange_out_shapes(job),
        scratch_shapes=list(scratch_shapes) + _exchange_sems(n),
        compiler_params=params, name=name,
    )(*operands, *[a for _, a in job])
    return res[:n_out], res[n_out:]


def _seg_mean(x, segp):
    hi = x.astype(BF16)
    lo = (x - hi.astype(F32)).astype(BF16)
    return _dot(hi, segp) + _dot(lo, segp)


def _rot_rows(x, shift):
    m, c = x.shape
    return pltpu.roll(x.reshape(m // 8, 8, c), shift, 1).reshape(m, c)


def _sublane_is(shape, s):
    return lax.broadcasted_iota(jnp.int32, shape, 0) % 8 == s


def _causal_tail(tail, prev_rot):
    rot = _rot_rows(tail, 1)
    return jnp.where(_sublane_is(tail.shape, 0), prev_rot, rot), rot


def _lookahead_head(head, next_rot):
    rot = _rot_rows(head, 7)
    return jnp.where(_sublane_is(head.shape, 7), next_rot, rot), rot


def _tile_token_index(t, tile_idx):
    r = lax.broadcasted_iota(jnp.int32, (t, 1), 0)
    return tile_idx * t + (r % 8) * (t // 8) + r // 8


def _interleave(x):
    t, c = x.shape
    return jnp.swapaxes(x.reshape(8, t // 8, c), 0, 1).reshape(t, c)


def _deinterleave(x):
    t, c = x.shape
    return jnp.swapaxes(x.reshape(t // 8, 8, c), 0, 1).reshape(t, c)


def _ffn_fwd(x1, modv, g1024, wup, wdn, cw, cb, name="ffn_fwd", job=None, loss_target=None):
    s_len = x1.shape[0]
    t = FFN_TILE
    n_tiles = s_len // t
    with_loss = loss_target is not None

    def body(x1_ref, *rest):
        if with_loss:
            tgt_ref, rest = rest[0], rest[1:]
            loss_ref, rest = rest[10], rest[:10] + rest[11:]
        mod_ref, g_ref, wup_ref, wdn_ref, cw_ref, cb_ref, x2_ref, y2_ref, p_ref, u_ref, ext_ref, carry_ref = rest
        i = pl.program_id(0)

        @pl.when(i == 0)
        def _():
            carry_ref[...] = jnp.zeros_like(carry_ref)
            if with_loss:
                loss_ref[...] = jnp.zeros_like(loss_ref)

        x1v = x1_ref[...]
        pre_g, post_g = g_ref[3:4, :], g_ref[4:5, :]
        sh2, sc2, g2 = mod_ref[3:4, :], mod_ref[4:5, :], mod_ref[5:6, :]
        xhat, _ = _rms_fwd(x1v)
        h2b = (xhat * pre_g * (1.0 + sc2) + sh2).astype(BF16)

        def conv_shard(s):
            p = _dot_nt(h2b, wup_ref[s])
            p_ref[s] = p.astype(BF16)
            ext_ref[0:FFN_HALO, :], carry_ref[s] = _causal_tail(p[t - FFN_HALO:t, :], carry_ref[s])
            ext_ref[FFN_HALO:FFN_HALO + t, :] = p
            w = cw_ref[s]
            u = w[0:1, :] * ext_ref[0:t, :] + w[1:2, :] * ext_ref[8:8 + t, :] + w[2:3, :] * p + cb_ref[s]
            u_ref[s] = u.astype(BF16)
            return u

        y2 = jnp.zeros((t, D_MODEL), F32)
        for j in range(FF_PAIRS):
            ug = conv_shard(j)
            uv = conv_shard(j + FF_PAIRS)
            ge, _ = _gelu(ug)
            y2 = y2 + _dot((ge * uv).astype(BF16), wdn_ref[j])
        y2_ref[...] = y2
        yhat, _ = _rms_fwd(y2)
        x2 = x1v + g2 * (yhat * post_g)
        if with_loss:
            diff = x2 - _interleave(tgt_ref[...])
            x2_ref[...] = diff * (1.0 / D_MODEL)
            loss_ref[...] += (0.5 / D_MODEL) * jnp.sum(diff * diff)
        else:
            x2_ref[...] = x2

    tile = pl.BlockSpec((t, D_MODEL), lambda i: (i, 0))
    consts = (modv, g1024, wup, wdn, cw, cb)
    shards = pl.BlockSpec((N_DEV, t, FF_SHARD), lambda i: (0, i, 0))
    out_specs = [tile, tile, shards, shards]
    out_shape = [jax.ShapeDtypeStruct((s_len, D_MODEL), F32), jax.ShapeDtypeStruct((s_len, D_MODEL), F32),
                 jax.ShapeDtypeStruct((N_DEV, s_len, FF_SHARD), BF16),
                 jax.ShapeDtypeStruct((N_DEV, s_len, FF_SHARD), BF16)]
    if with_loss:
        out_specs.append(pl.BlockSpec((8, 128), lambda i: (0, 0)))
        out_shape.append(jax.ShapeDtypeStruct((8, 128), F32))
    return _pallas_call_with_exchange(
        body,
        grid=(n_tiles,),
        in_specs=[tile] * (2 if with_loss else 1) + [_whole(c.shape) for c in consts],
        out_specs=out_specs,
        out_shape=out_shape,
        scratch_shapes=[pltpu.VMEM((FFN_HALO + t, FF_SHARD), F32), pltpu.VMEM((N_DEV, FFN_HALO, FF_SHARD), F32)],
        operands=(x1,) + ((loss_target,) if with_loss else ()) + consts,
        name=name, job=job)


def _ffn_bwd(dx2, x1, y2, p, u, modv, g1024, wup, wdn, cw, cb, name="ffn_bwd", job=None):
    s_len = x1.shape[0]
    t = FFN_TILE
    n_tiles = s_len // t
    hb = FFN_HALO

    def body(dx2_ref, x1_ref, y2_ref, p_ref, ph_ref, u_ref, mod_ref, g_ref, wup_ref, wdn_ref, cw_ref, cb_ref,
             dx1_ref, dp_ref, a_ref, dy2_ref, h2_ref, vec_ref, cgrad_ref, ext_ref, dext_ref, dcarry_ref):
        i = pl.program_id(0)
        tile_idx = n_tiles - 1 - i

        @pl.when(i == 0)
        def _():
            vec_ref[...] = jnp.zeros_like(vec_ref)
            cgrad_ref[...] = jnp.zeros_like(cgrad_ref)
            dcarry_ref[...] = jnp.zeros_like(dcarry_ref)

        dx2v, x1v, y2v = dx2_ref[...], x1_ref[...], y2_ref[...]
        pre_g, post_g = g_ref[3:4, :], g_ref[4:5, :]
        sh2, sc2, g2 = mod_ref[3:4, :], mod_ref[4:5, :], mod_ref[5:6, :]

        yhat, ry = _rms_fwd(y2v)
        vec_ref[1:2, :] += _colsum(dx2v * (yhat * post_g))
        dyn = dx2v * g2
        vec_ref[0:1, :] += _colsum(dyn * yhat)
        dy2b = _rms_bwd(dyn * post_g, yhat, ry).astype(BF16)
        dy2_ref[...] = dy2b

        xhat, rx = _rms_fwd(x1v)
        xn = xhat * pre_g
        h2_ref[...] = (xn * (1.0 + sc2) + sh2).astype(BF16)

        not_first = (tile_idx > 0).astype(F32)

        def recompute(s, slot):
            pf = p_ref[s].astype(F32)
            prev_rot = _rot_rows(ph_ref[s].astype(F32), 1) * not_first
            ext_ref[slot, 0:hb, :], _ = _causal_tail(pf[t - hb:t, :], prev_rot)
            ext_ref[slot, hb:hb + t, :] = pf
            return u_ref[s].astype(F32)

        def conv_bwd(s, slot, du):
            w = cw_ref[s]
            cgrad_ref[s, 0:1, :] += _colsum(du * ext_ref[slot, 0:t, :])
            cgrad_ref[s, 1:2, :] += _colsum(du * ext_ref[slot, 8:8 + t, :])
            cgrad_ref[s, 2:3, :] += _colsum(du * ext_ref[slot, 16:16 + t, :])
            cgrad_ref[s, 3:4, :] += _colsum(du)
            dext_ref[0:t, :] = du
            dext_ref[t:t + hb, :], dcarry_ref[s] = _lookahead_head(du[0:hb, :], dcarry_ref[s])
            dp = w[2:3, :] * du + w[1:2, :] * dext_ref[8:8 + t, :] + w[0:1, :] * dext_ref[16:16 + t, :]
            dpb = dp.astype(BF16)
            dp_ref[s] = dpb
            return _dot(dpb, wup_ref[s])

        dh2 = jnp.zeros((t, D_MODEL), F32)
        for j in range(FF_PAIRS):
            ug = recompute(j, 0)
            uv = recompute(j + FF_PAIRS, 1)
            ge, th = _gelu(ug)
            a_ref[j] = (ge * uv).astype(BF16)
            da = _dot_nt(dy2b, wdn_ref[j])
            dh2 = dh2 + conv_bwd(j, 0, da * uv * _gelu_grad(ug, th))
            dh2 = dh2 + conv_bwd(j + FF_PAIRS, 1, da * ge)

        vec_ref[2:3, :] += _colsum(dh2)
        vec_ref[3:4, :] += _colsum(dh2 * xn)
        dxn = dh2 * (1.0 + sc2)
        vec_ref[4:5, :] += _colsum(dxn * xhat)
        dx1_ref[...] = dx2v + _rms_bwd(dxn * pre_g, xhat, rx)

    rev = lambda i: (n_tiles - 1 - i, 0)
    tile = pl.BlockSpec((t, D_MODEL), rev)
    halo_idx = lambda i: (0, jnp.maximum((n_tiles - 1 - i) * (t // hb) - 1, 0), 0)
    return _pallas_call_with_exchange(
        body,
        grid=(n_tiles,),
        in_specs=[tile, tile, tile,
                  pl.BlockSpec((N_DEV, t, FF_SHARD), lambda i: (0, n_tiles - 1 - i, 0)),
                  pl.BlockSpec((N_DEV, hb, FF_SHARD), halo_idx),
                  pl.BlockSpec((N_DEV, t, FF_SHARD), lambda i: (0, n_tiles - 1 - i, 0)),
                  _whole(modv.shape), _whole(g1024.shape), _whole(wup.shape), _whole(wdn.shape),
                  _whole(cw.shape), _whole(cb.shape)],
        out_specs=[tile,
                   pl.BlockSpec((N_DEV, t, FF_SHARD), lambda i: (0, n_tiles - 1 - i, 0)),
                   pl.BlockSpec((FF_PAIRS, t, FF_SHARD), lambda i: (0, n_tiles - 1 - i, 0)),
                   tile, tile,
                   pl.BlockSpec((8, D_MODEL), lambda i: (0, 0)),
                   pl.BlockSpec((N_DEV, 8, FF_SHARD), lambda i: (0, 0, 0))],
        out_shape=[jax.ShapeDtypeStruct((s_len, D_MODEL), F32),
                   jax.ShapeDtypeStruct((N_DEV, s_len, FF_SHARD), BF16),
                   jax.ShapeDtypeStruct((FF_PAIRS, s_len, FF_SHARD), BF16),
                   jax.ShapeDtypeStruct((s_len, D_MODEL), BF16),
                   jax.ShapeDtypeStruct((s_len, D_MODEL), BF16),
                   jax.ShapeDtypeStruct((8, D_MODEL), F32),
                   jax.ShapeDtypeStruct((N_DEV, 8, FF_SHARD), F32)],
        scratch_shapes=[pltpu.VMEM((2, hb + t, FF_SHARD), F32), pltpu.VMEM((t + hb, FF_SHARD), F32),
                        pltpu.VMEM((N_DEV, hb, FF_SHARD), F32)],
        operands=(dx2, x1, y2, p, p, u, modv, g1024, wup, wdn, cw, cb),
        name=name, job=job)


def _wgrad(a, b, name, tk=WGRAD_TK, job=None):
    a_grouped, b_grouped = a.ndim == 3, b.ndim == 3
    groups = a.shape[0] if a_grouped else b.shape[0]
    s_len, m, n = a.shape[-2], a.shape[-1], b.shape[-1]
    tk = min(tk, s_len)
    n_k = s_len // tk

    def body(a_ref, b_ref, o_ref, acc_ref):
        k = pl.program_id(1)
        av = a_ref[0] if a_grouped else a_ref[...]
        bv = b_ref[0] if b_grouped else b_ref[...]
        part = _dot_tn(av, bv)
        if n_k == 1:
            o_ref[0] = part.astype(BF16)
            return

        @pl.when(k == 0)
        def _():
            acc_ref[...] = part

        @pl.when(jnp.logical_and(k > 0, k < n_k - 1))
        def _():
            acc_ref[...] += part

        @pl.when(k == n_k - 1)
        def _():
            o_ref[0] = (acc_ref[...] + part).astype(BF16)

    a_spec = pl.BlockSpec((1, tk, m), lambda g, k: (g, k, 0)) if a_grouped else pl.BlockSpec((tk, m), lambda g, k: (k, 0))
    b_spec = pl.BlockSpec((1, tk, n), lambda g, k: (g, k, 0)) if b_grouped else pl.BlockSpec((tk, n), lambda g, k: (k, 0))
    (out,), exchanged = _pallas_call_with_exchange(
        body,
        grid=(groups, n_k),
        in_specs=[a_spec, b_spec],
        out_specs=[pl.BlockSpec((1, m, n), lambda g, k: (g, 0, 0))],
        out_shape=[jax.ShapeDtypeStruct((groups, m, n), BF16)],
        scratch_shapes=[pltpu.VMEM((m, n), F32)],
        operands=(a, b),
        name=name, job=job)
    return (out, exchanged) if job else out


def _lane(shape):
    return lax.broadcasted_iota(jnp.int32, shape, 1)


def _by_pool_group(shape, vals):
    lane = _lane(shape)
    return jnp.where(lane < 64, vals[0], jnp.where(lane < 128, vals[1], jnp.where(lane < 192, vals[2], vals[3])))


def _pool_inv_counts(t, tile_idx):
    pos1 = _tile_token_index(t, tile_idx) + 1
    return [1.0 / jnp.minimum(pos1, w).astype(F32) for w in POOL_WINDOWS]


def _sgu_keep_mask(t):
    tok_r = _tile_token_index(t, 0)
    c = lax.broadcasted_iota(jnp.int32, (1, t), 1)
    tok_c = (c % 8) * (t // 8) + c // 8
    return jnp.logical_and(tok_r // CHUNK == tok_c // CHUNK, tok_r >= tok_c)


def _masked_sgu_w(sguw_ref):
    keep = _sgu_keep_mask(sguw_ref.shape[1])
    return [jnp.where(keep, sguw_ref[h], 0.0).astype(BF16) for h in range(SGU_HEADS)]


def _branches_fwd(z, tile_idx, p384_ref, cw_ref, wm, bmat_ref, pwbd_ref, psc_ref, segp_ref, g_ref, hext_ref, zext_ref,
                  h_prev_rot, z_prev_rot, conv_saved=None):
    t = z.shape[0]
    segp = segp_ref[...]
    r = {}
    u, _ = _gelu(z[:, 0:SGU_WIDTH])
    vraw, _ = _gelu(z[:, SGU_WIDTH:2 * SGU_WIDTH])
    xc = vraw - _seg_mean(vraw, segp)
    rstd_v = lax.rsqrt(_seg_mean(xc * xc, segp) + EPS)
    xh_v = xc * rstd_v
    vnb = (xh_v * p384_ref[0:1, :] + p384_ref[1:2, :]).astype(BF16)
    first_head = _lane((t, 128)) < HEAD_DIM
    f_pairs = []
    for pr in range(SGU_HEADS // 2):
        vp = vnb[:, pr * 128:(pr + 1) * 128]
        f_pairs.append(jnp.where(first_head, _dot(wm[2 * pr], vp), _dot(wm[2 * pr + 1], vp)))
    f = jnp.concatenate(f_pairs, axis=1) + bmat_ref[...]
    ya = u * f
    r.update(u=u, xh_v=xh_v, rstd_v=rstd_v, vnb=vnb, f=f)
    o_b = 2 * SGU_WIDTH
    a_in = z[:, o_b:o_b + CONV_WIDTH]
    sig_g = jax.nn.sigmoid(z[:, o_b + CONV_WIDTH:o_b + 2 * CONV_WIDTH])
    hh = a_in * sig_g
    hext_ref[0:t, :], r["h_rot"] = _causal_tail(hh, h_prev_rot)
    hext_ref[t:2 * t, :] = hh
    if conv_saved is None:
        conv = jnp.zeros((t, CONV_WIDTH), F32) + p384_ref[2:3, :]
        for k in range(CONV_K):
            conv = conv + cw_ref[k:k + 1, :] * hext_ref[pl.ds(t - 8 * (CONV_K - 1 - k), t), :]
        r["conv"] = conv
    else:
        conv = conv_saved
    cc = conv - _rowmean(conv)
    rstd_c = lax.rsqrt(_rowmean(cc * cc) + EPS)
    xh_c = cc * rstd_c
    cn = xh_c * p384_ref[3:4, :] + p384_ref[4:5, :]
    sig_c = jax.nn.sigmoid(cn)
    yb = cn * sig_c
    r.update(a_in=a_in, sig_g=sig_g, xh_c=xh_c, rstd_c=rstd_c, cn=cn, sig_c=sig_c)
    o_c = o_b + 2 * CONV_WIDTH
    zc = z[:, o_c:o_c + POOL_WIDTH]
    zext_ref[0:POOL_HALO, :], r["z_rot"] = _causal_tail(zc[t - POOL_HALO:t, :], z_prev_rot)
    zext_ref[POOL_HALO:POOL_HALO + t, :] = zc
    sums, acc = [], zc
    for j in range(1, POOL_WINDOWS[-1]):
        acc = acc + zext_ref[pl.ds(POOL_HALO - 8 * j, t), :]
        if j + 1 in POOL_WINDOWS:
            sums.append(acc)
    inv = _pool_inv_counts(t, tile_idx)
    dpool = _by_pool_group((t, POOL_WIDTH), [s * iv for s, iv in zip(sums, inv)]) - zc
    ycp = _dot(dpool.astype(BF16), pwbd_ref[...])
    yc = ycp * psc_ref[0:1, :]
    r.update(dpool=dpool, ycp=ycp)
    yha, ra = _rms_fwd(ya)
    yhb, rb = _rms_fwd(yb)
    yhc, rc = _rms_fwd(yc)
    bg = g_ref[2:3, :]
    ycat = jnp.concatenate([yha * bg[:, 0:384], yhb * bg[:, 384:768], yhc * bg[:, 768:1024]], axis=1)
    r.update(yha=yha, ra=ra, yhb=yhb, rb=rb, yhc=yhc, rc=rc, ycat=ycat)
    return r


def _mixer_fwd(x, modv, g1024, p384, cw, sguw, bmat, pwbd, psc, segp, win, wout, name="mixer_fwd", job=None,
               natural_x=False):
    s_len = x.shape[0]
    t = MIX_TILE

    def body(x_ref, mod_ref, g_ref, p384_ref, cw_ref, sguw_ref, bmat_ref, pwbd_ref, psc_ref, segp_ref, win_ref, wout_ref,
             x1_ref, z_ref, o_ref, conv_ref, hext_ref, zext_ref, hrot_ref, zrot_ref, wm_ref):
        i = pl.program_id(0)

        @pl.when(i == 0)
        def _():
            hrot_ref[...] = jnp.zeros_like(hrot_ref)
            zrot_ref[...] = jnp.zeros_like(zrot_ref)
            for h, wmh in enumerate(_masked_sgu_w(sguw_ref)):
                wm_ref[h] = wmh

        xv = _interleave(x_ref[...]) if natural_x else x_ref[...]
        sh1, sc1, g1 = mod_ref[0:1, :], mod_ref[1:2, :], mod_ref[2:3, :]
        xhat, _ = _rms_fwd(xv)
        h1 = xhat * g_ref[0:1, :] * (1.0 + sc1) + sh1
        z = _dot_nt(h1.astype(BF16), win_ref[...])
        z_ref[...] = z
        r = _branches_fwd(z, i, p384_ref, cw_ref, [wm_ref[h] for h in range(SGU_HEADS)], bmat_ref, pwbd_ref, psc_ref,
                          segp_ref, g_ref, hext_ref, zext_ref, hrot_ref[...], zrot_ref[...])
        hrot_ref[...] = r["h_rot"]
        zrot_ref[...] = r["z_rot"]
        conv_ref[...] = r["conv"]
        o = _dot(r["ycat"].astype(BF16), wout_ref[...])
        o_ref[...] = o
        ohat, _ = _rms_fwd(o)
        x1_ref[...] = xv + g1 * (ohat * g_ref[1:2, :])

    tile = pl.BlockSpec((t, D_MODEL), lambda i: (i, 0))
    consts = (modv, g1024, p384, cw, sguw, bmat, pwbd, psc, segp, win, wout)
    return _pallas_call_with_exchange(
        body,
        grid=(s_len // t,),
        in_specs=[tile] + [_whole(c.shape) for c in consts],
        out_specs=[tile, pl.BlockSpec((t, IN_WIDTH), lambda i: (i, 0)), tile,
                   pl.BlockSpec((t, CONV_WIDTH), lambda i: (i, 0))],
        out_shape=[jax.ShapeDtypeStruct((s_len, D_MODEL), F32), jax.ShapeDtypeStruct((s_len, IN_WIDTH), F32),
                   jax.ShapeDtypeStruct((s_len, D_MODEL), F32), jax.ShapeDtypeStruct((s_len, CONV_WIDTH), F32)],
        scratch_shapes=[pltpu.VMEM((2 * t, CONV_WIDTH), F32), pltpu.VMEM((POOL_HALO + t, POOL_WIDTH), F32),
                        pltpu.VMEM((t, CONV_WIDTH), F32), pltpu.VMEM((POOL_HALO, POOL_WIDTH), F32),
                        pltpu.VMEM((SGU_HEADS, t, t), BF16)],
        operands=(x, *consts),
        name=name, job=job)


def _mixer_bwd(dx1, x, o, z, conv, modv, g1024, p384, cw, sguw, bmat, pwbd, psc, segp, win, wout, name="mixer_bwd",
               job=None, natural_x=False):
    s_len = x.shape[0]
    t = MIX_TILE
    n_tiles = s_len // t

    def body(dx1_ref, x_ref, o_ref, z_ref, zh_ref, conv_ref, mod_ref, g_ref, p384_ref, cw_ref, sguw_ref, bmat_ref, pwbd_ref,
             psc_ref, segp_ref, win_ref, wout_ref,
             dx_ref, dz_ref, do_ref, ycat_ref, h1_ref, vec_ref, v384_ref, dcw_ref, dsguw_ref, dbmat_ref, dpw_ref,
             dpsc_ref, hext_ref, zext_ref, gext_ref, qext_ref, grot_ref, qrot_ref, wm_ref):
        i = pl.program_id(0)
        tile_idx = n_tiles - 1 - i

        @pl.when(i == 0)
        def _():
            for ref in (vec_ref, v384_ref, dcw_ref, dsguw_ref, dbmat_ref, dpw_ref, dpsc_ref, grot_ref, qrot_ref):
                ref[...] = jnp.zeros_like(ref)
            for h, wmh in enumerate(_masked_sgu_w(sguw_ref)):
                wm_ref[h] = wmh

        dx1v, ov, z = dx1_ref[...], o_ref[...], z_ref[...]
        xv = _interleave(x_ref[...]) if natural_x else x_ref[...]
        sh1, sc1, g1 = mod_ref[0:1, :], mod_ref[1:2, :], mod_ref[2:3, :]
        pre_g, post_g, bg = g_ref[0:1, :], g_ref[1:2, :], g_ref[2:3, :]
        segp = segp_ref[...]

        ohat, ro = _rms_fwd(ov)
        vec_ref[1:2, :] += _colsum(dx1v * (ohat * post_g))
        don = dx1v * g1
        vec_ref[0:1, :] += _colsum(don * ohat)
        dob = _rms_bwd(don * post_g, ohat, ro).astype(BF16)
        do_ref[...] = dob
        dycat = _dot_nt(dob, wout_ref[...])

        not_first = (tile_idx > 0).astype(F32)
        o_b = 2 * SGU_WIDTH
        o_c = o_b + 2 * CONV_WIDTH
        h_prev = zh_ref[:, o_b:o_b + CONV_WIDTH] * jax.nn.sigmoid(zh_ref[:, o_b + CONV_WIDTH:o_c])
        h_prev_rot = _rot_rows(h_prev, 1) * not_first
        z_prev_rot = _rot_rows(zh_ref[t - POOL_HALO:t, o_c:o_c + POOL_WIDTH], 1) * not_first
        wm = [wm_ref[h] for h in range(SGU_HEADS)]
        r = _branches_fwd(z, tile_idx, p384_ref, cw_ref, wm, bmat_ref, pwbd_ref, psc_ref, segp_ref, g_ref,
                          hext_ref, zext_ref, h_prev_rot, z_prev_rot, conv_saved=conv_ref[...])
        ycat_ref[...] = r["ycat"].astype(BF16)

        def branch_norm_bwd(dyn, yhat, rr, gain):
            return _colsum(dyn * yhat), _rms_bwd(dyn * gain, yhat, rr)

        dga, dya = branch_norm_bwd(dycat[:, 0:384], r["yha"], r["ra"], bg[:, 0:384])
        dgb, dyb = branch_norm_bwd(dycat[:, 384:768], r["yhb"], r["rb"], bg[:, 384:768])
        dgc, dyc = branch_norm_bwd(dycat[:, 768:1024], r["yhc"], r["rc"], bg[:, 768:1024])
        vec_ref[5:6, :] += jnp.concatenate([dga, dgb, dgc], axis=1)

        du_act = dya * r["f"]
        df = dya * r["u"]
        first_head = _lane((t, 128)) < HEAD_DIM
        dbmat_ref[...] += df
        dvn_pairs = []
        for pr in range(SGU_HEADS // 2):
            dfp = df[:, pr * 128:(pr + 1) * 128]
            df0 = jnp.where(first_head, dfp, 0.0).astype(BF16)
            df1 = jnp.where(first_head, 0.0, dfp).astype(BF16)
            vp = r["vnb"][:, pr * 128:(pr + 1) * 128]
            dvn_pairs.append(_dot_tn(wm[2 * pr], df0) + _dot_tn(wm[2 * pr + 1], df1))
            dsguw_ref[2 * pr] += _dot_nt(df0, vp)
            dsguw_ref[2 * pr + 1] += _dot_nt(df1, vp)
        dvn = jnp.concatenate(dvn_pairs, axis=1)
        v384_ref[0:1, :] += _colsum(dvn * r["xh_v"])
        v384_ref[1:2, :] += _colsum(dvn)
        dxh = dvn * p384_ref[0:1, :]
        dvraw = r["rstd_v"] * (dxh - _seg_mean(dxh, segp) - r["xh_v"] * _seg_mean(dxh * r["xh_v"], segp))
        zu, zv = z[:, 0:SGU_WIDTH], z[:, SGU_WIDTH:o_b]
        _, tu = _gelu(zu)
        _, tv = _gelu(zv)
        dz_u = du_act * _gelu_grad(zu, tu)
        dz_v = dvraw * _gelu_grad(zv, tv)

        cn, sig_c = r["cn"], r["sig_c"]
        dcn = dyb * (sig_c * (1.0 + cn * (1.0 - sig_c)))
        v384_ref[3:4, :] += _colsum(dcn * r["xh_c"])
        v384_ref[4:5, :] += _colsum(dcn)
        dxc = dcn * p384_ref[3:4, :]
        gconv = r["rstd_c"] * (dxc - _rowmean(dxc) - r["xh_c"] * _rowmean(dxc * r["xh_c"]))
        v384_ref[2:3, :] += _colsum(gconv)
        gext_ref[0:t, :] = gconv
        gext_ref[t:2 * t, :], grot_ref[...] = _lookahead_head(gconv, grot_ref[...])
        dhh = jnp.zeros((t, CONV_WIDTH), F32)
        for k in range(CONV_K):
            shift = CONV_K - 1 - k
            dcw_ref[k:k + 1, :] += _colsum(gconv * hext_ref[pl.ds(t - 8 * shift, t), :])
            dhh = dhh + cw_ref[k:k + 1, :] * gext_ref[pl.ds(8 * shift, t), :]
        sig_g = r["sig_g"]
        dz_a = dhh * sig_g
        dz_g = dhh * r["a_in"] * sig_g * (1.0 - sig_g)

        dpsc_ref[0:1, :] += _colsum(dyc * r["ycp"])
        dycp = (dyc * psc_ref[0:1, :]).astype(BF16)
        dpw_ref[...] += _dot_tn(r["dpool"].astype(BF16), dycp)
        ddp = _dot_nt(dycp, pwbd_ref[...])
        inv = _pool_inv_counts(t, tile_idx)
        q = ddp * _by_pool_group((t, POOL_WIDTH), inv)
        qext_ref[0:t, :] = q
        qext_ref[t:t + POOL_HALO, :], qrot_ref[...] = _lookahead_head(q[0:POOL_HALO, :], qrot_ref[...])
        sums, acc = [], q
        for j in range(1, POOL_WINDOWS[-1]):
            acc = acc + qext_ref[pl.ds(8 * j, t), :]
            if j + 1 in POOL_WINDOWS:
                sums.append(acc)
        dz_c = _by_pool_group((t, POOL_WIDTH), sums) - ddp

        dzb = jnp.concatenate([dz_u, dz_v, dz_a, dz_g, dz_c], axis=1).astype(BF16)
        dz_ref[...] = dzb
        dh1 = _dot(dzb, win_ref[...])

        xhat, rx = _rms_fwd(xv)
        xn = xhat * pre_g
        h1_ref[...] = (xn * (1.0 + sc1) + sh1).astype(BF16)
        vec_ref[2:3, :] += _colsum(dh1)
        vec_ref[3:4, :] += _colsum(dh1 * xn)
        dxn = dh1 * (1.0 + sc1)
        vec_ref[4:5, :] += _colsum(dxn * xhat)
        dx = dx1v + _rms_bwd(dxn * pre_g, xhat, rx)
        dx_ref[...] = _deinterleave(dx) if natural_x else dx

        @pl.when(i == n_tiles - 1)
        def _():
            keep = _sgu_keep_mask(t)
            for h in range(SGU_HEADS):
                dsguw_ref[h] = jnp.where(keep, dsguw_ref[h], 0.0)
            dbmat_ref[...] = float(HEAD_DIM) * _seg_mean(dbmat_ref[...], segp)

    rev = lambda i: (n_tiles - 1 - i, 0)
    tile = pl.BlockSpec((t, D_MODEL), rev)
    ztile = pl.BlockSpec((t, IN_WIDTH), rev)
    zhalo = pl.BlockSpec((t, IN_WIDTH), lambda i: (jnp.maximum(n_tiles - 2 - i, 0), 0))
    consts = (modv, g1024, p384, cw, sguw, bmat, pwbd, psc, segp, win, wout)
    acc = lambda shape: pl.BlockSpec(shape, lambda i: (0,) * len(shape))
    acc_shapes = [(8, D_MODEL), (8, SGU_WIDTH), (32, CONV_WIDTH), (SGU_HEADS, t, t), (t, SGU_WIDTH),
                  (POOL_WIDTH, POOL_WIDTH), (8, POOL_WIDTH)]
    return _pallas_call_with_exchange(
        body,
        grid=(n_tiles,),
        in_specs=[tile, tile, tile, ztile, zhalo, pl.BlockSpec((t, CONV_WIDTH), rev)] + [_whole(c.shape) for c in consts],
        out_specs=[tile, ztile, tile, tile, tile] + [acc(s) for s in acc_shapes],
        out_shape=[jax.ShapeDtypeStruct((s_len, D_MODEL), F32), jax.ShapeDtypeStruct((s_len, IN_WIDTH), BF16),
                   jax.ShapeDtypeStruct((s_len, D_MODEL), BF16), jax.ShapeDtypeStruct((s_len, D_MODEL), BF16),
                   jax.ShapeDtypeStruct((s_len, D_MODEL), BF16)] + [jax.ShapeDtypeStruct(s, F32) for s in acc_shapes],
        scratch_shapes=[pltpu.VMEM((2 * t, CONV_WIDTH), F32), pltpu.VMEM((POOL_HALO + t, POOL_WIDTH), F32),
                        pltpu.VMEM((2 * t, CONV_WIDTH), F32), pltpu.VMEM((t + POOL_HALO, POOL_WIDTH), F32),
                        pltpu.VMEM((t, CONV_WIDTH), F32), pltpu.VMEM((POOL_HALO, POOL_WIDTH), F32),
                        pltpu.VMEM((SGU_HEADS, t, t), BF16)],
        operands=(dx1, x, o, z, z, conv, *consts),
        name=name, job=job)


MOD_SHARD = 6 * D_MODEL // N_DEV

ROW_DMOD = 0
ROW_G1024 = 8
ROW_V384 = 16
ROW_SGU_B = 24
ROW_POOL_SCALE = 25
ROW_CONV_W = 32
ROW_FFN_CONV = 64
ROW_POOL_W = 96
ROW_SGU_W = 112
ROWS_PER_LAYER = 208
N_LAYERS = 2


def _gather_weights(c8, mod_w, mod_b8, job):
    kinds = [kind for kind, _ in job]
    shards = [a for _, a in job]
    n = len(shards)

    def body(c_ref, modw_ref, modb_ref, *rest):
        shard_refs = rest[:n]
        sc_all_ref, modrows_ref = rest[n], rest[n + 1]
        full_refs = rest[n + 2:2 * n + 2]
        send_buf, mod_recv, w_send, w_recv, w_local, sc_send, sc_recv, mod_send, mod_recv_sem = rest[2 * n + 2:]
        pos = _my_pos()
        me = _flat(pos)
        peers = [_peer(pos, k) for k in range(1, N_DEV)]

        w_copies = _exchange_copies(kinds, shard_refs, full_refs, w_send, w_recv, w_local, phase=0)
        for cp in w_copies:
            cp.start()

        cv = c_ref[...]
        sc_all_ref[me] = cv * jax.nn.sigmoid(cv)
        sc_copies = [_remote_copy(sc_all_ref.at[me], sc_all_ref.at[me], sc_send.at[k], sc_recv.at[k], peers[k])
                     for k in range(N_PEERS)]
        for cp in sc_copies:
            cp.start()
        for cp in sc_copies:
            cp.wait()

        sc = jnp.concatenate([sc_all_ref[j, 0:1, :] for j in range(N_DEV)], axis=0)
        send_buf[...] = jnp.zeros_like(send_buf)
        for l in range(N_LAYERS):
            part = jnp.dot(sc, modw_ref[l], precision=lax.Precision.HIGHEST, preferred_element_type=F32)
            for j in range(N_DEV):
                send_buf[j, l:l + 1, :] = part[j:j + 1, :]
        mod_recv[me] = send_buf[me]
        mod_copies = [_remote_copy(send_buf.at[_flat(peers[k])], mod_recv.at[me], mod_send.at[k], mod_recv_sem.at[k],
                                   peers[k]) for k in range(N_PEERS)]
        for cp in mod_copies:
            cp.start()
        for cp in mod_copies:
            cp.wait()
        modrows_ref[...] = jnp.zeros_like(modrows_ref)
        for l in range(N_LAYERS):
            row = jnp.concatenate([mod_recv[j, l:l + 1, :] for j in range(N_DEV)], axis=1)
            modrows_ref[l:l + 1, :] = row + modb_ref[l:l + 1, :]

        for cp in w_copies:
            cp.wait()
        relays = _exchange_copies(kinds, shard_refs, full_refs, w_send, w_recv, w_local, phase=1)
        for cp in relays:
            cp.start()
        for cp in relays:
            cp.wait()

    out_shape = ([jax.ShapeDtypeStruct((N_DEV, 8, D_MODEL), F32), jax.ShapeDtypeStruct((8, 6 * D_MODEL), F32)]
                 + [jax.ShapeDtypeStruct((N_DEV,) + s.shape, s.dtype) for s in shards])
    return pl.pallas_call(
        body,
        in_specs=[VMEM, VMEM, VMEM] + [ANY] * n,
        out_specs=[VMEM, VMEM] + [ANY] * n,
        out_shape=out_shape,
        scratch_shapes=[pltpu.VMEM((N_DEV, 8, MOD_SHARD), F32), pltpu.VMEM((N_DEV, 8, MOD_SHARD), F32),
                        pltpu.SemaphoreType.DMA((n, N_PEERS)), pltpu.SemaphoreType.DMA((n, N_PEERS)),
                        pltpu.SemaphoreType.DMA((n,)),
                        pltpu.SemaphoreType.DMA((N_PEERS,)), pltpu.SemaphoreType.DMA((N_PEERS,)),
                        pltpu.SemaphoreType.DMA((N_PEERS,)), pltpu.SemaphoreType.DMA((N_PEERS,))],
        compiler_params=pltpu.CompilerParams(vmem_limit_bytes=VMEM_LIMIT_BYTES),
        name="gather_weights",
    )(c8, mod_w, mod_b8, *shards)


def _small_sums(sc_all, small_all0, small_all1, job):
    kinds = [kind for kind, _ in job]
    n = len(job)

    def body(sc_all_ref, small_all0_ref, small_all1_ref, *rest):
        small_sum_ref, gmodw_ref = rest[n], rest[n + 1]
        copies = _exchange_copies(kinds, rest[:n], rest[n + 2:2 * n + 2], *rest[2 * n + 2:], phase=0)
        for cp in copies:
            cp.start()
        me = _flat(_my_pos())
        sc = jnp.concatenate([sc_all_ref[j, 0:1, :] for j in range(N_DEV)], axis=0)
        mine = lax.broadcasted_iota(jnp.int32, (2 * N_DEV, MOD_SHARD), 0) == me
        for l, parts in enumerate((small_all0_ref, small_all1_ref)):
            total = parts[0].astype(F32)
            for j in range(1, N_DEV):
                total = total + parts[j].astype(F32)
            small_sum_ref[l] = total
            dm = jnp.concatenate(
                [jnp.sum(jnp.where(mine, parts[j, ROW_DMOD:ROW_DMOD + 2 * N_DEV, 0:MOD_SHARD].astype(F32), 0.0),
                         axis=0, keepdims=True) for j in range(N_DEV)], axis=0)
            gmodw_ref[l] = lax.dot_general(sc, dm, (((0,), (0,)), ((), ())), precision=lax.Precision.HIGHEST,
                                           preferred_element_type=F32)
        for cp in copies:
            cp.wait()

    res = pl.pallas_call(
        body,
        in_specs=[VMEM, VMEM, VMEM] + [ANY] * n,
        out_specs=[VMEM, VMEM] + [ANY] * n,
        out_shape=[jax.ShapeDtypeStruct((N_LAYERS, ROWS_PER_LAYER, D_MODEL), F32),
                   jax.ShapeDtypeStruct((N_LAYERS, D_MODEL, MOD_SHARD), F32)] + _exchange_out_shapes(job),
        scratch_shapes=_exchange_sems(n),
        compiler_params=pltpu.CompilerParams(vmem_limit_bytes=VMEM_LIMIT_BYTES),
        name="small_sums",
    )(sc_all, small_all0, small_all1, *[a for _, a in job])
    return res[0], res[1], res[2:]


def _adam_update(g, w, m, v):
    m2 = ADAM_B1 * m + (1.0 - ADAM_B1) * g
    v2 = ADAM_B2 * v + (1.0 - ADAM_B2) * (g * g)
    m_hat = m2 / (1.0 - ADAM_B1 ** ADAM_STEP)
    v_hat = v2 / (1.0 - ADAM_B2 ** ADAM_STEP)
    delta = -ADAM_LR * (m_hat / (jnp.sqrt(v_hat) + ADAM_EPS) + ADAM_WD * w)
    return delta, m2, v2


def _pair_add(g, r1, row_chunk, name):
    _, rows, cols = g.shape
    core = lax.axis_index("c").astype(jnp.int32).reshape(1)

    def body(core_ref, g_ref, r_ref, o_ref):
        o_ref[0] = (g_ref[0, 0].astype(F32) + r_ref[0].astype(F32)).astype(BF16)

    blk = pl.BlockSpec((1, row_chunk, cols), lambda q, i, core_ref: (q, i, 0))
    grid_spec = pltpu.PrefetchScalarGridSpec(
        num_scalar_prefetch=1, grid=(N_CHIPS, rows // row_chunk),
        in_specs=[pl.BlockSpec((1, 1, row_chunk, cols), lambda q, i, core_ref: (q, core_ref[0], i, 0)), blk],
        out_specs=blk)
    return pl.pallas_call(
        body, grid_spec=grid_spec, out_shape=jax.ShapeDtypeStruct((N_CHIPS, rows, cols), BF16),
        compiler_params=_cparams(2), name=name,
    )(core, g.reshape(N_CHIPS, 2, rows, cols), r1)


def _adam_sharded(recv0, recv1, w, m, v, row_chunk, name):
    _, rows, cols = w.shape
    n_chunks = rows // row_chunk

    def body(r0_ref, r1_ref, w_ref, m_ref, v_ref, g_ref, d_ref, m2_ref, v2_ref):
        layer = pl.program_id(0)

        def run(r_ref):
            g = r_ref[0].astype(F32)
            for j in range(1, r_ref.shape[0]):
                g = g + r_ref[j].astype(F32)
            delta, m2, v2 = _adam_update(g, w_ref[0], m_ref[0], v_ref[0])
            g_ref[0], d_ref[0], m2_ref[0], v2_ref[0] = g, delta, m2, v2

        @pl.when(layer == 0)
        def _():
            run(r0_ref)

        @pl.when(layer == 1)
        def _():
            run(r1_ref)

    r0_spec = pl.BlockSpec((recv0.shape[0], row_chunk, cols), lambda l, i: (0, i * (1 - l) + (n_chunks - 1) * l, 0))
    r1_spec = pl.BlockSpec((recv1.shape[0], row_chunk, cols), lambda l, i: (0, i * l, 0))
    blk = pl.BlockSpec((1, row_chunk, cols), lambda l, i: (l, i, 0))
    out = jax.ShapeDtypeStruct(w.shape, F32)
    return pl.pallas_call(
        body,
        grid=(N_LAYERS, n_chunks),
        in_specs=[r0_spec, r1_spec, blk, blk, blk],
        out_specs=[blk] * 4,
        out_shape=[out] * 4,
        compiler_params=_cparams(2),
        name=name,
    )(recv0, recv1, w, m, v)


def _adam_dense(g, w, m, v, row_chunk, name):
    n_lead, rows, cols = w.shape

    def body(g_ref, w_ref, m_ref, v_ref, go_ref, d_ref, m2_ref, v2_ref):
        gv = g_ref[...]
        go_ref[...] = gv
        d_ref[...], m2_ref[...], v2_ref[...] = _adam_update(gv, w_ref[...], m_ref[...], v_ref[...])

    blk = pl.BlockSpec((1, row_chunk, cols), lambda l, i: (l, i, 0))
    out = jax.ShapeDtypeStruct(w.shape, F32)
    return pl.pallas_call(
        body,
        grid=(n_lead, rows // row_chunk),
        in_specs=[blk] * 4,
        out_specs=[blk] * 4,
        out_shape=[out] * 4,
        compiler_params=_cparams(2),
        name=name,
    )(g, w, m, v)


WEIGHT_NAMES = ("mod_w", "mod_b", "mix_pre_g", "mix_post_g", "w_in", "sgu_norm_g", "sgu_norm_b", "sgu_w", "sgu_b",
                "conv_w", "conv_b", "conv_norm_g", "conv_norm_b", "pool_w", "pool_scale", "branch_g", "w_out",
                "ffn_pre_g", "ffn_post_g", "ffn_up", "ffn_conv_w", "ffn_conv_b", "ffn_down")
SHARDED_BIG = ("w_in", "w_out", "ffn_up", "ffn_down")
SMALL_PACKED = tuple(n for n in WEIGHT_NAMES if n not in SHARDED_BIG + ("mod_w",))


def _rows8(rows, width=D_MODEL):
    out = [jnp.pad(r.astype(F32), (0, width - r.shape[0]))[None] for r in rows]
    out.append(jnp.zeros((8 - len(rows), width), F32))
    return jnp.concatenate(out, axis=0)


def _as_rows(a, width=D_MODEL):
    flat = a.astype(F32).reshape(-1)
    pad = (-flat.shape[0]) % width
    return jnp.pad(flat, (0, pad)).reshape(-1, width)


def _pad_cols(a, width=D_MODEL):
    return jnp.pad(a.astype(F32), ((0, 0), (0, width - a.shape[1])))


def _pack_rows(arrays):
    rows = jnp.concatenate([_as_rows(a) for a in arrays], axis=0)
    return jnp.pad(rows, ((0, (-rows.shape[0]) % 8), (0, 0)))


def _unpack_rows(packed, shapes):
    out, r = [], 0
    for shape in shapes:
        size = math.prod(shape)
        n_rows = -(-size // D_MODEL)
        out.append(packed[r:r + n_rows].reshape(-1)[:size].reshape(shape))
        r += n_rows
    return out


TILE_VREGS = MIX_TILE // 8
CHUNK_SUBLANES = CHUNK // TILE_VREGS
CHUNKS_PER_TILE = MIX_TILE // CHUNK


def _chunk_axis_to_tile(a, axis):
    shape = a.shape
    a = a.reshape(shape[:axis] + (CHUNK_SUBLANES, TILE_VREGS) + shape[axis + 1:])
    a = jnp.swapaxes(a, axis, axis + 1)
    a = jnp.tile(a, (1,) * (axis + 1) + (CHUNKS_PER_TILE,) + (1,) * (len(shape) - axis - 1))
    return a.reshape(shape[:axis] + (MIX_TILE,) + shape[axis + 1:])


def _tile_axis_split(a, axis):
    shape = a.shape
    return a.reshape(shape[:axis] + (TILE_VREGS, CHUNKS_PER_TILE, CHUNK_SUBLANES) + shape[axis + 1:])


def _chunk_axis_from_split(a, axis):
    shape = a.shape
    return jnp.swapaxes(a, axis, axis + 1).reshape(shape[:axis] + (CHUNK,) + shape[axis + 2:])


def _layer_consts(l, w, mod_rows, win, wout, conv_w_full):
    modv = _rows8(list(mod_rows[l].reshape(6, D_MODEL)))
    g1024 = _rows8([w["mix_pre_g"][l], w["mix_post_g"][l], w["branch_g"][l], w["ffn_pre_g"][l], w["ffn_post_g"][l]])
    p384 = _rows8([w["sgu_norm_g"][l], w["sgu_norm_b"][l], w["conv_b"][l], w["conv_norm_g"][l], w["conv_norm_b"][l]],
                  SGU_WIDTH)
    cw = jnp.pad(conv_w_full[l], ((0, 32 - CONV_K), (0, 0)))
    sguw = _chunk_axis_to_tile(_chunk_axis_to_tile(w["sgu_w"][l], 2), 1)
    bmat = _chunk_axis_to_tile(jnp.repeat(w["sgu_b"][l].T, HEAD_DIM, axis=1), 0)
    groups = len(POOL_WINDOWS)
    eye = jnp.eye(groups, dtype=F32)
    pwbd = (eye[:, None, :, None] * w["pool_w"][l][:, :, None, :]).reshape(POOL_WIDTH, POOL_WIDTH).astype(BF16)
    psc = _rows8([w["pool_scale"][l]], POOL_WIDTH)
    seg = jnp.arange(SGU_WIDTH) // HEAD_DIM
    segp = jnp.where(seg[:, None] == seg[None, :], 1.0 / HEAD_DIM, 0.0).astype(BF16)
    return modv, g1024, (modv, g1024, p384, cw, sguw, bmat, pwbd, psc, segp, win, wout)


def _small_grad_rows(mix, ffn):
    _, _, _, _, _, mvec, v384, dcw, dsguw, dbmat, dpw, dpsc = mix
    fvec, cgrad = ffn[5], ffn[6]
    dmod = jnp.stack([mvec[2], mvec[3], mvec[1], fvec[2], fvec[3], fvec[1]]).reshape(N_DEV, MOD_SHARD)
    g_rows = jnp.stack([mvec[4], mvec[0], mvec[5], fvec[4], fvec[0]])
    dbmat = _chunk_axis_from_split(_tile_axis_split(dbmat, 0).sum(axis=1), 0)
    dsguw = _tile_axis_split(_tile_axis_split(dsguw, 2), 1)
    dsguw = sum(dsguw[:, :, c, :, :, c, :] for c in range(CHUNKS_PER_TILE))
    dsguw = _chunk_axis_from_split(_chunk_axis_from_split(dsguw, 3), 1)
    dsgu_b = dbmat[:, ::HEAD_DIM].T.reshape(1, SGU_HEADS * CHUNK)
    groups = len(POOL_WINDOWS)
    gdim = POOL_WIDTH // groups
    dpw4 = dpw.reshape(groups, gdim, groups, gdim)
    dpool = jnp.stack([dpw4[g, :, g, :] for g in range(groups)])
    blocks = [_pad_cols(dmod), _rows8(list(g_rows)), _pad_cols(v384), _rows8([dsgu_b[0], dpsc[0]]), _pad_cols(dcw),
              _pad_cols(cgrad[:, 0:4, :].reshape(4 * N_DEV, FF_SHARD)), _as_rows(dpool), _as_rows(dsguw)]
    return jnp.concatenate(blocks, axis=0)


def _small_grads_from_rows(total):
    per = {n: [] for n in SMALL_PACKED}
    for l in range(N_LAYERS):
        s = total[l]
        per["mod_b"].append(s[ROW_DMOD:ROW_DMOD + N_DEV, :MOD_SHARD].reshape(6 * D_MODEL))
        for j, name in enumerate(("mix_pre_g", "mix_post_g", "branch_g", "ffn_pre_g", "ffn_post_g")):
            per[name].append(s[ROW_G1024 + j])
        for j, name in enumerate(("sgu_norm_g", "sgu_norm_b", "conv_b", "conv_norm_g", "conv_norm_b")):
            per[name].append(s[ROW_V384 + j, :SGU_WIDTH])
        per["sgu_b"].append(s[ROW_SGU_B, :SGU_HEADS * CHUNK].reshape(SGU_HEADS, CHUNK))
        per["pool_scale"].append(s[ROW_POOL_SCALE, :POOL_WIDTH])
        per["conv_w"].append(s[ROW_CONV_W:ROW_CONV_W + CONV_K, :CONV_WIDTH])
        fc = s[ROW_FFN_CONV:ROW_FFN_CONV + 4 * N_DEV, :FF_SHARD].reshape(N_DEV, 4, FF_SHARD)
        per["ffn_conv_w"].append(fc[:, 0:3, :].transpose(1, 0, 2).reshape(FFN_CONV_K, 2 * D_FF))
        per["ffn_conv_b"].append(fc[:, 3, :].reshape(2 * D_FF))
        per["pool_w"].append(s[ROW_POOL_W:ROW_POOL_W + 16].reshape(len(POOL_WINDOWS), HEAD_DIM, HEAD_DIM))
        per["sgu_w"].append(s[ROW_SGU_W:ROW_SGU_W + 96].reshape(SGU_HEADS, CHUNK, CHUNK))
    return {n: jnp.stack(v) for n, v in per.items()}


def kernel(x, c, mod_w, mod_b, mix_pre_g, mix_post_g, w_in, sgu_norm_g, sgu_norm_b, sgu_w, sgu_b, conv_w, conv_b, conv_norm_g, conv_norm_b, pool_w, pool_scale, branch_g, w_out, ffn_pre_g, ffn_post_g, ffn_up, ffn_conv_w, ffn_conv_b, ffn_down, loss_target, m_mod_w, m_mod_b, m_mix_pre_g, m_mix_post_g, m_w_in, m_sgu_norm_g, m_sgu_norm_b, m_sgu_w, m_sgu_b, m_conv_w, m_conv_b, m_conv_norm_g, m_conv_norm_b, m_pool_w, m_pool_scale, m_branch_g, m_w_out, m_ffn_pre_g, m_ffn_post_g, m_ffn_up, m_ffn_conv_w, m_ffn_conv_b, m_ffn_down, v_mod_w, v_mod_b, v_mix_pre_g, v_mix_post_g, v_w_in, v_sgu_norm_g, v_sgu_norm_b, v_sgu_w, v_sgu_b, v_conv_w, v_conv_b, v_conv_norm_g, v_conv_norm_b, v_pool_w, v_pool_scale, v_branch_g, v_w_out, v_ffn_pre_g, v_ffn_post_g, v_ffn_up, v_ffn_conv_w, v_ffn_conv_b, v_ffn_down):
    w = dict(zip(WEIGHT_NAMES, (mod_w, mod_b, mix_pre_g, mix_post_g, w_in, sgu_norm_g, sgu_norm_b, sgu_w, sgu_b, conv_w,
                                conv_b, conv_norm_g, conv_norm_b, pool_w, pool_scale, branch_g, w_out, ffn_pre_g,
                                ffn_post_g, ffn_up, ffn_conv_w, ffn_conv_b, ffn_down)))
    m = dict(zip(WEIGHT_NAMES, (m_mod_w, m_mod_b, m_mix_pre_g, m_mix_post_g, m_w_in, m_sgu_norm_g, m_sgu_norm_b, m_sgu_w,
                                m_sgu_b, m_conv_w, m_conv_b, m_conv_norm_g, m_conv_norm_b, m_pool_w, m_pool_scale,
                                m_branch_g, m_w_out, m_ffn_pre_g, m_ffn_post_g, m_ffn_up, m_ffn_conv_w, m_ffn_conv_b,
                                m_ffn_down)))
    v = dict(zip(WEIGHT_NAMES, (v_mod_w, v_mod_b, v_mix_pre_g, v_mix_post_g, v_w_in, v_sgu_norm_g, v_sgu_norm_b, v_sgu_w,
                                v_sgu_b, v_conv_w, v_conv_b, v_conv_norm_g, v_conv_norm_b, v_pool_w, v_pool_scale,
                                v_branch_g, v_w_out, v_ffn_pre_g, v_ffn_post_g, v_ffn_up, v_ffn_conv_w, v_ffn_conv_b,
                                v_ffn_down)))
    me = _flat(_my_pos())
    xs = x[0]
    s_len = xs.shape[0]

    transposed = ("w_in", "ffn_up")
    wt = {n: jnp.swapaxes(w[n], 1, 2) if n in transposed else w[n] for n in SHARDED_BIG}
    mt = {n: jnp.swapaxes(m[n], 1, 2) if n in transposed else m[n] for n in SHARDED_BIG}
    vt = {n: jnp.swapaxes(v[n], 1, 2) if n in transposed else v[n] for n in SHARDED_BIG}
    bf16_shards = [[wt[n][l].astype(BF16) for n in SHARDED_BIG] for l in range(N_LAYERS)]

    def mixer_operands(l, win_g, wout_g):
        win = win_g.reshape(IN_WIDTH, D_MODEL)
        return _layer_consts(l, w, mod_rows, win, wout_g.reshape(D_MODEL, D_MODEL), conv_w_full)

    def ffn_operands(l, modv, g1024, wup_g, wdn_g):
        wdn = wdn_g.reshape(FF_PAIRS, FF_SHARD, D_MODEL)
        return modv, g1024, wup_g, wdn, ffn_cw_full[:, l], ffn_conv_b[l].reshape(N_DEV, 1, FF_SHARD)

    w0, w1 = bf16_shards
    c8 = jnp.broadcast_to(c, (8, D_MODEL))
    mod_b8 = jnp.pad(mod_b, ((0, 8 - N_LAYERS), (0, 0)))
    sc_all, mod_rows, win0_g, wout0_g, conv_w_g, ffn_cw_full = _gather_weights(
        c8, mod_w, mod_b8, [("gather2", w0[0]), ("gather2", w0[1]), ("gather", conv_w), ("gather", ffn_conv_w)])
    conv_w_full = conv_w_g.transpose(1, 2, 0, 3).reshape(N_LAYERS, CONV_K, CONV_WIDTH)

    modv0, g0, mix_consts0 = mixer_operands(0, win0_g, wout0_g)
    (x1, z, o, cv), (wup0_g, wdn0_g) = _mixer_fwd(xs, *mix_consts0, name="mixer_fwd_l0", natural_x=True,
                                                  job=[("gather2", w0[2]), ("gather2", w0[3])])
    ffn_consts0 = ffn_operands(0, modv0, g0, wup0_g, wdn0_g)
    (x2, y2, p, u), (win1_g, wout1_g, wdn1_g) = _ffn_fwd(
        x1, *ffn_consts0, name="ffn_fwd_l0", job=[("gather2", w1[0]), ("gather2", w1[1]), ("gather2", w1[3])])
    saved = [(xs, z, o, cv, x1, y2, p, u)]
    modv1, g1, mix_consts1 = mixer_operands(1, win1_g, wout1_g)
    (x1, z, o, cv), (wup1_g,) = _mixer_fwd(x2, *mix_consts1, name="mixer_fwd_l1", job=[("gather2", w1[2])])
    ffn_consts1 = ffn_operands(1, modv1, g1, wup1_g, wdn1_g)
    (dh, y2, p, u, loss_tile), _ = _ffn_fwd(x1, *ffn_consts1, name="ffn_fwd_l1", loss_target=loss_target[0])
    saved.append((x2, z, o, cv, x1, y2, p, u))
    loss = lax.psum(loss_tile[0, 0], ("x", "y", "c"))

    def ffn_weight_grads(l, ffn):
        dp, a, dy2, h2 = ffn[1:5]
        d_up = _wgrad(dp, h2, f"wgrad_ffn_up_l{l}", tk=WGRAD_TK_FFN)
        d_dn = _wgrad(a, dy2, f"wgrad_ffn_down_l{l}", tk=WGRAD_TK_FFN).reshape(N_DEV, D_FF // N_DEV, D_MODEL)
        return d_up, d_dn

    def mixer_weight_grads(l, mix):
        dz, do, ycat, h1 = mix[1:5]
        d_in = _wgrad(dz[None], h1, f"wgrad_w_in_l{l}").reshape(N_DEV, IN_WIDTH // N_DEV, D_MODEL)
        d_out = _wgrad(ycat, do[None], f"wgrad_w_out_l{l}").reshape(N_DEV, D_MODEL // N_DEV, D_MODEL)
        return d_in, d_out

    x_in, z, o, cv, x1, y2, p, u = saved[1]
    ffn1, _ = _ffn_bwd(dh, x1, y2, p, u, *ffn_consts1, name="ffn_bwd_l1")
    d_up1, d_dn1 = ffn_weight_grads(1, ffn1)
    mix1, (sib_up1, sib_dn1) = _mixer_bwd(ffn1[0], x_in, o, z, cv, *mix_consts1, name="mixer_bwd_l1",
                                          job=[("scatter_p1", d_up1), ("scatter_p1", d_dn1)])
    chip_up1 = _pair_add(d_up1, sib_up1, 176, "pair_add_ffn_up_l1")
    chip_dn1 = _pair_add(d_dn1, sib_dn1, 176, "pair_add_ffn_down_l1")
    d_in1, d_out1 = mixer_weight_grads(1, mix1)
    small1 = _small_grad_rows(mix1, ffn1).astype(BF16)

    x_in, z, o, cv, x1, y2, p, u = saved[0]
    ffn0, job_out = _ffn_bwd(mix1[0], x1, y2, p, u, *ffn_consts0, name="ffn_bwd_l0",
                             job=[("scatter", d_in1), ("scatter", d_out1), ("scatter_p2", chip_up1),
                                  ("scatter_p2", chip_dn1), ("gather", small1)])
    recv1, small_all1 = job_out[0:4], job_out[4]
    dp, a, dy2, h2 = ffn0[1:5]
    d_dn0 = _wgrad(a, dy2, "wgrad_ffn_down_l0", tk=WGRAD_TK_FFN).reshape(N_DEV, D_FF // N_DEV, D_MODEL)
    d_up0, (recv_dn0,) = _wgrad(dp, h2, "wgrad_ffn_up_l0", tk=WGRAD_TK_FFN, job=[("scatter", d_dn0)])
    mix0, (recv_up0,) = _mixer_bwd(ffn0[0], x_in, o, z, cv, *mix_consts0, name="mixer_bwd_l0", natural_x=True,
                                   job=[("scatter", d_up0)])
    recv_ffn0 = (recv_up0, recv_dn0)
    grad_x = mix0[0][None]
    dz, do, ycat, h1 = mix0[1:5]
    small0 = _small_grad_rows(mix0, ffn0).astype(BF16)
    d_out0, (small_all0,) = _wgrad(ycat, do[None], "wgrad_w_out_l0", job=[("gather", small0)])
    d_out0 = d_out0.reshape(N_DEV, D_MODEL // N_DEV, D_MODEL)
    d_in0, (recv_out0,) = _wgrad(dz[None], h1, "wgrad_w_in_l0", job=[("scatter", d_out0)])
    d_in0 = d_in0.reshape(N_DEV, IN_WIDTH // N_DEV, D_MODEL)
    small_total, g_mod_w, (recv_in0,) = _small_sums(sc_all, small_all0, small_all1, [("scatter", d_in0)])
    recv0 = [recv_in0, recv_out0, recv_ffn0[0], recv_ffn0[1]]

    grads, deltas, new_m, new_v = {}, {}, {}, {}
    for j, (name, chunk) in enumerate((("w_in", 224), ("w_out", 128), ("ffn_up", 176), ("ffn_down", 176))):
        outs = _adam_sharded(recv0[j], recv1[j], wt[name], mt[name], vt[name], chunk, "adam_" + name)
        if name in transposed:
            outs = [jnp.swapaxes(t, 1, 2) for t in outs]
        grads[name], deltas[name], new_m[name], new_v[name] = outs
    grads["mod_w"], deltas["mod_w"], new_m["mod_w"], new_v["mod_w"] = _adam_dense(
        g_mod_w, mod_w, m_mod_w, v_mod_w, 256, "adam_mod_w")

    small_g = _small_grads_from_rows(small_total)
    small_g["conv_w"] = lax.dynamic_slice_in_dim(small_g["conv_w"], me * conv_w.shape[2], conv_w.shape[2], axis=2)
    small_g["ffn_conv_w"] = lax.dynamic_slice_in_dim(small_g["ffn_conv_w"], me * FF_SHARD, FF_SHARD, axis=2)
    shapes = [w[n].shape for n in SMALL_PACKED]
    packs = [_pack_rows([src[n] for n in SMALL_PACKED])[None] for src in (small_g, w, m, v)]
    _, d, m2, v2 = _adam_dense(*packs, packs[0].shape[1], "adam_small")
    for name, dd, mm, vv in zip(SMALL_PACKED, _unpack_rows(d[0], shapes), _unpack_rows(m2[0], shapes),
                                _unpack_rows(v2[0], shapes)):
        grads[name], deltas[name], new_m[name], new_v[name] = small_g[name], dd, mm, vv

    return (loss, grad_x, *[grads[n] for n in WEIGHT_NAMES], *[deltas[n] for n in WEIGHT_NAMES],
            *[new_m[n] for n in WEIGHT_NAMES], *[new_v[n] for n in WEIGHT_NAMES])
```

```python
import functools
import math

import jax
import jax.numpy as jnp
from jax import lax
from jax.experimental import pallas as pl
from jax.experimental.pallas import tpu as pltpu

F32 = jnp.float32
BF16 = jnp.bfloat16

D_MODEL = 1024
N_DEV = 8
SGU_WIDTH = 384
CONV_WIDTH = 384
POOL_WIDTH = 256
HEAD_DIM = 64
SGU_HEADS = 6
CHUNK = 128
CONV_K = 31
POOL_WINDOWS = (2, 4, 8, 16)
IN_WIDTH = 1792
D_FF = 2816
FF_SHARD = 2 * D_FF // N_DEV
FF_PAIRS = N_DEV // 2
FFN_CONV_K = 3
EPS = 1e-6
GELU_C0 = math.sqrt(2.0 / math.pi)
GELU_C1 = 0.044715

ADAM_LR = 0.001
ADAM_B1 = 0.9
ADAM_B2 = 0.999
ADAM_EPS = 1e-08
ADAM_WD = 0.01
ADAM_STEP = 10

VMEM_LIMIT_BYTES = 56 * 1024 * 1024
TILE = 256
MIX_TILE = TILE
FFN_TILE = TILE
FFN_HALO = 8 * (FFN_CONV_K - 1)
POOL_HALO = 8 * POOL_WINDOWS[-1]
WGRAD_TK = 2048
WGRAD_TK_FFN = 4096


def _cparams(n_axes):
    return pltpu.CompilerParams(dimension_semantics=("arbitrary",) * n_axes, vmem_limit_bytes=VMEM_LIMIT_BYTES)


def _whole(shape):
    nd = len(shape)
    return pl.BlockSpec(shape, lambda *_: (0,) * nd, pipeline_mode=pl.Buffered(1))


def _dot(a, b):
    return jnp.dot(a, b, preferred_element_type=F32)


def _dot_nt(a, b):
    return lax.dot_general(a, b, (((1,), (1,)), ((), ())), preferred_element_type=F32)


def _dot_tn(a, b):
    return lax.dot_general(a, b, (((0,), (0,)), ((), ())), preferred_element_type=F32)


def _gelu(x):
    t = jnp.tanh(GELU_C0 * (x + GELU_C1 * x * x * x))
    return 0.5 * x * (1.0 + t), t


def _gelu_grad(x, t):
    return 0.5 * (1.0 + t) + 0.5 * x * (1.0 - t * t) * (GELU_C0 * (1.0 + 3.0 * GELU_C1 * x * x))


def _rowmean(x):
    return jnp.mean(x, axis=-1, keepdims=True)


def _colsum(x):
    return jnp.sum(x, axis=0, keepdims=True)


def _rms_fwd(x):
    r = lax.rsqrt(_rowmean(x * x) + EPS)
    return x * r, r


def _rms_bwd(dxhat, xhat, r):
    return r * (dxhat - xhat * _rowmean(dxhat * xhat))


N_PEERS = N_DEV - 1
ANY = pl.BlockSpec(memory_space=pl.ANY)
VMEM = pl.BlockSpec(memory_space=pltpu.VMEM)


def _my_pos():
    return lax.axis_index("x"), lax.axis_index("y"), lax.axis_index("c")


def _peer(pos, k):
    x, y, c = pos
    return (1 - x if k & 4 else x, 1 - y if k & 2 else y, 1 - c if k & 1 else c)


def _flat(pos):
    return 4 * pos[0] + 2 * pos[1] + pos[2]


def _remote_copy(src, dst, send_sem, recv_sem, peer):
    return pltpu.make_async_remote_copy(src_ref=src, dst_ref=dst, send_sem=send_sem, recv_sem=recv_sem,
                                        device_id=peer, device_id_type=pl.DeviceIdType.MESH)


N_CHIPS = N_DEV // 2
SIBLING = 1
SAME_CORE_PEERS = (2, 4, 6)


def _exchange_out_shapes(job):
    def shape(kind, a):
        if kind in ("gather", "gather2"):
            return (N_DEV,) + a.shape
        if kind == "scatter_p1":
            return (N_CHIPS,) + a.shape[1:]
        return a.shape
    return [jax.ShapeDtypeStruct(shape(kind, a), a.dtype) for kind, a in job]


def _exchange_sems(n):
    return [pltpu.SemaphoreType.DMA((n, N_PEERS)), pltpu.SemaphoreType.DMA((n, N_PEERS)), pltpu.SemaphoreType.DMA((n,))]


def _exchange_copies(kinds, src_refs, dst_refs, send_sems, recv_sems, local_sems, phase):
    pos = _my_pos()
    me = _flat(pos)
    chip, core = 2 * pos[0] + pos[1], pos[2]
    copies = []

    def remote(a, src, dst, k, sem=None):
        sem = k - 1 if sem is None else sem
        copies.append(_remote_copy(src, dst, send_sems.at[a, sem], recv_sems.at[a, sem], _peer(pos, k)))

    for a, kind in enumerate(kinds):
        src, dst = src_refs[a], dst_refs[a]
        if phase == 1:
            if kind == "gather2":
                for k in SAME_CORE_PEERS:
                    remote(a, dst.at[me ^ k], dst.at[me ^ k], SIBLING, sem=k)
        elif kind in ("gather", "gather2"):
            copies.append(pltpu.make_async_copy(src, dst.at[me], local_sems.at[a]))
            for k in (range(1, N_DEV) if kind == "gather" else (SIBLING,) + SAME_CORE_PEERS):
                remote(a, src, dst.at[me], k)
        elif kind == "scatter":
            copies.append(pltpu.make_async_copy(src.at[me], dst.at[me], local_sems.at[a]))
            for k in range(1, N_DEV):
                remote(a, src.at[me ^ k], dst.at[me], k)
        elif kind == "scatter_p1":
            for q in range(N_CHIPS):
                remote(a, src.at[2 * q + 1 - core], dst.at[q], SIBLING, sem=q)
        elif kind == "scatter_p2":
            copies.append(pltpu.make_async_copy(src.at[chip], dst.at[chip], local_sems.at[a]))
            for k in SAME_CORE_PEERS:
                remote(a, src.at[chip ^ (k >> 1)], dst.at[chip], k)
    return copies


def _pallas_call_with_exchange(body, *, grid, in_specs, out_specs, out_shape, scratch_shapes, operands, name, job):
    params = _cparams(len(grid))
    if not job:
        outs = pl.pallas_call(body, grid=grid, in_specs=in_specs, out_specs=out_specs, out_shape=out_shape,
                              scratch_shapes=scratch_shapes, compiler_params=params, name=name)(*operands)
        return outs, []
    kinds = [kind for kind, _ in job]
    relayed = [kind if kind == "gather2" else None for kind in kinds]
    unrelayed = [None if kind == "gather2" else kind for kind in kinds]
    n, n_in, n_out, n_scr = len(job), len(in_specs), len(out_specs), len(scratch_shapes)
    n_steps = math.prod(grid)
    relay_step = max(n_steps - 2, 0)

    def wrapped(*refs):
        ins, jin = refs[:n_in], refs[n_in:n_in + n]
        outs, jout = refs[n_in + n:n_in + n + n_out], refs[n_in + n + n_out:n_in + 2 * n + n_out]
        scr = refs[n_in + 2 * n + n_out:n_in + 2 * n + n_out + n_scr]
        sems = refs[n_in + 2 * n + n_out + n_scr:]
        step = pl.program_id(0)
        for d in range(1, len(grid)):
            step = step * grid[d] + pl.program_id(d)

        def copies(which, phase):
            return _exchange_copies(which, jin, jout, *sems, phase=phase)

        @pl.when(step == 0)
        def _():
            for cp in copies(kinds, 0):
                cp.start()

        body(*ins, *outs, *scr)

        @pl.when(step == relay_step)
        def _():
            for cp in copies(relayed, 0):
                cp.wait()
            for cp in copies(relayed, 1):
                cp.start()

        @pl.when(step == n_steps - 1)
        def _():
            for cp in copies(unrelayed, 0) + copies(relayed, 1):
                cp.wait()

    res = pl.pallas_call(
        wrapped, grid=grid,
        in_specs=list(in_specs) + [ANY] * n,
        out_specs=list(out_specs) + [ANY] * n,
        out_shape=list(out_shape) + _exchange_out_shapes(job),
        scratch_shapes=list(scratch_shapes) + _exchange_sems(n),
        compiler_params=params, name=name,
    )(*operands, *[a for _, a in job])
    return res[:n_out], res[n_out:]


def _seg_mean(x, segp):
    hi = x.astype(BF16)
    lo = (x - hi.astype(F32)).astype(BF16)
    return _dot(hi, segp) + _dot(lo, segp)


def _rot_rows(x, shift):
    m, c = x.shape
    return pltpu.roll(x.reshape(m // 8, 8, c), shift, 1).reshape(m, c)


def _sublane_is(shape, s):
    return lax.broadcasted_iota(jnp.int32, shape, 0) % 8 == s


def _causal_tail(tail, prev_rot):
    rot = _rot_rows(tail, 1)
    return jnp.where(_sublane_is(tail.shape, 0), prev_rot, rot), rot


def _lookahead_head(head, next_rot):
    rot = _rot_rows(head, 7)
    return jnp.where(_sublane_is(head.shape, 7), next_rot, rot), rot


def _tile_token_index(t, tile_idx):
    r = lax.broadcasted_iota(jnp.int32, (t, 1), 0)
    return tile_idx * t + (r % 8) * (t // 8) + r // 8


def _interleave(x):
    t, c = x.shape
    return jnp.swapaxes(x.reshape(8, t // 8, c), 0, 1).reshape(t, c)


def _deinterleave(x):
    t, c = x.shape
    return jnp.swapaxes(x.reshape(t // 8, 8, c), 0, 1).reshape(t, c)


def _ffn_fwd(x1, modv, g1024, wup, wdn, cw, cb, name="ffn_fwd", job=None, loss_target=None):
    s_len = x1.shape[0]
    t = FFN_TILE
    n_tiles = s_len // t
    with_loss = loss_target is not None

    def body(x1_ref, *rest):
        if with_loss:
            tgt_ref, rest = rest[0], rest[1:]
            loss_ref, rest = rest[10], rest[:10] + rest[11:]
        mod_ref, g_ref, wup_ref, wdn_ref, cw_ref, cb_ref, x2_ref, y2_ref, p_ref, u_ref, ext_ref, carry_ref = rest
        i = pl.program_id(0)

        @pl.when(i == 0)
        def _():
            carry_ref[...] = jnp.zeros_like(carry_ref)
            if with_loss:
                loss_ref[...] = jnp.zeros_like(loss_ref)

        x1v = x1_ref[...]
        pre_g, post_g = g_ref[3:4, :], g_ref[4:5, :]
        sh2, sc2, g2 = mod_ref[3:4, :], mod_ref[4:5, :], mod_ref[5:6, :]
        xhat, _ = _rms_fwd(x1v)
        h2b = (xhat * pre_g * (1.0 + sc2) + sh2).astype(BF16)

        def conv_shard(s):
            p = _dot_nt(h2b, wup_ref[s])
            p_ref[s] = p.astype(BF16)
            ext_ref[0:FFN_HALO, :], carry_ref[s] = _causal_tail(p[t - FFN_HALO:t, :], carry_ref[s])
            ext_ref[FFN_HALO:FFN_HALO + t, :] = p
            w = cw_ref[s]
            u = w[0:1, :] * ext_ref[0:t, :] + w[1:2, :] * ext_ref[8:8 + t, :] + w[2:3, :] * p + cb_ref[s]
            u_ref[s] = u.astype(BF16)
            return u

        y2 = jnp.zeros((t, D_MODEL), F32)
        for j in range(FF_PAIRS):
            ug = conv_shard(j)
            uv = conv_shard(j + FF_PAIRS)
            ge, _ = _gelu(ug)
            y2 = y2 + _dot((ge * uv).astype(BF16), wdn_ref[j])
        y2_ref[...] = y2
        yhat, _ = _rms_fwd(y2)
        x2 = x1v + g2 * (yhat * post_g)
        if with_loss:
            diff = x2 - _interleave(tgt_ref[...])
            x2_ref[...] = diff * (1.0 / D_MODEL)
            loss_ref[...] += (0.5 / D_MODEL) * jnp.sum(diff * diff)
        else:
            x2_ref[...] = x2

    tile = pl.BlockSpec((t, D_MODEL), lambda i: (i, 0))
    consts = (modv, g1024, wup, wdn, cw, cb)
    shards = pl.BlockSpec((N_DEV, t, FF_SHARD), lambda i: (0, i, 0))
    out_specs = [tile, tile, shards, shards]
    out_shape = [jax.ShapeDtypeStruct((s_len, D_MODEL), F32), jax.ShapeDtypeStruct((s_len, D_MODEL), F32),
                 jax.ShapeDtypeStruct((N_DEV, s_len, FF_SHARD), BF16),
                 jax.ShapeDtypeStruct((N_DEV, s_len, FF_SHARD), BF16)]
    if with_loss:
        out_specs.append(pl.BlockSpec((8, 128), lambda i: (0, 0)))
        out_shape.append(jax.ShapeDtypeStruct((8, 128), F32))
    return _pallas_call_with_exchange(
        body,
        grid=(n_tiles,),
        in_specs=[tile] * (2 if with_loss else 1) + [_whole(c.shape) for c in consts],
        out_specs=out_specs,
        out_shape=out_shape,
        scratch_shapes=[pltpu.VMEM((FFN_HALO + t, FF_SHARD), F32), pltpu.VMEM((N_DEV, FFN_HALO, FF_SHARD), F32)],
        operands=(x1,) + ((loss_target,) if with_loss else ()) + consts,
        name=name, job=job)


def _ffn_bwd(dx2, x1, y2, p, u, modv, g1024, wup, wdn, cw, cb, name="ffn_bwd", job=None):
    s_len = x1.shape[0]
    t = FFN_TILE
    n_tiles = s_len // t
    hb = FFN_HALO

    def body(dx2_ref, x1_ref, y2_ref, p_ref, ph_ref, u_ref, mod_ref, g_ref, wup_ref, wdn_ref, cw_ref, cb_ref,
             dx1_ref, dp_ref, a_ref, dy2_ref, h2_ref, vec_ref, cgrad_ref, ext_ref, dext_ref, dcarry_ref):
        i = pl.program_id(0)
        tile_idx = n_tiles - 1 - i

        @pl.when(i == 0)
        def _():
            vec_ref[...] = jnp.zeros_like(vec_ref)
            cgrad_ref[...] = jnp.zeros_like(cgrad_ref)
            dcarry_ref[...] = jnp.zeros_like(dcarry_ref)

        dx2v, x1v, y2v = dx2_ref[...], x1_ref[...], y2_ref[...]
        pre_g, post_g = g_ref[3:4, :], g_ref[4:5, :]
        sh2, sc2, g2 = mod_ref[3:4, :], mod_ref[4:5, :], mod_ref[5:6, :]

        yhat, ry = _rms_fwd(y2v)
        vec_ref[1:2, :] += _colsum(dx2v * (yhat * post_g))
        dyn = dx2v * g2
        vec_ref[0:1, :] += _colsum(dyn * yhat)
        dy2b = _rms_bwd(dyn * post_g, yhat, ry).astype(BF16)
        dy2_ref[...] = dy2b

        xhat, rx = _rms_fwd(x1v)
        xn = xhat * pre_g
        h2_ref[...] = (xn * (1.0 + sc2) + sh2).astype(BF16)

        not_first = (tile_idx > 0).astype(F32)

        def recompute(s, slot):
            pf = p_ref[s].astype(F32)
            prev_rot = _rot_rows(ph_ref[s].astype(F32), 1) * not_first
            ext_ref[slot, 0:hb, :], _ = _causal_tail(pf[t - hb:t, :], prev_rot)
            ext_ref[slot, hb:hb + t, :] = pf
            return u_ref[s].astype(F32)

        def conv_bwd(s, slot, du):
            w = cw_ref[s]
            cgrad_ref[s, 0:1, :] += _colsum(du * ext_ref[slot, 0:t, :])
            cgrad_ref[s, 1:2, :] += _colsum(du * ext_ref[slot, 8:8 + t, :])
            cgrad_ref[s, 2:3, :] += _colsum(du * ext_ref[slot, 16:16 + t, :])
            cgrad_ref[s, 3:4, :] += _colsum(du)
            dext_ref[0:t, :] = du
            dext_ref[t:t + hb, :], dcarry_ref[s] = _lookahead_head(du[0:hb, :], dcarry_ref[s])
            dp = w[2:3, :] * du + w[1:2, :] * dext_ref[8:8 + t, :] + w[0:1, :] * dext_ref[16:16 + t, :]
            dpb = dp.astype(BF16)
            dp_ref[s] = dpb
            return _dot(dpb, wup_ref[s])

        dh2 = jnp.zeros((t, D_MODEL), F32)
        for j in range(FF_PAIRS):
            ug = recompute(j, 0)
            uv = recompute(j + FF_PAIRS, 1)
            ge, th = _gelu(ug)
            a_ref[j] = (ge * uv).astype(BF16)
            da = _dot_nt(dy2b, wdn_ref[j])
            dh2 = dh2 + conv_bwd(j, 0, da * uv * _gelu_grad(ug, th))
            dh2 = dh2 + conv_bwd(j + FF_PAIRS, 1, da * ge)

        vec_ref[2:3, :] += _colsum(dh2)
        vec_ref[3:4, :] += _colsum(dh2 * xn)
        dxn = dh2 * (1.0 + sc2)
        vec_ref[4:5, :] += _colsum(dxn * xhat)
        dx1_ref[...] = dx2v + _rms_bwd(dxn * pre_g, xhat, rx)

    rev = lambda i: (n_tiles - 1 - i, 0)
    tile = pl.BlockSpec((t, D_MODEL), rev)
    halo_idx = lambda i: (0, jnp.maximum((n_tiles - 1 - i) * (t // hb) - 1, 0), 0)
    return _pallas_call_with_exchange(
        body,
        grid=(n_tiles,),
        in_specs=[tile, tile, tile,
                  pl.BlockSpec((N_DEV, t, FF_SHARD), lambda i: (0, n_tiles - 1 - i, 0)),
                  pl.BlockSpec((N_DEV, hb, FF_SHARD), halo_idx),
                  pl.BlockSpec((N_DEV, t, FF_SHARD), lambda i: (0, n_tiles - 1 - i, 0)),
                  _whole(modv.shape), _whole(g1024.shape), _whole(wup.shape), _whole(wdn.shape),
                  _whole(cw.shape), _whole(cb.shape)],
        out_specs=[tile,
                   pl.BlockSpec((N_DEV, t, FF_SHARD), lambda i: (0, n_tiles - 1 - i, 0)),
                   pl.BlockSpec((FF_PAIRS, t, FF_SHARD), lambda i: (0, n_tiles - 1 - i, 0)),
                   tile, tile,
                   pl.BlockSpec((8, D_MODEL), lambda i: (0, 0)),
                   pl.BlockSpec((N_DEV, 8, FF_SHARD), lambda i: (0, 0, 0))],
        out_shape=[jax.ShapeDtypeStruct((s_len, D_MODEL), F32),
                   jax.ShapeDtypeStruct((N_DEV, s_len, FF_SHARD), BF16),
                   jax.ShapeDtypeStruct((FF_PAIRS, s_len, FF_SHARD), BF16),
                   jax.ShapeDtypeStruct((s_len, D_MODEL), BF16),
                   jax.ShapeDtypeStruct((s_len, D_MODEL), BF16),
                   jax.ShapeDtypeStruct((8, D_MODEL), F32),
                   jax.ShapeDtypeStruct((N_DEV, 8, FF_SHARD), F32)],
        scratch_shapes=[pltpu.VMEM((2, hb + t, FF_SHARD), F32), pltpu.VMEM((t + hb, FF_SHARD), F32),
                        pltpu.VMEM((N_DEV, hb, FF_SHARD), F32)],
        operands=(dx2, x1, y2, p, p, u, modv, g1024, wup, wdn, cw, cb),
        name=name, job=job)


def _wgrad(a, b, name, tk=WGRAD_TK, job=None):
    a_grouped, b_grouped = a.ndim == 3, b.ndim == 3
    groups = a.shape[0] if a_grouped else b.shape[0]
    s_len, m, n = a.shape[-2], a.shape[-1], b.shape[-1]
    tk = min(tk, s_len)
    n_k = s_len // tk

    def body(a_ref, b_ref, o_ref, acc_ref):
        k = pl.program_id(1)
        av = a_ref[0] if a_grouped else a_ref[...]
        bv = b_ref[0] if b_grouped else b_ref[...]
        part = _dot_tn(av, bv)
        if n_k == 1:
            o_ref[0] = part.astype(BF16)
            return

        @pl.when(k == 0)
        def _():
            acc_ref[...] = part

        @pl.when(jnp.logical_and(k > 0, k < n_k - 1))
        def _():
            acc_ref[...] += part

        @pl.when(k == n_k - 1)
        def _():
            o_ref[0] = (acc_ref[...] + part).astype(BF16)

    a_spec = pl.BlockSpec((1, tk, m), lambda g, k: (g, k, 0)) if a_grouped else pl.BlockSpec((tk, m), lambda g, k: (k, 0))
    b_spec = pl.BlockSpec((1, tk, n), lambda g, k: (g, k, 0)) if b_grouped else pl.BlockSpec((tk, n), lambda g, k: (k, 0))
    (out,), exchanged = _pallas_call_with_exchange(
        body,
        grid=(groups, n_k),
        in_specs=[a_spec, b_spec],
        out_specs=[pl.BlockSpec((1, m, n), lambda g, k: (g, 0, 0))],
        out_shape=[jax.ShapeDtypeStruct((groups, m, n), BF16)],
        scratch_shapes=[pltpu.VMEM((m, n), F32)],
        operands=(a, b),
        name=name, job=job)
    return (out, exchanged) if job else out


def _lane(shape):
    return lax.broadcasted_iota(jnp.int32, shape, 1)


def _by_pool_group(shape, vals):
    lane = _lane(shape)
    return jnp.where(lane < 64, vals[0], jnp.where(lane < 128, vals[1], jnp.where(lane < 192, vals[2], vals[3])))


def _pool_inv_counts(t, tile_idx):
    pos1 = _tile_token_index(t, tile_idx) + 1
    return [1.0 / jnp.minimum(pos1, w).astype(F32) for w in POOL_WINDOWS]


def _sgu_keep_mask(t):
    tok_r = _tile_token_index(t, 0)
    c = lax.broadcasted_iota(jnp.int32, (1, t), 1)
    tok_c = (c % 8) * (t // 8) + c // 8
    return jnp.logical_and(tok_r // CHUNK == tok_c // CHUNK, tok_r >= tok_c)


def _masked_sgu_w(sguw_ref):
    keep = _sgu_keep_mask(sguw_ref.shape[1])
    return [jnp.where(keep, sguw_ref[h], 0.0).astype(BF16) for h in range(SGU_HEADS)]


def _branches_fwd(z, tile_idx, p384_ref, cw_ref, wm, bmat_ref, pwbd_ref, psc_ref, segp_ref, g_ref, hext_ref, zext_ref,
                  h_prev_rot, z_prev_rot, conv_saved=None):
    t = z.shape[0]
    segp = segp_ref[...]
    r = {}
    u, _ = _gelu(z[:, 0:SGU_WIDTH])
    vraw, _ = _gelu(z[:, SGU_WIDTH:2 * SGU_WIDTH])
    xc = vraw - _seg_mean(vraw, segp)
    rstd_v = lax.rsqrt(_seg_mean(xc * xc, segp) + EPS)
    xh_v = xc * rstd_v
    vnb = (xh_v * p384_ref[0:1, :] + p384_ref[1:2, :]).astype(BF16)
    first_head = _lane((t, 128)) < HEAD_DIM
    f_pairs = []
    for pr in range(SGU_HEADS // 2):
        vp = vnb[:, pr * 128:(pr + 1) * 128]
        f_pairs.append(jnp.where(first_head, _dot(wm[2 * pr], vp), _dot(wm[2 * pr + 1], vp)))
    f = jnp.concatenate(f_pairs, axis=1) + bmat_ref[...]
    ya = u * f
    r.update(u=u, xh_v=xh_v, rstd_v=rstd_v, vnb=vnb, f=f)
    o_b = 2 * SGU_WIDTH
    a_in = z[:, o_b:o_b + CONV_WIDTH]
    sig_g = jax.nn.sigmoid(z[:, o_b + CONV_WIDTH:o_b + 2 * CONV_WIDTH])
    hh = a_in * sig_g
    hext_ref[0:t, :], r["h_rot"] = _causal_tail(hh, h_prev_rot)
    hext_ref[t:2 * t, :] = hh
    if conv_saved is None:
        conv = jnp.zeros((t, CONV_WIDTH), F32) + p384_ref[2:3, :]
        for k in range(CONV_K):
            conv = conv + cw_ref[k:k + 1, :] * hext_ref[pl.ds(t - 8 * (CONV_K - 1 - k), t), :]
        r["conv"] = conv
    else:
        conv = conv_saved
    cc = conv - _rowmean(conv)
    rstd_c = lax.rsqrt(_rowmean(cc * cc) + EPS)
    xh_c = cc * rstd_c
    cn = xh_c * p384_ref[3:4, :] + p384_ref[4:5, :]
    sig_c = jax.nn.sigmoid(cn)
    yb = cn * sig_c
    r.update(a_in=a_in, sig_g=sig_g, xh_c=xh_c, rstd_c=rstd_c, cn=cn, sig_c=sig_c)
    o_c = o_b + 2 * CONV_WIDTH
    zc = z[:, o_c:o_c + POOL_WIDTH]
    zext_ref[0:POOL_HALO, :], r["z_rot"] = _causal_tail(zc[t - POOL_HALO:t, :], z_prev_rot)
    zext_ref[POOL_HALO:POOL_HALO + t, :] = zc
    sums, acc = [], zc
    for j in range(1, POOL_WINDOWS[-1]):
        acc = acc + zext_ref[pl.ds(POOL_HALO - 8 * j, t), :]
        if j + 1 in POOL_WINDOWS:
            sums.append(acc)
    inv = _pool_inv_counts(t, tile_idx)
    dpool = _by_pool_group((t, POOL_WIDTH), [s * iv for s, iv in zip(sums, inv)]) - zc
    ycp = _dot(dpool.astype(BF16), pwbd_ref[...])
    yc = ycp * psc_ref[0:1, :]
    r.update(dpool=dpool, ycp=ycp)
    yha, ra = _rms_fwd(ya)
    yhb, rb = _rms_fwd(yb)
    yhc, rc = _rms_fwd(yc)
    bg = g_ref[2:3, :]
    ycat = jnp.concatenate([yha * bg[:, 0:384], yhb * bg[:, 384:768], yhc * bg[:, 768:1024]], axis=1)
    r.update(yha=yha, ra=ra, yhb=yhb, rb=rb, yhc=yhc, rc=rc, ycat=ycat)
    return r


def _mixer_fwd(x, modv, g1024, p384, cw, sguw, bmat, pwbd, psc, segp, win, wout, name="mixer_fwd", job=None,
               natural_x=False):
    s_len = x.shape[0]
    t = MIX_TILE

    def body(x_ref, mod_ref, g_ref, p384_ref, cw_ref, sguw_ref, bmat_ref, pwbd_ref, psc_ref, segp_ref, win_ref, wout_ref,
             x1_ref, z_ref, o_ref, conv_ref, hext_ref, zext_ref, hrot_ref, zrot_ref, wm_ref):
        i = pl.program_id(0)

        @pl.when(i == 0)
        def _():
            hrot_ref[...] = jnp.zeros_like(hrot_ref)
            zrot_ref[...] = jnp.zeros_like(zrot_ref)
            for h, wmh in enumerate(_masked_sgu_w(sguw_ref)):
                wm_ref[h] = wmh

        xv = _interleave(x_ref[...]) if natural_x else x_ref[...]
        sh1, sc1, g1 = mod_ref[0:1, :], mod_ref[1:2, :], mod_ref[2:3, :]
        xhat, _ = _rms_fwd(xv)
        h1 = xhat * g_ref[0:1, :] * (1.0 + sc1) + sh1
        z = _dot_nt(h1.astype(BF16), win_ref[...])
        z_ref[...] = z
        r = _branches_fwd(z, i, p384_ref, cw_ref, [wm_ref[h] for h in range(SGU_HEADS)], bmat_ref, pwbd_ref, psc_ref,
                          segp_ref, g_ref, hext_ref, zext_ref, hrot_ref[...], zrot_ref[...])
        hrot_ref[...] = r["h_rot"]
        zrot_ref[...] = r["z_rot"]
        conv_ref[...] = r["conv"]
        o = _dot(r["ycat"].astype(BF16), wout_ref[...])
        o_ref[...] = o
        ohat, _ = _rms_fwd(o)
        x1_ref[...] = xv + g1 * (ohat * g_ref[1:2, :])

    tile = pl.BlockSpec((t, D_MODEL), lambda i: (i, 0))
    consts = (modv, g1024, p384, cw, sguw, bmat, pwbd, psc, segp, win, wout)
    return _pallas_call_with_exchange(
        body,
        grid=(s_len // t,),
        in_specs=[tile] + [_whole(c.shape) for c in consts],
        out_specs=[tile, pl.BlockSpec((t, IN_WIDTH), lambda i: (i, 0)), tile,
                   pl.BlockSpec((t, CONV_WIDTH), lambda i: (i, 0))],
        out_shape=[jax.ShapeDtypeStruct((s_len, D_MODEL), F32), jax.ShapeDtypeStruct((s_len, IN_WIDTH), F32),
                   jax.ShapeDtypeStruct((s_len, D_MODEL), F32), jax.ShapeDtypeStruct((s_len, CONV_WIDTH), F32)],
        scratch_shapes=[pltpu.VMEM((2 * t, CONV_WIDTH), F32), pltpu.VMEM((POOL_HALO + t, POOL_WIDTH), F32),
                        pltpu.VMEM((t, CONV_WIDTH), F32), pltpu.VMEM((POOL_HALO, POOL_WIDTH), F32),
                        pltpu.VMEM((SGU_HEADS, t, t), BF16)],
        operands=(x, *consts),
        name=name, job=job)


def _mixer_bwd(dx1, x, o, z, conv, modv, g1024, p384, cw, sguw, bmat, pwbd, psc, segp, win, wout, name="mixer_bwd",
               job=None, natural_x=False):
    s_len = x.shape[0]
    t = MIX_TILE
    n_tiles = s_len // t

    def body(dx1_ref, x_ref, o_ref, z_ref, zh_ref, conv_ref, mod_ref, g_ref, p384_ref, cw_ref, sguw_ref, bmat_ref, pwbd_ref,
             psc_ref, segp_ref, win_ref, wout_ref,
             dx_ref, dz_ref, do_ref, ycat_ref, h1_ref, vec_ref, v384_ref, dcw_ref, dsguw_ref, dbmat_ref, dpw_ref,
             dpsc_ref, hext_ref, zext_ref, gext_ref, qext_ref, grot_ref, qrot_ref, wm_ref):
        i = pl.program_id(0)
        tile_idx = n_tiles - 1 - i

        @pl.when(i == 0)
        def _():
            for ref in (vec_ref, v384_ref, dcw_ref, dsguw_ref, dbmat_ref, dpw_ref, dpsc_ref, grot_ref, qrot_ref):
                ref[...] = jnp.zeros_like(ref)
            for h, wmh in enumerate(_masked_sgu_w(sguw_ref)):
                wm_ref[h] = wmh

        dx1v, ov, z = dx1_ref[...], o_ref[...], z_ref[...]
        xv = _interleave(x_ref[...]) if natural_x else x_ref[...]
        sh1, sc1, g1 = mod_ref[0:1, :], mod_ref[1:2, :], mod_ref[2:3, :]
        pre_g, post_g, bg = g_ref[0:1, :], g_ref[1:2, :], g_ref[2:3, :]
        segp = segp_ref[...]

        ohat, ro = _rms_fwd(ov)
        vec_ref[1:2, :] += _colsum(dx1v * (ohat * post_g))
        don = dx1v * g1
        vec_ref[0:1, :] += _colsum(don * ohat)
        dob = _rms_bwd(don * post_g, ohat, ro).astype(BF16)
        do_ref[...] = dob
        dycat = _dot_nt(dob, wout_ref[...])

        not_first = (tile_idx > 0).astype(F32)
        o_b = 2 * SGU_WIDTH
        o_c = o_b + 2 * CONV_WIDTH
        h_prev = zh_ref[:, o_b:o_b + CONV_WIDTH] * jax.nn.sigmoid(zh_ref[:, o_b + CONV_WIDTH:o_c])
        h_prev_rot = _rot_rows(h_prev, 1) * not_first
        z_prev_rot = _rot_rows(zh_ref[t - POOL_HALO:t, o_c:o_c + POOL_WIDTH], 1) * not_first
        wm = [wm_ref[h] for h in range(SGU_HEADS)]
        r = _branches_fwd(z, tile_idx, p384_ref, cw_ref, wm, bmat_ref, pwbd_ref, psc_ref, segp_ref, g_ref,
                          hext_ref, zext_ref, h_prev_rot, z_prev_rot, conv_saved=conv_ref[...])
        ycat_ref[...] = r["ycat"].astype(BF16)

        def branch_norm_bwd(dyn, yhat, rr, gain):
            return _colsum(dyn * yhat), _rms_bwd(dyn * gain, yhat, rr)

        dga, dya = branch_norm_bwd(dycat[:, 0:384], r["yha"], r["ra"], bg[:, 0:384])
        dgb, dyb = branch_norm_bwd(dycat[:, 384:768], r["yhb"], r["rb"], bg[:, 384:768])
        dgc, dyc = branch_norm_bwd(dycat[:, 768:1024], r["yhc"], r["rc"], bg[:, 768:1024])
        vec_ref[5:6, :] += jnp.concatenate([dga, dgb, dgc], axis=1)

        du_act = dya * r["f"]
        df = dya * r["u"]
        first_head = _lane((t, 128)) < HEAD_DIM
        dbmat_ref[...] += df
        dvn_pairs = []
        for pr in range(SGU_HEADS // 2):
            dfp = df[:, pr * 128:(pr + 1) * 128]
            df0 = jnp.where(first_head, dfp, 0.0).astype(BF16)
            df1 = jnp.where(first_head, 0.0, dfp).astype(BF16)
            vp = r["vnb"][:, pr * 128:(pr + 1) * 128]
            dvn_pairs.append(_dot_tn(wm[2 * pr], df0) + _dot_tn(wm[2 * pr + 1], df1))
            dsguw_ref[2 * pr] += _dot_nt(df0, vp)
            dsguw_ref[2 * pr + 1] += _dot_nt(df1, vp)
        dvn = jnp.concatenate(dvn_pairs, axis=1)
        v384_ref[0:1, :] += _colsum(dvn * r["xh_v"])
        v384_ref[1:2, :] += _colsum(dvn)
        dxh = dvn * p384_ref[0:1, :]
        dvraw = r["rstd_v"] * (dxh - _seg_mean(dxh, segp) - r["xh_v"] * _seg_mean(dxh * r["xh_v"], segp))
        zu, zv = z[:, 0:SGU_WIDTH], z[:, SGU_WIDTH:o_b]
        _, tu = _gelu(zu)
        _, tv = _gelu(zv)
        dz_u = du_act * _gelu_grad(zu, tu)
        dz_v = dvraw * _gelu_grad(zv, tv)

        cn, sig_c = r["cn"], r["sig_c"]
        dcn = dyb * (sig_c * (1.0 + cn * (1.0 - sig_c)))
        v384_ref[3:4, :] += _colsum(dcn * r["xh_c"])
        v384_ref[4:5, :] += _colsum(dcn)
        dxc = dcn * p384_ref[3:4, :]
        gconv = r["rstd_c"] * (dxc - _rowmean(dxc) - r["xh_c"] * _rowmean(dxc * r["xh_c"]))
        v384_ref[2:3, :] += _colsum(gconv)
        gext_ref[0:t, :] = gconv
        gext_ref[t:2 * t, :], grot_ref[...] = _lookahead_head(gconv, grot_ref[...])
        dhh = jnp.zeros((t, CONV_WIDTH), F32)
        for k in range(CONV_K):
            shift = CONV_K - 1 - k
            dcw_ref[k:k + 1, :] += _colsum(gconv * hext_ref[pl.ds(t - 8 * shift, t), :])
            dhh = dhh + cw_ref[k:k + 1, :] * gext_ref[pl.ds(8 * shift, t), :]
        sig_g = r["sig_g"]
        dz_a = dhh * sig_g
        dz_g = dhh * r["a_in"] * sig_g * (1.0 - sig_g)

        dpsc_ref[0:1, :] += _colsum(dyc * r["ycp"])
        dycp = (dyc * psc_ref[0:1, :]).astype(BF16)
        dpw_ref[...] += _dot_tn(r["dpool"].astype(BF16), dycp)
        ddp = _dot_nt(dycp, pwbd_ref[...])
        inv = _pool_inv_counts(t, tile_idx)
        q = ddp * _by_pool_group((t, POOL_WIDTH), inv)
        qext_ref[0:t, :] = q
        qext_ref[t:t + POOL_HALO, :], qrot_ref[...] = _lookahead_head(q[0:POOL_HALO, :], qrot_ref[...])
        sums, acc = [], q
        for j in range(1, POOL_WINDOWS[-1]):
            acc = acc + qext_ref[pl.ds(8 * j, t), :]
            if j + 1 in POOL_WINDOWS:
                sums.append(acc)
        dz_c = _by_pool_group((t, POOL_WIDTH), sums) - ddp

        dzb = jnp.concatenate([dz_u, dz_v, dz_a, dz_g, dz_c], axis=1).astype(BF16)
        dz_ref[...] = dzb
        dh1 = _dot(dzb, win_ref[...])

        xhat, rx = _rms_fwd(xv)
        xn = xhat * pre_g
        h1_ref[...] = (xn * (1.0 + sc1) + sh1).astype(BF16)
        vec_ref[2:3, :] += _colsum(dh1)
        vec_ref[3:4, :] += _colsum(dh1 * xn)
        dxn = dh1 * (1.0 + sc1)
        vec_ref[4:5, :] += _colsum(dxn * xhat)
        dx = dx1v + _rms_bwd(dxn * pre_g, xhat, rx)
        dx_ref[...] = _deinterleave(dx) if natural_x else dx

        @pl.when(i == n_tiles - 1)
        def _():
            keep = _sgu_keep_mask(t)
            for h in range(SGU_HEADS):
                dsguw_ref[h] = jnp.where(keep, dsguw_ref[h], 0.0)
            dbmat_ref[...] = float(HEAD_DIM) * _seg_mean(dbmat_ref[...], segp)

    rev = lambda i: (n_tiles - 1 - i, 0)
    tile = pl.BlockSpec((t, D_MODEL), rev)
    ztile = pl.BlockSpec((t, IN_WIDTH), rev)
    zhalo = pl.BlockSpec((t, IN_WIDTH), lambda i: (jnp.maximum(n_tiles - 2 - i, 0), 0))
    consts = (modv, g1024, p384, cw, sguw, bmat, pwbd, psc, segp, win, wout)
    acc = lambda shape: pl.BlockSpec(shape, lambda i: (0,) * len(shape))
    acc_shapes = [(8, D_MODEL), (8, SGU_WIDTH), (32, CONV_WIDTH), (SGU_HEADS, t, t), (t, SGU_WIDTH),
                  (POOL_WIDTH, POOL_WIDTH), (8, POOL_WIDTH)]
    return _pallas_call_with_exchange(
        body,
        grid=(n_tiles,),
        in_specs=[tile, tile, tile, ztile, zhalo, pl.BlockSpec((t, CONV_WIDTH), rev)] + [_whole(c.shape) for c in consts],
        out_specs=[tile, ztile, tile, tile, tile] + [acc(s) for s in acc_shapes],
        out_shape=[jax.ShapeDtypeStruct((s_len, D_MODEL), F32), jax.ShapeDtypeStruct((s_len, IN_WIDTH), BF16),
                   jax.ShapeDtypeStruct((s_len, D_MODEL), BF16), jax.ShapeDtypeStruct((s_len, D_MODEL), BF16),
                   jax.ShapeDtypeStruct((s_len, D_MODEL), BF16)] + [jax.ShapeDtypeStruct(s, F32) for s in acc_shapes],
        scratch_shapes=[pltpu.VMEM((2 * t, CONV_WIDTH), F32), pltpu.VMEM((POOL_HALO + t, POOL_WIDTH), F32),
                        pltpu.VMEM((2 * t, CONV_WIDTH), F32), pltpu.VMEM((t + POOL_HALO, POOL_WIDTH), F32),
                        pltpu.VMEM((t, CONV_WIDTH), F32), pltpu.VMEM((POOL_HALO, POOL_WIDTH), F32),
                        pltpu.VMEM((SGU_HEADS, t, t), BF16)],
        operands=(dx1, x, o, z, z, conv, *consts),
        name=name, job=job)


MOD_SHARD = 6 * D_MODEL // N_DEV

ROW_DMOD = 0
ROW_G1024 = 8
ROW_V384 = 16
ROW_SGU_B = 24
ROW_POOL_SCALE = 25
ROW_CONV_W = 32
ROW_FFN_CONV = 64
ROW_POOL_W = 96
ROW_SGU_W = 112
ROWS_PER_LAYER = 208
N_LAYERS = 2


def _gather_weights(c8, mod_w, mod_b8, job):
    kinds = [kind for kind, _ in job]
    shards = [a for _, a in job]
    n = len(shards)

    def body(c_ref, modw_ref, modb_ref, *rest):
        shard_refs = rest[:n]
        sc_all_ref, modrows_ref = rest[n], rest[n + 1]
        full_refs = rest[n + 2:2 * n + 2]
        send_buf, mod_recv, w_send, w_recv, w_local, sc_send, sc_recv, mod_send, mod_recv_sem = rest[2 * n + 2:]
        pos = _my_pos()
        me = _flat(pos)
        peers = [_peer(pos, k) for k in range(1, N_DEV)]

        w_copies = _exchange_copies(kinds, shard_refs, full_refs, w_send, w_recv, w_local, phase=0)
        for cp in w_copies:
            cp.start()

        cv = c_ref[...]
        sc_all_ref[me] = cv * jax.nn.sigmoid(cv)
        sc_copies = [_remote_copy(sc_all_ref.at[me], sc_all_ref.at[me], sc_send.at[k], sc_recv.at[k], peers[k])
                     for k in range(N_PEERS)]
        for cp in sc_copies:
            cp.start()
        for cp in sc_copies:
            cp.wait()

        sc = jnp.concatenate([sc_all_ref[j, 0:1, :] for j in range(N_DEV)], axis=0)
        send_buf[...] = jnp.zeros_like(send_buf)
        for l in range(N_LAYERS):
            part = jnp.dot(sc, modw_ref[l], precision=lax.Precision.HIGHEST, preferred_element_type=F32)
            for j in range(N_DEV):
                send_buf[j, l:l + 1, :] = part[j:j + 1, :]
        mod_recv[me] = send_buf[me]
        mod_copies = [_remote_copy(send_buf.at[_flat(peers[k])], mod_recv.at[me], mod_send.at[k], mod_recv_sem.at[k],
                                   peers[k]) for k in range(N_PEERS)]
        for cp in mod_copies:
            cp.start()
        for cp in mod_copies:
            cp.wait()
        modrows_ref[...] = jnp.zeros_like(modrows_ref)
        for l in range(N_LAYERS):
            row = jnp.concatenate([mod_recv[j, l:l + 1, :] for j in range(N_DEV)], axis=1)
            modrows_ref[l:l + 1, :] = row + modb_ref[l:l + 1, :]

        for cp in w_copies:
            cp.wait()
        relays = _exchange_copies(kinds, shard_refs, full_refs, w_send, w_recv, w_local, phase=1)
        for cp in relays:
            cp.start()
        for cp in relays:
            cp.wait()

    out_shape = ([jax.ShapeDtypeStruct((N_DEV, 8, D_MODEL), F32), jax.ShapeDtypeStruct((8, 6 * D_MODEL), F32)]
                 + [jax.ShapeDtypeStruct((N_DEV,) + s.shape, s.dtype) for s in shards])
    return pl.pallas_call(
        body,
        in_specs=[VMEM, VMEM, VMEM] + [ANY] * n,
        out_specs=[VMEM, VMEM] + [ANY] * n,
        out_shape=out_shape,
        scratch_shapes=[pltpu.VMEM((N_DEV, 8, MOD_SHARD), F32), pltpu.VMEM((N_DEV, 8, MOD_SHARD), F32),
                        pltpu.SemaphoreType.DMA((n, N_PEERS)), pltpu.SemaphoreType.DMA((n, N_PEERS)),
                        pltpu.SemaphoreType.DMA((n,)),
                        pltpu.SemaphoreType.DMA((N_PEERS,)), pltpu.SemaphoreType.DMA((N_PEERS,)),
                        pltpu.SemaphoreType.DMA((N_PEERS,)), pltpu.SemaphoreType.DMA((N_PEERS,))],
        compiler_params=pltpu.CompilerParams(vmem_limit_bytes=VMEM_LIMIT_BYTES),
        name="gather_weights",
    )(c8, mod_w, mod_b8, *shards)


def _small_sums(sc_all, small_all0, small_all1, job):
    kinds = [kind for kind, _ in job]
    n = len(job)

    def body(sc_all_ref, small_all0_ref, small_all1_ref, *rest):
        small_sum_ref, gmodw_ref = rest[n], rest[n + 1]
        copies = _exchange_copies(kinds, rest[:n], rest[n + 2:2 * n + 2], *rest[2 * n + 2:], phase=0)
        for cp in copies:
            cp.start()
        me = _flat(_my_pos())
        sc = jnp.concatenate([sc_all_ref[j, 0:1, :] for j in range(N_DEV)], axis=0)
        mine = lax.broadcasted_iota(jnp.int32, (2 * N_DEV, MOD_SHARD), 0) == me
        for l, parts in enumerate((small_all0_ref, small_all1_ref)):
            total = parts[0].astype(F32)
            for j in range(1, N_DEV):
                total = total + parts[j].astype(F32)
            small_sum_ref[l] = total
            dm = jnp.concatenate(
                [jnp.sum(jnp.where(mine, parts[j, ROW_DMOD:ROW_DMOD + 2 * N_DEV, 0:MOD_SHARD].astype(F32), 0.0),
                         axis=0, keepdims=True) for j in range(N_DEV)], axis=0)
            gmodw_ref[l] = lax.dot_general(sc, dm, (((0,), (0,)), ((), ())), precision=lax.Precision.HIGHEST,
                                           preferred_element_type=F32)
        for cp in copies:
            cp.wait()

    res = pl.pallas_call(
        body,
        in_specs=[VMEM, VMEM, VMEM] + [ANY] * n,
        out_specs=[VMEM, VMEM] + [ANY] * n,
        out_shape=[jax.ShapeDtypeStruct((N_LAYERS, ROWS_PER_LAYER, D_MODEL), F32),
                   jax.ShapeDtypeStruct((N_LAYERS, D_MODEL, MOD_SHARD), F32)] + _exchange_out_shapes(job),
        scratch_shapes=_exchange_sems(n),
        compiler_params=pltpu.CompilerParams(vmem_limit_bytes=VMEM_LIMIT_BYTES),
        name="small_sums",
    )(sc_all, small_all0, small_all1, *[a for _, a in job])
    return res[0], res[1], res[2:]


def _adam_update(g, w, m, v):
    m2 = ADAM_B1 * m + (1.0 - ADAM_B1) * g
    v2 = ADAM_B2 * v + (1.0 - ADAM_B2) * (g * g)
    m_hat = m2 / (1.0 - ADAM_B1 ** ADAM_STEP)
    v_hat = v2 / (1.0 - ADAM_B2 ** ADAM_STEP)
    delta = -ADAM_LR * (m_hat / (jnp.sqrt(v_hat) + ADAM_EPS) + ADAM_WD * w)
    return delta, m2, v2


def _pair_add(g, r1, row_chunk, name):
    _, rows, cols = g.shape
    core = lax.axis_index("c").astype(jnp.int32).reshape(1)

    def body(core_ref, g_ref, r_ref, o_ref):
        o_ref[0] = (g_ref[0, 0].astype(F32) + r_ref[0].astype(F32)).astype(BF16)

    blk = pl.BlockSpec((1, row_chunk, cols), lambda q, i, core_ref: (q, i, 0))
    grid_spec = pltpu.PrefetchScalarGridSpec(
        num_scalar_prefetch=1, grid=(N_CHIPS, rows // row_chunk),
        in_specs=[pl.BlockSpec((1, 1, row_chunk, cols), lambda q, i, core_ref: (q, core_ref[0], i, 0)), blk],
        out_specs=blk)
    return pl.pallas_call(
        body, grid_spec=grid_spec, out_shape=jax.ShapeDtypeStruct((N_CHIPS, rows, cols), BF16),
        compiler_params=_cparams(2), name=name,
    )(core, g.reshape(N_CHIPS, 2, rows, cols), r1)


def _adam_sharded(recv0, recv1, w, m, v, row_chunk, name):
    _, rows, cols = w.shape
    n_chunks = rows // row_chunk

    def body(r0_ref, r1_ref, w_ref, m_ref, v_ref, g_ref, d_ref, m2_ref, v2_ref):
        layer = pl.program_id(0)

        def run(r_ref):
            g = r_ref[0].astype(F32)
            for j in range(1, r_ref.shape[0]):
                g = g + r_ref[j].astype(F32)
            delta, m2, v2 = _adam_update(g, w_ref[0], m_ref[0], v_ref[0])
            g_ref[0], d_ref[0], m2_ref[0], v2_ref[0] = g, delta, m2, v2

        @pl.when(layer == 0)
        def _():
            run(r0_ref)

        @pl.when(layer == 1)
        def _():
            run(r1_ref)

    r0_spec = pl.BlockSpec((recv0.shape[0], row_chunk, cols), lambda l, i: (0, i * (1 - l) + (n_chunks - 1) * l, 0))
    r1_spec = pl.BlockSpec((recv1.shape[0], row_chunk, cols), lambda l, i: (0, i * l, 0))
    blk = pl.BlockSpec((1, row_chunk, cols), lambda l, i: (l, i, 0))
    out = jax.ShapeDtypeStruct(w.shape, F32)
    return pl.pallas_call(
        body,
        grid=(N_LAYERS, n_chunks),
        in_specs=[r0_spec, r1_spec, blk, blk, blk],
        out_specs=[blk] * 4,
        out_shape=[out] * 4,
        compiler_params=_cparams(2),
        name=name,
    )(recv0, recv1, w, m, v)


def _adam_dense(g, w, m, v, row_chunk, name):
    n_lead, rows, cols = w.shape

    def body(g_ref, w_ref, m_ref, v_ref, go_ref, d_ref, m2_ref, v2_ref):
        gv = g_ref[...]
        go_ref[...] = gv
        d_ref[...], m2_ref[...], v2_ref[...] = _adam_update(gv, w_ref[...], m_ref[...], v_ref[...])

    blk = pl.BlockSpec((1, row_chunk, cols), lambda l, i: (l, i, 0))
    out = jax.ShapeDtypeStruct(w.shape, F32)
    return pl.pallas_call(
        body,
        grid=(n_lead, rows // row_chunk),
        in_specs=[blk] * 4,
        out_specs=[blk] * 4,
        out_shape=[out] * 4,
        compiler_params=_cparams(2),
        name=name,
    )(g, w, m, v)


WEIGHT_NAMES = ("mod_w", "mod_b", "mix_pre_g", "mix_post_g", "w_in", "sgu_norm_g", "sgu_norm_b", "sgu_w", "sgu_b",
                "conv_w", "conv_b", "conv_norm_g", "conv_norm_b", "pool_w", "pool_scale", "branch_g", "w_out",
                "ffn_pre_g", "ffn_post_g", "ffn_up", "ffn_conv_w", "ffn_conv_b", "ffn_down")
SHARDED_BIG = ("w_in", "w_out", "ffn_up", "ffn_down")
SMALL_PACKED = tuple(n for n in WEIGHT_NAMES if n not in SHARDED_BIG + ("mod_w",))


def _rows8(rows, width=D_MODEL):
    out = [jnp.pad(r.astype(F32), (0, width - r.shape[0]))[None] for r in rows]
    out.append(jnp.zeros((8 - len(rows), width), F32))
    return jnp.concatenate(out, axis=0)


def _as_rows(a, width=D_MODEL):
    flat = a.astype(F32).reshape(-1)
    pad = (-flat.shape[0]) % width
    return jnp.pad(flat, (0, pad)).reshape(-1, width)


def _pad_cols(a, width=D_MODEL):
    return jnp.pad(a.astype(F32), ((0, 0), (0, width - a.shape[1])))


def _pack_rows(arrays):
    rows = jnp.concatenate([_as_rows(a) for a in arrays], axis=0)
    return jnp.pad(rows, ((0, (-rows.shape[0]) % 8), (0, 0)))


def _unpack_rows(packed, shapes):
    out, r = [], 0
    for shape in shapes:
        size = math.prod(shape)
        n_rows = -(-size // D_MODEL)
        out.append(packed[r:r + n_rows].reshape(-1)[:size].reshape(shape))
        r += n_rows
    return out


TILE_VREGS = MIX_TILE // 8
CHUNK_SUBLANES = CHUNK // TILE_VREGS
CHUNKS_PER_TILE = MIX_TILE // CHUNK


def _chunk_axis_to_tile(a, axis):
    shape = a.shape
    a = a.reshape(shape[:axis] + (CHUNK_SUBLANES, TILE_VREGS) + shape[axis + 1:])
    a = jnp.swapaxes(a, axis, axis + 1)
    a = jnp.tile(a, (1,) * (axis + 1) + (CHUNKS_PER_TILE,) + (1,) * (len(shape) - axis - 1))
    return a.reshape(shape[:axis] + (MIX_TILE,) + shape[axis + 1:])


def _tile_axis_to_natural(a, axis):
    shape = a.shape
    a = a.reshape(shape[:axis] + (TILE_VREGS, 8) + shape[axis + 1:])
    return jnp.swapaxes(a, axis, axis + 1).reshape(shape)


def _layer_consts(l, w, mod_rows, win, wout, conv_w_full):
    modv = _rows8(list(mod_rows[l].reshape(6, D_MODEL)))
    g1024 = _rows8([w["mix_pre_g"][l], w["mix_post_g"][l], w["branch_g"][l], w["ffn_pre_g"][l], w["ffn_post_g"][l]])
    p384 = _rows8([w["sgu_norm_g"][l], w["sgu_norm_b"][l], w["conv_b"][l], w["conv_norm_g"][l], w["conv_norm_b"][l]],
                  SGU_WIDTH)
    cw = jnp.pad(conv_w_full[l], ((0, 32 - CONV_K), (0, 0)))
    sguw = _chunk_axis_to_tile(_chunk_axis_to_tile(w["sgu_w"][l], 2), 1)
    bmat = _chunk_axis_to_tile(jnp.repeat(w["sgu_b"][l].T, HEAD_DIM, axis=1), 0)
    groups = len(POOL_WINDOWS)
    eye = jnp.eye(groups, dtype=F32)
    pwbd = (eye[:, None, :, None] * w["pool_w"][l][:, :, None, :]).reshape(POOL_WIDTH, POOL_WIDTH).astype(BF16)
    psc = _rows8([w["pool_scale"][l]], POOL_WIDTH)
    seg = jnp.arange(SGU_WIDTH) // HEAD_DIM
    segp = jnp.where(seg[:, None] == seg[None, :], 1.0 / HEAD_DIM, 0.0).astype(BF16)
    return modv, g1024, (modv, g1024, p384, cw, sguw, bmat, pwbd, psc, segp, win, wout)


def _small_grad_rows(mix, ffn):
    _, _, _, _, _, mvec, v384, dcw, dsguw, dbmat, dpw, dpsc = mix
    fvec, cgrad = ffn[5], ffn[6]
    dmod = jnp.stack([mvec[2], mvec[3], mvec[1], fvec[2], fvec[3], fvec[1]]).reshape(N_DEV, MOD_SHARD)
    g_rows = jnp.stack([mvec[4], mvec[0], mvec[5], fvec[4], fvec[0]])
    dbmat = _tile_axis_to_natural(dbmat, 0).reshape(CHUNKS_PER_TILE, CHUNK, SGU_WIDTH).sum(axis=0)
    dsguw = _tile_axis_to_natural(_tile_axis_to_natural(dsguw, 1), 2)
    dsguw = dsguw.reshape(SGU_HEADS, CHUNKS_PER_TILE, CHUNK, CHUNKS_PER_TILE, CHUNK)
    dsguw = sum(dsguw[:, b, :, b, :] for b in range(CHUNKS_PER_TILE))
    dsgu_b = dbmat[:, ::HEAD_DIM].T.reshape(1, SGU_HEADS * CHUNK)
    groups = len(POOL_WINDOWS)
    gdim = POOL_WIDTH // groups
    dpw4 = dpw.reshape(groups, gdim, groups, gdim)
    dpool = jnp.stack([dpw4[g, :, g, :] for g in range(groups)])
    blocks = [_pad_cols(dmod), _rows8(list(g_rows)), _pad_cols(v384), _rows8([dsgu_b[0], dpsc[0]]), _pad_cols(dcw),
              _pad_cols(cgrad[:, 0:4, :].reshape(4 * N_DEV, FF_SHARD)), _as_rows(dpool), _as_rows(dsguw)]
    return jnp.concatenate(blocks, axis=0)


def _small_grads_from_rows(total):
    per = {n: [] for n in SMALL_PACKED}
    for l in range(N_LAYERS):
        s = total[l]
        per["mod_b"].append(s[ROW_DMOD:ROW_DMOD + N_DEV, :MOD_SHARD].reshape(6 * D_MODEL))
        for j, name in enumerate(("mix_pre_g", "mix_post_g", "branch_g", "ffn_pre_g", "ffn_post_g")):
            per[name].append(s[ROW_G1024 + j])
        for j, name in enumerate(("sgu_norm_g", "sgu_norm_b", "conv_b", "conv_norm_g", "conv_norm_b")):
            per[name].append(s[ROW_V384 + j, :SGU_WIDTH])
        per["sgu_b"].append(s[ROW_SGU_B, :SGU_HEADS * CHUNK].reshape(SGU_HEADS, CHUNK))
        per["pool_scale"].append(s[ROW_POOL_SCALE, :POOL_WIDTH])
        per["conv_w"].append(s[ROW_CONV_W:ROW_CONV_W + CONV_K, :CONV_WIDTH])
        fc = s[ROW_FFN_CONV:ROW_FFN_CONV + 4 * N_DEV, :FF_SHARD].reshape(N_DEV, 4, FF_SHARD)
        per["ffn_conv_w"].append(fc[:, 0:3, :].transpose(1, 0, 2).reshape(FFN_CONV_K, 2 * D_FF))
        per["ffn_conv_b"].append(fc[:, 3, :].reshape(2 * D_FF))
        per["pool_w"].append(s[ROW_POOL_W:ROW_POOL_W + 16].reshape(len(POOL_WINDOWS), HEAD_DIM, HEAD_DIM))
        per["sgu_w"].append(s[ROW_SGU_W:ROW_SGU_W + 96].reshape(SGU_HEADS, CHUNK, CHUNK))
    return {n: jnp.stack(v) for n, v in per.items()}


def kernel(x, c, mod_w, mod_b, mix_pre_g, mix_post_g, w_in, sgu_norm_g, sgu_norm_b, sgu_w, sgu_b, conv_w, conv_b, conv_norm_g, conv_norm_b, pool_w, pool_scale, branch_g, w_out, ffn_pre_g, ffn_post_g, ffn_up, ffn_conv_w, ffn_conv_b, ffn_down, loss_target, m_mod_w, m_mod_b, m_mix_pre_g, m_mix_post_g, m_w_in, m_sgu_norm_g, m_sgu_norm_b, m_sgu_w, m_sgu_b, m_conv_w, m_conv_b, m_conv_norm_g, m_conv_norm_b, m_pool_w, m_pool_scale, m_branch_g, m_w_out, m_ffn_pre_g, m_ffn_post_g, m_ffn_up, m_ffn_conv_w, m_ffn_conv_b, m_ffn_down, v_mod_w, v_mod_b, v_mix_pre_g, v_mix_post_g, v_w_in, v_sgu_norm_g, v_sgu_norm_b, v_sgu_w, v_sgu_b, v_conv_w, v_conv_b, v_conv_norm_g, v_conv_norm_b, v_pool_w, v_pool_scale, v_branch_g, v_w_out, v_ffn_pre_g, v_ffn_post_g, v_ffn_up, v_ffn_conv_w, v_ffn_conv_b, v_ffn_down):
    w = dict(zip(WEIGHT_NAMES, (mod_w, mod_b, mix_pre_g, mix_post_g, w_in, sgu_norm_g, sgu_norm_b, sgu_w, sgu_b, conv_w,
                                conv_b, conv_norm_g, conv_norm_b, pool_w, pool_scale, branch_g, w_out, ffn_pre_g,
                                ffn_post_g, ffn_up, ffn_conv_w, ffn_conv_b, ffn_down)))
    m = dict(zip(WEIGHT_NAMES, (m_mod_w, m_mod_b, m_mix_pre_g, m_mix_post_g, m_w_in, m_sgu_norm_g, m_sgu_norm_b, m_sgu_w,
                                m_sgu_b, m_conv_w, m_conv_b, m_conv_norm_g, m_conv_norm_b, m_pool_w, m_pool_scale,
                                m_branch_g, m_w_out, m_ffn_pre_g, m_ffn_post_g, m_ffn_up, m_ffn_conv_w, m_ffn_conv_b,
                                m_ffn_down)))
    v = dict(zip(WEIGHT_NAMES, (v_mod_w, v_mod_b, v_mix_pre_g, v_mix_post_g, v_w_in, v_sgu_norm_g, v_sgu_norm_b, v_sgu_w,
                                v_sgu_b, v_conv_w, v_conv_b, v_conv_norm_g, v_conv_norm_b, v_pool_w, v_pool_scale,
                                v_branch_g, v_w_out, v_ffn_pre_g, v_ffn_post_g, v_ffn_up, v_ffn_conv_w, v_ffn_conv_b,
                                v_ffn_down)))
    me = _flat(_my_pos())
    xs = x[0]
    s_len = xs.shape[0]

    transposed = ("w_in", "ffn_up")
    wt = {n: jnp.swapaxes(w[n], 1, 2) if n in transposed else w[n] for n in SHARDED_BIG}
    mt = {n: jnp.swapaxes(m[n], 1, 2) if n in transposed else m[n] for n in SHARDED_BIG}
    vt = {n: jnp.swapaxes(v[n], 1, 2) if n in transposed else v[n] for n in SHARDED_BIG}
    bf16_shards = [[wt[n][l].astype(BF16) for n in SHARDED_BIG] for l in range(N_LAYERS)]

    def mixer_operands(l, win_g, wout_g):
        win = win_g.reshape(IN_WIDTH, D_MODEL)
        return _layer_consts(l, w, mod_rows, win, wout_g.reshape(D_MODEL, D_MODEL), conv_w_full)

    def ffn_operands(l, modv, g1024, wup_g, wdn_g):
        wdn = wdn_g.reshape(FF_PAIRS, FF_SHARD, D_MODEL)
        return modv, g1024, wup_g, wdn, ffn_cw_full[:, l], ffn_conv_b[l].reshape(N_DEV, 1, FF_SHARD)

    w0, w1 = bf16_shards
    c8 = jnp.broadcast_to(c, (8, D_MODEL))
    mod_b8 = jnp.pad(mod_b, ((0, 8 - N_LAYERS), (0, 0)))
    sc_all, mod_rows, win0_g, wout0_g, conv_w_g, ffn_cw_full = _gather_weights(
        c8, mod_w, mod_b8, [("gather2", w0[0]), ("gather2", w0[1]), ("gather", conv_w), ("gather", ffn_conv_w)])
    conv_w_full = conv_w_g.transpose(1, 2, 0, 3).reshape(N_LAYERS, CONV_K, CONV_WIDTH)

    modv0, g0, mix_consts0 = mixer_operands(0, win0_g, wout0_g)
    (x1, z, o, cv), (wup0_g, wdn0_g) = _mixer_fwd(xs, *mix_consts0, name="mixer_fwd_l0", natural_x=True,
                                                  job=[("gather2", w0[2]), ("gather2", w0[3])])
    ffn_consts0 = ffn_operands(0, modv0, g0, wup0_g, wdn0_g)
    (x2, y2, p, u), (win1_g, wout1_g, wdn1_g) = _ffn_fwd(
        x1, *ffn_consts0, name="ffn_fwd_l0", job=[("gather2", w1[0]), ("gather2", w1[1]), ("gather2", w1[3])])
    saved = [(xs, z, o, cv, x1, y2, p, u)]
    modv1, g1, mix_consts1 = mixer_operands(1, win1_g, wout1_g)
    (x1, z, o, cv), (wup1_g,) = _mixer_fwd(x2, *mix_consts1, name="mixer_fwd_l1", job=[("gather2", w1[2])])
    ffn_consts1 = ffn_operands(1, modv1, g1, wup1_g, wdn1_g)
    (dh, y2, p, u, loss_tile), _ = _ffn_fwd(x1, *ffn_consts1, name="ffn_fwd_l1", loss_target=loss_target[0])
    saved.append((x2, z, o, cv, x1, y2, p, u))
    loss = lax.psum(loss_tile[0, 0], ("x", "y", "c"))

    def ffn_weight_grads(l, ffn):
        dp, a, dy2, h2 = ffn[1:5]
        d_up = _wgrad(dp, h2, f"wgrad_ffn_up_l{l}", tk=WGRAD_TK_FFN)
        d_dn = _wgrad(a, dy2, f"wgrad_ffn_down_l{l}", tk=WGRAD_TK_FFN).reshape(N_DEV, D_FF // N_DEV, D_MODEL)
        return d_up, d_dn

    def mixer_weight_grads(l, mix):
        dz, do, ycat, h1 = mix[1:5]
        d_in = _wgrad(dz[None], h1, f"wgrad_w_in_l{l}").reshape(N_DEV, IN_WIDTH // N_DEV, D_MODEL)
        d_out = _wgrad(ycat, do[None], f"wgrad_w_out_l{l}").reshape(N_DEV, D_MODEL // N_DEV, D_MODEL)
        return d_in, d_out

    x_in, z, o, cv, x1, y2, p, u = saved[1]
    ffn1, _ = _ffn_bwd(dh, x1, y2, p, u, *ffn_consts1, name="ffn_bwd_l1")
    d_up1, d_dn1 = ffn_weight_grads(1, ffn1)
    mix1, (sib_up1, sib_dn1) = _mixer_bwd(ffn1[0], x_in, o, z, cv, *mix_consts1, name="mixer_bwd_l1",
                                          job=[("scatter_p1", d_up1), ("scatter_p1", d_dn1)])
    chip_up1 = _pair_add(d_up1, sib_up1, 176, "pair_add_ffn_up_l1")
    chip_dn1 = _pair_add(d_dn1, sib_dn1, 176, "pair_add_ffn_down_l1")
    d_in1, d_out1 = mixer_weight_grads(1, mix1)
    small1 = _small_grad_rows(mix1, ffn1).astype(BF16)

    x_in, z, o, cv, x1, y2, p, u = saved[0]
    ffn0, job_out = _ffn_bwd(mix1[0], x1, y2, p, u, *ffn_consts0, name="ffn_bwd_l0",
                             job=[("scatter", d_in1), ("scatter", d_out1), ("scatter_p2", chip_up1),
                                  ("scatter_p2", chip_dn1), ("gather", small1)])
    recv1, small_all1 = job_out[0:4], job_out[4]
    dp, a, dy2, h2 = ffn0[1:5]
    d_dn0 = _wgrad(a, dy2, "wgrad_ffn_down_l0", tk=WGRAD_TK_FFN).reshape(N_DEV, D_FF // N_DEV, D_MODEL)
    d_up0, (recv_dn0,) = _wgrad(dp, h2, "wgrad_ffn_up_l0", tk=WGRAD_TK_FFN, job=[("scatter", d_dn0)])
    mix0, (recv_up0,) = _mixer_bwd(ffn0[0], x_in, o, z, cv, *mix_consts0, name="mixer_bwd_l0", natural_x=True,
                                   job=[("scatter", d_up0)])
    recv_ffn0 = (recv_up0, recv_dn0)
    grad_x = mix0[0][None]
    dz, do, ycat, h1 = mix0[1:5]
    small0 = _small_grad_rows(mix0, ffn0).astype(BF16)
    d_out0, (small_all0,) = _wgrad(ycat, do[None], "wgrad_w_out_l0", job=[("gather", small0)])
    d_out0 = d_out0.reshape(N_DEV, D_MODEL // N_DEV, D_MODEL)
    d_in0, (recv_out0,) = _wgrad(dz[None], h1, "wgrad_w_in_l0", job=[("scatter", d_out0)])
    d_in0 = d_in0.reshape(N_DEV, IN_WIDTH // N_DEV, D_MODEL)
    small_total, g_mod_w, (recv_in0,) = _small_sums(sc_all, small_all0, small_all1, [("scatter", d_in0)])
    recv0 = [recv_in0, recv_out0, recv_ffn0[0], recv_ffn0[1]]

    grads, deltas, new_m, new_v = {}, {}, {}, {}
    for j, (name, chunk) in enumerate((("w_in", 224), ("w_out", 128), ("ffn_up", 176), ("ffn_down", 176))):
        outs = _adam_sharded(recv0[j], recv1[j], wt[name], mt[name], vt[name], chunk, "adam_" + name)
        if name in transposed:
            outs = [jnp.swapaxes(t, 1, 2) for t in outs]
        grads[name], deltas[name], new_m[name], new_v[name] = outs
    grads["mod_w"], deltas["mod_w"], new_m["mod_w"], new_v["mod_w"] = _adam_dense(
        g_mod_w, mod_w, m_mod_w, v_mod_w, 256, "adam_mod_w")

    small_g = _small_grads_from_rows(small_total)
    small_g["conv_w"] = lax.dynamic_slice_in_dim(small_g["conv_w"], me * conv_w.shape[2], conv_w.shape[2], axis=2)
    small_g["ffn_conv_w"] = lax.dynamic_slice_in_dim(small_g["ffn_conv_w"], me * FF_SHARD, FF_SHARD, axis=2)
    shapes = [w[n].shape for n in SMALL_PACKED]
    packs = [_pack_rows([src[n] for n in SMALL_PACKED])[None] for src in (small_g, w, m, v)]
    _, d, m2, v2 = _adam_dense(*packs, packs[0].shape[1], "adam_small")
    for name, dd, mm, vv in zip(SMALL_PACKED, _unpack_rows(d[0], shapes), _unpack_rows(m2[0], shapes),
                                _unpack_rows(v2[0], shapes)):
        grads[name], deltas[name], new_m[name], new_v[name] = small_g[name], dd, mm, vv

    return (loss, grad_x, *[grads[n] for n in WEIGHT_NAMES], *[deltas[n] for n in WEIGHT_NAMES],
            *[new_m[n] for n in WEIGHT_NAMES], *[new_v[n] for n in WEIGHT_NAMES])
```

```python
import math

import jax
import jax.numpy as jnp
from jax import lax
from jax.experimental import pallas as pl
from jax.experimental.pallas import tpu as pltpu

F32 = jnp.float32
BF16 = jnp.bfloat16

D_MODEL = 1024
N_DEV = 8
SGU_WIDTH = 384
CONV_WIDTH = 384
POOL_WIDTH = 256
HEAD_DIM = 64
SGU_HEADS = 6
CHUNK = 128
CONV_K = 31
POOL_WINDOWS = (2, 4, 8, 16)
IN_WIDTH = 1792
D_FF = 2816
FF_SHARD = 2 * D_FF // N_DEV
FF_PAIRS = N_DEV // 2
FFN_CONV_K = 3
EPS = 1e-6
GELU_C0 = math.sqrt(2.0 / math.pi)
GELU_C1 = 0.044715

ADAM_LR = 0.001
ADAM_B1 = 0.9
ADAM_B2 = 0.999
ADAM_EPS = 1e-08
ADAM_WD = 0.01
ADAM_STEP = 10

VMEM_LIMIT_BYTES = 56 * 1024 * 1024
TILE = 256
MIX_TILE = TILE
FFN_TILE = TILE
FFN_HALO = 8 * (FFN_CONV_K - 1)
POOL_HALO = 8 * POOL_WINDOWS[-1]
WGRAD_TK = 2048
WGRAD_TK_FFN = 4096


def _cparams(n_axes):
    return pltpu.CompilerParams(dimension_semantics=("arbitrary",) * n_axes, vmem_limit_bytes=VMEM_LIMIT_BYTES)


def _whole(shape):
    nd = len(shape)
    return pl.BlockSpec(shape, lambda *_: (0,) * nd, pipeline_mode=pl.Buffered(1))


def _dot(a, b):
    return jnp.dot(a, b, preferred_element_type=F32)


def _dot_nt(a, b):
    return lax.dot_general(a, b, (((1,), (1,)), ((), ())), preferred_element_type=F32)


def _dot_tn(a, b):
    return lax.dot_general(a, b, (((0,), (0,)), ((), ())), preferred_element_type=F32)


def _gelu(x):
    t = jnp.tanh(GELU_C0 * (x + GELU_C1 * x * x * x))
    return 0.5 * x * (1.0 + t), t


def _gelu_grad(x, t):
    return 0.5 * (1.0 + t) + 0.5 * x * (1.0 - t * t) * (GELU_C0 * (1.0 + 3.0 * GELU_C1 * x * x))


def _rowmean(x):
    return jnp.mean(x, axis=-1, keepdims=True)


def _colsum(x):
    return jnp.sum(x, axis=0, keepdims=True)


def _rms_fwd(x):
    r = lax.rsqrt(_rowmean(x * x) + EPS)
    return x * r, r


def _rms_bwd(dxhat, xhat, r):
    return r * (dxhat - xhat * _rowmean(dxhat * xhat))


N_PEERS = N_DEV - 1
ANY = pl.BlockSpec(memory_space=pl.ANY)
VMEM = pl.BlockSpec(memory_space=pltpu.VMEM)


def _my_pos():
    return lax.axis_index("x"), lax.axis_index("y"), lax.axis_index("c")


def _peer(pos, k):
    x, y, c = pos
    return (1 - x if k & 4 else x, 1 - y if k & 2 else y, 1 - c if k & 1 else c)


def _flat(pos):
    return 4 * pos[0] + 2 * pos[1] + pos[2]


def _remote_copy(src, dst, send_sem, recv_sem, peer):
    return pltpu.make_async_remote_copy(src_ref=src, dst_ref=dst, send_sem=send_sem, recv_sem=recv_sem,
                                        device_id=peer, device_id_type=pl.DeviceIdType.MESH)


N_CHIPS = N_DEV // 2
SIBLING = 1
SAME_CORE_PEERS = (2, 4, 6)


def _exchange_out_shapes(job):
    def shape(kind, a):
        if kind in ("gather", "gather2"):
            return (N_DEV,) + a.shape
        if kind == "scatter_p1":
            return (N_CHIPS,) + a.shape[1:]
        return a.shape
    return [jax.ShapeDtypeStruct(shape(kind, a), a.dtype) for kind, a in job]


def _exchange_sems(n):
    return [pltpu.SemaphoreType.DMA((n, N_PEERS)), pltpu.SemaphoreType.DMA((n, N_PEERS)), pltpu.SemaphoreType.DMA((n,))]


def _exchange_copies(kinds, src_refs, dst_refs, send_sems, recv_sems, local_sems, phase):
    pos = _my_pos()
    me = _flat(pos)
    chip, core = 2 * pos[0] + pos[1], pos[2]
    copies = []

    def remote(a, src, dst, k, sem=None):
        sem = k - 1 if sem is None else sem
        copies.append(_remote_copy(src, dst, send_sems.at[a, sem], recv_sems.at[a, sem], _peer(pos, k)))

    for a, kind in enumerate(kinds):
        src, dst = src_refs[a], dst_refs[a]
        if phase == 1:
            if kind == "gather2":
                for k in SAME_CORE_PEERS:
                    remote(a, dst.at[me ^ k], dst.at[me ^ k], SIBLING, sem=k)
        elif kind in ("gather", "gather2"):
            copies.append(pltpu.make_async_copy(src, dst.at[me], local_sems.at[a]))
            for k in (range(1, N_DEV) if kind == "gather" else (SIBLING,) + SAME_CORE_PEERS):
                remote(a, src, dst.at[me], k)
        elif kind == "scatter":
            copies.append(pltpu.make_async_copy(src.at[me], dst.at[me], local_sems.at[a]))
            for k in range(1, N_DEV):
                remote(a, src.at[me ^ k], dst.at[me], k)
        elif kind == "scatter_p1":
            for q in range(N_CHIPS):
                remote(a, src.at[2 * q + 1 - core], dst.at[q], SIBLING, sem=q)
        elif kind == "scatter_p2":
            copies.append(pltpu.make_async_copy(src.at[chip], dst.at[chip], local_sems.at[a]))
            for k in SAME_CORE_PEERS:
                remote(a, src.at[chip ^ (k >> 1)], dst.at[chip], k)
    return copies


def _pallas_call_with_exchange(body, *, grid, in_specs, out_specs, out_shape, scratch_shapes, operands, name, job):
    params = _cparams(len(grid))
    if not job:
        outs = pl.pallas_call(body, grid=grid, in_specs=in_specs, out_specs=out_specs, out_shape=out_shape,
                              scratch_shapes=scratch_shapes, compiler_params=params, name=name)(*operands)
        return outs, []
    kinds = [kind for kind, _ in job]
    relayed = [kind if kind == "gather2" else None for kind in kinds]
    unrelayed = [None if kind == "gather2" else kind for kind in kinds]
    n, n_in, n_out, n_scr = len(job), len(in_specs), len(out_specs), len(scratch_shapes)
    n_steps = math.prod(grid)
    relay_step = max(n_steps - 2, 0)

    def wrapped(*refs):
        ins, jin = refs[:n_in], refs[n_in:n_in + n]
        outs, jout = refs[n_in + n:n_in + n + n_out], refs[n_in + n + n_out:n_in + 2 * n + n_out]
        scr = refs[n_in + 2 * n + n_out:n_in + 2 * n + n_out + n_scr]
        sems = refs[n_in + 2 * n + n_out + n_scr:]
        step = pl.program_id(0)
        for d in range(1, len(grid)):
            step = step * grid[d] + pl.program_id(d)

        def copies(which, phase):
            return _exchange_copies(which, jin, jout, *sems, phase=phase)

        @pl.when(step == 0)
        def _():
            for cp in copies(kinds, 0):
                cp.start()

        body(*ins, *outs, *scr)

        @pl.when(step == relay_step)
        def _():
            for cp in copies(relayed, 0):
                cp.wait()
            for cp in copies(relayed, 1):
                cp.start()

        @pl.when(step == n_steps - 1)
        def _():
            for cp in copies(unrelayed, 0) + copies(relayed, 1):
                cp.wait()

    res = pl.pallas_call(
        wrapped, grid=grid,
        in_specs=list(in_specs) + [ANY] * n,
        out_specs=list(out_specs) + [ANY] * n,
        out_shape=list(out_shape) + _exchange_out_shapes(job),
        scratch_shapes=list(scratch_shapes) + _exchange_sems(n),
        compiler_params=params, name=name,
    )(*operands, *[a for _, a in job])
    return res[:n_out], res[n_out:]


def _seg_mean(x, segp):
    hi = x.astype(BF16)
    lo = (x - hi.astype(F32)).astype(BF16)
    return _dot(hi, segp) + _dot(lo, segp)


def _rot_rows(x, shift):
    m, c = x.shape
    return pltpu.roll(x.reshape(m // 8, 8, c), shift, 1).reshape(m, c)


def _sublane_is(shape, s):
    return lax.broadcasted_iota(jnp.int32, shape, 0) % 8 == s


def _causal_tail(tail, prev_rot):
    rot = _rot_rows(tail, 1)
    return jnp.where(_sublane_is(tail.shape, 0), prev_rot, rot), rot


def _lookahead_head(head, next_rot):
    rot = _rot_rows(head, 7)
    return jnp.where(_sublane_is(head.shape, 7), next_rot, rot), rot


def _tile_token_index(t, tile_idx):
    r = lax.broadcasted_iota(jnp.int32, (t, 1), 0)
    return tile_idx * t + (r % 8) * (t // 8) + r // 8


def _interleave(x):
    t, c = x.shape
    return jnp.swapaxes(x.reshape(8, t // 8, c), 0, 1).reshape(t, c)


def _deinterleave(x):
    t, c = x.shape
    return jnp.swapaxes(x.reshape(t // 8, 8, c), 0, 1).reshape(t, c)


def _ffn_fwd(x1, modv, g1024, wup, wdn, cw, cb, name="ffn_fwd", job=None, loss_target=None):
    s_len = x1.shape[0]
    t = FFN_TILE
    n_tiles = s_len // t
    with_loss = loss_target is not None

    def body(x1_ref, *rest):
        if with_loss:
            tgt_ref, rest = rest[0], rest[1:]
            loss_ref, rest = rest[10], rest[:10] + rest[11:]
        mod_ref, g_ref, wup_ref, wdn_ref, cw_ref, cb_ref, x2_ref, y2_ref, p_ref, u_ref, ext_ref, carry_ref = rest
        i = pl.program_id(0)

        @pl.when(i == 0)
        def _():
            carry_ref[...] = jnp.zeros_like(carry_ref)
            if with_loss:
                loss_ref[...] = jnp.zeros_like(loss_ref)

        x1v = x1_ref[...]
        pre_g, post_g = g_ref[3:4, :], g_ref[4:5, :]
        sh2, sc2, g2 = mod_ref[3:4, :], mod_ref[4:5, :], mod_ref[5:6, :]
        xhat, _ = _rms_fwd(x1v)
        h2b = (xhat * pre_g * (1.0 + sc2) + sh2).astype(BF16)

        def conv_shard(s):
            p = _dot_nt(h2b, wup_ref[s])
            p_ref[s] = p.astype(BF16)
            ext_ref[0:FFN_HALO, :], carry_ref[s] = _causal_tail(p[t - FFN_HALO:t, :], carry_ref[s])
            ext_ref[FFN_HALO:FFN_HALO + t, :] = p
            w = cw_ref[s]
            u = w[0:1, :] * ext_ref[0:t, :] + w[1:2, :] * ext_ref[8:8 + t, :] + w[2:3, :] * p + cb_ref[s]
            u_ref[s] = u.astype(BF16)
            return u

        y2 = jnp.zeros((t, D_MODEL), F32)
        for j in range(FF_PAIRS):
            ug = conv_shard(j)
            uv = conv_shard(j + FF_PAIRS)
            ge, _ = _gelu(ug)
            y2 = y2 + _dot((ge * uv).astype(BF16), wdn_ref[j])
        y2_ref[...] = y2
        yhat, _ = _rms_fwd(y2)
        x2 = x1v + g2 * (yhat * post_g)
        if with_loss:
            diff = x2 - _interleave(tgt_ref[...])
            x2_ref[...] = diff * (1.0 / D_MODEL)
            loss_ref[...] += (0.5 / D_MODEL) * jnp.sum(diff * diff)
        else:
            x2_ref[...] = x2

    tile = pl.BlockSpec((t, D_MODEL), lambda i: (i, 0))
    consts = (modv, g1024, wup, wdn, cw, cb)
    shards = pl.BlockSpec((N_DEV, t, FF_SHARD), lambda i: (0, i, 0))
    out_specs = [tile, tile, shards, shards]
    out_shape = [jax.ShapeDtypeStruct((s_len, D_MODEL), F32), jax.ShapeDtypeStruct((s_len, D_MODEL), F32),
                 jax.ShapeDtypeStruct((N_DEV, s_len, FF_SHARD), BF16),
                 jax.ShapeDtypeStruct((N_DEV, s_len, FF_SHARD), BF16)]
    if with_loss:
        out_specs.append(pl.BlockSpec((8, 128), lambda i: (0, 0)))
        out_shape.append(jax.ShapeDtypeStruct((8, 128), F32))
    return _pallas_call_with_exchange(
        body,
        grid=(n_tiles,),
        in_specs=[tile] * (2 if with_loss else 1) + [_whole(c.shape) for c in consts],
        out_specs=out_specs,
        out_shape=out_shape,
        scratch_shapes=[pltpu.VMEM((FFN_HALO + t, FF_SHARD), F32), pltpu.VMEM((N_DEV, FFN_HALO, FF_SHARD), F32)],
        operands=(x1,) + ((loss_target,) if with_loss else ()) + consts,
        name=name, job=job)


def _ffn_bwd(dx2, x1, y2, p, u, modv, g1024, wup, wdn, cw, name="ffn_bwd", job=None):
    s_len = x1.shape[0]
    t = FFN_TILE
    n_tiles = s_len // t
    hb = FFN_HALO

    def body(dx2_ref, x1_ref, y2_ref, p_ref, ph_ref, u_ref, mod_ref, g_ref, wup_ref, wdn_ref, cw_ref,
             dx1_ref, dp_ref, a_ref, dy2_ref, h2_ref, vec_ref, cgrad_ref, ext_ref, dext_ref, dcarry_ref):
        i = pl.program_id(0)
        tile_idx = n_tiles - 1 - i

        @pl.when(i == 0)
        def _():
            vec_ref[...] = jnp.zeros_like(vec_ref)
            cgrad_ref[...] = jnp.zeros_like(cgrad_ref)
            dcarry_ref[...] = jnp.zeros_like(dcarry_ref)

        dx2v, x1v, y2v = dx2_ref[...], x1_ref[...], y2_ref[...]
        pre_g, post_g = g_ref[3:4, :], g_ref[4:5, :]
        sh2, sc2, g2 = mod_ref[3:4, :], mod_ref[4:5, :], mod_ref[5:6, :]

        yhat, ry = _rms_fwd(y2v)
        vec_ref[1:2, :] += _colsum(dx2v * (yhat * post_g))
        dyn = dx2v * g2
        vec_ref[0:1, :] += _colsum(dyn * yhat)
        dy2b = _rms_bwd(dyn * post_g, yhat, ry).astype(BF16)
        dy2_ref[...] = dy2b

        xhat, rx = _rms_fwd(x1v)
        xn = xhat * pre_g
        h2_ref[...] = (xn * (1.0 + sc2) + sh2).astype(BF16)

        not_first = (tile_idx > 0).astype(F32)

        def recompute(s, slot):
            pf = p_ref[s].astype(F32)
            prev_rot = _rot_rows(ph_ref[s].astype(F32), 1) * not_first
            ext_ref[slot, 0:hb, :], _ = _causal_tail(pf[t - hb:t, :], prev_rot)
            ext_ref[slot, hb:hb + t, :] = pf
            return u_ref[s].astype(F32)

        def conv_bwd(s, slot, du):
            w = cw_ref[s]
            cgrad_ref[s, 0:1, :] += _colsum(du * ext_ref[slot, 0:t, :])
            cgrad_ref[s, 1:2, :] += _colsum(du * ext_ref[slot, 8:8 + t, :])
            cgrad_ref[s, 2:3, :] += _colsum(du * ext_ref[slot, 16:16 + t, :])
            cgrad_ref[s, 3:4, :] += _colsum(du)
            dext_ref[0:t, :] = du
            dext_ref[t:t + hb, :], dcarry_ref[s] = _lookahead_head(du[0:hb, :], dcarry_ref[s])
            dp = w[2:3, :] * du + w[1:2, :] * dext_ref[8:8 + t, :] + w[0:1, :] * dext_ref[16:16 + t, :]
            dpb = dp.astype(BF16)
            dp_ref[s] = dpb
            return _dot(dpb, wup_ref[s])

        dh2 = jnp.zeros((t, D_MODEL), F32)
        for j in range(FF_PAIRS):
            ug = recompute(j, 0)
            uv = recompute(j + FF_PAIRS, 1)
            ge, th = _gelu(ug)
            a_ref[j] = (ge * uv).astype(BF16)
            da = _dot_nt(dy2b, wdn_ref[j])
            dh2 = dh2 + conv_bwd(j, 0, da * uv * _gelu_grad(ug, th))
            dh2 = dh2 + conv_bwd(j + FF_PAIRS, 1, da * ge)

        vec_ref[2:3, :] += _colsum(dh2)
        vec_ref[3:4, :] += _colsum(dh2 * xn)
        dxn = dh2 * (1.0 + sc2)
        vec_ref[4:5, :] += _colsum(dxn * xhat)
        dx1_ref[...] = dx2v + _rms_bwd(dxn * pre_g, xhat, rx)

    rev = lambda i: (n_tiles - 1 - i, 0)
    tile = pl.BlockSpec((t, D_MODEL), rev)
    halo_idx = lambda i: (0, jnp.maximum((n_tiles - 1 - i) * (t // hb) - 1, 0), 0)
    return _pallas_call_with_exchange(
        body,
        grid=(n_tiles,),
        in_specs=[tile, tile, tile,
                  pl.BlockSpec((N_DEV, t, FF_SHARD), lambda i: (0, n_tiles - 1 - i, 0)),
                  pl.BlockSpec((N_DEV, hb, FF_SHARD), halo_idx),
                  pl.BlockSpec((N_DEV, t, FF_SHARD), lambda i: (0, n_tiles - 1 - i, 0)),
                  _whole(modv.shape), _whole(g1024.shape), _whole(wup.shape), _whole(wdn.shape),
                  _whole(cw.shape)],
        out_specs=[tile,
                   pl.BlockSpec((N_DEV, t, FF_SHARD), lambda i: (0, n_tiles - 1 - i, 0)),
                   pl.BlockSpec((FF_PAIRS, t, FF_SHARD), lambda i: (0, n_tiles - 1 - i, 0)),
                   tile, tile,
                   pl.BlockSpec((8, D_MODEL), lambda i: (0, 0)),
                   pl.BlockSpec((N_DEV, 8, FF_SHARD), lambda i: (0, 0, 0))],
        out_shape=[jax.ShapeDtypeStruct((s_len, D_MODEL), F32),
                   jax.ShapeDtypeStruct((N_DEV, s_len, FF_SHARD), BF16),
                   jax.ShapeDtypeStruct((FF_PAIRS, s_len, FF_SHARD), BF16),
                   jax.ShapeDtypeStruct((s_len, D_MODEL), BF16),
                   jax.ShapeDtypeStruct((s_len, D_MODEL), BF16),
                   jax.ShapeDtypeStruct((8, D_MODEL), F32),
                   jax.ShapeDtypeStruct((N_DEV, 8, FF_SHARD), F32)],
        scratch_shapes=[pltpu.VMEM((2, hb + t, FF_SHARD), F32), pltpu.VMEM((t + hb, FF_SHARD), F32),
                        pltpu.VMEM((N_DEV, hb, FF_SHARD), F32)],
        operands=(dx2, x1, y2, p, p, u, modv, g1024, wup, wdn, cw),
        name=name, job=job)


def _wgrad(a, b, name, tk=WGRAD_TK, job=None):
    a_grouped, b_grouped = a.ndim == 3, b.ndim == 3
    groups = a.shape[0] if a_grouped else b.shape[0]
    s_len, m, n = a.shape[-2], a.shape[-1], b.shape[-1]
    tk = min(tk, s_len)
    n_k = s_len // tk

    def body(a_ref, b_ref, o_ref, acc_ref):
        k = pl.program_id(1)
        av = a_ref[0] if a_grouped else a_ref[...]
        bv = b_ref[0] if b_grouped else b_ref[...]
        part = _dot_tn(av, bv)
        if n_k == 1:
            o_ref[0] = part.astype(BF16)
            return

        @pl.when(k == 0)
        def _():
            acc_ref[...] = part

        @pl.when(jnp.logical_and(k > 0, k < n_k - 1))
        def _():
            acc_ref[...] += part

        @pl.when(k == n_k - 1)
        def _():
            o_ref[0] = (acc_ref[...] + part).astype(BF16)

    a_spec = pl.BlockSpec((1, tk, m), lambda g, k: (g, k, 0)) if a_grouped else pl.BlockSpec((tk, m), lambda g, k: (k, 0))
    b_spec = pl.BlockSpec((1, tk, n), lambda g, k: (g, k, 0)) if b_grouped else pl.BlockSpec((tk, n), lambda g, k: (k, 0))
    (out,), exchanged = _pallas_call_with_exchange(
        body,
        grid=(groups, n_k),
        in_specs=[a_spec, b_spec],
        out_specs=[pl.BlockSpec((1, m, n), lambda g, k: (g, 0, 0))],
        out_shape=[jax.ShapeDtypeStruct((groups, m, n), BF16)],
        scratch_shapes=[pltpu.VMEM((m, n), F32)],
        operands=(a, b),
        name=name, job=job)
    return (out, exchanged) if job else out


def _lane(shape):
    return lax.broadcasted_iota(jnp.int32, shape, 1)


def _by_pool_group(shape, vals):
    lane = _lane(shape)
    return jnp.where(lane < 64, vals[0], jnp.where(lane < 128, vals[1], jnp.where(lane < 192, vals[2], vals[3])))


def _pool_inv_counts(t, tile_idx):
    pos1 = _tile_token_index(t, tile_idx) + 1
    return [1.0 / jnp.minimum(pos1, w).astype(F32) for w in POOL_WINDOWS]


def _sgu_keep_mask(t):
    tok_r = _tile_token_index(t, 0)
    c = lax.broadcasted_iota(jnp.int32, (1, t), 1)
    tok_c = (c % 8) * (t // 8) + c // 8
    return jnp.logical_and(tok_r // CHUNK == tok_c // CHUNK, tok_r >= tok_c)


def _masked_sgu_w(sguw_ref):
    keep = _sgu_keep_mask(sguw_ref.shape[1])
    return [jnp.where(keep, sguw_ref[h], 0.0).astype(BF16) for h in range(SGU_HEADS)]


def _branches_fwd(z, tile_idx, p384_ref, cw_ref, wm, bmat_ref, pwbd_ref, psc_ref, segp_ref, g_ref, hext_ref, zext_ref,
                  h_prev_rot, z_prev_rot, conv_saved=None):
    t = z.shape[0]
    segp = segp_ref[...]
    r = {}
    u, _ = _gelu(z[:, 0:SGU_WIDTH])
    vraw, _ = _gelu(z[:, SGU_WIDTH:2 * SGU_WIDTH])
    xc = vraw - _seg_mean(vraw, segp)
    rstd_v = lax.rsqrt(_seg_mean(xc * xc, segp) + EPS)
    xh_v = xc * rstd_v
    vnb = (xh_v * p384_ref[0:1, :] + p384_ref[1:2, :]).astype(BF16)
    first_head = _lane((t, 128)) < HEAD_DIM
    f_pairs = []
    for pr in range(SGU_HEADS // 2):
        vp = vnb[:, pr * 128:(pr + 1) * 128]
        f_pairs.append(jnp.where(first_head, _dot(wm[2 * pr], vp), _dot(wm[2 * pr + 1], vp)))
    f = jnp.concatenate(f_pairs, axis=1) + bmat_ref[...]
    ya = u * f
    r.update(u=u, xh_v=xh_v, rstd_v=rstd_v, vnb=vnb, f=f)
    o_b = 2 * SGU_WIDTH
    a_in = z[:, o_b:o_b + CONV_WIDTH]
    sig_g = jax.nn.sigmoid(z[:, o_b + CONV_WIDTH:o_b + 2 * CONV_WIDTH])
    hh = a_in * sig_g
    hext_ref[0:t, :], r["h_rot"] = _causal_tail(hh, h_prev_rot)
    hext_ref[t:2 * t, :] = hh
    if conv_saved is None:
        conv = jnp.zeros((t, CONV_WIDTH), F32) + p384_ref[2:3, :]
        for k in range(CONV_K):
            conv = conv + cw_ref[k:k + 1, :] * hext_ref[pl.ds(t - 8 * (CONV_K - 1 - k), t), :]
        r["conv"] = conv
    else:
        conv = conv_saved
    cc = conv - _rowmean(conv)
    rstd_c = lax.rsqrt(_rowmean(cc * cc) + EPS)
    xh_c = cc * rstd_c
    cn = xh_c * p384_ref[3:4, :] + p384_ref[4:5, :]
    sig_c = jax.nn.sigmoid(cn)
    yb = cn * sig_c
    r.update(a_in=a_in, sig_g=sig_g, xh_c=xh_c, rstd_c=rstd_c, cn=cn, sig_c=sig_c)
    o_c = o_b + 2 * CONV_WIDTH
    zc = z[:, o_c:o_c + POOL_WIDTH]
    zext_ref[0:POOL_HALO, :], r["z_rot"] = _causal_tail(zc[t - POOL_HALO:t, :], z_prev_rot)
    zext_ref[POOL_HALO:POOL_HALO + t, :] = zc
    sums, acc = [], zc
    for j in range(1, POOL_WINDOWS[-1]):
        acc = acc + zext_ref[pl.ds(POOL_HALO - 8 * j, t), :]
        if j + 1 in POOL_WINDOWS:
            sums.append(acc)
    inv = _pool_inv_counts(t, tile_idx)
    dpool = _by_pool_group((t, POOL_WIDTH), [s * iv for s, iv in zip(sums, inv)]) - zc
    ycp = _dot(dpool.astype(BF16), pwbd_ref[...])
    yc = ycp * psc_ref[0:1, :]
    r.update(dpool=dpool, ycp=ycp)
    yha, ra = _rms_fwd(ya)
    yhb, rb = _rms_fwd(yb)
    yhc, rc = _rms_fwd(yc)
    bg = g_ref[2:3, :]
    ycat = jnp.concatenate([yha * bg[:, 0:384], yhb * bg[:, 384:768], yhc * bg[:, 768:1024]], axis=1)
    r.update(yha=yha, ra=ra, yhb=yhb, rb=rb, yhc=yhc, rc=rc, ycat=ycat)
    return r


def _mixer_fwd(x, modv, g1024, p384, cw, sguw, bmat, pwbd, psc, segp, win, wout, name="mixer_fwd", job=None,
               natural_x=False):
    s_len = x.shape[0]
    t = MIX_TILE

    def body(x_ref, mod_ref, g_ref, p384_ref, cw_ref, sguw_ref, bmat_ref, pwbd_ref, psc_ref, segp_ref, win_ref, wout_ref,
             x1_ref, z_ref, o_ref, conv_ref, hext_ref, zext_ref, hrot_ref, zrot_ref, wm_ref):
        i = pl.program_id(0)

        @pl.when(i == 0)
        def _():
            hrot_ref[...] = jnp.zeros_like(hrot_ref)
            zrot_ref[...] = jnp.zeros_like(zrot_ref)
            for h, wmh in enumerate(_masked_sgu_w(sguw_ref)):
                wm_ref[h] = wmh

        xv = _interleave(x_ref[...]) if natural_x else x_ref[...]
        sh1, sc1, g1 = mod_ref[0:1, :], mod_ref[1:2, :], mod_ref[2:3, :]
        xhat, _ = _rms_fwd(xv)
        h1 = xhat * g_ref[0:1, :] * (1.0 + sc1) + sh1
        z = _dot_nt(h1.astype(BF16), win_ref[...])
        z_ref[...] = z
        r = _branches_fwd(z, i, p384_ref, cw_ref, [wm_ref[h] for h in range(SGU_HEADS)], bmat_ref, pwbd_ref, psc_ref,
                          segp_ref, g_ref, hext_ref, zext_ref, hrot_ref[...], zrot_ref[...])
        hrot_ref[...] = r["h_rot"]
        zrot_ref[...] = r["z_rot"]
        conv_ref[...] = r["conv"]
        o = _dot(r["ycat"].astype(BF16), wout_ref[...])
        o_ref[...] = o
        ohat, _ = _rms_fwd(o)
        x1_ref[...] = xv + g1 * (ohat * g_ref[1:2, :])

    tile = pl.BlockSpec((t, D_MODEL), lambda i: (i, 0))
    consts = (modv, g1024, p384, cw, sguw, bmat, pwbd, psc, segp, win, wout)
    return _pallas_call_with_exchange(
        body,
        grid=(s_len // t,),
        in_specs=[tile] + [_whole(c.shape) for c in consts],
        out_specs=[tile, pl.BlockSpec((t, IN_WIDTH), lambda i: (i, 0)), tile,
                   pl.BlockSpec((t, CONV_WIDTH), lambda i: (i, 0))],
        out_shape=[jax.ShapeDtypeStruct((s_len, D_MODEL), F32), jax.ShapeDtypeStruct((s_len, IN_WIDTH), F32),
                   jax.ShapeDtypeStruct((s_len, D_MODEL), F32), jax.ShapeDtypeStruct((s_len, CONV_WIDTH), F32)],
        scratch_shapes=[pltpu.VMEM((2 * t, CONV_WIDTH), F32), pltpu.VMEM((POOL_HALO + t, POOL_WIDTH), F32),
                        pltpu.VMEM((t, CONV_WIDTH), F32), pltpu.VMEM((POOL_HALO, POOL_WIDTH), F32),
                        pltpu.VMEM((SGU_HEADS, t, t), BF16)],
        operands=(x, *consts),
        name=name, job=job)


def _mixer_bwd(dx1, x, o, z, conv, modv, g1024, p384, cw, sguw, bmat, pwbd, psc, segp, win, wout, name="mixer_bwd",
               job=None, natural_x=False):
    s_len = x.shape[0]
    t = MIX_TILE
    n_tiles = s_len // t

    def body(dx1_ref, x_ref, o_ref, z_ref, zh_ref, conv_ref, mod_ref, g_ref, p384_ref, cw_ref, sguw_ref, bmat_ref, pwbd_ref,
             psc_ref, segp_ref, win_ref, wout_ref,
             dx_ref, dz_ref, do_ref, ycat_ref, h1_ref, vec_ref, v384_ref, dcw_ref, dsguw_ref, dbmat_ref, dpw_ref,
             dpsc_ref, hext_ref, zext_ref, gext_ref, qext_ref, grot_ref, qrot_ref, wm_ref):
        i = pl.program_id(0)
        tile_idx = n_tiles - 1 - i

        @pl.when(i == 0)
        def _():
            for ref in (vec_ref, v384_ref, dcw_ref, dsguw_ref, dbmat_ref, dpw_ref, dpsc_ref, grot_ref, qrot_ref):
                ref[...] = jnp.zeros_like(ref)
            for h, wmh in enumerate(_masked_sgu_w(sguw_ref)):
                wm_ref[h] = wmh

        dx1v, ov, z = dx1_ref[...], o_ref[...], z_ref[...]
        xv = _interleave(x_ref[...]) if natural_x else x_ref[...]
        sh1, sc1, g1 = mod_ref[0:1, :], mod_ref[1:2, :], mod_ref[2:3, :]
        pre_g, post_g, bg = g_ref[0:1, :], g_ref[1:2, :], g_ref[2:3, :]
        segp = segp_ref[...]

        ohat, ro = _rms_fwd(ov)
        vec_ref[1:2, :] += _colsum(dx1v * (ohat * post_g))
        don = dx1v * g1
        vec_ref[0:1, :] += _colsum(don * ohat)
        dob = _rms_bwd(don * post_g, ohat, ro).astype(BF16)
        do_ref[...] = dob
        dycat = _dot_nt(dob, wout_ref[...])

        not_first = (tile_idx > 0).astype(F32)
        o_b = 2 * SGU_WIDTH
        o_c = o_b + 2 * CONV_WIDTH
        h_prev = zh_ref[:, o_b:o_b + CONV_WIDTH] * jax.nn.sigmoid(zh_ref[:, o_b + CONV_WIDTH:o_c])
        h_prev_rot = _rot_rows(h_prev, 1) * not_first
        z_prev_rot = _rot_rows(zh_ref[t - POOL_HALO:t, o_c:o_c + POOL_WIDTH], 1) * not_first
        wm = [wm_ref[h] for h in range(SGU_HEADS)]
        r = _branches_fwd(z, tile_idx, p384_ref, cw_ref, wm, bmat_ref, pwbd_ref, psc_ref, segp_ref, g_ref,
                          hext_ref, zext_ref, h_prev_rot, z_prev_rot, conv_saved=conv_ref[...])
        ycat_ref[...] = r["ycat"].astype(BF16)

        def branch_norm_bwd(dyn, yhat, rr, gain):
            return _colsum(dyn * yhat), _rms_bwd(dyn * gain, yhat, rr)

        dga, dya = branch_norm_bwd(dycat[:, 0:384], r["yha"], r["ra"], bg[:, 0:384])
        dgb, dyb = branch_norm_bwd(dycat[:, 384:768], r["yhb"], r["rb"], bg[:, 384:768])
        dgc, dyc = branch_norm_bwd(dycat[:, 768:1024], r["yhc"], r["rc"], bg[:, 768:1024])
        vec_ref[5:6, :] += jnp.concatenate([dga, dgb, dgc], axis=1)

        du_act = dya * r["f"]
        df = dya * r["u"]
        first_head = _lane((t, 128)) < HEAD_DIM
        dbmat_ref[...] += df
        dvn_pairs = []
        for pr in range(SGU_HEADS // 2):
            dfp = df[:, pr * 128:(pr + 1) * 128]
            df0 = jnp.where(first_head, dfp, 0.0).astype(BF16)
            df1 = jnp.where(first_head, 0.0, dfp).astype(BF16)
            vp = r["vnb"][:, pr * 128:(pr + 1) * 128]
            dvn_pairs.append(_dot_tn(wm[2 * pr], df0) + _dot_tn(wm[2 * pr + 1], df1))
            dsguw_ref[2 * pr] += _dot_nt(df0, vp)
            dsguw_ref[2 * pr + 1] += _dot_nt(df1, vp)
        dvn = jnp.concatenate(dvn_pairs, axis=1)
        v384_ref[0:1, :] += _colsum(dvn * r["xh_v"])
        v384_ref[1:2, :] += _colsum(dvn)
        dxh = dvn * p384_ref[0:1, :]
        dvraw = r["rstd_v"] * (dxh - _seg_mean(dxh, segp) - r["xh_v"] * _seg_mean(dxh * r["xh_v"], segp))
        zu, zv = z[:, 0:SGU_WIDTH], z[:, SGU_WIDTH:o_b]
        _, tu = _gelu(zu)
        _, tv = _gelu(zv)
        dz_u = du_act * _gelu_grad(zu, tu)
        dz_v = dvraw * _gelu_grad(zv, tv)

        cn, sig_c = r["cn"], r["sig_c"]
        dcn = dyb * (sig_c * (1.0 + cn * (1.0 - sig_c)))
        v384_ref[3:4, :] += _colsum(dcn * r["xh_c"])
        v384_ref[4:5, :] += _colsum(dcn)
        dxc = dcn * p384_ref[3:4, :]
        gconv = r["rstd_c"] * (dxc - _rowmean(dxc) - r["xh_c"] * _rowmean(dxc * r["xh_c"]))
        v384_ref[2:3, :] += _colsum(gconv)
        gext_ref[0:t, :] = gconv
        gext_ref[t:2 * t, :], grot_ref[...] = _lookahead_head(gconv, grot_ref[...])
        dhh = jnp.zeros((t, CONV_WIDTH), F32)
        for k in range(CONV_K):
            shift = CONV_K - 1 - k
            dcw_ref[k:k + 1, :] += _colsum(gconv * hext_ref[pl.ds(t - 8 * shift, t), :])
            dhh = dhh + cw_ref[k:k + 1, :] * gext_ref[pl.ds(8 * shift, t), :]
        sig_g = r["sig_g"]
        dz_a = dhh * sig_g
        dz_g = dhh * r["a_in"] * sig_g * (1.0 - sig_g)

        dpsc_ref[0:1, :] += _colsum(dyc * r["ycp"])
        dycp = (dyc * psc_ref[0:1, :]).astype(BF16)
        dpw_ref[...] += _dot_tn(r["dpool"].astype(BF16), dycp)
        ddp = _dot_nt(dycp, pwbd_ref[...])
        inv = _pool_inv_counts(t, tile_idx)
        q = ddp * _by_pool_group((t, POOL_WIDTH), inv)
        qext_ref[0:t, :] = q
        qext_ref[t:t + POOL_HALO, :], qrot_ref[...] = _lookahead_head(q[0:POOL_HALO, :], qrot_ref[...])
        sums, acc = [], q
        for j in range(1, POOL_WINDOWS[-1]):
            acc = acc + qext_ref[pl.ds(8 * j, t), :]
            if j + 1 in POOL_WINDOWS:
                sums.append(acc)
        dz_c = _by_pool_group((t, POOL_WIDTH), sums) - ddp

        dzb = jnp.concatenate([dz_u, dz_v, dz_a, dz_g, dz_c], axis=1).astype(BF16)
        dz_ref[...] = dzb
        dh1 = _dot(dzb, win_ref[...])

        xhat, rx = _rms_fwd(xv)
        xn = xhat * pre_g
        h1_ref[...] = (xn * (1.0 + sc1) + sh1).astype(BF16)
        vec_ref[2:3, :] += _colsum(dh1)
        vec_ref[3:4, :] += _colsum(dh1 * xn)
        dxn = dh1 * (1.0 + sc1)
        vec_ref[4:5, :] += _colsum(dxn * xhat)
        dx = dx1v + _rms_bwd(dxn * pre_g, xhat, rx)
        dx_ref[...] = _deinterleave(dx) if natural_x else dx

        @pl.when(i == n_tiles - 1)
        def _():
            keep = _sgu_keep_mask(t)
            for h in range(SGU_HEADS):
                dsguw_ref[h] = jnp.where(keep, dsguw_ref[h], 0.0)
            dbmat_ref[...] = float(HEAD_DIM) * _seg_mean(dbmat_ref[...], segp)

    rev = lambda i: (n_tiles - 1 - i, 0)
    tile = pl.BlockSpec((t, D_MODEL), rev)
    ztile = pl.BlockSpec((t, IN_WIDTH), rev)
    zhalo = pl.BlockSpec((t, IN_WIDTH), lambda i: (jnp.maximum(n_tiles - 2 - i, 0), 0))
    consts = (modv, g1024, p384, cw, sguw, bmat, pwbd, psc, segp, win, wout)
    acc = lambda shape: pl.BlockSpec(shape, lambda i: (0,) * len(shape))
    acc_shapes = [(8, D_MODEL), (8, SGU_WIDTH), (32, CONV_WIDTH), (SGU_HEADS, t, t), (t, SGU_WIDTH),
                  (POOL_WIDTH, POOL_WIDTH), (8, POOL_WIDTH)]
    return _pallas_call_with_exchange(
        body,
        grid=(n_tiles,),
        in_specs=[tile, tile, tile, ztile, zhalo, pl.BlockSpec((t, CONV_WIDTH), rev)] + [_whole(c.shape) for c in consts],
        out_specs=[tile, ztile, tile, tile, tile] + [acc(s) for s in acc_shapes],
        out_shape=[jax.ShapeDtypeStruct((s_len, D_MODEL), F32), jax.ShapeDtypeStruct((s_len, IN_WIDTH), BF16),
                   jax.ShapeDtypeStruct((s_len, D_MODEL), BF16), jax.ShapeDtypeStruct((s_len, D_MODEL), BF16),
                   jax.ShapeDtypeStruct((s_len, D_MODEL), BF16)] + [jax.ShapeDtypeStruct(s, F32) for s in acc_shapes],
        scratch_shapes=[pltpu.VMEM((2 * t, CONV_WIDTH), F32), pltpu.VMEM((POOL_HALO + t, POOL_WIDTH), F32),
                        pltpu.VMEM((2 * t, CONV_WIDTH), F32), pltpu.VMEM((t + POOL_HALO, POOL_WIDTH), F32),
                        pltpu.VMEM((t, CONV_WIDTH), F32), pltpu.VMEM((POOL_HALO, POOL_WIDTH), F32),
                        pltpu.VMEM((SGU_HEADS, t, t), BF16)],
        operands=(dx1, x, o, z, z, conv, *consts),
        name=name, job=job)


MOD_SHARD = 6 * D_MODEL // N_DEV

ROW_DMOD = 0
ROW_G1024 = 8
ROW_V384 = 16
ROW_SGU_B = 24
ROW_POOL_SCALE = 25
ROW_CONV_W = 32
ROW_FFN_CONV = 64
ROW_POOL_W = 96
ROW_SGU_W = 112
ROWS_PER_LAYER = 208
N_LAYERS = 2


def _gather_weights(c8, mod_w, mod_b8, job):
    kinds = [kind for kind, _ in job]
    shards = [a for _, a in job]
    n = len(shards)

    def body(c_ref, modw_ref, modb_ref, *rest):
        shard_refs = rest[:n]
        sc_all_ref, modrows_ref = rest[n], rest[n + 1]
        full_refs = rest[n + 2:2 * n + 2]
        send_buf, mod_recv, w_send, w_recv, w_local, sc_send, sc_recv, mod_send, mod_recv_sem = rest[2 * n + 2:]
        pos = _my_pos()
        me = _flat(pos)
        peers = [_peer(pos, k) for k in range(1, N_DEV)]

        w_copies = _exchange_copies(kinds, shard_refs, full_refs, w_send, w_recv, w_local, phase=0)
        for cp in w_copies:
            cp.start()

        cv = c_ref[...]
        sc_all_ref[me] = cv * jax.nn.sigmoid(cv)
        sc_copies = [_remote_copy(sc_all_ref.at[me], sc_all_ref.at[me], sc_send.at[k], sc_recv.at[k], peers[k])
                     for k in range(N_PEERS)]
        for cp in sc_copies:
            cp.start()
        for cp in sc_copies:
            cp.wait()

        sc = jnp.concatenate([sc_all_ref[j, 0:1, :] for j in range(N_DEV)], axis=0)
        send_buf[...] = jnp.zeros_like(send_buf)
        for l in range(N_LAYERS):
            part = jnp.dot(sc, modw_ref[l], precision=lax.Precision.HIGHEST, preferred_element_type=F32)
            for j in range(N_DEV):
                send_buf[j, l:l + 1, :] = part[j:j + 1, :]
        mod_recv[me] = send_buf[me]
        mod_copies = [_remote_copy(send_buf.at[_flat(peers[k])], mod_recv.at[me], mod_send.at[k], mod_recv_sem.at[k],
                                   peers[k]) for k in range(N_PEERS)]
        for cp in mod_copies:
            cp.start()
        for cp in mod_copies:
            cp.wait()
        modrows_ref[...] = jnp.zeros_like(modrows_ref)
        for l in range(N_LAYERS):
            row = jnp.concatenate([mod_recv[j, l:l + 1, :] for j in range(N_DEV)], axis=1)
            modrows_ref[l:l + 1, :] = row + modb_ref[l:l + 1, :]

        for cp in w_copies:
            cp.wait()
        relays = _exchange_copies(kinds, shard_refs, full_refs, w_send, w_recv, w_local, phase=1)
        for cp in relays:
            cp.start()
        for cp in relays:
            cp.wait()

    out_shape = ([jax.ShapeDtypeStruct((N_DEV, 8, D_MODEL), F32), jax.ShapeDtypeStruct((8, 6 * D_MODEL), F32)]
                 + [jax.ShapeDtypeStruct((N_DEV,) + s.shape, s.dtype) for s in shards])
    return pl.pallas_call(
        body,
        in_specs=[VMEM, VMEM, VMEM] + [ANY] * n,
        out_specs=[VMEM, VMEM] + [ANY] * n,
        out_shape=out_shape,
        scratch_shapes=[pltpu.VMEM((N_DEV, 8, MOD_SHARD), F32), pltpu.VMEM((N_DEV, 8, MOD_SHARD), F32),
                        pltpu.SemaphoreType.DMA((n, N_PEERS)), pltpu.SemaphoreType.DMA((n, N_PEERS)),
                        pltpu.SemaphoreType.DMA((n,)),
                        pltpu.SemaphoreType.DMA((N_PEERS,)), pltpu.SemaphoreType.DMA((N_PEERS,)),
                        pltpu.SemaphoreType.DMA((N_PEERS,)), pltpu.SemaphoreType.DMA((N_PEERS,))],
        compiler_params=pltpu.CompilerParams(vmem_limit_bytes=VMEM_LIMIT_BYTES),
        name="gather_weights",
    )(c8, mod_w, mod_b8, *shards)


def _small_sums(sc_all, small_all0, small_all1, job):
    kinds = [kind for kind, _ in job]
    n = len(job)

    def body(sc_all_ref, small_all0_ref, small_all1_ref, *rest):
        small_sum_ref, gmodw_ref = rest[n], rest[n + 1]
        copies = _exchange_copies(kinds, rest[:n], rest[n + 2:2 * n + 2], *rest[2 * n + 2:], phase=0)
        for cp in copies:
            cp.start()
        me = _flat(_my_pos())
        sc = jnp.concatenate([sc_all_ref[j, 0:1, :] for j in range(N_DEV)], axis=0)
        mine = lax.broadcasted_iota(jnp.int32, (2 * N_DEV, MOD_SHARD), 0) == me
        for l, parts in enumerate((small_all0_ref, small_all1_ref)):
            total = parts[0].astype(F32)
            for j in range(1, N_DEV):
                total = total + parts[j].astype(F32)
            small_sum_ref[l] = total
            dm = jnp.concatenate(
                [jnp.sum(jnp.where(mine, parts[j, ROW_DMOD:ROW_DMOD + 2 * N_DEV, 0:MOD_SHARD].astype(F32), 0.0),
                         axis=0, keepdims=True) for j in range(N_DEV)], axis=0)
            gmodw_ref[l] = lax.dot_general(sc, dm, (((0,), (0,)), ((), ())), precision=lax.Precision.HIGHEST,
                                           preferred_element_type=F32)
        for cp in copies:
            cp.wait()

    res = pl.pallas_call(
        body,
        in_specs=[VMEM, VMEM, VMEM] + [ANY] * n,
        out_specs=[VMEM, VMEM] + [ANY] * n,
        out_shape=[jax.ShapeDtypeStruct((N_LAYERS, ROWS_PER_LAYER, D_MODEL), F32),
                   jax.ShapeDtypeStruct((N_LAYERS, D_MODEL, MOD_SHARD), F32)] + _exchange_out_shapes(job),
        scratch_shapes=_exchange_sems(n),
        compiler_params=pltpu.CompilerParams(vmem_limit_bytes=VMEM_LIMIT_BYTES),
        name="small_sums",
    )(sc_all, small_all0, small_all1, *[a for _, a in job])
    return res[0], res[1], res[2:]


def _adam_update(g, w, m, v):
    m2 = ADAM_B1 * m + (1.0 - ADAM_B1) * g
    v2 = ADAM_B2 * v + (1.0 - ADAM_B2) * (g * g)
    m_hat = m2 / (1.0 - ADAM_B1 ** ADAM_STEP)
    v_hat = v2 / (1.0 - ADAM_B2 ** ADAM_STEP)
    delta = -ADAM_LR * (m_hat / (jnp.sqrt(v_hat) + ADAM_EPS) + ADAM_WD * w)
    return delta, m2, v2


def _pair_add(g, r1, row_chunk, name):
    _, rows, cols = g.shape
    core = lax.axis_index("c").astype(jnp.int32).reshape(1)

    def body(core_ref, g_ref, r_ref, o_ref):
        o_ref[0] = (g_ref[0, 0].astype(F32) + r_ref[0].astype(F32)).astype(BF16)

    blk = pl.BlockSpec((1, row_chunk, cols), lambda q, i, core_ref: (q, i, 0))
    grid_spec = pltpu.PrefetchScalarGridSpec(
        num_scalar_prefetch=1, grid=(N_CHIPS, rows // row_chunk),
        in_specs=[pl.BlockSpec((1, 1, row_chunk, cols), lambda q, i, core_ref: (q, core_ref[0], i, 0)), blk],
        out_specs=blk)
    return pl.pallas_call(
        body, grid_spec=grid_spec, out_shape=jax.ShapeDtypeStruct((N_CHIPS, rows, cols), BF16),
        compiler_params=_cparams(2), name=name,
    )(core, g.reshape(N_CHIPS, 2, rows, cols), r1)


def _adam_sharded(recv0, recv1, w, m, v, row_chunk, name):
    _, rows, cols = w.shape
    n_chunks = rows // row_chunk

    def body(r0_ref, r1_ref, w_ref, m_ref, v_ref, g_ref, d_ref, m2_ref, v2_ref):
        layer = pl.program_id(0)

        def run(r_ref):
            g = r_ref[0].astype(F32)
            for j in range(1, r_ref.shape[0]):
                g = g + r_ref[j].astype(F32)
            delta, m2, v2 = _adam_update(g, w_ref[0], m_ref[0], v_ref[0])
            g_ref[0], d_ref[0], m2_ref[0], v2_ref[0] = g, delta, m2, v2

        @pl.when(layer == 0)
        def _():
            run(r0_ref)

        @pl.when(layer == 1)
        def _():
            run(r1_ref)

    r0_spec = pl.BlockSpec((recv0.shape[0], row_chunk, cols), lambda l, i: (0, i * (1 - l) + (n_chunks - 1) * l, 0))
    r1_spec = pl.BlockSpec((recv1.shape[0], row_chunk, cols), lambda l, i: (0, i * l, 0))
    blk = pl.BlockSpec((1, row_chunk, cols), lambda l, i: (l, i, 0))
    out = jax.ShapeDtypeStruct(w.shape, F32)
    return pl.pallas_call(
        body,
        grid=(N_LAYERS, n_chunks),
        in_specs=[r0_spec, r1_spec, blk, blk, blk],
        out_specs=[blk] * 4,
        out_shape=[out] * 4,
        compiler_params=_cparams(2),
        name=name,
    )(recv0, recv1, w, m, v)


def _adam_dense(g, w, m, v, row_chunk, name):
    n_lead, rows, cols = w.shape

    def body(g_ref, w_ref, m_ref, v_ref, go_ref, d_ref, m2_ref, v2_ref):
        gv = g_ref[...]
        go_ref[...] = gv
        d_ref[...], m2_ref[...], v2_ref[...] = _adam_update(gv, w_ref[...], m_ref[...], v_ref[...])

    blk = pl.BlockSpec((1, row_chunk, cols), lambda l, i: (l, i, 0))
    out = jax.ShapeDtypeStruct(w.shape, F32)
    return pl.pallas_call(
        body,
        grid=(n_lead, rows // row_chunk),
        in_specs=[blk] * 4,
        out_specs=[blk] * 4,
        out_shape=[out] * 4,
        compiler_params=_cparams(2),
        name=name,
    )(g, w, m, v)


WEIGHT_NAMES = ("mod_w", "mod_b", "mix_pre_g", "mix_post_g", "w_in", "sgu_norm_g", "sgu_norm_b", "sgu_w", "sgu_b",
                "conv_w", "conv_b", "conv_norm_g", "conv_norm_b", "pool_w", "pool_scale", "branch_g", "w_out",
                "ffn_pre_g", "ffn_post_g", "ffn_up", "ffn_conv_w", "ffn_conv_b", "ffn_down")
SHARDED_BIG = ("w_in", "w_out", "ffn_up", "ffn_down")
SMALL_PACKED = tuple(n for n in WEIGHT_NAMES if n not in SHARDED_BIG + ("mod_w",))


def _rows8(rows, width=D_MODEL):
    out = [jnp.pad(r.astype(F32), (0, width - r.shape[0]))[None] for r in rows]
    out.append(jnp.zeros((8 - len(rows), width), F32))
    return jnp.concatenate(out, axis=0)


def _as_rows(a, width=D_MODEL):
    flat = a.astype(F32).reshape(-1)
    pad = (-flat.shape[0]) % width
    return jnp.pad(flat, (0, pad)).reshape(-1, width)


def _pad_cols(a, width=D_MODEL):
    return jnp.pad(a.astype(F32), ((0, 0), (0, width - a.shape[1])))


def _pack_rows(arrays):
    rows = jnp.concatenate([_as_rows(a) for a in arrays], axis=0)
    return jnp.pad(rows, ((0, (-rows.shape[0]) % 8), (0, 0)))


def _unpack_rows(packed, shapes):
    out, r = [], 0
    for shape in shapes:
        size = math.prod(shape)
        n_rows = -(-size // D_MODEL)
        out.append(packed[r:r + n_rows].reshape(-1)[:size].reshape(shape))
        r += n_rows
    return out


TILE_VREGS = MIX_TILE // 8
CHUNK_SUBLANES = CHUNK // TILE_VREGS
CHUNKS_PER_TILE = MIX_TILE // CHUNK


def _chunk_axis_to_tile(a, axis):
    shape = a.shape
    a = a.reshape(shape[:axis] + (CHUNK_SUBLANES, TILE_VREGS) + shape[axis + 1:])
    a = jnp.swapaxes(a, axis, axis + 1)
    a = jnp.tile(a, (1,) * (axis + 1) + (CHUNKS_PER_TILE,) + (1,) * (len(shape) - axis - 1))
    return a.reshape(shape[:axis] + (MIX_TILE,) + shape[axis + 1:])


def _tile_axis_to_natural(a, axis):
    shape = a.shape
    a = a.reshape(shape[:axis] + (TILE_VREGS, 8) + shape[axis + 1:])
    return jnp.swapaxes(a, axis, axis + 1).reshape(shape)


def _layer_consts(l, w, mod_rows, win, wout, conv_w_full):
    modv = _rows8(list(mod_rows[l].reshape(6, D_MODEL)))
    g1024 = _rows8([w["mix_pre_g"][l], w["mix_post_g"][l], w["branch_g"][l], w["ffn_pre_g"][l], w["ffn_post_g"][l]])
    p384 = _rows8([w["sgu_norm_g"][l], w["sgu_norm_b"][l], w["conv_b"][l], w["conv_norm_g"][l], w["conv_norm_b"][l]],
                  SGU_WIDTH)
    cw = jnp.pad(conv_w_full[l], ((0, 32 - CONV_K), (0, 0)))
    sguw = _chunk_axis_to_tile(_chunk_axis_to_tile(w["sgu_w"][l], 2), 1)
    bmat = _chunk_axis_to_tile(jnp.repeat(w["sgu_b"][l].T, HEAD_DIM, axis=1), 0)
    groups = len(POOL_WINDOWS)
    eye = jnp.eye(groups, dtype=F32)
    pwbd = (eye[:, None, :, None] * w["pool_w"][l][:, :, None, :]).reshape(POOL_WIDTH, POOL_WIDTH).astype(BF16)
    psc = _rows8([w["pool_scale"][l]], POOL_WIDTH)
    seg = jnp.arange(SGU_WIDTH) // HEAD_DIM
    segp = jnp.where(seg[:, None] == seg[None, :], 1.0 / HEAD_DIM, 0.0).astype(BF16)
    return modv, g1024, (modv, g1024, p384, cw, sguw, bmat, pwbd, psc, segp, win, wout)


def _small_grad_rows(mix, ffn):
    _, _, _, _, _, mvec, v384, dcw, dsguw, dbmat, dpw, dpsc = mix
    fvec, cgrad = ffn[5], ffn[6]
    dmod = jnp.stack([mvec[2], mvec[3], mvec[1], fvec[2], fvec[3], fvec[1]]).reshape(N_DEV, MOD_SHARD)
    g_rows = jnp.stack([mvec[4], mvec[0], mvec[5], fvec[4], fvec[0]])
    dbmat = _tile_axis_to_natural(dbmat, 0).reshape(CHUNKS_PER_TILE, CHUNK, SGU_WIDTH).sum(axis=0)
    dsguw = _tile_axis_to_natural(_tile_axis_to_natural(dsguw, 1), 2)
    dsguw = dsguw.reshape(SGU_HEADS, CHUNKS_PER_TILE, CHUNK, CHUNKS_PER_TILE, CHUNK)
    dsguw = sum(dsguw[:, b, :, b, :] for b in range(CHUNKS_PER_TILE))
    dsgu_b = dbmat[:, ::HEAD_DIM].T.reshape(1, SGU_HEADS * CHUNK)
    groups = len(POOL_WINDOWS)
    gdim = POOL_WIDTH // groups
    dpw4 = dpw.reshape(groups, gdim, groups, gdim)
    dpool = jnp.stack([dpw4[g, :, g, :] for g in range(groups)])
    blocks = [_pad_cols(dmod), _rows8(list(g_rows)), _pad_cols(v384), _rows8([dsgu_b[0], dpsc[0]]), _pad_cols(dcw),
              _pad_cols(cgrad[:, 0:4, :].reshape(4 * N_DEV, FF_SHARD)), _as_rows(dpool), _as_rows(dsguw)]
    return jnp.concatenate(blocks, axis=0)


def _small_grads_from_rows(total):
    per = {n: [] for n in SMALL_PACKED}
    for l in range(N_LAYERS):
        s = total[l]
        per["mod_b"].append(s[ROW_DMOD:ROW_DMOD + N_DEV, :MOD_SHARD].reshape(6 * D_MODEL))
        for j, name in enumerate(("mix_pre_g", "mix_post_g", "branch_g", "ffn_pre_g", "ffn_post_g")):
            per[name].append(s[ROW_G1024 + j])
        for j, name in enumerate(("sgu_norm_g", "sgu_norm_b", "conv_b", "conv_norm_g", "conv_norm_b")):
            per[name].append(s[ROW_V384 + j, :SGU_WIDTH])
        per["sgu_b"].append(s[ROW_SGU_B, :SGU_HEADS * CHUNK].reshape(SGU_HEADS, CHUNK))
        per["pool_scale"].append(s[ROW_POOL_SCALE, :POOL_WIDTH])
        per["conv_w"].append(s[ROW_CONV_W:ROW_CONV_W + CONV_K, :CONV_WIDTH])
        fc = s[ROW_FFN_CONV:ROW_FFN_CONV + 4 * N_DEV, :FF_SHARD].reshape(N_DEV, 4, FF_SHARD)
        per["ffn_conv_w"].append(fc[:, 0:3, :].transpose(1, 0, 2).reshape(FFN_CONV_K, 2 * D_FF))
        per["ffn_conv_b"].append(fc[:, 3, :].reshape(2 * D_FF))
        per["pool_w"].append(s[ROW_POOL_W:ROW_POOL_W + 16].reshape(len(POOL_WINDOWS), HEAD_DIM, HEAD_DIM))
        per["sgu_w"].append(s[ROW_SGU_W:ROW_SGU_W + 96].reshape(SGU_HEADS, CHUNK, CHUNK))
    return {n: jnp.stack(v) for n, v in per.items()}


def kernel(x, c, mod_w, mod_b, mix_pre_g, mix_post_g, w_in, sgu_norm_g, sgu_norm_b, sgu_w, sgu_b, conv_w, conv_b, conv_norm_g, conv_norm_b, pool_w, pool_scale, branch_g, w_out, ffn_pre_g, ffn_post_g, ffn_up, ffn_conv_w, ffn_conv_b, ffn_down, loss_target, m_mod_w, m_mod_b, m_mix_pre_g, m_mix_post_g, m_w_in, m_sgu_norm_g, m_sgu_norm_b, m_sgu_w, m_sgu_b, m_conv_w, m_conv_b, m_conv_norm_g, m_conv_norm_b, m_pool_w, m_pool_scale, m_branch_g, m_w_out, m_ffn_pre_g, m_ffn_post_g, m_ffn_up, m_ffn_conv_w, m_ffn_conv_b, m_ffn_down, v_mod_w, v_mod_b, v_mix_pre_g, v_mix_post_g, v_w_in, v_sgu_norm_g, v_sgu_norm_b, v_sgu_w, v_sgu_b, v_conv_w, v_conv_b, v_conv_norm_g, v_conv_norm_b, v_pool_w, v_pool_scale, v_branch_g, v_w_out, v_ffn_pre_g, v_ffn_post_g, v_ffn_up, v_ffn_conv_w, v_ffn_conv_b, v_ffn_down):
    w = dict(zip(WEIGHT_NAMES, (mod_w, mod_b, mix_pre_g, mix_post_g, w_in, sgu_norm_g, sgu_norm_b, sgu_w, sgu_b, conv_w,
                                conv_b, conv_norm_g, conv_norm_b, pool_w, pool_scale, branch_g, w_out, ffn_pre_g,
                                ffn_post_g, ffn_up, ffn_conv_w, ffn_conv_b, ffn_down)))
    m = dict(zip(WEIGHT_NAMES, (m_mod_w, m_mod_b, m_mix_pre_g, m_mix_post_g, m_w_in, m_sgu_norm_g, m_sgu_norm_b, m_sgu_w,
                                m_sgu_b, m_conv_w, m_conv_b, m_conv_norm_g, m_conv_norm_b, m_pool_w, m_pool_scale,
                                m_branch_g, m_w_out, m_ffn_pre_g, m_ffn_post_g, m_ffn_up, m_ffn_conv_w, m_ffn_conv_b,
                                m_ffn_down)))
    v = dict(zip(WEIGHT_NAMES, (v_mod_w, v_mod_b, v_mix_pre_g, v_mix_post_g, v_w_in, v_sgu_norm_g, v_sgu_norm_b, v_sgu_w,
                                v_sgu_b, v_conv_w, v_conv_b, v_conv_norm_g, v_conv_norm_b, v_pool_w, v_pool_scale,
                                v_branch_g, v_w_out, v_ffn_pre_g, v_ffn_post_g, v_ffn_up, v_ffn_conv_w, v_ffn_conv_b,
                                v_ffn_down)))
    me = _flat(_my_pos())
    xs = x[0]

    transposed = ("w_in", "ffn_up")
    wt = {n: jnp.swapaxes(w[n], 1, 2) if n in transposed else w[n] for n in SHARDED_BIG}
    mt = {n: jnp.swapaxes(m[n], 1, 2) if n in transposed else m[n] for n in SHARDED_BIG}
    vt = {n: jnp.swapaxes(v[n], 1, 2) if n in transposed else v[n] for n in SHARDED_BIG}
    bf16_shards = [[wt[n][l].astype(BF16) for n in SHARDED_BIG] for l in range(N_LAYERS)]

    def mixer_operands(l, win_g, wout_g):
        win = win_g.reshape(IN_WIDTH, D_MODEL)
        return _layer_consts(l, w, mod_rows, win, wout_g.reshape(D_MODEL, D_MODEL), conv_w_full)

    def ffn_operands(l, modv, g1024, wup_g, wdn_g):
        wdn = wdn_g.reshape(FF_PAIRS, FF_SHARD, D_MODEL)
        return modv, g1024, wup_g, wdn, ffn_cw_full[:, l], ffn_conv_b[l].reshape(N_DEV, 1, FF_SHARD)

    w0, w1 = bf16_shards
    c8 = jnp.broadcast_to(c, (8, D_MODEL))
    mod_b8 = jnp.pad(mod_b, ((0, 8 - N_LAYERS), (0, 0)))
    sc_all, mod_rows, win0_g, wout0_g, conv_w_g, ffn_cw_full = _gather_weights(
        c8, mod_w, mod_b8, [("gather2", w0[0]), ("gather2", w0[1]), ("gather", conv_w), ("gather", ffn_conv_w)])
    conv_w_full = conv_w_g.transpose(1, 2, 0, 3).reshape(N_LAYERS, CONV_K, CONV_WIDTH)

    modv0, g0, mix_consts0 = mixer_operands(0, win0_g, wout0_g)
    (x1, z, o, cv), (wup0_g, wdn0_g) = _mixer_fwd(xs, *mix_consts0, name="mixer_fwd_l0", natural_x=True,
                                                  job=[("gather2", w0[2]), ("gather2", w0[3])])
    ffn_consts0 = ffn_operands(0, modv0, g0, wup0_g, wdn0_g)
    (x2, y2, p, u), (win1_g, wout1_g, wdn1_g) = _ffn_fwd(
        x1, *ffn_consts0, name="ffn_fwd_l0", job=[("gather2", w1[0]), ("gather2", w1[1]), ("gather2", w1[3])])
    saved = [(xs, z, o, cv, x1, y2, p, u)]
    modv1, g1, mix_consts1 = mixer_operands(1, win1_g, wout1_g)
    (x1, z, o, cv), (wup1_g,) = _mixer_fwd(x2, *mix_consts1, name="mixer_fwd_l1", job=[("gather2", w1[2])])
    ffn_consts1 = ffn_operands(1, modv1, g1, wup1_g, wdn1_g)
    (dh, y2, p, u, loss_tile), _ = _ffn_fwd(x1, *ffn_consts1, name="ffn_fwd_l1", loss_target=loss_target[0])
    saved.append((x2, z, o, cv, x1, y2, p, u))
    loss = lax.psum(loss_tile[0, 0], ("x", "y", "c"))

    def ffn_weight_grads(l, ffn):
        dp, a, dy2, h2 = ffn[1:5]
        d_up = _wgrad(dp, h2, f"wgrad_ffn_up_l{l}", tk=WGRAD_TK_FFN)
        d_dn = _wgrad(a, dy2, f"wgrad_ffn_down_l{l}", tk=WGRAD_TK_FFN).reshape(N_DEV, D_FF // N_DEV, D_MODEL)
        return d_up, d_dn

    def mixer_weight_grads(l, mix):
        dz, do, ycat, h1 = mix[1:5]
        d_in = _wgrad(dz[None], h1, f"wgrad_w_in_l{l}").reshape(N_DEV, IN_WIDTH // N_DEV, D_MODEL)
        d_out = _wgrad(ycat, do[None], f"wgrad_w_out_l{l}").reshape(N_DEV, D_MODEL // N_DEV, D_MODEL)
        return d_in, d_out

    x_in, z, o, cv, x1, y2, p, u = saved[1]
    ffn1, _ = _ffn_bwd(dh, x1, y2, p, u, *ffn_consts1[:-1], name="ffn_bwd_l1")
    d_up1, d_dn1 = ffn_weight_grads(1, ffn1)
    mix1, (sib_up1, sib_dn1) = _mixer_bwd(ffn1[0], x_in, o, z, cv, *mix_consts1, name="mixer_bwd_l1",
                                          job=[("scatter_p1", d_up1), ("scatter_p1", d_dn1)])
    chip_up1 = _pair_add(d_up1, sib_up1, 176, "pair_add_ffn_up_l1")
    chip_dn1 = _pair_add(d_dn1, sib_dn1, 176, "pair_add_ffn_down_l1")
    d_in1, d_out1 = mixer_weight_grads(1, mix1)
    small1 = _small_grad_rows(mix1, ffn1).astype(BF16)

    x_in, z, o, cv, x1, y2, p, u = saved[0]
    ffn0, job_out = _ffn_bwd(mix1[0], x1, y2, p, u, *ffn_consts0[:-1], name="ffn_bwd_l0",
                             job=[("scatter", d_in1), ("scatter", d_out1), ("scatter_p2", chip_up1),
                                  ("scatter_p2", chip_dn1), ("gather", small1)])
    recv1, small_all1 = job_out[0:4], job_out[4]
    dp, a, dy2, h2 = ffn0[1:5]
    d_dn0 = _wgrad(a, dy2, "wgrad_ffn_down_l0", tk=WGRAD_TK_FFN).reshape(N_DEV, D_FF // N_DEV, D_MODEL)
    d_up0, (recv_dn0,) = _wgrad(dp, h2, "wgrad_ffn_up_l0", tk=WGRAD_TK_FFN, job=[("scatter", d_dn0)])
    mix0, (recv_up0,) = _mixer_bwd(ffn0[0], x_in, o, z, cv, *mix_consts0, name="mixer_bwd_l0", natural_x=True,
                                   job=[("scatter", d_up0)])
    recv_ffn0 = (recv_up0, recv_dn0)
    grad_x = mix0[0][None]
    dz, do, ycat, h1 = mix0[1:5]
    small0 = _small_grad_rows(mix0, ffn0).astype(BF16)
    d_out0, (small_all0,) = _wgrad(ycat, do[None], "wgrad_w_out_l0", job=[("gather", small0)])
    d_out0 = d_out0.reshape(N_DEV, D_MODEL // N_DEV, D_MODEL)
    d_in0, (recv_out0,) = _wgrad(dz[None], h1, "wgrad_w_in_l0", job=[("scatter", d_out0)])
    d_in0 = d_in0.reshape(N_DEV, IN_WIDTH // N_DEV, D_MODEL)
    small_total, g_mod_w, (recv_in0,) = _small_sums(sc_all, small_all0, small_all1, [("scatter", d_in0)])
    recv0 = [recv_in0, recv_out0, recv_ffn0[0], recv_ffn0[1]]

    grads, deltas, new_m, new_v = {}, {}, {}, {}
    for j, (name, chunk) in enumerate((("w_in", 224), ("w_out", 128), ("ffn_up", 176), ("ffn_down", 176))):
        outs = _adam_sharded(recv0[j], recv1[j], wt[name], mt[name], vt[name], chunk, "adam_" + name)
        if name in transposed:
            outs = [jnp.swapaxes(t, 1, 2) for t in outs]
        grads[name], deltas[name], new_m[name], new_v[name] = outs
    grads["mod_w"], deltas["mod_w"], new_m["mod_w"], new_v["mod_w"] = _adam_dense(
        g_mod_w, mod_w, m_mod_w, v_mod_w, 256, "adam_mod_w")

    small_g = _small_grads_from_rows(small_total)
    small_g["conv_w"] = lax.dynamic_slice_in_dim(small_g["conv_w"], me * conv_w.shape[2], conv_w.shape[2], axis=2)
    small_g["ffn_conv_w"] = lax.dynamic_slice_in_dim(small_g["ffn_conv_w"], me * FF_SHARD, FF_SHARD, axis=2)
    shapes = [w[n].shape for n in SMALL_PACKED]
    packs = [_pack_rows([src[n] for n in SMALL_PACKED])[None] for src in (small_g, w, m, v)]
    _, d, m2, v2 = _adam_dense(*packs, packs[0].shape[1], "adam_small")
    for name, dd, mm, vv in zip(SMALL_PACKED, _unpack_rows(d[0], shapes), _unpack_rows(m2[0], shapes),
                                _unpack_rows(v2[0], shapes)):
        grads[name], deltas[name], new_m[name], new_v[name] = small_g[name], dd, mm, vv

    return (loss, grad_x, *[grads[n] for n in WEIGHT_NAMES], *[deltas[n] for n in WEIGHT_NAMES],
            *[new_m[n] for n in WEIGHT_NAMES], *[new_v[n] for n in WEIGHT_NAMES])
```

```python
import math

import jax
import jax.numpy as jnp
from jax import lax
from jax.experimental import pallas as pl
from jax.experimental.pallas import tpu as pltpu

F32 = jnp.float32
BF16 = jnp.bfloat16

D_MODEL = 1024
N_DEV = 8
SGU_WIDTH = 384
CONV_WIDTH = 384
POOL_WIDTH = 256
HEAD_DIM = 64
SGU_HEADS = 6
CHUNK = 128
CONV_K = 31
POOL_WINDOWS = (2, 4, 8, 16)
IN_WIDTH = 1792
D_FF = 2816
FF_SHARD = 2 * D_FF // N_DEV
FF_PAIRS = N_DEV // 2
FFN_CONV_K = 3
EPS = 1e-6
GELU_C0 = math.sqrt(2.0 / math.pi)
GELU_C1 = 0.044715

ADAM_LR = 0.001
ADAM_B1 = 0.9
ADAM_B2 = 0.999
ADAM_EPS = 1e-08
ADAM_WD = 0.01
ADAM_STEP = 10

VMEM_LIMIT_BYTES = 56 * 1024 * 1024
TILE = 256
MIX_TILE = TILE
FFN_TILE = TILE
FFN_HALO = 8 * (FFN_CONV_K - 1)
POOL_HALO = 8 * POOL_WINDOWS[-1]
WGRAD_TK = 2048
WGRAD_TK_FFN = 4096


def _cparams(n_axes):
    return pltpu.CompilerParams(dimension_semantics=("arbitrary",) * n_axes, vmem_limit_bytes=VMEM_LIMIT_BYTES)


def _whole(shape):
    nd = len(shape)
    return pl.BlockSpec(shape, lambda *_: (0,) * nd, pipeline_mode=pl.Buffered(1))


def _dot(a, b):
    return jnp.dot(a, b, preferred_element_type=F32)


def _dot_nt(a, b):
    return lax.dot_general(a, b, (((1,), (1,)), ((), ())), preferred_element_type=F32)


def _dot_tn(a, b):
    return lax.dot_general(a, b, (((0,), (0,)), ((), ())), preferred_element_type=F32)


def _gelu(x):
    t = jnp.tanh(GELU_C0 * (x + GELU_C1 * x * x * x))
    return 0.5 * x * (1.0 + t), t


def _gelu_grad(x, t):
    return 0.5 * (1.0 + t) + 0.5 * x * (1.0 - t * t) * (GELU_C0 * (1.0 + 3.0 * GELU_C1 * x * x))


def _rowmean(x):
    return jnp.mean(x, axis=-1, keepdims=True)


def _colsum(x):
    return jnp.sum(x, axis=0, keepdims=True)


def _rms_fwd(x):
    r = lax.rsqrt(_rowmean(x * x) + EPS)
    return x * r, r


def _rms_bwd(dxhat, xhat, r):
    return r * (dxhat - xhat * _rowmean(dxhat * xhat))


N_PEERS = N_DEV - 1
ANY = pl.BlockSpec(memory_space=pl.ANY)
VMEM = pl.BlockSpec(memory_space=pltpu.VMEM)


def _my_pos():
    return lax.axis_index("x"), lax.axis_index("y"), lax.axis_index("c")


def _peer(pos, k):
    x, y, c = pos
    return (1 - x if k & 4 else x, 1 - y if k & 2 else y, 1 - c if k & 1 else c)


def _flat(pos):
    return 4 * pos[0] + 2 * pos[1] + pos[2]


def _remote_copy(src, dst, send_sem, recv_sem, peer):
    return pltpu.make_async_remote_copy(src_ref=src, dst_ref=dst, send_sem=send_sem, recv_sem=recv_sem,
                                        device_id=peer, device_id_type=pl.DeviceIdType.MESH)


N_CHIPS = N_DEV // 2
SIBLING = 1
SAME_CORE_PEERS = (2, 4, 6)


def _exchange_out_shapes(job):
    def shape(kind, a):
        if kind in ("gather", "gather2"):
            return (N_DEV,) + a.shape
        if kind == "scatter_p1":
            return (N_CHIPS,) + a.shape[1:]
        return a.shape
    return [jax.ShapeDtypeStruct(shape(kind, a), a.dtype) for kind, a in job]


def _exchange_sems(n):
    return [pltpu.SemaphoreType.DMA((n, N_PEERS)), pltpu.SemaphoreType.DMA((n, N_PEERS)), pltpu.SemaphoreType.DMA((n,))]


def _exchange_copies(kinds, src_refs, dst_refs, send_sems, recv_sems, local_sems, phase):
    pos = _my_pos()
    me = _flat(pos)
    chip, core = 2 * pos[0] + pos[1], pos[2]
    copies = []

    def remote(a, src, dst, k, sem=None):
        sem = k - 1 if sem is None else sem
        copies.append(_remote_copy(src, dst, send_sems.at[a, sem], recv_sems.at[a, sem], _peer(pos, k)))

    for a, kind in enumerate(kinds):
        src, dst = src_refs[a], dst_refs[a]
        if phase == 1:
            if kind == "gather2":
                for k in SAME_CORE_PEERS:
                    remote(a, dst.at[me ^ k], dst.at[me ^ k], SIBLING, sem=k)
        elif kind in ("gather", "gather2"):
            copies.append(pltpu.make_async_copy(src, dst.at[me], local_sems.at[a]))
            for k in (range(1, N_DEV) if kind == "gather" else (SIBLING,) + SAME_CORE_PEERS):
                remote(a, src, dst.at[me], k)
        elif kind == "scatter":
            copies.append(pltpu.make_async_copy(src.at[me], dst.at[me], local_sems.at[a]))
            for k in range(1, N_DEV):
                remote(a, src.at[me ^ k], dst.at[me], k)
        elif kind == "scatter_p1":
            for q in range(N_CHIPS):
                remote(a, src.at[2 * q + 1 - core], dst.at[q], SIBLING, sem=q)
        elif kind == "scatter_p2":
            copies.append(pltpu.make_async_copy(src.at[chip], dst.at[chip], local_sems.at[a]))
            for k in SAME_CORE_PEERS:
                remote(a, src.at[chip ^ (k >> 1)], dst.at[chip], k)
    return copies


def _pallas_call_with_exchange(body, *, grid, in_specs, out_specs, out_shape, scratch_shapes, operands, name, job):
    params = _cparams(len(grid))
    if not job:
        outs = pl.pallas_call(body, grid=grid, in_specs=in_specs, out_specs=out_specs, out_shape=out_shape,
                              scratch_shapes=scratch_shapes, compiler_params=params, name=name)(*operands)
        return outs, []
    kinds = [kind for kind, _ in job]
    relayed = [kind if kind == "gather2" else None for kind in kinds]
    unrelayed = [None if kind == "gather2" else kind for kind in kinds]
    n, n_in, n_out, n_scr = len(job), len(in_specs), len(out_specs), len(scratch_shapes)
    n_steps = math.prod(grid)
    relay_step = max(n_steps - 2, 0)

    def wrapped(*refs):
        ins, jin = refs[:n_in], refs[n_in:n_in + n]
        outs, jout = refs[n_in + n:n_in + n + n_out], refs[n_in + n + n_out:n_in + 2 * n + n_out]
        scr = refs[n_in + 2 * n + n_out:n_in + 2 * n + n_out + n_scr]
        sems = refs[n_in + 2 * n + n_out + n_scr:]
        step = pl.program_id(0)
        for d in range(1, len(grid)):
            step = step * grid[d] + pl.program_id(d)

        def copies(which, phase):
            return _exchange_copies(which, jin, jout, *sems, phase=phase)

        @pl.when(step == 0)
        def _():
            for cp in copies(kinds, 0):
                cp.start()

        body(*ins, *outs, *scr)

        @pl.when(step == relay_step)
        def _():
            for cp in copies(relayed, 0):
                cp.wait()
            for cp in copies(relayed, 1):
                cp.start()

        @pl.when(step == n_steps - 1)
        def _():
            for cp in copies(unrelayed, 0) + copies(relayed, 1):
                cp.wait()

    res = pl.pallas_call(
        wrapped, grid=grid,
        in_specs=list(in_specs) + [ANY] * n,
        out_specs=list(out_specs) + [ANY] * n,
        out_shape=list(out_shape) + _exchange_out_shapes(job),
        scratch_shapes=list(scratch_shapes) + _exchange_sems(n),
        compiler_params=params, name=name,
    )(*operands, *[a for _, a in job])
    return res[:n_out], res[n_out:]


def _seg_mean(x, segp):
    hi = x.astype(BF16)
    lo = (x - hi.astype(F32)).astype(BF16)
    return _dot(hi, segp) + _dot(lo, segp)


def _rot_rows(x, shift):
    m, c = x.shape
    return pltpu.roll(x.reshape(m // 8, 8, c), shift, 1).reshape(m, c)


def _sublane_is(shape, s):
    return lax.broadcasted_iota(jnp.int32, shape, 0) % 8 == s


def _causal_tail(tail, prev_rot):
    rot = _rot_rows(tail, 1)
    return jnp.where(_sublane_is(tail.shape, 0), prev_rot, rot), rot


def _lookahead_head(head, next_rot):
    rot = _rot_rows(head, 7)
    return jnp.where(_sublane_is(head.shape, 7), next_rot, rot), rot


def _tile_token_index(t, tile_idx):
    r = lax.broadcasted_iota(jnp.int32, (t, 1), 0)
    return tile_idx * t + (r % 8) * (t // 8) + r // 8


def _interleave(x):
    t, c = x.shape
    return jnp.swapaxes(x.reshape(8, t // 8, c), 0, 1).reshape(t, c)


def _deinterleave(x):
    t, c = x.shape
    return jnp.swapaxes(x.reshape(t // 8, 8, c), 0, 1).reshape(t, c)


def _ffn_fwd(x1, modv, g1024, wup, wdn, cw, cb, name="ffn_fwd", job=None, loss_target=None):
    s_len = x1.shape[0]
    t = FFN_TILE
    n_tiles = s_len // t
    with_loss = loss_target is not None

    def body(x1_ref, *rest):
        if with_loss:
            tgt_ref, rest = rest[0], rest[1:]
            loss_ref, rest = rest[10], rest[:10] + rest[11:]
        mod_ref, g_ref, wup_ref, wdn_ref, cw_ref, cb_ref, x2_ref, y2_ref, p_ref, u_ref, ext_ref, carry_ref = rest
        i = pl.program_id(0)

        @pl.when(i == 0)
        def _():
            carry_ref[...] = jnp.zeros_like(carry_ref)
            if with_loss:
                loss_ref[...] = jnp.zeros_like(loss_ref)

        x1v = x1_ref[...]
        pre_g, post_g = g_ref[3:4, :], g_ref[4:5, :]
        sh2, sc2, g2 = mod_ref[3:4, :], mod_ref[4:5, :], mod_ref[5:6, :]
        xhat, _ = _rms_fwd(x1v)
        h2b = (xhat * pre_g * (1.0 + sc2) + sh2).astype(BF16)

        def conv_shard(s):
            p = _dot_nt(h2b, wup_ref[s])
            p_ref[s] = p.astype(BF16)
            ext_ref[0:FFN_HALO, :], carry_ref[s] = _causal_tail(p[t - FFN_HALO:t, :], carry_ref[s])
            ext_ref[FFN_HALO:FFN_HALO + t, :] = p
            w = cw_ref[s]
            u = w[0:1, :] * ext_ref[0:t, :] + w[1:2, :] * ext_ref[8:8 + t, :] + w[2:3, :] * p + cb_ref[s]
            u_ref[s] = u.astype(BF16)
            return u

        y2 = jnp.zeros((t, D_MODEL), F32)
        for j in range(FF_PAIRS):
            ug = conv_shard(j)
            uv = conv_shard(j + FF_PAIRS)
            ge, _ = _gelu(ug)
            y2 = y2 + _dot((ge * uv).astype(BF16), wdn_ref[j])
        y2_ref[...] = y2
        yhat, _ = _rms_fwd(y2)
        x2 = x1v + g2 * (yhat * post_g)
        if with_loss:
            diff = x2 - _interleave(tgt_ref[...])
            x2_ref[...] = diff * (1.0 / D_MODEL)
            loss_ref[...] += (0.5 / D_MODEL) * jnp.sum(diff * diff)
        else:
            x2_ref[...] = x2

    tile = pl.BlockSpec((t, D_MODEL), lambda i: (i, 0))
    consts = (modv, g1024, wup, wdn, cw, cb)
    shards = pl.BlockSpec((N_DEV, t, FF_SHARD), lambda i: (0, i, 0))
    out_specs = [tile, tile, shards, shards]
    out_shape = [jax.ShapeDtypeStruct((s_len, D_MODEL), F32), jax.ShapeDtypeStruct((s_len, D_MODEL), F32),
                 jax.ShapeDtypeStruct((N_DEV, s_len, FF_SHARD), BF16),
                 jax.ShapeDtypeStruct((N_DEV, s_len, FF_SHARD), BF16)]
    if with_loss:
        out_specs.append(pl.BlockSpec((8, 128), lambda i: (0, 0)))
        out_shape.append(jax.ShapeDtypeStruct((8, 128), F32))
    return _pallas_call_with_exchange(
        body,
        grid=(n_tiles,),
        in_specs=[tile] * (2 if with_loss else 1) + [_whole(c.shape) for c in consts],
        out_specs=out_specs,
        out_shape=out_shape,
        scratch_shapes=[pltpu.VMEM((FFN_HALO + t, FF_SHARD), F32), pltpu.VMEM((N_DEV, FFN_HALO, FF_SHARD), F32)],
        operands=(x1,) + ((loss_target,) if with_loss else ()) + consts,
        name=name, job=job)


def _ffn_bwd(dx2, x1, y2, p, u, modv, g1024, wup, wdn, cw, name="ffn_bwd", job=None):
    s_len = x1.shape[0]
    t = FFN_TILE
    n_tiles = s_len // t
    hb = FFN_HALO

    def body(dx2_ref, x1_ref, y2_ref, p_ref, ph_ref, u_ref, mod_ref, g_ref, wup_ref, wdn_ref, cw_ref,
             dx1_ref, dp_ref, a_ref, dy2_ref, h2_ref, vec_ref, cgrad_ref, ext_ref, dext_ref, dcarry_ref):
        i = pl.program_id(0)
        tile_idx = n_tiles - 1 - i

        @pl.when(i == 0)
        def _():
            vec_ref[...] = jnp.zeros_like(vec_ref)
            cgrad_ref[...] = jnp.zeros_like(cgrad_ref)
            dcarry_ref[...] = jnp.zeros_like(dcarry_ref)

        dx2v, x1v, y2v = dx2_ref[...], x1_ref[...], y2_ref[...]
        pre_g, post_g = g_ref[3:4, :], g_ref[4:5, :]
        sh2, sc2, g2 = mod_ref[3:4, :], mod_ref[4:5, :], mod_ref[5:6, :]

        yhat, ry = _rms_fwd(y2v)
        vec_ref[1:2, :] += _colsum(dx2v * (yhat * post_g))
        dyn = dx2v * g2
        vec_ref[0:1, :] += _colsum(dyn * yhat)
        dy2b = _rms_bwd(dyn * post_g, yhat, ry).astype(BF16)
        dy2_ref[...] = dy2b

        xhat, rx = _rms_fwd(x1v)
        xn = xhat * pre_g
        h2_ref[...] = (xn * (1.0 + sc2) + sh2).astype(BF16)

        not_first = (tile_idx > 0).astype(F32)

        def recompute(s, slot):
            pf = p_ref[s].astype(F32)
            prev_rot = _rot_rows(ph_ref[s].astype(F32), 1) * not_first
            ext_ref[slot, 0:hb, :], _ = _causal_tail(pf[t - hb:t, :], prev_rot)
            ext_ref[slot, hb:hb + t, :] = pf
            return u_ref[s].astype(F32)

        def conv_bwd(s, slot, du):
            w = cw_ref[s]
            cgrad_ref[s, 0:1, :] += _colsum(du * ext_ref[slot, 0:t, :])
            cgrad_ref[s, 1:2, :] += _colsum(du * ext_ref[slot, 8:8 + t, :])
            cgrad_ref[s, 2:3, :] += _colsum(du * ext_ref[slot, 16:16 + t, :])
            cgrad_ref[s, 3:4, :] += _colsum(du)
            dext_ref[0:t, :] = du
            dext_ref[t:t + hb, :], dcarry_ref[s] = _lookahead_head(du[0:hb, :], dcarry_ref[s])
            dp = w[2:3, :] * du + w[1:2, :] * dext_ref[8:8 + t, :] + w[0:1, :] * dext_ref[16:16 + t, :]
            dpb = dp.astype(BF16)
            dp_ref[s] = dpb
            return _dot(dpb, wup_ref[s])

        dh2 = jnp.zeros((t, D_MODEL), F32)
        for j in range(FF_PAIRS):
            ug = recompute(j, 0)
            uv = recompute(j + FF_PAIRS, 1)
            ge, th = _gelu(ug)
            a_ref[j] = (ge * uv).astype(BF16)
            da = _dot_nt(dy2b, wdn_ref[j])
            dh2 = dh2 + conv_bwd(j, 0, da * uv * _gelu_grad(ug, th))
            dh2 = dh2 + conv_bwd(j + FF_PAIRS, 1, da * ge)

        vec_ref[2:3, :] += _colsum(dh2)
        vec_ref[3:4, :] += _colsum(dh2 * xn)
        dxn = dh2 * (1.0 + sc2)
        vec_ref[4:5, :] += _colsum(dxn * xhat)
        dx1_ref[...] = dx2v + _rms_bwd(dxn * pre_g, xhat, rx)

    rev = lambda i: (n_tiles - 1 - i, 0)
    tile = pl.BlockSpec((t, D_MODEL), rev)
    halo_idx = lambda i: (0, jnp.maximum((n_tiles - 1 - i) * (t // hb) - 1, 0), 0)
    return _pallas_call_with_exchange(
        body,
        grid=(n_tiles,),
        in_specs=[tile, tile, tile,
                  pl.BlockSpec((N_DEV, t, FF_SHARD), lambda i: (0, n_tiles - 1 - i, 0)),
                  pl.BlockSpec((N_DEV, hb, FF_SHARD), halo_idx),
                  pl.BlockSpec((N_DEV, t, FF_SHARD), lambda i: (0, n_tiles - 1 - i, 0)),
                  _whole(modv.shape), _whole(g1024.shape), _whole(wup.shape), _whole(wdn.shape),
                  _whole(cw.shape)],
        out_specs=[tile,
                   pl.BlockSpec((N_DEV, t, FF_SHARD), lambda i: (0, n_tiles - 1 - i, 0)),
                   pl.BlockSpec((FF_PAIRS, t, FF_SHARD), lambda i: (0, n_tiles - 1 - i, 0)),
                   tile, tile,
                   pl.BlockSpec((8, D_MODEL), lambda i: (0, 0)),
                   pl.BlockSpec((N_DEV, 8, FF_SHARD), lambda i: (0, 0, 0))],
        out_shape=[jax.ShapeDtypeStruct((s_len, D_MODEL), F32),
                   jax.ShapeDtypeStruct((N_DEV, s_len, FF_SHARD), BF16),
                   jax.ShapeDtypeStruct((FF_PAIRS, s_len, FF_SHARD), BF16),
                   jax.ShapeDtypeStruct((s_len, D_MODEL), BF16),
                   jax.ShapeDtypeStruct((s_len, D_MODEL), BF16),
                   jax.ShapeDtypeStruct((8, D_MODEL), F32),
                   jax.ShapeDtypeStruct((N_DEV, 8, FF_SHARD), F32)],
        scratch_shapes=[pltpu.VMEM((2, hb + t, FF_SHARD), F32), pltpu.VMEM((t + hb, FF_SHARD), F32),
                        pltpu.VMEM((N_DEV, hb, FF_SHARD), F32)],
        operands=(dx2, x1, y2, p, p, u, modv, g1024, wup, wdn, cw),
        name=name, job=job)


def _wgrad(a, b, name, tk=WGRAD_TK, job=None):
    a_grouped, b_grouped = a.ndim == 3, b.ndim == 3
    groups = a.shape[0] if a_grouped else b.shape[0]
    s_len, m, n = a.shape[-2], a.shape[-1], b.shape[-1]
    tk = min(tk, s_len)
    n_k = s_len // tk

    def body(a_ref, b_ref, o_ref, acc_ref):
        k = pl.program_id(1)
        av = a_ref[0] if a_grouped else a_ref[...]
        bv = b_ref[0] if b_grouped else b_ref[...]
        part = _dot_tn(av, bv)
        if n_k == 1:
            o_ref[0] = part.astype(BF16)
            return

        @pl.when(k == 0)
        def _():
            acc_ref[...] = part

        @pl.when(jnp.logical_and(k > 0, k < n_k - 1))
        def _():
            acc_ref[...] += part

        @pl.when(k == n_k - 1)
        def _():
            o_ref[0] = (acc_ref[...] + part).astype(BF16)

    a_spec = pl.BlockSpec((1, tk, m), lambda g, k: (g, k, 0)) if a_grouped else pl.BlockSpec((tk, m), lambda g, k: (k, 0))
    b_spec = pl.BlockSpec((1, tk, n), lambda g, k: (g, k, 0)) if b_grouped else pl.BlockSpec((tk, n), lambda g, k: (k, 0))
    (out,), exchanged = _pallas_call_with_exchange(
        body,
        grid=(groups, n_k),
        in_specs=[a_spec, b_spec],
        out_specs=[pl.BlockSpec((1, m, n), lambda g, k: (g, 0, 0))],
        out_shape=[jax.ShapeDtypeStruct((groups, m, n), BF16)],
        scratch_shapes=[pltpu.VMEM((m, n), F32)],
        operands=(a, b),
        name=name, job=job)
    return (out, exchanged) if job else out


def _lane(shape):
    return lax.broadcasted_iota(jnp.int32, shape, 1)


def _by_pool_group(shape, vals):
    lane = _lane(shape)
    return jnp.where(lane < 64, vals[0], jnp.where(lane < 128, vals[1], jnp.where(lane < 192, vals[2], vals[3])))


def _pool_inv_counts(t, tile_idx):
    pos1 = _tile_token_index(t, tile_idx) + 1
    return [1.0 / jnp.minimum(pos1, w).astype(F32) for w in POOL_WINDOWS]


def _sgu_keep_mask(t):
    tok_r = _tile_token_index(t, 0)
    c = lax.broadcasted_iota(jnp.int32, (1, t), 1)
    tok_c = (c % 8) * (t // 8) + c // 8
    return jnp.logical_and(tok_r // CHUNK == tok_c // CHUNK, tok_r >= tok_c)


def _masked_sgu_w(sguw_ref):
    keep = _sgu_keep_mask(sguw_ref.shape[1])
    return [jnp.where(keep, sguw_ref[h], 0.0).astype(BF16) for h in range(SGU_HEADS)]


def _branches_fwd(z, tile_idx, p384_ref, cw_ref, wm, bmat_ref, pwbd_ref, psc_ref, segp_ref, g_ref, hext_ref, zext_ref,
                  h_prev_rot, z_prev_rot, conv_saved=None):
    t = z.shape[0]
    segp = segp_ref[...]
    r = {}
    u, _ = _gelu(z[:, 0:SGU_WIDTH])
    vraw, _ = _gelu(z[:, SGU_WIDTH:2 * SGU_WIDTH])
    xc = vraw - _seg_mean(vraw, segp)
    rstd_v = lax.rsqrt(_seg_mean(xc * xc, segp) + EPS)
    xh_v = xc * rstd_v
    vnb = (xh_v * p384_ref[0:1, :] + p384_ref[1:2, :]).astype(BF16)
    first_head = _lane((t, 128)) < HEAD_DIM
    f_pairs = []
    for pr in range(SGU_HEADS // 2):
        vp = vnb[:, pr * 128:(pr + 1) * 128]
        f_pairs.append(jnp.where(first_head, _dot(wm[2 * pr], vp), _dot(wm[2 * pr + 1], vp)))
    f = jnp.concatenate(f_pairs, axis=1) + bmat_ref[...]
    ya = u * f
    r.update(u=u, xh_v=xh_v, rstd_v=rstd_v, vnb=vnb, f=f)
    o_b = 2 * SGU_WIDTH
    a_in = z[:, o_b:o_b + CONV_WIDTH]
    sig_g = jax.nn.sigmoid(z[:, o_b + CONV_WIDTH:o_b + 2 * CONV_WIDTH])
    hh = a_in * sig_g
    hext_ref[0:t, :], r["h_rot"] = _causal_tail(hh, h_prev_rot)
    hext_ref[t:2 * t, :] = hh
    if conv_saved is None:
        conv = jnp.zeros((t, CONV_WIDTH), F32) + p384_ref[2:3, :]
        for k in range(CONV_K):
            conv = conv + cw_ref[k:k + 1, :] * hext_ref[pl.ds(t - 8 * (CONV_K - 1 - k), t), :]
        r["conv"] = conv
    else:
        conv = conv_saved
    cc = conv - _rowmean(conv)
    rstd_c = lax.rsqrt(_rowmean(cc * cc) + EPS)
    xh_c = cc * rstd_c
    cn = xh_c * p384_ref[3:4, :] + p384_ref[4:5, :]
    sig_c = jax.nn.sigmoid(cn)
    yb = cn * sig_c
    r.update(a_in=a_in, sig_g=sig_g, xh_c=xh_c, rstd_c=rstd_c, cn=cn, sig_c=sig_c)
    o_c = o_b + 2 * CONV_WIDTH
    zc = z[:, o_c:o_c + POOL_WIDTH]
    zext_ref[0:POOL_HALO, :], r["z_rot"] = _causal_tail(zc[t - POOL_HALO:t, :], z_prev_rot)
    zext_ref[POOL_HALO:POOL_HALO + t, :] = zc
    sums, acc = [], zc
    for j in range(1, POOL_WINDOWS[-1]):
        acc = acc + zext_ref[pl.ds(POOL_HALO - 8 * j, t), :]
        if j + 1 in POOL_WINDOWS:
            sums.append(acc)
    inv = _pool_inv_counts(t, tile_idx)
    dpool = _by_pool_group((t, POOL_WIDTH), [s * iv for s, iv in zip(sums, inv)]) - zc
    ycp = _dot(dpool.astype(BF16), pwbd_ref[...])
    yc = ycp * psc_ref[0:1, :]
    r.update(dpool=dpool, ycp=ycp)
    yha, ra = _rms_fwd(ya)
    yhb, rb = _rms_fwd(yb)
    yhc, rc = _rms_fwd(yc)
    bg = g_ref[2:3, :]
    ycat = jnp.concatenate([yha * bg[:, 0:384], yhb * bg[:, 384:768], yhc * bg[:, 768:1024]], axis=1)
    r.update(yha=yha, ra=ra, yhb=yhb, rb=rb, yhc=yhc, rc=rc, ycat=ycat)
    return r


def _mixer_fwd(x, modv, g1024, p384, cw, sguw, bmat, pwbd, psc, segp, win, wout, name="mixer_fwd", job=None,
               natural_x=False):
    s_len = x.shape[0]
    t = MIX_TILE

    def body(x_ref, mod_ref, g_ref, p384_ref, cw_ref, sguw_ref, bmat_ref, pwbd_ref, psc_ref, segp_ref, win_ref, wout_ref,
             x1_ref, z_ref, o_ref, conv_ref, hext_ref, zext_ref, hrot_ref, zrot_ref, wm_ref):
        i = pl.program_id(0)

        @pl.when(i == 0)
        def _():
            hrot_ref[...] = jnp.zeros_like(hrot_ref)
            zrot_ref[...] = jnp.zeros_like(zrot_ref)
            for h, wmh in enumerate(_masked_sgu_w(sguw_ref)):
                wm_ref[h] = wmh

        xv = _interleave(x_ref[...]) if natural_x else x_ref[...]
        sh1, sc1, g1 = mod_ref[0:1, :], mod_ref[1:2, :], mod_ref[2:3, :]
        xhat, _ = _rms_fwd(xv)
        h1 = xhat * g_ref[0:1, :] * (1.0 + sc1) + sh1
        z = _dot_nt(h1.astype(BF16), win_ref[...])
        z_ref[...] = z
        r = _branches_fwd(z, i, p384_ref, cw_ref, [wm_ref[h] for h in range(SGU_HEADS)], bmat_ref, pwbd_ref, psc_ref,
                          segp_ref, g_ref, hext_ref, zext_ref, hrot_ref[...], zrot_ref[...])
        hrot_ref[...] = r["h_rot"]
        zrot_ref[...] = r["z_rot"]
        conv_ref[...] = r["conv"]
        o = _dot(r["ycat"].astype(BF16), wout_ref[...])
        o_ref[...] = o
        ohat, _ = _rms_fwd(o)
        x1_ref[...] = xv + g1 * (ohat * g_ref[1:2, :])

    tile = pl.BlockSpec((t, D_MODEL), lambda i: (i, 0))
    consts = (modv, g1024, p384, cw, sguw, bmat, pwbd, psc, segp, win, wout)
    return _pallas_call_with_exchange(
        body,
        grid=(s_len // t,),
        in_specs=[tile] + [_whole(c.shape) for c in consts],
        out_specs=[tile, pl.BlockSpec((t, IN_WIDTH), lambda i: (i, 0)), tile,
                   pl.BlockSpec((t, CONV_WIDTH), lambda i: (i, 0))],
        out_shape=[jax.ShapeDtypeStruct((s_len, D_MODEL), F32), jax.ShapeDtypeStruct((s_len, IN_WIDTH), F32),
                   jax.ShapeDtypeStruct((s_len, D_MODEL), F32), jax.ShapeDtypeStruct((s_len, CONV_WIDTH), F32)],
        scratch_shapes=[pltpu.VMEM((2 * t, CONV_WIDTH), F32), pltpu.VMEM((POOL_HALO + t, POOL_WIDTH), F32),
                        pltpu.VMEM((t, CONV_WIDTH), F32), pltpu.VMEM((POOL_HALO, POOL_WIDTH), F32),
                        pltpu.VMEM((SGU_HEADS, t, t), BF16)],
        operands=(x, *consts),
        name=name, job=job)


def _mixer_bwd(dx1, x, o, z, conv, modv, g1024, p384, cw, sguw, bmat, pwbd, psc, segp, win, wout, name="mixer_bwd",
               job=None, natural_x=False):
    s_len = x.shape[0]
    t = MIX_TILE
    n_tiles = s_len // t

    def body(dx1_ref, x_ref, o_ref, z_ref, zh_ref, conv_ref, mod_ref, g_ref, p384_ref, cw_ref, sguw_ref, bmat_ref, pwbd_ref,
             psc_ref, segp_ref, win_ref, wout_ref,
             dx_ref, dz_ref, do_ref, ycat_ref, h1_ref, vec_ref, v384_ref, dcw_ref, dsguw_out_ref, dbmat_out_ref, dpw_ref,
             dpsc_ref, hext_ref, zext_ref, gext_ref, qext_ref, grot_ref, qrot_ref, wm_ref, dsguw_ref, dbmat_ref):
        i = pl.program_id(0)
        tile_idx = n_tiles - 1 - i

        @pl.when(i == 0)
        def _():
            for ref in (vec_ref, v384_ref, dcw_ref, dsguw_ref, dbmat_ref, dpw_ref, dpsc_ref, grot_ref, qrot_ref):
                ref[...] = jnp.zeros_like(ref)
            for h, wmh in enumerate(_masked_sgu_w(sguw_ref)):
                wm_ref[h] = wmh

        dx1v, ov, z = dx1_ref[...], o_ref[...], z_ref[...]
        xv = _interleave(x_ref[...]) if natural_x else x_ref[...]
        sh1, sc1, g1 = mod_ref[0:1, :], mod_ref[1:2, :], mod_ref[2:3, :]
        pre_g, post_g, bg = g_ref[0:1, :], g_ref[1:2, :], g_ref[2:3, :]
        segp = segp_ref[...]

        ohat, ro = _rms_fwd(ov)
        vec_ref[1:2, :] += _colsum(dx1v * (ohat * post_g))
        don = dx1v * g1
        vec_ref[0:1, :] += _colsum(don * ohat)
        dob = _rms_bwd(don * post_g, ohat, ro).astype(BF16)
        do_ref[...] = dob
        dycat = _dot_nt(dob, wout_ref[...])

        not_first = (tile_idx > 0).astype(F32)
        o_b = 2 * SGU_WIDTH
        o_c = o_b + 2 * CONV_WIDTH
        h_prev = zh_ref[:, o_b:o_b + CONV_WIDTH] * jax.nn.sigmoid(zh_ref[:, o_b + CONV_WIDTH:o_c])
        h_prev_rot = _rot_rows(h_prev, 1) * not_first
        z_prev_rot = _rot_rows(zh_ref[t - POOL_HALO:t, o_c:o_c + POOL_WIDTH], 1) * not_first
        wm = [wm_ref[h] for h in range(SGU_HEADS)]
        r = _branches_fwd(z, tile_idx, p384_ref, cw_ref, wm, bmat_ref, pwbd_ref, psc_ref, segp_ref, g_ref,
                          hext_ref, zext_ref, h_prev_rot, z_prev_rot, conv_saved=conv_ref[...])
        ycat_ref[...] = r["ycat"].astype(BF16)

        def branch_norm_bwd(dyn, yhat, rr, gain):
            return _colsum(dyn * yhat), _rms_bwd(dyn * gain, yhat, rr)

        dga, dya = branch_norm_bwd(dycat[:, 0:384], r["yha"], r["ra"], bg[:, 0:384])
        dgb, dyb = branch_norm_bwd(dycat[:, 384:768], r["yhb"], r["rb"], bg[:, 384:768])
        dgc, dyc = branch_norm_bwd(dycat[:, 768:1024], r["yhc"], r["rc"], bg[:, 768:1024])
        vec_ref[5:6, :] += jnp.concatenate([dga, dgb, dgc], axis=1)

        du_act = dya * r["f"]
        df = dya * r["u"]
        first_head = _lane((t, 128)) < HEAD_DIM
        dbmat_ref[...] += df
        dvn_pairs = []
        for pr in range(SGU_HEADS // 2):
            dfp = df[:, pr * 128:(pr + 1) * 128]
            df0 = jnp.where(first_head, dfp, 0.0).astype(BF16)
            df1 = jnp.where(first_head, 0.0, dfp).astype(BF16)
            vp = r["vnb"][:, pr * 128:(pr + 1) * 128]
            dvn_pairs.append(_dot_tn(wm[2 * pr], df0) + _dot_tn(wm[2 * pr + 1], df1))
            dsguw_ref[2 * pr] += _dot_nt(df0, vp)
            dsguw_ref[2 * pr + 1] += _dot_nt(df1, vp)
        dvn = jnp.concatenate(dvn_pairs, axis=1)
        v384_ref[0:1, :] += _colsum(dvn * r["xh_v"])
        v384_ref[1:2, :] += _colsum(dvn)
        dxh = dvn * p384_ref[0:1, :]
        dvraw = r["rstd_v"] * (dxh - _seg_mean(dxh, segp) - r["xh_v"] * _seg_mean(dxh * r["xh_v"], segp))
        zu, zv = z[:, 0:SGU_WIDTH], z[:, SGU_WIDTH:o_b]
        _, tu = _gelu(zu)
        _, tv = _gelu(zv)
        dz_u = du_act * _gelu_grad(zu, tu)
        dz_v = dvraw * _gelu_grad(zv, tv)

        cn, sig_c = r["cn"], r["sig_c"]
        dcn = dyb * (sig_c * (1.0 + cn * (1.0 - sig_c)))
        v384_ref[3:4, :] += _colsum(dcn * r["xh_c"])
        v384_ref[4:5, :] += _colsum(dcn)
        dxc = dcn * p384_ref[3:4, :]
        gconv = r["rstd_c"] * (dxc - _rowmean(dxc) - r["xh_c"] * _rowmean(dxc * r["xh_c"]))
        v384_ref[2:3, :] += _colsum(gconv)
        gext_ref[0:t, :] = gconv
        gext_ref[t:2 * t, :], grot_ref[...] = _lookahead_head(gconv, grot_ref[...])
        dhh = jnp.zeros((t, CONV_WIDTH), F32)
        for k in range(CONV_K):
            shift = CONV_K - 1 - k
            dcw_ref[k:k + 1, :] += _colsum(gconv * hext_ref[pl.ds(t - 8 * shift, t), :])
            dhh = dhh + cw_ref[k:k + 1, :] * gext_ref[pl.ds(8 * shift, t), :]
        sig_g = r["sig_g"]
        dz_a = dhh * sig_g
        dz_g = dhh * r["a_in"] * sig_g * (1.0 - sig_g)

        dpsc_ref[0:1, :] += _colsum(dyc * r["ycp"])
        dycp = (dyc * psc_ref[0:1, :]).astype(BF16)
        dpw_ref[...] += _dot_tn(r["dpool"].astype(BF16), dycp)
        ddp = _dot_nt(dycp, pwbd_ref[...])
        inv = _pool_inv_counts(t, tile_idx)
        q = ddp * _by_pool_group((t, POOL_WIDTH), inv)
        qext_ref[0:t, :] = q
        qext_ref[t:t + POOL_HALO, :], qrot_ref[...] = _lookahead_head(q[0:POOL_HALO, :], qrot_ref[...])
        sums, acc = [], q
        for j in range(1, POOL_WINDOWS[-1]):
            acc = acc + qext_ref[pl.ds(8 * j, t), :]
            if j + 1 in POOL_WINDOWS:
                sums.append(acc)
        dz_c = _by_pool_group((t, POOL_WIDTH), sums) - ddp

        dzb = jnp.concatenate([dz_u, dz_v, dz_a, dz_g, dz_c], axis=1).astype(BF16)
        dz_ref[...] = dzb
        dh1 = _dot(dzb, win_ref[...])

        xhat, rx = _rms_fwd(xv)
        xn = xhat * pre_g
        h1_ref[...] = (xn * (1.0 + sc1) + sh1).astype(BF16)
        vec_ref[2:3, :] += _colsum(dh1)
        vec_ref[3:4, :] += _colsum(dh1 * xn)
        dxn = dh1 * (1.0 + sc1)
        vec_ref[4:5, :] += _colsum(dxn * xhat)
        dx = dx1v + _rms_bwd(dxn * pre_g, xhat, rx)
        dx_ref[...] = _deinterleave(dx) if natural_x else dx

        @pl.when(i == n_tiles - 1)
        def _():
            keep = _sgu_keep_mask(t)
            for h in range(SGU_HEADS):
                rows_natural = _deinterleave(jnp.where(keep, dsguw_ref[h], 0.0))
                natural = _deinterleave(rows_natural.T).T
                dsguw_out_ref[h] = sum(natural[b * CHUNK:(b + 1) * CHUNK, b * CHUNK:(b + 1) * CHUNK]
                                       for b in range(t // CHUNK))
            dbmat = _deinterleave(float(HEAD_DIM) * _seg_mean(dbmat_ref[...], segp))
            dbmat_out_ref[...] = sum(dbmat[b * CHUNK:(b + 1) * CHUNK, :] for b in range(t // CHUNK))

    rev = lambda i: (n_tiles - 1 - i, 0)
    tile = pl.BlockSpec((t, D_MODEL), rev)
    ztile = pl.BlockSpec((t, IN_WIDTH), rev)
    zhalo = pl.BlockSpec((t, IN_WIDTH), lambda i: (jnp.maximum(n_tiles - 2 - i, 0), 0))
    consts = (modv, g1024, p384, cw, sguw, bmat, pwbd, psc, segp, win, wout)
    acc = lambda shape: pl.BlockSpec(shape, lambda i: (0,) * len(shape))
    acc_shapes = [(8, D_MODEL), (8, SGU_WIDTH), (32, CONV_WIDTH), (SGU_HEADS, CHUNK, CHUNK), (CHUNK, SGU_WIDTH),
                  (POOL_WIDTH, POOL_WIDTH), (8, POOL_WIDTH)]
    return _pallas_call_with_exchange(
        body,
        grid=(n_tiles,),
        in_specs=[tile, tile, tile, ztile, zhalo, pl.BlockSpec((t, CONV_WIDTH), rev)] + [_whole(c.shape) for c in consts],
        out_specs=[tile, ztile, tile, tile, tile] + [acc(s) for s in acc_shapes],
        out_shape=[jax.ShapeDtypeStruct((s_len, D_MODEL), F32), jax.ShapeDtypeStruct((s_len, IN_WIDTH), BF16),
                   jax.ShapeDtypeStruct((s_len, D_MODEL), BF16), jax.ShapeDtypeStruct((s_len, D_MODEL), BF16),
                   jax.ShapeDtypeStruct((s_len, D_MODEL), BF16)] + [jax.ShapeDtypeStruct(s, F32) for s in acc_shapes],
        scratch_shapes=[pltpu.VMEM((2 * t, CONV_WIDTH), F32), pltpu.VMEM((POOL_HALO + t, POOL_WIDTH), F32),
                        pltpu.VMEM((2 * t, CONV_WIDTH), F32), pltpu.VMEM((t + POOL_HALO, POOL_WIDTH), F32),
                        pltpu.VMEM((t, CONV_WIDTH), F32), pltpu.VMEM((POOL_HALO, POOL_WIDTH), F32),
                        pltpu.VMEM((SGU_HEADS, t, t), BF16), pltpu.VMEM((SGU_HEADS, t, t), F32),
                        pltpu.VMEM((t, SGU_WIDTH), F32)],
        operands=(dx1, x, o, z, z, conv, *consts),
        name=name, job=job)


MOD_SHARD = 6 * D_MODEL // N_DEV

ROW_DMOD = 0
ROW_G1024 = 8
ROW_V384 = 16
ROW_SGU_B = 24
ROW_POOL_SCALE = 25
ROW_CONV_W = 32
ROW_FFN_CONV = 64
ROW_POOL_W = 96
ROW_SGU_W = 112
ROWS_PER_LAYER = 208
N_LAYERS = 2


def _gather_weights(c8, mod_w, mod_b8, job):
    kinds = [kind for kind, _ in job]
    shards = [a for _, a in job]
    n = len(shards)

    def body(c_ref, modw_ref, modb_ref, *rest):
        shard_refs = rest[:n]
        sc_all_ref, modrows_ref = rest[n], rest[n + 1]
        full_refs = rest[n + 2:2 * n + 2]
        send_buf, mod_recv, w_send, w_recv, w_local, sc_send, sc_recv, mod_send, mod_recv_sem = rest[2 * n + 2:]
        pos = _my_pos()
        me = _flat(pos)
        peers = [_peer(pos, k) for k in range(1, N_DEV)]

        w_copies = _exchange_copies(kinds, shard_refs, full_refs, w_send, w_recv, w_local, phase=0)
        for cp in w_copies:
            cp.start()

        cv = c_ref[...]
        sc_all_ref[me] = cv * jax.nn.sigmoid(cv)
        sc_copies = [_remote_copy(sc_all_ref.at[me], sc_all_ref.at[me], sc_send.at[k], sc_recv.at[k], peers[k])
                     for k in range(N_PEERS)]
        for cp in sc_copies:
            cp.start()
        for cp in sc_copies:
            cp.wait()

        sc = jnp.concatenate([sc_all_ref[j, 0:1, :] for j in range(N_DEV)], axis=0)
        send_buf[...] = jnp.zeros_like(send_buf)
        for l in range(N_LAYERS):
            part = jnp.dot(sc, modw_ref[l], precision=lax.Precision.HIGHEST, preferred_element_type=F32)
            for j in range(N_DEV):
                send_buf[j, l:l + 1, :] = part[j:j + 1, :]
        mod_recv[me] = send_buf[me]
        mod_copies = [_remote_copy(send_buf.at[_flat(peers[k])], mod_recv.at[me], mod_send.at[k], mod_recv_sem.at[k],
                                   peers[k]) for k in range(N_PEERS)]
        for cp in mod_copies:
            cp.start()
        for cp in mod_copies:
            cp.wait()
        modrows_ref[...] = jnp.zeros_like(modrows_ref)
        for l in range(N_LAYERS):
            row = jnp.concatenate([mod_recv[j, l:l + 1, :] for j in range(N_DEV)], axis=1)
            modrows_ref[l:l + 1, :] = row + modb_ref[l:l + 1, :]

        for cp in w_copies:
            cp.wait()
        relays = _exchange_copies(kinds, shard_refs, full_refs, w_send, w_recv, w_local, phase=1)
        for cp in relays:
            cp.start()
        for cp in relays:
            cp.wait()

    out_shape = ([jax.ShapeDtypeStruct((N_DEV, 8, D_MODEL), F32), jax.ShapeDtypeStruct((8, 6 * D_MODEL), F32)]
                 + [jax.ShapeDtypeStruct((N_DEV,) + s.shape, s.dtype) for s in shards])
    return pl.pallas_call(
        body,
        in_specs=[VMEM, VMEM, VMEM] + [ANY] * n,
        out_specs=[VMEM, VMEM] + [ANY] * n,
        out_shape=out_shape,
        scratch_shapes=[pltpu.VMEM((N_DEV, 8, MOD_SHARD), F32), pltpu.VMEM((N_DEV, 8, MOD_SHARD), F32),
                        pltpu.SemaphoreType.DMA((n, N_PEERS)), pltpu.SemaphoreType.DMA((n, N_PEERS)),
                        pltpu.SemaphoreType.DMA((n,)),
                        pltpu.SemaphoreType.DMA((N_PEERS,)), pltpu.SemaphoreType.DMA((N_PEERS,)),
                        pltpu.SemaphoreType.DMA((N_PEERS,)), pltpu.SemaphoreType.DMA((N_PEERS,))],
        compiler_params=pltpu.CompilerParams(vmem_limit_bytes=VMEM_LIMIT_BYTES),
        name="gather_weights",
    )(c8, mod_w, mod_b8, *shards)


def _small_sums(sc_all, small_all0, small_all1, job):
    kinds = [kind for kind, _ in job]
    n = len(job)

    def body(sc_all_ref, small_all0_ref, small_all1_ref, *rest):
        small_sum_ref, gmodw_ref = rest[n], rest[n + 1]
        copies = _exchange_copies(kinds, rest[:n], rest[n + 2:2 * n + 2], *rest[2 * n + 2:], phase=0)
        for cp in copies:
            cp.start()
        me = _flat(_my_pos())
        sc = jnp.concatenate([sc_all_ref[j, 0:1, :] for j in range(N_DEV)], axis=0)
        mine = lax.broadcasted_iota(jnp.int32, (2 * N_DEV, MOD_SHARD), 0) == me
        for l, parts in enumerate((small_all0_ref, small_all1_ref)):
            total = parts[0].astype(F32)
            for j in range(1, N_DEV):
                total = total + parts[j].astype(F32)
            small_sum_ref[l] = total
            dm = jnp.concatenate(
                [jnp.sum(jnp.where(mine, parts[j, ROW_DMOD:ROW_DMOD + 2 * N_DEV, 0:MOD_SHARD].astype(F32), 0.0),
                         axis=0, keepdims=True) for j in range(N_DEV)], axis=0)
            gmodw_ref[l] = lax.dot_general(sc, dm, (((0,), (0,)), ((), ())), precision=lax.Precision.HIGHEST,
                                           preferred_element_type=F32)
        for cp in copies:
            cp.wait()

    res = pl.pallas_call(
        body,
        in_specs=[VMEM, VMEM, VMEM] + [ANY] * n,
        out_specs=[VMEM, VMEM] + [ANY] * n,
        out_shape=[jax.ShapeDtypeStruct((N_LAYERS, ROWS_PER_LAYER, D_MODEL), F32),
                   jax.ShapeDtypeStruct((N_LAYERS, D_MODEL, MOD_SHARD), F32)] + _exchange_out_shapes(job),
        scratch_shapes=_exchange_sems(n),
        compiler_params=pltpu.CompilerParams(vmem_limit_bytes=VMEM_LIMIT_BYTES),
        name="small_sums",
    )(sc_all, small_all0, small_all1, *[a for _, a in job])
    return res[0], res[1], res[2:]


def _adam_update(g, w, m, v):
    m2 = ADAM_B1 * m + (1.0 - ADAM_B1) * g
    v2 = ADAM_B2 * v + (1.0 - ADAM_B2) * (g * g)
    m_hat = m2 / (1.0 - ADAM_B1 ** ADAM_STEP)
    v_hat = v2 / (1.0 - ADAM_B2 ** ADAM_STEP)
    delta = -ADAM_LR * (m_hat / (jnp.sqrt(v_hat) + ADAM_EPS) + ADAM_WD * w)
    return delta, m2, v2


def _pair_add(g, r1, row_chunk, name):
    _, rows, cols = g.shape
    core = lax.axis_index("c").astype(jnp.int32).reshape(1)

    def body(core_ref, g_ref, r_ref, o_ref):
        o_ref[0] = (g_ref[0, 0].astype(F32) + r_ref[0].astype(F32)).astype(BF16)

    blk = pl.BlockSpec((1, row_chunk, cols), lambda q, i, core_ref: (q, i, 0))
    grid_spec = pltpu.PrefetchScalarGridSpec(
        num_scalar_prefetch=1, grid=(N_CHIPS, rows // row_chunk),
        in_specs=[pl.BlockSpec((1, 1, row_chunk, cols), lambda q, i, core_ref: (q, core_ref[0], i, 0)), blk],
        out_specs=blk)
    return pl.pallas_call(
        body, grid_spec=grid_spec, out_shape=jax.ShapeDtypeStruct((N_CHIPS, rows, cols), BF16),
        compiler_params=_cparams(2), name=name,
    )(core, g.reshape(N_CHIPS, 2, rows, cols), r1)


def _adam_sharded(recv0, recv1, w, m, v, row_chunk, name):
    _, rows, cols = w.shape
    n_chunks = rows // row_chunk

    def body(r0_ref, r1_ref, w_ref, m_ref, v_ref, g_ref, d_ref, m2_ref, v2_ref):
        layer = pl.program_id(0)

        def run(r_ref):
            g = r_ref[0].astype(F32)
            for j in range(1, r_ref.shape[0]):
                g = g + r_ref[j].astype(F32)
            delta, m2, v2 = _adam_update(g, w_ref[0], m_ref[0], v_ref[0])
            g_ref[0], d_ref[0], m2_ref[0], v2_ref[0] = g, delta, m2, v2

        @pl.when(layer == 0)
        def _():
            run(r0_ref)

        @pl.when(layer == 1)
        def _():
            run(r1_ref)

    r0_spec = pl.BlockSpec((recv0.shape[0], row_chunk, cols), lambda l, i: (0, i * (1 - l) + (n_chunks - 1) * l, 0))
    r1_spec = pl.BlockSpec((recv1.shape[0], row_chunk, cols), lambda l, i: (0, i * l, 0))
    blk = pl.BlockSpec((1, row_chunk, cols), lambda l, i: (l, i, 0))
    out = jax.ShapeDtypeStruct(w.shape, F32)
    return pl.pallas_call(
        body,
        grid=(N_LAYERS, n_chunks),
        in_specs=[r0_spec, r1_spec, blk, blk, blk],
        out_specs=[blk] * 4,
        out_shape=[out] * 4,
        compiler_params=_cparams(2),
        name=name,
    )(recv0, recv1, w, m, v)


def _adam_dense(g, w, m, v, row_chunk, name):
    n_lead, rows, cols = w.shape

    def body(g_ref, w_ref, m_ref, v_ref, go_ref, d_ref, m2_ref, v2_ref):
        gv = g_ref[...]
        go_ref[...] = gv
        d_ref[...], m2_ref[...], v2_ref[...] = _adam_update(gv, w_ref[...], m_ref[...], v_ref[...])

    blk = pl.BlockSpec((1, row_chunk, cols), lambda l, i: (l, i, 0))
    out = jax.ShapeDtypeStruct(w.shape, F32)
    return pl.pallas_call(
        body,
        grid=(n_lead, rows // row_chunk),
        in_specs=[blk] * 4,
        out_specs=[blk] * 4,
        out_shape=[out] * 4,
        compiler_params=_cparams(2),
        name=name,
    )(g, w, m, v)


WEIGHT_NAMES = ("mod_w", "mod_b", "mix_pre_g", "mix_post_g", "w_in", "sgu_norm_g", "sgu_norm_b", "sgu_w", "sgu_b",
                "conv_w", "conv_b", "conv_norm_g", "conv_norm_b", "pool_w", "pool_scale", "branch_g", "w_out",
                "ffn_pre_g", "ffn_post_g", "ffn_up", "ffn_conv_w", "ffn_conv_b", "ffn_down")
SHARDED_BIG = ("w_in", "w_out", "ffn_up", "ffn_down")
SMALL_PACKED = tuple(n for n in WEIGHT_NAMES if n not in SHARDED_BIG + ("mod_w",))


def _rows8(rows, width=D_MODEL):
    out = [jnp.pad(r.astype(F32), (0, width - r.shape[0]))[None] for r in rows]
    out.append(jnp.zeros((8 - len(rows), width), F32))
    return jnp.concatenate(out, axis=0)


def _as_rows(a, width=D_MODEL):
    flat = a.astype(F32).reshape(-1)
    pad = (-flat.shape[0]) % width
    return jnp.pad(flat, (0, pad)).reshape(-1, width)


def _pad_cols(a, width=D_MODEL):
    return jnp.pad(a.astype(F32), ((0, 0), (0, width - a.shape[1])))


def _pack_rows(arrays):
    rows = jnp.concatenate([_as_rows(a) for a in arrays], axis=0)
    return jnp.pad(rows, ((0, (-rows.shape[0]) % 8), (0, 0)))


def _unpack_rows(packed, shapes):
    out, r = [], 0
    for shape in shapes:
        size = math.prod(shape)
        n_rows = -(-size // D_MODEL)
        out.append(packed[r:r + n_rows].reshape(-1)[:size].reshape(shape))
        r += n_rows
    return out


TILE_VREGS = MIX_TILE // 8
CHUNK_SUBLANES = CHUNK // TILE_VREGS
CHUNKS_PER_TILE = MIX_TILE // CHUNK


def _chunk_axis_to_tile(a, axis):
    shape = a.shape
    a = a.reshape(shape[:axis] + (CHUNK_SUBLANES, TILE_VREGS) + shape[axis + 1:])
    a = jnp.swapaxes(a, axis, axis + 1)
    a = jnp.tile(a, (1,) * (axis + 1) + (CHUNKS_PER_TILE,) + (1,) * (len(shape) - axis - 1))
    return a.reshape(shape[:axis] + (MIX_TILE,) + shape[axis + 1:])


def _layer_consts(l, w, mod_rows, win, wout, conv_w_full):
    modv = _rows8(list(mod_rows[l].reshape(6, D_MODEL)))
    g1024 = _rows8([w["mix_pre_g"][l], w["mix_post_g"][l], w["branch_g"][l], w["ffn_pre_g"][l], w["ffn_post_g"][l]])
    p384 = _rows8([w["sgu_norm_g"][l], w["sgu_norm_b"][l], w["conv_b"][l], w["conv_norm_g"][l], w["conv_norm_b"][l]],
                  SGU_WIDTH)
    cw = jnp.pad(conv_w_full[l], ((0, 32 - CONV_K), (0, 0)))
    sguw = _chunk_axis_to_tile(_chunk_axis_to_tile(w["sgu_w"][l], 2), 1)
    bmat = _chunk_axis_to_tile(jnp.repeat(w["sgu_b"][l].T, HEAD_DIM, axis=1), 0)
    groups = len(POOL_WINDOWS)
    eye = jnp.eye(groups, dtype=F32)
    pwbd = (eye[:, None, :, None] * w["pool_w"][l][:, :, None, :]).reshape(POOL_WIDTH, POOL_WIDTH).astype(BF16)
    psc = _rows8([w["pool_scale"][l]], POOL_WIDTH)
    seg = jnp.arange(SGU_WIDTH) // HEAD_DIM
    segp = jnp.where(seg[:, None] == seg[None, :], 1.0 / HEAD_DIM, 0.0).astype(BF16)
    return modv, g1024, (modv, g1024, p384, cw, sguw, bmat, pwbd, psc, segp, win, wout)


def _small_grad_rows(mix, ffn):
    _, _, _, _, _, mvec, v384, dcw, dsguw, dbmat, dpw, dpsc = mix
    fvec, cgrad = ffn[5], ffn[6]
    dmod = jnp.stack([mvec[2], mvec[3], mvec[1], fvec[2], fvec[3], fvec[1]]).reshape(N_DEV, MOD_SHARD)
    g_rows = jnp.stack([mvec[4], mvec[0], mvec[5], fvec[4], fvec[0]])
    dsgu_b = dbmat[:, ::HEAD_DIM].T.reshape(1, SGU_HEADS * CHUNK)
    groups = len(POOL_WINDOWS)
    gdim = POOL_WIDTH // groups
    dpw4 = dpw.reshape(groups, gdim, groups, gdim)
    dpool = jnp.stack([dpw4[g, :, g, :] for g in range(groups)])
    blocks = [_pad_cols(dmod), _rows8(list(g_rows)), _pad_cols(v384), _rows8([dsgu_b[0], dpsc[0]]), _pad_cols(dcw),
              _pad_cols(cgrad[:, 0:4, :].reshape(4 * N_DEV, FF_SHARD)), _as_rows(dpool), _as_rows(dsguw)]
    return jnp.concatenate(blocks, axis=0)


def _small_grads_from_rows(total):
    per = {n: [] for n in SMALL_PACKED}
    for l in range(N_LAYERS):
        s = total[l]
        per["mod_b"].append(s[ROW_DMOD:ROW_DMOD + N_DEV, :MOD_SHARD].reshape(6 * D_MODEL))
        for j, name in enumerate(("mix_pre_g", "mix_post_g", "branch_g", "ffn_pre_g", "ffn_post_g")):
            per[name].append(s[ROW_G1024 + j])
        for j, name in enumerate(("sgu_norm_g", "sgu_norm_b", "conv_b", "conv_norm_g", "conv_norm_b")):
            per[name].append(s[ROW_V384 + j, :SGU_WIDTH])
        per["sgu_b"].append(s[ROW_SGU_B, :SGU_HEADS * CHUNK].reshape(SGU_HEADS, CHUNK))
        per["pool_scale"].append(s[ROW_POOL_SCALE, :POOL_WIDTH])
        per["conv_w"].append(s[ROW_CONV_W:ROW_CONV_W + CONV_K, :CONV_WIDTH])
        fc = s[ROW_FFN_CONV:ROW_FFN_CONV + 4 * N_DEV, :FF_SHARD].reshape(N_DEV, 4, FF_SHARD)
        per["ffn_conv_w"].append(fc[:, 0:3, :].transpose(1, 0, 2).reshape(FFN_CONV_K, 2 * D_FF))
        per["ffn_conv_b"].append(fc[:, 3, :].reshape(2 * D_FF))
        per["pool_w"].append(s[ROW_POOL_W:ROW_POOL_W + 16].reshape(len(POOL_WINDOWS), HEAD_DIM, HEAD_DIM))
        per["sgu_w"].append(s[ROW_SGU_W:ROW_SGU_W + 96].reshape(SGU_HEADS, CHUNK, CHUNK))
    return {n: jnp.stack(v) for n, v in per.items()}


def kernel(x, c, mod_w, mod_b, mix_pre_g, mix_post_g, w_in, sgu_norm_g, sgu_norm_b, sgu_w, sgu_b, conv_w, conv_b, conv_norm_g, conv_norm_b, pool_w, pool_scale, branch_g, w_out, ffn_pre_g, ffn_post_g, ffn_up, ffn_conv_w, ffn_conv_b, ffn_down, loss_target, m_mod_w, m_mod_b, m_mix_pre_g, m_mix_post_g, m_w_in, m_sgu_norm_g, m_sgu_norm_b, m_sgu_w, m_sgu_b, m_conv_w, m_conv_b, m_conv_norm_g, m_conv_norm_b, m_pool_w, m_pool_scale, m_branch_g, m_w_out, m_ffn_pre_g, m_ffn_post_g, m_ffn_up, m_ffn_conv_w, m_ffn_conv_b, m_ffn_down, v_mod_w, v_mod_b, v_mix_pre_g, v_mix_post_g, v_w_in, v_sgu_norm_g, v_sgu_norm_b, v_sgu_w, v_sgu_b, v_conv_w, v_conv_b, v_conv_norm_g, v_conv_norm_b, v_pool_w, v_pool_scale, v_branch_g, v_w_out, v_ffn_pre_g, v_ffn_post_g, v_ffn_up, v_ffn_conv_w, v_ffn_conv_b, v_ffn_down):
    w = dict(zip(WEIGHT_NAMES, (mod_w, mod_b, mix_pre_g, mix_post_g, w_in, sgu_norm_g, sgu_norm_b, sgu_w, sgu_b, conv_w,
                                conv_b, conv_norm_g, conv_norm_b, pool_w, pool_scale, branch_g, w_out, ffn_pre_g,
                                ffn_post_g, ffn_up, ffn_conv_w, ffn_conv_b, ffn_down)))
    m = dict(zip(WEIGHT_NAMES, (m_mod_w, m_mod_b, m_mix_pre_g, m_mix_post_g, m_w_in, m_sgu_norm_g, m_sgu_norm_b, m_sgu_w,
                                m_sgu_b, m_conv_w, m_conv_b, m_conv_norm_g, m_conv_norm_b, m_pool_w, m_pool_scale,
                                m_branch_g, m_w_out, m_ffn_pre_g, m_ffn_post_g, m_ffn_up, m_ffn_conv_w, m_ffn_conv_b,
                                m_ffn_down)))
    v = dict(zip(WEIGHT_NAMES, (v_mod_w, v_mod_b, v_mix_pre_g, v_mix_post_g, v_w_in, v_sgu_norm_g, v_sgu_norm_b, v_sgu_w,
                                v_sgu_b, v_conv_w, v_conv_b, v_conv_norm_g, v_conv_norm_b, v_pool_w, v_pool_scale,
                                v_branch_g, v_w_out, v_ffn_pre_g, v_ffn_post_g, v_ffn_up, v_ffn_conv_w, v_ffn_conv_b,
                                v_ffn_down)))
    me = _flat(_my_pos())
    xs = x[0]

    transposed = ("w_in", "ffn_up")
    wt = {n: jnp.swapaxes(w[n], 1, 2) if n in transposed else w[n] for n in SHARDED_BIG}
    mt = {n: jnp.swapaxes(m[n], 1, 2) if n in transposed else m[n] for n in SHARDED_BIG}
    vt = {n: jnp.swapaxes(v[n], 1, 2) if n in transposed else v[n] for n in SHARDED_BIG}
    bf16_shards = [[wt[n][l].astype(BF16) for n in SHARDED_BIG] for l in range(N_LAYERS)]

    def mixer_operands(l, win_g, wout_g):
        win = win_g.reshape(IN_WIDTH, D_MODEL)
        return _layer_consts(l, w, mod_rows, win, wout_g.reshape(D_MODEL, D_MODEL), conv_w_full)

    def ffn_operands(l, modv, g1024, wup_g, wdn_g):
        wdn = wdn_g.reshape(FF_PAIRS, FF_SHARD, D_MODEL)
        return modv, g1024, wup_g, wdn, ffn_cw_full[:, l], ffn_conv_b[l].reshape(N_DEV, 1, FF_SHARD)

    w0, w1 = bf16_shards
    c8 = jnp.broadcast_to(c, (8, D_MODEL))
    mod_b8 = jnp.pad(mod_b, ((0, 8 - N_LAYERS), (0, 0)))
    sc_all, mod_rows, win0_g, wout0_g, conv_w_g, ffn_cw_full = _gather_weights(
        c8, mod_w, mod_b8, [("gather2", w0[0]), ("gather2", w0[1]), ("gather", conv_w), ("gather", ffn_conv_w)])
    conv_w_full = conv_w_g.transpose(1, 2, 0, 3).reshape(N_LAYERS, CONV_K, CONV_WIDTH)

    modv0, g0, mix_consts0 = mixer_operands(0, win0_g, wout0_g)
    (x1, z, o, cv), (wup0_g, wdn0_g) = _mixer_fwd(xs, *mix_consts0, name="mixer_fwd_l0", natural_x=True,
                                                  job=[("gather2", w0[2]), ("gather2", w0[3])])
    ffn_consts0 = ffn_operands(0, modv0, g0, wup0_g, wdn0_g)
    (x2, y2, p, u), (win1_g, wout1_g, wdn1_g) = _ffn_fwd(
        x1, *ffn_consts0, name="ffn_fwd_l0", job=[("gather2", w1[0]), ("gather2", w1[1]), ("gather2", w1[3])])
    saved = [(xs, z, o, cv, x1, y2, p, u)]
    modv1, g1, mix_consts1 = mixer_operands(1, win1_g, wout1_g)
    (x1, z, o, cv), (wup1_g,) = _mixer_fwd(x2, *mix_consts1, name="mixer_fwd_l1", job=[("gather2", w1[2])])
    ffn_consts1 = ffn_operands(1, modv1, g1, wup1_g, wdn1_g)
    (dh, y2, p, u, loss_tile), _ = _ffn_fwd(x1, *ffn_consts1, name="ffn_fwd_l1", loss_target=loss_target[0])
    saved.append((x2, z, o, cv, x1, y2, p, u))
    loss = lax.psum(loss_tile[0, 0], ("x", "y", "c"))

    def ffn_weight_grads(l, ffn):
        dp, a, dy2, h2 = ffn[1:5]
        d_up = _wgrad(dp, h2, f"wgrad_ffn_up_l{l}", tk=WGRAD_TK_FFN)
        d_dn = _wgrad(a, dy2, f"wgrad_ffn_down_l{l}", tk=WGRAD_TK_FFN).reshape(N_DEV, D_FF // N_DEV, D_MODEL)
        return d_up, d_dn

    def mixer_weight_grads(l, mix):
        dz, do, ycat, h1 = mix[1:5]
        d_in = _wgrad(dz[None], h1, f"wgrad_w_in_l{l}").reshape(N_DEV, IN_WIDTH // N_DEV, D_MODEL)
        d_out = _wgrad(ycat, do[None], f"wgrad_w_out_l{l}").reshape(N_DEV, D_MODEL // N_DEV, D_MODEL)
        return d_in, d_out

    x_in, z, o, cv, x1, y2, p, u = saved[1]
    ffn1, _ = _ffn_bwd(dh, x1, y2, p, u, *ffn_consts1[:-1], name="ffn_bwd_l1")
    d_up1, d_dn1 = ffn_weight_grads(1, ffn1)
    mix1, (sib_up1, sib_dn1) = _mixer_bwd(ffn1[0], x_in, o, z, cv, *mix_consts1, name="mixer_bwd_l1",
                                          job=[("scatter_p1", d_up1), ("scatter_p1", d_dn1)])
    chip_up1 = _pair_add(d_up1, sib_up1, 176, "pair_add_ffn_up_l1")
    chip_dn1 = _pair_add(d_dn1, sib_dn1, 176, "pair_add_ffn_down_l1")
    d_in1, d_out1 = mixer_weight_grads(1, mix1)
    small1 = _small_grad_rows(mix1, ffn1).astype(BF16)

    x_in, z, o, cv, x1, y2, p, u = saved[0]
    ffn0, job_out = _ffn_bwd(mix1[0], x1, y2, p, u, *ffn_consts0[:-1], name="ffn_bwd_l0",
                             job=[("scatter", d_in1), ("scatter", d_out1), ("scatter_p2", chip_up1),
                                  ("scatter_p2", chip_dn1)])
    recv1 = job_out
    dp, a, dy2, h2 = ffn0[1:5]
    d_dn0 = _wgrad(a, dy2, "wgrad_ffn_down_l0", tk=WGRAD_TK_FFN).reshape(N_DEV, D_FF // N_DEV, D_MODEL)
    d_up0, (recv_dn0,) = _wgrad(dp, h2, "wgrad_ffn_up_l0", tk=WGRAD_TK_FFN, job=[("scatter", d_dn0)])
    mix0, (recv_up0, small_all1) = _mixer_bwd(ffn0[0], x_in, o, z, cv, *mix_consts0, name="mixer_bwd_l0",
                                              natural_x=True, job=[("scatter", d_up0), ("gather", small1)])
    recv_ffn0 = (recv_up0, recv_dn0)
    grad_x = mix0[0][None]
    dz, do, ycat, h1 = mix0[1:5]
    small0 = _small_grad_rows(mix0, ffn0).astype(BF16)
    d_out0, (small_all0,) = _wgrad(ycat, do[None], "wgrad_w_out_l0", job=[("gather", small0)])
    d_out0 = d_out0.reshape(N_DEV, D_MODEL // N_DEV, D_MODEL)
    d_in0, (recv_out0,) = _wgrad(dz[None], h1, "wgrad_w_in_l0", job=[("scatter", d_out0)])
    d_in0 = d_in0.reshape(N_DEV, IN_WIDTH // N_DEV, D_MODEL)
    small_total, g_mod_w, (recv_in0,) = _small_sums(sc_all, small_all0, small_all1, [("scatter", d_in0)])
    recv0 = [recv_in0, recv_out0, recv_ffn0[0], recv_ffn0[1]]

    grads, deltas, new_m, new_v = {}, {}, {}, {}
    for j, (name, chunk) in enumerate((("w_in", 224), ("w_out", 128), ("ffn_up", 176), ("ffn_down", 176))):
        outs = _adam_sharded(recv0[j], recv1[j], wt[name], mt[name], vt[name], chunk, "adam_" + name)
        if name in transposed:
            outs = [jnp.swapaxes(t, 1, 2) for t in outs]
        grads[name], deltas[name], new_m[name], new_v[name] = outs
    grads["mod_w"], deltas["mod_w"], new_m["mod_w"], new_v["mod_w"] = _adam_dense(
        g_mod_w, mod_w, m_mod_w, v_mod_w, 256, "adam_mod_w")

    small_g = _small_grads_from_rows(small_total)
    small_g["conv_w"] = lax.dynamic_slice_in_dim(small_g["conv_w"], me * conv_w.shape[2], conv_w.shape[2], axis=2)
    small_g["ffn_conv_w"] = lax.dynamic_slice_in_dim(small_g["ffn_conv_w"], me * FF_SHARD, FF_SHARD, axis=2)
    shapes = [w[n].shape for n in SMALL_PACKED]
    packs = [_pack_rows([src[n] for n in SMALL_PACKED])[None] for src in (small_g, w, m, v)]
    _, d, m2, v2 = _adam_dense(*packs, packs[0].shape[1], "adam_small")
    for name, dd, mm, vv in zip(SMALL_PACKED, _unpack_rows(d[0], shapes), _unpack_rows(m2[0], shapes),
                                _unpack_rows(v2[0], shapes)):
        grads[name], deltas[name], new_m[name], new_v[name] = small_g[name], dd, mm, vv

    return (loss, grad_x, *[grads[n] for n in WEIGHT_NAMES], *[deltas[n] for n in WEIGHT_NAMES],
            *[new_m[n] for n in WEIGHT_NAMES], *[new_v[n] for n in WEIGHT_NAMES])
```

```python
import math

import jax
import jax.numpy as jnp
from jax import lax
from jax.experimental import pallas as pl
from jax.experimental.pallas import tpu as pltpu

F32 = jnp.float32
BF16 = jnp.bfloat16

D_MODEL = 1024
N_DEV = 8
SGU_WIDTH = 384
CONV_WIDTH = 384
POOL_WIDTH = 256
HEAD_DIM = 64
SGU_HEADS = 6
CHUNK = 128
CONV_K = 31
POOL_WINDOWS = (2, 4, 8, 16)
IN_WIDTH = 1792
D_FF = 2816
FF_SHARD = 2 * D_FF // N_DEV
FF_PAIRS = N_DEV // 2
FFN_CONV_K = 3
EPS = 1e-6
GELU_C0 = math.sqrt(2.0 / math.pi)
GELU_C1 = 0.044715

ADAM_LR = 0.001
ADAM_B1 = 0.9
ADAM_B2 = 0.999
ADAM_EPS = 1e-08
ADAM_WD = 0.01
ADAM_STEP = 10

VMEM_LIMIT_BYTES = 56 * 1024 * 1024
TILE = 256
MIX_TILE = TILE
FFN_TILE = TILE
FFN_HALO = 8 * (FFN_CONV_K - 1)
POOL_HALO = 8 * POOL_WINDOWS[-1]
WGRAD_TK = 2048
WGRAD_TK_FFN = 4096


def _cparams(n_axes):
    return pltpu.CompilerParams(dimension_semantics=("arbitrary",) * n_axes, vmem_limit_bytes=VMEM_LIMIT_BYTES)


def _whole(shape):
    nd = len(shape)
    return pl.BlockSpec(shape, lambda *_: (0,) * nd, pipeline_mode=pl.Buffered(1))


def _dot(a, b):
    return jnp.dot(a, b, preferred_element_type=F32)


def _dot_nt(a, b):
    return lax.dot_general(a, b, (((1,), (1,)), ((), ())), preferred_element_type=F32)


def _dot_tn(a, b):
    return lax.dot_general(a, b, (((0,), (0,)), ((), ())), preferred_element_type=F32)


def _gelu(x):
    t = jnp.tanh(GELU_C0 * (x + GELU_C1 * x * x * x))
    return 0.5 * x * (1.0 + t), t


def _gelu_grad(x, t):
    return 0.5 * (1.0 + t) + 0.5 * x * (1.0 - t * t) * (GELU_C0 * (1.0 + 3.0 * GELU_C1 * x * x))


def _rowmean(x):
    return jnp.mean(x, axis=-1, keepdims=True)


def _colsum(x):
    return jnp.sum(x, axis=0, keepdims=True)


def _rms_fwd(x):
    r = lax.rsqrt(_rowmean(x * x) + EPS)
    return x * r, r


def _rms_bwd(dxhat, xhat, r):
    return r * (dxhat - xhat * _rowmean(dxhat * xhat))


N_PEERS = N_DEV - 1
ANY = pl.BlockSpec(memory_space=pl.ANY)
VMEM = pl.BlockSpec(memory_space=pltpu.VMEM)


def _my_pos():
    return lax.axis_index("x"), lax.axis_index("y"), lax.axis_index("c")


def _peer(pos, k):
    x, y, c = pos
    return (1 - x if k & 4 else x, 1 - y if k & 2 else y, 1 - c if k & 1 else c)


def _flat(pos):
    return 4 * pos[0] + 2 * pos[1] + pos[2]


def _remote_copy(src, dst, send_sem, recv_sem, peer):
    return pltpu.make_async_remote_copy(src_ref=src, dst_ref=dst, send_sem=send_sem, recv_sem=recv_sem,
                                        device_id=peer, device_id_type=pl.DeviceIdType.MESH)


N_CHIPS = N_DEV // 2
SIBLING = 1
SAME_CORE_PEERS = (2, 4, 6)


def _exchange_out_shapes(job):
    def shape(kind, a):
        if kind in ("gather", "gather2"):
            return (N_DEV,) + a.shape
        if kind == "scatter_p1":
            return (N_CHIPS,) + a.shape[1:]
        return a.shape
    return [jax.ShapeDtypeStruct(shape(kind, a), a.dtype) for kind, a in job]


def _exchange_sems(n):
    return [pltpu.SemaphoreType.DMA((n, N_PEERS)), pltpu.SemaphoreType.DMA((n, N_PEERS)), pltpu.SemaphoreType.DMA((n,))]


def _exchange_copies(kinds, src_refs, dst_refs, send_sems, recv_sems, local_sems, phase):
    pos = _my_pos()
    me = _flat(pos)
    chip, core = 2 * pos[0] + pos[1], pos[2]
    copies = []

    def remote(a, src, dst, k, sem=None):
        sem = k - 1 if sem is None else sem
        copies.append(_remote_copy(src, dst, send_sems.at[a, sem], recv_sems.at[a, sem], _peer(pos, k)))

    for a, kind in enumerate(kinds):
        src, dst = src_refs[a], dst_refs[a]
        if phase == 1:
            if kind == "gather2":
                for k in SAME_CORE_PEERS:
                    remote(a, dst.at[me ^ k], dst.at[me ^ k], SIBLING, sem=k)
        elif kind in ("gather", "gather2"):
            copies.append(pltpu.make_async_copy(src, dst.at[me], local_sems.at[a]))
            for k in (range(1, N_DEV) if kind == "gather" else (SIBLING,) + SAME_CORE_PEERS):
                remote(a, src, dst.at[me], k)
        elif kind == "scatter":
            copies.append(pltpu.make_async_copy(src.at[me], dst.at[me], local_sems.at[a]))
            for k in range(1, N_DEV):
                remote(a, src.at[me ^ k], dst.at[me], k)
        elif kind == "scatter_p1":
            for q in range(N_CHIPS):
                remote(a, src.at[2 * q + 1 - core], dst.at[q], SIBLING, sem=q)
        elif kind == "scatter_p2":
            copies.append(pltpu.make_async_copy(src.at[chip], dst.at[chip], local_sems.at[a]))
            for k in SAME_CORE_PEERS:
                remote(a, src.at[chip ^ (k >> 1)], dst.at[chip], k)
    return copies


def _pallas_call_with_exchange(body, *, grid, in_specs, out_specs, out_shape, scratch_shapes, operands, name, job):
    params = _cparams(len(grid))
    if not job:
        outs = pl.pallas_call(body, grid=grid, in_specs=in_specs, out_specs=out_specs, out_shape=out_shape,
                              scratch_shapes=scratch_shapes, compiler_params=params, name=name)(*operands)
        return outs, []
    kinds = [kind for kind, _ in job]
    relayed = [kind if kind == "gather2" else None for kind in kinds]
    unrelayed = [None if kind == "gather2" else kind for kind in kinds]
    n, n_in, n_out, n_scr = len(job), len(in_specs), len(out_specs), len(scratch_shapes)
    n_steps = math.prod(grid)
    relay_step = max(n_steps - 2, 0)

    def wrapped(*refs):
        ins, jin = refs[:n_in], refs[n_in:n_in + n]
        outs, jout = refs[n_in + n:n_in + n + n_out], refs[n_in + n + n_out:n_in + 2 * n + n_out]
        scr = refs[n_in + 2 * n + n_out:n_in + 2 * n + n_out + n_scr]
        sems = refs[n_in + 2 * n + n_out + n_scr:]
        step = pl.program_id(0)
        for d in range(1, len(grid)):
            step = step * grid[d] + pl.program_id(d)

        def copies(which, phase):
            return _exchange_copies(which, jin, jout, *sems, phase=phase)

        @pl.when(step == 0)
        def _():
            for cp in copies(kinds, 0):
                cp.start()

        body(*ins, *outs, *scr)

        @pl.when(step == relay_step)
        def _():
            for cp in copies(relayed, 0):
                cp.wait()
            for cp in copies(relayed, 1):
                cp.start()

        @pl.when(step == n_steps - 1)
        def _():
            for cp in copies(unrelayed, 0) + copies(relayed, 1):
                cp.wait()

    res = pl.pallas_call(
        wrapped, grid=grid,
        in_specs=list(in_specs) + [ANY] * n,
        out_specs=list(out_specs) + [ANY] * n,
        out_shape=list(out_shape) + _exchange_out_shapes(job),
        scratch_shapes=list(scratch_shapes) + _exchange_sems(n),
        compiler_params=params, name=name,
    )(*operands, *[a for _, a in job])
    return res[:n_out], res[n_out:]


def _seg_mean(x, segp):
    hi = x.astype(BF16)
    lo = (x - hi.astype(F32)).astype(BF16)
    return _dot(hi, segp) + _dot(lo, segp)


def _rot_rows(x, shift):
    m, c = x.shape
    return pltpu.roll(x.reshape(m // 8, 8, c), shift, 1).reshape(m, c)


def _sublane_is(shape, s):
    return lax.broadcasted_iota(jnp.int32, shape, 0) % 8 == s


def _causal_tail(tail, prev_rot):
    rot = _rot_rows(tail, 1)
    return jnp.where(_sublane_is(tail.shape, 0), prev_rot, rot), rot


def _lookahead_head(head, next_rot):
    rot = _rot_rows(head, 7)
    return jnp.where(_sublane_is(head.shape, 7), next_rot, rot), rot


def _tile_token_index(t, tile_idx):
    r = lax.broadcasted_iota(jnp.int32, (t, 1), 0)
    return tile_idx * t + (r % 8) * (t // 8) + r // 8


def _interleave(x):
    t, c = x.shape
    return jnp.swapaxes(x.reshape(8, t // 8, c), 0, 1).reshape(t, c)


def _deinterleave(x):
    t, c = x.shape
    return jnp.swapaxes(x.reshape(t // 8, 8, c), 0, 1).reshape(t, c)


def _ffn_fwd(x1, modv, g1024, wup, wdn, cw, cb, name="ffn_fwd", job=None, loss_target=None):
    s_len = x1.shape[0]
    t = FFN_TILE
    n_tiles = s_len // t
    with_loss = loss_target is not None

    def body(x1_ref, *rest):
        if with_loss:
            tgt_ref, rest = rest[0], rest[1:]
            loss_ref, rest = rest[10], rest[:10] + rest[11:]
        mod_ref, g_ref, wup_ref, wdn_ref, cw_ref, cb_ref, x2_ref, y2_ref, p_ref, u_ref, ext_ref, carry_ref = rest
        i = pl.program_id(0)

        @pl.when(i == 0)
        def _():
            carry_ref[...] = jnp.zeros_like(carry_ref)
            if with_loss:
                loss_ref[...] = jnp.zeros_like(loss_ref)

        x1v = x1_ref[...]
        pre_g, post_g = g_ref[3:4, :], g_ref[4:5, :]
        sh2, sc2, g2 = mod_ref[3:4, :], mod_ref[4:5, :], mod_ref[5:6, :]
        xhat, _ = _rms_fwd(x1v)
        h2b = (xhat * pre_g * (1.0 + sc2) + sh2).astype(BF16)

        def conv_shard(s):
            p = _dot_nt(h2b, wup_ref[s])
            p_ref[s] = p.astype(BF16)
            ext_ref[0:FFN_HALO, :], carry_ref[s] = _causal_tail(p[t - FFN_HALO:t, :], carry_ref[s])
            ext_ref[FFN_HALO:FFN_HALO + t, :] = p
            w = cw_ref[s]
            u = w[0:1, :] * ext_ref[0:t, :] + w[1:2, :] * ext_ref[8:8 + t, :] + w[2:3, :] * p + cb_ref[s]
            u_ref[s] = u.astype(BF16)
            return u

        y2 = jnp.zeros((t, D_MODEL), F32)
        for j in range(FF_PAIRS):
            ug = conv_shard(j)
            uv = conv_shard(j + FF_PAIRS)
            ge, _ = _gelu(ug)
            y2 = y2 + _dot((ge * uv).astype(BF16), wdn_ref[j])
        y2_ref[...] = y2
        yhat, _ = _rms_fwd(y2)
        x2 = x1v + g2 * (yhat * post_g)
        if with_loss:
            diff = x2 - _interleave(tgt_ref[...])
            x2_ref[...] = diff * (1.0 / D_MODEL)
            loss_ref[...] += (0.5 / D_MODEL) * jnp.sum(diff * diff)
        else:
            x2_ref[...] = x2

    tile = pl.BlockSpec((t, D_MODEL), lambda i: (i, 0))
    consts = (modv, g1024, wup, wdn, cw, cb)
    shards = pl.BlockSpec((N_DEV, t, FF_SHARD), lambda i: (0, i, 0))
    out_specs = [tile, tile, shards, shards]
    out_shape = [jax.ShapeDtypeStruct((s_len, D_MODEL), F32), jax.ShapeDtypeStruct((s_len, D_MODEL), F32),
                 jax.ShapeDtypeStruct((N_DEV, s_len, FF_SHARD), BF16),
                 jax.ShapeDtypeStruct((N_DEV, s_len, FF_SHARD), BF16)]
    if with_loss:
        out_specs.append(pl.BlockSpec((8, 128), lambda i: (0, 0)))
        out_shape.append(jax.ShapeDtypeStruct((8, 128), F32))
    return _pallas_call_with_exchange(
        body,
        grid=(n_tiles,),
        in_specs=[tile] * (2 if with_loss else 1) + [_whole(c.shape) for c in consts],
        out_specs=out_specs,
        out_shape=out_shape,
        scratch_shapes=[pltpu.VMEM((FFN_HALO + t, FF_SHARD), F32), pltpu.VMEM((N_DEV, FFN_HALO, FF_SHARD), F32)],
        operands=(x1,) + ((loss_target,) if with_loss else ()) + consts,
        name=name, job=job)


def _ffn_bwd(dx2, x1, y2, p, u, modv, g1024, wup, wdn, cw, name="ffn_bwd", job=None):
    s_len = x1.shape[0]
    t = FFN_TILE
    n_tiles = s_len // t
    hb = FFN_HALO

    def body(dx2_ref, x1_ref, y2_ref, p_ref, ph_ref, u_ref, mod_ref, g_ref, wup_ref, wdn_ref, cw_ref,
             dx1_ref, dp_ref, a_ref, dy2_ref, h2_ref, vec_ref, cgrad_ref, ext_ref, dext_ref, dcarry_ref):
        i = pl.program_id(0)
        tile_idx = n_tiles - 1 - i

        @pl.when(i == 0)
        def _():
            vec_ref[...] = jnp.zeros_like(vec_ref)
            cgrad_ref[...] = jnp.zeros_like(cgrad_ref)
            dcarry_ref[...] = jnp.zeros_like(dcarry_ref)

        dx2v, x1v, y2v = dx2_ref[...], x1_ref[...], y2_ref[...]
        pre_g, post_g = g_ref[3:4, :], g_ref[4:5, :]
        sh2, sc2, g2 = mod_ref[3:4, :], mod_ref[4:5, :], mod_ref[5:6, :]

        yhat, ry = _rms_fwd(y2v)
        vec_ref[1:2, :] += _colsum(dx2v * (yhat * post_g))
        dyn = dx2v * g2
        vec_ref[0:1, :] += _colsum(dyn * yhat)
        dy2b = _rms_bwd(dyn * post_g, yhat, ry).astype(BF16)
        dy2_ref[...] = dy2b

        xhat, rx = _rms_fwd(x1v)
        xn = xhat * pre_g
        h2_ref[...] = (xn * (1.0 + sc2) + sh2).astype(BF16)

        not_first = (tile_idx > 0).astype(F32)

        def recompute(s, slot):
            pf = p_ref[s].astype(F32)
            prev_rot = _rot_rows(ph_ref[s].astype(F32), 1) * not_first
            ext_ref[slot, 0:hb, :], _ = _causal_tail(pf[t - hb:t, :], prev_rot)
            ext_ref[slot, hb:hb + t, :] = pf
            return u_ref[s].astype(F32)

        def conv_bwd(s, slot, du):
            w = cw_ref[s]
            cgrad_ref[s, 0:1, :] += _colsum(du * ext_ref[slot, 0:t, :])
            cgrad_ref[s, 1:2, :] += _colsum(du * ext_ref[slot, 8:8 + t, :])
            cgrad_ref[s, 2:3, :] += _colsum(du * ext_ref[slot, 16:16 + t, :])
            cgrad_ref[s, 3:4, :] += _colsum(du)
            dext_ref[0:t, :] = du
            dext_ref[t:t + hb, :], dcarry_ref[s] = _lookahead_head(du[0:hb, :], dcarry_ref[s])
            dp = w[2:3, :] * du + w[1:2, :] * dext_ref[8:8 + t, :] + w[0:1, :] * dext_ref[16:16 + t, :]
            dpb = dp.astype(BF16)
            dp_ref[s] = dpb
            return _dot(dpb, wup_ref[s])

        dh2 = jnp.zeros((t, D_MODEL), F32)
        for j in range(FF_PAIRS):
            ug = recompute(j, 0)
            uv = recompute(j + FF_PAIRS, 1)
            ge, th = _gelu(ug)
            a_ref[j] = (ge * uv).astype(BF16)
            da = _dot_nt(dy2b, wdn_ref[j])
            dh2 = dh2 + conv_bwd(j, 0, da * uv * _gelu_grad(ug, th))
            dh2 = dh2 + conv_bwd(j + FF_PAIRS, 1, da * ge)

        vec_ref[2:3, :] += _colsum(dh2)
        vec_ref[3:4, :] += _colsum(dh2 * xn)
        dxn = dh2 * (1.0 + sc2)
        vec_ref[4:5, :] += _colsum(dxn * xhat)
        dx1_ref[...] = dx2v + _rms_bwd(dxn * pre_g, xhat, rx)

    rev = lambda i: (n_tiles - 1 - i, 0)
    tile = pl.BlockSpec((t, D_MODEL), rev)
    halo_idx = lambda i: (0, jnp.maximum((n_tiles - 1 - i) * (t // hb) - 1, 0), 0)
    return _pallas_call_with_exchange(
        body,
        grid=(n_tiles,),
        in_specs=[tile, tile, tile,
                  pl.BlockSpec((N_DEV, t, FF_SHARD), lambda i: (0, n_tiles - 1 - i, 0)),
                  pl.BlockSpec((N_DEV, hb, FF_SHARD), halo_idx),
                  pl.BlockSpec((N_DEV, t, FF_SHARD), lambda i: (0, n_tiles - 1 - i, 0)),
                  _whole(modv.shape), _whole(g1024.shape), _whole(wup.shape), _whole(wdn.shape),
                  _whole(cw.shape)],
        out_specs=[tile,
                   pl.BlockSpec((N_DEV, t, FF_SHARD), lambda i: (0, n_tiles - 1 - i, 0)),
                   pl.BlockSpec((FF_PAIRS, t, FF_SHARD), lambda i: (0, n_tiles - 1 - i, 0)),
                   tile, tile,
                   pl.BlockSpec((8, D_MODEL), lambda i: (0, 0)),
                   pl.BlockSpec((N_DEV, 8, FF_SHARD), lambda i: (0, 0, 0))],
        out_shape=[jax.ShapeDtypeStruct((s_len, D_MODEL), F32),
                   jax.ShapeDtypeStruct((N_DEV, s_len, FF_SHARD), BF16),
                   jax.ShapeDtypeStruct((FF_PAIRS, s_len, FF_SHARD), BF16),
                   jax.ShapeDtypeStruct((s_len, D_MODEL), BF16),
                   jax.ShapeDtypeStruct((s_len, D_MODEL), BF16),
                   jax.ShapeDtypeStruct((8, D_MODEL), F32),
                   jax.ShapeDtypeStruct((N_DEV, 8, FF_SHARD), F32)],
        scratch_shapes=[pltpu.VMEM((2, hb + t, FF_SHARD), F32), pltpu.VMEM((t + hb, FF_SHARD), F32),
                        pltpu.VMEM((N_DEV, hb, FF_SHARD), F32)],
        operands=(dx2, x1, y2, p, p, u, modv, g1024, wup, wdn, cw),
        name=name, job=job)


def _wgrad(a, b, name, tk=WGRAD_TK, job=None):
    a_grouped, b_grouped = a.ndim == 3, b.ndim == 3
    groups = a.shape[0] if a_grouped else b.shape[0]
    s_len, m, n = a.shape[-2], a.shape[-1], b.shape[-1]
    tk = min(tk, s_len)
    n_k = s_len // tk

    def body(a_ref, b_ref, o_ref, acc_ref):
        k = pl.program_id(1)
        av = a_ref[0] if a_grouped else a_ref[...]
        bv = b_ref[0] if b_grouped else b_ref[...]
        part = _dot_tn(av, bv)
        if n_k == 1:
            o_ref[0] = part.astype(BF16)
            return

        @pl.when(k == 0)
        def _():
            acc_ref[...] = part

        @pl.when(jnp.logical_and(k > 0, k < n_k - 1))
        def _():
            acc_ref[...] += part

        @pl.when(k == n_k - 1)
        def _():
            o_ref[0] = (acc_ref[...] + part).astype(BF16)

    a_spec = pl.BlockSpec((1, tk, m), lambda g, k: (g, k, 0)) if a_grouped else pl.BlockSpec((tk, m), lambda g, k: (k, 0))
    b_spec = pl.BlockSpec((1, tk, n), lambda g, k: (g, k, 0)) if b_grouped else pl.BlockSpec((tk, n), lambda g, k: (k, 0))
    (out,), exchanged = _pallas_call_with_exchange(
        body,
        grid=(groups, n_k),
        in_specs=[a_spec, b_spec],
        out_specs=[pl.BlockSpec((1, m, n), lambda g, k: (g, 0, 0))],
        out_shape=[jax.ShapeDtypeStruct((groups, m, n), BF16)],
        scratch_shapes=[pltpu.VMEM((m, n), F32)],
        operands=(a, b),
        name=name, job=job)
    return (out, exchanged) if job else out


def _lane(shape):
    return lax.broadcasted_iota(jnp.int32, shape, 1)


def _by_pool_group(shape, vals):
    lane = _lane(shape)
    return jnp.where(lane < 64, vals[0], jnp.where(lane < 128, vals[1], jnp.where(lane < 192, vals[2], vals[3])))


def _pool_inv_counts(t, tile_idx):
    pos1 = _tile_token_index(t, tile_idx) + 1
    return [1.0 / jnp.minimum(pos1, w).astype(F32) for w in POOL_WINDOWS]


def _sgu_keep_mask(t):
    tok_r = _tile_token_index(t, 0)
    c = lax.broadcasted_iota(jnp.int32, (1, t), 1)
    tok_c = (c % 8) * (t // 8) + c // 8
    return jnp.logical_and(tok_r // CHUNK == tok_c // CHUNK, tok_r >= tok_c)


def _masked_sgu_w(sguw_ref):
    keep = _sgu_keep_mask(sguw_ref.shape[1])
    return [jnp.where(keep, sguw_ref[h], 0.0).astype(BF16) for h in range(SGU_HEADS)]


def _branches_fwd(z, tile_idx, p384_ref, cw_ref, wm, bmat_ref, pwbd_ref, psc_ref, segp_ref, g_ref, hext_ref, zext_ref,
                  h_prev_rot, z_prev_rot, conv_saved=None):
    t = z.shape[0]
    segp = segp_ref[...]
    r = {}
    u, _ = _gelu(z[:, 0:SGU_WIDTH])
    vraw, _ = _gelu(z[:, SGU_WIDTH:2 * SGU_WIDTH])
    xc = vraw - _seg_mean(vraw, segp)
    rstd_v = lax.rsqrt(_seg_mean(xc * xc, segp) + EPS)
    xh_v = xc * rstd_v
    vnb = (xh_v * p384_ref[0:1, :] + p384_ref[1:2, :]).astype(BF16)
    first_head = _lane((t, 128)) < HEAD_DIM
    f_pairs = []
    for pr in range(SGU_HEADS // 2):
        vp = vnb[:, pr * 128:(pr + 1) * 128]
        f_pairs.append(jnp.where(first_head, _dot(wm[2 * pr], vp), _dot(wm[2 * pr + 1], vp)))
    f = jnp.concatenate(f_pairs, axis=1) + bmat_ref[...]
    ya = u * f
    r.update(u=u, xh_v=xh_v, rstd_v=rstd_v, vnb=vnb, f=f)
    o_b = 2 * SGU_WIDTH
    a_in = z[:, o_b:o_b + CONV_WIDTH]
    sig_g = jax.nn.sigmoid(z[:, o_b + CONV_WIDTH:o_b + 2 * CONV_WIDTH])
    hh = a_in * sig_g
    hext_ref[0:t, :], r["h_rot"] = _causal_tail(hh, h_prev_rot)
    hext_ref[t:2 * t, :] = hh
    if conv_saved is None:
        conv = jnp.zeros((t, CONV_WIDTH), F32) + p384_ref[2:3, :]
        for k in range(CONV_K):
            conv = conv + cw_ref[k:k + 1, :] * hext_ref[pl.ds(t - 8 * (CONV_K - 1 - k), t), :]
        r["conv"] = conv
    else:
        conv = conv_saved
    cc = conv - _rowmean(conv)
    rstd_c = lax.rsqrt(_rowmean(cc * cc) + EPS)
    xh_c = cc * rstd_c
    cn = xh_c * p384_ref[3:4, :] + p384_ref[4:5, :]
    sig_c = jax.nn.sigmoid(cn)
    yb = cn * sig_c
    r.update(a_in=a_in, sig_g=sig_g, xh_c=xh_c, rstd_c=rstd_c, cn=cn, sig_c=sig_c)
    o_c = o_b + 2 * CONV_WIDTH
    zc = z[:, o_c:o_c + POOL_WIDTH]
    zext_ref[0:POOL_HALO, :], r["z_rot"] = _causal_tail(zc[t - POOL_HALO:t, :], z_prev_rot)
    zext_ref[POOL_HALO:POOL_HALO + t, :] = zc
    sums, acc = [], zc
    for j in range(1, POOL_WINDOWS[-1]):
        acc = acc + zext_ref[pl.ds(POOL_HALO - 8 * j, t), :]
        if j + 1 in POOL_WINDOWS:
            sums.append(acc)
    inv = _pool_inv_counts(t, tile_idx)
    dpool = _by_pool_group((t, POOL_WIDTH), [s * iv for s, iv in zip(sums, inv)]) - zc
    ycp = _dot(dpool.astype(BF16), pwbd_ref[...])
    yc = ycp * psc_ref[0:1, :]
    r.update(dpool=dpool, ycp=ycp)
    yha, ra = _rms_fwd(ya)
    yhb, rb = _rms_fwd(yb)
    yhc, rc = _rms_fwd(yc)
    bg = g_ref[2:3, :]
    ycat = jnp.concatenate([yha * bg[:, 0:384], yhb * bg[:, 384:768], yhc * bg[:, 768:1024]], axis=1)
    r.update(yha=yha, ra=ra, yhb=yhb, rb=rb, yhc=yhc, rc=rc, ycat=ycat)
    return r


def _mixer_fwd(x, modv, g1024, p384, cw, sguw, bmat, pwbd, psc, segp, win, wout, name="mixer_fwd", job=None,
               natural_x=False):
    s_len = x.shape[0]
    t = MIX_TILE

    def body(x_ref, mod_ref, g_ref, p384_ref, cw_ref, sguw_ref, bmat_ref, pwbd_ref, psc_ref, segp_ref, win_ref, wout_ref,
             x1_ref, z_ref, o_ref, conv_ref, hext_ref, zext_ref, hrot_ref, zrot_ref, wm_ref):
        i = pl.program_id(0)

        @pl.when(i == 0)
        def _():
            hrot_ref[...] = jnp.zeros_like(hrot_ref)
            zrot_ref[...] = jnp.zeros_like(zrot_ref)
            for h, wmh in enumerate(_masked_sgu_w(sguw_ref)):
                wm_ref[h] = wmh

        xv = _interleave(x_ref[...]) if natural_x else x_ref[...]
        sh1, sc1, g1 = mod_ref[0:1, :], mod_ref[1:2, :], mod_ref[2:3, :]
        xhat, _ = _rms_fwd(xv)
        h1 = xhat * g_ref[0:1, :] * (1.0 + sc1) + sh1
        z = _dot_nt(h1.astype(BF16), win_ref[...])
        z_ref[...] = z
        r = _branches_fwd(z, i, p384_ref, cw_ref, [wm_ref[h] for h in range(SGU_HEADS)], bmat_ref, pwbd_ref, psc_ref,
                          segp_ref, g_ref, hext_ref, zext_ref, hrot_ref[...], zrot_ref[...])
        hrot_ref[...] = r["h_rot"]
        zrot_ref[...] = r["z_rot"]
        conv_ref[...] = r["conv"]
        o = _dot(r["ycat"].astype(BF16), wout_ref[...])
        o_ref[...] = o
        ohat, _ = _rms_fwd(o)
        x1_ref[...] = xv + g1 * (ohat * g_ref[1:2, :])

    tile = pl.BlockSpec((t, D_MODEL), lambda i: (i, 0))
    consts = (modv, g1024, p384, cw, sguw, bmat, pwbd, psc, segp, win, wout)
    return _pallas_call_with_exchange(
        body,
        grid=(s_len // t,),
        in_specs=[tile] + [_whole(c.shape) for c in consts],
        out_specs=[tile, pl.BlockSpec((t, IN_WIDTH), lambda i: (i, 0)), tile,
                   pl.BlockSpec((t, CONV_WIDTH), lambda i: (i, 0))],
        out_shape=[jax.ShapeDtypeStruct((s_len, D_MODEL), F32), jax.ShapeDtypeStruct((s_len, IN_WIDTH), F32),
                   jax.ShapeDtypeStruct((s_len, D_MODEL), F32), jax.ShapeDtypeStruct((s_len, CONV_WIDTH), F32)],
        scratch_shapes=[pltpu.VMEM((2 * t, CONV_WIDTH), F32), pltpu.VMEM((POOL_HALO + t, POOL_WIDTH), F32),
                        pltpu.VMEM((t, CONV_WIDTH), F32), pltpu.VMEM((POOL_HALO, POOL_WIDTH), F32),
                        pltpu.VMEM((SGU_HEADS, t, t), BF16)],
        operands=(x, *consts),
        name=name, job=job)


def _mixer_bwd(dx1, x, o, z, conv, modv, g1024, p384, cw, sguw, bmat, pwbd, psc, segp, win, wout, name="mixer_bwd",
               job=None, natural_x=False):
    s_len = x.shape[0]
    t = MIX_TILE
    n_tiles = s_len // t

    def body(dx1_ref, x_ref, o_ref, z_ref, zh_ref, conv_ref, mod_ref, g_ref, p384_ref, cw_ref, sguw_ref, bmat_ref, pwbd_ref,
             psc_ref, segp_ref, win_ref, wout_ref,
             dx_ref, dz_ref, do_ref, ycat_ref, h1_ref, vec_ref, v384_ref, dcw_ref, dsguw_out_ref, dbmat_out_ref, dpw_ref,
             dpsc_ref, hext_ref, zext_ref, gext_ref, qext_ref, grot_ref, qrot_ref, wm_ref, dsguw_ref, dbmat_ref):
        i = pl.program_id(0)
        tile_idx = n_tiles - 1 - i

        @pl.when(i == 0)
        def _():
            for ref in (vec_ref, v384_ref, dcw_ref, dsguw_ref, dbmat_ref, dpw_ref, dpsc_ref, grot_ref, qrot_ref):
                ref[...] = jnp.zeros_like(ref)
            for h, wmh in enumerate(_masked_sgu_w(sguw_ref)):
                wm_ref[h] = wmh

        dx1v, ov, z = dx1_ref[...], o_ref[...], z_ref[...]
        xv = _interleave(x_ref[...]) if natural_x else x_ref[...]
        sh1, sc1, g1 = mod_ref[0:1, :], mod_ref[1:2, :], mod_ref[2:3, :]
        pre_g, post_g, bg = g_ref[0:1, :], g_ref[1:2, :], g_ref[2:3, :]
        segp = segp_ref[...]

        ohat, ro = _rms_fwd(ov)
        vec_ref[1:2, :] += _colsum(dx1v * (ohat * post_g))
        don = dx1v * g1
        vec_ref[0:1, :] += _colsum(don * ohat)
        dob = _rms_bwd(don * post_g, ohat, ro).astype(BF16)
        do_ref[...] = dob
        dycat = _dot_nt(dob, wout_ref[...])

        not_first = (tile_idx > 0).astype(F32)
        o_b = 2 * SGU_WIDTH
        o_c = o_b + 2 * CONV_WIDTH
        h_prev = zh_ref[:, o_b:o_b + CONV_WIDTH] * jax.nn.sigmoid(zh_ref[:, o_b + CONV_WIDTH:o_c])
        h_prev_rot = _rot_rows(h_prev, 1) * not_first
        z_prev_rot = _rot_rows(zh_ref[t - POOL_HALO:t, o_c:o_c + POOL_WIDTH], 1) * not_first
        wm = [wm_ref[h] for h in range(SGU_HEADS)]
        r = _branches_fwd(z, tile_idx, p384_ref, cw_ref, wm, bmat_ref, pwbd_ref, psc_ref, segp_ref, g_ref,
                          hext_ref, zext_ref, h_prev_rot, z_prev_rot, conv_saved=conv_ref[...])
        ycat_ref[...] = r["ycat"].astype(BF16)

        def branch_norm_bwd(dyn, yhat, rr, gain):
            return _colsum(dyn * yhat), _rms_bwd(dyn * gain, yhat, rr)

        dga, dya = branch_norm_bwd(dycat[:, 0:384], r["yha"], r["ra"], bg[:, 0:384])
        dgb, dyb = branch_norm_bwd(dycat[:, 384:768], r["yhb"], r["rb"], bg[:, 384:768])
        dgc, dyc = branch_norm_bwd(dycat[:, 768:1024], r["yhc"], r["rc"], bg[:, 768:1024])
        vec_ref[5:6, :] += jnp.concatenate([dga, dgb, dgc], axis=1)

        du_act = dya * r["f"]
        df = dya * r["u"]
        first_head = _lane((t, 128)) < HEAD_DIM
        dbmat_ref[...] += df
        dvn_pairs = []
        for pr in range(SGU_HEADS // 2):
            dfp = df[:, pr * 128:(pr + 1) * 128]
            df0 = jnp.where(first_head, dfp, 0.0).astype(BF16)
            df1 = jnp.where(first_head, 0.0, dfp).astype(BF16)
            vp = r["vnb"][:, pr * 128:(pr + 1) * 128]
            dvn_pairs.append(_dot_tn(wm[2 * pr], df0) + _dot_tn(wm[2 * pr + 1], df1))
            dsguw_ref[2 * pr] += _dot_nt(df0, vp)
            dsguw_ref[2 * pr + 1] += _dot_nt(df1, vp)
        dvn = jnp.concatenate(dvn_pairs, axis=1)
        v384_ref[0:1, :] += _colsum(dvn * r["xh_v"])
        v384_ref[1:2, :] += _colsum(dvn)
        dxh = dvn * p384_ref[0:1, :]
        dvraw = r["rstd_v"] * (dxh - _seg_mean(dxh, segp) - r["xh_v"] * _seg_mean(dxh * r["xh_v"], segp))
        zu, zv = z[:, 0:SGU_WIDTH], z[:, SGU_WIDTH:o_b]
        _, tu = _gelu(zu)
        _, tv = _gelu(zv)
        dz_u = du_act * _gelu_grad(zu, tu)
        dz_v = dvraw * _gelu_grad(zv, tv)

        cn, sig_c = r["cn"], r["sig_c"]
        dcn = dyb * (sig_c * (1.0 + cn * (1.0 - sig_c)))
        v384_ref[3:4, :] += _colsum(dcn * r["xh_c"])
        v384_ref[4:5, :] += _colsum(dcn)
        dxc = dcn * p384_ref[3:4, :]
        gconv = r["rstd_c"] * (dxc - _rowmean(dxc) - r["xh_c"] * _rowmean(dxc * r["xh_c"]))
        v384_ref[2:3, :] += _colsum(gconv)
        gext_ref[0:t, :] = gconv
        gext_ref[t:2 * t, :], grot_ref[...] = _lookahead_head(gconv, grot_ref[...])
        dhh = jnp.zeros((t, CONV_WIDTH), F32)
        for k in range(CONV_K):
            shift = CONV_K - 1 - k
            dcw_ref[k:k + 1, :] += _colsum(gconv * hext_ref[pl.ds(t - 8 * shift, t), :])
            dhh = dhh + cw_ref[k:k + 1, :] * gext_ref[pl.ds(8 * shift, t), :]
        sig_g = r["sig_g"]
        dz_a = dhh * sig_g
        dz_g = dhh * r["a_in"] * sig_g * (1.0 - sig_g)

        dpsc_ref[0:1, :] += _colsum(dyc * r["ycp"])
        dycp = (dyc * psc_ref[0:1, :]).astype(BF16)
        dpw_ref[...] += _dot_tn(r["dpool"].astype(BF16), dycp)
        ddp = _dot_nt(dycp, pwbd_ref[...])
        inv = _pool_inv_counts(t, tile_idx)
        q = ddp * _by_pool_group((t, POOL_WIDTH), inv)
        qext_ref[0:t, :] = q
        qext_ref[t:t + POOL_HALO, :], qrot_ref[...] = _lookahead_head(q[0:POOL_HALO, :], qrot_ref[...])
        sums, acc = [], q
        for j in range(1, POOL_WINDOWS[-1]):
            acc = acc + qext_ref[pl.ds(8 * j, t), :]
            if j + 1 in POOL_WINDOWS:
                sums.append(acc)
        dz_c = _by_pool_group((t, POOL_WIDTH), sums) - ddp

        dzb = jnp.concatenate([dz_u, dz_v, dz_a, dz_g, dz_c], axis=1).astype(BF16)
        dz_ref[...] = dzb
        dh1 = _dot(dzb, win_ref[...])

        xhat, rx = _rms_fwd(xv)
        xn = xhat * pre_g
        h1_ref[...] = (xn * (1.0 + sc1) + sh1).astype(BF16)
        vec_ref[2:3, :] += _colsum(dh1)
        vec_ref[3:4, :] += _colsum(dh1 * xn)
        dxn = dh1 * (1.0 + sc1)
        vec_ref[4:5, :] += _colsum(dxn * xhat)
        dx = dx1v + _rms_bwd(dxn * pre_g, xhat, rx)
        dx_ref[...] = _deinterleave(dx) if natural_x else dx

        @pl.when(i == n_tiles - 1)
        def _():
            keep = _sgu_keep_mask(t)
            for h in range(SGU_HEADS):
                rows_natural = _deinterleave(jnp.where(keep, dsguw_ref[h], 0.0))
                natural = _deinterleave(rows_natural.T).T
                dsguw_out_ref[h] = sum(natural[b * CHUNK:(b + 1) * CHUNK, b * CHUNK:(b + 1) * CHUNK]
                                       for b in range(t // CHUNK))
            dbmat = _deinterleave(float(HEAD_DIM) * _seg_mean(dbmat_ref[...], segp))
            dbmat_out_ref[...] = sum(dbmat[b * CHUNK:(b + 1) * CHUNK, :] for b in range(t // CHUNK))

    rev = lambda i: (n_tiles - 1 - i, 0)
    tile = pl.BlockSpec((t, D_MODEL), rev)
    ztile = pl.BlockSpec((t, IN_WIDTH), rev)
    zhalo = pl.BlockSpec((t, IN_WIDTH), lambda i: (jnp.maximum(n_tiles - 2 - i, 0), 0))
    consts = (modv, g1024, p384, cw, sguw, bmat, pwbd, psc, segp, win, wout)
    acc = lambda shape: pl.BlockSpec(shape, lambda i: (0,) * len(shape))
    acc_shapes = [(8, D_MODEL), (8, SGU_WIDTH), (32, CONV_WIDTH), (SGU_HEADS, CHUNK, CHUNK), (CHUNK, SGU_WIDTH),
                  (POOL_WIDTH, POOL_WIDTH), (8, POOL_WIDTH)]
    return _pallas_call_with_exchange(
        body,
        grid=(n_tiles,),
        in_specs=[tile, tile, tile, ztile, zhalo, pl.BlockSpec((t, CONV_WIDTH), rev)] + [_whole(c.shape) for c in consts],
        out_specs=[tile, ztile, tile, tile, tile] + [acc(s) for s in acc_shapes],
        out_shape=[jax.ShapeDtypeStruct((s_len, D_MODEL), F32), jax.ShapeDtypeStruct((s_len, IN_WIDTH), BF16),
                   jax.ShapeDtypeStruct((s_len, D_MODEL), BF16), jax.ShapeDtypeStruct((s_len, D_MODEL), BF16),
                   jax.ShapeDtypeStruct((s_len, D_MODEL), BF16)] + [jax.ShapeDtypeStruct(s, F32) for s in acc_shapes],
        scratch_shapes=[pltpu.VMEM((2 * t, CONV_WIDTH), F32), pltpu.VMEM((POOL_HALO + t, POOL_WIDTH), F32),
                        pltpu.VMEM((2 * t, CONV_WIDTH), F32), pltpu.VMEM((t + POOL_HALO, POOL_WIDTH), F32),
                        pltpu.VMEM((t, CONV_WIDTH), F32), pltpu.VMEM((POOL_HALO, POOL_WIDTH), F32),
                        pltpu.VMEM((SGU_HEADS, t, t), BF16), pltpu.VMEM((SGU_HEADS, t, t), F32),
                        pltpu.VMEM((t, SGU_WIDTH), F32)],
        operands=(dx1, x, o, z, z, conv, *consts),
        name=name, job=job)


MOD_SHARD = 6 * D_MODEL // N_DEV

ROW_DMOD = 0
ROW_G1024 = 8
ROW_V384 = 16
ROW_SGU_B = 24
ROW_POOL_SCALE = 25
ROW_LOSS = 26
ROW_CONV_W = 32
ROW_FFN_CONV = 64
ROW_POOL_W = 96
ROW_SGU_W = 112
ROWS_PER_LAYER = 208
N_LAYERS = 2


def _gather_weights(c8, mod_w, mod_b8, job):
    kinds = [kind for kind, _ in job]
    shards = [a for _, a in job]
    n = len(shards)

    def body(c_ref, modw_ref, modb_ref, *rest):
        shard_refs = rest[:n]
        sc_all_ref, modrows_ref = rest[n], rest[n + 1]
        full_refs = rest[n + 2:2 * n + 2]
        send_buf, mod_recv, w_send, w_recv, w_local, sc_send, sc_recv, mod_send, mod_recv_sem = rest[2 * n + 2:]
        pos = _my_pos()
        me = _flat(pos)
        peers = [_peer(pos, k) for k in range(1, N_DEV)]

        w_copies = _exchange_copies(kinds, shard_refs, full_refs, w_send, w_recv, w_local, phase=0)
        for cp in w_copies:
            cp.start()

        cv = c_ref[...]
        sc_all_ref[me] = cv * jax.nn.sigmoid(cv)
        sc_copies = [_remote_copy(sc_all_ref.at[me], sc_all_ref.at[me], sc_send.at[k], sc_recv.at[k], peers[k])
                     for k in range(N_PEERS)]
        for cp in sc_copies:
            cp.start()
        for cp in sc_copies:
            cp.wait()

        sc = jnp.concatenate([sc_all_ref[j, 0:1, :] for j in range(N_DEV)], axis=0)
        send_buf[...] = jnp.zeros_like(send_buf)
        for l in range(N_LAYERS):
            part = jnp.dot(sc, modw_ref[l], precision=lax.Precision.HIGHEST, preferred_element_type=F32)
            for j in range(N_DEV):
                send_buf[j, l:l + 1, :] = part[j:j + 1, :]
        mod_recv[me] = send_buf[me]
        mod_copies = [_remote_copy(send_buf.at[_flat(peers[k])], mod_recv.at[me], mod_send.at[k], mod_recv_sem.at[k],
                                   peers[k]) for k in range(N_PEERS)]
        for cp in mod_copies:
            cp.start()
        for cp in mod_copies:
            cp.wait()
        modrows_ref[...] = jnp.zeros_like(modrows_ref)
        for l in range(N_LAYERS):
            row = jnp.concatenate([mod_recv[j, l:l + 1, :] for j in range(N_DEV)], axis=1)
            modrows_ref[l:l + 1, :] = row + modb_ref[l:l + 1, :]

        for cp in w_copies:
            cp.wait()
        relays = _exchange_copies(kinds, shard_refs, full_refs, w_send, w_recv, w_local, phase=1)
        for cp in relays:
            cp.start()
        for cp in relays:
            cp.wait()

    out_shape = ([jax.ShapeDtypeStruct((N_DEV, 8, D_MODEL), F32), jax.ShapeDtypeStruct((8, 6 * D_MODEL), F32)]
                 + [jax.ShapeDtypeStruct((N_DEV,) + s.shape, s.dtype) for s in shards])
    return pl.pallas_call(
        body,
        in_specs=[VMEM, VMEM, VMEM] + [ANY] * n,
        out_specs=[VMEM, VMEM] + [ANY] * n,
        out_shape=out_shape,
        scratch_shapes=[pltpu.VMEM((N_DEV, 8, MOD_SHARD), F32), pltpu.VMEM((N_DEV, 8, MOD_SHARD), F32),
                        pltpu.SemaphoreType.DMA((n, N_PEERS)), pltpu.SemaphoreType.DMA((n, N_PEERS)),
                        pltpu.SemaphoreType.DMA((n,)),
                        pltpu.SemaphoreType.DMA((N_PEERS,)), pltpu.SemaphoreType.DMA((N_PEERS,)),
                        pltpu.SemaphoreType.DMA((N_PEERS,)), pltpu.SemaphoreType.DMA((N_PEERS,))],
        compiler_params=pltpu.CompilerParams(vmem_limit_bytes=VMEM_LIMIT_BYTES),
        name="gather_weights",
    )(c8, mod_w, mod_b8, *shards)


def _small_sums(sc_all, small_all0, small_all1, job):
    kinds = [kind for kind, _ in job]
    n = len(job)

    def body(sc_all_ref, small_all0_ref, small_all1_ref, *rest):
        small_sum_ref, gmodw_ref = rest[n], rest[n + 1]
        copies = _exchange_copies(kinds, rest[:n], rest[n + 2:2 * n + 2], *rest[2 * n + 2:], phase=0)
        for cp in copies:
            cp.start()
        me = _flat(_my_pos())
        sc = jnp.concatenate([sc_all_ref[j, 0:1, :] for j in range(N_DEV)], axis=0)
        mine = lax.broadcasted_iota(jnp.int32, (2 * N_DEV, MOD_SHARD), 0) == me
        for l, parts in enumerate((small_all0_ref, small_all1_ref)):
            total = parts[0].astype(F32)
            for j in range(1, N_DEV):
                total = total + parts[j].astype(F32)
            small_sum_ref[l] = total
            dm = jnp.concatenate(
                [jnp.sum(jnp.where(mine, parts[j, ROW_DMOD:ROW_DMOD + 2 * N_DEV, 0:MOD_SHARD].astype(F32), 0.0),
                         axis=0, keepdims=True) for j in range(N_DEV)], axis=0)
            gmodw_ref[l] = lax.dot_general(sc, dm, (((0,), (0,)), ((), ())), precision=lax.Precision.HIGHEST,
                                           preferred_element_type=F32)
        for cp in copies:
            cp.wait()

    res = pl.pallas_call(
        body,
        in_specs=[VMEM, VMEM, VMEM] + [ANY] * n,
        out_specs=[VMEM, VMEM] + [ANY] * n,
        out_shape=[jax.ShapeDtypeStruct((N_LAYERS, ROWS_PER_LAYER, D_MODEL), F32),
                   jax.ShapeDtypeStruct((N_LAYERS, D_MODEL, MOD_SHARD), F32)] + _exchange_out_shapes(job),
        scratch_shapes=_exchange_sems(n),
        compiler_params=pltpu.CompilerParams(vmem_limit_bytes=VMEM_LIMIT_BYTES),
        name="small_sums",
    )(sc_all, small_all0, small_all1, *[a for _, a in job])
    return res[0], res[1], res[2:]


def _adam_update(g, w, m, v):
    m2 = ADAM_B1 * m + (1.0 - ADAM_B1) * g
    v2 = ADAM_B2 * v + (1.0 - ADAM_B2) * (g * g)
    m_hat = m2 / (1.0 - ADAM_B1 ** ADAM_STEP)
    v_hat = v2 / (1.0 - ADAM_B2 ** ADAM_STEP)
    delta = -ADAM_LR * (m_hat / (jnp.sqrt(v_hat) + ADAM_EPS) + ADAM_WD * w)
    return delta, m2, v2


def _pair_add(g, r1, row_chunk, name):
    _, rows, cols = g.shape
    core = lax.axis_index("c").astype(jnp.int32).reshape(1)

    def body(core_ref, g_ref, r_ref, o_ref):
        o_ref[0] = (g_ref[0, 0].astype(F32) + r_ref[0].astype(F32)).astype(BF16)

    blk = pl.BlockSpec((1, row_chunk, cols), lambda q, i, core_ref: (q, i, 0))
    grid_spec = pltpu.PrefetchScalarGridSpec(
        num_scalar_prefetch=1, grid=(N_CHIPS, rows // row_chunk),
        in_specs=[pl.BlockSpec((1, 1, row_chunk, cols), lambda q, i, core_ref: (q, core_ref[0], i, 0)), blk],
        out_specs=blk)
    return pl.pallas_call(
        body, grid_spec=grid_spec, out_shape=jax.ShapeDtypeStruct((N_CHIPS, rows, cols), BF16),
        compiler_params=_cparams(2), name=name,
    )(core, g.reshape(N_CHIPS, 2, rows, cols), r1)


def _adam_sharded(recv0, recv1, w, m, v, row_chunk, name):
    _, rows, cols = w.shape
    n_chunks = rows // row_chunk

    def body(r0_ref, r1_ref, w_ref, m_ref, v_ref, g_ref, d_ref, m2_ref, v2_ref):
        layer = pl.program_id(0)

        def run(r_ref):
            g = r_ref[0].astype(F32)
            for j in range(1, r_ref.shape[0]):
                g = g + r_ref[j].astype(F32)
            delta, m2, v2 = _adam_update(g, w_ref[0], m_ref[0], v_ref[0])
            g_ref[0], d_ref[0], m2_ref[0], v2_ref[0] = g, delta, m2, v2

        @pl.when(layer == 0)
        def _():
            run(r0_ref)

        @pl.when(layer == 1)
        def _():
            run(r1_ref)

    r0_spec = pl.BlockSpec((recv0.shape[0], row_chunk, cols), lambda l, i: (0, i * (1 - l) + (n_chunks - 1) * l, 0))
    r1_spec = pl.BlockSpec((recv1.shape[0], row_chunk, cols), lambda l, i: (0, i * l, 0))
    blk = pl.BlockSpec((1, row_chunk, cols), lambda l, i: (l, i, 0))
    out = jax.ShapeDtypeStruct(w.shape, F32)
    return pl.pallas_call(
        body,
        grid=(N_LAYERS, n_chunks),
        in_specs=[r0_spec, r1_spec, blk, blk, blk],
        out_specs=[blk] * 4,
        out_shape=[out] * 4,
        compiler_params=_cparams(2),
        name=name,
    )(recv0, recv1, w, m, v)


def _adam_dense(g, w, m, v, row_chunk, name):
    n_lead, rows, cols = w.shape

    def body(g_ref, w_ref, m_ref, v_ref, go_ref, d_ref, m2_ref, v2_ref):
        gv = g_ref[...]
        go_ref[...] = gv
        d_ref[...], m2_ref[...], v2_ref[...] = _adam_update(gv, w_ref[...], m_ref[...], v_ref[...])

    blk = pl.BlockSpec((1, row_chunk, cols), lambda l, i: (l, i, 0))
    out = jax.ShapeDtypeStruct(w.shape, F32)
    return pl.pallas_call(
        body,
        grid=(n_lead, rows // row_chunk),
        in_specs=[blk] * 4,
        out_specs=[blk] * 4,
        out_shape=[out] * 4,
        compiler_params=_cparams(2),
        name=name,
    )(g, w, m, v)


WEIGHT_NAMES = ("mod_w", "mod_b", "mix_pre_g", "mix_post_g", "w_in", "sgu_norm_g", "sgu_norm_b", "sgu_w", "sgu_b",
                "conv_w", "conv_b", "conv_norm_g", "conv_norm_b", "pool_w", "pool_scale", "branch_g", "w_out",
                "ffn_pre_g", "ffn_post_g", "ffn_up", "ffn_conv_w", "ffn_conv_b", "ffn_down")
SHARDED_BIG = ("w_in", "w_out", "ffn_up", "ffn_down")
SMALL_PACKED = tuple(n for n in WEIGHT_NAMES if n not in SHARDED_BIG + ("mod_w",))


def _rows8(rows, width=D_MODEL):
    out = [jnp.pad(r.astype(F32), (0, width - r.shape[0]))[None] for r in rows]
    out.append(jnp.zeros((8 - len(rows), width), F32))
    return jnp.concatenate(out, axis=0)


def _as_rows(a, width=D_MODEL):
    flat = a.astype(F32).reshape(-1)
    pad = (-flat.shape[0]) % width
    return jnp.pad(flat, (0, pad)).reshape(-1, width)


def _pad_cols(a, width=D_MODEL):
    return jnp.pad(a.astype(F32), ((0, 0), (0, width - a.shape[1])))


def _pack_rows(arrays):
    rows = jnp.concatenate([_as_rows(a) for a in arrays], axis=0)
    return jnp.pad(rows, ((0, (-rows.shape[0]) % 8), (0, 0)))


def _unpack_rows(packed, shapes):
    out, r = [], 0
    for shape in shapes:
        size = math.prod(shape)
        n_rows = -(-size // D_MODEL)
        out.append(packed[r:r + n_rows].reshape(-1)[:size].reshape(shape))
        r += n_rows
    return out


TILE_VREGS = MIX_TILE // 8
CHUNK_SUBLANES = CHUNK // TILE_VREGS
CHUNKS_PER_TILE = MIX_TILE // CHUNK


def _chunk_axis_to_tile(a, axis):
    shape = a.shape
    a = a.reshape(shape[:axis] + (CHUNK_SUBLANES, TILE_VREGS) + shape[axis + 1:])
    a = jnp.swapaxes(a, axis, axis + 1)
    a = jnp.tile(a, (1,) * (axis + 1) + (CHUNKS_PER_TILE,) + (1,) * (len(shape) - axis - 1))
    return a.reshape(shape[:axis] + (MIX_TILE,) + shape[axis + 1:])


def _layer_consts(l, w, mod_rows, win, wout, conv_w_full):
    modv = _rows8(list(mod_rows[l].reshape(6, D_MODEL)))
    g1024 = _rows8([w["mix_pre_g"][l], w["mix_post_g"][l], w["branch_g"][l], w["ffn_pre_g"][l], w["ffn_post_g"][l]])
    p384 = _rows8([w["sgu_norm_g"][l], w["sgu_norm_b"][l], w["conv_b"][l], w["conv_norm_g"][l], w["conv_norm_b"][l]],
                  SGU_WIDTH)
    cw = jnp.pad(conv_w_full[l], ((0, 32 - CONV_K), (0, 0)))
    sguw = _chunk_axis_to_tile(_chunk_axis_to_tile(w["sgu_w"][l], 2), 1)
    bmat = _chunk_axis_to_tile(jnp.repeat(w["sgu_b"][l].T, HEAD_DIM, axis=1), 0)
    groups = len(POOL_WINDOWS)
    eye = jnp.eye(groups, dtype=F32)
    pwbd = (eye[:, None, :, None] * w["pool_w"][l][:, :, None, :]).reshape(POOL_WIDTH, POOL_WIDTH).astype(BF16)
    psc = _rows8([w["pool_scale"][l]], POOL_WIDTH)
    seg = jnp.arange(SGU_WIDTH) // HEAD_DIM
    segp = jnp.where(seg[:, None] == seg[None, :], 1.0 / HEAD_DIM, 0.0).astype(BF16)
    return modv, g1024, (modv, g1024, p384, cw, sguw, bmat, pwbd, psc, segp, win, wout)


def _small_grad_rows(mix, ffn, loss_pieces=None):
    _, _, _, _, _, mvec, v384, dcw, dsguw, dbmat, dpw, dpsc = mix
    fvec, cgrad = ffn[5], ffn[6]
    dmod = jnp.stack([mvec[2], mvec[3], mvec[1], fvec[2], fvec[3], fvec[1]]).reshape(N_DEV, MOD_SHARD)
    g_rows = jnp.stack([mvec[4], mvec[0], mvec[5], fvec[4], fvec[0]])
    dsgu_b = dbmat[:, ::HEAD_DIM].T.reshape(1, SGU_HEADS * CHUNK)
    groups = len(POOL_WINDOWS)
    gdim = POOL_WIDTH // groups
    dpw4 = dpw.reshape(groups, gdim, groups, gdim)
    dpool = jnp.stack([dpw4[g, :, g, :] for g in range(groups)])
    misc = [dsgu_b[0], dpsc[0]] + ([] if loss_pieces is None else [loss_pieces])
    blocks = [_pad_cols(dmod), _rows8(list(g_rows)), _pad_cols(v384), _rows8(misc), _pad_cols(dcw),
              _pad_cols(cgrad[:, 0:4, :].reshape(4 * N_DEV, FF_SHARD)), _as_rows(dpool), _as_rows(dsguw)]
    return jnp.concatenate(blocks, axis=0)


def _small_grads_from_rows(total):
    per = {n: [] for n in SMALL_PACKED}
    for l in range(N_LAYERS):
        s = total[l]
        per["mod_b"].append(s[ROW_DMOD:ROW_DMOD + N_DEV, :MOD_SHARD].reshape(6 * D_MODEL))
        for j, name in enumerate(("mix_pre_g", "mix_post_g", "branch_g", "ffn_pre_g", "ffn_post_g")):
            per[name].append(s[ROW_G1024 + j])
        for j, name in enumerate(("sgu_norm_g", "sgu_norm_b", "conv_b", "conv_norm_g", "conv_norm_b")):
            per[name].append(s[ROW_V384 + j, :SGU_WIDTH])
        per["sgu_b"].append(s[ROW_SGU_B, :SGU_HEADS * CHUNK].reshape(SGU_HEADS, CHUNK))
        per["pool_scale"].append(s[ROW_POOL_SCALE, :POOL_WIDTH])
        per["conv_w"].append(s[ROW_CONV_W:ROW_CONV_W + CONV_K, :CONV_WIDTH])
        fc = s[ROW_FFN_CONV:ROW_FFN_CONV + 4 * N_DEV, :FF_SHARD].reshape(N_DEV, 4, FF_SHARD)
        per["ffn_conv_w"].append(fc[:, 0:3, :].transpose(1, 0, 2).reshape(FFN_CONV_K, 2 * D_FF))
        per["ffn_conv_b"].append(fc[:, 3, :].reshape(2 * D_FF))
        per["pool_w"].append(s[ROW_POOL_W:ROW_POOL_W + 16].reshape(len(POOL_WINDOWS), HEAD_DIM, HEAD_DIM))
        per["sgu_w"].append(s[ROW_SGU_W:ROW_SGU_W + 96].reshape(SGU_HEADS, CHUNK, CHUNK))
    return {n: jnp.stack(v) for n, v in per.items()}


def kernel(x, c, mod_w, mod_b, mix_pre_g, mix_post_g, w_in, sgu_norm_g, sgu_norm_b, sgu_w, sgu_b, conv_w, conv_b, conv_norm_g, conv_norm_b, pool_w, pool_scale, branch_g, w_out, ffn_pre_g, ffn_post_g, ffn_up, ffn_conv_w, ffn_conv_b, ffn_down, loss_target, m_mod_w, m_mod_b, m_mix_pre_g, m_mix_post_g, m_w_in, m_sgu_norm_g, m_sgu_norm_b, m_sgu_w, m_sgu_b, m_conv_w, m_conv_b, m_conv_norm_g, m_conv_norm_b, m_pool_w, m_pool_scale, m_branch_g, m_w_out, m_ffn_pre_g, m_ffn_post_g, m_ffn_up, m_ffn_conv_w, m_ffn_conv_b, m_ffn_down, v_mod_w, v_mod_b, v_mix_pre_g, v_mix_post_g, v_w_in, v_sgu_norm_g, v_sgu_norm_b, v_sgu_w, v_sgu_b, v_conv_w, v_conv_b, v_conv_norm_g, v_conv_norm_b, v_pool_w, v_pool_scale, v_branch_g, v_w_out, v_ffn_pre_g, v_ffn_post_g, v_ffn_up, v_ffn_conv_w, v_ffn_conv_b, v_ffn_down):
    w = dict(zip(WEIGHT_NAMES, (mod_w, mod_b, mix_pre_g, mix_post_g, w_in, sgu_norm_g, sgu_norm_b, sgu_w, sgu_b, conv_w,
                                conv_b, conv_norm_g, conv_norm_b, pool_w, pool_scale, branch_g, w_out, ffn_pre_g,
                                ffn_post_g, ffn_up, ffn_conv_w, ffn_conv_b, ffn_down)))
    m = dict(zip(WEIGHT_NAMES, (m_mod_w, m_mod_b, m_mix_pre_g, m_mix_post_g, m_w_in, m_sgu_norm_g, m_sgu_norm_b, m_sgu_w,
                                m_sgu_b, m_conv_w, m_conv_b, m_conv_norm_g, m_conv_norm_b, m_pool_w, m_pool_scale,
                                m_branch_g, m_w_out, m_ffn_pre_g, m_ffn_post_g, m_ffn_up, m_ffn_conv_w, m_ffn_conv_b,
                                m_ffn_down)))
    v = dict(zip(WEIGHT_NAMES, (v_mod_w, v_mod_b, v_mix_pre_g, v_mix_post_g, v_w_in, v_sgu_norm_g, v_sgu_norm_b, v_sgu_w,
                                v_sgu_b, v_conv_w, v_conv_b, v_conv_norm_g, v_conv_norm_b, v_pool_w, v_pool_scale,
                                v_branch_g, v_w_out, v_ffn_pre_g, v_ffn_post_g, v_ffn_up, v_ffn_conv_w, v_ffn_conv_b,
                                v_ffn_down)))
    me = _flat(_my_pos())
    xs = x[0]

    transposed = ("w_in", "ffn_up")
    wt = {n: jnp.swapaxes(w[n], 1, 2) if n in transposed else w[n] for n in SHARDED_BIG}
    mt = {n: jnp.swapaxes(m[n], 1, 2) if n in transposed else m[n] for n in SHARDED_BIG}
    vt = {n: jnp.swapaxes(v[n], 1, 2) if n in transposed else v[n] for n in SHARDED_BIG}
    bf16_shards = [[wt[n][l].astype(BF16) for n in SHARDED_BIG] for l in range(N_LAYERS)]

    def mixer_operands(l, win_g, wout_g):
        win = win_g.reshape(IN_WIDTH, D_MODEL)
        return _layer_consts(l, w, mod_rows, win, wout_g.reshape(D_MODEL, D_MODEL), conv_w_full)

    def ffn_operands(l, modv, g1024, wup_g, wdn_g):
        wdn = wdn_g.reshape(FF_PAIRS, FF_SHARD, D_MODEL)
        return modv, g1024, wup_g, wdn, ffn_cw_full[:, l], ffn_conv_b[l].reshape(N_DEV, 1, FF_SHARD)

    w0, w1 = bf16_shards
    c8 = jnp.broadcast_to(c, (8, D_MODEL))
    mod_b8 = jnp.pad(mod_b, ((0, 8 - N_LAYERS), (0, 0)))
    sc_all, mod_rows, win0_g, wout0_g, conv_w_g, ffn_cw_full = _gather_weights(
        c8, mod_w, mod_b8, [("gather2", w0[0]), ("gather2", w0[1]), ("gather", conv_w), ("gather", ffn_conv_w)])
    conv_w_full = conv_w_g.transpose(1, 2, 0, 3).reshape(N_LAYERS, CONV_K, CONV_WIDTH)

    modv0, g0, mix_consts0 = mixer_operands(0, win0_g, wout0_g)
    (x1, z, o, cv), (wup0_g, wdn0_g) = _mixer_fwd(xs, *mix_consts0, name="mixer_fwd_l0", natural_x=True,
                                                  job=[("gather2", w0[2]), ("gather2", w0[3])])
    ffn_consts0 = ffn_operands(0, modv0, g0, wup0_g, wdn0_g)
    (x2, y2, p, u), (win1_g, wout1_g, wdn1_g) = _ffn_fwd(
        x1, *ffn_consts0, name="ffn_fwd_l0", job=[("gather2", w1[0]), ("gather2", w1[1]), ("gather2", w1[3])])
    saved = [(xs, z, o, cv, x1, y2, p, u)]
    modv1, g1, mix_consts1 = mixer_operands(1, win1_g, wout1_g)
    (x1, z, o, cv), (wup1_g,) = _mixer_fwd(x2, *mix_consts1, name="mixer_fwd_l1", job=[("gather2", w1[2])])
    ffn_consts1 = ffn_operands(1, modv1, g1, wup1_g, wdn1_g)
    (dh, y2, p, u, loss_tile), _ = _ffn_fwd(x1, *ffn_consts1, name="ffn_fwd_l1", loss_target=loss_target[0])
    saved.append((x2, z, o, cv, x1, y2, p, u))
    loss_hi = loss_tile[0, 0].astype(BF16).astype(F32)
    loss_mid = (loss_tile[0, 0] - loss_hi).astype(BF16).astype(F32)
    loss_pieces = jnp.stack([loss_hi, loss_mid, loss_tile[0, 0] - loss_hi - loss_mid])

    def ffn_weight_grads(l, ffn):
        dp, a, dy2, h2 = ffn[1:5]
        d_up = _wgrad(dp, h2, f"wgrad_ffn_up_l{l}", tk=WGRAD_TK_FFN)
        d_dn = _wgrad(a, dy2, f"wgrad_ffn_down_l{l}", tk=WGRAD_TK_FFN).reshape(N_DEV, D_FF // N_DEV, D_MODEL)
        return d_up, d_dn

    def mixer_weight_grads(l, mix):
        dz, do, ycat, h1 = mix[1:5]
        d_in = _wgrad(dz[None], h1, f"wgrad_w_in_l{l}").reshape(N_DEV, IN_WIDTH // N_DEV, D_MODEL)
        d_out = _wgrad(ycat, do[None], f"wgrad_w_out_l{l}").reshape(N_DEV, D_MODEL // N_DEV, D_MODEL)
        return d_in, d_out

    x_in, z, o, cv, x1, y2, p, u = saved[1]
    ffn1, _ = _ffn_bwd(dh, x1, y2, p, u, *ffn_consts1[:-1], name="ffn_bwd_l1")
    d_up1, d_dn1 = ffn_weight_grads(1, ffn1)
    mix1, (sib_up1, sib_dn1) = _mixer_bwd(ffn1[0], x_in, o, z, cv, *mix_consts1, name="mixer_bwd_l1",
                                          job=[("scatter_p1", d_up1), ("scatter_p1", d_dn1)])
    chip_up1 = _pair_add(d_up1, sib_up1, 176, "pair_add_ffn_up_l1")
    chip_dn1 = _pair_add(d_dn1, sib_dn1, 176, "pair_add_ffn_down_l1")
    d_in1, d_out1 = mixer_weight_grads(1, mix1)
    small1 = _small_grad_rows(mix1, ffn1).astype(BF16)

    x_in, z, o, cv, x1, y2, p, u = saved[0]
    ffn0, job_out = _ffn_bwd(mix1[0], x1, y2, p, u, *ffn_consts0[:-1], name="ffn_bwd_l0",
                             job=[("scatter", d_in1), ("scatter", d_out1), ("scatter_p2", chip_up1),
                                  ("scatter_p2", chip_dn1)])
    recv1 = job_out
    dp, a, dy2, h2 = ffn0[1:5]
    d_dn0, (small_all1,) = _wgrad(a, dy2, "wgrad_ffn_down_l0", tk=WGRAD_TK_FFN, job=[("gather", small1)])
    d_dn0 = d_dn0.reshape(N_DEV, D_FF // N_DEV, D_MODEL)
    d_up0, (recv_dn0,) = _wgrad(dp, h2, "wgrad_ffn_up_l0", tk=WGRAD_TK_FFN, job=[("scatter", d_dn0)])
    mix0, (recv_up0,) = _mixer_bwd(ffn0[0], x_in, o, z, cv, *mix_consts0, name="mixer_bwd_l0", natural_x=True,
                                   job=[("scatter", d_up0)])
    recv_ffn0 = (recv_up0, recv_dn0)
    grad_x = mix0[0][None]
    dz, do, ycat, h1 = mix0[1:5]
    small0 = _small_grad_rows(mix0, ffn0, loss_pieces).astype(BF16)
    d_out0, (small_all0,) = _wgrad(ycat, do[None], "wgrad_w_out_l0", job=[("gather", small0)])
    d_out0 = d_out0.reshape(N_DEV, D_MODEL // N_DEV, D_MODEL)
    d_in0, (recv_out0,) = _wgrad(dz[None], h1, "wgrad_w_in_l0", job=[("scatter", d_out0)])
    d_in0 = d_in0.reshape(N_DEV, IN_WIDTH // N_DEV, D_MODEL)
    small_total, g_mod_w, (recv_in0,) = _small_sums(sc_all, small_all0, small_all1, [("scatter", d_in0)])
    recv0 = [recv_in0, recv_out0, recv_ffn0[0], recv_ffn0[1]]
    loss = small_total[0, ROW_LOSS, 0] + small_total[0, ROW_LOSS, 1] + small_total[0, ROW_LOSS, 2]

    grads, deltas, new_m, new_v = {}, {}, {}, {}
    for j, (name, chunk) in enumerate((("w_in", 224), ("w_out", 128), ("ffn_up", 176), ("ffn_down", 176))):
        outs = _adam_sharded(recv0[j], recv1[j], wt[name], mt[name], vt[name], chunk, "adam_" + name)
        if name in transposed:
            outs = [jnp.swapaxes(t, 1, 2) for t in outs]
        grads[name], deltas[name], new_m[name], new_v[name] = outs
    grads["mod_w"], deltas["mod_w"], new_m["mod_w"], new_v["mod_w"] = _adam_dense(
        g_mod_w, mod_w, m_mod_w, v_mod_w, 256, "adam_mod_w")

    small_g = _small_grads_from_rows(small_total)
    small_g["conv_w"] = lax.dynamic_slice_in_dim(small_g["conv_w"], me * conv_w.shape[2], conv_w.shape[2], axis=2)
    small_g["ffn_conv_w"] = lax.dynamic_slice_in_dim(small_g["ffn_conv_w"], me * FF_SHARD, FF_SHARD, axis=2)
    shapes = [w[n].shape for n in SMALL_PACKED]
    packs = [_pack_rows([src[n] for n in SMALL_PACKED])[None] for src in (small_g, w, m, v)]
    _, d, m2, v2 = _adam_dense(*packs, packs[0].shape[1], "adam_small")
    for name, dd, mm, vv in zip(SMALL_PACKED, _unpack_rows(d[0], shapes), _unpack_rows(m2[0], shapes),
                                _unpack_rows(v2[0], shapes)):
        grads[name], deltas[name], new_m[name], new_v[name] = small_g[name], dd, mm, vv

    return (loss, grad_x, *[grads[n] for n in WEIGHT_NAMES], *[deltas[n] for n in WEIGHT_NAMES],
            *[new_m[n] for n in WEIGHT_NAMES], *[new_v[n] for n in WEIGHT_NAMES])
```

```python
import math

import jax
import jax.numpy as jnp
from jax import lax
from jax.experimental import pallas as pl
from jax.experimental.pallas import tpu as pltpu

F32 = jnp.float32
BF16 = jnp.bfloat16

D_MODEL = 1024
N_DEV = 8
SGU_WIDTH = 384
CONV_WIDTH = 384
POOL_WIDTH = 256
HEAD_DIM = 64
SGU_HEADS = 6
CHUNK = 128
CONV_K = 31
POOL_WINDOWS = (2, 4, 8, 16)
IN_WIDTH = 1792
D_FF = 2816
FF_SHARD = 2 * D_FF // N_DEV
FF_PAIRS = N_DEV // 2
FFN_CONV_K = 3
EPS = 1e-6
GELU_C0 = math.sqrt(2.0 / math.pi)
GELU_C1 = 0.044715

ADAM_LR = 0.001
ADAM_B1 = 0.9
ADAM_B2 = 0.999
ADAM_EPS = 1e-08
ADAM_WD = 0.01
ADAM_STEP = 10

VMEM_LIMIT_BYTES = 56 * 1024 * 1024
TILE = 256
MIX_TILE = TILE
FFN_TILE = TILE
FFN_HALO = 8 * (FFN_CONV_K - 1)
POOL_HALO = 8 * POOL_WINDOWS[-1]
WGRAD_TK = 2048
WGRAD_TK_FFN = 4096


def _cparams(n_axes):
    return pltpu.CompilerParams(dimension_semantics=("arbitrary",) * n_axes, vmem_limit_bytes=VMEM_LIMIT_BYTES)


def _whole(shape):
    nd = len(shape)
    return pl.BlockSpec(shape, lambda *_: (0,) * nd, pipeline_mode=pl.Buffered(1))


def _dot(a, b):
    return jnp.dot(a, b, preferred_element_type=F32)


def _dot_nt(a, b):
    return lax.dot_general(a, b, (((1,), (1,)), ((), ())), preferred_element_type=F32)


def _dot_tn(a, b):
    return lax.dot_general(a, b, (((0,), (0,)), ((), ())), preferred_element_type=F32)


def _gelu(x):
    t = jnp.tanh(GELU_C0 * (x + GELU_C1 * x * x * x))
    return 0.5 * x * (1.0 + t), t


def _gelu_grad(x, t):
    return 0.5 * (1.0 + t) + 0.5 * x * (1.0 - t * t) * (GELU_C0 * (1.0 + 3.0 * GELU_C1 * x * x))


def _rowmean(x):
    return jnp.mean(x, axis=-1, keepdims=True)


def _colsum(x):
    return jnp.sum(x, axis=0, keepdims=True)


def _rms_fwd(x):
    r = lax.rsqrt(_rowmean(x * x) + EPS)
    return x * r, r


def _rms_bwd(dxhat, xhat, r):
    return r * (dxhat - xhat * _rowmean(dxhat * xhat))


N_PEERS = N_DEV - 1
ANY = pl.BlockSpec(memory_space=pl.ANY)
VMEM = pl.BlockSpec(memory_space=pltpu.VMEM)


def _my_pos():
    return lax.axis_index("x"), lax.axis_index("y"), lax.axis_index("c")


def _peer(pos, k):
    x, y, c = pos
    return (1 - x if k & 4 else x, 1 - y if k & 2 else y, 1 - c if k & 1 else c)


def _flat(pos):
    return 4 * pos[0] + 2 * pos[1] + pos[2]


def _remote_copy(src, dst, send_sem, recv_sem, peer):
    return pltpu.make_async_remote_copy(src_ref=src, dst_ref=dst, send_sem=send_sem, recv_sem=recv_sem,
                                        device_id=peer, device_id_type=pl.DeviceIdType.MESH)


N_CHIPS = N_DEV // 2
SIBLING = 1
SAME_CORE_PEERS = (2, 4, 6)


def _exchange_out_shapes(job):
    def shape(kind, a):
        if kind in ("gather", "gather2"):
            return (N_DEV,) + a.shape
        if kind == "scatter_p1":
            return (N_CHIPS,) + a.shape[1:]
        return a.shape
    return [jax.ShapeDtypeStruct(shape(kind, a), a.dtype) for kind, a in job]


def _exchange_sems(n):
    return [pltpu.SemaphoreType.DMA((n, N_PEERS)), pltpu.SemaphoreType.DMA((n, N_PEERS)), pltpu.SemaphoreType.DMA((n,))]


def _exchange_copies(kinds, src_refs, dst_refs, send_sems, recv_sems, local_sems, phase):
    pos = _my_pos()
    me = _flat(pos)
    chip, core = 2 * pos[0] + pos[1], pos[2]
    copies = []

    def remote(a, src, dst, k, sem=None):
        sem = k - 1 if sem is None else sem
        copies.append(_remote_copy(src, dst, send_sems.at[a, sem], recv_sems.at[a, sem], _peer(pos, k)))

    for a, kind in enumerate(kinds):
        src, dst = src_refs[a], dst_refs[a]
        if phase == 1:
            if kind == "gather2":
                for k in SAME_CORE_PEERS:
                    remote(a, dst.at[me ^ k], dst.at[me ^ k], SIBLING, sem=k)
        elif kind in ("gather", "gather2"):
            copies.append(pltpu.make_async_copy(src, dst.at[me], local_sems.at[a]))
            for k in (range(1, N_DEV) if kind == "gather" else (SIBLING,) + SAME_CORE_PEERS):
                remote(a, src, dst.at[me], k)
        elif kind == "scatter":
            copies.append(pltpu.make_async_copy(src.at[me], dst.at[me], local_sems.at[a]))
            for k in range(1, N_DEV):
                remote(a, src.at[me ^ k], dst.at[me], k)
        elif kind == "scatter_p1":
            for q in range(N_CHIPS):
                remote(a, src.at[2 * q + 1 - core], dst.at[q], SIBLING, sem=q)
        elif kind == "scatter_p2":
            copies.append(pltpu.make_async_copy(src.at[chip], dst.at[chip], local_sems.at[a]))
            for k in SAME_CORE_PEERS:
                remote(a, src.at[chip ^ (k >> 1)], dst.at[chip], k)
    return copies


def _pallas_call_with_exchange(body, *, grid, in_specs, out_specs, out_shape, scratch_shapes, operands, name, job):
    params = _cparams(len(grid))
    if not job:
        outs = pl.pallas_call(body, grid=grid, in_specs=in_specs, out_specs=out_specs, out_shape=out_shape,
                              scratch_shapes=scratch_shapes, compiler_params=params, name=name)(*operands)
        return outs, []
    kinds = [kind for kind, _ in job]
    relayed = [kind if kind == "gather2" else None for kind in kinds]
    unrelayed = [None if kind == "gather2" else kind for kind in kinds]
    n, n_in, n_out, n_scr = len(job), len(in_specs), len(out_specs), len(scratch_shapes)
    n_steps = math.prod(grid)
    relay_step = max(n_steps - 2, 0)

    def wrapped(*refs):
        ins, jin = refs[:n_in], refs[n_in:n_in + n]
        outs, jout = refs[n_in + n:n_in + n + n_out], refs[n_in + n + n_out:n_in + 2 * n + n_out]
        scr = refs[n_in + 2 * n + n_out:n_in + 2 * n + n_out + n_scr]
        sems = refs[n_in + 2 * n + n_out + n_scr:]
        step = pl.program_id(0)
        for d in range(1, len(grid)):
            step = step * grid[d] + pl.program_id(d)

        def copies(which, phase):
            return _exchange_copies(which, jin, jout, *sems, phase=phase)

        @pl.when(step == 0)
        def _():
            for cp in copies(kinds, 0):
                cp.start()

        body(*ins, *outs, *scr)

        @pl.when(step == relay_step)
        def _():
            for cp in copies(relayed, 0):
                cp.wait()
            for cp in copies(relayed, 1):
                cp.start()

        @pl.when(step == n_steps - 1)
        def _():
            for cp in copies(unrelayed, 0) + copies(relayed, 1):
                cp.wait()

    res = pl.pallas_call(
        wrapped, grid=grid,
        in_specs=list(in_specs) + [ANY] * n,
        out_specs=list(out_specs) + [ANY] * n,
        out_shape=list(out_shape) + _exchange_out_shapes(job),
        scratch_shapes=list(scratch_shapes) + _exchange_sems(n),
        compiler_params=params, name=name,
    )(*operands, *[a for _, a in job])
    return res[:n_out], res[n_out:]


def _seg_mean(x, segp):
    hi = x.astype(BF16)
    lo = (x - hi.astype(F32)).astype(BF16)
    return _dot(hi, segp) + _dot(lo, segp)


def _rot_rows(x, shift):
    m, c = x.shape
    return pltpu.roll(x.reshape(m // 8, 8, c), shift, 1).reshape(m, c)


def _sublane_is(shape, s):
    return lax.broadcasted_iota(jnp.int32, shape, 0) % 8 == s


def _causal_tail(tail, prev_rot):
    rot = _rot_rows(tail, 1)
    return jnp.where(_sublane_is(tail.shape, 0), prev_rot, rot), rot


def _lookahead_head(head, next_rot):
    rot = _rot_rows(head, 7)
    return jnp.where(_sublane_is(head.shape, 7), next_rot, rot), rot


def _tile_token_index(t, tile_idx):
    r = lax.broadcasted_iota(jnp.int32, (t, 1), 0)
    return tile_idx * t + (r % 8) * (t // 8) + r // 8


def _interleave(x):
    t, c = x.shape
    return jnp.swapaxes(x.reshape(8, t // 8, c), 0, 1).reshape(t, c)


def _deinterleave(x):
    t, c = x.shape
    return jnp.swapaxes(x.reshape(t // 8, 8, c), 0, 1).reshape(t, c)


def _ffn_fwd(x1, modv, g1024, wup, wdn, cw, cb, name="ffn_fwd", job=None, loss_target=None):
    s_len = x1.shape[0]
    t = FFN_TILE
    n_tiles = s_len // t
    with_loss = loss_target is not None

    def body(x1_ref, *rest):
        if with_loss:
            tgt_ref, rest = rest[0], rest[1:]
            loss_ref, rest = rest[10], rest[:10] + rest[11:]
        mod_ref, g_ref, wup_ref, wdn_ref, cw_ref, cb_ref, x2_ref, y2_ref, p_ref, u_ref, ext_ref, carry_ref = rest
        i = pl.program_id(0)

        @pl.when(i == 0)
        def _():
            carry_ref[...] = jnp.zeros_like(carry_ref)
            if with_loss:
                loss_ref[...] = jnp.zeros_like(loss_ref)

        x1v = x1_ref[...]
        pre_g, post_g = g_ref[3:4, :], g_ref[4:5, :]
        sh2, sc2, g2 = mod_ref[3:4, :], mod_ref[4:5, :], mod_ref[5:6, :]
        xhat, _ = _rms_fwd(x1v)
        h2b = (xhat * pre_g * (1.0 + sc2) + sh2).astype(BF16)

        def conv_shard(s):
            p = _dot_nt(h2b, wup_ref[s])
            p_ref[s] = p.astype(BF16)
            ext_ref[0:FFN_HALO, :], carry_ref[s] = _causal_tail(p[t - FFN_HALO:t, :], carry_ref[s])
            ext_ref[FFN_HALO:FFN_HALO + t, :] = p
            w = cw_ref[s]
            u = w[0:1, :] * ext_ref[0:t, :] + w[1:2, :] * ext_ref[8:8 + t, :] + w[2:3, :] * p + cb_ref[s]
            u_ref[s] = u.astype(BF16)
            return u

        y2 = jnp.zeros((t, D_MODEL), F32)
        for j in range(FF_PAIRS):
            ug = conv_shard(j)
            uv = conv_shard(j + FF_PAIRS)
            ge, _ = _gelu(ug)
            y2 = y2 + _dot((ge * uv).astype(BF16), wdn_ref[j])
        y2_ref[...] = y2
        yhat, _ = _rms_fwd(y2)
        x2 = x1v + g2 * (yhat * post_g)
        if with_loss:
            diff = x2 - _interleave(tgt_ref[...])
            x2_ref[...] = diff * (1.0 / D_MODEL)
            loss_ref[...] += (0.5 / D_MODEL) * jnp.sum(diff * diff)
        else:
            x2_ref[...] = x2

    tile = pl.BlockSpec((t, D_MODEL), lambda i: (i, 0))
    consts = (modv, g1024, wup, wdn, cw, cb)
    shards = pl.BlockSpec((N_DEV, t, FF_SHARD), lambda i: (0, i, 0))
    out_specs = [tile, tile, shards, shards]
    out_shape = [jax.ShapeDtypeStruct((s_len, D_MODEL), F32), jax.ShapeDtypeStruct((s_len, D_MODEL), F32),
                 jax.ShapeDtypeStruct((N_DEV, s_len, FF_SHARD), BF16),
                 jax.ShapeDtypeStruct((N_DEV, s_len, FF_SHARD), BF16)]
    if with_loss:
        out_specs.append(pl.BlockSpec((8, 128), lambda i: (0, 0)))
        out_shape.append(jax.ShapeDtypeStruct((8, 128), F32))
    return _pallas_call_with_exchange(
        body,
        grid=(n_tiles,),
        in_specs=[tile] * (2 if with_loss else 1) + [_whole(c.shape) for c in consts],
        out_specs=out_specs,
        out_shape=out_shape,
        scratch_shapes=[pltpu.VMEM((FFN_HALO + t, FF_SHARD), F32), pltpu.VMEM((N_DEV, FFN_HALO, FF_SHARD), F32)],
        operands=(x1,) + ((loss_target,) if with_loss else ()) + consts,
        name=name, job=job)


def _ffn_bwd(dx2, x1, y2, p, u, modv, g1024, wup, wdn, cw, name="ffn_bwd", job=None):
    s_len = x1.shape[0]
    t = FFN_TILE
    n_tiles = s_len // t
    hb = FFN_HALO

    def body(dx2_ref, x1_ref, y2_ref, p_ref, ph_ref, u_ref, mod_ref, g_ref, wup_ref, wdn_ref, cw_ref,
             dx1_ref, dp_ref, a_ref, dy2_ref, h2_ref, vec_ref, cgrad_ref, ext_ref, dext_ref, dcarry_ref):
        i = pl.program_id(0)
        tile_idx = n_tiles - 1 - i

        @pl.when(i == 0)
        def _():
            vec_ref[...] = jnp.zeros_like(vec_ref)
            cgrad_ref[...] = jnp.zeros_like(cgrad_ref)
            dcarry_ref[...] = jnp.zeros_like(dcarry_ref)

        dx2v, x1v, y2v = dx2_ref[...], x1_ref[...], y2_ref[...]
        pre_g, post_g = g_ref[3:4, :], g_ref[4:5, :]
        sh2, sc2, g2 = mod_ref[3:4, :], mod_ref[4:5, :], mod_ref[5:6, :]

        yhat, ry = _rms_fwd(y2v)
        vec_ref[1:2, :] += _colsum(dx2v * (yhat * post_g))
        dyn = dx2v * g2
        vec_ref[0:1, :] += _colsum(dyn * yhat)
        dy2b = _rms_bwd(dyn * post_g, yhat, ry).astype(BF16)
        dy2_ref[...] = dy2b

        xhat, rx = _rms_fwd(x1v)
        xn = xhat * pre_g
        h2_ref[...] = (xn * (1.0 + sc2) + sh2).astype(BF16)

        not_first = (tile_idx > 0).astype(F32)

        def recompute(s, slot):
            pf = p_ref[s].astype(F32)
            prev_rot = _rot_rows(ph_ref[s].astype(F32), 1) * not_first
            ext_ref[slot, 0:hb, :], _ = _causal_tail(pf[t - hb:t, :], prev_rot)
            ext_ref[slot, hb:hb + t, :] = pf
            return u_ref[s].astype(F32)

        def conv_bwd(s, slot, du):
            w = cw_ref[s]
            cgrad_ref[s, 0:1, :] += _colsum(du * ext_ref[slot, 0:t, :])
            cgrad_ref[s, 1:2, :] += _colsum(du * ext_ref[slot, 8:8 + t, :])
            cgrad_ref[s, 2:3, :] += _colsum(du * ext_ref[slot, 16:16 + t, :])
            cgrad_ref[s, 3:4, :] += _colsum(du)
            dext_ref[0:t, :] = du
            dext_ref[t:t + hb, :], dcarry_ref[s] = _lookahead_head(du[0:hb, :], dcarry_ref[s])
            dp = w[2:3, :] * du + w[1:2, :] * dext_ref[8:8 + t, :] + w[0:1, :] * dext_ref[16:16 + t, :]
            dpb = dp.astype(BF16)
            dp_ref[s] = dpb
            return _dot(dpb, wup_ref[s])

        dh2 = jnp.zeros((t, D_MODEL), F32)
        for j in range(FF_PAIRS):
            ug = recompute(j, 0)
            uv = recompute(j + FF_PAIRS, 1)
            ge, th = _gelu(ug)
            a_ref[j] = (ge * uv).astype(BF16)
            da = _dot_nt(dy2b, wdn_ref[j])
            dh2 = dh2 + conv_bwd(j, 0, da * uv * _gelu_grad(ug, th))
            dh2 = dh2 + conv_bwd(j + FF_PAIRS, 1, da * ge)

        vec_ref[2:3, :] += _colsum(dh2)
        vec_ref[3:4, :] += _colsum(dh2 * xn)
        dxn = dh2 * (1.0 + sc2)
        vec_ref[4:5, :] += _colsum(dxn * xhat)
        dx1_ref[...] = dx2v + _rms_bwd(dxn * pre_g, xhat, rx)

    rev = lambda i: (n_tiles - 1 - i, 0)
    tile = pl.BlockSpec((t, D_MODEL), rev)
    halo_idx = lambda i: (0, jnp.maximum((n_tiles - 1 - i) * (t // hb) - 1, 0), 0)
    return _pallas_call_with_exchange(
        body,
        grid=(n_tiles,),
        in_specs=[tile, tile, tile,
                  pl.BlockSpec((N_DEV, t, FF_SHARD), lambda i: (0, n_tiles - 1 - i, 0)),
                  pl.BlockSpec((N_DEV, hb, FF_SHARD), halo_idx),
                  pl.BlockSpec((N_DEV, t, FF_SHARD), lambda i: (0, n_tiles - 1 - i, 0)),
                  _whole(modv.shape), _whole(g1024.shape), _whole(wup.shape), _whole(wdn.shape),
                  _whole(cw.shape)],
        out_specs=[tile,
                   pl.BlockSpec((N_DEV, t, FF_SHARD), lambda i: (0, n_tiles - 1 - i, 0)),
                   pl.BlockSpec((FF_PAIRS, t, FF_SHARD), lambda i: (0, n_tiles - 1 - i, 0)),
                   tile, tile,
                   pl.BlockSpec((8, D_MODEL), lambda i: (0, 0)),
                   pl.BlockSpec((N_DEV, 8, FF_SHARD), lambda i: (0, 0, 0))],
        out_shape=[jax.ShapeDtypeStruct((s_len, D_MODEL), F32),
                   jax.ShapeDtypeStruct((N_DEV, s_len, FF_SHARD), BF16),
                   jax.ShapeDtypeStruct((FF_PAIRS, s_len, FF_SHARD), BF16),
                   jax.ShapeDtypeStruct((s_len, D_MODEL), BF16),
                   jax.ShapeDtypeStruct((s_len, D_MODEL), BF16),
                   jax.ShapeDtypeStruct((8, D_MODEL), F32),
                   jax.ShapeDtypeStruct((N_DEV, 8, FF_SHARD), F32)],
        scratch_shapes=[pltpu.VMEM((2, hb + t, FF_SHARD), F32), pltpu.VMEM((t + hb, FF_SHARD), F32),
                        pltpu.VMEM((N_DEV, hb, FF_SHARD), F32)],
        operands=(dx2, x1, y2, p, p, u, modv, g1024, wup, wdn, cw),
        name=name, job=job)


def _wgrad(a, b, name, tk=WGRAD_TK, job=None):
    a_grouped, b_grouped = a.ndim == 3, b.ndim == 3
    groups = a.shape[0] if a_grouped else b.shape[0]
    s_len, m, n = a.shape[-2], a.shape[-1], b.shape[-1]
    tk = min(tk, s_len)
    n_k = s_len // tk

    def body(a_ref, b_ref, o_ref, acc_ref):
        k = pl.program_id(1)
        av = a_ref[0] if a_grouped else a_ref[...]
        bv = b_ref[0] if b_grouped else b_ref[...]
        part = _dot_tn(av, bv)
        if n_k == 1:
            o_ref[0] = part.astype(BF16)
            return

        @pl.when(k == 0)
        def _():
            acc_ref[...] = part

        @pl.when(jnp.logical_and(k > 0, k < n_k - 1))
        def _():
            acc_ref[...] += part

        @pl.when(k == n_k - 1)
        def _():
            o_ref[0] = (acc_ref[...] + part).astype(BF16)

    a_spec = pl.BlockSpec((1, tk, m), lambda g, k: (g, k, 0)) if a_grouped else pl.BlockSpec((tk, m), lambda g, k: (k, 0))
    b_spec = pl.BlockSpec((1, tk, n), lambda g, k: (g, k, 0)) if b_grouped else pl.BlockSpec((tk, n), lambda g, k: (k, 0))
    (out,), exchanged = _pallas_call_with_exchange(
        body,
        grid=(groups, n_k),
        in_specs=[a_spec, b_spec],
        out_specs=[pl.BlockSpec((1, m, n), lambda g, k: (g, 0, 0))],
        out_shape=[jax.ShapeDtypeStruct((groups, m, n), BF16)],
        scratch_shapes=[pltpu.VMEM((m, n), F32)],
        operands=(a, b),
        name=name, job=job)
    return (out, exchanged) if job else out


def _lane(shape):
    return lax.broadcasted_iota(jnp.int32, shape, 1)


def _by_pool_group(shape, vals):
    lane = _lane(shape)
    return jnp.where(lane < 64, vals[0], jnp.where(lane < 128, vals[1], jnp.where(lane < 192, vals[2], vals[3])))


def _pool_inv_counts(t, tile_idx):
    pos1 = _tile_token_index(t, tile_idx) + 1
    return [1.0 / jnp.minimum(pos1, w).astype(F32) for w in POOL_WINDOWS]


def _sgu_keep_mask(t):
    tok_r = _tile_token_index(t, 0)
    c = lax.broadcasted_iota(jnp.int32, (1, t), 1)
    tok_c = (c % 8) * (t // 8) + c // 8
    return jnp.logical_and(tok_r // CHUNK == tok_c // CHUNK, tok_r >= tok_c)


def _masked_sgu_w(sguw_ref):
    keep = _sgu_keep_mask(sguw_ref.shape[1])
    return [jnp.where(keep, sguw_ref[h], 0.0).astype(BF16) for h in range(SGU_HEADS)]


def _branches_fwd(z, tile_idx, p384_ref, cw_ref, wm, bmat_ref, pwbd_ref, psc_ref, segp_ref, g_ref, hext_ref, zext_ref,
                  h_prev_rot, z_prev_rot, conv_saved=None):
    t = z.shape[0]
    segp = segp_ref[...]
    r = {}
    u, _ = _gelu(z[:, 0:SGU_WIDTH])
    vraw, _ = _gelu(z[:, SGU_WIDTH:2 * SGU_WIDTH])
    xc = vraw - _seg_mean(vraw, segp)
    rstd_v = lax.rsqrt(_seg_mean(xc * xc, segp) + EPS)
    xh_v = xc * rstd_v
    vnb = (xh_v * p384_ref[0:1, :] + p384_ref[1:2, :]).astype(BF16)
    first_head = _lane((t, 128)) < HEAD_DIM
    f_pairs = []
    for pr in range(SGU_HEADS // 2):
        vp = vnb[:, pr * 128:(pr + 1) * 128]
        f_pairs.append(jnp.where(first_head, _dot(wm[2 * pr], vp), _dot(wm[2 * pr + 1], vp)))
    f = jnp.concatenate(f_pairs, axis=1) + bmat_ref[...]
    ya = u * f
    r.update(u=u, xh_v=xh_v, rstd_v=rstd_v, vnb=vnb, f=f)
    o_b = 2 * SGU_WIDTH
    a_in = z[:, o_b:o_b + CONV_WIDTH]
    sig_g = jax.nn.sigmoid(z[:, o_b + CONV_WIDTH:o_b + 2 * CONV_WIDTH])
    hh = a_in * sig_g
    hext_ref[0:t, :], r["h_rot"] = _causal_tail(hh, h_prev_rot)
    hext_ref[t:2 * t, :] = hh
    if conv_saved is None:
        conv = jnp.zeros((t, CONV_WIDTH), F32) + p384_ref[2:3, :]
        for k in range(CONV_K):
            conv = conv + cw_ref[k:k + 1, :] * hext_ref[pl.ds(t - 8 * (CONV_K - 1 - k), t), :]
        r["conv"] = conv
    else:
        conv = conv_saved
    cc = conv - _rowmean(conv)
    rstd_c = lax.rsqrt(_rowmean(cc * cc) + EPS)
    xh_c = cc * rstd_c
    cn = xh_c * p384_ref[3:4, :] + p384_ref[4:5, :]
    sig_c = jax.nn.sigmoid(cn)
    yb = cn * sig_c
    r.update(a_in=a_in, sig_g=sig_g, xh_c=xh_c, rstd_c=rstd_c, cn=cn, sig_c=sig_c)
    o_c = o_b + 2 * CONV_WIDTH
    zc = z[:, o_c:o_c + POOL_WIDTH]
    zext_ref[0:POOL_HALO, :], r["z_rot"] = _causal_tail(zc[t - POOL_HALO:t, :], z_prev_rot)
    zext_ref[POOL_HALO:POOL_HALO + t, :] = zc
    sums, acc = [], zc
    for j in range(1, POOL_WINDOWS[-1]):
        acc = acc + zext_ref[pl.ds(POOL_HALO - 8 * j, t), :]
        if j + 1 in POOL_WINDOWS:
            sums.append(acc)
    inv = _pool_inv_counts(t, tile_idx)
    dpool = _by_pool_group((t, POOL_WIDTH), [s * iv for s, iv in zip(sums, inv)]) - zc
    ycp = _dot(dpool.astype(BF16), pwbd_ref[...])
    yc = ycp * psc_ref[0:1, :]
    r.update(dpool=dpool, ycp=ycp)
    yha, ra = _rms_fwd(ya)
    yhb, rb = _rms_fwd(yb)
    yhc, rc = _rms_fwd(yc)
    bg = g_ref[2:3, :]
    ycat = jnp.concatenate([yha * bg[:, 0:384], yhb * bg[:, 384:768], yhc * bg[:, 768:1024]], axis=1)
    r.update(yha=yha, ra=ra, yhb=yhb, rb=rb, yhc=yhc, rc=rc, ycat=ycat)
    return r


def _mixer_fwd(x, modv, g1024, p384, cw, sguw, bmat, pwbd, psc, segp, win, wout, name="mixer_fwd", job=None,
               natural_x=False):
    s_len = x.shape[0]
    t = MIX_TILE

    def body(x_ref, mod_ref, g_ref, p384_ref, cw_ref, sguw_ref, bmat_ref, pwbd_ref, psc_ref, segp_ref, win_ref, wout_ref,
             x1_ref, z_ref, o_ref, conv_ref, hext_ref, zext_ref, hrot_ref, zrot_ref, wm_ref):
        i = pl.program_id(0)

        @pl.when(i == 0)
        def _():
            hrot_ref[...] = jnp.zeros_like(hrot_ref)
            zrot_ref[...] = jnp.zeros_like(zrot_ref)
            for h, wmh in enumerate(_masked_sgu_w(sguw_ref)):
                wm_ref[h] = wmh

        xv = _interleave(x_ref[...]) if natural_x else x_ref[...]
        sh1, sc1, g1 = mod_ref[0:1, :], mod_ref[1:2, :], mod_ref[2:3, :]
        xhat, _ = _rms_fwd(xv)
        h1 = xhat * g_ref[0:1, :] * (1.0 + sc1) + sh1
        z = _dot_nt(h1.astype(BF16), win_ref[...])
        z_ref[...] = z
        r = _branches_fwd(z, i, p384_ref, cw_ref, [wm_ref[h] for h in range(SGU_HEADS)], bmat_ref, pwbd_ref, psc_ref,
                          segp_ref, g_ref, hext_ref, zext_ref, hrot_ref[...], zrot_ref[...])
        hrot_ref[...] = r["h_rot"]
        zrot_ref[...] = r["z_rot"]
        conv_ref[...] = r["conv"]
        o = _dot(r["ycat"].astype(BF16), wout_ref[...])
        o_ref[...] = o
        ohat, _ = _rms_fwd(o)
        x1_ref[...] = xv + g1 * (ohat * g_ref[1:2, :])

    tile = pl.BlockSpec((t, D_MODEL), lambda i: (i, 0))
    consts = (modv, g1024, p384, cw, sguw, bmat, pwbd, psc, segp, win, wout)
    return _pallas_call_with_exchange(
        body,
        grid=(s_len // t,),
        in_specs=[tile] + [_whole(c.shape) for c in consts],
        out_specs=[tile, pl.BlockSpec((t, IN_WIDTH), lambda i: (i, 0)), tile,
                   pl.BlockSpec((t, CONV_WIDTH), lambda i: (i, 0))],
        out_shape=[jax.ShapeDtypeStruct((s_len, D_MODEL), F32), jax.ShapeDtypeStruct((s_len, IN_WIDTH), F32),
                   jax.ShapeDtypeStruct((s_len, D_MODEL), F32), jax.ShapeDtypeStruct((s_len, CONV_WIDTH), F32)],
        scratch_shapes=[pltpu.VMEM((2 * t, CONV_WIDTH), F32), pltpu.VMEM((POOL_HALO + t, POOL_WIDTH), F32),
                        pltpu.VMEM((t, CONV_WIDTH), F32), pltpu.VMEM((POOL_HALO, POOL_WIDTH), F32),
                        pltpu.VMEM((SGU_HEADS, t, t), BF16)],
        operands=(x, *consts),
        name=name, job=job)


def _mixer_bwd(dx1, x, o, z, conv, modv, g1024, p384, cw, sguw, bmat, pwbd, psc, segp, win, wout, name="mixer_bwd",
               job=None, natural_x=False):
    s_len = x.shape[0]
    t = MIX_TILE
    n_tiles = s_len // t

    def body(dx1_ref, x_ref, o_ref, z_ref, zh_ref, conv_ref, mod_ref, g_ref, p384_ref, cw_ref, sguw_ref, bmat_ref, pwbd_ref,
             psc_ref, segp_ref, win_ref, wout_ref,
             dx_ref, dz_ref, do_ref, ycat_ref, h1_ref, vec_ref, v384_ref, dcw_ref, dsguw_out_ref, dbmat_out_ref, dpw_ref,
             dpsc_ref, hext_ref, zext_ref, gext_ref, qext_ref, grot_ref, qrot_ref, wm_ref, dsguw_ref, dbmat_ref):
        i = pl.program_id(0)
        tile_idx = n_tiles - 1 - i

        @pl.when(i == 0)
        def _():
            for ref in (vec_ref, v384_ref, dcw_ref, dsguw_ref, dbmat_ref, dpw_ref, dpsc_ref, grot_ref, qrot_ref):
                ref[...] = jnp.zeros_like(ref)
            for h, wmh in enumerate(_masked_sgu_w(sguw_ref)):
                wm_ref[h] = wmh

        dx1v, ov, z = dx1_ref[...], o_ref[...], z_ref[...]
        xv = _interleave(x_ref[...]) if natural_x else x_ref[...]
        sh1, sc1, g1 = mod_ref[0:1, :], mod_ref[1:2, :], mod_ref[2:3, :]
        pre_g, post_g, bg = g_ref[0:1, :], g_ref[1:2, :], g_ref[2:3, :]
        segp = segp_ref[...]

        ohat, ro = _rms_fwd(ov)
        vec_ref[1:2, :] += _colsum(dx1v * (ohat * post_g))
        don = dx1v * g1
        vec_ref[0:1, :] += _colsum(don * ohat)
        dob = _rms_bwd(don * post_g, ohat, ro).astype(BF16)
        do_ref[...] = dob
        dycat = _dot_nt(dob, wout_ref[...])

        not_first = (tile_idx > 0).astype(F32)
        o_b = 2 * SGU_WIDTH
        o_c = o_b + 2 * CONV_WIDTH
        h_prev = zh_ref[:, o_b:o_b + CONV_WIDTH] * jax.nn.sigmoid(zh_ref[:, o_b + CONV_WIDTH:o_c])
        h_prev_rot = _rot_rows(h_prev, 1) * not_first
        z_prev_rot = _rot_rows(zh_ref[t - POOL_HALO:t, o_c:o_c + POOL_WIDTH], 1) * not_first
        wm = [wm_ref[h] for h in range(SGU_HEADS)]
        r = _branches_fwd(z, tile_idx, p384_ref, cw_ref, wm, bmat_ref, pwbd_ref, psc_ref, segp_ref, g_ref,
                          hext_ref, zext_ref, h_prev_rot, z_prev_rot, conv_saved=conv_ref[...])
        ycat_ref[...] = r["ycat"].astype(BF16)

        def branch_norm_bwd(dyn, yhat, rr, gain):
            return _colsum(dyn * yhat), _rms_bwd(dyn * gain, yhat, rr)

        dga, dya = branch_norm_bwd(dycat[:, 0:384], r["yha"], r["ra"], bg[:, 0:384])
        dgb, dyb = branch_norm_bwd(dycat[:, 384:768], r["yhb"], r["rb"], bg[:, 384:768])
        dgc, dyc = branch_norm_bwd(dycat[:, 768:1024], r["yhc"], r["rc"], bg[:, 768:1024])
        vec_ref[5:6, :] += jnp.concatenate([dga, dgb, dgc], axis=1)

        du_act = dya * r["f"]
        df = dya * r["u"]
        first_head = _lane((t, 128)) < HEAD_DIM
        dbmat_ref[...] += df
        dvn_pairs = []
        for pr in range(SGU_HEADS // 2):
            dfp = df[:, pr * 128:(pr + 1) * 128]
            df0 = jnp.where(first_head, dfp, 0.0).astype(BF16)
            df1 = jnp.where(first_head, 0.0, dfp).astype(BF16)
            vp = r["vnb"][:, pr * 128:(pr + 1) * 128]
            dvn_pairs.append(_dot_tn(wm[2 * pr], df0) + _dot_tn(wm[2 * pr + 1], df1))
            dsguw_ref[2 * pr] += _dot_nt(df0, vp)
            dsguw_ref[2 * pr + 1] += _dot_nt(df1, vp)
        dvn = jnp.concatenate(dvn_pairs, axis=1)
        v384_ref[0:1, :] += _colsum(dvn * r["xh_v"])
        v384_ref[1:2, :] += _colsum(dvn)
        dxh = dvn * p384_ref[0:1, :]
        dvraw = r["rstd_v"] * (dxh - _seg_mean(dxh, segp) - r["xh_v"] * _seg_mean(dxh * r["xh_v"], segp))
        zu, zv = z[:, 0:SGU_WIDTH], z[:, SGU_WIDTH:o_b]
        _, tu = _gelu(zu)
        _, tv = _gelu(zv)
        dz_u = du_act * _gelu_grad(zu, tu)
        dz_v = dvraw * _gelu_grad(zv, tv)

        cn, sig_c = r["cn"], r["sig_c"]
        dcn = dyb * (sig_c * (1.0 + cn * (1.0 - sig_c)))
        v384_ref[3:4, :] += _colsum(dcn * r["xh_c"])
        v384_ref[4:5, :] += _colsum(dcn)
        dxc = dcn * p384_ref[3:4, :]
        gconv = r["rstd_c"] * (dxc - _rowmean(dxc) - r["xh_c"] * _rowmean(dxc * r["xh_c"]))
        v384_ref[2:3, :] += _colsum(gconv)
        gext_ref[0:t, :] = gconv
        gext_ref[t:2 * t, :], grot_ref[...] = _lookahead_head(gconv, grot_ref[...])
        dhh = jnp.zeros((t, CONV_WIDTH), F32)
        for k in range(CONV_K):
            shift = CONV_K - 1 - k
            dcw_ref[k:k + 1, :] += _colsum(gconv * hext_ref[pl.ds(t - 8 * shift, t), :])
            dhh = dhh + cw_ref[k:k + 1, :] * gext_ref[pl.ds(8 * shift, t), :]
        sig_g = r["sig_g"]
        dz_a = dhh * sig_g
        dz_g = dhh * r["a_in"] * sig_g * (1.0 - sig_g)

        dpsc_ref[0:1, :] += _colsum(dyc * r["ycp"])
        dycp = (dyc * psc_ref[0:1, :]).astype(BF16)
        dpw_ref[...] += _dot_tn(r["dpool"].astype(BF16), dycp)
        ddp = _dot_nt(dycp, pwbd_ref[...])
        inv = _pool_inv_counts(t, tile_idx)
        q = ddp * _by_pool_group((t, POOL_WIDTH), inv)
        qext_ref[0:t, :] = q
        qext_ref[t:t + POOL_HALO, :], qrot_ref[...] = _lookahead_head(q[0:POOL_HALO, :], qrot_ref[...])
        sums, acc = [], q
        for j in range(1, POOL_WINDOWS[-1]):
            acc = acc + qext_ref[pl.ds(8 * j, t), :]
            if j + 1 in POOL_WINDOWS:
                sums.append(acc)
        dz_c = _by_pool_group((t, POOL_WIDTH), sums) - ddp

        dzb = jnp.concatenate([dz_u, dz_v, dz_a, dz_g, dz_c], axis=1).astype(BF16)
        dz_ref[...] = dzb
        dh1 = _dot(dzb, win_ref[...])

        xhat, rx = _rms_fwd(xv)
        xn = xhat * pre_g
        h1_ref[...] = (xn * (1.0 + sc1) + sh1).astype(BF16)
        vec_ref[2:3, :] += _colsum(dh1)
        vec_ref[3:4, :] += _colsum(dh1 * xn)
        dxn = dh1 * (1.0 + sc1)
        vec_ref[4:5, :] += _colsum(dxn * xhat)
        dx = dx1v + _rms_bwd(dxn * pre_g, xhat, rx)
        dx_ref[...] = _deinterleave(dx) if natural_x else dx

        @pl.when(i == n_tiles - 1)
        def _():
            keep = _sgu_keep_mask(t)
            for h in range(SGU_HEADS):
                rows_natural = _deinterleave(jnp.where(keep, dsguw_ref[h], 0.0))
                natural = _deinterleave(rows_natural.T).T
                dsguw_out_ref[h] = sum(natural[b * CHUNK:(b + 1) * CHUNK, b * CHUNK:(b + 1) * CHUNK]
                                       for b in range(t // CHUNK))
            dbmat = _deinterleave(float(HEAD_DIM) * _seg_mean(dbmat_ref[...], segp))
            dbmat_out_ref[...] = sum(dbmat[b * CHUNK:(b + 1) * CHUNK, :] for b in range(t // CHUNK))

    rev = lambda i: (n_tiles - 1 - i, 0)
    tile = pl.BlockSpec((t, D_MODEL), rev)
    ztile = pl.BlockSpec((t, IN_WIDTH), rev)
    zhalo = pl.BlockSpec((t, IN_WIDTH), lambda i: (jnp.maximum(n_tiles - 2 - i, 0), 0))
    consts = (modv, g1024, p384, cw, sguw, bmat, pwbd, psc, segp, win, wout)
    acc = lambda shape: pl.BlockSpec(shape, lambda i: (0,) * len(shape))
    acc_shapes = [(8, D_MODEL), (8, SGU_WIDTH), (32, CONV_WIDTH), (SGU_HEADS, CHUNK, CHUNK), (CHUNK, SGU_WIDTH),
                  (POOL_WIDTH, POOL_WIDTH), (8, POOL_WIDTH)]
    return _pallas_call_with_exchange(
        body,
        grid=(n_tiles,),
        in_specs=[tile, tile, tile, ztile, zhalo, pl.BlockSpec((t, CONV_WIDTH), rev)] + [_whole(c.shape) for c in consts],
        out_specs=[tile, ztile, tile, tile, tile] + [acc(s) for s in acc_shapes],
        out_shape=[jax.ShapeDtypeStruct((s_len, D_MODEL), F32), jax.ShapeDtypeStruct((s_len, IN_WIDTH), BF16),
                   jax.ShapeDtypeStruct((s_len, D_MODEL), BF16), jax.ShapeDtypeStruct((s_len, D_MODEL), BF16),
                   jax.ShapeDtypeStruct((s_len, D_MODEL), BF16)] + [jax.ShapeDtypeStruct(s, F32) for s in acc_shapes],
        scratch_shapes=[pltpu.VMEM((2 * t, CONV_WIDTH), F32), pltpu.VMEM((POOL_HALO + t, POOL_WIDTH), F32),
                        pltpu.VMEM((2 * t, CONV_WIDTH), F32), pltpu.VMEM((t + POOL_HALO, POOL_WIDTH), F32),
                        pltpu.VMEM((t, CONV_WIDTH), F32), pltpu.VMEM((POOL_HALO, POOL_WIDTH), F32),
                        pltpu.VMEM((SGU_HEADS, t, t), BF16), pltpu.VMEM((SGU_HEADS, t, t), F32),
                        pltpu.VMEM((t, SGU_WIDTH), F32)],
        operands=(dx1, x, o, z, z, conv, *consts),
        name=name, job=job)


MOD_SHARD = 6 * D_MODEL // N_DEV

ROW_DMOD = 0
ROW_G1024 = 8
ROW_V384 = 16
ROW_SGU_B = 24
ROW_POOL_SCALE = 25
ROW_LOSS = 26
ROW_CONV_W = 32
ROW_FFN_CONV = 64
ROW_POOL_W = 96
ROW_SGU_W = 112
ROWS_PER_LAYER = 208
N_LAYERS = 2


def _gather_weights(c8, mod_w, mod_b8, job):
    kinds = [kind for kind, _ in job]
    shards = [a for _, a in job]
    n = len(shards)

    def body(c_ref, modw_ref, modb_ref, *rest):
        shard_refs = rest[:n]
        sc_all_ref, modrows_ref = rest[n], rest[n + 1]
        full_refs = rest[n + 2:2 * n + 2]
        send_buf, mod_recv, w_send, w_recv, w_local, sc_send, sc_recv, mod_send, mod_recv_sem = rest[2 * n + 2:]
        pos = _my_pos()
        me = _flat(pos)
        peers = [_peer(pos, k) for k in range(1, N_DEV)]

        w_copies = _exchange_copies(kinds, shard_refs, full_refs, w_send, w_recv, w_local, phase=0)
        for cp in w_copies:
            cp.start()

        cv = c_ref[...]
        sc_all_ref[me] = cv * jax.nn.sigmoid(cv)
        sc_copies = [_remote_copy(sc_all_ref.at[me], sc_all_ref.at[me], sc_send.at[k], sc_recv.at[k], peers[k])
                     for k in range(N_PEERS)]
        for cp in sc_copies:
            cp.start()
        for cp in sc_copies:
            cp.wait()

        sc = jnp.concatenate([sc_all_ref[j, 0:1, :] for j in range(N_DEV)], axis=0)
        send_buf[...] = jnp.zeros_like(send_buf)
        for l in range(N_LAYERS):
            part = jnp.dot(sc, modw_ref[l], precision=lax.Precision.HIGHEST, preferred_element_type=F32)
            for j in range(N_DEV):
                send_buf[j, l:l + 1, :] = part[j:j + 1, :]
        mod_recv[me] = send_buf[me]
        mod_copies = [_remote_copy(send_buf.at[_flat(peers[k])], mod_recv.at[me], mod_send.at[k], mod_recv_sem.at[k],
                                   peers[k]) for k in range(N_PEERS)]
        for cp in mod_copies:
            cp.start()
        for cp in mod_copies:
            cp.wait()
        modrows_ref[...] = jnp.zeros_like(modrows_ref)
        for l in range(N_LAYERS):
            row = jnp.concatenate([mod_recv[j, l:l + 1, :] for j in range(N_DEV)], axis=1)
            modrows_ref[l:l + 1, :] = row + modb_ref[l:l + 1, :]

        for cp in w_copies:
            cp.wait()
        relays = _exchange_copies(kinds, shard_refs, full_refs, w_send, w_recv, w_local, phase=1)
        for cp in relays:
            cp.start()
        for cp in relays:
            cp.wait()

    out_shape = ([jax.ShapeDtypeStruct((N_DEV, 8, D_MODEL), F32), jax.ShapeDtypeStruct((8, 6 * D_MODEL), F32)]
                 + [jax.ShapeDtypeStruct((N_DEV,) + s.shape, s.dtype) for s in shards])
    return pl.pallas_call(
        body,
        in_specs=[VMEM, VMEM, VMEM] + [ANY] * n,
        out_specs=[VMEM, VMEM] + [ANY] * n,
        out_shape=out_shape,
        scratch_shapes=[pltpu.VMEM((N_DEV, 8, MOD_SHARD), F32), pltpu.VMEM((N_DEV, 8, MOD_SHARD), F32),
                        pltpu.SemaphoreType.DMA((n, N_PEERS)), pltpu.SemaphoreType.DMA((n, N_PEERS)),
                        pltpu.SemaphoreType.DMA((n,)),
                        pltpu.SemaphoreType.DMA((N_PEERS,)), pltpu.SemaphoreType.DMA((N_PEERS,)),
                        pltpu.SemaphoreType.DMA((N_PEERS,)), pltpu.SemaphoreType.DMA((N_PEERS,))],
        compiler_params=pltpu.CompilerParams(vmem_limit_bytes=VMEM_LIMIT_BYTES),
        name="gather_weights",
    )(c8, mod_w, mod_b8, *shards)


def _small_sums(sc_all, small_all0, small_all1, job):
    kinds = [kind for kind, _ in job]
    n = len(job)

    def body(sc_all_ref, small_all0_ref, small_all1_ref, *rest):
        small_sum_ref, gmodw_ref = rest[n], rest[n + 1]
        copies = _exchange_copies(kinds, rest[:n], rest[n + 2:2 * n + 2], *rest[2 * n + 2:], phase=0)
        for cp in copies:
            cp.start()
        me = _flat(_my_pos())
        sc = jnp.concatenate([sc_all_ref[j, 0:1, :] for j in range(N_DEV)], axis=0)
        mine = lax.broadcasted_iota(jnp.int32, (2 * N_DEV, MOD_SHARD), 0) == me
        for l, parts in enumerate((small_all0_ref, small_all1_ref)):
            total = parts[0].astype(F32)
            for j in range(1, N_DEV):
                total = total + parts[j].astype(F32)
            small_sum_ref[l] = total
            dm = jnp.concatenate(
                [jnp.sum(jnp.where(mine, parts[j, ROW_DMOD:ROW_DMOD + 2 * N_DEV, 0:MOD_SHARD].astype(F32), 0.0),
                         axis=0, keepdims=True) for j in range(N_DEV)], axis=0)
            gmodw_ref[l] = lax.dot_general(sc, dm, (((0,), (0,)), ((), ())), precision=lax.Precision.HIGHEST,
                                           preferred_element_type=F32)
        for cp in copies:
            cp.wait()

    res = pl.pallas_call(
        body,
        in_specs=[VMEM, VMEM, VMEM] + [ANY] * n,
        out_specs=[VMEM, VMEM] + [ANY] * n,
        out_shape=[jax.ShapeDtypeStruct((N_LAYERS, ROWS_PER_LAYER, D_MODEL), F32),
                   jax.ShapeDtypeStruct((N_LAYERS, D_MODEL, MOD_SHARD), F32)] + _exchange_out_shapes(job),
        scratch_shapes=_exchange_sems(n),
        compiler_params=pltpu.CompilerParams(vmem_limit_bytes=VMEM_LIMIT_BYTES),
        name="small_sums",
    )(sc_all, small_all0, small_all1, *[a for _, a in job])
    return res[0], res[1], res[2:]


def _adam_update(g, w, m, v):
    m2 = ADAM_B1 * m + (1.0 - ADAM_B1) * g
    v2 = ADAM_B2 * v + (1.0 - ADAM_B2) * (g * g)
    m_hat = m2 / (1.0 - ADAM_B1 ** ADAM_STEP)
    v_hat = v2 / (1.0 - ADAM_B2 ** ADAM_STEP)
    delta = -ADAM_LR * (m_hat / (jnp.sqrt(v_hat) + ADAM_EPS) + ADAM_WD * w)
    return delta, m2, v2


def _pair_add(g, r1, row_chunk, name):
    _, rows, cols = g.shape
    core = lax.axis_index("c").astype(jnp.int32).reshape(1)

    def body(core_ref, g_ref, r_ref, o_ref):
        o_ref[0] = (g_ref[0, 0].astype(F32) + r_ref[0].astype(F32)).astype(BF16)

    blk = pl.BlockSpec((1, row_chunk, cols), lambda q, i, core_ref: (q, i, 0))
    grid_spec = pltpu.PrefetchScalarGridSpec(
        num_scalar_prefetch=1, grid=(N_CHIPS, rows // row_chunk),
        in_specs=[pl.BlockSpec((1, 1, row_chunk, cols), lambda q, i, core_ref: (q, core_ref[0], i, 0)), blk],
        out_specs=blk)
    return pl.pallas_call(
        body, grid_spec=grid_spec, out_shape=jax.ShapeDtypeStruct((N_CHIPS, rows, cols), BF16),
        compiler_params=_cparams(2), name=name,
    )(core, g.reshape(N_CHIPS, 2, rows, cols), r1)


def _adam_sharded(recv0, recv1, w, m, v, row_chunk, name):
    _, rows, cols = w.shape
    n_chunks = rows // row_chunk

    def body(r0_ref, r1_ref, w_ref, m_ref, v_ref, g_ref, d_ref, m2_ref, v2_ref):
        layer = pl.program_id(0)

        def run(r_ref):
            g = r_ref[0].astype(F32)
            for j in range(1, r_ref.shape[0]):
                g = g + r_ref[j].astype(F32)
            delta, m2, v2 = _adam_update(g, w_ref[0], m_ref[0], v_ref[0])
            g_ref[0], d_ref[0], m2_ref[0], v2_ref[0] = g, delta, m2, v2

        @pl.when(layer == 0)
        def _():
            run(r0_ref)

        @pl.when(layer == 1)
        def _():
            run(r1_ref)

    r0_spec = pl.BlockSpec((recv0.shape[0], row_chunk, cols), lambda l, i: (0, i * (1 - l) + (n_chunks - 1) * l, 0))
    r1_spec = pl.BlockSpec((recv1.shape[0], row_chunk, cols), lambda l, i: (0, i * l, 0))
    blk = pl.BlockSpec((1, row_chunk, cols), lambda l, i: (l, i, 0))
    out = jax.ShapeDtypeStruct(w.shape, F32)
    return pl.pallas_call(
        body,
        grid=(N_LAYERS, n_chunks),
        in_specs=[r0_spec, r1_spec, blk, blk, blk],
        out_specs=[blk] * 4,
        out_shape=[out] * 4,
        compiler_params=_cparams(2),
        name=name,
    )(recv0, recv1, w, m, v)


def _adam_dense(g, w, m, v, row_chunk, name):
    n_lead, rows, cols = w.shape

    def body(g_ref, w_ref, m_ref, v_ref, go_ref, d_ref, m2_ref, v2_ref):
        gv = g_ref[...]
        go_ref[...] = gv
        d_ref[...], m2_ref[...], v2_ref[...] = _adam_update(gv, w_ref[...], m_ref[...], v_ref[...])

    blk = pl.BlockSpec((1, row_chunk, cols), lambda l, i: (l, i, 0))
    out = jax.ShapeDtypeStruct(w.shape, F32)
    return pl.pallas_call(
        body,
        grid=(n_lead, rows // row_chunk),
        in_specs=[blk] * 4,
        out_specs=[blk] * 4,
        out_shape=[out] * 4,
        compiler_params=_cparams(2),
        name=name,
    )(g, w, m, v)


WEIGHT_NAMES = ("mod_w", "mod_b", "mix_pre_g", "mix_post_g", "w_in", "sgu_norm_g", "sgu_norm_b", "sgu_w", "sgu_b",
                "conv_w", "conv_b", "conv_norm_g", "conv_norm_b", "pool_w", "pool_scale", "branch_g", "w_out",
                "ffn_pre_g", "ffn_post_g", "ffn_up", "ffn_conv_w", "ffn_conv_b", "ffn_down")
SHARDED_BIG = ("w_in", "w_out", "ffn_up", "ffn_down")
SMALL_PACKED = tuple(n for n in WEIGHT_NAMES if n not in SHARDED_BIG + ("mod_w",))


def _rows8(rows, width=D_MODEL):
    out = [jnp.pad(r.astype(F32), (0, width - r.shape[0]))[None] for r in rows]
    out.append(jnp.zeros((8 - len(rows), width), F32))
    return jnp.concatenate(out, axis=0)


def _as_rows(a, width=D_MODEL):
    flat = a.astype(F32).reshape(-1)
    pad = (-flat.shape[0]) % width
    return jnp.pad(flat, (0, pad)).reshape(-1, width)


def _pad_cols(a, width=D_MODEL):
    return jnp.pad(a.astype(F32), ((0, 0), (0, width - a.shape[1])))


def _pack_rows(arrays):
    rows = jnp.concatenate([_as_rows(a) for a in arrays], axis=0)
    return jnp.pad(rows, ((0, (-rows.shape[0]) % 8), (0, 0)))


def _unpack_rows(packed, shapes):
    out, r = [], 0
    for shape in shapes:
        size = math.prod(shape)
        n_rows = -(-size // D_MODEL)
        out.append(packed[r:r + n_rows].reshape(-1)[:size].reshape(shape))
        r += n_rows
    return out


TILE_VREGS = MIX_TILE // 8
CHUNK_SUBLANES = CHUNK // TILE_VREGS
CHUNKS_PER_TILE = MIX_TILE // CHUNK


def _chunk_axis_to_tile(a, axis):
    shape = a.shape
    a = a.reshape(shape[:axis] + (CHUNK_SUBLANES, TILE_VREGS) + shape[axis + 1:])
    a = jnp.swapaxes(a, axis, axis + 1)
    a = jnp.tile(a, (1,) * (axis + 1) + (CHUNKS_PER_TILE,) + (1,) * (len(shape) - axis - 1))
    return a.reshape(shape[:axis] + (MIX_TILE,) + shape[axis + 1:])


def _layer_consts(l, w, mod_rows, win, wout, conv_w_full):
    modv = _rows8(list(mod_rows[l].reshape(6, D_MODEL)))
    g1024 = _rows8([w["mix_pre_g"][l], w["mix_post_g"][l], w["branch_g"][l], w["ffn_pre_g"][l], w["ffn_post_g"][l]])
    p384 = _rows8([w["sgu_norm_g"][l], w["sgu_norm_b"][l], w["conv_b"][l], w["conv_norm_g"][l], w["conv_norm_b"][l]],
                  SGU_WIDTH)
    cw = jnp.pad(conv_w_full[l], ((0, 32 - CONV_K), (0, 0)))
    sguw = _chunk_axis_to_tile(_chunk_axis_to_tile(w["sgu_w"][l], 2), 1)
    bmat = _chunk_axis_to_tile(jnp.repeat(w["sgu_b"][l].T, HEAD_DIM, axis=1), 0)
    groups = len(POOL_WINDOWS)
    eye = jnp.eye(groups, dtype=F32)
    pwbd = (eye[:, None, :, None] * w["pool_w"][l][:, :, None, :]).reshape(POOL_WIDTH, POOL_WIDTH).astype(BF16)
    psc = _rows8([w["pool_scale"][l]], POOL_WIDTH)
    seg = jnp.arange(SGU_WIDTH) // HEAD_DIM
    segp = jnp.where(seg[:, None] == seg[None, :], 1.0 / HEAD_DIM, 0.0).astype(BF16)
    return modv, g1024, (modv, g1024, p384, cw, sguw, bmat, pwbd, psc, segp, win, wout)


def _small_grad_rows(mix, ffn, loss_pieces=None):
    _, _, _, _, _, mvec, v384, dcw, dsguw, dbmat, dpw, dpsc = mix
    fvec, cgrad = ffn[5], ffn[6]
    dmod = jnp.stack([mvec[2], mvec[3], mvec[1], fvec[2], fvec[3], fvec[1]]).reshape(N_DEV, MOD_SHARD)
    g_rows = jnp.stack([mvec[4], mvec[0], mvec[5], fvec[4], fvec[0]])
    dsgu_b = dbmat[:, ::HEAD_DIM].T.reshape(1, SGU_HEADS * CHUNK)
    groups = len(POOL_WINDOWS)
    gdim = POOL_WIDTH // groups
    dpw4 = dpw.reshape(groups, gdim, groups, gdim)
    dpool = jnp.stack([dpw4[g, :, g, :] for g in range(groups)])
    misc = [dsgu_b[0], dpsc[0]] + ([] if loss_pieces is None else [loss_pieces])
    blocks = [_pad_cols(dmod), _rows8(list(g_rows)), _pad_cols(v384), _rows8(misc), _pad_cols(dcw),
              _pad_cols(cgrad[:, 0:4, :].reshape(4 * N_DEV, FF_SHARD)), _as_rows(dpool), _as_rows(dsguw)]
    return jnp.concatenate(blocks, axis=0)


def _small_grads_from_rows(total):
    per = {n: [] for n in SMALL_PACKED}
    for l in range(N_LAYERS):
        s = total[l]
        per["mod_b"].append(s[ROW_DMOD:ROW_DMOD + N_DEV, :MOD_SHARD].reshape(6 * D_MODEL))
        for j, name in enumerate(("mix_pre_g", "mix_post_g", "branch_g", "ffn_pre_g", "ffn_post_g")):
            per[name].append(s[ROW_G1024 + j])
        for j, name in enumerate(("sgu_norm_g", "sgu_norm_b", "conv_b", "conv_norm_g", "conv_norm_b")):
            per[name].append(s[ROW_V384 + j, :SGU_WIDTH])
        per["sgu_b"].append(s[ROW_SGU_B, :SGU_HEADS * CHUNK].reshape(SGU_HEADS, CHUNK))
        per["pool_scale"].append(s[ROW_POOL_SCALE, :POOL_WIDTH])
        per["conv_w"].append(s[ROW_CONV_W:ROW_CONV_W + CONV_K, :CONV_WIDTH])
        fc = s[ROW_FFN_CONV:ROW_FFN_CONV + 4 * N_DEV, :FF_SHARD].reshape(N_DEV, 4, FF_SHARD)
        per["ffn_conv_w"].append(fc[:, 0:3, :].transpose(1, 0, 2).reshape(FFN_CONV_K, 2 * D_FF))
        per["ffn_conv_b"].append(fc[:, 3, :].reshape(2 * D_FF))
        per["pool_w"].append(s[ROW_POOL_W:ROW_POOL_W + 16].reshape(len(POOL_WINDOWS), HEAD_DIM, HEAD_DIM))
        per["sgu_w"].append(s[ROW_SGU_W:ROW_SGU_W + 96].reshape(SGU_HEADS, CHUNK, CHUNK))
    return {n: jnp.stack(v) for n, v in per.items()}


def kernel(x, c, mod_w, mod_b, mix_pre_g, mix_post_g, w_in, sgu_norm_g, sgu_norm_b, sgu_w, sgu_b, conv_w, conv_b, conv_norm_g, conv_norm_b, pool_w, pool_scale, branch_g, w_out, ffn_pre_g, ffn_post_g, ffn_up, ffn_conv_w, ffn_conv_b, ffn_down, loss_target, m_mod_w, m_mod_b, m_mix_pre_g, m_mix_post_g, m_w_in, m_sgu_norm_g, m_sgu_norm_b, m_sgu_w, m_sgu_b, m_conv_w, m_conv_b, m_conv_norm_g, m_conv_norm_b, m_pool_w, m_pool_scale, m_branch_g, m_w_out, m_ffn_pre_g, m_ffn_post_g, m_ffn_up, m_ffn_conv_w, m_ffn_conv_b, m_ffn_down, v_mod_w, v_mod_b, v_mix_pre_g, v_mix_post_g, v_w_in, v_sgu_norm_g, v_sgu_norm_b, v_sgu_w, v_sgu_b, v_conv_w, v_conv_b, v_conv_norm_g, v_conv_norm_b, v_pool_w, v_pool_scale, v_branch_g, v_w_out, v_ffn_pre_g, v_ffn_post_g, v_ffn_up, v_ffn_conv_w, v_ffn_conv_b, v_ffn_down):
    w = dict(zip(WEIGHT_NAMES, (mod_w, mod_b, mix_pre_g, mix_post_g, w_in, sgu_norm_g, sgu_norm_b, sgu_w, sgu_b, conv_w,
                                conv_b, conv_norm_g, conv_norm_b, pool_w, pool_scale, branch_g, w_out, ffn_pre_g,
                                ffn_post_g, ffn_up, ffn_conv_w, ffn_conv_b, ffn_down)))
    m = dict(zip(WEIGHT_NAMES, (m_mod_w, m_mod_b, m_mix_pre_g, m_mix_post_g, m_w_in, m_sgu_norm_g, m_sgu_norm_b, m_sgu_w,
                                m_sgu_b, m_conv_w, m_conv_b, m_conv_norm_g, m_conv_norm_b, m_pool_w, m_pool_scale,
                                m_branch_g, m_w_out, m_ffn_pre_g, m_ffn_post_g, m_ffn_up, m_ffn_conv_w, m_ffn_conv_b,
                                m_ffn_down)))
    v = dict(zip(WEIGHT_NAMES, (v_mod_w, v_mod_b, v_mix_pre_g, v_mix_post_g, v_w_in, v_sgu_norm_g, v_sgu_norm_b, v_sgu_w,
                                v_sgu_b, v_conv_w, v_conv_b, v_conv_norm_g, v_conv_norm_b, v_pool_w, v_pool_scale,
                                v_branch_g, v_w_out, v_ffn_pre_g, v_ffn_post_g, v_ffn_up, v_ffn_conv_w, v_ffn_conv_b,
                                v_ffn_down)))
    me = _flat(_my_pos())
    xs = x[0]

    transposed = ("w_in", "ffn_up")
    wt = {n: jnp.swapaxes(w[n], 1, 2) if n in transposed else w[n] for n in SHARDED_BIG}
    mt = {n: jnp.swapaxes(m[n], 1, 2) if n in transposed else m[n] for n in SHARDED_BIG}
    vt = {n: jnp.swapaxes(v[n], 1, 2) if n in transposed else v[n] for n in SHARDED_BIG}
    bf16_shards = [[wt[n][l].astype(BF16) for n in SHARDED_BIG] for l in range(N_LAYERS)]

    def mixer_operands(l, win_g, wout_g):
        win = win_g.reshape(IN_WIDTH, D_MODEL)
        return _layer_consts(l, w, mod_rows, win, wout_g.reshape(D_MODEL, D_MODEL), conv_w_full)

    def ffn_operands(l, modv, g1024, wup_g, wdn_g):
        wdn = wdn_g.reshape(FF_PAIRS, FF_SHARD, D_MODEL)
        return modv, g1024, wup_g, wdn, ffn_cw_full[:, l], ffn_conv_b[l].reshape(N_DEV, 1, FF_SHARD)

    w0, w1 = bf16_shards
    c8 = jnp.broadcast_to(c, (8, D_MODEL))
    mod_b8 = jnp.pad(mod_b, ((0, 8 - N_LAYERS), (0, 0)))
    sc_all, mod_rows, win0_g, wout0_g, conv_w_g, ffn_cw_full = _gather_weights(
        c8, mod_w, mod_b8, [("gather2", w0[0]), ("gather2", w0[1]), ("gather", conv_w), ("gather", ffn_conv_w)])
    conv_w_full = conv_w_g.transpose(1, 2, 0, 3).reshape(N_LAYERS, CONV_K, CONV_WIDTH)

    modv0, g0, mix_consts0 = mixer_operands(0, win0_g, wout0_g)
    (x1, z, o, cv), (wup0_g, wdn0_g) = _mixer_fwd(xs, *mix_consts0, name="mixer_fwd_l0", natural_x=True,
                                                  job=[("gather2", w0[2]), ("gather2", w0[3])])
    ffn_consts0 = ffn_operands(0, modv0, g0, wup0_g, wdn0_g)
    (x2, y2, p, u), (win1_g, wout1_g, wdn1_g) = _ffn_fwd(
        x1, *ffn_consts0, name="ffn_fwd_l0", job=[("gather2", w1[0]), ("gather2", w1[1]), ("gather2", w1[3])])
    saved = [(xs, z, o, cv, x1, y2, p, u)]
    modv1, g1, mix_consts1 = mixer_operands(1, win1_g, wout1_g)
    (x1, z, o, cv), (wup1_g,) = _mixer_fwd(x2, *mix_consts1, name="mixer_fwd_l1", job=[("gather2", w1[2])])
    ffn_consts1 = ffn_operands(1, modv1, g1, wup1_g, wdn1_g)
    (dh, y2, p, u, loss_tile), _ = _ffn_fwd(x1, *ffn_consts1, name="ffn_fwd_l1", loss_target=loss_target[0])
    saved.append((x2, z, o, cv, x1, y2, p, u))
    loss_hi = loss_tile[0, 0].astype(BF16).astype(F32)
    loss_mid = (loss_tile[0, 0] - loss_hi).astype(BF16).astype(F32)
    loss_pieces = jnp.stack([loss_hi, loss_mid, loss_tile[0, 0] - loss_hi - loss_mid])

    def ffn_weight_grads(l, ffn):
        dp, a, dy2, h2 = ffn[1:5]
        d_up = _wgrad(dp, h2, f"wgrad_ffn_up_l{l}", tk=WGRAD_TK_FFN)
        d_dn = _wgrad(a, dy2, f"wgrad_ffn_down_l{l}", tk=WGRAD_TK_FFN).reshape(N_DEV, D_FF // N_DEV, D_MODEL)
        return d_up, d_dn

    def mixer_weight_grads(l, mix):
        dz, do, ycat, h1 = mix[1:5]
        d_in = _wgrad(dz[None], h1, f"wgrad_w_in_l{l}").reshape(N_DEV, IN_WIDTH // N_DEV, D_MODEL)
        d_out = _wgrad(ycat, do[None], f"wgrad_w_out_l{l}").reshape(N_DEV, D_MODEL // N_DEV, D_MODEL)
        return d_in, d_out

    x_in, z, o, cv, x1, y2, p, u = saved[1]
    ffn1, _ = _ffn_bwd(dh, x1, y2, p, u, *ffn_consts1[:-1], name="ffn_bwd_l1")
    d_up1, d_dn1 = ffn_weight_grads(1, ffn1)
    mix1, (sib_up1, sib_dn1) = _mixer_bwd(ffn1[0], x_in, o, z, cv, *mix_consts1, name="mixer_bwd_l1",
                                          job=[("scatter_p1", d_up1), ("scatter_p1", d_dn1)])
    chip_up1 = _pair_add(d_up1, sib_up1, 176, "pair_add_ffn_up_l1")
    chip_dn1 = _pair_add(d_dn1, sib_dn1, 176, "pair_add_ffn_down_l1")
    d_in1, d_out1 = mixer_weight_grads(1, mix1)
    small1 = _small_grad_rows(mix1, ffn1).astype(BF16)

    x_in, z, o, cv, x1, y2, p, u = saved[0]
    ffn0, job_out = _ffn_bwd(mix1[0], x1, y2, p, u, *ffn_consts0[:-1], name="ffn_bwd_l0",
                             job=[("scatter", d_in1), ("scatter", d_out1), ("scatter_p2", chip_up1),
                                  ("scatter_p2", chip_dn1)])
    recv1 = job_out
    dp, a, dy2, h2 = ffn0[1:5]
    d_dn0, (small_all1,) = _wgrad(a, dy2, "wgrad_ffn_down_l0", tk=WGRAD_TK_FFN, job=[("gather2", small1)])
    d_dn0 = d_dn0.reshape(N_DEV, D_FF // N_DEV, D_MODEL)
    d_up0, (recv_dn0,) = _wgrad(dp, h2, "wgrad_ffn_up_l0", tk=WGRAD_TK_FFN, job=[("scatter", d_dn0)])
    mix0, (recv_up0,) = _mixer_bwd(ffn0[0], x_in, o, z, cv, *mix_consts0, name="mixer_bwd_l0", natural_x=True,
                                   job=[("scatter", d_up0)])
    recv_ffn0 = (recv_up0, recv_dn0)
    grad_x = mix0[0][None]
    dz, do, ycat, h1 = mix0[1:5]
    small0 = _small_grad_rows(mix0, ffn0, loss_pieces).astype(BF16)
    d_out0, (small_all0,) = _wgrad(ycat, do[None], "wgrad_w_out_l0", tk=WGRAD_TK // 2, job=[("gather2", small0)])
    d_out0 = d_out0.reshape(N_DEV, D_MODEL // N_DEV, D_MODEL)
    d_in0, (recv_out0,) = _wgrad(dz[None], h1, "wgrad_w_in_l0", job=[("scatter", d_out0)])
    d_in0 = d_in0.reshape(N_DEV, IN_WIDTH // N_DEV, D_MODEL)
    small_total, g_mod_w, (recv_in0,) = _small_sums(sc_all, small_all0, small_all1, [("scatter", d_in0)])
    recv0 = [recv_in0, recv_out0, recv_ffn0[0], recv_ffn0[1]]
    loss = small_total[0, ROW_LOSS, 0] + small_total[0, ROW_LOSS, 1] + small_total[0, ROW_LOSS, 2]

    grads, deltas, new_m, new_v = {}, {}, {}, {}
    for j, (name, chunk) in enumerate((("w_in", 224), ("w_out", 128), ("ffn_up", 176), ("ffn_down", 176))):
        outs = _adam_sharded(recv0[j], recv1[j], wt[name], mt[name], vt[name], chunk, "adam_" + name)
        if name in transposed:
            outs = [jnp.swapaxes(t, 1, 2) for t in outs]
        grads[name], deltas[name], new_m[name], new_v[name] = outs
    grads["mod_w"], deltas["mod_w"], new_m["mod_w"], new_v["mod_w"] = _adam_dense(
        g_mod_w, mod_w, m_mod_w, v_mod_w, 256, "adam_mod_w")

    small_g = _small_grads_from_rows(small_total)
    small_g["conv_w"] = lax.dynamic_slice_in_dim(small_g["conv_w"], me * conv_w.shape[2], conv_w.shape[2], axis=2)
    small_g["ffn_conv_w"] = lax.dynamic_slice_in_dim(small_g["ffn_conv_w"], me * FF_SHARD, FF_SHARD, axis=2)
    shapes = [w[n].shape for n in SMALL_PACKED]
    packs = [_pack_rows([src[n] for n in SMALL_PACKED])[None] for src in (small_g, w, m, v)]
    _, d, m2, v2 = _adam_dense(*packs, packs[0].shape[1], "adam_small")
    for name, dd, mm, vv in zip(SMALL_PACKED, _unpack_rows(d[0], shapes), _unpack_rows(m2[0], shapes),
                                _unpack_rows(v2[0], shapes)):
        grads[name], deltas[name], new_m[name], new_v[name] = small_g[name], dd, mm, vv

    return (loss, grad_x, *[grads[n] for n in WEIGHT_NAMES], *[deltas[n] for n in WEIGHT_NAMES],
            *[new_m[n] for n in WEIGHT_NAMES], *[new_v[n] for n in WEIGHT_NAMES])
```

```python
import math

import jax
import jax.numpy as jnp
from jax import lax
from jax.experimental import pallas as pl
from jax.experimental.pallas import tpu as pltpu

F32 = jnp.float32
BF16 = jnp.bfloat16

D_MODEL = 1024
N_DEV = 8
SGU_WIDTH = 384
CONV_WIDTH = 384
POOL_WIDTH = 256
HEAD_DIM = 64
SGU_HEADS = 6
CHUNK = 128
CONV_K = 31
POOL_WINDOWS = (2, 4, 8, 16)
IN_WIDTH = 1792
D_FF = 2816
FF_SHARD = 2 * D_FF // N_DEV
FF_PAIRS = N_DEV // 2
FFN_CONV_K = 3
EPS = 1e-6
GELU_C0 = math.sqrt(2.0 / math.pi)
GELU_C1 = 0.044715

ADAM_LR = 0.001
ADAM_B1 = 0.9
ADAM_B2 = 0.999
ADAM_EPS = 1e-08
ADAM_WD = 0.01
ADAM_STEP = 10

VMEM_LIMIT_BYTES = 56 * 1024 * 1024
TILE = 256
MIX_TILE = TILE
FFN_TILE = TILE
FFN_HALO = 8 * (FFN_CONV_K - 1)
POOL_HALO = 8 * POOL_WINDOWS[-1]
WGRAD_TK = 2048
WGRAD_TK_FFN = 4096


def _cparams(n_axes):
    return pltpu.CompilerParams(dimension_semantics=("arbitrary",) * n_axes, vmem_limit_bytes=VMEM_LIMIT_BYTES)


def _whole(shape):
    nd = len(shape)
    return pl.BlockSpec(shape, lambda *_: (0,) * nd, pipeline_mode=pl.Buffered(1))


def _dot(a, b):
    return jnp.dot(a, b, preferred_element_type=F32)


def _dot_nt(a, b):
    return lax.dot_general(a, b, (((1,), (1,)), ((), ())), preferred_element_type=F32)


def _dot_tn(a, b):
    return lax.dot_general(a, b, (((0,), (0,)), ((), ())), preferred_element_type=F32)


def _gelu(x):
    t = jnp.tanh(GELU_C0 * (x + GELU_C1 * x * x * x))
    return 0.5 * x * (1.0 + t), t


def _gelu_grad(x, t):
    return 0.5 * (1.0 + t) + 0.5 * x * (1.0 - t * t) * (GELU_C0 * (1.0 + 3.0 * GELU_C1 * x * x))


def _rowmean(x):
    return jnp.mean(x, axis=-1, keepdims=True)


def _colsum(x):
    return jnp.sum(x, axis=0, keepdims=True)


def _rms_fwd(x):
    r = lax.rsqrt(_rowmean(x * x) + EPS)
    return x * r, r


def _rms_bwd(dxhat, xhat, r):
    return r * (dxhat - xhat * _rowmean(dxhat * xhat))


N_PEERS = N_DEV - 1
ANY = pl.BlockSpec(memory_space=pl.ANY)
VMEM = pl.BlockSpec(memory_space=pltpu.VMEM)


def _my_pos():
    return lax.axis_index("x"), lax.axis_index("y"), lax.axis_index("c")


def _peer(pos, k):
    x, y, c = pos
    return (1 - x if k & 4 else x, 1 - y if k & 2 else y, 1 - c if k & 1 else c)


def _flat(pos):
    return 4 * pos[0] + 2 * pos[1] + pos[2]


def _remote_copy(src, dst, send_sem, recv_sem, peer):
    return pltpu.make_async_remote_copy(src_ref=src, dst_ref=dst, send_sem=send_sem, recv_sem=recv_sem,
                                        device_id=peer, device_id_type=pl.DeviceIdType.MESH)


N_CHIPS = N_DEV // 2
SIBLING = 1
SAME_CORE_PEERS = (2, 4, 6)


def _exchange_out_shapes(job):
    def shape(kind, a):
        if kind in ("gather", "gather2"):
            return (N_DEV,) + a.shape
        if kind == "scatter_p1":
            return (N_CHIPS,) + a.shape[1:]
        return a.shape
    return [jax.ShapeDtypeStruct(shape(kind, a), a.dtype) for kind, a in job]


def _exchange_sems(n):
    return [pltpu.SemaphoreType.DMA((n, N_PEERS)), pltpu.SemaphoreType.DMA((n, N_PEERS)), pltpu.SemaphoreType.DMA((n,))]


def _exchange_copies(kinds, src_refs, dst_refs, send_sems, recv_sems, local_sems, phase):
    pos = _my_pos()
    me = _flat(pos)
    chip, core = 2 * pos[0] + pos[1], pos[2]
    copies = []

    def remote(a, src, dst, k, sem=None):
        sem = k - 1 if sem is None else sem
        copies.append(_remote_copy(src, dst, send_sems.at[a, sem], recv_sems.at[a, sem], _peer(pos, k)))

    for a, kind in enumerate(kinds):
        src, dst = src_refs[a], dst_refs[a]
        if phase == 1:
            if kind == "gather2":
                for k in SAME_CORE_PEERS:
                    remote(a, dst.at[me ^ k], dst.at[me ^ k], SIBLING, sem=k)
        elif kind in ("gather", "gather2"):
            copies.append(pltpu.make_async_copy(src, dst.at[me], local_sems.at[a]))
            for k in (range(1, N_DEV) if kind == "gather" else (SIBLING,) + SAME_CORE_PEERS):
                remote(a, src, dst.at[me], k)
        elif kind == "scatter":
            copies.append(pltpu.make_async_copy(src.at[me], dst.at[me], local_sems.at[a]))
            for k in range(1, N_DEV):
                remote(a, src.at[me ^ k], dst.at[me], k)
        elif kind == "scatter_p1":
            for q in range(N_CHIPS):
                remote(a, src.at[2 * q + 1 - core], dst.at[q], SIBLING, sem=q)
        elif kind == "scatter_p2":
            copies.append(pltpu.make_async_copy(src.at[chip], dst.at[chip], local_sems.at[a]))
            for k in SAME_CORE_PEERS:
                remote(a, src.at[chip ^ (k >> 1)], dst.at[chip], k)
    return copies


def _pallas_call_with_exchange(body, *, grid, in_specs, out_specs, out_shape, scratch_shapes, operands, name, job):
    params = _cparams(len(grid))
    if not job:
        outs = pl.pallas_call(body, grid=grid, in_specs=in_specs, out_specs=out_specs, out_shape=out_shape,
                              scratch_shapes=scratch_shapes, compiler_params=params, name=name)(*operands)
        return outs, []
    kinds = [kind for kind, _ in job]
    relayed = [kind if kind == "gather2" else None for kind in kinds]
    unrelayed = [None if kind == "gather2" else kind for kind in kinds]
    n, n_in, n_out, n_scr = len(job), len(in_specs), len(out_specs), len(scratch_shapes)
    n_steps = math.prod(grid)
    relay_step = max(n_steps - 2, 0)

    def wrapped(*refs):
        ins, jin = refs[:n_in], refs[n_in:n_in + n]
        outs, jout = refs[n_in + n:n_in + n + n_out], refs[n_in + n + n_out:n_in + 2 * n + n_out]
        scr = refs[n_in + 2 * n + n_out:n_in + 2 * n + n_out + n_scr]
        sems = refs[n_in + 2 * n + n_out + n_scr:]
        step = pl.program_id(0)
        for d in range(1, len(grid)):
            step = step * grid[d] + pl.program_id(d)

        def copies(which, phase):
            return _exchange_copies(which, jin, jout, *sems, phase=phase)

        @pl.when(step == 0)
        def _():
            for cp in copies(kinds, 0):
                cp.start()

        body(*ins, *outs, *scr)

        @pl.when(step == relay_step)
        def _():
            for cp in copies(relayed, 0):
                cp.wait()
            for cp in copies(relayed, 1):
                cp.start()

        @pl.when(step == n_steps - 1)
        def _():
            for cp in copies(unrelayed, 0) + copies(relayed, 1):
                cp.wait()

    res = pl.pallas_call(
        wrapped, grid=grid,
        in_specs=list(in_specs) + [ANY] * n,
        out_specs=list(out_specs) + [ANY] * n,
        out_shape=list(out_shape) + _exchange_out_shapes(job),
        scratch_shapes=list(scratch_shapes) + _exchange_sems(n),
        compiler_params=params, name=name,
    )(*operands, *[a for _, a in job])
    return res[:n_out], res[n_out:]


def _seg_mean(x, segp):
    hi = x.astype(BF16)
    lo = (x - hi.astype(F32)).astype(BF16)
    return _dot(hi, segp) + _dot(lo, segp)


def _rot_rows(x, shift):
    m, c = x.shape
    return pltpu.roll(x.reshape(m // 8, 8, c), shift, 1).reshape(m, c)


def _sublane_is(shape, s):
    return lax.broadcasted_iota(jnp.int32, shape, 0) % 8 == s


def _causal_tail(tail, prev_rot):
    rot = _rot_rows(tail, 1)
    return jnp.where(_sublane_is(tail.shape, 0), prev_rot, rot), rot


def _lookahead_head(head, next_rot):
    rot = _rot_rows(head, 7)
    return jnp.where(_sublane_is(head.shape, 7), next_rot, rot), rot


def _tile_token_index(t, tile_idx):
    r = lax.broadcasted_iota(jnp.int32, (t, 1), 0)
    return tile_idx * t + (r % 8) * (t // 8) + r // 8


def _interleave(x):
    t, c = x.shape
    return jnp.swapaxes(x.reshape(8, t // 8, c), 0, 1).reshape(t, c)


def _deinterleave(x):
    t, c = x.shape
    return jnp.swapaxes(x.reshape(t // 8, 8, c), 0, 1).reshape(t, c)


def _ffn_fwd(x1, modv, g1024, wup, wdn, cw, cb, name="ffn_fwd", job=None, loss_target=None):
    s_len = x1.shape[0]
    t = FFN_TILE
    n_tiles = s_len // t
    with_loss = loss_target is not None

    def body(x1_ref, *rest):
        if with_loss:
            tgt_ref, rest = rest[0], rest[1:]
            loss_ref, rest = rest[10], rest[:10] + rest[11:]
        mod_ref, g_ref, wup_ref, wdn_ref, cw_ref, cb_ref, x2_ref, y2_ref, p_ref, u_ref, ext_ref, carry_ref = rest
        i = pl.program_id(0)

        @pl.when(i == 0)
        def _():
            carry_ref[...] = jnp.zeros_like(carry_ref)
            if with_loss:
                loss_ref[...] = jnp.zeros_like(loss_ref)

        x1v = x1_ref[...]
        pre_g, post_g = g_ref[3:4, :], g_ref[4:5, :]
        sh2, sc2, g2 = mod_ref[3:4, :], mod_ref[4:5, :], mod_ref[5:6, :]
        xhat, _ = _rms_fwd(x1v)
        h2b = (xhat * pre_g * (1.0 + sc2) + sh2).astype(BF16)

        def conv_shard(s):
            p = _dot_nt(h2b, wup_ref[s])
            p_ref[s] = p.astype(BF16)
            ext_ref[0:FFN_HALO, :], carry_ref[s] = _causal_tail(p[t - FFN_HALO:t, :], carry_ref[s])
            ext_ref[FFN_HALO:FFN_HALO + t, :] = p
            w = cw_ref[s]
            u = w[0:1, :] * ext_ref[0:t, :] + w[1:2, :] * ext_ref[8:8 + t, :] + w[2:3, :] * p + cb_ref[s]
            u_ref[s] = u.astype(BF16)
            return u

        y2 = jnp.zeros((t, D_MODEL), F32)
        for j in range(FF_PAIRS):
            ug = conv_shard(j)
            uv = conv_shard(j + FF_PAIRS)
            ge, _ = _gelu(ug)
            y2 = y2 + _dot((ge * uv).astype(BF16), wdn_ref[j])
        y2_ref[...] = y2
        yhat, _ = _rms_fwd(y2)
        x2 = x1v + g2 * (yhat * post_g)
        if with_loss:
            diff = x2 - _interleave(tgt_ref[...])
            x2_ref[...] = diff * (1.0 / D_MODEL)
            loss_ref[...] += (0.5 / D_MODEL) * jnp.sum(diff * diff)
        else:
            x2_ref[...] = x2

    tile = pl.BlockSpec((t, D_MODEL), lambda i: (i, 0))
    consts = (modv, g1024, wup, wdn, cw, cb)
    shards = pl.BlockSpec((N_DEV, t, FF_SHARD), lambda i: (0, i, 0))
    out_specs = [tile, tile, shards, shards]
    out_shape = [jax.ShapeDtypeStruct((s_len, D_MODEL), F32), jax.ShapeDtypeStruct((s_len, D_MODEL), F32),
                 jax.ShapeDtypeStruct((N_DEV, s_len, FF_SHARD), BF16),
                 jax.ShapeDtypeStruct((N_DEV, s_len, FF_SHARD), BF16)]
    if with_loss:
        out_specs.append(pl.BlockSpec((8, 128), lambda i: (0, 0)))
        out_shape.append(jax.ShapeDtypeStruct((8, 128), F32))
    return _pallas_call_with_exchange(
        body,
        grid=(n_tiles,),
        in_specs=[tile] * (2 if with_loss else 1) + [_whole(c.shape) for c in consts],
        out_specs=out_specs,
        out_shape=out_shape,
        scratch_shapes=[pltpu.VMEM((FFN_HALO + t, FF_SHARD), F32), pltpu.VMEM((N_DEV, FFN_HALO, FF_SHARD), F32)],
        operands=(x1,) + ((loss_target,) if with_loss else ()) + consts,
        name=name, job=job)


def _ffn_bwd(dx2, x1, y2, p, u, modv, g1024, wup, wdn, cw, name="ffn_bwd", job=None):
    s_len = x1.shape[0]
    t = FFN_TILE
    n_tiles = s_len // t
    hb = FFN_HALO

    def body(dx2_ref, x1_ref, y2_ref, p_ref, ph_ref, u_ref, mod_ref, g_ref, wup_ref, wdn_ref, cw_ref,
             dx1_ref, dp_ref, a_ref, dy2_ref, h2_ref, vec_ref, cgrad_ref, ext_ref, dext_ref, dcarry_ref):
        i = pl.program_id(0)
        tile_idx = n_tiles - 1 - i

        @pl.when(i == 0)
        def _():
            vec_ref[...] = jnp.zeros_like(vec_ref)
            cgrad_ref[...] = jnp.zeros_like(cgrad_ref)
            dcarry_ref[...] = jnp.zeros_like(dcarry_ref)

        dx2v, x1v, y2v = dx2_ref[...], x1_ref[...], y2_ref[...]
        pre_g, post_g = g_ref[3:4, :], g_ref[4:5, :]
        sh2, sc2, g2 = mod_ref[3:4, :], mod_ref[4:5, :], mod_ref[5:6, :]

        yhat, ry = _rms_fwd(y2v)
        vec_ref[1:2, :] += _colsum(dx2v * (yhat * post_g))
        dyn = dx2v * g2
        vec_ref[0:1, :] += _colsum(dyn * yhat)
        dy2b = _rms_bwd(dyn * post_g, yhat, ry).astype(BF16)
        dy2_ref[...] = dy2b

        xhat, rx = _rms_fwd(x1v)
        xn = xhat * pre_g
        h2_ref[...] = (xn * (1.0 + sc2) + sh2).astype(BF16)

        not_first = (tile_idx > 0).astype(F32)

        def recompute(s, slot):
            pf = p_ref[s].astype(F32)
            prev_rot = _rot_rows(ph_ref[s].astype(F32), 1) * not_first
            ext_ref[slot, 0:hb, :], _ = _causal_tail(pf[t - hb:t, :], prev_rot)
            ext_ref[slot, hb:hb + t, :] = pf
            return u_ref[s].astype(F32)

        def conv_bwd(s, slot, du):
            w = cw_ref[s]
            cgrad_ref[s, 0:1, :] += _colsum(du * ext_ref[slot, 0:t, :])
            cgrad_ref[s, 1:2, :] += _colsum(du * ext_ref[slot, 8:8 + t, :])
            cgrad_ref[s, 2:3, :] += _colsum(du * ext_ref[slot, 16:16 + t, :])
            cgrad_ref[s, 3:4, :] += _colsum(du)
            dext_ref[0:t, :] = du
            dext_ref[t:t + hb, :], dcarry_ref[s] = _lookahead_head(du[0:hb, :], dcarry_ref[s])
            dp = w[2:3, :] * du + w[1:2, :] * dext_ref[8:8 + t, :] + w[0:1, :] * dext_ref[16:16 + t, :]
            dpb = dp.astype(BF16)
            dp_ref[s] = dpb
            return _dot(dpb, wup_ref[s])

        dh2 = jnp.zeros((t, D_MODEL), F32)
        for j in range(FF_PAIRS):
            ug = recompute(j, 0)
            uv = recompute(j + FF_PAIRS, 1)
            ge, th = _gelu(ug)
            a_ref[j] = (ge * uv).astype(BF16)
            da = _dot_nt(dy2b, wdn_ref[j])
            dh2 = dh2 + conv_bwd(j, 0, da * uv * _gelu_grad(ug, th))
            dh2 = dh2 + conv_bwd(j + FF_PAIRS, 1, da * ge)

        vec_ref[2:3, :] += _colsum(dh2)
        vec_ref[3:4, :] += _colsum(dh2 * xn)
        dxn = dh2 * (1.0 + sc2)
        vec_ref[4:5, :] += _colsum(dxn * xhat)
        dx1_ref[...] = dx2v + _rms_bwd(dxn * pre_g, xhat, rx)

    rev = lambda i: (n_tiles - 1 - i, 0)
    tile = pl.BlockSpec((t, D_MODEL), rev)
    halo_idx = lambda i: (0, jnp.maximum((n_tiles - 1 - i) * (t // hb) - 1, 0), 0)
    return _pallas_call_with_exchange(
        body,
        grid=(n_tiles,),
        in_specs=[tile, tile, tile,
                  pl.BlockSpec((N_DEV, t, FF_SHARD), lambda i: (0, n_tiles - 1 - i, 0)),
                  pl.BlockSpec((N_DEV, hb, FF_SHARD), halo_idx),
                  pl.BlockSpec((N_DEV, t, FF_SHARD), lambda i: (0, n_tiles - 1 - i, 0)),
                  _whole(modv.shape), _whole(g1024.shape), _whole(wup.shape), _whole(wdn.shape),
                  _whole(cw.shape)],
        out_specs=[tile,
                   pl.BlockSpec((N_DEV, t, FF_SHARD), lambda i: (0, n_tiles - 1 - i, 0)),
                   pl.BlockSpec((FF_PAIRS, t, FF_SHARD), lambda i: (0, n_tiles - 1 - i, 0)),
                   tile, tile,
                   pl.BlockSpec((8, D_MODEL), lambda i: (0, 0)),
                   pl.BlockSpec((N_DEV, 8, FF_SHARD), lambda i: (0, 0, 0))],
        out_shape=[jax.ShapeDtypeStruct((s_len, D_MODEL), F32),
                   jax.ShapeDtypeStruct((N_DEV, s_len, FF_SHARD), BF16),
                   jax.ShapeDtypeStruct((FF_PAIRS, s_len, FF_SHARD), BF16),
                   jax.ShapeDtypeStruct((s_len, D_MODEL), BF16),
                   jax.ShapeDtypeStruct((s_len, D_MODEL), BF16),
                   jax.ShapeDtypeStruct((8, D_MODEL), F32),
                   jax.ShapeDtypeStruct((N_DEV, 8, FF_SHARD), F32)],
        scratch_shapes=[pltpu.VMEM((2, hb + t, FF_SHARD), F32), pltpu.VMEM((t + hb, FF_SHARD), F32),
                        pltpu.VMEM((N_DEV, hb, FF_SHARD), F32)],
        operands=(dx2, x1, y2, p, p, u, modv, g1024, wup, wdn, cw),
        name=name, job=job)


def _wgrad(a, b, name, tk=WGRAD_TK, job=None):
    a_grouped, b_grouped = a.ndim == 3, b.ndim == 3
    groups = a.shape[0] if a_grouped else b.shape[0]
    s_len, m, n = a.shape[-2], a.shape[-1], b.shape[-1]
    tk = min(tk, s_len)
    n_k = s_len // tk

    def body(a_ref, b_ref, o_ref, acc_ref):
        k = pl.program_id(1)
        av = a_ref[0] if a_grouped else a_ref[...]
        bv = b_ref[0] if b_grouped else b_ref[...]
        part = _dot_tn(av, bv)
        if n_k == 1:
            o_ref[0] = part.astype(BF16)
            return

        @pl.when(k == 0)
        def _():
            acc_ref[...] = part

        @pl.when(jnp.logical_and(k > 0, k < n_k - 1))
        def _():
            acc_ref[...] += part

        @pl.when(k == n_k - 1)
        def _():
            o_ref[0] = (acc_ref[...] + part).astype(BF16)

    a_spec = pl.BlockSpec((1, tk, m), lambda g, k: (g, k, 0)) if a_grouped else pl.BlockSpec((tk, m), lambda g, k: (k, 0))
    b_spec = pl.BlockSpec((1, tk, n), lambda g, k: (g, k, 0)) if b_grouped else pl.BlockSpec((tk, n), lambda g, k: (k, 0))
    (out,), exchanged = _pallas_call_with_exchange(
        body,
        grid=(groups, n_k),
        in_specs=[a_spec, b_spec],
        out_specs=[pl.BlockSpec((1, m, n), lambda g, k: (g, 0, 0))],
        out_shape=[jax.ShapeDtypeStruct((groups, m, n), BF16)],
        scratch_shapes=[pltpu.VMEM((m, n), F32)],
        operands=(a, b),
        name=name, job=job)
    return (out, exchanged) if job else out


def _lane(shape):
    return lax.broadcasted_iota(jnp.int32, shape, 1)


def _by_pool_group(shape, vals):
    lane = _lane(shape)
    return jnp.where(lane < 64, vals[0], jnp.where(lane < 128, vals[1], jnp.where(lane < 192, vals[2], vals[3])))


def _pool_inv_counts(t, tile_idx):
    pos1 = _tile_token_index(t, tile_idx) + 1
    return [1.0 / jnp.minimum(pos1, w).astype(F32) for w in POOL_WINDOWS]


def _sgu_keep_mask(t):
    tok_r = _tile_token_index(t, 0)
    c = lax.broadcasted_iota(jnp.int32, (1, t), 1)
    tok_c = (c % 8) * (t // 8) + c // 8
    return jnp.logical_and(tok_r // CHUNK == tok_c // CHUNK, tok_r >= tok_c)


def _masked_sgu_w(sguw_ref):
    keep = _sgu_keep_mask(sguw_ref.shape[1])
    return [jnp.where(keep, sguw_ref[h], 0.0).astype(BF16) for h in range(SGU_HEADS)]


def _branches_fwd(z, tile_idx, p384_ref, cw_ref, wm, bmat_ref, pwbd_ref, psc_ref, segp_ref, g_ref, hext_ref, zext_ref,
                  h_prev_rot, z_prev_rot, conv_saved=None):
    t = z.shape[0]
    segp = segp_ref[...]
    r = {}
    u, _ = _gelu(z[:, 0:SGU_WIDTH])
    vraw, _ = _gelu(z[:, SGU_WIDTH:2 * SGU_WIDTH])
    xc = vraw - _seg_mean(vraw, segp)
    rstd_v = lax.rsqrt(_seg_mean(xc * xc, segp) + EPS)
    xh_v = xc * rstd_v
    vnb = (xh_v * p384_ref[0:1, :] + p384_ref[1:2, :]).astype(BF16)
    first_head = _lane((t, 128)) < HEAD_DIM
    f_pairs = []
    for pr in range(SGU_HEADS // 2):
        vp = vnb[:, pr * 128:(pr + 1) * 128]
        f_pairs.append(jnp.where(first_head, _dot(wm[2 * pr], vp), _dot(wm[2 * pr + 1], vp)))
    f = jnp.concatenate(f_pairs, axis=1) + bmat_ref[...]
    ya = u * f
    r.update(u=u, xh_v=xh_v, rstd_v=rstd_v, vnb=vnb, f=f)
    o_b = 2 * SGU_WIDTH
    a_in = z[:, o_b:o_b + CONV_WIDTH]
    sig_g = jax.nn.sigmoid(z[:, o_b + CONV_WIDTH:o_b + 2 * CONV_WIDTH])
    hh = a_in * sig_g
    hext_ref[0:t, :], r["h_rot"] = _causal_tail(hh, h_prev_rot)
    hext_ref[t:2 * t, :] = hh
    if conv_saved is None:
        conv = jnp.zeros((t, CONV_WIDTH), F32) + p384_ref[2:3, :]
        for k in range(CONV_K):
            conv = conv + cw_ref[k:k + 1, :] * hext_ref[pl.ds(t - 8 * (CONV_K - 1 - k), t), :]
        r["conv"] = conv
    else:
        conv = conv_saved
    cc = conv - _rowmean(conv)
    rstd_c = lax.rsqrt(_rowmean(cc * cc) + EPS)
    xh_c = cc * rstd_c
    cn = xh_c * p384_ref[3:4, :] + p384_ref[4:5, :]
    sig_c = jax.nn.sigmoid(cn)
    yb = cn * sig_c
    r.update(a_in=a_in, sig_g=sig_g, xh_c=xh_c, rstd_c=rstd_c, cn=cn, sig_c=sig_c)
    o_c = o_b + 2 * CONV_WIDTH
    zc = z[:, o_c:o_c + POOL_WIDTH]
    zext_ref[0:POOL_HALO, :], r["z_rot"] = _causal_tail(zc[t - POOL_HALO:t, :], z_prev_rot)
    zext_ref[POOL_HALO:POOL_HALO + t, :] = zc
    sums, acc = [], zc
    for j in range(1, POOL_WINDOWS[-1]):
        acc = acc + zext_ref[pl.ds(POOL_HALO - 8 * j, t), :]
        if j + 1 in POOL_WINDOWS:
            sums.append(acc)
    inv = _pool_inv_counts(t, tile_idx)
    dpool = _by_pool_group((t, POOL_WIDTH), [s * iv for s, iv in zip(sums, inv)]) - zc
    ycp = _dot(dpool.astype(BF16), pwbd_ref[...])
    yc = ycp * psc_ref[0:1, :]
    r.update(dpool=dpool, ycp=ycp)
    yha, ra = _rms_fwd(ya)
    yhb, rb = _rms_fwd(yb)
    yhc, rc = _rms_fwd(yc)
    bg = g_ref[2:3, :]
    ycat = jnp.concatenate([yha * bg[:, 0:384], yhb * bg[:, 384:768], yhc * bg[:, 768:1024]], axis=1)
    r.update(yha=yha, ra=ra, yhb=yhb, rb=rb, yhc=yhc, rc=rc, ycat=ycat)
    return r


def _mixer_fwd(x, modv, g1024, p384, cw, sguw, bmat, pwbd, psc, segp, win, wout, name="mixer_fwd", job=None,
               natural_x=False):
    s_len = x.shape[0]
    t = MIX_TILE

    def body(x_ref, mod_ref, g_ref, p384_ref, cw_ref, sguw_ref, bmat_ref, pwbd_ref, psc_ref, segp_ref, win_ref, wout_ref,
             x1_ref, z_ref, o_ref, conv_ref, hext_ref, zext_ref, hrot_ref, zrot_ref, wm_ref):
        i = pl.program_id(0)

        @pl.when(i == 0)
        def _():
            hrot_ref[...] = jnp.zeros_like(hrot_ref)
            zrot_ref[...] = jnp.zeros_like(zrot_ref)
            for h, wmh in enumerate(_masked_sgu_w(sguw_ref)):
                wm_ref[h] = wmh

        xv = _interleave(x_ref[...]) if natural_x else x_ref[...]
        sh1, sc1, g1 = mod_ref[0:1, :], mod_ref[1:2, :], mod_ref[2:3, :]
        xhat, _ = _rms_fwd(xv)
        h1 = xhat * g_ref[0:1, :] * (1.0 + sc1) + sh1
        z = _dot_nt(h1.astype(BF16), win_ref[...])
        z_ref[...] = z
        r = _branches_fwd(z, i, p384_ref, cw_ref, [wm_ref[h] for h in range(SGU_HEADS)], bmat_ref, pwbd_ref, psc_ref,
                          segp_ref, g_ref, hext_ref, zext_ref, hrot_ref[...], zrot_ref[...])
        hrot_ref[...] = r["h_rot"]
        zrot_ref[...] = r["z_rot"]
        conv_ref[...] = r["conv"]
        o = _dot(r["ycat"].astype(BF16), wout_ref[...])
        o_ref[...] = o
        ohat, _ = _rms_fwd(o)
        x1_ref[...] = xv + g1 * (ohat * g_ref[1:2, :])

    tile = pl.BlockSpec((t, D_MODEL), lambda i: (i, 0))
    consts = (modv, g1024, p384, cw, sguw, bmat, pwbd, psc, segp, win, wout)
    return _pallas_call_with_exchange(
        body,
        grid=(s_len // t,),
        in_specs=[tile] + [_whole(c.shape) for c in consts],
        out_specs=[tile, pl.BlockSpec((t, IN_WIDTH), lambda i: (i, 0)), tile,
                   pl.BlockSpec((t, CONV_WIDTH), lambda i: (i, 0))],
        out_shape=[jax.ShapeDtypeStruct((s_len, D_MODEL), F32), jax.ShapeDtypeStruct((s_len, IN_WIDTH), F32),
                   jax.ShapeDtypeStruct((s_len, D_MODEL), F32), jax.ShapeDtypeStruct((s_len, CONV_WIDTH), F32)],
        scratch_shapes=[pltpu.VMEM((2 * t, CONV_WIDTH), F32), pltpu.VMEM((POOL_HALO + t, POOL_WIDTH), F32),
                        pltpu.VMEM((t, CONV_WIDTH), F32), pltpu.VMEM((POOL_HALO, POOL_WIDTH), F32),
                        pltpu.VMEM((SGU_HEADS, t, t), BF16)],
        operands=(x, *consts),
        name=name, job=job)


def _mixer_bwd(dx1, x, o, z, conv, modv, g1024, p384, cw, sguw, bmat, pwbd, psc, segp, win, wout, name="mixer_bwd",
               job=None, natural_x=False):
    s_len = x.shape[0]
    t = MIX_TILE
    n_tiles = s_len // t

    def body(dx1_ref, x_ref, o_ref, z_ref, zh_ref, conv_ref, mod_ref, g_ref, p384_ref, cw_ref, sguw_ref, bmat_ref, pwbd_ref,
             psc_ref, segp_ref, win_ref, wout_ref,
             dx_ref, dz_ref, do_ref, ycat_ref, h1_ref, vec_ref, v384_ref, dcw_ref, dsguw_out_ref, dbmat_out_ref, dpw_ref,
             dpsc_ref, hext_ref, zext_ref, gext_ref, qext_ref, grot_ref, qrot_ref, wm_ref, dsguw_ref, dbmat_ref):
        i = pl.program_id(0)
        tile_idx = n_tiles - 1 - i

        @pl.when(i == 0)
        def _():
            for ref in (vec_ref, v384_ref, dcw_ref, dsguw_ref, dbmat_ref, dpw_ref, dpsc_ref, grot_ref, qrot_ref):
                ref[...] = jnp.zeros_like(ref)
            for h, wmh in enumerate(_masked_sgu_w(sguw_ref)):
                wm_ref[h] = wmh

        dx1v, ov, z = dx1_ref[...], o_ref[...], z_ref[...]
        xv = _interleave(x_ref[...]) if natural_x else x_ref[...]
        sh1, sc1, g1 = mod_ref[0:1, :], mod_ref[1:2, :], mod_ref[2:3, :]
        pre_g, post_g, bg = g_ref[0:1, :], g_ref[1:2, :], g_ref[2:3, :]
        segp = segp_ref[...]

        ohat, ro = _rms_fwd(ov)
        vec_ref[1:2, :] += _colsum(dx1v * (ohat * post_g))
        don = dx1v * g1
        vec_ref[0:1, :] += _colsum(don * ohat)
        dob = _rms_bwd(don * post_g, ohat, ro).astype(BF16)
        do_ref[...] = dob
        dycat = _dot_nt(dob, wout_ref[...])

        not_first = (tile_idx > 0).astype(F32)
        o_b = 2 * SGU_WIDTH
        o_c = o_b + 2 * CONV_WIDTH
        h_prev = zh_ref[:, o_b:o_b + CONV_WIDTH] * jax.nn.sigmoid(zh_ref[:, o_b + CONV_WIDTH:o_c])
        h_prev_rot = _rot_rows(h_prev, 1) * not_first
        z_prev_rot = _rot_rows(zh_ref[t - POOL_HALO:t, o_c:o_c + POOL_WIDTH], 1) * not_first
        wm = [wm_ref[h] for h in range(SGU_HEADS)]
        r = _branches_fwd(z, tile_idx, p384_ref, cw_ref, wm, bmat_ref, pwbd_ref, psc_ref, segp_ref, g_ref,
                          hext_ref, zext_ref, h_prev_rot, z_prev_rot, conv_saved=conv_ref[...])
        ycat_ref[...] = r["ycat"].astype(BF16)

        def branch_norm_bwd(dyn, yhat, rr, gain):
            return _colsum(dyn * yhat), _rms_bwd(dyn * gain, yhat, rr)

        dga, dya = branch_norm_bwd(dycat[:, 0:384], r["yha"], r["ra"], bg[:, 0:384])
        dgb, dyb = branch_norm_bwd(dycat[:, 384:768], r["yhb"], r["rb"], bg[:, 384:768])
        dgc, dyc = branch_norm_bwd(dycat[:, 768:1024], r["yhc"], r["rc"], bg[:, 768:1024])
        vec_ref[5:6, :] += jnp.concatenate([dga, dgb, dgc], axis=1)

        du_act = dya * r["f"]
        df = dya * r["u"]
        first_head = _lane((t, 128)) < HEAD_DIM
        dbmat_ref[...] += df
        dvn_pairs = []
        for pr in range(SGU_HEADS // 2):
            dfp = df[:, pr * 128:(pr + 1) * 128]
            df0 = jnp.where(first_head, dfp, 0.0).astype(BF16)
            df1 = jnp.where(first_head, 0.0, dfp).astype(BF16)
            vp = r["vnb"][:, pr * 128:(pr + 1) * 128]
            dvn_pairs.append(_dot_tn(wm[2 * pr], df0) + _dot_tn(wm[2 * pr + 1], df1))
            dsguw_ref[2 * pr] += _dot_nt(df0, vp)
            dsguw_ref[2 * pr + 1] += _dot_nt(df1, vp)
        dvn = jnp.concatenate(dvn_pairs, axis=1)
        v384_ref[0:1, :] += _colsum(dvn * r["xh_v"])
        v384_ref[1:2, :] += _colsum(dvn)
        dxh = dvn * p384_ref[0:1, :]
        dvraw = r["rstd_v"] * (dxh - _seg_mean(dxh, segp) - r["xh_v"] * _seg_mean(dxh * r["xh_v"], segp))
        zu, zv = z[:, 0:SGU_WIDTH], z[:, SGU_WIDTH:o_b]
        _, tu = _gelu(zu)
        _, tv = _gelu(zv)
        dz_u = du_act * _gelu_grad(zu, tu)
        dz_v = dvraw * _gelu_grad(zv, tv)

        cn, sig_c = r["cn"], r["sig_c"]
        dcn = dyb * (sig_c * (1.0 + cn * (1.0 - sig_c)))
        v384_ref[3:4, :] += _colsum(dcn * r["xh_c"])
        v384_ref[4:5, :] += _colsum(dcn)
        dxc = dcn * p384_ref[3:4, :]
        gconv = r["rstd_c"] * (dxc - _rowmean(dxc) - r["xh_c"] * _rowmean(dxc * r["xh_c"]))
        v384_ref[2:3, :] += _colsum(gconv)
        gext_ref[0:t, :] = gconv
        gext_ref[t:2 * t, :], grot_ref[...] = _lookahead_head(gconv, grot_ref[...])
        dhh = jnp.zeros((t, CONV_WIDTH), F32)
        for k in range(CONV_K):
            shift = CONV_K - 1 - k
            dcw_ref[k:k + 1, :] += _colsum(gconv * hext_ref[pl.ds(t - 8 * shift, t), :])
            dhh = dhh + cw_ref[k:k + 1, :] * gext_ref[pl.ds(8 * shift, t), :]
        sig_g = r["sig_g"]
        dz_a = dhh * sig_g
        dz_g = dhh * r["a_in"] * sig_g * (1.0 - sig_g)

        dpsc_ref[0:1, :] += _colsum(dyc * r["ycp"])
        dycp = (dyc * psc_ref[0:1, :]).astype(BF16)
        dpw_ref[...] += _dot_tn(r["dpool"].astype(BF16), dycp)
        ddp = _dot_nt(dycp, pwbd_ref[...])
        inv = _pool_inv_counts(t, tile_idx)
        q = ddp * _by_pool_group((t, POOL_WIDTH), inv)
        qext_ref[0:t, :] = q
        qext_ref[t:t + POOL_HALO, :], qrot_ref[...] = _lookahead_head(q[0:POOL_HALO, :], qrot_ref[...])
        sums, acc = [], q
        for j in range(1, POOL_WINDOWS[-1]):
            acc = acc + qext_ref[pl.ds(8 * j, t), :]
            if j + 1 in POOL_WINDOWS:
                sums.append(acc)
        dz_c = _by_pool_group((t, POOL_WIDTH), sums) - ddp

        dzb = jnp.concatenate([dz_u, dz_v, dz_a, dz_g, dz_c], axis=1).astype(BF16)
        dz_ref[...] = dzb
        dh1 = _dot(dzb, win_ref[...])

        xhat, rx = _rms_fwd(xv)
        xn = xhat * pre_g
        h1_ref[...] = (xn * (1.0 + sc1) + sh1).astype(BF16)
        vec_ref[2:3, :] += _colsum(dh1)
        vec_ref[3:4, :] += _colsum(dh1 * xn)
        dxn = dh1 * (1.0 + sc1)
        vec_ref[4:5, :] += _colsum(dxn * xhat)
        dx = dx1v + _rms_bwd(dxn * pre_g, xhat, rx)
        dx_ref[...] = _deinterleave(dx) if natural_x else dx

        @pl.when(i == n_tiles - 1)
        def _():
            keep = _sgu_keep_mask(t)
            for h in range(SGU_HEADS):
                rows_natural = _deinterleave(jnp.where(keep, dsguw_ref[h], 0.0))
                natural = _deinterleave(rows_natural.T).T
                dsguw_out_ref[h] = sum(natural[b * CHUNK:(b + 1) * CHUNK, b * CHUNK:(b + 1) * CHUNK]
                                       for b in range(t // CHUNK))
            dbmat = _deinterleave(float(HEAD_DIM) * _seg_mean(dbmat_ref[...], segp))
            dbmat_out_ref[...] = sum(dbmat[b * CHUNK:(b + 1) * CHUNK, :] for b in range(t // CHUNK))

    rev = lambda i: (n_tiles - 1 - i, 0)
    tile = pl.BlockSpec((t, D_MODEL), rev)
    ztile = pl.BlockSpec((t, IN_WIDTH), rev)
    zhalo = pl.BlockSpec((t, IN_WIDTH), lambda i: (jnp.maximum(n_tiles - 2 - i, 0), 0))
    consts = (modv, g1024, p384, cw, sguw, bmat, pwbd, psc, segp, win, wout)
    acc = lambda shape: pl.BlockSpec(shape, lambda i: (0,) * len(shape))
    acc_shapes = [(8, D_MODEL), (8, SGU_WIDTH), (32, CONV_WIDTH), (SGU_HEADS, CHUNK, CHUNK), (CHUNK, SGU_WIDTH),
                  (POOL_WIDTH, POOL_WIDTH), (8, POOL_WIDTH)]
    return _pallas_call_with_exchange(
        body,
        grid=(n_tiles,),
        in_specs=[tile, tile, tile, ztile, zhalo, pl.BlockSpec((t, CONV_WIDTH), rev)] + [_whole(c.shape) for c in consts],
        out_specs=[tile, ztile, tile, tile, tile] + [acc(s) for s in acc_shapes],
        out_shape=[jax.ShapeDtypeStruct((s_len, D_MODEL), F32), jax.ShapeDtypeStruct((s_len, IN_WIDTH), BF16),
                   jax.ShapeDtypeStruct((s_len, D_MODEL), BF16), jax.ShapeDtypeStruct((s_len, D_MODEL), BF16),
                   jax.ShapeDtypeStruct((s_len, D_MODEL), BF16)] + [jax.ShapeDtypeStruct(s, F32) for s in acc_shapes],
        scratch_shapes=[pltpu.VMEM((2 * t, CONV_WIDTH), F32), pltpu.VMEM((POOL_HALO + t, POOL_WIDTH), F32),
                        pltpu.VMEM((2 * t, CONV_WIDTH), F32), pltpu.VMEM((t + POOL_HALO, POOL_WIDTH), F32),
                        pltpu.VMEM((t, CONV_WIDTH), F32), pltpu.VMEM((POOL_HALO, POOL_WIDTH), F32),
                        pltpu.VMEM((SGU_HEADS, t, t), BF16), pltpu.VMEM((SGU_HEADS, t, t), F32),
                        pltpu.VMEM((t, SGU_WIDTH), F32)],
        operands=(dx1, x, o, z, z, conv, *consts),
        name=name, job=job)


MOD_SHARD = 6 * D_MODEL // N_DEV

ROW_DMOD = 0
ROW_G1024 = 8
ROW_V384 = 16
ROW_SGU_B = 24
ROW_POOL_SCALE = 25
ROW_LOSS = 26
ROW_CONV_W = 32
ROW_FFN_CONV = 64
ROW_POOL_W = 96
ROW_SGU_W = 112
ROWS_PER_LAYER = 208
N_LAYERS = 2


def _gather_weights(c8, mod_w, mod_b8, job):
    kinds = [kind for kind, _ in job]
    shards = [a for _, a in job]
    n = len(shards)

    def body(c_ref, modw_ref, modb_ref, *rest):
        shard_refs = rest[:n]
        sc_all_ref, modrows_ref = rest[n], rest[n + 1]
        full_refs = rest[n + 2:2 * n + 2]
        send_buf, mod_recv, w_send, w_recv, w_local, sc_send, sc_recv, mod_send, mod_recv_sem = rest[2 * n + 2:]
        pos = _my_pos()
        me = _flat(pos)
        peers = [_peer(pos, k) for k in range(1, N_DEV)]

        w_copies = _exchange_copies(kinds, shard_refs, full_refs, w_send, w_recv, w_local, phase=0)
        for cp in w_copies:
            cp.start()

        cv = c_ref[...]
        sc_all_ref[me] = cv * jax.nn.sigmoid(cv)
        sc_copies = [_remote_copy(sc_all_ref.at[me], sc_all_ref.at[me], sc_send.at[k], sc_recv.at[k], peers[k])
                     for k in range(N_PEERS)]
        for cp in sc_copies:
            cp.start()
        for cp in sc_copies:
            cp.wait()

        sc = jnp.concatenate([sc_all_ref[j, 0:1, :] for j in range(N_DEV)], axis=0)
        send_buf[...] = jnp.zeros_like(send_buf)
        for l in range(N_LAYERS):
            part = jnp.dot(sc, modw_ref[l], precision=lax.Precision.HIGHEST, preferred_element_type=F32)
            for j in range(N_DEV):
                send_buf[j, l:l + 1, :] = part[j:j + 1, :]
        mod_recv[me] = send_buf[me]
        mod_copies = [_remote_copy(send_buf.at[_flat(peers[k])], mod_recv.at[me], mod_send.at[k], mod_recv_sem.at[k],
                                   peers[k]) for k in range(N_PEERS)]
        for cp in mod_copies:
            cp.start()
        for cp in mod_copies:
            cp.wait()
        modrows_ref[...] = jnp.zeros_like(modrows_ref)
        for l in range(N_LAYERS):
            row = jnp.concatenate([mod_recv[j, l:l + 1, :] for j in range(N_DEV)], axis=1)
            modrows_ref[l:l + 1, :] = row + modb_ref[l:l + 1, :]

        for cp in w_copies:
            cp.wait()
        relays = _exchange_copies(kinds, shard_refs, full_refs, w_send, w_recv, w_local, phase=1)
        for cp in relays:
            cp.start()
        for cp in relays:
            cp.wait()

    out_shape = ([jax.ShapeDtypeStruct((N_DEV, 8, D_MODEL), F32), jax.ShapeDtypeStruct((8, 6 * D_MODEL), F32)]
                 + [jax.ShapeDtypeStruct((N_DEV,) + s.shape, s.dtype) for s in shards])
    return pl.pallas_call(
        body,
        in_specs=[VMEM, VMEM, VMEM] + [ANY] * n,
        out_specs=[VMEM, VMEM] + [ANY] * n,
        out_shape=out_shape,
        scratch_shapes=[pltpu.VMEM((N_DEV, 8, MOD_SHARD), F32), pltpu.VMEM((N_DEV, 8, MOD_SHARD), F32),
                        pltpu.SemaphoreType.DMA((n, N_PEERS)), pltpu.SemaphoreType.DMA((n, N_PEERS)),
                        pltpu.SemaphoreType.DMA((n,)),
                        pltpu.SemaphoreType.DMA((N_PEERS,)), pltpu.SemaphoreType.DMA((N_PEERS,)),
                        pltpu.SemaphoreType.DMA((N_PEERS,)), pltpu.SemaphoreType.DMA((N_PEERS,))],
        compiler_params=pltpu.CompilerParams(vmem_limit_bytes=VMEM_LIMIT_BYTES),
        name="gather_weights",
    )(c8, mod_w, mod_b8, *shards)


def _small_sums(sc_all, small_all0, small_all1, job):
    kinds = [kind for kind, _ in job]
    n = len(job)

    def body(sc_all_ref, small_all0_ref, small_all1_ref, *rest):
        small_sum_ref, gmodw_ref = rest[n], rest[n + 1]
        copies = _exchange_copies(kinds, rest[:n], rest[n + 2:2 * n + 2], *rest[2 * n + 2:], phase=0)
        for cp in copies:
            cp.start()
        me = _flat(_my_pos())
        sc = jnp.concatenate([sc_all_ref[j, 0:1, :] for j in range(N_DEV)], axis=0)
        mine = lax.broadcasted_iota(jnp.int32, (2 * N_DEV, MOD_SHARD), 0) == me
        for l, parts in enumerate((small_all0_ref, small_all1_ref)):
            total = parts[0].astype(F32)
            for j in range(1, N_DEV):
                total = total + parts[j].astype(F32)
            small_sum_ref[l] = total
            dm = jnp.concatenate(
                [jnp.sum(jnp.where(mine, parts[j, ROW_DMOD:ROW_DMOD + 2 * N_DEV, 0:MOD_SHARD].astype(F32), 0.0),
                         axis=0, keepdims=True) for j in range(N_DEV)], axis=0)
            gmodw_ref[l] = lax.dot_general(sc, dm, (((0,), (0,)), ((), ())), precision=lax.Precision.HIGHEST,
                                           preferred_element_type=F32)
        for cp in copies:
            cp.wait()

    res = pl.pallas_call(
        body,
        in_specs=[VMEM, VMEM, VMEM] + [ANY] * n,
        out_specs=[VMEM, VMEM] + [ANY] * n,
        out_shape=[jax.ShapeDtypeStruct((N_LAYERS, ROWS_PER_LAYER, D_MODEL), F32),
                   jax.ShapeDtypeStruct((N_LAYERS, D_MODEL, MOD_SHARD), F32)] + _exchange_out_shapes(job),
        scratch_shapes=_exchange_sems(n),
        compiler_params=pltpu.CompilerParams(vmem_limit_bytes=VMEM_LIMIT_BYTES),
        name="small_sums",
    )(sc_all, small_all0, small_all1, *[a for _, a in job])
    return res[0], res[1], res[2:]


def _adam_update(g, w, m, v):
    m2 = ADAM_B1 * m + (1.0 - ADAM_B1) * g
    v2 = ADAM_B2 * v + (1.0 - ADAM_B2) * (g * g)
    m_hat = m2 / (1.0 - ADAM_B1 ** ADAM_STEP)
    v_hat = v2 / (1.0 - ADAM_B2 ** ADAM_STEP)
    delta = -ADAM_LR * (m_hat / (jnp.sqrt(v_hat) + ADAM_EPS) + ADAM_WD * w)
    return delta, m2, v2


def _pair_add(g, r1, row_chunk, name):
    _, rows, cols = g.shape
    core = lax.axis_index("c").astype(jnp.int32).reshape(1)

    def body(core_ref, g_ref, r_ref, o_ref):
        o_ref[0] = (g_ref[0, 0].astype(F32) + r_ref[0].astype(F32)).astype(BF16)

    blk = pl.BlockSpec((1, row_chunk, cols), lambda q, i, core_ref: (q, i, 0))
    grid_spec = pltpu.PrefetchScalarGridSpec(
        num_scalar_prefetch=1, grid=(N_CHIPS, rows // row_chunk),
        in_specs=[pl.BlockSpec((1, 1, row_chunk, cols), lambda q, i, core_ref: (q, core_ref[0], i, 0)), blk],
        out_specs=blk)
    return pl.pallas_call(
        body, grid_spec=grid_spec, out_shape=jax.ShapeDtypeStruct((N_CHIPS, rows, cols), BF16),
        compiler_params=_cparams(2), name=name,
    )(core, g.reshape(N_CHIPS, 2, rows, cols), r1)


def _adam_sharded(recv0, recv1, w, m, v, row_chunk, name):
    _, rows, cols = w.shape
    n_chunks = rows // row_chunk

    def body(r0_ref, r1_ref, w_ref, m_ref, v_ref, g_ref, d_ref, m2_ref, v2_ref):
        layer = pl.program_id(0)

        def run(r_ref):
            g = r_ref[0].astype(F32)
            for j in range(1, r_ref.shape[0]):
                g = g + r_ref[j].astype(F32)
            delta, m2, v2 = _adam_update(g, w_ref[0], m_ref[0], v_ref[0])
            g_ref[0], d_ref[0], m2_ref[0], v2_ref[0] = g, delta, m2, v2

        @pl.when(layer == 0)
        def _():
            run(r0_ref)

        @pl.when(layer == 1)
        def _():
            run(r1_ref)

    r0_spec = pl.BlockSpec((recv0.shape[0], row_chunk, cols), lambda l, i: (0, i * (1 - l) + (n_chunks - 1) * l, 0))
    r1_spec = pl.BlockSpec((recv1.shape[0], row_chunk, cols), lambda l, i: (0, i * l, 0))
    blk = pl.BlockSpec((1, row_chunk, cols), lambda l, i: (l, i, 0))
    out = jax.ShapeDtypeStruct(w.shape, F32)
    return pl.pallas_call(
        body,
        grid=(N_LAYERS, n_chunks),
        in_specs=[r0_spec, r1_spec, blk, blk, blk],
        out_specs=[blk] * 4,
        out_shape=[out] * 4,
        compiler_params=_cparams(2),
        name=name,
    )(recv0, recv1, w, m, v)


def _adam_dense(g, w, m, v, row_chunk, name):
    n_lead, rows, cols = w.shape

    def body(g_ref, w_ref, m_ref, v_ref, go_ref, d_ref, m2_ref, v2_ref):
        gv = g_ref[...]
        go_ref[...] = gv
        d_ref[...], m2_ref[...], v2_ref[...] = _adam_update(gv, w_ref[...], m_ref[...], v_ref[...])

    blk = pl.BlockSpec((1, row_chunk, cols), lambda l, i: (l, i, 0))
    out = jax.ShapeDtypeStruct(w.shape, F32)
    return pl.pallas_call(
        body,
        grid=(n_lead, rows // row_chunk),
        in_specs=[blk] * 4,
        out_specs=[blk] * 4,
        out_shape=[out] * 4,
        compiler_params=_cparams(2),
        name=name,
    )(g, w, m, v)


WEIGHT_NAMES = ("mod_w", "mod_b", "mix_pre_g", "mix_post_g", "w_in", "sgu_norm_g", "sgu_norm_b", "sgu_w", "sgu_b",
                "conv_w", "conv_b", "conv_norm_g", "conv_norm_b", "pool_w", "pool_scale", "branch_g", "w_out",
                "ffn_pre_g", "ffn_post_g", "ffn_up", "ffn_conv_w", "ffn_conv_b", "ffn_down")
SHARDED_BIG = ("w_in", "w_out", "ffn_up", "ffn_down")
SMALL_PACKED = tuple(n for n in WEIGHT_NAMES if n not in SHARDED_BIG + ("mod_w",))


def _rows8(rows, width=D_MODEL):
    out = [jnp.pad(r.astype(F32), (0, width - r.shape[0]))[None] for r in rows]
    out.append(jnp.zeros((8 - len(rows), width), F32))
    return jnp.concatenate(out, axis=0)


def _as_rows(a, width=D_MODEL):
    flat = a.astype(F32).reshape(-1)
    pad = (-flat.shape[0]) % width
    return jnp.pad(flat, (0, pad)).reshape(-1, width)


def _pad_cols(a, width=D_MODEL):
    return jnp.pad(a.astype(F32), ((0, 0), (0, width - a.shape[1])))


def _pack_rows(arrays):
    rows = jnp.concatenate([_as_rows(a) for a in arrays], axis=0)
    return jnp.pad(rows, ((0, (-rows.shape[0]) % 8), (0, 0)))


def _unpack_rows(packed, shapes):
    out, r = [], 0
    for shape in shapes:
        size = math.prod(shape)
        n_rows = -(-size // D_MODEL)
        out.append(packed[r:r + n_rows].reshape(-1)[:size].reshape(shape))
        r += n_rows
    return out


TILE_VREGS = MIX_TILE // 8
CHUNK_SUBLANES = CHUNK // TILE_VREGS
CHUNKS_PER_TILE = MIX_TILE // CHUNK


def _chunk_axis_to_tile(a, axis):
    shape = a.shape
    a = a.reshape(shape[:axis] + (CHUNK_SUBLANES, TILE_VREGS) + shape[axis + 1:])
    a = jnp.swapaxes(a, axis, axis + 1)
    a = jnp.tile(a, (1,) * (axis + 1) + (CHUNKS_PER_TILE,) + (1,) * (len(shape) - axis - 1))
    return a.reshape(shape[:axis] + (MIX_TILE,) + shape[axis + 1:])


def _layer_consts(l, w, mod_rows, win, wout, conv_w_full):
    modv = _rows8(list(mod_rows[l].reshape(6, D_MODEL)))
    g1024 = _rows8([w["mix_pre_g"][l], w["mix_post_g"][l], w["branch_g"][l], w["ffn_pre_g"][l], w["ffn_post_g"][l]])
    p384 = _rows8([w["sgu_norm_g"][l], w["sgu_norm_b"][l], w["conv_b"][l], w["conv_norm_g"][l], w["conv_norm_b"][l]],
                  SGU_WIDTH)
    cw = jnp.pad(conv_w_full[l], ((0, 32 - CONV_K), (0, 0)))
    sguw = _chunk_axis_to_tile(_chunk_axis_to_tile(w["sgu_w"][l], 2), 1)
    bmat = _chunk_axis_to_tile(jnp.repeat(w["sgu_b"][l].T, HEAD_DIM, axis=1), 0)
    groups = len(POOL_WINDOWS)
    eye = jnp.eye(groups, dtype=F32)
    pwbd = (eye[:, None, :, None] * w["pool_w"][l][:, :, None, :]).reshape(POOL_WIDTH, POOL_WIDTH).astype(BF16)
    psc = _rows8([w["pool_scale"][l]], POOL_WIDTH)
    seg = jnp.arange(SGU_WIDTH) // HEAD_DIM
    segp = jnp.where(seg[:, None] == seg[None, :], 1.0 / HEAD_DIM, 0.0).astype(BF16)
    return modv, g1024, (modv, g1024, p384, cw, sguw, bmat, pwbd, psc, segp, win, wout)


def _small_grad_rows(mix, ffn, loss_pieces=None):
    _, _, _, _, _, mvec, v384, dcw, dsguw, dbmat, dpw, dpsc = mix
    fvec, cgrad = ffn[5], ffn[6]
    dmod = jnp.stack([mvec[2], mvec[3], mvec[1], fvec[2], fvec[3], fvec[1]]).reshape(N_DEV, MOD_SHARD)
    g_rows = jnp.stack([mvec[4], mvec[0], mvec[5], fvec[4], fvec[0]])
    dsgu_b = dbmat[:, ::HEAD_DIM].T.reshape(1, SGU_HEADS * CHUNK)
    groups = len(POOL_WINDOWS)
    gdim = POOL_WIDTH // groups
    dpw4 = dpw.reshape(groups, gdim, groups, gdim)
    dpool = jnp.stack([dpw4[g, :, g, :] for g in range(groups)])
    misc = [dsgu_b[0], dpsc[0]] + ([] if loss_pieces is None else [loss_pieces])
    blocks = [_pad_cols(dmod), _rows8(list(g_rows)), _pad_cols(v384), _rows8(misc), _pad_cols(dcw),
              _pad_cols(cgrad[:, 0:4, :].reshape(4 * N_DEV, FF_SHARD)), _as_rows(dpool), _as_rows(dsguw)]
    return jnp.concatenate(blocks, axis=0)


def _small_grads_from_rows(total):
    per = {n: [] for n in SMALL_PACKED}
    for l in range(N_LAYERS):
        s = total[l]
        per["mod_b"].append(s[ROW_DMOD:ROW_DMOD + N_DEV, :MOD_SHARD].reshape(6 * D_MODEL))
        for j, name in enumerate(("mix_pre_g", "mix_post_g", "branch_g", "ffn_pre_g", "ffn_post_g")):
            per[name].append(s[ROW_G1024 + j])
        for j, name in enumerate(("sgu_norm_g", "sgu_norm_b", "conv_b", "conv_norm_g", "conv_norm_b")):
            per[name].append(s[ROW_V384 + j, :SGU_WIDTH])
        per["sgu_b"].append(s[ROW_SGU_B, :SGU_HEADS * CHUNK].reshape(SGU_HEADS, CHUNK))
        per["pool_scale"].append(s[ROW_POOL_SCALE, :POOL_WIDTH])
        per["conv_w"].append(s[ROW_CONV_W:ROW_CONV_W + CONV_K, :CONV_WIDTH])
        fc = s[ROW_FFN_CONV:ROW_FFN_CONV + 4 * N_DEV, :FF_SHARD].reshape(N_DEV, 4, FF_SHARD)
        per["ffn_conv_w"].append(fc[:, 0:3, :].transpose(1, 0, 2).reshape(FFN_CONV_K, 2 * D_FF))
        per["ffn_conv_b"].append(fc[:, 3, :].reshape(2 * D_FF))
        per["pool_w"].append(s[ROW_POOL_W:ROW_POOL_W + 16].reshape(len(POOL_WINDOWS), HEAD_DIM, HEAD_DIM))
        per["sgu_w"].append(s[ROW_SGU_W:ROW_SGU_W + 96].reshape(SGU_HEADS, CHUNK, CHUNK))
    return {n: jnp.stack(v) for n, v in per.items()}


def kernel(x, c, mod_w, mod_b, mix_pre_g, mix_post_g, w_in, sgu_norm_g, sgu_norm_b, sgu_w, sgu_b, conv_w, conv_b, conv_norm_g, conv_norm_b, pool_w, pool_scale, branch_g, w_out, ffn_pre_g, ffn_post_g, ffn_up, ffn_conv_w, ffn_conv_b, ffn_down, loss_target, m_mod_w, m_mod_b, m_mix_pre_g, m_mix_post_g, m_w_in, m_sgu_norm_g, m_sgu_norm_b, m_sgu_w, m_sgu_b, m_conv_w, m_conv_b, m_conv_norm_g, m_conv_norm_b, m_pool_w, m_pool_scale, m_branch_g, m_w_out, m_ffn_pre_g, m_ffn_post_g, m_ffn_up, m_ffn_conv_w, m_ffn_conv_b, m_ffn_down, v_mod_w, v_mod_b, v_mix_pre_g, v_mix_post_g, v_w_in, v_sgu_norm_g, v_sgu_norm_b, v_sgu_w, v_sgu_b, v_conv_w, v_conv_b, v_conv_norm_g, v_conv_norm_b, v_pool_w, v_pool_scale, v_branch_g, v_w_out, v_ffn_pre_g, v_ffn_post_g, v_ffn_up, v_ffn_conv_w, v_ffn_conv_b, v_ffn_down):
    w = dict(zip(WEIGHT_NAMES, (mod_w, mod_b, mix_pre_g, mix_post_g, w_in, sgu_norm_g, sgu_norm_b, sgu_w, sgu_b, conv_w,
                                conv_b, conv_norm_g, conv_norm_b, pool_w, pool_scale, branch_g, w_out, ffn_pre_g,
                                ffn_post_g, ffn_up, ffn_conv_w, ffn_conv_b, ffn_down)))
    m = dict(zip(WEIGHT_NAMES, (m_mod_w, m_mod_b, m_mix_pre_g, m_mix_post_g, m_w_in, m_sgu_norm_g, m_sgu_norm_b, m_sgu_w,
                                m_sgu_b, m_conv_w, m_conv_b, m_conv_norm_g, m_conv_norm_b, m_pool_w, m_pool_scale,
                                m_branch_g, m_w_out, m_ffn_pre_g, m_ffn_post_g, m_ffn_up, m_ffn_conv_w, m_ffn_conv_b,
                                m_ffn_down)))
    v = dict(zip(WEIGHT_NAMES, (v_mod_w, v_mod_b, v_mix_pre_g, v_mix_post_g, v_w_in, v_sgu_norm_g, v_sgu_norm_b, v_sgu_w,
                                v_sgu_b, v_conv_w, v_conv_b, v_conv_norm_g, v_conv_norm_b, v_pool_w, v_pool_scale,
                                v_branch_g, v_w_out, v_ffn_pre_g, v_ffn_post_g, v_ffn_up, v_ffn_conv_w, v_ffn_conv_b,
                                v_ffn_down)))
    me = _flat(_my_pos())
    xs = x[0]

    transposed = ("w_in", "ffn_up")
    wt = {n: jnp.swapaxes(w[n], 1, 2) if n in transposed else w[n] for n in SHARDED_BIG}
    mt = {n: jnp.swapaxes(m[n], 1, 2) if n in transposed else m[n] for n in SHARDED_BIG}
    vt = {n: jnp.swapaxes(v[n], 1, 2) if n in transposed else v[n] for n in SHARDED_BIG}
    bf16_shards = [[wt[n][l].astype(BF16) for n in SHARDED_BIG] for l in range(N_LAYERS)]

    def mixer_operands(l, win_g, wout_g):
        win = win_g.reshape(IN_WIDTH, D_MODEL)
        return _layer_consts(l, w, mod_rows, win, wout_g.reshape(D_MODEL, D_MODEL), conv_w_full)

    def ffn_operands(l, modv, g1024, wup_g, wdn_g):
        wdn = wdn_g.reshape(FF_PAIRS, FF_SHARD, D_MODEL)
        return modv, g1024, wup_g, wdn, ffn_cw_full[:, l], ffn_conv_b[l].reshape(N_DEV, 1, FF_SHARD)

    w0, w1 = bf16_shards
    c8 = jnp.broadcast_to(c, (8, D_MODEL))
    mod_b8 = jnp.pad(mod_b, ((0, 8 - N_LAYERS), (0, 0)))
    sc_all, mod_rows, win0_g, wout0_g, conv_w_g, ffn_cw_full = _gather_weights(
        c8, mod_w, mod_b8, [("gather2", w0[0]), ("gather2", w0[1]), ("gather", conv_w), ("gather", ffn_conv_w)])
    conv_w_full = conv_w_g.transpose(1, 2, 0, 3).reshape(N_LAYERS, CONV_K, CONV_WIDTH)

    modv0, g0, mix_consts0 = mixer_operands(0, win0_g, wout0_g)
    (x1, z, o, cv), (wup0_g, wdn0_g) = _mixer_fwd(xs, *mix_consts0, name="mixer_fwd_l0", natural_x=True,
                                                  job=[("gather2", w0[2]), ("gather2", w0[3])])
    ffn_consts0 = ffn_operands(0, modv0, g0, wup0_g, wdn0_g)
    (x2, y2, p, u), (win1_g, wout1_g, wdn1_g) = _ffn_fwd(
        x1, *ffn_consts0, name="ffn_fwd_l0", job=[("gather2", w1[0]), ("gather2", w1[1]), ("gather2", w1[3])])
    saved = [(xs, z, o, cv, x1, y2, p, u)]
    modv1, g1, mix_consts1 = mixer_operands(1, win1_g, wout1_g)
    (x1, z, o, cv), (wup1_g,) = _mixer_fwd(x2, *mix_consts1, name="mixer_fwd_l1", job=[("gather2", w1[2])])
    ffn_consts1 = ffn_operands(1, modv1, g1, wup1_g, wdn1_g)
    (dh, y2, p, u, loss_tile), _ = _ffn_fwd(x1, *ffn_consts1, name="ffn_fwd_l1", loss_target=loss_target[0])
    saved.append((x2, z, o, cv, x1, y2, p, u))
    loss_hi = loss_tile[0, 0].astype(BF16).astype(F32)
    loss_mid = (loss_tile[0, 0] - loss_hi).astype(BF16).astype(F32)
    loss_pieces = jnp.stack([loss_hi, loss_mid, loss_tile[0, 0] - loss_hi - loss_mid])

    def ffn_weight_grads(l, ffn):
        dp, a, dy2, h2 = ffn[1:5]
        d_up = _wgrad(dp, h2, f"wgrad_ffn_up_l{l}", tk=WGRAD_TK_FFN)
        d_dn = _wgrad(a, dy2, f"wgrad_ffn_down_l{l}", tk=WGRAD_TK_FFN).reshape(N_DEV, D_FF // N_DEV, D_MODEL)
        return d_up, d_dn

    def mixer_weight_grads(l, mix):
        dz, do, ycat, h1 = mix[1:5]
        d_in = _wgrad(dz[None], h1, f"wgrad_w_in_l{l}").reshape(N_DEV, IN_WIDTH // N_DEV, D_MODEL)
        d_out = _wgrad(ycat, do[None], f"wgrad_w_out_l{l}").reshape(N_DEV, D_MODEL // N_DEV, D_MODEL)
        return d_in, d_out

    x_in, z, o, cv, x1, y2, p, u = saved[1]
    ffn1, _ = _ffn_bwd(dh, x1, y2, p, u, *ffn_consts1[:-1], name="ffn_bwd_l1")
    d_up1, d_dn1 = ffn_weight_grads(1, ffn1)
    mix1, (sib_up1, sib_dn1) = _mixer_bwd(ffn1[0], x_in, o, z, cv, *mix_consts1, name="mixer_bwd_l1",
                                          job=[("scatter_p1", d_up1), ("scatter_p1", d_dn1)])
    chip_up1 = _pair_add(d_up1, sib_up1, 176, "pair_add_ffn_up_l1")
    chip_dn1 = _pair_add(d_dn1, sib_dn1, 176, "pair_add_ffn_down_l1")
    d_in1, d_out1 = mixer_weight_grads(1, mix1)
    small1 = _small_grad_rows(mix1, ffn1).astype(BF16)

    x_in, z, o, cv, x1, y2, p, u = saved[0]
    ffn0, job_out = _ffn_bwd(mix1[0], x1, y2, p, u, *ffn_consts0[:-1], name="ffn_bwd_l0",
                             job=[("scatter", d_in1), ("scatter", d_out1), ("scatter_p2", chip_up1),
                                  ("scatter_p2", chip_dn1)])
    recv1 = job_out
    dp, a, dy2, h2 = ffn0[1:5]
    d_dn0, (small_all1,) = _wgrad(a, dy2, "wgrad_ffn_down_l0", tk=WGRAD_TK_FFN, job=[("gather2", small1)])
    d_dn0 = d_dn0.reshape(N_DEV, D_FF // N_DEV, D_MODEL)
    d_up0, (recv_dn0,) = _wgrad(dp, h2, "wgrad_ffn_up_l0", tk=WGRAD_TK_FFN, job=[("scatter", d_dn0)])
    mix0, (recv_up0,) = _mixer_bwd(ffn0[0], x_in, o, z, cv, *mix_consts0, name="mixer_bwd_l0", natural_x=True,
                                   job=[("scatter", d_up0)])
    recv_ffn0 = (recv_up0, recv_dn0)
    grad_x = mix0[0][None]
    dz, do, ycat, h1 = mix0[1:5]
    small0 = _small_grad_rows(mix0, ffn0, loss_pieces).astype(BF16)
    d_in0, (small_all0,) = _wgrad(dz[None], h1, "wgrad_w_in_l0", tk=WGRAD_TK // 2, job=[("gather2", small0)])
    d_in0 = d_in0.reshape(N_DEV, IN_WIDTH // N_DEV, D_MODEL)
    d_out0, (recv_in0,) = _wgrad(ycat, do[None], "wgrad_w_out_l0", tk=WGRAD_TK // 2, job=[("scatter", d_in0)])
    d_out0 = d_out0.reshape(N_DEV, D_MODEL // N_DEV, D_MODEL)
    small_total, g_mod_w, (recv_out0,) = _small_sums(sc_all, small_all0, small_all1, [("scatter", d_out0)])
    recv0 = [recv_in0, recv_out0, recv_ffn0[0], recv_ffn0[1]]
    loss = small_total[0, ROW_LOSS, 0] + small_total[0, ROW_LOSS, 1] + small_total[0, ROW_LOSS, 2]

    grads, deltas, new_m, new_v = {}, {}, {}, {}
    for j, (name, chunk) in enumerate((("w_in", 224), ("w_out", 128), ("ffn_up", 176), ("ffn_down", 176))):
        outs = _adam_sharded(recv0[j], recv1[j], wt[name], mt[name], vt[name], chunk, "adam_" + name)
        if name in transposed:
            outs = [jnp.swapaxes(t, 1, 2) for t in outs]
        grads[name], deltas[name], new_m[name], new_v[name] = outs
    grads["mod_w"], deltas["mod_w"], new_m["mod_w"], new_v["mod_w"] = _adam_dense(
        g_mod_w, mod_w, m_mod_w, v_mod_w, 256, "adam_mod_w")

    small_g = _small_grads_from_rows(small_total)
    small_g["conv_w"] = lax.dynamic_slice_in_dim(small_g["conv_w"], me * conv_w.shape[2], conv_w.shape[2], axis=2)
    small_g["ffn_conv_w"] = lax.dynamic_slice_in_dim(small_g["ffn_conv_w"], me * FF_SHARD, FF_SHARD, axis=2)
    shapes = [w[n].shape for n in SMALL_PACKED]
    packs = [_pack_rows([src[n] for n in SMALL_PACKED])[None] for src in (small_g, w, m, v)]
    _, d, m2, v2 = _adam_dense(*packs, packs[0].shape[1], "adam_small")
    for name, dd, mm, vv in zip(SMALL_PACKED, _unpack_rows(d[0], shapes), _unpack_rows(m2[0], shapes),
                                _unpack_rows(v2[0], shapes)):
        grads[name], deltas[name], new_m[name], new_v[name] = small_g[name], dd, mm, vv

    return (loss, grad_x, *[grads[n] for n in WEIGHT_NAMES], *[deltas[n] for n in WEIGHT_NAMES],
            *[new_m[n] for n in WEIGHT_NAMES], *[new_v[n] for n in WEIGHT_NAMES])
```

```python
import math

import jax
import jax.numpy as jnp
from jax import lax
from jax.experimental import pallas as pl
from jax.experimental.pallas import tpu as pltpu

F32 = jnp.float32
BF16 = jnp.bfloat16

D_MODEL = 1024
N_DEV = 8
SGU_WIDTH = 384
CONV_WIDTH = 384
POOL_WIDTH = 256
HEAD_DIM = 64
SGU_HEADS = 6
CHUNK = 128
CONV_K = 31
POOL_WINDOWS = (2, 4, 8, 16)
IN_WIDTH = 1792
D_FF = 2816
FF_SHARD = 2 * D_FF // N_DEV
FF_PAIRS = N_DEV // 2
FFN_CONV_K = 3
EPS = 1e-6
GELU_C0 = math.sqrt(2.0 / math.pi)
GELU_C1 = 0.044715

ADAM_LR = 0.001
ADAM_B1 = 0.9
ADAM_B2 = 0.999
ADAM_EPS = 1e-08
ADAM_WD = 0.01
ADAM_STEP = 10

VMEM_LIMIT_BYTES = 56 * 1024 * 1024
TILE = 256
MIX_TILE = TILE
FFN_TILE = TILE
FFN_HALO = 8 * (FFN_CONV_K - 1)
POOL_HALO = 8 * POOL_WINDOWS[-1]
WGRAD_TK = 2048
WGRAD_TK_FFN = 4096


def _cparams(n_axes):
    return pltpu.CompilerParams(dimension_semantics=("arbitrary",) * n_axes, vmem_limit_bytes=VMEM_LIMIT_BYTES)


def _whole(shape):
    nd = len(shape)
    return pl.BlockSpec(shape, lambda *_: (0,) * nd, pipeline_mode=pl.Buffered(1))


def _dot(a, b):
    return jnp.dot(a, b, preferred_element_type=F32)


def _dot_nt(a, b):
    return lax.dot_general(a, b, (((1,), (1,)), ((), ())), preferred_element_type=F32)


def _dot_tn(a, b):
    return lax.dot_general(a, b, (((0,), (0,)), ((), ())), preferred_element_type=F32)


def _gelu(x):
    t = jnp.tanh(GELU_C0 * (x + GELU_C1 * x * x * x))
    return 0.5 * x * (1.0 + t), t


def _gelu_grad(x, t):
    return 0.5 * (1.0 + t) + 0.5 * x * (1.0 - t * t) * (GELU_C0 * (1.0 + 3.0 * GELU_C1 * x * x))


def _rowmean(x):
    return jnp.mean(x, axis=-1, keepdims=True)


def _colsum(x):
    return jnp.sum(x, axis=0, keepdims=True)


def _rms_fwd(x):
    r = lax.rsqrt(_rowmean(x * x) + EPS)
    return x * r, r


def _rms_bwd(dxhat, xhat, r):
    return r * (dxhat - xhat * _rowmean(dxhat * xhat))


N_PEERS = N_DEV - 1
ANY = pl.BlockSpec(memory_space=pl.ANY)
VMEM = pl.BlockSpec(memory_space=pltpu.VMEM)


def _my_pos():
    return lax.axis_index("x"), lax.axis_index("y"), lax.axis_index("c")


def _peer(pos, k):
    x, y, c = pos
    return (1 - x if k & 4 else x, 1 - y if k & 2 else y, 1 - c if k & 1 else c)


def _flat(pos):
    return 4 * pos[0] + 2 * pos[1] + pos[2]


def _remote_copy(src, dst, send_sem, recv_sem, peer):
    return pltpu.make_async_remote_copy(src_ref=src, dst_ref=dst, send_sem=send_sem, recv_sem=recv_sem,
                                        device_id=peer, device_id_type=pl.DeviceIdType.MESH)


N_CHIPS = N_DEV // 2
SIBLING = 1
SAME_CORE_PEERS = (2, 4, 6)


def _exchange_out_shapes(job):
    def shape(kind, a):
        if kind in ("gather", "gather2"):
            return (N_DEV,) + a.shape
        if kind == "scatter_p1":
            return (N_CHIPS,) + a.shape[1:]
        return a.shape
    return [jax.ShapeDtypeStruct(shape(kind, a), a.dtype) for kind, a in job]


def _exchange_sems(n):
    return [pltpu.SemaphoreType.DMA((n, N_PEERS)), pltpu.SemaphoreType.DMA((n, N_PEERS)), pltpu.SemaphoreType.DMA((n,))]


def _exchange_copies(kinds, src_refs, dst_refs, send_sems, recv_sems, local_sems, phase):
    pos = _my_pos()
    me = _flat(pos)
    chip, core = 2 * pos[0] + pos[1], pos[2]
    copies = []

    def remote(a, src, dst, k, sem=None):
        sem = k - 1 if sem is None else sem
        copies.append(_remote_copy(src, dst, send_sems.at[a, sem], recv_sems.at[a, sem], _peer(pos, k)))

    for a, kind in enumerate(kinds):
        src, dst = src_refs[a], dst_refs[a]
        if phase == 1:
            if kind == "gather2":
                for k in SAME_CORE_PEERS:
                    remote(a, dst.at[me ^ k], dst.at[me ^ k], SIBLING, sem=k)
        elif kind in ("gather", "gather2"):
            copies.append(pltpu.make_async_copy(src, dst.at[me], local_sems.at[a]))
            for k in (range(1, N_DEV) if kind == "gather" else (SIBLING,) + SAME_CORE_PEERS):
                remote(a, src, dst.at[me], k)
        elif kind == "scatter":
            copies.append(pltpu.make_async_copy(src.at[me], dst.at[me], local_sems.at[a]))
            for k in range(1, N_DEV):
                remote(a, src.at[me ^ k], dst.at[me], k)
        elif kind == "scatter_p1":
            for q in range(N_CHIPS):
                remote(a, src.at[2 * q + 1 - core], dst.at[q], SIBLING, sem=q)
        elif kind == "scatter_p2":
            copies.append(pltpu.make_async_copy(src.at[chip], dst.at[chip], local_sems.at[a]))
            for k in SAME_CORE_PEERS:
                remote(a, src.at[chip ^ (k >> 1)], dst.at[chip], k)
    return copies


def _pallas_call_with_exchange(body, *, grid, in_specs, out_specs, out_shape, scratch_shapes, operands, name, job,
                               relay_steps_before_end=2):
    params = _cparams(len(grid))
    if not job:
        outs = pl.pallas_call(body, grid=grid, in_specs=in_specs, out_specs=out_specs, out_shape=out_shape,
                              scratch_shapes=scratch_shapes, compiler_params=params, name=name)(*operands)
        return outs, []
    kinds = [kind for kind, _ in job]
    relayed = [kind if kind == "gather2" else None for kind in kinds]
    unrelayed = [None if kind == "gather2" else kind for kind in kinds]
    n, n_in, n_out, n_scr = len(job), len(in_specs), len(out_specs), len(scratch_shapes)
    n_steps = math.prod(grid)
    relay_step = max(n_steps - relay_steps_before_end, 0)

    def wrapped(*refs):
        ins, jin = refs[:n_in], refs[n_in:n_in + n]
        outs, jout = refs[n_in + n:n_in + n + n_out], refs[n_in + n + n_out:n_in + 2 * n + n_out]
        scr = refs[n_in + 2 * n + n_out:n_in + 2 * n + n_out + n_scr]
        sems = refs[n_in + 2 * n + n_out + n_scr:]
        step = pl.program_id(0)
        for d in range(1, len(grid)):
            step = step * grid[d] + pl.program_id(d)

        def copies(which, phase):
            return _exchange_copies(which, jin, jout, *sems, phase=phase)

        @pl.when(step == 0)
        def _():
            for cp in copies(kinds, 0):
                cp.start()

        body(*ins, *outs, *scr)

        @pl.when(step == relay_step)
        def _():
            for cp in copies(relayed, 0):
                cp.wait()
            for cp in copies(relayed, 1):
                cp.start()

        @pl.when(step == n_steps - 1)
        def _():
            for cp in copies(unrelayed, 0) + copies(relayed, 1):
                cp.wait()

    res = pl.pallas_call(
        wrapped, grid=grid,
        in_specs=list(in_specs) + [ANY] * n,
        out_specs=list(out_specs) + [ANY] * n,
        out_shape=list(out_shape) + _exchange_out_shapes(job),
        scratch_shapes=list(scratch_shapes) + _exchange_sems(n),
        compiler_params=params, name=name,
    )(*operands, *[a for _, a in job])
    return res[:n_out], res[n_out:]


def _seg_mean(x, segp):
    hi = x.astype(BF16)
    lo = (x - hi.astype(F32)).astype(BF16)
    return _dot(hi, segp) + _dot(lo, segp)


def _rot_rows(x, shift):
    m, c = x.shape
    return pltpu.roll(x.reshape(m // 8, 8, c), shift, 1).reshape(m, c)


def _sublane_is(shape, s):
    return lax.broadcasted_iota(jnp.int32, shape, 0) % 8 == s


def _causal_tail(tail, prev_rot):
    rot = _rot_rows(tail, 1)
    return jnp.where(_sublane_is(tail.shape, 0), prev_rot, rot), rot


def _lookahead_head(head, next_rot):
    rot = _rot_rows(head, 7)
    return jnp.where(_sublane_is(head.shape, 7), next_rot, rot), rot


def _tile_token_index(t, tile_idx):
    r = lax.broadcasted_iota(jnp.int32, (t, 1), 0)
    return tile_idx * t + (r % 8) * (t // 8) + r // 8


def _interleave(x):
    t, c = x.shape
    return jnp.swapaxes(x.reshape(8, t // 8, c), 0, 1).reshape(t, c)


def _deinterleave(x):
    t, c = x.shape
    return jnp.swapaxes(x.reshape(t // 8, 8, c), 0, 1).reshape(t, c)


def _ffn_fwd(x1, modv, g1024, wup, wdn, cw, cb, name="ffn_fwd", job=None, loss_target=None):
    s_len = x1.shape[0]
    t = FFN_TILE
    n_tiles = s_len // t
    with_loss = loss_target is not None

    def body(x1_ref, *rest):
        if with_loss:
            tgt_ref, rest = rest[0], rest[1:]
            loss_ref, rest = rest[10], rest[:10] + rest[11:]
        mod_ref, g_ref, wup_ref, wdn_ref, cw_ref, cb_ref, x2_ref, y2_ref, p_ref, u_ref, ext_ref, carry_ref = rest
        i = pl.program_id(0)

        @pl.when(i == 0)
        def _():
            carry_ref[...] = jnp.zeros_like(carry_ref)
            if with_loss:
                loss_ref[...] = jnp.zeros_like(loss_ref)

        x1v = x1_ref[...]
        pre_g, post_g = g_ref[3:4, :], g_ref[4:5, :]
        sh2, sc2, g2 = mod_ref[3:4, :], mod_ref[4:5, :], mod_ref[5:6, :]
        xhat, _ = _rms_fwd(x1v)
        h2b = (xhat * pre_g * (1.0 + sc2) + sh2).astype(BF16)

        def conv_shard(s):
            p = _dot_nt(h2b, wup_ref[s])
            p_ref[s] = p.astype(BF16)
            ext_ref[0:FFN_HALO, :], carry_ref[s] = _causal_tail(p[t - FFN_HALO:t, :], carry_ref[s])
            ext_ref[FFN_HALO:FFN_HALO + t, :] = p
            w = cw_ref[s]
            u = w[0:1, :] * ext_ref[0:t, :] + w[1:2, :] * ext_ref[8:8 + t, :] + w[2:3, :] * p + cb_ref[s]
            u_ref[s] = u.astype(BF16)
            return u

        y2 = jnp.zeros((t, D_MODEL), F32)
        for j in range(FF_PAIRS):
            ug = conv_shard(j)
            uv = conv_shard(j + FF_PAIRS)
            ge, _ = _gelu(ug)
            y2 = y2 + _dot((ge * uv).astype(BF16), wdn_ref[j])
        y2_ref[...] = y2
        yhat, _ = _rms_fwd(y2)
        x2 = x1v + g2 * (yhat * post_g)
        if with_loss:
            diff = x2 - _interleave(tgt_ref[...])
            x2_ref[...] = diff * (1.0 / D_MODEL)
            sq = jnp.sum((diff * diff).reshape(t // 8, 8, D_MODEL), axis=0)
            loss_ref[...] += (0.5 / D_MODEL) * sum(sq[:, 128 * j:128 * (j + 1)] for j in range(D_MODEL // 128))
        else:
            x2_ref[...] = x2

    tile = pl.BlockSpec((t, D_MODEL), lambda i: (i, 0))
    consts = (modv, g1024, wup, wdn, cw, cb)
    shards = pl.BlockSpec((N_DEV, t, FF_SHARD), lambda i: (0, i, 0))
    out_specs = [tile, tile, shards, shards]
    out_shape = [jax.ShapeDtypeStruct((s_len, D_MODEL), F32), jax.ShapeDtypeStruct((s_len, D_MODEL), F32),
                 jax.ShapeDtypeStruct((N_DEV, s_len, FF_SHARD), BF16),
                 jax.ShapeDtypeStruct((N_DEV, s_len, FF_SHARD), BF16)]
    if with_loss:
        out_specs.append(pl.BlockSpec((8, 128), lambda i: (0, 0)))
        out_shape.append(jax.ShapeDtypeStruct((8, 128), F32))
    return _pallas_call_with_exchange(
        body,
        grid=(n_tiles,),
        in_specs=[tile] * (2 if with_loss else 1) + [_whole(c.shape) for c in consts],
        out_specs=out_specs,
        out_shape=out_shape,
        scratch_shapes=[pltpu.VMEM((FFN_HALO + t, FF_SHARD), F32), pltpu.VMEM((N_DEV, FFN_HALO, FF_SHARD), F32)],
        operands=(x1,) + ((loss_target,) if with_loss else ()) + consts,
        name=name, job=job)


def _ffn_bwd(dx2, x1, y2, p, u, modv, g1024, wup, wdn, cw, name="ffn_bwd", job=None):
    s_len = x1.shape[0]
    t = FFN_TILE
    n_tiles = s_len // t
    hb = FFN_HALO

    def body(dx2_ref, x1_ref, y2_ref, p_ref, ph_ref, u_ref, mod_ref, g_ref, wup_ref, wdn_ref, cw_ref,
             dx1_ref, dp_ref, a_ref, dy2_ref, h2_ref, vec_ref, cgrad_ref, ext_ref, dext_ref, dcarry_ref):
        i = pl.program_id(0)
        tile_idx = n_tiles - 1 - i

        @pl.when(i == 0)
        def _():
            vec_ref[...] = jnp.zeros_like(vec_ref)
            cgrad_ref[...] = jnp.zeros_like(cgrad_ref)
            dcarry_ref[...] = jnp.zeros_like(dcarry_ref)

        dx2v, x1v, y2v = dx2_ref[...], x1_ref[...], y2_ref[...]
        pre_g, post_g = g_ref[3:4, :], g_ref[4:5, :]
        sh2, sc2, g2 = mod_ref[3:4, :], mod_ref[4:5, :], mod_ref[5:6, :]

        yhat, ry = _rms_fwd(y2v)
        vec_ref[1:2, :] += _colsum(dx2v * (yhat * post_g))
        dyn = dx2v * g2
        vec_ref[0:1, :] += _colsum(dyn * yhat)
        dy2b = _rms_bwd(dyn * post_g, yhat, ry).astype(BF16)
        dy2_ref[...] = dy2b

        xhat, rx = _rms_fwd(x1v)
        xn = xhat * pre_g
        h2_ref[...] = (xn * (1.0 + sc2) + sh2).astype(BF16)

        not_first = (tile_idx > 0).astype(F32)

        def recompute(s, slot):
            pf = p_ref[s].astype(F32)
            prev_rot = _rot_rows(ph_ref[s].astype(F32), 1) * not_first
            ext_ref[slot, 0:hb, :], _ = _causal_tail(pf[t - hb:t, :], prev_rot)
            ext_ref[slot, hb:hb + t, :] = pf
            return u_ref[s].astype(F32)

        def conv_bwd(s, slot, du):
            w = cw_ref[s]
            cgrad_ref[s, 0:1, :] += _colsum(du * ext_ref[slot, 0:t, :])
            cgrad_ref[s, 1:2, :] += _colsum(du * ext_ref[slot, 8:8 + t, :])
            cgrad_ref[s, 2:3, :] += _colsum(du * ext_ref[slot, 16:16 + t, :])
            cgrad_ref[s, 3:4, :] += _colsum(du)
            dext_ref[0:t, :] = du
            dext_ref[t:t + hb, :], dcarry_ref[s] = _lookahead_head(du[0:hb, :], dcarry_ref[s])
            dp = w[2:3, :] * du + w[1:2, :] * dext_ref[8:8 + t, :] + w[0:1, :] * dext_ref[16:16 + t, :]
            dpb = dp.astype(BF16)
            dp_ref[s] = dpb
            return _dot(dpb, wup_ref[s])

        dh2 = jnp.zeros((t, D_MODEL), F32)
        for j in range(FF_PAIRS):
            ug = recompute(j, 0)
            uv = recompute(j + FF_PAIRS, 1)
            ge, th = _gelu(ug)
            a_ref[j] = (ge * uv).astype(BF16)
            da = _dot_nt(dy2b, wdn_ref[j])
            dh2 = dh2 + conv_bwd(j, 0, da * uv * _gelu_grad(ug, th))
            dh2 = dh2 + conv_bwd(j + FF_PAIRS, 1, da * ge)

        vec_ref[2:3, :] += _colsum(dh2)
        vec_ref[3:4, :] += _colsum(dh2 * xn)
        dxn = dh2 * (1.0 + sc2)
        vec_ref[4:5, :] += _colsum(dxn * xhat)
        dx1_ref[...] = dx2v + _rms_bwd(dxn * pre_g, xhat, rx)

    rev = lambda i: (n_tiles - 1 - i, 0)
    tile = pl.BlockSpec((t, D_MODEL), rev)
    halo_idx = lambda i: (0, jnp.maximum((n_tiles - 1 - i) * (t // hb) - 1, 0), 0)
    return _pallas_call_with_exchange(
        body,
        grid=(n_tiles,),
        in_specs=[tile, tile, tile,
                  pl.BlockSpec((N_DEV, t, FF_SHARD), lambda i: (0, n_tiles - 1 - i, 0)),
                  pl.BlockSpec((N_DEV, hb, FF_SHARD), halo_idx),
                  pl.BlockSpec((N_DEV, t, FF_SHARD), lambda i: (0, n_tiles - 1 - i, 0)),
                  _whole(modv.shape), _whole(g1024.shape), _whole(wup.shape), _whole(wdn.shape),
                  _whole(cw.shape)],
        out_specs=[tile,
                   pl.BlockSpec((N_DEV, t, FF_SHARD), lambda i: (0, n_tiles - 1 - i, 0)),
                   pl.BlockSpec((FF_PAIRS, t, FF_SHARD), lambda i: (0, n_tiles - 1 - i, 0)),
                   tile, tile,
                   pl.BlockSpec((8, D_MODEL), lambda i: (0, 0)),
                   pl.BlockSpec((N_DEV, 8, FF_SHARD), lambda i: (0, 0, 0))],
        out_shape=[jax.ShapeDtypeStruct((s_len, D_MODEL), F32),
                   jax.ShapeDtypeStruct((N_DEV, s_len, FF_SHARD), BF16),
                   jax.ShapeDtypeStruct((FF_PAIRS, s_len, FF_SHARD), BF16),
                   jax.ShapeDtypeStruct((s_len, D_MODEL), BF16),
                   jax.ShapeDtypeStruct((s_len, D_MODEL), BF16),
                   jax.ShapeDtypeStruct((8, D_MODEL), F32),
                   jax.ShapeDtypeStruct((N_DEV, 8, FF_SHARD), F32)],
        scratch_shapes=[pltpu.VMEM((2, hb + t, FF_SHARD), F32), pltpu.VMEM((t + hb, FF_SHARD), F32),
                        pltpu.VMEM((N_DEV, hb, FF_SHARD), F32)],
        operands=(dx2, x1, y2, p, p, u, modv, g1024, wup, wdn, cw),
        name=name, job=job)


def _wgrad(a, b, name, tk=WGRAD_TK, job=None):
    a_grouped, b_grouped = a.ndim == 3, b.ndim == 3
    groups = a.shape[0] if a_grouped else b.shape[0]
    s_len, m, n = a.shape[-2], a.shape[-1], b.shape[-1]
    tk = min(tk, s_len)
    n_k = s_len // tk

    def body(a_ref, b_ref, o_ref, acc_ref):
        k = pl.program_id(1)
        av = a_ref[0] if a_grouped else a_ref[...]
        bv = b_ref[0] if b_grouped else b_ref[...]
        part = _dot_tn(av, bv)
        if n_k == 1:
            o_ref[0] = part.astype(BF16)
            return

        @pl.when(k == 0)
        def _():
            acc_ref[...] = part

        @pl.when(jnp.logical_and(k > 0, k < n_k - 1))
        def _():
            acc_ref[...] += part

        @pl.when(k == n_k - 1)
        def _():
            o_ref[0] = (acc_ref[...] + part).astype(BF16)

    a_spec = pl.BlockSpec((1, tk, m), lambda g, k: (g, k, 0)) if a_grouped else pl.BlockSpec((tk, m), lambda g, k: (k, 0))
    b_spec = pl.BlockSpec((1, tk, n), lambda g, k: (g, k, 0)) if b_grouped else pl.BlockSpec((tk, n), lambda g, k: (k, 0))
    (out,), exchanged = _pallas_call_with_exchange(
        body,
        grid=(groups, n_k),
        in_specs=[a_spec, b_spec],
        out_specs=[pl.BlockSpec((1, m, n), lambda g, k: (g, 0, 0))],
        out_shape=[jax.ShapeDtypeStruct((groups, m, n), BF16)],
        scratch_shapes=[pltpu.VMEM((m, n), F32)],
        operands=(a, b),
        name=name, job=job)
    return (out, exchanged) if job else out


def _lane(shape):
    return lax.broadcasted_iota(jnp.int32, shape, 1)


def _by_pool_group(shape, vals):
    lane = _lane(shape)
    return jnp.where(lane < 64, vals[0], jnp.where(lane < 128, vals[1], jnp.where(lane < 192, vals[2], vals[3])))


def _pool_inv_counts(t, tile_idx):
    pos1 = _tile_token_index(t, tile_idx) + 1
    return [1.0 / jnp.minimum(pos1, w).astype(F32) for w in POOL_WINDOWS]


def _sgu_keep_mask(t):
    tok_r = _tile_token_index(t, 0)
    c = lax.broadcasted_iota(jnp.int32, (1, t), 1)
    tok_c = (c % 8) * (t // 8) + c // 8
    return jnp.logical_and(tok_r // CHUNK == tok_c // CHUNK, tok_r >= tok_c)


def _masked_sgu_w(sguw_ref):
    keep = _sgu_keep_mask(sguw_ref.shape[1])
    return [jnp.where(keep, sguw_ref[h], 0.0).astype(BF16) for h in range(SGU_HEADS)]


def _branches_fwd(z, tile_idx, p384_ref, cw_ref, wm, bmat_ref, pwbd_ref, psc_ref, segp_ref, g_ref, hext_ref, zext_ref,
                  h_prev_rot, z_prev_rot, conv_saved=None):
    t = z.shape[0]
    segp = segp_ref[...]
    r = {}
    u, _ = _gelu(z[:, 0:SGU_WIDTH])
    vraw, _ = _gelu(z[:, SGU_WIDTH:2 * SGU_WIDTH])
    xc = vraw - _seg_mean(vraw, segp)
    rstd_v = lax.rsqrt(_seg_mean(xc * xc, segp) + EPS)
    xh_v = xc * rstd_v
    vnb = (xh_v * p384_ref[0:1, :] + p384_ref[1:2, :]).astype(BF16)
    first_head = _lane((t, 128)) < HEAD_DIM
    f_pairs = []
    for pr in range(SGU_HEADS // 2):
        vp = vnb[:, pr * 128:(pr + 1) * 128]
        f_pairs.append(jnp.where(first_head, _dot(wm[2 * pr], vp), _dot(wm[2 * pr + 1], vp)))
    f = jnp.concatenate(f_pairs, axis=1) + bmat_ref[...]
    ya = u * f
    r.update(u=u, xh_v=xh_v, rstd_v=rstd_v, vnb=vnb, f=f)
    o_b = 2 * SGU_WIDTH
    a_in = z[:, o_b:o_b + CONV_WIDTH]
    sig_g = jax.nn.sigmoid(z[:, o_b + CONV_WIDTH:o_b + 2 * CONV_WIDTH])
    hh = a_in * sig_g
    hext_ref[0:t, :], r["h_rot"] = _causal_tail(hh, h_prev_rot)
    hext_ref[t:2 * t, :] = hh
    if conv_saved is None:
        conv = jnp.zeros((t, CONV_WIDTH), F32) + p384_ref[2:3, :]
        for k in range(CONV_K):
            conv = conv + cw_ref[k:k + 1, :] * hext_ref[pl.ds(t - 8 * (CONV_K - 1 - k), t), :]
        r["conv"] = conv
    else:
        conv = conv_saved
    cc = conv - _rowmean(conv)
    rstd_c = lax.rsqrt(_rowmean(cc * cc) + EPS)
    xh_c = cc * rstd_c
    cn = xh_c * p384_ref[3:4, :] + p384_ref[4:5, :]
    sig_c = jax.nn.sigmoid(cn)
    yb = cn * sig_c
    r.update(a_in=a_in, sig_g=sig_g, xh_c=xh_c, rstd_c=rstd_c, cn=cn, sig_c=sig_c)
    o_c = o_b + 2 * CONV_WIDTH
    zc = z[:, o_c:o_c + POOL_WIDTH]
    zext_ref[0:POOL_HALO, :], r["z_rot"] = _causal_tail(zc[t - POOL_HALO:t, :], z_prev_rot)
    zext_ref[POOL_HALO:POOL_HALO + t, :] = zc
    sums, acc = [], zc
    for j in range(1, POOL_WINDOWS[-1]):
        acc = acc + zext_ref[pl.ds(POOL_HALO - 8 * j, t), :]
        if j + 1 in POOL_WINDOWS:
            sums.append(acc)
    inv = _pool_inv_counts(t, tile_idx)
    dpool = _by_pool_group((t, POOL_WIDTH), [s * iv for s, iv in zip(sums, inv)]) - zc
    ycp = _dot(dpool.astype(BF16), pwbd_ref[...])
    yc = ycp * psc_ref[0:1, :]
    r.update(dpool=dpool, ycp=ycp)
    yha, ra = _rms_fwd(ya)
    yhb, rb = _rms_fwd(yb)
    yhc, rc = _rms_fwd(yc)
    bg = g_ref[2:3, :]
    ycat = jnp.concatenate([yha * bg[:, 0:384], yhb * bg[:, 384:768], yhc * bg[:, 768:1024]], axis=1)
    r.update(yha=yha, ra=ra, yhb=yhb, rb=rb, yhc=yhc, rc=rc, ycat=ycat)
    return r


def _mixer_fwd(x, modv, g1024, p384, cw, sguw, bmat, pwbd, psc, segp, win, wout, name="mixer_fwd", job=None,
               natural_x=False, relay_steps_before_end=2):
    s_len = x.shape[0]
    t = MIX_TILE

    def body(x_ref, mod_ref, g_ref, p384_ref, cw_ref, sguw_ref, bmat_ref, pwbd_ref, psc_ref, segp_ref, win_ref, wout_ref,
             x1_ref, z_ref, o_ref, conv_ref, hext_ref, zext_ref, hrot_ref, zrot_ref, wm_ref):
        i = pl.program_id(0)

        @pl.when(i == 0)
        def _():
            hrot_ref[...] = jnp.zeros_like(hrot_ref)
            zrot_ref[...] = jnp.zeros_like(zrot_ref)
            for h, wmh in enumerate(_masked_sgu_w(sguw_ref)):
                wm_ref[h] = wmh

        xv = _interleave(x_ref[...]) if natural_x else x_ref[...]
        sh1, sc1, g1 = mod_ref[0:1, :], mod_ref[1:2, :], mod_ref[2:3, :]
        xhat, _ = _rms_fwd(xv)
        h1 = xhat * g_ref[0:1, :] * (1.0 + sc1) + sh1
        z = _dot_nt(h1.astype(BF16), win_ref[...])
        z_ref[...] = z
        r = _branches_fwd(z, i, p384_ref, cw_ref, [wm_ref[h] for h in range(SGU_HEADS)], bmat_ref, pwbd_ref, psc_ref,
                          segp_ref, g_ref, hext_ref, zext_ref, hrot_ref[...], zrot_ref[...])
        hrot_ref[...] = r["h_rot"]
        zrot_ref[...] = r["z_rot"]
        conv_ref[...] = r["conv"]
        o = _dot(r["ycat"].astype(BF16), wout_ref[...])
        o_ref[...] = o
        ohat, _ = _rms_fwd(o)
        x1_ref[...] = xv + g1 * (ohat * g_ref[1:2, :])

    tile = pl.BlockSpec((t, D_MODEL), lambda i: (i, 0))
    consts = (modv, g1024, p384, cw, sguw, bmat, pwbd, psc, segp, win, wout)
    return _pallas_call_with_exchange(
        body,
        grid=(s_len // t,),
        in_specs=[tile] + [_whole(c.shape) for c in consts],
        out_specs=[tile, pl.BlockSpec((t, IN_WIDTH), lambda i: (i, 0)), tile,
                   pl.BlockSpec((t, CONV_WIDTH), lambda i: (i, 0))],
        out_shape=[jax.ShapeDtypeStruct((s_len, D_MODEL), F32), jax.ShapeDtypeStruct((s_len, IN_WIDTH), F32),
                   jax.ShapeDtypeStruct((s_len, D_MODEL), F32), jax.ShapeDtypeStruct((s_len, CONV_WIDTH), F32)],
        scratch_shapes=[pltpu.VMEM((2 * t, CONV_WIDTH), F32), pltpu.VMEM((POOL_HALO + t, POOL_WIDTH), F32),
                        pltpu.VMEM((t, CONV_WIDTH), F32), pltpu.VMEM((POOL_HALO, POOL_WIDTH), F32),
                        pltpu.VMEM((SGU_HEADS, t, t), BF16)],
        operands=(x, *consts),
        name=name, job=job, relay_steps_before_end=relay_steps_before_end)


def _mixer_bwd(dx1, x, o, z, conv, modv, g1024, p384, cw, sguw, bmat, pwbd, psc, segp, win, wout, name="mixer_bwd",
               job=None, natural_x=False):
    s_len = x.shape[0]
    t = MIX_TILE
    n_tiles = s_len // t

    def body(dx1_ref, x_ref, o_ref, z_ref, zh_ref, conv_ref, mod_ref, g_ref, p384_ref, cw_ref, sguw_ref, bmat_ref, pwbd_ref,
             psc_ref, segp_ref, win_ref, wout_ref,
             dx_ref, dz_ref, do_ref, ycat_ref, h1_ref, vec_ref, v384_ref, dcw_ref, dsguw_out_ref, dbmat_out_ref, dpw_ref,
             dpsc_ref, hext_ref, zext_ref, gext_ref, qext_ref, grot_ref, qrot_ref, wm_ref, dsguw_ref, dbmat_ref):
        i = pl.program_id(0)
        tile_idx = n_tiles - 1 - i

        @pl.when(i == 0)
        def _():
            for ref in (vec_ref, v384_ref, dcw_ref, dsguw_ref, dbmat_ref, dpw_ref, dpsc_ref, grot_ref, qrot_ref):
                ref[...] = jnp.zeros_like(ref)
            for h, wmh in enumerate(_masked_sgu_w(sguw_ref)):
                wm_ref[h] = wmh

        dx1v, ov, z = dx1_ref[...], o_ref[...], z_ref[...]
        xv = _interleave(x_ref[...]) if natural_x else x_ref[...]
        sh1, sc1, g1 = mod_ref[0:1, :], mod_ref[1:2, :], mod_ref[2:3, :]
        pre_g, post_g, bg = g_ref[0:1, :], g_ref[1:2, :], g_ref[2:3, :]
        segp = segp_ref[...]

        ohat, ro = _rms_fwd(ov)
        vec_ref[1:2, :] += _colsum(dx1v * (ohat * post_g))
        don = dx1v * g1
        vec_ref[0:1, :] += _colsum(don * ohat)
        dob = _rms_bwd(don * post_g, ohat, ro).astype(BF16)
        do_ref[...] = dob
        dycat = _dot_nt(dob, wout_ref[...])

        not_first = (tile_idx > 0).astype(F32)
        o_b = 2 * SGU_WIDTH
        o_c = o_b + 2 * CONV_WIDTH
        h_prev = zh_ref[:, o_b:o_b + CONV_WIDTH] * jax.nn.sigmoid(zh_ref[:, o_b + CONV_WIDTH:o_c])
        h_prev_rot = _rot_rows(h_prev, 1) * not_first
        z_prev_rot = _rot_rows(zh_ref[t - POOL_HALO:t, o_c:o_c + POOL_WIDTH], 1) * not_first
        wm = [wm_ref[h] for h in range(SGU_HEADS)]
        r = _branches_fwd(z, tile_idx, p384_ref, cw_ref, wm, bmat_ref, pwbd_ref, psc_ref, segp_ref, g_ref,
                          hext_ref, zext_ref, h_prev_rot, z_prev_rot, conv_saved=conv_ref[...])
        ycat_ref[...] = r["ycat"].astype(BF16)

        def branch_norm_bwd(dyn, yhat, rr, gain):
            return _colsum(dyn * yhat), _rms_bwd(dyn * gain, yhat, rr)

        dga, dya = branch_norm_bwd(dycat[:, 0:384], r["yha"], r["ra"], bg[:, 0:384])
        dgb, dyb = branch_norm_bwd(dycat[:, 384:768], r["yhb"], r["rb"], bg[:, 384:768])
        dgc, dyc = branch_norm_bwd(dycat[:, 768:1024], r["yhc"], r["rc"], bg[:, 768:1024])
        vec_ref[5:6, :] += jnp.concatenate([dga, dgb, dgc], axis=1)

        du_act = dya * r["f"]
        df = dya * r["u"]
        first_head = _lane((t, 128)) < HEAD_DIM
        dbmat_ref[...] += df
        dvn_pairs = []
        for pr in range(SGU_HEADS // 2):
            dfp = df[:, pr * 128:(pr + 1) * 128]
            df0 = jnp.where(first_head, dfp, 0.0).astype(BF16)
            df1 = jnp.where(first_head, 0.0, dfp).astype(BF16)
            vp = r["vnb"][:, pr * 128:(pr + 1) * 128]
            dvn_pairs.append(_dot_tn(wm[2 * pr], df0) + _dot_tn(wm[2 * pr + 1], df1))
            dsguw_ref[2 * pr] += _dot_nt(df0, vp)
            dsguw_ref[2 * pr + 1] += _dot_nt(df1, vp)
        dvn = jnp.concatenate(dvn_pairs, axis=1)
        v384_ref[0:1, :] += _colsum(dvn * r["xh_v"])
        v384_ref[1:2, :] += _colsum(dvn)
        dxh = dvn * p384_ref[0:1, :]
        dvraw = r["rstd_v"] * (dxh - _seg_mean(dxh, segp) - r["xh_v"] * _seg_mean(dxh * r["xh_v"], segp))
        zu, zv = z[:, 0:SGU_WIDTH], z[:, SGU_WIDTH:o_b]
        _, tu = _gelu(zu)
        _, tv = _gelu(zv)
        dz_u = du_act * _gelu_grad(zu, tu)
        dz_v = dvraw * _gelu_grad(zv, tv)

        cn, sig_c = r["cn"], r["sig_c"]
        dcn = dyb * (sig_c * (1.0 + cn * (1.0 - sig_c)))
        v384_ref[3:4, :] += _colsum(dcn * r["xh_c"])
        v384_ref[4:5, :] += _colsum(dcn)
        dxc = dcn * p384_ref[3:4, :]
        gconv = r["rstd_c"] * (dxc - _rowmean(dxc) - r["xh_c"] * _rowmean(dxc * r["xh_c"]))
        v384_ref[2:3, :] += _colsum(gconv)
        gext_ref[0:t, :] = gconv
        gext_ref[t:2 * t, :], grot_ref[...] = _lookahead_head(gconv, grot_ref[...])
        dhh = jnp.zeros((t, CONV_WIDTH), F32)
        for k in range(CONV_K):
            shift = CONV_K - 1 - k
            dcw_ref[k:k + 1, :] += _colsum(gconv * hext_ref[pl.ds(t - 8 * shift, t), :])
            dhh = dhh + cw_ref[k:k + 1, :] * gext_ref[pl.ds(8 * shift, t), :]
        sig_g = r["sig_g"]
        dz_a = dhh * sig_g
        dz_g = dhh * r["a_in"] * sig_g * (1.0 - sig_g)

        dpsc_ref[0:1, :] += _colsum(dyc * r["ycp"])
        dycp = (dyc * psc_ref[0:1, :]).astype(BF16)
        dpw_ref[...] += _dot_tn(r["dpool"].astype(BF16), dycp)
        ddp = _dot_nt(dycp, pwbd_ref[...])
        inv = _pool_inv_counts(t, tile_idx)
        q = ddp * _by_pool_group((t, POOL_WIDTH), inv)
        qext_ref[0:t, :] = q
        qext_ref[t:t + POOL_HALO, :], qrot_ref[...] = _lookahead_head(q[0:POOL_HALO, :], qrot_ref[...])
        sums, acc = [], q
        for j in range(1, POOL_WINDOWS[-1]):
            acc = acc + qext_ref[pl.ds(8 * j, t), :]
            if j + 1 in POOL_WINDOWS:
                sums.append(acc)
        dz_c = _by_pool_group((t, POOL_WIDTH), sums) - ddp

        dzb = jnp.concatenate([dz_u, dz_v, dz_a, dz_g, dz_c], axis=1).astype(BF16)
        dz_ref[...] = dzb
        dh1 = _dot(dzb, win_ref[...])

        xhat, rx = _rms_fwd(xv)
        xn = xhat * pre_g
        h1_ref[...] = (xn * (1.0 + sc1) + sh1).astype(BF16)
        vec_ref[2:3, :] += _colsum(dh1)
        vec_ref[3:4, :] += _colsum(dh1 * xn)
        dxn = dh1 * (1.0 + sc1)
        vec_ref[4:5, :] += _colsum(dxn * xhat)
        dx = dx1v + _rms_bwd(dxn * pre_g, xhat, rx)
        dx_ref[...] = _deinterleave(dx) if natural_x else dx

        @pl.when(i == n_tiles - 1)
        def _():
            keep = _sgu_keep_mask(t)
            for h in range(SGU_HEADS):
                rows_natural = _deinterleave(jnp.where(keep, dsguw_ref[h], 0.0))
                natural = _deinterleave(rows_natural.T).T
                dsguw_out_ref[h] = sum(natural[b * CHUNK:(b + 1) * CHUNK, b * CHUNK:(b + 1) * CHUNK]
                                       for b in range(t // CHUNK))
            dbmat = _deinterleave(float(HEAD_DIM) * _seg_mean(dbmat_ref[...], segp))
            dbmat_out_ref[...] = sum(dbmat[b * CHUNK:(b + 1) * CHUNK, :] for b in range(t // CHUNK))

    rev = lambda i: (n_tiles - 1 - i, 0)
    tile = pl.BlockSpec((t, D_MODEL), rev)
    ztile = pl.BlockSpec((t, IN_WIDTH), rev)
    zhalo = pl.BlockSpec((t, IN_WIDTH), lambda i: (jnp.maximum(n_tiles - 2 - i, 0), 0))
    consts = (modv, g1024, p384, cw, sguw, bmat, pwbd, psc, segp, win, wout)
    acc = lambda shape: pl.BlockSpec(shape, lambda i: (0,) * len(shape))
    acc_shapes = [(8, D_MODEL), (8, SGU_WIDTH), (32, CONV_WIDTH), (SGU_HEADS, CHUNK, CHUNK), (CHUNK, SGU_WIDTH),
                  (POOL_WIDTH, POOL_WIDTH), (8, POOL_WIDTH)]
    return _pallas_call_with_exchange(
        body,
        grid=(n_tiles,),
        in_specs=[tile, tile, tile, ztile, zhalo, pl.BlockSpec((t, CONV_WIDTH), rev)] + [_whole(c.shape) for c in consts],
        out_specs=[tile, ztile, tile, tile, tile] + [acc(s) for s in acc_shapes],
        out_shape=[jax.ShapeDtypeStruct((s_len, D_MODEL), F32), jax.ShapeDtypeStruct((s_len, IN_WIDTH), BF16),
                   jax.ShapeDtypeStruct((s_len, D_MODEL), BF16), jax.ShapeDtypeStruct((s_len, D_MODEL), BF16),
                   jax.ShapeDtypeStruct((s_len, D_MODEL), BF16)] + [jax.ShapeDtypeStruct(s, F32) for s in acc_shapes],
        scratch_shapes=[pltpu.VMEM((2 * t, CONV_WIDTH), F32), pltpu.VMEM((POOL_HALO + t, POOL_WIDTH), F32),
                        pltpu.VMEM((2 * t, CONV_WIDTH), F32), pltpu.VMEM((t + POOL_HALO, POOL_WIDTH), F32),
                        pltpu.VMEM((t, CONV_WIDTH), F32), pltpu.VMEM((POOL_HALO, POOL_WIDTH), F32),
                        pltpu.VMEM((SGU_HEADS, t, t), BF16), pltpu.VMEM((SGU_HEADS, t, t), F32),
                        pltpu.VMEM((t, SGU_WIDTH), F32)],
        operands=(dx1, x, o, z, z, conv, *consts),
        name=name, job=job)


MOD_SHARD = 6 * D_MODEL // N_DEV

ROW_DMOD = 0
ROW_G1024 = 8
ROW_V384 = 16
ROW_SGU_B = 24
ROW_POOL_SCALE = 25
ROW_LOSS = 26
ROW_CONV_W = 32
ROW_FFN_CONV = 64
ROW_POOL_W = 96
ROW_SGU_W = 112
ROWS_PER_LAYER = 208
N_LAYERS = 2


def _gather_weights(c8, mod_w, mod_b8, job):
    kinds = [kind for kind, _ in job]
    shards = [a for _, a in job]
    n = len(shards)

    def body(c_ref, modw_ref, modb_ref, *rest):
        shard_refs = rest[:n]
        sc_all_ref, modrows_ref = rest[n], rest[n + 1]
        full_refs = rest[n + 2:2 * n + 2]
        send_buf, mod_recv, w_send, w_recv, w_local, sc_send, sc_recv, mod_send, mod_recv_sem = rest[2 * n + 2:]
        pos = _my_pos()
        me = _flat(pos)
        peers = [_peer(pos, k) for k in range(1, N_DEV)]

        w_copies = _exchange_copies(kinds, shard_refs, full_refs, w_send, w_recv, w_local, phase=0)
        for cp in w_copies:
            cp.start()

        cv = c_ref[...]
        sc_all_ref[me] = cv * jax.nn.sigmoid(cv)
        sc_copies = [_remote_copy(sc_all_ref.at[me], sc_all_ref.at[me], sc_send.at[k], sc_recv.at[k], peers[k])
                     for k in range(N_PEERS)]
        for cp in sc_copies:
            cp.start()
        for cp in sc_copies:
            cp.wait()

        sc = jnp.concatenate([sc_all_ref[j, 0:1, :] for j in range(N_DEV)], axis=0)
        send_buf[...] = jnp.zeros_like(send_buf)
        for l in range(N_LAYERS):
            part = jnp.dot(sc, modw_ref[l], precision=lax.Precision.HIGHEST, preferred_element_type=F32)
            for j in range(N_DEV):
                send_buf[j, l:l + 1, :] = part[j:j + 1, :]
        mod_recv[me] = send_buf[me]
        mod_copies = [_remote_copy(send_buf.at[_flat(peers[k])], mod_recv.at[me], mod_send.at[k], mod_recv_sem.at[k],
                                   peers[k]) for k in range(N_PEERS)]
        for cp in mod_copies:
            cp.start()
        for cp in mod_copies:
            cp.wait()
        modrows_ref[...] = jnp.zeros_like(modrows_ref)
        for l in range(N_LAYERS):
            row = jnp.concatenate([mod_recv[j, l:l + 1, :] for j in range(N_DEV)], axis=1)
            modrows_ref[l:l + 1, :] = row + modb_ref[l:l + 1, :]

        for cp in w_copies:
            cp.wait()
        relays = _exchange_copies(kinds, shard_refs, full_refs, w_send, w_recv, w_local, phase=1)
        for cp in relays:
            cp.start()
        for cp in relays:
            cp.wait()

    out_shape = ([jax.ShapeDtypeStruct((N_DEV, 8, D_MODEL), F32), jax.ShapeDtypeStruct((8, 6 * D_MODEL), F32)]
                 + [jax.ShapeDtypeStruct((N_DEV,) + s.shape, s.dtype) for s in shards])
    return pl.pallas_call(
        body,
        in_specs=[VMEM, VMEM, VMEM] + [ANY] * n,
        out_specs=[VMEM, VMEM] + [ANY] * n,
        out_shape=out_shape,
        scratch_shapes=[pltpu.VMEM((N_DEV, 8, MOD_SHARD), F32), pltpu.VMEM((N_DEV, 8, MOD_SHARD), F32),
                        pltpu.SemaphoreType.DMA((n, N_PEERS)), pltpu.SemaphoreType.DMA((n, N_PEERS)),
                        pltpu.SemaphoreType.DMA((n,)),
                        pltpu.SemaphoreType.DMA((N_PEERS,)), pltpu.SemaphoreType.DMA((N_PEERS,)),
                        pltpu.SemaphoreType.DMA((N_PEERS,)), pltpu.SemaphoreType.DMA((N_PEERS,))],
        compiler_params=pltpu.CompilerParams(vmem_limit_bytes=VMEM_LIMIT_BYTES),
        name="gather_weights",
    )(c8, mod_w, mod_b8, *shards)


def _small_sums(sc_all, small_all0, small_all1, job):
    kinds = [kind for kind, _ in job]
    n = len(job)

    def body(sc_all_ref, small_all0_ref, small_all1_ref, *rest):
        small_sum_ref, gmodw_ref = rest[n], rest[n + 1]
        copies = _exchange_copies(kinds, rest[:n], rest[n + 2:2 * n + 2], *rest[2 * n + 2:], phase=0)
        for cp in copies:
            cp.start()
        me = _flat(_my_pos())
        sc = jnp.concatenate([sc_all_ref[j, 0:1, :] for j in range(N_DEV)], axis=0)
        mine = lax.broadcasted_iota(jnp.int32, (2 * N_DEV, MOD_SHARD), 0) == me
        for l, parts in enumerate((small_all0_ref, small_all1_ref)):
            total = parts[0].astype(F32)
            for j in range(1, N_DEV):
                total = total + parts[j].astype(F32)
            small_sum_ref[l] = total
            dm = jnp.concatenate(
                [jnp.sum(jnp.where(mine, parts[j, ROW_DMOD:ROW_DMOD + 2 * N_DEV, 0:MOD_SHARD].astype(F32), 0.0),
                         axis=0, keepdims=True) for j in range(N_DEV)], axis=0)
            gmodw_ref[l] = lax.dot_general(sc, dm, (((0,), (0,)), ((), ())), precision=lax.Precision.HIGHEST,
                                           preferred_element_type=F32)
        for cp in copies:
            cp.wait()

    res = pl.pallas_call(
        body,
        in_specs=[VMEM, VMEM, VMEM] + [ANY] * n,
        out_specs=[VMEM, VMEM] + [ANY] * n,
        out_shape=[jax.ShapeDtypeStruct((N_LAYERS, ROWS_PER_LAYER, D_MODEL), F32),
                   jax.ShapeDtypeStruct((N_LAYERS, D_MODEL, MOD_SHARD), F32)] + _exchange_out_shapes(job),
        scratch_shapes=_exchange_sems(n),
        compiler_params=pltpu.CompilerParams(vmem_limit_bytes=VMEM_LIMIT_BYTES),
        name="small_sums",
    )(sc_all, small_all0, small_all1, *[a for _, a in job])
    return res[0], res[1], res[2:]


def _adam_update(g, w, m, v):
    m2 = ADAM_B1 * m + (1.0 - ADAM_B1) * g
    v2 = ADAM_B2 * v + (1.0 - ADAM_B2) * (g * g)
    m_hat = m2 / (1.0 - ADAM_B1 ** ADAM_STEP)
    v_hat = v2 / (1.0 - ADAM_B2 ** ADAM_STEP)
    delta = -ADAM_LR * (m_hat / (jnp.sqrt(v_hat) + ADAM_EPS) + ADAM_WD * w)
    return delta, m2, v2


def _pair_add(g, r1, row_chunk, name):
    _, rows, cols = g.shape
    core = lax.axis_index("c").astype(jnp.int32).reshape(1)

    def body(core_ref, g_ref, r_ref, o_ref):
        o_ref[0] = (g_ref[0, 0].astype(F32) + r_ref[0].astype(F32)).astype(BF16)

    blk = pl.BlockSpec((1, row_chunk, cols), lambda q, i, core_ref: (q, i, 0))
    grid_spec = pltpu.PrefetchScalarGridSpec(
        num_scalar_prefetch=1, grid=(N_CHIPS, rows // row_chunk),
        in_specs=[pl.BlockSpec((1, 1, row_chunk, cols), lambda q, i, core_ref: (q, core_ref[0], i, 0)), blk],
        out_specs=blk)
    return pl.pallas_call(
        body, grid_spec=grid_spec, out_shape=jax.ShapeDtypeStruct((N_CHIPS, rows, cols), BF16),
        compiler_params=_cparams(2), name=name,
    )(core, g.reshape(N_CHIPS, 2, rows, cols), r1)


def _adam_sharded(recv0, recv1, w, m, v, row_chunk, name):
    _, rows, cols = w.shape
    n_chunks = rows // row_chunk

    def body(r0_ref, r1_ref, w_ref, m_ref, v_ref, g_ref, d_ref, m2_ref, v2_ref):
        layer = pl.program_id(0)

        def run(r_ref):
            g = r_ref[0].astype(F32)
            for j in range(1, r_ref.shape[0]):
                g = g + r_ref[j].astype(F32)
            delta, m2, v2 = _adam_update(g, w_ref[0], m_ref[0], v_ref[0])
            g_ref[0], d_ref[0], m2_ref[0], v2_ref[0] = g, delta, m2, v2

        @pl.when(layer == 0)
        def _():
            run(r0_ref)

        @pl.when(layer == 1)
        def _():
            run(r1_ref)

    r0_spec = pl.BlockSpec((recv0.shape[0], row_chunk, cols), lambda l, i: (0, i * (1 - l) + (n_chunks - 1) * l, 0))
    r1_spec = pl.BlockSpec((recv1.shape[0], row_chunk, cols), lambda l, i: (0, i * l, 0))
    blk = pl.BlockSpec((1, row_chunk, cols), lambda l, i: (l, i, 0))
    out = jax.ShapeDtypeStruct(w.shape, F32)
    return pl.pallas_call(
        body,
        grid=(N_LAYERS, n_chunks),
        in_specs=[r0_spec, r1_spec, blk, blk, blk],
        out_specs=[blk] * 4,
        out_shape=[out] * 4,
        compiler_params=_cparams(2),
        name=name,
    )(recv0, recv1, w, m, v)


def _adam_dense(g, w, m, v, row_chunk, name):
    n_lead, rows, cols = w.shape

    def body(g_ref, w_ref, m_ref, v_ref, go_ref, d_ref, m2_ref, v2_ref):
        gv = g_ref[...]
        go_ref[...] = gv
        d_ref[...], m2_ref[...], v2_ref[...] = _adam_update(gv, w_ref[...], m_ref[...], v_ref[...])

    blk = pl.BlockSpec((1, row_chunk, cols), lambda l, i: (l, i, 0))
    out = jax.ShapeDtypeStruct(w.shape, F32)
    return pl.pallas_call(
        body,
        grid=(n_lead, rows // row_chunk),
        in_specs=[blk] * 4,
        out_specs=[blk] * 4,
        out_shape=[out] * 4,
        compiler_params=_cparams(2),
        name=name,
    )(g, w, m, v)


WEIGHT_NAMES = ("mod_w", "mod_b", "mix_pre_g", "mix_post_g", "w_in", "sgu_norm_g", "sgu_norm_b", "sgu_w", "sgu_b",
                "conv_w", "conv_b", "conv_norm_g", "conv_norm_b", "pool_w", "pool_scale", "branch_g", "w_out",
                "ffn_pre_g", "ffn_post_g", "ffn_up", "ffn_conv_w", "ffn_conv_b", "ffn_down")
SHARDED_BIG = ("w_in", "w_out", "ffn_up", "ffn_down")
SMALL_PACKED = tuple(n for n in WEIGHT_NAMES if n not in SHARDED_BIG + ("mod_w",))


def _rows8(rows, width=D_MODEL):
    out = [jnp.pad(r.astype(F32), (0, width - r.shape[0]))[None] for r in rows]
    out.append(jnp.zeros((8 - len(rows), width), F32))
    return jnp.concatenate(out, axis=0)


def _as_rows(a, width=D_MODEL):
    flat = a.astype(F32).reshape(-1)
    pad = (-flat.shape[0]) % width
    return jnp.pad(flat, (0, pad)).reshape(-1, width)


def _pad_cols(a, width=D_MODEL):
    return jnp.pad(a.astype(F32), ((0, 0), (0, width - a.shape[1])))


def _pack_rows(arrays):
    rows = jnp.concatenate([_as_rows(a) for a in arrays], axis=0)
    return jnp.pad(rows, ((0, (-rows.shape[0]) % 8), (0, 0)))


def _unpack_rows(packed, shapes):
    out, r = [], 0
    for shape in shapes:
        size = math.prod(shape)
        n_rows = -(-size // D_MODEL)
        out.append(packed[r:r + n_rows].reshape(-1)[:size].reshape(shape))
        r += n_rows
    return out


TILE_VREGS = MIX_TILE // 8
CHUNK_SUBLANES = CHUNK // TILE_VREGS
CHUNKS_PER_TILE = MIX_TILE // CHUNK


def _chunk_axis_to_tile(a, axis):
    shape = a.shape
    a = a.reshape(shape[:axis] + (CHUNK_SUBLANES, TILE_VREGS) + shape[axis + 1:])
    a = jnp.swapaxes(a, axis, axis + 1)
    a = jnp.tile(a, (1,) * (axis + 1) + (CHUNKS_PER_TILE,) + (1,) * (len(shape) - axis - 1))
    return a.reshape(shape[:axis] + (MIX_TILE,) + shape[axis + 1:])


def _layer_consts(l, w, mod_rows, win, wout, conv_w_full):
    modv = _rows8(list(mod_rows[l].reshape(6, D_MODEL)))
    g1024 = _rows8([w["mix_pre_g"][l], w["mix_post_g"][l], w["branch_g"][l], w["ffn_pre_g"][l], w["ffn_post_g"][l]])
    p384 = _rows8([w["sgu_norm_g"][l], w["sgu_norm_b"][l], w["conv_b"][l], w["conv_norm_g"][l], w["conv_norm_b"][l]],
                  SGU_WIDTH)
    cw = jnp.pad(conv_w_full[l], ((0, 32 - CONV_K), (0, 0)))
    sguw = _chunk_axis_to_tile(_chunk_axis_to_tile(w["sgu_w"][l], 2), 1)
    bmat = _chunk_axis_to_tile(jnp.repeat(w["sgu_b"][l].T, HEAD_DIM, axis=1), 0)
    groups = len(POOL_WINDOWS)
    eye = jnp.eye(groups, dtype=F32)
    pwbd = (eye[:, None, :, None] * w["pool_w"][l][:, :, None, :]).reshape(POOL_WIDTH, POOL_WIDTH).astype(BF16)
    psc = _rows8([w["pool_scale"][l]], POOL_WIDTH)
    seg = jnp.arange(SGU_WIDTH) // HEAD_DIM
    segp = jnp.where(seg[:, None] == seg[None, :], 1.0 / HEAD_DIM, 0.0).astype(BF16)
    return modv, g1024, (modv, g1024, p384, cw, sguw, bmat, pwbd, psc, segp, win, wout)


def _small_grad_rows(mix, ffn, loss_pieces=None):
    _, _, _, _, _, mvec, v384, dcw, dsguw, dbmat, dpw, dpsc = mix
    fvec, cgrad = ffn[5], ffn[6]
    dmod = jnp.stack([mvec[2], mvec[3], mvec[1], fvec[2], fvec[3], fvec[1]]).reshape(N_DEV, MOD_SHARD)
    g_rows = jnp.stack([mvec[4], mvec[0], mvec[5], fvec[4], fvec[0]])
    dsgu_b = dbmat[:, ::HEAD_DIM].T.reshape(1, SGU_HEADS * CHUNK)
    groups = len(POOL_WINDOWS)
    gdim = POOL_WIDTH // groups
    dpw4 = dpw.reshape(groups, gdim, groups, gdim)
    dpool = jnp.stack([dpw4[g, :, g, :] for g in range(groups)])
    misc = [dsgu_b[0], dpsc[0]] + ([] if loss_pieces is None else [loss_pieces])
    blocks = [_pad_cols(dmod), _rows8(list(g_rows)), _pad_cols(v384), _rows8(misc), _pad_cols(dcw),
              _pad_cols(cgrad[:, 0:4, :].reshape(4 * N_DEV, FF_SHARD)), _as_rows(dpool), _as_rows(dsguw)]
    return jnp.concatenate(blocks, axis=0)


def _small_grads_from_rows(total):
    per = {n: [] for n in SMALL_PACKED}
    for l in range(N_LAYERS):
        s = total[l]
        per["mod_b"].append(s[ROW_DMOD:ROW_DMOD + N_DEV, :MOD_SHARD].reshape(6 * D_MODEL))
        for j, name in enumerate(("mix_pre_g", "mix_post_g", "branch_g", "ffn_pre_g", "ffn_post_g")):
            per[name].append(s[ROW_G1024 + j])
        for j, name in enumerate(("sgu_norm_g", "sgu_norm_b", "conv_b", "conv_norm_g", "conv_norm_b")):
            per[name].append(s[ROW_V384 + j, :SGU_WIDTH])
        per["sgu_b"].append(s[ROW_SGU_B, :SGU_HEADS * CHUNK].reshape(SGU_HEADS, CHUNK))
        per["pool_scale"].append(s[ROW_POOL_SCALE, :POOL_WIDTH])
        per["conv_w"].append(s[ROW_CONV_W:ROW_CONV_W + CONV_K, :CONV_WIDTH])
        fc = s[ROW_FFN_CONV:ROW_FFN_CONV + 4 * N_DEV, :FF_SHARD].reshape(N_DEV, 4, FF_SHARD)
        per["ffn_conv_w"].append(fc[:, 0:3, :].transpose(1, 0, 2).reshape(FFN_CONV_K, 2 * D_FF))
        per["ffn_conv_b"].append(fc[:, 3, :].reshape(2 * D_FF))
        per["pool_w"].append(s[ROW_POOL_W:ROW_POOL_W + 16].reshape(len(POOL_WINDOWS), HEAD_DIM, HEAD_DIM))
        per["sgu_w"].append(s[ROW_SGU_W:ROW_SGU_W + 96].reshape(SGU_HEADS, CHUNK, CHUNK))
    return {n: jnp.stack(v) for n, v in per.items()}


def kernel(x, c, mod_w, mod_b, mix_pre_g, mix_post_g, w_in, sgu_norm_g, sgu_norm_b, sgu_w, sgu_b, conv_w, conv_b, conv_norm_g, conv_norm_b, pool_w, pool_scale, branch_g, w_out, ffn_pre_g, ffn_post_g, ffn_up, ffn_conv_w, ffn_conv_b, ffn_down, loss_target, m_mod_w, m_mod_b, m_mix_pre_g, m_mix_post_g, m_w_in, m_sgu_norm_g, m_sgu_norm_b, m_sgu_w, m_sgu_b, m_conv_w, m_conv_b, m_conv_norm_g, m_conv_norm_b, m_pool_w, m_pool_scale, m_branch_g, m_w_out, m_ffn_pre_g, m_ffn_post_g, m_ffn_up, m_ffn_conv_w, m_ffn_conv_b, m_ffn_down, v_mod_w, v_mod_b, v_mix_pre_g, v_mix_post_g, v_w_in, v_sgu_norm_g, v_sgu_norm_b, v_sgu_w, v_sgu_b, v_conv_w, v_conv_b, v_conv_norm_g, v_conv_norm_b, v_pool_w, v_pool_scale, v_branch_g, v_w_out, v_ffn_pre_g, v_ffn_post_g, v_ffn_up, v_ffn_conv_w, v_ffn_conv_b, v_ffn_down):
    w = dict(zip(WEIGHT_NAMES, (mod_w, mod_b, mix_pre_g, mix_post_g, w_in, sgu_norm_g, sgu_norm_b, sgu_w, sgu_b, conv_w,
                                conv_b, conv_norm_g, conv_norm_b, pool_w, pool_scale, branch_g, w_out, ffn_pre_g,
                                ffn_post_g, ffn_up, ffn_conv_w, ffn_conv_b, ffn_down)))
    m = dict(zip(WEIGHT_NAMES, (m_mod_w, m_mod_b, m_mix_pre_g, m_mix_post_g, m_w_in, m_sgu_norm_g, m_sgu_norm_b, m_sgu_w,
                                m_sgu_b, m_conv_w, m_conv_b, m_conv_norm_g, m_conv_norm_b, m_pool_w, m_pool_scale,
                                m_branch_g, m_w_out, m_ffn_pre_g, m_ffn_post_g, m_ffn_up, m_ffn_conv_w, m_ffn_conv_b,
                                m_ffn_down)))
    v = dict(zip(WEIGHT_NAMES, (v_mod_w, v_mod_b, v_mix_pre_g, v_mix_post_g, v_w_in, v_sgu_norm_g, v_sgu_norm_b, v_sgu_w,
                                v_sgu_b, v_conv_w, v_conv_b, v_conv_norm_g, v_conv_norm_b, v_pool_w, v_pool_scale,
                                v_branch_g, v_w_out, v_ffn_pre_g, v_ffn_post_g, v_ffn_up, v_ffn_conv_w, v_ffn_conv_b,
                                v_ffn_down)))
    me = _flat(_my_pos())
    xs = x[0]

    transposed = ("w_in", "ffn_up")
    wt = {n: jnp.swapaxes(w[n], 1, 2) if n in transposed else w[n] for n in SHARDED_BIG}
    mt = {n: jnp.swapaxes(m[n], 1, 2) if n in transposed else m[n] for n in SHARDED_BIG}
    vt = {n: jnp.swapaxes(v[n], 1, 2) if n in transposed else v[n] for n in SHARDED_BIG}
    bf16_shards = [[wt[n][l].astype(BF16) for n in SHARDED_BIG] for l in range(N_LAYERS)]

    def mixer_operands(l, win_g, wout_g):
        win = win_g.reshape(IN_WIDTH, D_MODEL)
        return _layer_consts(l, w, mod_rows, win, wout_g.reshape(D_MODEL, D_MODEL), conv_w_full)

    def ffn_operands(l, modv, g1024, wup_g, wdn_g):
        wdn = wdn_g.reshape(FF_PAIRS, FF_SHARD, D_MODEL)
        return modv, g1024, wup_g, wdn, ffn_cw_full[:, l], ffn_conv_b[l].reshape(N_DEV, 1, FF_SHARD)

    w0, w1 = bf16_shards
    c8 = jnp.broadcast_to(c, (8, D_MODEL))
    mod_b8 = jnp.pad(mod_b, ((0, 8 - N_LAYERS), (0, 0)))
    sc_all, mod_rows, win0_g, wout0_g, conv_w_g, ffn_cw_full = _gather_weights(
        c8, mod_w, mod_b8, [("gather2", w0[0]), ("gather2", w0[1]), ("gather", conv_w), ("gather", ffn_conv_w)])
    conv_w_full = conv_w_g.transpose(1, 2, 0, 3).reshape(N_LAYERS, CONV_K, CONV_WIDTH)

    modv0, g0, mix_consts0 = mixer_operands(0, win0_g, wout0_g)
    (x1, z, o, cv), (wup0_g, wdn0_g) = _mixer_fwd(xs, *mix_consts0, name="mixer_fwd_l0", natural_x=True,
                                                  job=[("gather2", w0[2]), ("gather2", w0[3])],
                                                  relay_steps_before_end=1)
    ffn_consts0 = ffn_operands(0, modv0, g0, wup0_g, wdn0_g)
    (x2, y2, p, u), (win1_g, wout1_g, wdn1_g) = _ffn_fwd(
        x1, *ffn_consts0, name="ffn_fwd_l0", job=[("gather2", w1[0]), ("gather2", w1[1]), ("gather2", w1[3])])
    saved = [(xs, z, o, cv, x1, y2, p, u)]
    modv1, g1, mix_consts1 = mixer_operands(1, win1_g, wout1_g)
    (x1, z, o, cv), (wup1_g,) = _mixer_fwd(x2, *mix_consts1, name="mixer_fwd_l1", job=[("gather2", w1[2])])
    ffn_consts1 = ffn_operands(1, modv1, g1, wup1_g, wdn1_g)
    (dh, y2, p, u, loss_tile), _ = _ffn_fwd(x1, *ffn_consts1, name="ffn_fwd_l1", loss_target=loss_target[0])
    saved.append((x2, z, o, cv, x1, y2, p, u))
    loss_local = jnp.sum(loss_tile)
    loss_hi = loss_local.astype(BF16).astype(F32)
    loss_mid = (loss_local - loss_hi).astype(BF16).astype(F32)
    loss_pieces = jnp.stack([loss_hi, loss_mid, loss_local - loss_hi - loss_mid])

    def ffn_weight_grads(l, ffn):
        dp, a, dy2, h2 = ffn[1:5]
        d_up = _wgrad(dp, h2, f"wgrad_ffn_up_l{l}", tk=WGRAD_TK_FFN)
        d_dn = _wgrad(a, dy2, f"wgrad_ffn_down_l{l}", tk=WGRAD_TK_FFN).reshape(N_DEV, D_FF // N_DEV, D_MODEL)
        return d_up, d_dn

    def mixer_weight_grads(l, mix):
        dz, do, ycat, h1 = mix[1:5]
        d_in = _wgrad(dz[None], h1, f"wgrad_w_in_l{l}").reshape(N_DEV, IN_WIDTH // N_DEV, D_MODEL)
        d_out = _wgrad(ycat, do[None], f"wgrad_w_out_l{l}").reshape(N_DEV, D_MODEL // N_DEV, D_MODEL)
        return d_in, d_out

    x_in, z, o, cv, x1, y2, p, u = saved[1]
    ffn1, _ = _ffn_bwd(dh, x1, y2, p, u, *ffn_consts1[:-1], name="ffn_bwd_l1")
    d_up1, d_dn1 = ffn_weight_grads(1, ffn1)
    mix1, (sib_up1, sib_dn1) = _mixer_bwd(ffn1[0], x_in, o, z, cv, *mix_consts1, name="mixer_bwd_l1",
                                          job=[("scatter_p1", d_up1), ("scatter_p1", d_dn1)])
    chip_up1 = _pair_add(d_up1, sib_up1, 176, "pair_add_ffn_up_l1")
    chip_dn1 = _pair_add(d_dn1, sib_dn1, 176, "pair_add_ffn_down_l1")
    d_in1, d_out1 = mixer_weight_grads(1, mix1)
    small1 = _small_grad_rows(mix1, ffn1).astype(BF16)

    x_in, z, o, cv, x1, y2, p, u = saved[0]
    ffn0, job_out = _ffn_bwd(mix1[0], x1, y2, p, u, *ffn_consts0[:-1], name="ffn_bwd_l0",
                             job=[("scatter", d_in1), ("scatter", d_out1), ("scatter_p2", chip_up1),
                                  ("scatter_p2", chip_dn1)])
    recv1 = job_out
    dp, a, dy2, h2 = ffn0[1:5]
    d_dn0, (small_all1,) = _wgrad(a, dy2, "wgrad_ffn_down_l0", tk=WGRAD_TK_FFN, job=[("gather2", small1)])
    d_dn0 = d_dn0.reshape(N_DEV, D_FF // N_DEV, D_MODEL)
    d_up0, (recv_dn0,) = _wgrad(dp, h2, "wgrad_ffn_up_l0", tk=WGRAD_TK_FFN, job=[("scatter", d_dn0)])
    mix0, (recv_up0,) = _mixer_bwd(ffn0[0], x_in, o, z, cv, *mix_consts0, name="mixer_bwd_l0", natural_x=True,
                                   job=[("scatter", d_up0)])
    recv_ffn0 = (recv_up0, recv_dn0)
    grad_x = mix0[0][None]
    dz, do, ycat, h1 = mix0[1:5]
    small0 = _small_grad_rows(mix0, ffn0, loss_pieces).astype(BF16)
    d_in0, (small_all0,) = _wgrad(dz[None], h1, "wgrad_w_in_l0", tk=WGRAD_TK // 2, job=[("gather2", small0)])
    d_in0 = d_in0.reshape(N_DEV, IN_WIDTH // N_DEV, D_MODEL)
    d_out0, (recv_in0,) = _wgrad(ycat, do[None], "wgrad_w_out_l0", tk=WGRAD_TK // 2, job=[("scatter", d_in0)])
    d_out0 = d_out0.reshape(N_DEV, D_MODEL // N_DEV, D_MODEL)
    small_total, g_mod_w, (recv_out0,) = _small_sums(sc_all, small_all0, small_all1, [("scatter", d_out0)])
    recv0 = [recv_in0, recv_out0, recv_ffn0[0], recv_ffn0[1]]
    loss = small_total[0, ROW_LOSS, 0] + small_total[0, ROW_LOSS, 1] + small_total[0, ROW_LOSS, 2]

    grads, deltas, new_m, new_v = {}, {}, {}, {}
    for j, (name, chunk) in enumerate((("w_in", 224), ("w_out", 128), ("ffn_up", 176), ("ffn_down", 176))):
        outs = _adam_sharded(recv0[j], recv1[j], wt[name], mt[name], vt[name], chunk, "adam_" + name)
        if name in transposed:
            outs = [jnp.swapaxes(t, 1, 2) for t in outs]
        grads[name], deltas[name], new_m[name], new_v[name] = outs
    grads["mod_w"], deltas["mod_w"], new_m["mod_w"], new_v["mod_w"] = _adam_dense(
        g_mod_w, mod_w, m_mod_w, v_mod_w, 256, "adam_mod_w")

    small_g = _small_grads_from_rows(small_total)
    small_g["conv_w"] = lax.dynamic_slice_in_dim(small_g["conv_w"], me * conv_w.shape[2], conv_w.shape[2], axis=2)
    small_g["ffn_conv_w"] = lax.dynamic_slice_in_dim(small_g["ffn_conv_w"], me * FF_SHARD, FF_SHARD, axis=2)
    shapes = [w[n].shape for n in SMALL_PACKED]
    packs = [_pack_rows([src[n] for n in SMALL_PACKED])[None] for src in (small_g, w, m, v)]
    _, d, m2, v2 = _adam_dense(*packs, packs[0].shape[1], "adam_small")
    for name, dd, mm, vv in zip(SMALL_PACKED, _unpack_rows(d[0], shapes), _unpack_rows(m2[0], shapes),
                                _unpack_rows(v2[0], shapes)):
        grads[name], deltas[name], new_m[name], new_v[name] = small_g[name], dd, mm, vv

    return (loss, grad_x, *[grads[n] for n in WEIGHT_NAMES], *[deltas[n] for n in WEIGHT_NAMES],
            *[new_m[n] for n in WEIGHT_NAMES], *[new_v[n] for n in WEIGHT_NAMES])
```

```python
import math

import jax
import jax.numpy as jnp
from jax import lax
from jax.experimental import pallas as pl
from jax.experimental.pallas import tpu as pltpu

F32 = jnp.float32
BF16 = jnp.bfloat16

D_MODEL = 1024
N_DEV = 8
SGU_WIDTH = 384
CONV_WIDTH = 384
POOL_WIDTH = 256
HEAD_DIM = 64
SGU_HEADS = 6
CHUNK = 128
CONV_K = 31
POOL_WINDOWS = (2, 4, 8, 16)
IN_WIDTH = 1792
D_FF = 2816
FF_SHARD = 2 * D_FF // N_DEV
FF_PAIRS = N_DEV // 2
FFN_CONV_K = 3
EPS = 1e-6
GELU_C0 = math.sqrt(2.0 / math.pi)
GELU_C1 = 0.044715

ADAM_LR = 0.001
ADAM_B1 = 0.9
ADAM_B2 = 0.999
ADAM_EPS = 1e-08
ADAM_WD = 0.01
ADAM_STEP = 10

VMEM_LIMIT_BYTES = 56 * 1024 * 1024
TILE = 256
MIX_TILE = TILE
FFN_TILE = TILE
FFN_HALO = 8 * (FFN_CONV_K - 1)
POOL_HALO = 8 * POOL_WINDOWS[-1]
WGRAD_TK = 2048
WGRAD_TK_FFN = 4096


def _cparams(n_axes):
    return pltpu.CompilerParams(dimension_semantics=("arbitrary",) * n_axes, vmem_limit_bytes=VMEM_LIMIT_BYTES)


def _whole(shape):
    nd = len(shape)
    return pl.BlockSpec(shape, lambda *_: (0,) * nd, pipeline_mode=pl.Buffered(1))


def _dot(a, b):
    return jnp.dot(a, b, preferred_element_type=F32)


def _dot_nt(a, b):
    return lax.dot_general(a, b, (((1,), (1,)), ((), ())), preferred_element_type=F32)


def _dot_tn(a, b):
    return lax.dot_general(a, b, (((0,), (0,)), ((), ())), preferred_element_type=F32)


def _gelu(x):
    t = jnp.tanh(GELU_C0 * (x + GELU_C1 * x * x * x))
    return 0.5 * x * (1.0 + t), t


def _gelu_grad(x, t):
    return 0.5 * (1.0 + t) + 0.5 * x * (1.0 - t * t) * (GELU_C0 * (1.0 + 3.0 * GELU_C1 * x * x))


def _rowmean(x):
    return jnp.mean(x, axis=-1, keepdims=True)


def _colsum(x):
    return jnp.sum(x, axis=0, keepdims=True)


def _rms_fwd(x):
    r = lax.rsqrt(_rowmean(x * x) + EPS)
    return x * r, r


def _rms_bwd(dxhat, xhat, r):
    return r * (dxhat - xhat * _rowmean(dxhat * xhat))


N_PEERS = N_DEV - 1
ANY = pl.BlockSpec(memory_space=pl.ANY)
VMEM = pl.BlockSpec(memory_space=pltpu.VMEM)


def _my_pos():
    return lax.axis_index("x"), lax.axis_index("y"), lax.axis_index("c")


def _peer(pos, k):
    x, y, c = pos
    return (1 - x if k & 4 else x, 1 - y if k & 2 else y, 1 - c if k & 1 else c)


def _flat(pos):
    return 4 * pos[0] + 2 * pos[1] + pos[2]


def _remote_copy(src, dst, send_sem, recv_sem, peer):
    return pltpu.make_async_remote_copy(src_ref=src, dst_ref=dst, send_sem=send_sem, recv_sem=recv_sem,
                                        device_id=peer, device_id_type=pl.DeviceIdType.MESH)


N_CHIPS = N_DEV // 2
SIBLING = 1
SAME_CORE_PEERS = (2, 4, 6)


def _exchange_out_shapes(job):
    def shape(kind, a):
        if kind in ("gather", "gather2"):
            return (N_DEV,) + a.shape
        if kind == "scatter_p1":
            return (N_CHIPS,) + a.shape[1:]
        return a.shape
    return [jax.ShapeDtypeStruct(shape(kind, a), a.dtype) for kind, a in job]


def _exchange_sems(n):
    return [pltpu.SemaphoreType.DMA((n, N_PEERS)), pltpu.SemaphoreType.DMA((n, N_PEERS)), pltpu.SemaphoreType.DMA((n,))]


def _exchange_copies(kinds, src_refs, dst_refs, send_sems, recv_sems, local_sems, phase):
    pos = _my_pos()
    me = _flat(pos)
    chip, core = 2 * pos[0] + pos[1], pos[2]
    copies = []

    def remote(a, src, dst, k, sem=None):
        sem = k - 1 if sem is None else sem
        copies.append(_remote_copy(src, dst, send_sems.at[a, sem], recv_sems.at[a, sem], _peer(pos, k)))

    for a, kind in enumerate(kinds):
        src, dst = src_refs[a], dst_refs[a]
        if phase == 1:
            if kind == "gather2":
                for k in SAME_CORE_PEERS:
                    remote(a, dst.at[me ^ k], dst.at[me ^ k], SIBLING, sem=k)
        elif kind in ("gather", "gather2"):
            copies.append(pltpu.make_async_copy(src, dst.at[me], local_sems.at[a]))
            for k in (range(1, N_DEV) if kind == "gather" else (SIBLING,) + SAME_CORE_PEERS):
                remote(a, src, dst.at[me], k)
        elif kind == "scatter":
            copies.append(pltpu.make_async_copy(src.at[me], dst.at[me], local_sems.at[a]))
            for k in range(1, N_DEV):
                remote(a, src.at[me ^ k], dst.at[me], k)
        elif kind == "scatter_p1":
            for q in range(N_CHIPS):
                remote(a, src.at[2 * q + 1 - core], dst.at[q], SIBLING, sem=q)
        elif kind == "scatter_p2":
            copies.append(pltpu.make_async_copy(src.at[chip], dst.at[chip], local_sems.at[a]))
            for k in SAME_CORE_PEERS:
                remote(a, src.at[chip ^ (k >> 1)], dst.at[chip], k)
    return copies


def _pallas_call_with_exchange(body, *, grid, in_specs, out_specs, out_shape, scratch_shapes, operands, name, job,
                               relay_steps_before_end=2):
    params = _cparams(len(grid))
    if not job:
        outs = pl.pallas_call(body, grid=grid, in_specs=in_specs, out_specs=out_specs, out_shape=out_shape,
                              scratch_shapes=scratch_shapes, compiler_params=params, name=name)(*operands)
        return outs, []
    kinds = [kind for kind, _ in job]
    relayed = [kind if kind == "gather2" else None for kind in kinds]
    unrelayed = [None if kind == "gather2" else kind for kind in kinds]
    n, n_in, n_out, n_scr = len(job), len(in_specs), len(out_specs), len(scratch_shapes)
    n_steps = math.prod(grid)
    relay_step = max(n_steps - relay_steps_before_end, 0)

    def wrapped(*refs):
        ins, jin = refs[:n_in], refs[n_in:n_in + n]
        outs, jout = refs[n_in + n:n_in + n + n_out], refs[n_in + n + n_out:n_in + 2 * n + n_out]
        scr = refs[n_in + 2 * n + n_out:n_in + 2 * n + n_out + n_scr]
        sems = refs[n_in + 2 * n + n_out + n_scr:]
        step = pl.program_id(0)
        for d in range(1, len(grid)):
            step = step * grid[d] + pl.program_id(d)

        def copies(which, phase):
            return _exchange_copies(which, jin, jout, *sems, phase=phase)

        @pl.when(step == 0)
        def _():
            for cp in copies(kinds, 0):
                cp.start()

        body(*ins, *outs, *scr)

        @pl.when(step == relay_step)
        def _():
            for cp in copies(relayed, 0):
                cp.wait()
            for cp in copies(relayed, 1):
                cp.start()

        @pl.when(step == n_steps - 1)
        def _():
            for cp in copies(unrelayed, 0) + copies(relayed, 1):
                cp.wait()

    res = pl.pallas_call(
        wrapped, grid=grid,
        in_specs=list(in_specs) + [ANY] * n,
        out_specs=list(out_specs) + [ANY] * n,
        out_shape=list(out_shape) + _exchange_out_shapes(job),
        scratch_shapes=list(scratch_shapes) + _exchange_sems(n),
        compiler_params=params, name=name,
    )(*operands, *[a for _, a in job])
    return res[:n_out], res[n_out:]


def _seg_mean(x, segp):
    hi = x.astype(BF16)
    lo = (x - hi.astype(F32)).astype(BF16)
    return _dot(hi, segp) + _dot(lo, segp)


def _rot_rows(x, shift):
    m, c = x.shape
    return pltpu.roll(x.reshape(m // 8, 8, c), shift, 1).reshape(m, c)


def _sublane_is(shape, s):
    return lax.broadcasted_iota(jnp.int32, shape, 0) % 8 == s


def _causal_tail(tail, prev_rot):
    rot = _rot_rows(tail, 1)
    return jnp.where(_sublane_is(tail.shape, 0), prev_rot, rot), rot


def _lookahead_head(head, next_rot):
    rot = _rot_rows(head, 7)
    return jnp.where(_sublane_is(head.shape, 7), next_rot, rot), rot


def _tile_token_index(t, tile_idx):
    r = lax.broadcasted_iota(jnp.int32, (t, 1), 0)
    return tile_idx * t + (r % 8) * (t // 8) + r // 8


def _interleave(x):
    t, c = x.shape
    return jnp.swapaxes(x.reshape(8, t // 8, c), 0, 1).reshape(t, c)


def _deinterleave(x):
    t, c = x.shape
    return jnp.swapaxes(x.reshape(t // 8, 8, c), 0, 1).reshape(t, c)


def _ffn_fwd(x1, modv, g1024, wup, wdn, cw, cb, name="ffn_fwd", job=None, loss_target=None):
    s_len = x1.shape[0]
    t = FFN_TILE
    n_tiles = s_len // t
    with_loss = loss_target is not None

    def body(x1_ref, *rest):
        if with_loss:
            tgt_ref, rest = rest[0], rest[1:]
            loss_ref, rest = rest[10], rest[:10] + rest[11:]
        mod_ref, g_ref, wup_ref, wdn_ref, cw_ref, cb_ref, x2_ref, y2_ref, p_ref, u_ref, ext_ref, carry_ref = rest
        i = pl.program_id(0)

        @pl.when(i == 0)
        def _():
            carry_ref[...] = jnp.zeros_like(carry_ref)
            if with_loss:
                loss_ref[...] = jnp.zeros_like(loss_ref)

        x1v = x1_ref[...]
        pre_g, post_g = g_ref[3:4, :], g_ref[4:5, :]
        sh2, sc2, g2 = mod_ref[3:4, :], mod_ref[4:5, :], mod_ref[5:6, :]
        xhat, _ = _rms_fwd(x1v)
        h2b = (xhat * pre_g * (1.0 + sc2) + sh2).astype(BF16)

        def conv_shard(s):
            p = _dot_nt(h2b, wup_ref[s])
            p_ref[s] = p.astype(BF16)
            ext_ref[0:FFN_HALO, :], carry_ref[s] = _causal_tail(p[t - FFN_HALO:t, :], carry_ref[s])
            ext_ref[FFN_HALO:FFN_HALO + t, :] = p
            w = cw_ref[s]
            u = w[0:1, :] * ext_ref[0:t, :] + w[1:2, :] * ext_ref[8:8 + t, :] + w[2:3, :] * p + cb_ref[s]
            u_ref[s] = u.astype(BF16)
            return u

        y2 = jnp.zeros((t, D_MODEL), F32)
        for j in range(FF_PAIRS):
            ug = conv_shard(j)
            uv = conv_shard(j + FF_PAIRS)
            ge, _ = _gelu(ug)
            y2 = y2 + _dot((ge * uv).astype(BF16), wdn_ref[j])
        y2_ref[...] = y2
        yhat, _ = _rms_fwd(y2)
        x2 = x1v + g2 * (yhat * post_g)
        if with_loss:
            diff = x2 - _interleave(tgt_ref[...])
            x2_ref[...] = diff * (1.0 / D_MODEL)
            sq = jnp.sum((diff * diff).reshape(t // 8, 8, D_MODEL), axis=0)
            loss_ref[...] += (0.5 / D_MODEL) * sum(sq[:, 128 * j:128 * (j + 1)] for j in range(D_MODEL // 128))
        else:
            x2_ref[...] = x2

    tile = pl.BlockSpec((t, D_MODEL), lambda i: (i, 0))
    consts = (modv, g1024, wup, wdn, cw, cb)
    shards = pl.BlockSpec((N_DEV, t, FF_SHARD), lambda i: (0, i, 0))
    out_specs = [tile, tile, shards, shards]
    out_shape = [jax.ShapeDtypeStruct((s_len, D_MODEL), F32), jax.ShapeDtypeStruct((s_len, D_MODEL), F32),
                 jax.ShapeDtypeStruct((N_DEV, s_len, FF_SHARD), BF16),
                 jax.ShapeDtypeStruct((N_DEV, s_len, FF_SHARD), BF16)]
    if with_loss:
        out_specs.append(pl.BlockSpec((8, 128), lambda i: (0, 0)))
        out_shape.append(jax.ShapeDtypeStruct((8, 128), F32))
    return _pallas_call_with_exchange(
        body,
        grid=(n_tiles,),
        in_specs=[tile] * (2 if with_loss else 1) + [_whole(c.shape) for c in consts],
        out_specs=out_specs,
        out_shape=out_shape,
        scratch_shapes=[pltpu.VMEM((FFN_HALO + t, FF_SHARD), F32), pltpu.VMEM((N_DEV, FFN_HALO, FF_SHARD), F32)],
        operands=(x1,) + ((loss_target,) if with_loss else ()) + consts,
        name=name, job=job)


def _ffn_bwd(dx2, x1, y2, p, u, modv, g1024, wup, wdn, cw, name="ffn_bwd", job=None):
    s_len = x1.shape[0]
    t = FFN_TILE
    n_tiles = s_len // t
    hb = FFN_HALO

    def body(dx2_ref, x1_ref, y2_ref, p_ref, u_ref, mod_ref, g_ref, wup_ref, wdn_ref, cw_ref,
             dx1_ref, dp_ref, a_ref, dy2_ref, h2_ref, vec_ref, cgrad_ref, dext_ref, dcarry_ref):
        i = pl.program_id(0)

        @pl.when(i == 0)
        def _():
            vec_ref[...] = jnp.zeros_like(vec_ref)
            cgrad_ref[...] = jnp.zeros_like(cgrad_ref)
            dcarry_ref[...] = jnp.zeros_like(dcarry_ref)

        dx2v, x1v, y2v = dx2_ref[...], x1_ref[...], y2_ref[...]
        pre_g, post_g = g_ref[3:4, :], g_ref[4:5, :]
        sh2, sc2, g2 = mod_ref[3:4, :], mod_ref[4:5, :], mod_ref[5:6, :]

        yhat, ry = _rms_fwd(y2v)
        vec_ref[1:2, :] += _colsum(dx2v * (yhat * post_g))
        dyn = dx2v * g2
        vec_ref[0:1, :] += _colsum(dyn * yhat)
        dy2b = _rms_bwd(dyn * post_g, yhat, ry).astype(BF16)
        dy2_ref[...] = dy2b

        xhat, rx = _rms_fwd(x1v)
        xn = xhat * pre_g
        h2_ref[...] = (xn * (1.0 + sc2) + sh2).astype(BF16)

        def conv_bwd(s, du):
            w = cw_ref[s]
            pf = p_ref[s].astype(F32)
            dext_ref[0:t, :] = du
            dext_ref[t:t + hb, :], dcarry_ref[s] = _lookahead_head(du[0:hb, :], dcarry_ref[s])
            du1, du2 = dext_ref[8:8 + t, :], dext_ref[16:16 + t, :]
            cgrad_ref[s, 0:1, :] += _colsum(du2 * pf)
            cgrad_ref[s, 1:2, :] += _colsum(du1 * pf)
            cgrad_ref[s, 2:3, :] += _colsum(du * pf)
            cgrad_ref[s, 3:4, :] += _colsum(du)
            dpb = (w[2:3, :] * du + w[1:2, :] * du1 + w[0:1, :] * du2).astype(BF16)
            dp_ref[s] = dpb
            return _dot(dpb, wup_ref[s])

        dh2 = jnp.zeros((t, D_MODEL), F32)
        for j in range(FF_PAIRS):
            ug = u_ref[j].astype(F32)
            uv = u_ref[j + FF_PAIRS].astype(F32)
            ge, th = _gelu(ug)
            a_ref[j] = (ge * uv).astype(BF16)
            da = _dot_nt(dy2b, wdn_ref[j])
            dh2 = dh2 + conv_bwd(j, da * uv * _gelu_grad(ug, th))
            dh2 = dh2 + conv_bwd(j + FF_PAIRS, da * ge)

        vec_ref[2:3, :] += _colsum(dh2)
        vec_ref[3:4, :] += _colsum(dh2 * xn)
        dxn = dh2 * (1.0 + sc2)
        vec_ref[4:5, :] += _colsum(dxn * xhat)
        dx1_ref[...] = dx2v + _rms_bwd(dxn * pre_g, xhat, rx)

    rev = lambda i: (n_tiles - 1 - i, 0)
    tile = pl.BlockSpec((t, D_MODEL), rev)
    return _pallas_call_with_exchange(
        body,
        grid=(n_tiles,),
        in_specs=[tile, tile, tile,
                  pl.BlockSpec((N_DEV, t, FF_SHARD), lambda i: (0, n_tiles - 1 - i, 0)),
                  pl.BlockSpec((N_DEV, t, FF_SHARD), lambda i: (0, n_tiles - 1 - i, 0)),
                  _whole(modv.shape), _whole(g1024.shape), _whole(wup.shape), _whole(wdn.shape),
                  _whole(cw.shape)],
        out_specs=[tile,
                   pl.BlockSpec((N_DEV, t, FF_SHARD), lambda i: (0, n_tiles - 1 - i, 0)),
                   pl.BlockSpec((FF_PAIRS, t, FF_SHARD), lambda i: (0, n_tiles - 1 - i, 0)),
                   tile, tile,
                   pl.BlockSpec((8, D_MODEL), lambda i: (0, 0)),
                   pl.BlockSpec((N_DEV, 8, FF_SHARD), lambda i: (0, 0, 0))],
        out_shape=[jax.ShapeDtypeStruct((s_len, D_MODEL), F32),
                   jax.ShapeDtypeStruct((N_DEV, s_len, FF_SHARD), BF16),
                   jax.ShapeDtypeStruct((FF_PAIRS, s_len, FF_SHARD), BF16),
                   jax.ShapeDtypeStruct((s_len, D_MODEL), BF16),
                   jax.ShapeDtypeStruct((s_len, D_MODEL), BF16),
                   jax.ShapeDtypeStruct((8, D_MODEL), F32),
                   jax.ShapeDtypeStruct((N_DEV, 8, FF_SHARD), F32)],
        scratch_shapes=[pltpu.VMEM((t + hb, FF_SHARD), F32), pltpu.VMEM((N_DEV, hb, FF_SHARD), F32)],
        operands=(dx2, x1, y2, p, u, modv, g1024, wup, wdn, cw),
        name=name, job=job)


def _wgrad(a, b, name, tk=WGRAD_TK, job=None):
    a_grouped, b_grouped = a.ndim == 3, b.ndim == 3
    groups = a.shape[0] if a_grouped else b.shape[0]
    s_len, m, n = a.shape[-2], a.shape[-1], b.shape[-1]
    tk = min(tk, s_len)
    n_k = s_len // tk

    def body(a_ref, b_ref, o_ref, acc_ref):
        k = pl.program_id(1)
        av = a_ref[0] if a_grouped else a_ref[...]
        bv = b_ref[0] if b_grouped else b_ref[...]
        part = _dot_tn(av, bv)
        if n_k == 1:
            o_ref[0] = part.astype(BF16)
            return

        @pl.when(k == 0)
        def _():
            acc_ref[...] = part

        @pl.when(jnp.logical_and(k > 0, k < n_k - 1))
        def _():
            acc_ref[...] += part

        @pl.when(k == n_k - 1)
        def _():
            o_ref[0] = (acc_ref[...] + part).astype(BF16)

    a_spec = pl.BlockSpec((1, tk, m), lambda g, k: (g, k, 0)) if a_grouped else pl.BlockSpec((tk, m), lambda g, k: (k, 0))
    b_spec = pl.BlockSpec((1, tk, n), lambda g, k: (g, k, 0)) if b_grouped else pl.BlockSpec((tk, n), lambda g, k: (k, 0))
    (out,), exchanged = _pallas_call_with_exchange(
        body,
        grid=(groups, n_k),
        in_specs=[a_spec, b_spec],
        out_specs=[pl.BlockSpec((1, m, n), lambda g, k: (g, 0, 0))],
        out_shape=[jax.ShapeDtypeStruct((groups, m, n), BF16)],
        scratch_shapes=[pltpu.VMEM((m, n), F32)],
        operands=(a, b),
        name=name, job=job)
    return (out, exchanged) if job else out


def _lane(shape):
    return lax.broadcasted_iota(jnp.int32, shape, 1)


def _by_pool_group(shape, vals):
    lane = _lane(shape)
    return jnp.where(lane < 64, vals[0], jnp.where(lane < 128, vals[1], jnp.where(lane < 192, vals[2], vals[3])))


def _pool_inv_counts(t, tile_idx):
    pos1 = _tile_token_index(t, tile_idx) + 1
    return [1.0 / jnp.minimum(pos1, w).astype(F32) for w in POOL_WINDOWS]


def _sgu_keep_mask(t):
    tok_r = _tile_token_index(t, 0)
    c = lax.broadcasted_iota(jnp.int32, (1, t), 1)
    tok_c = (c % 8) * (t // 8) + c // 8
    return jnp.logical_and(tok_r // CHUNK == tok_c // CHUNK, tok_r >= tok_c)


def _masked_sgu_w(sguw_ref):
    keep = _sgu_keep_mask(sguw_ref.shape[1])
    return [jnp.where(keep, sguw_ref[h], 0.0).astype(BF16) for h in range(SGU_HEADS)]


def _branches_fwd(z, tile_idx, p384_ref, cw_ref, wm, bmat_ref, pwbd_ref, psc_ref, segp_ref, g_ref, hext_ref, zext_ref,
                  h_prev_rot, z_prev_rot, conv_saved=None):
    t = z.shape[0]
    segp = segp_ref[...]
    r = {}
    u, _ = _gelu(z[:, 0:SGU_WIDTH])
    vraw, _ = _gelu(z[:, SGU_WIDTH:2 * SGU_WIDTH])
    xc = vraw - _seg_mean(vraw, segp)
    rstd_v = lax.rsqrt(_seg_mean(xc * xc, segp) + EPS)
    xh_v = xc * rstd_v
    vnb = (xh_v * p384_ref[0:1, :] + p384_ref[1:2, :]).astype(BF16)
    first_head = _lane((t, 128)) < HEAD_DIM
    f_pairs = []
    for pr in range(SGU_HEADS // 2):
        vp = vnb[:, pr * 128:(pr + 1) * 128]
        f_pairs.append(jnp.where(first_head, _dot(wm[2 * pr], vp), _dot(wm[2 * pr + 1], vp)))
    f = jnp.concatenate(f_pairs, axis=1) + bmat_ref[...]
    ya = u * f
    r.update(u=u, xh_v=xh_v, rstd_v=rstd_v, vnb=vnb, f=f)
    o_b = 2 * SGU_WIDTH
    a_in = z[:, o_b:o_b + CONV_WIDTH]
    sig_g = jax.nn.sigmoid(z[:, o_b + CONV_WIDTH:o_b + 2 * CONV_WIDTH])
    hh = a_in * sig_g
    if conv_saved is None:
        hext_ref[0:t, :], r["h_rot"] = _causal_tail(hh, h_prev_rot)
        hext_ref[t:2 * t, :] = hh
        conv = jnp.zeros((t, CONV_WIDTH), F32) + p384_ref[2:3, :]
        for k in range(CONV_K):
            conv = conv + cw_ref[k:k + 1, :] * hext_ref[pl.ds(t - 8 * (CONV_K - 1 - k), t), :]
        r["conv"] = conv
    else:
        conv = conv_saved
    cc = conv - _rowmean(conv)
    rstd_c = lax.rsqrt(_rowmean(cc * cc) + EPS)
    xh_c = cc * rstd_c
    cn = xh_c * p384_ref[3:4, :] + p384_ref[4:5, :]
    sig_c = jax.nn.sigmoid(cn)
    yb = cn * sig_c
    r.update(a_in=a_in, sig_g=sig_g, hh=hh, xh_c=xh_c, rstd_c=rstd_c, cn=cn, sig_c=sig_c)
    o_c = o_b + 2 * CONV_WIDTH
    zc = z[:, o_c:o_c + POOL_WIDTH]
    zext_ref[0:POOL_HALO, :], r["z_rot"] = _causal_tail(zc[t - POOL_HALO:t, :], z_prev_rot)
    zext_ref[POOL_HALO:POOL_HALO + t, :] = zc
    sums, acc = [], zc
    for j in range(1, POOL_WINDOWS[-1]):
        acc = acc + zext_ref[pl.ds(POOL_HALO - 8 * j, t), :]
        if j + 1 in POOL_WINDOWS:
            sums.append(acc)
    inv = _pool_inv_counts(t, tile_idx)
    dpool = _by_pool_group((t, POOL_WIDTH), [s * iv for s, iv in zip(sums, inv)]) - zc
    ycp = _dot(dpool.astype(BF16), pwbd_ref[...])
    yc = ycp * psc_ref[0:1, :]
    r.update(dpool=dpool, ycp=ycp)
    yha, ra = _rms_fwd(ya)
    yhb, rb = _rms_fwd(yb)
    yhc, rc = _rms_fwd(yc)
    bg = g_ref[2:3, :]
    ycat = jnp.concatenate([yha * bg[:, 0:384], yhb * bg[:, 384:768], yhc * bg[:, 768:1024]], axis=1)
    r.update(yha=yha, ra=ra, yhb=yhb, rb=rb, yhc=yhc, rc=rc, ycat=ycat)
    return r


def _mixer_fwd(x, modv, g1024, p384, cw, sguw, bmat, pwbd, psc, segp, win, wout, name="mixer_fwd", job=None,
               natural_x=False, relay_steps_before_end=2):
    s_len = x.shape[0]
    t = MIX_TILE

    def body(x_ref, mod_ref, g_ref, p384_ref, cw_ref, sguw_ref, bmat_ref, pwbd_ref, psc_ref, segp_ref, win_ref, wout_ref,
             x1_ref, z_ref, o_ref, conv_ref, hext_ref, zext_ref, hrot_ref, zrot_ref, wm_ref):
        i = pl.program_id(0)

        @pl.when(i == 0)
        def _():
            hrot_ref[...] = jnp.zeros_like(hrot_ref)
            zrot_ref[...] = jnp.zeros_like(zrot_ref)
            for h, wmh in enumerate(_masked_sgu_w(sguw_ref)):
                wm_ref[h] = wmh

        xv = _interleave(x_ref[...]) if natural_x else x_ref[...]
        sh1, sc1, g1 = mod_ref[0:1, :], mod_ref[1:2, :], mod_ref[2:3, :]
        xhat, _ = _rms_fwd(xv)
        h1 = xhat * g_ref[0:1, :] * (1.0 + sc1) + sh1
        z = _dot_nt(h1.astype(BF16), win_ref[...])
        z_ref[...] = z
        r = _branches_fwd(z, i, p384_ref, cw_ref, [wm_ref[h] for h in range(SGU_HEADS)], bmat_ref, pwbd_ref, psc_ref,
                          segp_ref, g_ref, hext_ref, zext_ref, hrot_ref[...], zrot_ref[...])
        hrot_ref[...] = r["h_rot"]
        zrot_ref[...] = r["z_rot"]
        conv_ref[...] = r["conv"]
        o = _dot(r["ycat"].astype(BF16), wout_ref[...])
        o_ref[...] = o
        ohat, _ = _rms_fwd(o)
        x1_ref[...] = xv + g1 * (ohat * g_ref[1:2, :])

    tile = pl.BlockSpec((t, D_MODEL), lambda i: (i, 0))
    consts = (modv, g1024, p384, cw, sguw, bmat, pwbd, psc, segp, win, wout)
    return _pallas_call_with_exchange(
        body,
        grid=(s_len // t,),
        in_specs=[tile] + [_whole(c.shape) for c in consts],
        out_specs=[tile, pl.BlockSpec((t, IN_WIDTH), lambda i: (i, 0)), tile,
                   pl.BlockSpec((t, CONV_WIDTH), lambda i: (i, 0))],
        out_shape=[jax.ShapeDtypeStruct((s_len, D_MODEL), F32), jax.ShapeDtypeStruct((s_len, IN_WIDTH), F32),
                   jax.ShapeDtypeStruct((s_len, D_MODEL), F32), jax.ShapeDtypeStruct((s_len, CONV_WIDTH), F32)],
        scratch_shapes=[pltpu.VMEM((2 * t, CONV_WIDTH), F32), pltpu.VMEM((POOL_HALO + t, POOL_WIDTH), F32),
                        pltpu.VMEM((t, CONV_WIDTH), F32), pltpu.VMEM((POOL_HALO, POOL_WIDTH), F32),
                        pltpu.VMEM((SGU_HEADS, t, t), BF16)],
        operands=(x, *consts),
        name=name, job=job, relay_steps_before_end=relay_steps_before_end)


def _mixer_bwd(dx1, x, o, z, conv, modv, g1024, p384, cw, sguw, bmat, pwbd, psc, segp, win, wout, name="mixer_bwd",
               job=None, natural_x=False):
    s_len = x.shape[0]
    t = MIX_TILE
    n_tiles = s_len // t

    def body(dx1_ref, x_ref, o_ref, z_ref, zh_ref, conv_ref, mod_ref, g_ref, p384_ref, cw_ref, sguw_ref, bmat_ref, pwbd_ref,
             psc_ref, segp_ref, win_ref, wout_ref,
             dx_ref, dz_ref, do_ref, ycat_ref, h1_ref, vec_ref, v384_ref, dcw_ref, dsguw_out_ref, dbmat_out_ref, dpw_ref,
             dpsc_ref, zext_ref, gext_ref, qext_ref, grot_ref, qrot_ref, wm_ref, dsguw_ref, dbmat_ref):
        i = pl.program_id(0)
        tile_idx = n_tiles - 1 - i

        @pl.when(i == 0)
        def _():
            for ref in (vec_ref, v384_ref, dcw_ref, dsguw_ref, dbmat_ref, dpw_ref, dpsc_ref, grot_ref, qrot_ref):
                ref[...] = jnp.zeros_like(ref)
            for h, wmh in enumerate(_masked_sgu_w(sguw_ref)):
                wm_ref[h] = wmh

        dx1v, ov, z = dx1_ref[...], o_ref[...], z_ref[...]
        xv = _interleave(x_ref[...]) if natural_x else x_ref[...]
        sh1, sc1, g1 = mod_ref[0:1, :], mod_ref[1:2, :], mod_ref[2:3, :]
        pre_g, post_g, bg = g_ref[0:1, :], g_ref[1:2, :], g_ref[2:3, :]
        segp = segp_ref[...]

        ohat, ro = _rms_fwd(ov)
        vec_ref[1:2, :] += _colsum(dx1v * (ohat * post_g))
        don = dx1v * g1
        vec_ref[0:1, :] += _colsum(don * ohat)
        dob = _rms_bwd(don * post_g, ohat, ro).astype(BF16)
        do_ref[...] = dob
        dycat = _dot_nt(dob, wout_ref[...])

        not_first = (tile_idx > 0).astype(F32)
        o_b = 2 * SGU_WIDTH
        o_c = o_b + 2 * CONV_WIDTH
        z_prev_rot = _rot_rows(zh_ref[:, o_c:o_c + POOL_WIDTH], 1) * not_first
        wm = [wm_ref[h] for h in range(SGU_HEADS)]
        r = _branches_fwd(z, tile_idx, p384_ref, cw_ref, wm, bmat_ref, pwbd_ref, psc_ref, segp_ref, g_ref,
                          None, zext_ref, None, z_prev_rot, conv_saved=conv_ref[...])
        ycat_ref[...] = r["ycat"].astype(BF16)

        def branch_norm_bwd(dyn, yhat, rr, gain):
            return _colsum(dyn * yhat), _rms_bwd(dyn * gain, yhat, rr)

        dga, dya = branch_norm_bwd(dycat[:, 0:384], r["yha"], r["ra"], bg[:, 0:384])
        dgb, dyb = branch_norm_bwd(dycat[:, 384:768], r["yhb"], r["rb"], bg[:, 384:768])
        dgc, dyc = branch_norm_bwd(dycat[:, 768:1024], r["yhc"], r["rc"], bg[:, 768:1024])
        vec_ref[5:6, :] += jnp.concatenate([dga, dgb, dgc], axis=1)

        du_act = dya * r["f"]
        df = dya * r["u"]
        first_head = _lane((t, 128)) < HEAD_DIM
        dbmat_ref[...] += df
        dvn_pairs = []
        for pr in range(SGU_HEADS // 2):
            dfp = df[:, pr * 128:(pr + 1) * 128]
            df0 = jnp.where(first_head, dfp, 0.0).astype(BF16)
            df1 = jnp.where(first_head, 0.0, dfp).astype(BF16)
            vp = r["vnb"][:, pr * 128:(pr + 1) * 128]
            dvn_pairs.append(_dot_tn(wm[2 * pr], df0) + _dot_tn(wm[2 * pr + 1], df1))
            dsguw_ref[2 * pr] += _dot_nt(df0, vp)
            dsguw_ref[2 * pr + 1] += _dot_nt(df1, vp)
        dvn = jnp.concatenate(dvn_pairs, axis=1)
        v384_ref[0:1, :] += _colsum(dvn * r["xh_v"])
        v384_ref[1:2, :] += _colsum(dvn)
        dxh = dvn * p384_ref[0:1, :]
        dvraw = r["rstd_v"] * (dxh - _seg_mean(dxh, segp) - r["xh_v"] * _seg_mean(dxh * r["xh_v"], segp))
        zu, zv = z[:, 0:SGU_WIDTH], z[:, SGU_WIDTH:o_b]
        _, tu = _gelu(zu)
        _, tv = _gelu(zv)
        dz_u = du_act * _gelu_grad(zu, tu)
        dz_v = dvraw * _gelu_grad(zv, tv)

        cn, sig_c = r["cn"], r["sig_c"]
        dcn = dyb * (sig_c * (1.0 + cn * (1.0 - sig_c)))
        v384_ref[3:4, :] += _colsum(dcn * r["xh_c"])
        v384_ref[4:5, :] += _colsum(dcn)
        dxc = dcn * p384_ref[3:4, :]
        gconv = r["rstd_c"] * (dxc - _rowmean(dxc) - r["xh_c"] * _rowmean(dxc * r["xh_c"]))
        v384_ref[2:3, :] += _colsum(gconv)
        gext_ref[0:t, :] = gconv
        gext_ref[t:2 * t, :], grot_ref[...] = _lookahead_head(gconv, grot_ref[...])
        dhh = jnp.zeros((t, CONV_WIDTH), F32)
        hh = r["hh"]
        for k in range(CONV_K):
            shift = CONV_K - 1 - k
            g_ahead = gext_ref[pl.ds(8 * shift, t), :]
            dcw_ref[k:k + 1, :] += _colsum(g_ahead * hh)
            dhh = dhh + cw_ref[k:k + 1, :] * g_ahead
        sig_g = r["sig_g"]
        dz_a = dhh * sig_g
        dz_g = dhh * r["a_in"] * sig_g * (1.0 - sig_g)

        dpsc_ref[0:1, :] += _colsum(dyc * r["ycp"])
        dycp = (dyc * psc_ref[0:1, :]).astype(BF16)
        dpw_ref[...] += _dot_tn(r["dpool"].astype(BF16), dycp)
        ddp = _dot_nt(dycp, pwbd_ref[...])
        inv = _pool_inv_counts(t, tile_idx)
        q = ddp * _by_pool_group((t, POOL_WIDTH), inv)
        qext_ref[0:t, :] = q
        qext_ref[t:t + POOL_HALO, :], qrot_ref[...] = _lookahead_head(q[0:POOL_HALO, :], qrot_ref[...])
        sums, acc = [], q
        for j in range(1, POOL_WINDOWS[-1]):
            acc = acc + qext_ref[pl.ds(8 * j, t), :]
            if j + 1 in POOL_WINDOWS:
                sums.append(acc)
        dz_c = _by_pool_group((t, POOL_WIDTH), sums) - ddp

        dzb = jnp.concatenate([dz_u, dz_v, dz_a, dz_g, dz_c], axis=1).astype(BF16)
        dz_ref[...] = dzb
        dh1 = _dot(dzb, win_ref[...])

        xhat, rx = _rms_fwd(xv)
        xn = xhat * pre_g
        h1_ref[...] = (xn * (1.0 + sc1) + sh1).astype(BF16)
        vec_ref[2:3, :] += _colsum(dh1)
        vec_ref[3:4, :] += _colsum(dh1 * xn)
        dxn = dh1 * (1.0 + sc1)
        vec_ref[4:5, :] += _colsum(dxn * xhat)
        dx = dx1v + _rms_bwd(dxn * pre_g, xhat, rx)
        dx_ref[...] = _deinterleave(dx) if natural_x else dx

        @pl.when(i == n_tiles - 1)
        def _():
            keep = _sgu_keep_mask(t)
            for h in range(SGU_HEADS):
                rows_natural = _deinterleave(jnp.where(keep, dsguw_ref[h], 0.0))
                natural = _deinterleave(rows_natural.T).T
                dsguw_out_ref[h] = sum(natural[b * CHUNK:(b + 1) * CHUNK, b * CHUNK:(b + 1) * CHUNK]
                                       for b in range(t // CHUNK))
            dbmat = _deinterleave(float(HEAD_DIM) * _seg_mean(dbmat_ref[...], segp))
            dbmat_out_ref[...] = sum(dbmat[b * CHUNK:(b + 1) * CHUNK, :] for b in range(t // CHUNK))

    rev = lambda i: (n_tiles - 1 - i, 0)
    tile = pl.BlockSpec((t, D_MODEL), rev)
    ztile = pl.BlockSpec((t, IN_WIDTH), rev)
    zhalo = pl.BlockSpec((POOL_HALO, IN_WIDTH),
                         lambda i: (jnp.maximum((n_tiles - 1 - i) * (t // POOL_HALO) - 1, 0), 0))
    consts = (modv, g1024, p384, cw, sguw, bmat, pwbd, psc, segp, win, wout)
    acc = lambda shape: pl.BlockSpec(shape, lambda i: (0,) * len(shape))
    acc_shapes = [(8, D_MODEL), (8, SGU_WIDTH), (32, CONV_WIDTH), (SGU_HEADS, CHUNK, CHUNK), (CHUNK, SGU_WIDTH),
                  (POOL_WIDTH, POOL_WIDTH), (8, POOL_WIDTH)]
    return _pallas_call_with_exchange(
        body,
        grid=(n_tiles,),
        in_specs=[tile, tile, tile, ztile, zhalo, pl.BlockSpec((t, CONV_WIDTH), rev)] + [_whole(c.shape) for c in consts],
        out_specs=[tile, ztile, tile, tile, tile] + [acc(s) for s in acc_shapes],
        out_shape=[jax.ShapeDtypeStruct((s_len, D_MODEL), F32), jax.ShapeDtypeStruct((s_len, IN_WIDTH), BF16),
                   jax.ShapeDtypeStruct((s_len, D_MODEL), BF16), jax.ShapeDtypeStruct((s_len, D_MODEL), BF16),
                   jax.ShapeDtypeStruct((s_len, D_MODEL), BF16)] + [jax.ShapeDtypeStruct(s, F32) for s in acc_shapes],
        scratch_shapes=[pltpu.VMEM((POOL_HALO + t, POOL_WIDTH), F32),
                        pltpu.VMEM((2 * t, CONV_WIDTH), F32), pltpu.VMEM((t + POOL_HALO, POOL_WIDTH), F32),
                        pltpu.VMEM((t, CONV_WIDTH), F32), pltpu.VMEM((POOL_HALO, POOL_WIDTH), F32),
                        pltpu.VMEM((SGU_HEADS, t, t), BF16), pltpu.VMEM((SGU_HEADS, t, t), F32),
                        pltpu.VMEM((t, SGU_WIDTH), F32)],
        operands=(dx1, x, o, z, z, conv, *consts),
        name=name, job=job)


MOD_SHARD = 6 * D_MODEL // N_DEV

ROW_DMOD = 0
ROW_G1024 = 8
ROW_V384 = 16
ROW_SGU_B = 24
ROW_POOL_SCALE = 25
ROW_LOSS = 26
ROW_CONV_W = 32
ROW_FFN_CONV = 64
ROW_POOL_W = 96
ROW_SGU_W = 112
ROWS_PER_LAYER = 208
N_LAYERS = 2


def _gather_weights(c8, mod_w, mod_b8, job):
    kinds = [kind for kind, _ in job]
    shards = [a for _, a in job]
    n = len(shards)

    def body(c_ref, modw_ref, modb_ref, *rest):
        shard_refs = rest[:n]
        sc_all_ref, modrows_ref = rest[n], rest[n + 1]
        full_refs = rest[n + 2:2 * n + 2]
        send_buf, mod_recv, w_send, w_recv, w_local, sc_send, sc_recv, mod_send, mod_recv_sem = rest[2 * n + 2:]
        pos = _my_pos()
        me = _flat(pos)
        peers = [_peer(pos, k) for k in range(1, N_DEV)]

        w_copies = _exchange_copies(kinds, shard_refs, full_refs, w_send, w_recv, w_local, phase=0)
        for cp in w_copies:
            cp.start()

        cv = c_ref[...]
        sc_all_ref[me] = cv * jax.nn.sigmoid(cv)
        sc_copies = [_remote_copy(sc_all_ref.at[me], sc_all_ref.at[me], sc_send.at[k], sc_recv.at[k], peers[k])
                     for k in range(N_PEERS)]
        for cp in sc_copies:
            cp.start()
        for cp in sc_copies:
            cp.wait()

        sc = jnp.concatenate([sc_all_ref[j, 0:1, :] for j in range(N_DEV)], axis=0)
        send_buf[...] = jnp.zeros_like(send_buf)
        for l in range(N_LAYERS):
            part = jnp.dot(sc, modw_ref[l], precision=lax.Precision.HIGHEST, preferred_element_type=F32)
            for j in range(N_DEV):
                send_buf[j, l:l + 1, :] = part[j:j + 1, :]
        mod_recv[me] = send_buf[me]
        mod_copies = [_remote_copy(send_buf.at[_flat(peers[k])], mod_recv.at[me], mod_send.at[k], mod_recv_sem.at[k],
                                   peers[k]) for k in range(N_PEERS)]
        for cp in mod_copies:
            cp.start()
        for cp in mod_copies:
            cp.wait()
        modrows_ref[...] = jnp.zeros_like(modrows_ref)
        for l in range(N_LAYERS):
            row = jnp.concatenate([mod_recv[j, l:l + 1, :] for j in range(N_DEV)], axis=1)
            modrows_ref[l:l + 1, :] = row + modb_ref[l:l + 1, :]

        for cp in w_copies:
            cp.wait()
        relays = _exchange_copies(kinds, shard_refs, full_refs, w_send, w_recv, w_local, phase=1)
        for cp in relays:
            cp.start()
        for cp in relays:
            cp.wait()

    out_shape = ([jax.ShapeDtypeStruct((N_DEV, 8, D_MODEL), F32), jax.ShapeDtypeStruct((8, 6 * D_MODEL), F32)]
                 + [jax.ShapeDtypeStruct((N_DEV,) + s.shape, s.dtype) for s in shards])
    return pl.pallas_call(
        body,
        in_specs=[VMEM, VMEM, VMEM] + [ANY] * n,
        out_specs=[VMEM, VMEM] + [ANY] * n,
        out_shape=out_shape,
        scratch_shapes=[pltpu.VMEM((N_DEV, 8, MOD_SHARD), F32), pltpu.VMEM((N_DEV, 8, MOD_SHARD), F32),
                        pltpu.SemaphoreType.DMA((n, N_PEERS)), pltpu.SemaphoreType.DMA((n, N_PEERS)),
                        pltpu.SemaphoreType.DMA((n,)),
                        pltpu.SemaphoreType.DMA((N_PEERS,)), pltpu.SemaphoreType.DMA((N_PEERS,)),
                        pltpu.SemaphoreType.DMA((N_PEERS,)), pltpu.SemaphoreType.DMA((N_PEERS,))],
        compiler_params=pltpu.CompilerParams(vmem_limit_bytes=VMEM_LIMIT_BYTES),
        name="gather_weights",
    )(c8, mod_w, mod_b8, *shards)


def _small_sums(sc_all, small_all0, small_all1, job):
    kinds = [kind for kind, _ in job]
    n = len(job)

    def body(sc_all_ref, small_all0_ref, small_all1_ref, *rest):
        small_sum_ref, gmodw_ref = rest[n], rest[n + 1]
        copies = _exchange_copies(kinds, rest[:n], rest[n + 2:2 * n + 2], *rest[2 * n + 2:], phase=0)
        for cp in copies:
            cp.start()
        me = _flat(_my_pos())
        sc = jnp.concatenate([sc_all_ref[j, 0:1, :] for j in range(N_DEV)], axis=0)
        mine = lax.broadcasted_iota(jnp.int32, (2 * N_DEV, MOD_SHARD), 0) == me
        for l, parts in enumerate((small_all0_ref, small_all1_ref)):
            total = parts[0].astype(F32)
            for j in range(1, N_DEV):
                total = total + parts[j].astype(F32)
            small_sum_ref[l] = total
            dm = jnp.concatenate(
                [jnp.sum(jnp.where(mine, parts[j, ROW_DMOD:ROW_DMOD + 2 * N_DEV, 0:MOD_SHARD].astype(F32), 0.0),
                         axis=0, keepdims=True) for j in range(N_DEV)], axis=0)
            gmodw_ref[l] = lax.dot_general(sc, dm, (((0,), (0,)), ((), ())), precision=lax.Precision.HIGHEST,
                                           preferred_element_type=F32)
        for cp in copies:
            cp.wait()

    res = pl.pallas_call(
        body,
        in_specs=[VMEM, VMEM, VMEM] + [ANY] * n,
        out_specs=[VMEM, VMEM] + [ANY] * n,
        out_shape=[jax.ShapeDtypeStruct((N_LAYERS, ROWS_PER_LAYER, D_MODEL), F32),
                   jax.ShapeDtypeStruct((N_LAYERS, D_MODEL, MOD_SHARD), F32)] + _exchange_out_shapes(job),
        scratch_shapes=_exchange_sems(n),
        compiler_params=pltpu.CompilerParams(vmem_limit_bytes=VMEM_LIMIT_BYTES),
        name="small_sums",
    )(sc_all, small_all0, small_all1, *[a for _, a in job])
    return res[0], res[1], res[2:]


def _adam_update(g, w, m, v):
    m2 = ADAM_B1 * m + (1.0 - ADAM_B1) * g
    v2 = ADAM_B2 * v + (1.0 - ADAM_B2) * (g * g)
    m_hat = m2 / (1.0 - ADAM_B1 ** ADAM_STEP)
    v_hat = v2 / (1.0 - ADAM_B2 ** ADAM_STEP)
    delta = -ADAM_LR * (m_hat / (jnp.sqrt(v_hat) + ADAM_EPS) + ADAM_WD * w)
    return delta, m2, v2


def _pair_add(g, r1, row_chunk, name):
    _, rows, cols = g.shape
    core = lax.axis_index("c").astype(jnp.int32).reshape(1)

    def body(core_ref, g_ref, r_ref, o_ref):
        o_ref[0] = (g_ref[0, 0].astype(F32) + r_ref[0].astype(F32)).astype(BF16)

    blk = pl.BlockSpec((1, row_chunk, cols), lambda q, i, core_ref: (q, i, 0))
    grid_spec = pltpu.PrefetchScalarGridSpec(
        num_scalar_prefetch=1, grid=(N_CHIPS, rows // row_chunk),
        in_specs=[pl.BlockSpec((1, 1, row_chunk, cols), lambda q, i, core_ref: (q, core_ref[0], i, 0)), blk],
        out_specs=blk)
    return pl.pallas_call(
        body, grid_spec=grid_spec, out_shape=jax.ShapeDtypeStruct((N_CHIPS, rows, cols), BF16),
        compiler_params=_cparams(2), name=name,
    )(core, g.reshape(N_CHIPS, 2, rows, cols), r1)


def _adam_sharded(recv0, recv1, w, m, v, row_chunk, name):
    _, rows, cols = w.shape
    n_chunks = rows // row_chunk

    def body(r0_ref, r1_ref, w_ref, m_ref, v_ref, g_ref, d_ref, m2_ref, v2_ref):
        layer = pl.program_id(0)

        def run(r_ref):
            g = r_ref[0].astype(F32)
            for j in range(1, r_ref.shape[0]):
                g = g + r_ref[j].astype(F32)
            delta, m2, v2 = _adam_update(g, w_ref[0], m_ref[0], v_ref[0])
            g_ref[0], d_ref[0], m2_ref[0], v2_ref[0] = g, delta, m2, v2

        @pl.when(layer == 0)
        def _():
            run(r0_ref)

        @pl.when(layer == 1)
        def _():
            run(r1_ref)

    r0_spec = pl.BlockSpec((recv0.shape[0], row_chunk, cols), lambda l, i: (0, i * (1 - l) + (n_chunks - 1) * l, 0))
    r1_spec = pl.BlockSpec((recv1.shape[0], row_chunk, cols), lambda l, i: (0, i * l, 0))
    blk = pl.BlockSpec((1, row_chunk, cols), lambda l, i: (l, i, 0))
    out = jax.ShapeDtypeStruct(w.shape, F32)
    return pl.pallas_call(
        body,
        grid=(N_LAYERS, n_chunks),
        in_specs=[r0_spec, r1_spec, blk, blk, blk],
        out_specs=[blk] * 4,
        out_shape=[out] * 4,
        compiler_params=_cparams(2),
        name=name,
    )(recv0, recv1, w, m, v)


def _adam_dense(g, w, m, v, row_chunk, name):
    n_lead, rows, cols = w.shape

    def body(g_ref, w_ref, m_ref, v_ref, go_ref, d_ref, m2_ref, v2_ref):
        gv = g_ref[...]
        go_ref[...] = gv
        d_ref[...], m2_ref[...], v2_ref[...] = _adam_update(gv, w_ref[...], m_ref[...], v_ref[...])

    blk = pl.BlockSpec((1, row_chunk, cols), lambda l, i: (l, i, 0))
    out = jax.ShapeDtypeStruct(w.shape, F32)
    return pl.pallas_call(
        body,
        grid=(n_lead, rows // row_chunk),
        in_specs=[blk] * 4,
        out_specs=[blk] * 4,
        out_shape=[out] * 4,
        compiler_params=_cparams(2),
        name=name,
    )(g, w, m, v)


WEIGHT_NAMES = ("mod_w", "mod_b", "mix_pre_g", "mix_post_g", "w_in", "sgu_norm_g", "sgu_norm_b", "sgu_w", "sgu_b",
                "conv_w", "conv_b", "conv_norm_g", "conv_norm_b", "pool_w", "pool_scale", "branch_g", "w_out",
                "ffn_pre_g", "ffn_post_g", "ffn_up", "ffn_conv_w", "ffn_conv_b", "ffn_down")
SHARDED_BIG = ("w_in", "w_out", "ffn_up", "ffn_down")
SMALL_PACKED = tuple(n for n in WEIGHT_NAMES if n not in SHARDED_BIG + ("mod_w",))


def _rows8(rows, width=D_MODEL):
    out = [jnp.pad(r.astype(F32), (0, width - r.shape[0]))[None] for r in rows]
    out.append(jnp.zeros((8 - len(rows), width), F32))
    return jnp.concatenate(out, axis=0)


def _as_rows(a, width=D_MODEL):
    flat = a.astype(F32).reshape(-1)
    pad = (-flat.shape[0]) % width
    return jnp.pad(flat, (0, pad)).reshape(-1, width)


def _pad_cols(a, width=D_MODEL):
    return jnp.pad(a.astype(F32), ((0, 0), (0, width - a.shape[1])))


def _pack_rows(arrays):
    rows = jnp.concatenate([_as_rows(a) for a in arrays], axis=0)
    return jnp.pad(rows, ((0, (-rows.shape[0]) % 8), (0, 0)))


def _unpack_rows(packed, shapes):
    out, r = [], 0
    for shape in shapes:
        size = math.prod(shape)
        n_rows = -(-size // D_MODEL)
        out.append(packed[r:r + n_rows].reshape(-1)[:size].reshape(shape))
        r += n_rows
    return out


TILE_VREGS = MIX_TILE // 8
CHUNK_SUBLANES = CHUNK // TILE_VREGS
CHUNKS_PER_TILE = MIX_TILE // CHUNK


def _chunk_axis_to_tile(a, axis):
    shape = a.shape
    a = a.reshape(shape[:axis] + (CHUNK_SUBLANES, TILE_VREGS) + shape[axis + 1:])
    a = jnp.swapaxes(a, axis, axis + 1)
    a = jnp.tile(a, (1,) * (axis + 1) + (CHUNKS_PER_TILE,) + (1,) * (len(shape) - axis - 1))
    return a.reshape(shape[:axis] + (MIX_TILE,) + shape[axis + 1:])


def _layer_consts(l, w, mod_rows, win, wout, conv_w_full):
    modv = _rows8(list(mod_rows[l].reshape(6, D_MODEL)))
    g1024 = _rows8([w["mix_pre_g"][l], w["mix_post_g"][l], w["branch_g"][l], w["ffn_pre_g"][l], w["ffn_post_g"][l]])
    p384 = _rows8([w["sgu_norm_g"][l], w["sgu_norm_b"][l], w["conv_b"][l], w["conv_norm_g"][l], w["conv_norm_b"][l]],
                  SGU_WIDTH)
    cw = jnp.pad(conv_w_full[l], ((0, 32 - CONV_K), (0, 0)))
    sguw = _chunk_axis_to_tile(_chunk_axis_to_tile(w["sgu_w"][l], 2), 1)
    bmat = _chunk_axis_to_tile(jnp.repeat(w["sgu_b"][l].T, HEAD_DIM, axis=1), 0)
    groups = len(POOL_WINDOWS)
    eye = jnp.eye(groups, dtype=F32)
    pwbd = (eye[:, None, :, None] * w["pool_w"][l][:, :, None, :]).reshape(POOL_WIDTH, POOL_WIDTH).astype(BF16)
    psc = _rows8([w["pool_scale"][l]], POOL_WIDTH)
    seg = jnp.arange(SGU_WIDTH) // HEAD_DIM
    segp = jnp.where(seg[:, None] == seg[None, :], 1.0 / HEAD_DIM, 0.0).astype(BF16)
    return modv, g1024, (modv, g1024, p384, cw, sguw, bmat, pwbd, psc, segp, win, wout)


def _small_grad_rows(mix, ffn, loss_pieces=None):
    _, _, _, _, _, mvec, v384, dcw, dsguw, dbmat, dpw, dpsc = mix
    fvec, cgrad = ffn[5], ffn[6]
    dmod = jnp.stack([mvec[2], mvec[3], mvec[1], fvec[2], fvec[3], fvec[1]]).reshape(N_DEV, MOD_SHARD)
    g_rows = jnp.stack([mvec[4], mvec[0], mvec[5], fvec[4], fvec[0]])
    dsgu_b = dbmat[:, ::HEAD_DIM].T.reshape(1, SGU_HEADS * CHUNK)
    groups = len(POOL_WINDOWS)
    gdim = POOL_WIDTH // groups
    dpw4 = dpw.reshape(groups, gdim, groups, gdim)
    dpool = jnp.stack([dpw4[g, :, g, :] for g in range(groups)])
    misc = [dsgu_b[0], dpsc[0]] + ([] if loss_pieces is None else [loss_pieces])
    blocks = [_pad_cols(dmod), _rows8(list(g_rows)), _pad_cols(v384), _rows8(misc), _pad_cols(dcw),
              _pad_cols(cgrad[:, 0:4, :].reshape(4 * N_DEV, FF_SHARD)), _as_rows(dpool), _as_rows(dsguw)]
    return jnp.concatenate(blocks, axis=0)


def _small_grads_from_rows(total):
    per = {n: [] for n in SMALL_PACKED}
    for l in range(N_LAYERS):
        s = total[l]
        per["mod_b"].append(s[ROW_DMOD:ROW_DMOD + N_DEV, :MOD_SHARD].reshape(6 * D_MODEL))
        for j, name in enumerate(("mix_pre_g", "mix_post_g", "branch_g", "ffn_pre_g", "ffn_post_g")):
            per[name].append(s[ROW_G1024 + j])
        for j, name in enumerate(("sgu_norm_g", "sgu_norm_b", "conv_b", "conv_norm_g", "conv_norm_b")):
            per[name].append(s[ROW_V384 + j, :SGU_WIDTH])
        per["sgu_b"].append(s[ROW_SGU_B, :SGU_HEADS * CHUNK].reshape(SGU_HEADS, CHUNK))
        per["pool_scale"].append(s[ROW_POOL_SCALE, :POOL_WIDTH])
        per["conv_w"].append(s[ROW_CONV_W:ROW_CONV_W + CONV_K, :CONV_WIDTH])
        fc = s[ROW_FFN_CONV:ROW_FFN_CONV + 4 * N_DEV, :FF_SHARD].reshape(N_DEV, 4, FF_SHARD)
        per["ffn_conv_w"].append(fc[:, 0:3, :].transpose(1, 0, 2).reshape(FFN_CONV_K, 2 * D_FF))
        per["ffn_conv_b"].append(fc[:, 3, :].reshape(2 * D_FF))
        per["pool_w"].append(s[ROW_POOL_W:ROW_POOL_W + 16].reshape(len(POOL_WINDOWS), HEAD_DIM, HEAD_DIM))
        per["sgu_w"].append(s[ROW_SGU_W:ROW_SGU_W + 96].reshape(SGU_HEADS, CHUNK, CHUNK))
    return {n: jnp.stack(v) for n, v in per.items()}


def kernel(x, c, mod_w, mod_b, mix_pre_g, mix_post_g, w_in, sgu_norm_g, sgu_norm_b, sgu_w, sgu_b, conv_w, conv_b, conv_norm_g, conv_norm_b, pool_w, pool_scale, branch_g, w_out, ffn_pre_g, ffn_post_g, ffn_up, ffn_conv_w, ffn_conv_b, ffn_down, loss_target, m_mod_w, m_mod_b, m_mix_pre_g, m_mix_post_g, m_w_in, m_sgu_norm_g, m_sgu_norm_b, m_sgu_w, m_sgu_b, m_conv_w, m_conv_b, m_conv_norm_g, m_conv_norm_b, m_pool_w, m_pool_scale, m_branch_g, m_w_out, m_ffn_pre_g, m_ffn_post_g, m_ffn_up, m_ffn_conv_w, m_ffn_conv_b, m_ffn_down, v_mod_w, v_mod_b, v_mix_pre_g, v_mix_post_g, v_w_in, v_sgu_norm_g, v_sgu_norm_b, v_sgu_w, v_sgu_b, v_conv_w, v_conv_b, v_conv_norm_g, v_conv_norm_b, v_pool_w, v_pool_scale, v_branch_g, v_w_out, v_ffn_pre_g, v_ffn_post_g, v_ffn_up, v_ffn_conv_w, v_ffn_conv_b, v_ffn_down):
    w = dict(zip(WEIGHT_NAMES, (mod_w, mod_b, mix_pre_g, mix_post_g, w_in, sgu_norm_g, sgu_norm_b, sgu_w, sgu_b, conv_w,
                                conv_b, conv_norm_g, conv_norm_b, pool_w, pool_scale, branch_g, w_out, ffn_pre_g,
                                ffn_post_g, ffn_up, ffn_conv_w, ffn_conv_b, ffn_down)))
    m = dict(zip(WEIGHT_NAMES, (m_mod_w, m_mod_b, m_mix_pre_g, m_mix_post_g, m_w_in, m_sgu_norm_g, m_sgu_norm_b, m_sgu_w,
                                m_sgu_b, m_conv_w, m_conv_b, m_conv_norm_g, m_conv_norm_b, m_pool_w, m_pool_scale,
                                m_branch_g, m_w_out, m_ffn_pre_g, m_ffn_post_g, m_ffn_up, m_ffn_conv_w, m_ffn_conv_b,
                                m_ffn_down)))
    v = dict(zip(WEIGHT_NAMES, (v_mod_w, v_mod_b, v_mix_pre_g, v_mix_post_g, v_w_in, v_sgu_norm_g, v_sgu_norm_b, v_sgu_w,
                                v_sgu_b, v_conv_w, v_conv_b, v_conv_norm_g, v_conv_norm_b, v_pool_w, v_pool_scale,
                                v_branch_g, v_w_out, v_ffn_pre_g, v_ffn_post_g, v_ffn_up, v_ffn_conv_w, v_ffn_conv_b,
                                v_ffn_down)))
    me = _flat(_my_pos())
    xs = x[0]

    transposed = ("w_in", "ffn_up")
    wt = {n: jnp.swapaxes(w[n], 1, 2) if n in transposed else w[n] for n in SHARDED_BIG}
    mt = {n: jnp.swapaxes(m[n], 1, 2) if n in transposed else m[n] for n in SHARDED_BIG}
    vt = {n: jnp.swapaxes(v[n], 1, 2) if n in transposed else v[n] for n in SHARDED_BIG}
    bf16_shards = [[wt[n][l].astype(BF16) for n in SHARDED_BIG] for l in range(N_LAYERS)]

    def mixer_operands(l, win_g, wout_g):
        win = win_g.reshape(IN_WIDTH, D_MODEL)
        return _layer_consts(l, w, mod_rows, win, wout_g.reshape(D_MODEL, D_MODEL), conv_w_full)

    def ffn_operands(l, modv, g1024, wup_g, wdn_g):
        wdn = wdn_g.reshape(FF_PAIRS, FF_SHARD, D_MODEL)
        return modv, g1024, wup_g, wdn, ffn_cw_full[:, l], ffn_conv_b[l].reshape(N_DEV, 1, FF_SHARD)

    w0, w1 = bf16_shards
    c8 = jnp.broadcast_to(c, (8, D_MODEL))
    mod_b8 = jnp.pad(mod_b, ((0, 8 - N_LAYERS), (0, 0)))
    sc_all, mod_rows, win0_g, wout0_g, conv_w_g, ffn_cw_full = _gather_weights(
        c8, mod_w, mod_b8, [("gather2", w0[0]), ("gather2", w0[1]), ("gather", conv_w), ("gather", ffn_conv_w)])
    conv_w_full = conv_w_g.transpose(1, 2, 0, 3).reshape(N_LAYERS, CONV_K, CONV_WIDTH)

    modv0, g0, mix_consts0 = mixer_operands(0, win0_g, wout0_g)
    (x1, z, o, cv), (wup0_g, wdn0_g) = _mixer_fwd(xs, *mix_consts0, name="mixer_fwd_l0", natural_x=True,
                                                  job=[("gather2", w0[2]), ("gather2", w0[3])],
                                                  relay_steps_before_end=1)
    ffn_consts0 = ffn_operands(0, modv0, g0, wup0_g, wdn0_g)
    (x2, y2, p, u), (win1_g, wout1_g, wdn1_g) = _ffn_fwd(
        x1, *ffn_consts0, name="ffn_fwd_l0", job=[("gather2", w1[0]), ("gather2", w1[1]), ("gather2", w1[3])])
    saved = [(xs, z, o, cv, x1, y2, p, u)]
    modv1, g1, mix_consts1 = mixer_operands(1, win1_g, wout1_g)
    (x1, z, o, cv), (wup1_g,) = _mixer_fwd(x2, *mix_consts1, name="mixer_fwd_l1", job=[("gather2", w1[2])])
    ffn_consts1 = ffn_operands(1, modv1, g1, wup1_g, wdn1_g)
    (dh, y2, p, u, loss_tile), _ = _ffn_fwd(x1, *ffn_consts1, name="ffn_fwd_l1", loss_target=loss_target[0])
    saved.append((x2, z, o, cv, x1, y2, p, u))
    loss_local = jnp.sum(loss_tile)
    loss_hi = loss_local.astype(BF16).astype(F32)
    loss_mid = (loss_local - loss_hi).astype(BF16).astype(F32)
    loss_pieces = jnp.stack([loss_hi, loss_mid, loss_local - loss_hi - loss_mid])

    def ffn_weight_grads(l, ffn):
        dp, a, dy2, h2 = ffn[1:5]
        d_up = _wgrad(dp, h2, f"wgrad_ffn_up_l{l}", tk=WGRAD_TK_FFN)
        d_dn = _wgrad(a, dy2, f"wgrad_ffn_down_l{l}", tk=WGRAD_TK_FFN).reshape(N_DEV, D_FF // N_DEV, D_MODEL)
        return d_up, d_dn

    def mixer_weight_grads(l, mix):
        dz, do, ycat, h1 = mix[1:5]
        d_in = _wgrad(dz[None], h1, f"wgrad_w_in_l{l}").reshape(N_DEV, IN_WIDTH // N_DEV, D_MODEL)
        d_out = _wgrad(ycat, do[None], f"wgrad_w_out_l{l}").reshape(N_DEV, D_MODEL // N_DEV, D_MODEL)
        return d_in, d_out

    x_in, z, o, cv, x1, y2, p, u = saved[1]
    ffn1, _ = _ffn_bwd(dh, x1, y2, p, u, *ffn_consts1[:-1], name="ffn_bwd_l1")
    d_up1, d_dn1 = ffn_weight_grads(1, ffn1)
    mix1, (sib_up1, sib_dn1) = _mixer_bwd(ffn1[0], x_in, o, z, cv, *mix_consts1, name="mixer_bwd_l1",
                                          job=[("scatter_p1", d_up1), ("scatter_p1", d_dn1)])
    chip_up1 = _pair_add(d_up1, sib_up1, 176, "pair_add_ffn_up_l1")
    chip_dn1 = _pair_add(d_dn1, sib_dn1, 176, "pair_add_ffn_down_l1")
    d_in1, d_out1 = mixer_weight_grads(1, mix1)
    small1 = _small_grad_rows(mix1, ffn1).astype(BF16)

    x_in, z, o, cv, x1, y2, p, u = saved[0]
    ffn0, job_out = _ffn_bwd(mix1[0], x1, y2, p, u, *ffn_consts0[:-1], name="ffn_bwd_l0",
                             job=[("scatter", d_in1), ("scatter", d_out1), ("scatter_p2", chip_up1),
                                  ("scatter_p2", chip_dn1)])
    recv1 = job_out
    dp, a, dy2, h2 = ffn0[1:5]
    d_dn0, (small_all1,) = _wgrad(a, dy2, "wgrad_ffn_down_l0", tk=WGRAD_TK_FFN, job=[("gather2", small1)])
    d_dn0 = d_dn0.reshape(N_DEV, D_FF // N_DEV, D_MODEL)
    d_up0, (recv_dn0,) = _wgrad(dp, h2, "wgrad_ffn_up_l0", tk=WGRAD_TK_FFN, job=[("scatter", d_dn0)])
    mix0, (recv_up0,) = _mixer_bwd(ffn0[0], x_in, o, z, cv, *mix_consts0, name="mixer_bwd_l0", natural_x=True,
                                   job=[("scatter", d_up0)])
    recv_ffn0 = (recv_up0, recv_dn0)
    grad_x = mix0[0][None]
    dz, do, ycat, h1 = mix0[1:5]
    small0 = _small_grad_rows(mix0, ffn0, loss_pieces).astype(BF16)
    d_in0, (small_all0,) = _wgrad(dz[None], h1, "wgrad_w_in_l0", tk=WGRAD_TK // 2, job=[("gather2", small0)])
    d_in0 = d_in0.reshape(N_DEV, IN_WIDTH // N_DEV, D_MODEL)
    d_out0, (recv_in0,) = _wgrad(ycat, do[None], "wgrad_w_out_l0", tk=WGRAD_TK // 2, job=[("scatter", d_in0)])
    d_out0 = d_out0.reshape(N_DEV, D_MODEL // N_DEV, D_MODEL)
    small_total, g_mod_w, (recv_out0,) = _small_sums(sc_all, small_all0, small_all1, [("scatter", d_out0)])
    recv0 = [recv_in0, recv_out0, recv_ffn0[0], recv_ffn0[1]]
    loss = small_total[0, ROW_LOSS, 0] + small_total[0, ROW_LOSS, 1] + small_total[0, ROW_LOSS, 2]

    grads, deltas, new_m, new_v = {}, {}, {}, {}
    for j, (name, chunk) in enumerate((("w_in", 224), ("w_out", 128), ("ffn_up", 176), ("ffn_down", 176))):
        outs = _adam_sharded(recv0[j], recv1[j], wt[name], mt[name], vt[name], chunk, "adam_" + name)
        if name in transposed:
            outs = [jnp.swapaxes(t, 1, 2) for t in outs]
        grads[name], deltas[name], new_m[name], new_v[name] = outs
    grads["mod_w"], deltas["mod_w"], new_m["mod_w"], new_v["mod_w"] = _adam_dense(
        g_mod_w, mod_w, m_mod_w, v_mod_w, 256, "adam_mod_w")

    small_g = _small_grads_from_rows(small_total)
    small_g["conv_w"] = lax.dynamic_slice_in_dim(small_g["conv_w"], me * conv_w.shape[2], conv_w.shape[2], axis=2)
    small_g["ffn_conv_w"] = lax.dynamic_slice_in_dim(small_g["ffn_conv_w"], me * FF_SHARD, FF_SHARD, axis=2)
    shapes = [w[n].shape for n in SMALL_PACKED]
    packs = [_pack_rows([src[n] for n in SMALL_PACKED])[None] for src in (small_g, w, m, v)]
    _, d, m2, v2 = _adam_dense(*packs, packs[0].shape[1], "adam_small")
    for name, dd, mm, vv in zip(SMALL_PACKED, _unpack_rows(d[0], shapes), _unpack_rows(m2[0], shapes),
                                _unpack_rows(v2[0], shapes)):
        grads[name], deltas[name], new_m[name], new_v[name] = small_g[name], dd, mm, vv

    return (loss, grad_x, *[grads[n] for n in WEIGHT_NAMES], *[deltas[n] for n in WEIGHT_NAMES],
            *[new_m[n] for n in WEIGHT_NAMES], *[new_v[n] for n in WEIGHT_NAMES])
```

```python
import math

import jax
import jax.numpy as jnp
from jax import lax
from jax.experimental import pallas as pl
from jax.experimental.pallas import tpu as pltpu

F32 = jnp.float32
BF16 = jnp.bfloat16

D_MODEL = 1024
N_DEV = 8
SGU_WIDTH = 384
CONV_WIDTH = 384
POOL_WIDTH = 256
HEAD_DIM = 64
SGU_HEADS = 6
CHUNK = 128
CONV_K = 31
POOL_WINDOWS = (2, 4, 8, 16)
IN_WIDTH = 1792
D_FF = 2816
FF_SHARD = 2 * D_FF // N_DEV
FF_PAIRS = N_DEV // 2
FFN_CONV_K = 3
EPS = 1e-6
GELU_C0 = math.sqrt(2.0 / math.pi)
GELU_C1 = 0.044715

ADAM_LR = 0.001
ADAM_B1 = 0.9
ADAM_B2 = 0.999
ADAM_EPS = 1e-08
ADAM_WD = 0.01
ADAM_STEP = 10

VMEM_LIMIT_BYTES = 56 * 1024 * 1024
TILE = 256
MIX_TILE = TILE
FFN_TILE = TILE
FFN_HALO = 8 * (FFN_CONV_K - 1)
POOL_HALO = 8 * POOL_WINDOWS[-1]
WGRAD_TK = 2048
WGRAD_TK_FFN = 4096


def _cparams(n_axes):
    return pltpu.CompilerParams(dimension_semantics=("arbitrary",) * n_axes, vmem_limit_bytes=VMEM_LIMIT_BYTES)


def _whole(shape):
    nd = len(shape)
    return pl.BlockSpec(shape, lambda *_: (0,) * nd, pipeline_mode=pl.Buffered(1))


def _dot(a, b):
    return jnp.dot(a, b, preferred_element_type=F32)


def _dot_nt(a, b):
    return lax.dot_general(a, b, (((1,), (1,)), ((), ())), preferred_element_type=F32)


def _dot_tn(a, b):
    return lax.dot_general(a, b, (((0,), (0,)), ((), ())), preferred_element_type=F32)


def _gelu(x):
    t = jnp.tanh(GELU_C0 * (x + GELU_C1 * x * x * x))
    return 0.5 * x * (1.0 + t), t


def _gelu_grad(x, t):
    return 0.5 * (1.0 + t) + 0.5 * x * (1.0 - t * t) * (GELU_C0 * (1.0 + 3.0 * GELU_C1 * x * x))


def _rowmean(x):
    return jnp.mean(x, axis=-1, keepdims=True)


def _colsum(x):
    return jnp.sum(x, axis=0, keepdims=True)


def _rms_fwd(x):
    r = lax.rsqrt(_rowmean(x * x) + EPS)
    return x * r, r


def _rms_bwd(dxhat, xhat, r):
    return r * (dxhat - xhat * _rowmean(dxhat * xhat))


N_PEERS = N_DEV - 1
ANY = pl.BlockSpec(memory_space=pl.ANY)
VMEM = pl.BlockSpec(memory_space=pltpu.VMEM)


def _my_pos():
    return lax.axis_index("x"), lax.axis_index("y"), lax.axis_index("c")


def _peer(pos, k):
    x, y, c = pos
    return (1 - x if k & 4 else x, 1 - y if k & 2 else y, 1 - c if k & 1 else c)


def _flat(pos):
    return 4 * pos[0] + 2 * pos[1] + pos[2]


def _remote_copy(src, dst, send_sem, recv_sem, peer):
    return pltpu.make_async_remote_copy(src_ref=src, dst_ref=dst, send_sem=send_sem, recv_sem=recv_sem,
                                        device_id=peer, device_id_type=pl.DeviceIdType.MESH)


N_CHIPS = N_DEV // 2
SIBLING = 1
SAME_CORE_PEERS = (2, 4, 6)


def _exchange_out_shapes(job):
    def shape(kind, a):
        if kind in ("gather", "gather2"):
            return (N_DEV,) + a.shape
        if kind == "scatter_p1":
            return (N_CHIPS,) + a.shape[1:]
        return a.shape
    return [jax.ShapeDtypeStruct(shape(kind, a), a.dtype) for kind, a in job]


def _exchange_sems(n):
    return [pltpu.SemaphoreType.DMA((n, N_PEERS)), pltpu.SemaphoreType.DMA((n, N_PEERS)), pltpu.SemaphoreType.DMA((n,))]


def _exchange_copies(kinds, src_refs, dst_refs, send_sems, recv_sems, local_sems, phase):
    pos = _my_pos()
    me = _flat(pos)
    chip, core = 2 * pos[0] + pos[1], pos[2]
    copies = []

    def remote(a, src, dst, k, sem=None):
        sem = k - 1 if sem is None else sem
        copies.append(_remote_copy(src, dst, send_sems.at[a, sem], recv_sems.at[a, sem], _peer(pos, k)))

    for a, kind in enumerate(kinds):
        src, dst = src_refs[a], dst_refs[a]
        if phase == 1:
            if kind == "gather2":
                for k in SAME_CORE_PEERS:
                    remote(a, dst.at[me ^ k], dst.at[me ^ k], SIBLING, sem=k)
        elif kind in ("gather", "gather2"):
            copies.append(pltpu.make_async_copy(src, dst.at[me], local_sems.at[a]))
            for k in (range(1, N_DEV) if kind == "gather" else (SIBLING,) + SAME_CORE_PEERS):
                remote(a, src, dst.at[me], k)
        elif kind == "scatter":
            copies.append(pltpu.make_async_copy(src.at[me], dst.at[me], local_sems.at[a]))
            for k in range(1, N_DEV):
                remote(a, src.at[me ^ k], dst.at[me], k)
        elif kind == "scatter_p1":
            for q in range(N_CHIPS):
                remote(a, src.at[2 * q + 1 - core], dst.at[q], SIBLING, sem=q)
        elif kind == "scatter_p2":
            copies.append(pltpu.make_async_copy(src.at[chip], dst.at[chip], local_sems.at[a]))
            for k in SAME_CORE_PEERS:
                remote(a, src.at[chip ^ (k >> 1)], dst.at[chip], k)
    return copies


def _pallas_call_with_exchange(body, *, grid, in_specs, out_specs, out_shape, scratch_shapes, operands, name, job,
                               relay_steps_before_end=2):
    params = _cparams(len(grid))
    if not job:
        outs = pl.pallas_call(body, grid=grid, in_specs=in_specs, out_specs=out_specs, out_shape=out_shape,
                              scratch_shapes=scratch_shapes, compiler_params=params, name=name)(*operands)
        return outs, []
    kinds = [kind for kind, _ in job]
    relayed = [kind if kind == "gather2" else None for kind in kinds]
    unrelayed = [None if kind == "gather2" else kind for kind in kinds]
    n, n_in, n_out, n_scr = len(job), len(in_specs), len(out_specs), len(scratch_shapes)
    n_steps = math.prod(grid)
    relay_step = max(n_steps - relay_steps_before_end, 0)

    def wrapped(*refs):
        ins, jin = refs[:n_in], refs[n_in:n_in + n]
        outs, jout = refs[n_in + n:n_in + n + n_out], refs[n_in + n + n_out:n_in + 2 * n + n_out]
        scr = refs[n_in + 2 * n + n_out:n_in + 2 * n + n_out + n_scr]
        sems = refs[n_in + 2 * n + n_out + n_scr:]
        step = pl.program_id(0)
        for d in range(1, len(grid)):
            step = step * grid[d] + pl.program_id(d)

        def copies(which, phase):
            return _exchange_copies(which, jin, jout, *sems, phase=phase)

        @pl.when(step == 0)
        def _():
            for cp in copies(kinds, 0):
                cp.start()

        body(*ins, *outs, *scr)

        @pl.when(step == relay_step)
        def _():
            for cp in copies(relayed, 0):
                cp.wait()
            for cp in copies(relayed, 1):
                cp.start()

        @pl.when(step == n_steps - 1)
        def _():
            for cp in copies(unrelayed, 0) + copies(relayed, 1):
                cp.wait()

    res = pl.pallas_call(
        wrapped, grid=grid,
        in_specs=list(in_specs) + [ANY] * n,
        out_specs=list(out_specs) + [ANY] * n,
        out_shape=list(out_shape) + _exchange_out_shapes(job),
        scratch_shapes=list(scratch_shapes) + _exchange_sems(n),
        compiler_params=params, name=name,
    )(*operands, *[a for _, a in job])
    return res[:n_out], res[n_out:]


def _seg_mean(x, segp):
    hi = x.astype(BF16)
    lo = (x - hi.astype(F32)).astype(BF16)
    return _dot(hi, segp) + _dot(lo, segp)


def _rot_rows(x, shift):
    m, c = x.shape
    return pltpu.roll(x.reshape(m // 8, 8, c), shift, 1).reshape(m, c)


def _sublane_is(shape, s):
    return lax.broadcasted_iota(jnp.int32, shape, 0) % 8 == s


def _causal_tail(tail, prev_rot):
    rot = _rot_rows(tail, 1)
    return jnp.where(_sublane_is(tail.shape, 0), prev_rot, rot), rot


def _lookahead_head(head, next_rot):
    rot = _rot_rows(head, 7)
    return jnp.where(_sublane_is(head.shape, 7), next_rot, rot), rot


def _tile_token_index(t, tile_idx):
    r = lax.broadcasted_iota(jnp.int32, (t, 1), 0)
    return tile_idx * t + (r % 8) * (t // 8) + r // 8


def _interleave(x):
    t, c = x.shape
    return jnp.swapaxes(x.reshape(8, t // 8, c), 0, 1).reshape(t, c)


def _deinterleave(x):
    t, c = x.shape
    return jnp.swapaxes(x.reshape(t // 8, 8, c), 0, 1).reshape(t, c)


def _ffn_fwd(x1, modv, g1024, wup, wdn, cw, cb, name="ffn_fwd", job=None, loss_target=None):
    s_len = x1.shape[0]
    t = FFN_TILE
    n_tiles = s_len // t
    with_loss = loss_target is not None

    def body(x1_ref, *rest):
        if with_loss:
            tgt_ref, rest = rest[0], rest[1:]
            loss_ref, rest = rest[10], rest[:10] + rest[11:]
        mod_ref, g_ref, wup_ref, wdn_ref, cw_ref, cb_ref, x2_ref, y2_ref, p_ref, u_ref, ext_ref, carry_ref = rest
        i = pl.program_id(0)

        @pl.when(i == 0)
        def _():
            carry_ref[...] = jnp.zeros_like(carry_ref)
            if with_loss:
                loss_ref[...] = jnp.zeros_like(loss_ref)

        x1v = x1_ref[...]
        pre_g, post_g = g_ref[3:4, :], g_ref[4:5, :]
        sh2, sc2, g2 = mod_ref[3:4, :], mod_ref[4:5, :], mod_ref[5:6, :]
        xhat, _ = _rms_fwd(x1v)
        h2b = (xhat * pre_g * (1.0 + sc2) + sh2).astype(BF16)

        def conv_shard(s):
            p = _dot_nt(h2b, wup_ref[s])
            p_ref[s] = p.astype(BF16)
            ext_ref[0:FFN_HALO, :], carry_ref[s] = _causal_tail(p[t - FFN_HALO:t, :], carry_ref[s])
            ext_ref[FFN_HALO:FFN_HALO + t, :] = p
            w = cw_ref[s]
            u = w[0:1, :] * ext_ref[0:t, :] + w[1:2, :] * ext_ref[8:8 + t, :] + w[2:3, :] * p + cb_ref[s]
            u_ref[s] = u.astype(BF16)
            return u

        y2 = jnp.zeros((t, D_MODEL), F32)
        for j in range(FF_PAIRS):
            ug = conv_shard(j)
            uv = conv_shard(j + FF_PAIRS)
            ge, _ = _gelu(ug)
            y2 = y2 + _dot((ge * uv).astype(BF16), wdn_ref[j])
        y2_ref[...] = y2
        yhat, _ = _rms_fwd(y2)
        x2 = x1v + g2 * (yhat * post_g)
        if with_loss:
            diff = x2 - _interleave(tgt_ref[...])
            x2_ref[...] = diff * (1.0 / D_MODEL)
            sq = jnp.sum((diff * diff).reshape(t // 8, 8, D_MODEL), axis=0)
            loss_ref[...] += (0.5 / D_MODEL) * sum(sq[:, 128 * j:128 * (j + 1)] for j in range(D_MODEL // 128))
        else:
            x2_ref[...] = x2

    tile = pl.BlockSpec((t, D_MODEL), lambda i: (i, 0))
    consts = (modv, g1024, wup, wdn, cw, cb)
    shards = pl.BlockSpec((N_DEV, t, FF_SHARD), lambda i: (0, i, 0))
    out_specs = [tile, tile, shards, shards]
    out_shape = [jax.ShapeDtypeStruct((s_len, D_MODEL), F32), jax.ShapeDtypeStruct((s_len, D_MODEL), F32),
                 jax.ShapeDtypeStruct((N_DEV, s_len, FF_SHARD), BF16),
                 jax.ShapeDtypeStruct((N_DEV, s_len, FF_SHARD), BF16)]
    if with_loss:
        out_specs.append(pl.BlockSpec((8, 128), lambda i: (0, 0)))
        out_shape.append(jax.ShapeDtypeStruct((8, 128), F32))
    return _pallas_call_with_exchange(
        body,
        grid=(n_tiles,),
        in_specs=[tile] * (2 if with_loss else 1) + [_whole(c.shape) for c in consts],
        out_specs=out_specs,
        out_shape=out_shape,
        scratch_shapes=[pltpu.VMEM((FFN_HALO + t, FF_SHARD), F32), pltpu.VMEM((N_DEV, FFN_HALO, FF_SHARD), F32)],
        operands=(x1,) + ((loss_target,) if with_loss else ()) + consts,
        name=name, job=job)


def _ffn_bwd(dx2, x1, y2, p, u, modv, g1024, wup, wdn, cw, name="ffn_bwd", job=None):
    s_len = x1.shape[0]
    t = FFN_TILE
    n_tiles = s_len // t
    hb = FFN_HALO

    def body(dx2_ref, x1_ref, y2_ref, p_ref, u_ref, mod_ref, g_ref, wup_ref, wdn_ref, cw_ref,
             dx1_ref, dp_ref, a_ref, dy2_ref, h2_ref, vec_ref, cgrad_ref, dext_ref, dcarry_ref):
        i = pl.program_id(0)

        @pl.when(i == 0)
        def _():
            vec_ref[...] = jnp.zeros_like(vec_ref)
            cgrad_ref[...] = jnp.zeros_like(cgrad_ref)
            dcarry_ref[...] = jnp.zeros_like(dcarry_ref)

        dx2v, x1v, y2v = dx2_ref[...], x1_ref[...], y2_ref[...]
        pre_g, post_g = g_ref[3:4, :], g_ref[4:5, :]
        sh2, sc2, g2 = mod_ref[3:4, :], mod_ref[4:5, :], mod_ref[5:6, :]

        yhat, ry = _rms_fwd(y2v)
        vec_ref[1:2, :] += _colsum(dx2v * (yhat * post_g))
        dyn = dx2v * g2
        vec_ref[0:1, :] += _colsum(dyn * yhat)
        dy2b = _rms_bwd(dyn * post_g, yhat, ry).astype(BF16)
        dy2_ref[...] = dy2b

        xhat, rx = _rms_fwd(x1v)
        xn = xhat * pre_g
        h2_ref[...] = (xn * (1.0 + sc2) + sh2).astype(BF16)

        def conv_bwd(s, du):
            w = cw_ref[s]
            pf = p_ref[s].astype(F32)
            dext_ref[0:t, :] = du
            dext_ref[t:t + hb, :], dcarry_ref[s] = _lookahead_head(du[0:hb, :], dcarry_ref[s])
            du1, du2 = dext_ref[8:8 + t, :], dext_ref[16:16 + t, :]
            cgrad_ref[s, 0:1, :] += _colsum(du2 * pf)
            cgrad_ref[s, 1:2, :] += _colsum(du1 * pf)
            cgrad_ref[s, 2:3, :] += _colsum(du * pf)
            cgrad_ref[s, 3:4, :] += _colsum(du)
            dpb = (w[2:3, :] * du + w[1:2, :] * du1 + w[0:1, :] * du2).astype(BF16)
            dp_ref[s] = dpb
            return _dot(dpb, wup_ref[s])

        dh2 = jnp.zeros((t, D_MODEL), F32)
        for j in range(FF_PAIRS):
            ug = u_ref[j].astype(F32)
            uv = u_ref[j + FF_PAIRS].astype(F32)
            ge, th = _gelu(ug)
            a_ref[j] = (ge * uv).astype(BF16)
            da = _dot_nt(dy2b, wdn_ref[j])
            dh2 = dh2 + conv_bwd(j, da * uv * _gelu_grad(ug, th))
            dh2 = dh2 + conv_bwd(j + FF_PAIRS, da * ge)

        vec_ref[2:3, :] += _colsum(dh2)
        vec_ref[3:4, :] += _colsum(dh2 * xn)
        dxn = dh2 * (1.0 + sc2)
        vec_ref[4:5, :] += _colsum(dxn * xhat)
        dx1_ref[...] = dx2v + _rms_bwd(dxn * pre_g, xhat, rx)

    rev = lambda i: (n_tiles - 1 - i, 0)
    tile = pl.BlockSpec((t, D_MODEL), rev)
    return _pallas_call_with_exchange(
        body,
        grid=(n_tiles,),
        in_specs=[tile, tile, tile,
                  pl.BlockSpec((N_DEV, t, FF_SHARD), lambda i: (0, n_tiles - 1 - i, 0)),
                  pl.BlockSpec((N_DEV, t, FF_SHARD), lambda i: (0, n_tiles - 1 - i, 0)),
                  _whole(modv.shape), _whole(g1024.shape), _whole(wup.shape), _whole(wdn.shape),
                  _whole(cw.shape)],
        out_specs=[tile,
                   pl.BlockSpec((N_DEV, t, FF_SHARD), lambda i: (0, n_tiles - 1 - i, 0)),
                   pl.BlockSpec((FF_PAIRS, t, FF_SHARD), lambda i: (0, n_tiles - 1 - i, 0)),
                   tile, tile,
                   pl.BlockSpec((8, D_MODEL), lambda i: (0, 0)),
                   pl.BlockSpec((N_DEV, 8, FF_SHARD), lambda i: (0, 0, 0))],
        out_shape=[jax.ShapeDtypeStruct((s_len, D_MODEL), F32),
                   jax.ShapeDtypeStruct((N_DEV, s_len, FF_SHARD), BF16),
                   jax.ShapeDtypeStruct((FF_PAIRS, s_len, FF_SHARD), BF16),
                   jax.ShapeDtypeStruct((s_len, D_MODEL), BF16),
                   jax.ShapeDtypeStruct((s_len, D_MODEL), BF16),
                   jax.ShapeDtypeStruct((8, D_MODEL), F32),
                   jax.ShapeDtypeStruct((N_DEV, 8, FF_SHARD), F32)],
        scratch_shapes=[pltpu.VMEM((t + hb, FF_SHARD), F32), pltpu.VMEM((N_DEV, hb, FF_SHARD), F32)],
        operands=(dx2, x1, y2, p, u, modv, g1024, wup, wdn, cw),
        name=name, job=job)


def _wgrad(a, b, name, tk=WGRAD_TK, job=None):
    a_grouped, b_grouped = a.ndim == 3, b.ndim == 3
    groups = a.shape[0] if a_grouped else b.shape[0]
    s_len, m, n = a.shape[-2], a.shape[-1], b.shape[-1]
    tk = min(tk, s_len)
    n_k = s_len // tk

    def body(a_ref, b_ref, o_ref, acc_ref):
        k = pl.program_id(1)
        av = a_ref[0] if a_grouped else a_ref[...]
        bv = b_ref[0] if b_grouped else b_ref[...]
        part = _dot_tn(av, bv)
        if n_k == 1:
            o_ref[0] = part.astype(BF16)
            return

        @pl.when(k == 0)
        def _():
            acc_ref[...] = part

        @pl.when(jnp.logical_and(k > 0, k < n_k - 1))
        def _():
            acc_ref[...] += part

        @pl.when(k == n_k - 1)
        def _():
            o_ref[0] = (acc_ref[...] + part).astype(BF16)

    a_spec = pl.BlockSpec((1, tk, m), lambda g, k: (g, k, 0)) if a_grouped else pl.BlockSpec((tk, m), lambda g, k: (k, 0))
    b_spec = pl.BlockSpec((1, tk, n), lambda g, k: (g, k, 0)) if b_grouped else pl.BlockSpec((tk, n), lambda g, k: (k, 0))
    (out,), exchanged = _pallas_call_with_exchange(
        body,
        grid=(groups, n_k),
        in_specs=[a_spec, b_spec],
        out_specs=[pl.BlockSpec((1, m, n), lambda g, k: (g, 0, 0))],
        out_shape=[jax.ShapeDtypeStruct((groups, m, n), BF16)],
        scratch_shapes=[pltpu.VMEM((m, n), F32)],
        operands=(a, b),
        name=name, job=job)
    return (out, exchanged) if job else out


def _lane(shape):
    return lax.broadcasted_iota(jnp.int32, shape, 1)


def _by_pool_group(shape, vals):
    lane = _lane(shape)
    return jnp.where(lane < 64, vals[0], jnp.where(lane < 128, vals[1], jnp.where(lane < 192, vals[2], vals[3])))


def _pool_inv_counts(t, tile_idx):
    pos1 = _tile_token_index(t, tile_idx) + 1
    return [1.0 / jnp.minimum(pos1, w).astype(F32) for w in POOL_WINDOWS]


def _sgu_keep_mask(t):
    tok_r = _tile_token_index(t, 0)
    c = lax.broadcasted_iota(jnp.int32, (1, t), 1)
    tok_c = (c % 8) * (t // 8) + c // 8
    return jnp.logical_and(tok_r // CHUNK == tok_c // CHUNK, tok_r >= tok_c)


def _masked_sgu_w(sguw_ref):
    keep = _sgu_keep_mask(sguw_ref.shape[1])
    return [jnp.where(keep, sguw_ref[h], 0.0).astype(BF16) for h in range(SGU_HEADS)]


def _branches_fwd(z, tile_idx, p384_ref, cw_ref, wm, bmat_ref, pwbd_ref, psc_ref, segp_ref, g_ref, hext_ref, zext_ref,
                  h_prev_rot, z_prev_rot, conv_saved=None):
    t = z.shape[0]
    segp = segp_ref[...]
    r = {}
    u, _ = _gelu(z[:, 0:SGU_WIDTH])
    vraw, _ = _gelu(z[:, SGU_WIDTH:2 * SGU_WIDTH])
    xc = vraw - _seg_mean(vraw, segp)
    rstd_v = lax.rsqrt(_seg_mean(xc * xc, segp) + EPS)
    xh_v = xc * rstd_v
    vnb = (xh_v * p384_ref[0:1, :] + p384_ref[1:2, :]).astype(BF16)
    first_head = _lane((t, 128)) < HEAD_DIM
    f_pairs = []
    for pr in range(SGU_HEADS // 2):
        vp = vnb[:, pr * 128:(pr + 1) * 128]
        f_pairs.append(jnp.where(first_head, _dot(wm[2 * pr], vp), _dot(wm[2 * pr + 1], vp)))
    f = jnp.concatenate(f_pairs, axis=1) + bmat_ref[...]
    ya = u * f
    r.update(u=u, xh_v=xh_v, rstd_v=rstd_v, vnb=vnb, f=f)
    o_b = 2 * SGU_WIDTH
    a_in = z[:, o_b:o_b + CONV_WIDTH]
    sig_g = jax.nn.sigmoid(z[:, o_b + CONV_WIDTH:o_b + 2 * CONV_WIDTH])
    hh = a_in * sig_g
    if conv_saved is None:
        hext_ref[0:t, :], r["h_rot"] = _causal_tail(hh, h_prev_rot)
        hext_ref[t:2 * t, :] = hh
        conv = jnp.zeros((t, CONV_WIDTH), F32) + p384_ref[2:3, :]
        for k in range(CONV_K):
            conv = conv + cw_ref[k:k + 1, :] * hext_ref[pl.ds(t - 8 * (CONV_K - 1 - k), t), :]
        r["conv"] = conv
    else:
        conv = conv_saved
    cc = conv - _rowmean(conv)
    rstd_c = lax.rsqrt(_rowmean(cc * cc) + EPS)
    xh_c = cc * rstd_c
    cn = xh_c * p384_ref[3:4, :] + p384_ref[4:5, :]
    sig_c = jax.nn.sigmoid(cn)
    yb = cn * sig_c
    r.update(a_in=a_in, sig_g=sig_g, hh=hh, xh_c=xh_c, rstd_c=rstd_c, cn=cn, sig_c=sig_c)
    o_c = o_b + 2 * CONV_WIDTH
    zc = z[:, o_c:o_c + POOL_WIDTH]
    zext_ref[0:POOL_HALO, :], r["z_rot"] = _causal_tail(zc[t - POOL_HALO:t, :], z_prev_rot)
    zext_ref[POOL_HALO:POOL_HALO + t, :] = zc
    sums, acc = [], zc
    for j in range(1, POOL_WINDOWS[-1]):
        acc = acc + zext_ref[pl.ds(POOL_HALO - 8 * j, t), :]
        if j + 1 in POOL_WINDOWS:
            sums.append(acc)
    inv = _pool_inv_counts(t, tile_idx)
    dpool = _by_pool_group((t, POOL_WIDTH), [s * iv for s, iv in zip(sums, inv)]) - zc
    ycp = _dot(dpool.astype(BF16), pwbd_ref[...])
    yc = ycp * psc_ref[0:1, :]
    r.update(dpool=dpool, ycp=ycp)
    yha, ra = _rms_fwd(ya)
    yhb, rb = _rms_fwd(yb)
    yhc, rc = _rms_fwd(yc)
    bg = g_ref[2:3, :]
    ycat = jnp.concatenate([yha * bg[:, 0:384], yhb * bg[:, 384:768], yhc * bg[:, 768:1024]], axis=1)
    r.update(yha=yha, ra=ra, yhb=yhb, rb=rb, yhc=yhc, rc=rc, ycat=ycat)
    return r


def _mixer_fwd(x, modv, g1024, p384, cw, sguw, bmat, pwbd, psc, segp, win, wout, name="mixer_fwd", job=None,
               natural_x=False, relay_steps_before_end=2):
    s_len = x.shape[0]
    t = MIX_TILE

    def body(x_ref, mod_ref, g_ref, p384_ref, cw_ref, sguw_ref, bmat_ref, pwbd_ref, psc_ref, segp_ref, win_ref, wout_ref,
             x1_ref, z_ref, o_ref, conv_ref, hext_ref, zext_ref, hrot_ref, zrot_ref, wm_ref):
        i = pl.program_id(0)

        @pl.when(i == 0)
        def _():
            hrot_ref[...] = jnp.zeros_like(hrot_ref)
            zrot_ref[...] = jnp.zeros_like(zrot_ref)
            for h, wmh in enumerate(_masked_sgu_w(sguw_ref)):
                wm_ref[h] = wmh

        xv = _interleave(x_ref[...]) if natural_x else x_ref[...]
        sh1, sc1, g1 = mod_ref[0:1, :], mod_ref[1:2, :], mod_ref[2:3, :]
        xhat, _ = _rms_fwd(xv)
        h1 = xhat * g_ref[0:1, :] * (1.0 + sc1) + sh1
        z = _dot_nt(h1.astype(BF16), win_ref[...])
        z_ref[...] = z
        r = _branches_fwd(z, i, p384_ref, cw_ref, [wm_ref[h] for h in range(SGU_HEADS)], bmat_ref, pwbd_ref, psc_ref,
                          segp_ref, g_ref, hext_ref, zext_ref, hrot_ref[...], zrot_ref[...])
        hrot_ref[...] = r["h_rot"]
        zrot_ref[...] = r["z_rot"]
        conv_ref[...] = r["conv"]
        o = _dot(r["ycat"].astype(BF16), wout_ref[...])
        o_ref[...] = o
        ohat, _ = _rms_fwd(o)
        x1_ref[...] = xv + g1 * (ohat * g_ref[1:2, :])

    tile = pl.BlockSpec((t, D_MODEL), lambda i: (i, 0))
    consts = (modv, g1024, p384, cw, sguw, bmat, pwbd, psc, segp, win, wout)
    return _pallas_call_with_exchange(
        body,
        grid=(s_len // t,),
        in_specs=[tile] + [_whole(c.shape) for c in consts],
        out_specs=[tile, pl.BlockSpec((t, IN_WIDTH), lambda i: (i, 0)), tile,
                   pl.BlockSpec((t, CONV_WIDTH), lambda i: (i, 0))],
        out_shape=[jax.ShapeDtypeStruct((s_len, D_MODEL), F32), jax.ShapeDtypeStruct((s_len, IN_WIDTH), F32),
                   jax.ShapeDtypeStruct((s_len, D_MODEL), F32), jax.ShapeDtypeStruct((s_len, CONV_WIDTH), F32)],
        scratch_shapes=[pltpu.VMEM((2 * t, CONV_WIDTH), F32), pltpu.VMEM((POOL_HALO + t, POOL_WIDTH), F32),
                        pltpu.VMEM((t, CONV_WIDTH), F32), pltpu.VMEM((POOL_HALO, POOL_WIDTH), F32),
                        pltpu.VMEM((SGU_HEADS, t, t), BF16)],
        operands=(x, *consts),
        name=name, job=job, relay_steps_before_end=relay_steps_before_end)


def _mixer_bwd(dx1, x, o, z, conv, modv, g1024, p384, cw, sguw, bmat, pwbd, psc, segp, win, wout, name="mixer_bwd",
               job=None, natural_x=False):
    s_len = x.shape[0]
    t = MIX_TILE
    n_tiles = s_len // t

    def body(dx1_ref, x_ref, o_ref, z_ref, zh_ref, conv_ref, mod_ref, g_ref, p384_ref, cw_ref, sguw_ref, bmat_ref, pwbd_ref,
             psc_ref, segp_ref, win_ref, wout_ref,
             dx_ref, dz_ref, do_ref, ycat_ref, h1_ref, vec_ref, v384_ref, dcw_ref, dsguw_out_ref, dbmat_out_ref, dpw_ref,
             dpsc_ref, zext_ref, gext_ref, qext_ref, grot_ref, qrot_ref, wm_ref, dsguw_ref, dbmat_ref):
        i = pl.program_id(0)
        tile_idx = n_tiles - 1 - i

        @pl.when(i == 0)
        def _():
            for ref in (vec_ref, v384_ref, dcw_ref, dsguw_ref, dbmat_ref, dpw_ref, dpsc_ref, grot_ref, qrot_ref):
                ref[...] = jnp.zeros_like(ref)
            for h, wmh in enumerate(_masked_sgu_w(sguw_ref)):
                wm_ref[h] = wmh

        dx1v, ov, z = dx1_ref[...], o_ref[...], z_ref[...]
        xv = _interleave(x_ref[...]) if natural_x else x_ref[...]
        sh1, sc1, g1 = mod_ref[0:1, :], mod_ref[1:2, :], mod_ref[2:3, :]
        pre_g, post_g, bg = g_ref[0:1, :], g_ref[1:2, :], g_ref[2:3, :]
        segp = segp_ref[...]

        ohat, ro = _rms_fwd(ov)
        vec_ref[1:2, :] += _colsum(dx1v * (ohat * post_g))
        don = dx1v * g1
        vec_ref[0:1, :] += _colsum(don * ohat)
        dob = _rms_bwd(don * post_g, ohat, ro).astype(BF16)
        do_ref[...] = dob
        dycat = _dot_nt(dob, wout_ref[...])

        not_first = (tile_idx > 0).astype(F32)
        o_b = 2 * SGU_WIDTH
        o_c = o_b + 2 * CONV_WIDTH
        z_prev_rot = _rot_rows(zh_ref[:, o_c:o_c + POOL_WIDTH], 1) * not_first
        wm = [wm_ref[h] for h in range(SGU_HEADS)]
        r = _branches_fwd(z, tile_idx, p384_ref, cw_ref, wm, bmat_ref, pwbd_ref, psc_ref, segp_ref, g_ref,
                          None, zext_ref, None, z_prev_rot, conv_saved=conv_ref[...])
        ycat_ref[...] = r["ycat"].astype(BF16)

        def branch_norm_bwd(dyn, yhat, rr, gain):
            return _colsum(dyn * yhat), _rms_bwd(dyn * gain, yhat, rr)

        dga, dya = branch_norm_bwd(dycat[:, 0:384], r["yha"], r["ra"], bg[:, 0:384])
        dgb, dyb = branch_norm_bwd(dycat[:, 384:768], r["yhb"], r["rb"], bg[:, 384:768])
        dgc, dyc = branch_norm_bwd(dycat[:, 768:1024], r["yhc"], r["rc"], bg[:, 768:1024])
        vec_ref[5:6, :] += jnp.concatenate([dga, dgb, dgc], axis=1)

        du_act = dya * r["f"]
        df = dya * r["u"]
        first_head = _lane((t, 128)) < HEAD_DIM
        dbmat_ref[...] += df
        dvn_pairs = []
        for pr in range(SGU_HEADS // 2):
            dfp = df[:, pr * 128:(pr + 1) * 128]
            df0 = jnp.where(first_head, dfp, 0.0).astype(BF16)
            df1 = jnp.where(first_head, 0.0, dfp).astype(BF16)
            vp = r["vnb"][:, pr * 128:(pr + 1) * 128]
            dvn_pairs.append(_dot_tn(wm[2 * pr], df0) + _dot_tn(wm[2 * pr + 1], df1))
            dsguw_ref[2 * pr] += _dot_nt(df0, vp)
            dsguw_ref[2 * pr + 1] += _dot_nt(df1, vp)
        dvn = jnp.concatenate(dvn_pairs, axis=1)
        v384_ref[0:1, :] += _colsum(dvn * r["xh_v"])
        v384_ref[1:2, :] += _colsum(dvn)
        dxh = dvn * p384_ref[0:1, :]
        dvraw = r["rstd_v"] * (dxh - _seg_mean(dxh, segp) - r["xh_v"] * _seg_mean(dxh * r["xh_v"], segp))
        zu, zv = z[:, 0:SGU_WIDTH], z[:, SGU_WIDTH:o_b]
        _, tu = _gelu(zu)
        _, tv = _gelu(zv)
        dz_u = du_act * _gelu_grad(zu, tu)
        dz_v = dvraw * _gelu_grad(zv, tv)

        cn, sig_c = r["cn"], r["sig_c"]
        dcn = dyb * (sig_c * (1.0 + cn * (1.0 - sig_c)))
        v384_ref[3:4, :] += _colsum(dcn * r["xh_c"])
        v384_ref[4:5, :] += _colsum(dcn)
        dxc = dcn * p384_ref[3:4, :]
        gconv = r["rstd_c"] * (dxc - _rowmean(dxc) - r["xh_c"] * _rowmean(dxc * r["xh_c"]))
        v384_ref[2:3, :] += _colsum(gconv)
        gext_ref[0:t, :] = gconv
        gext_ref[t:2 * t, :], grot_ref[...] = _lookahead_head(gconv, grot_ref[...])
        dhh = jnp.zeros((t, CONV_WIDTH), F32)
        hh = r["hh"]
        for k in range(CONV_K):
            shift = CONV_K - 1 - k
            g_ahead = gext_ref[pl.ds(8 * shift, t), :]
            dcw_ref[k:k + 1, :] += _colsum(g_ahead * hh)
            dhh = dhh + cw_ref[k:k + 1, :] * g_ahead
        sig_g = r["sig_g"]
        dz_a = dhh * sig_g
        dz_g = dhh * r["a_in"] * sig_g * (1.0 - sig_g)

        dpsc_ref[0:1, :] += _colsum(dyc * r["ycp"])
        dycp = (dyc * psc_ref[0:1, :]).astype(BF16)
        dpw_ref[...] += _dot_tn(r["dpool"].astype(BF16), dycp)
        ddp = _dot_nt(dycp, pwbd_ref[...])
        inv = _pool_inv_counts(t, tile_idx)
        q = ddp * _by_pool_group((t, POOL_WIDTH), inv)
        qext_ref[0:t, :] = q
        qext_ref[t:t + POOL_HALO, :], qrot_ref[...] = _lookahead_head(q[0:POOL_HALO, :], qrot_ref[...])
        sums, acc = [], q
        for j in range(1, POOL_WINDOWS[-1]):
            acc = acc + qext_ref[pl.ds(8 * j, t), :]
            if j + 1 in POOL_WINDOWS:
                sums.append(acc)
        dz_c = _by_pool_group((t, POOL_WIDTH), sums) - ddp

        dzb = jnp.concatenate([dz_u, dz_v, dz_a, dz_g, dz_c], axis=1).astype(BF16)
        dz_ref[...] = dzb
        dh1 = _dot(dzb, win_ref[...])

        xhat, rx = _rms_fwd(xv)
        xn = xhat * pre_g
        h1_ref[...] = (xn * (1.0 + sc1) + sh1).astype(BF16)
        vec_ref[2:3, :] += _colsum(dh1)
        vec_ref[3:4, :] += _colsum(dh1 * xn)
        dxn = dh1 * (1.0 + sc1)
        vec_ref[4:5, :] += _colsum(dxn * xhat)
        dx = dx1v + _rms_bwd(dxn * pre_g, xhat, rx)
        dx_ref[...] = _deinterleave(dx) if natural_x else dx

        @pl.when(i == n_tiles - 1)
        def _():
            keep = _sgu_keep_mask(t)
            for h in range(SGU_HEADS):
                rows_natural = _deinterleave(jnp.where(keep, dsguw_ref[h], 0.0))
                natural = _deinterleave(rows_natural.T).T
                dsguw_out_ref[h] = sum(natural[b * CHUNK:(b + 1) * CHUNK, b * CHUNK:(b + 1) * CHUNK]
                                       for b in range(t // CHUNK))
            dbmat = _deinterleave(float(HEAD_DIM) * _seg_mean(dbmat_ref[...], segp))
            dbmat_out_ref[...] = sum(dbmat[b * CHUNK:(b + 1) * CHUNK, :] for b in range(t // CHUNK))

    rev = lambda i: (n_tiles - 1 - i, 0)
    tile = pl.BlockSpec((t, D_MODEL), rev)
    ztile = pl.BlockSpec((t, IN_WIDTH), rev)
    zhalo = pl.BlockSpec((POOL_HALO, IN_WIDTH),
                         lambda i: (jnp.maximum((n_tiles - 1 - i) * (t // POOL_HALO) - 1, 0), 0))
    consts = (modv, g1024, p384, cw, sguw, bmat, pwbd, psc, segp, win, wout)
    acc = lambda shape: pl.BlockSpec(shape, lambda i: (0,) * len(shape))
    acc_shapes = [(8, D_MODEL), (8, SGU_WIDTH), (32, CONV_WIDTH), (SGU_HEADS, CHUNK, CHUNK), (CHUNK, SGU_WIDTH),
                  (POOL_WIDTH, POOL_WIDTH), (8, POOL_WIDTH)]
    return _pallas_call_with_exchange(
        body,
        grid=(n_tiles,),
        in_specs=[tile, tile, tile, ztile, zhalo, pl.BlockSpec((t, CONV_WIDTH), rev)] + [_whole(c.shape) for c in consts],
        out_specs=[tile, ztile, tile, tile, tile] + [acc(s) for s in acc_shapes],
        out_shape=[jax.ShapeDtypeStruct((s_len, D_MODEL), F32), jax.ShapeDtypeStruct((s_len, IN_WIDTH), BF16),
                   jax.ShapeDtypeStruct((s_len, D_MODEL), BF16), jax.ShapeDtypeStruct((s_len, D_MODEL), BF16),
                   jax.ShapeDtypeStruct((s_len, D_MODEL), BF16)] + [jax.ShapeDtypeStruct(s, F32) for s in acc_shapes],
        scratch_shapes=[pltpu.VMEM((POOL_HALO + t, POOL_WIDTH), F32),
                        pltpu.VMEM((2 * t, CONV_WIDTH), F32), pltpu.VMEM((t + POOL_HALO, POOL_WIDTH), F32),
                        pltpu.VMEM((t, CONV_WIDTH), F32), pltpu.VMEM((POOL_HALO, POOL_WIDTH), F32),
                        pltpu.VMEM((SGU_HEADS, t, t), BF16), pltpu.VMEM((SGU_HEADS, t, t), F32),
                        pltpu.VMEM((t, SGU_WIDTH), F32)],
        operands=(dx1, x, o, z, z, conv, *consts),
        name=name, job=job)


MOD_SHARD = 6 * D_MODEL // N_DEV

ROW_DMOD = 0
ROW_G1024 = 8
ROW_V384 = 16
ROW_SGU_B = 24
ROW_POOL_SCALE = 25
ROW_LOSS = 26
ROW_CONV_W = 32
ROW_FFN_CONV = 64
ROW_POOL_W = 96
ROW_SGU_W = 112
ROWS_PER_LAYER = 208
N_LAYERS = 2


def _gather_weights(c8, mod_w, mod_b8, job):
    kinds = [kind for kind, _ in job]
    shards = [a for _, a in job]
    n = len(shards)

    def body(c_ref, modw_ref, modb_ref, *rest):
        shard_refs = rest[:n]
        sc_all_ref, modrows_ref = rest[n], rest[n + 1]
        full_refs = rest[n + 2:2 * n + 2]
        send_buf, mod_recv, w_send, w_recv, w_local, sc_send, sc_recv, mod_send, mod_recv_sem = rest[2 * n + 2:]
        pos = _my_pos()
        me = _flat(pos)
        peers = [_peer(pos, k) for k in range(1, N_DEV)]

        w_copies = _exchange_copies(kinds, shard_refs, full_refs, w_send, w_recv, w_local, phase=0)
        for cp in w_copies:
            cp.start()

        cv = c_ref[...]
        sc_all_ref[me] = cv * jax.nn.sigmoid(cv)
        sc_copies = [_remote_copy(sc_all_ref.at[me], sc_all_ref.at[me], sc_send.at[k], sc_recv.at[k], peers[k])
                     for k in range(N_PEERS)]
        for cp in sc_copies:
            cp.start()
        for cp in sc_copies:
            cp.wait()

        sc = jnp.concatenate([sc_all_ref[j, 0:1, :] for j in range(N_DEV)], axis=0)
        send_buf[...] = jnp.zeros_like(send_buf)
        for l in range(N_LAYERS):
            part = jnp.dot(sc, modw_ref[l], precision=lax.Precision.HIGHEST, preferred_element_type=F32)
            for j in range(N_DEV):
                send_buf[j, l:l + 1, :] = part[j:j + 1, :]
        mod_recv[me] = send_buf[me]
        mod_copies = [_remote_copy(send_buf.at[_flat(peers[k])], mod_recv.at[me], mod_send.at[k], mod_recv_sem.at[k],
                                   peers[k]) for k in range(N_PEERS)]
        for cp in mod_copies:
            cp.start()
        for cp in mod_copies:
            cp.wait()
        modrows_ref[...] = jnp.zeros_like(modrows_ref)
        for l in range(N_LAYERS):
            row = jnp.concatenate([mod_recv[j, l:l + 1, :] for j in range(N_DEV)], axis=1)
            modrows_ref[l:l + 1, :] = row + modb_ref[l:l + 1, :]

        for cp in w_copies:
            cp.wait()
        relays = _exchange_copies(kinds, shard_refs, full_refs, w_send, w_recv, w_local, phase=1)
        for cp in relays:
            cp.start()
        for cp in relays:
            cp.wait()

    out_shape = ([jax.ShapeDtypeStruct((N_DEV, 8, D_MODEL), F32), jax.ShapeDtypeStruct((8, 6 * D_MODEL), F32)]
                 + [jax.ShapeDtypeStruct((N_DEV,) + s.shape, s.dtype) for s in shards])
    return pl.pallas_call(
        body,
        in_specs=[VMEM, VMEM, VMEM] + [ANY] * n,
        out_specs=[VMEM, VMEM] + [ANY] * n,
        out_shape=out_shape,
        scratch_shapes=[pltpu.VMEM((N_DEV, 8, MOD_SHARD), F32), pltpu.VMEM((N_DEV, 8, MOD_SHARD), F32),
                        pltpu.SemaphoreType.DMA((n, N_PEERS)), pltpu.SemaphoreType.DMA((n, N_PEERS)),
                        pltpu.SemaphoreType.DMA((n,)),
                        pltpu.SemaphoreType.DMA((N_PEERS,)), pltpu.SemaphoreType.DMA((N_PEERS,)),
                        pltpu.SemaphoreType.DMA((N_PEERS,)), pltpu.SemaphoreType.DMA((N_PEERS,))],
        compiler_params=pltpu.CompilerParams(vmem_limit_bytes=VMEM_LIMIT_BYTES),
        name="gather_weights",
    )(c8, mod_w, mod_b8, *shards)


def _small_sums(sc_all, small_all0, small_all1, job):
    kinds = [kind for kind, _ in job]
    n = len(job)

    def body(sc_all_ref, small_all0_ref, small_all1_ref, *rest):
        small_sum_ref, gmodw_ref = rest[n], rest[n + 1]
        copies = _exchange_copies(kinds, rest[:n], rest[n + 2:2 * n + 2], *rest[2 * n + 2:], phase=0)
        for cp in copies:
            cp.start()
        me = _flat(_my_pos())
        sc = jnp.concatenate([sc_all_ref[j, 0:1, :] for j in range(N_DEV)], axis=0)
        mine = lax.broadcasted_iota(jnp.int32, (2 * N_DEV, MOD_SHARD), 0) == me
        for l, parts in enumerate((small_all0_ref, small_all1_ref)):
            total = parts[0].astype(F32)
            for j in range(1, N_DEV):
                total = total + parts[j].astype(F32)
            small_sum_ref[l] = total
            dm = jnp.concatenate(
                [jnp.sum(jnp.where(mine, parts[j, ROW_DMOD:ROW_DMOD + 2 * N_DEV, 0:MOD_SHARD].astype(F32), 0.0),
                         axis=0, keepdims=True) for j in range(N_DEV)], axis=0)
            gmodw_ref[l] = lax.dot_general(sc, dm, (((0,), (0,)), ((), ())), precision=lax.Precision.HIGHEST,
                                           preferred_element_type=F32)
        for cp in copies:
            cp.wait()

    res = pl.pallas_call(
        body,
        in_specs=[VMEM, VMEM, VMEM] + [ANY] * n,
        out_specs=[VMEM, VMEM] + [ANY] * n,
        out_shape=[jax.ShapeDtypeStruct((N_LAYERS, ROWS_PER_LAYER, D_MODEL), F32),
                   jax.ShapeDtypeStruct((N_LAYERS, D_MODEL, MOD_SHARD), F32)] + _exchange_out_shapes(job),
        scratch_shapes=_exchange_sems(n),
        compiler_params=pltpu.CompilerParams(vmem_limit_bytes=VMEM_LIMIT_BYTES),
        name="small_sums",
    )(sc_all, small_all0, small_all1, *[a for _, a in job])
    return res[0], res[1], res[2:]


def _adam_update(g, w, m, v):
    m2 = ADAM_B1 * m + (1.0 - ADAM_B1) * g
    v2 = ADAM_B2 * v + (1.0 - ADAM_B2) * (g * g)
    m_hat = m2 * (1.0 / (1.0 - ADAM_B1 ** ADAM_STEP))
    v_hat = v2 * (1.0 / (1.0 - ADAM_B2 ** ADAM_STEP))
    delta = -ADAM_LR * (m_hat / (jnp.sqrt(v_hat) + ADAM_EPS) + ADAM_WD * w)
    return delta, m2, v2


def _pair_add(g, r1, row_chunk, name):
    _, rows, cols = g.shape
    core = lax.axis_index("c").astype(jnp.int32).reshape(1)

    def body(core_ref, g_ref, r_ref, o_ref):
        o_ref[0] = (g_ref[0, 0].astype(F32) + r_ref[0].astype(F32)).astype(BF16)

    blk = pl.BlockSpec((1, row_chunk, cols), lambda q, i, core_ref: (q, i, 0))
    grid_spec = pltpu.PrefetchScalarGridSpec(
        num_scalar_prefetch=1, grid=(N_CHIPS, rows // row_chunk),
        in_specs=[pl.BlockSpec((1, 1, row_chunk, cols), lambda q, i, core_ref: (q, core_ref[0], i, 0)), blk],
        out_specs=blk)
    return pl.pallas_call(
        body, grid_spec=grid_spec, out_shape=jax.ShapeDtypeStruct((N_CHIPS, rows, cols), BF16),
        compiler_params=_cparams(2), name=name,
    )(core, g.reshape(N_CHIPS, 2, rows, cols), r1)


def _adam_sharded(recv0, recv1, w, m, v, row_chunk, name):
    _, rows, cols = w.shape
    n_chunks = rows // row_chunk

    def body(r0_ref, r1_ref, w_ref, m_ref, v_ref, g_ref, d_ref, m2_ref, v2_ref):
        layer = pl.program_id(0)

        def run(r_ref):
            g = r_ref[0].astype(F32)
            for j in range(1, r_ref.shape[0]):
                g = g + r_ref[j].astype(F32)
            delta, m2, v2 = _adam_update(g, w_ref[0], m_ref[0], v_ref[0])
            g_ref[0], d_ref[0], m2_ref[0], v2_ref[0] = g, delta, m2, v2

        @pl.when(layer == 0)
        def _():
            run(r0_ref)

        @pl.when(layer == 1)
        def _():
            run(r1_ref)

    r0_spec = pl.BlockSpec((recv0.shape[0], row_chunk, cols), lambda l, i: (0, i * (1 - l) + (n_chunks - 1) * l, 0))
    r1_spec = pl.BlockSpec((recv1.shape[0], row_chunk, cols), lambda l, i: (0, i * l, 0))
    blk = pl.BlockSpec((1, row_chunk, cols), lambda l, i: (l, i, 0))
    out = jax.ShapeDtypeStruct(w.shape, F32)
    return pl.pallas_call(
        body,
        grid=(N_LAYERS, n_chunks),
        in_specs=[r0_spec, r1_spec, blk, blk, blk],
        out_specs=[blk] * 4,
        out_shape=[out] * 4,
        compiler_params=_cparams(2),
        name=name,
    )(recv0, recv1, w, m, v)


def _adam_dense(g, w, m, v, row_chunk, name):
    n_lead, rows, cols = w.shape

    def body(g_ref, w_ref, m_ref, v_ref, go_ref, d_ref, m2_ref, v2_ref):
        gv = g_ref[...]
        go_ref[...] = gv
        d_ref[...], m2_ref[...], v2_ref[...] = _adam_update(gv, w_ref[...], m_ref[...], v_ref[...])

    blk = pl.BlockSpec((1, row_chunk, cols), lambda l, i: (l, i, 0))
    out = jax.ShapeDtypeStruct(w.shape, F32)
    return pl.pallas_call(
        body,
        grid=(n_lead, rows // row_chunk),
        in_specs=[blk] * 4,
        out_specs=[blk] * 4,
        out_shape=[out] * 4,
        compiler_params=_cparams(2),
        name=name,
    )(g, w, m, v)


WEIGHT_NAMES = ("mod_w", "mod_b", "mix_pre_g", "mix_post_g", "w_in", "sgu_norm_g", "sgu_norm_b", "sgu_w", "sgu_b",
                "conv_w", "conv_b", "conv_norm_g", "conv_norm_b", "pool_w", "pool_scale", "branch_g", "w_out",
                "ffn_pre_g", "ffn_post_g", "ffn_up", "ffn_conv_w", "ffn_conv_b", "ffn_down")
SHARDED_BIG = ("w_in", "w_out", "ffn_up", "ffn_down")
SMALL_PACKED = tuple(n for n in WEIGHT_NAMES if n not in SHARDED_BIG + ("mod_w",))


def _rows8(rows, width=D_MODEL):
    out = [jnp.pad(r.astype(F32), (0, width - r.shape[0]))[None] for r in rows]
    out.append(jnp.zeros((8 - len(rows), width), F32))
    return jnp.concatenate(out, axis=0)


def _as_rows(a, width=D_MODEL):
    flat = a.astype(F32).reshape(-1)
    pad = (-flat.shape[0]) % width
    return jnp.pad(flat, (0, pad)).reshape(-1, width)


def _pad_cols(a, width=D_MODEL):
    return jnp.pad(a.astype(F32), ((0, 0), (0, width - a.shape[1])))


def _pack_rows(arrays):
    rows = jnp.concatenate([_as_rows(a) for a in arrays], axis=0)
    return jnp.pad(rows, ((0, (-rows.shape[0]) % 8), (0, 0)))


def _unpack_rows(packed, shapes):
    out, r = [], 0
    for shape in shapes:
        size = math.prod(shape)
        n_rows = -(-size // D_MODEL)
        out.append(packed[r:r + n_rows].reshape(-1)[:size].reshape(shape))
        r += n_rows
    return out


TILE_VREGS = MIX_TILE // 8
CHUNK_SUBLANES = CHUNK // TILE_VREGS
CHUNKS_PER_TILE = MIX_TILE // CHUNK


def _chunk_axis_to_tile(a, axis):
    shape = a.shape
    a = a.reshape(shape[:axis] + (CHUNK_SUBLANES, TILE_VREGS) + shape[axis + 1:])
    a = jnp.swapaxes(a, axis, axis + 1)
    a = jnp.tile(a, (1,) * (axis + 1) + (CHUNKS_PER_TILE,) + (1,) * (len(shape) - axis - 1))
    return a.reshape(shape[:axis] + (MIX_TILE,) + shape[axis + 1:])


def _layer_consts(l, w, mod_rows, win, wout, conv_w_full):
    modv = _rows8(list(mod_rows[l].reshape(6, D_MODEL)))
    g1024 = _rows8([w["mix_pre_g"][l], w["mix_post_g"][l], w["branch_g"][l], w["ffn_pre_g"][l], w["ffn_post_g"][l]])
    p384 = _rows8([w["sgu_norm_g"][l], w["sgu_norm_b"][l], w["conv_b"][l], w["conv_norm_g"][l], w["conv_norm_b"][l]],
                  SGU_WIDTH)
    cw = jnp.pad(conv_w_full[l], ((0, 32 - CONV_K), (0, 0)))
    sguw = _chunk_axis_to_tile(_chunk_axis_to_tile(w["sgu_w"][l], 2), 1)
    bmat = _chunk_axis_to_tile(jnp.repeat(w["sgu_b"][l].T, HEAD_DIM, axis=1), 0)
    groups = len(POOL_WINDOWS)
    eye = jnp.eye(groups, dtype=F32)
    pwbd = (eye[:, None, :, None] * w["pool_w"][l][:, :, None, :]).reshape(POOL_WIDTH, POOL_WIDTH).astype(BF16)
    psc = _rows8([w["pool_scale"][l]], POOL_WIDTH)
    seg = jnp.arange(SGU_WIDTH) // HEAD_DIM
    segp = jnp.where(seg[:, None] == seg[None, :], 1.0 / HEAD_DIM, 0.0).astype(BF16)
    return modv, g1024, (modv, g1024, p384, cw, sguw, bmat, pwbd, psc, segp, win, wout)


def _small_grad_rows(mix, ffn, loss_pieces=None):
    _, _, _, _, _, mvec, v384, dcw, dsguw, dbmat, dpw, dpsc = mix
    fvec, cgrad = ffn[5], ffn[6]
    dmod = jnp.stack([mvec[2], mvec[3], mvec[1], fvec[2], fvec[3], fvec[1]]).reshape(N_DEV, MOD_SHARD)
    g_rows = jnp.stack([mvec[4], mvec[0], mvec[5], fvec[4], fvec[0]])
    dsgu_b = dbmat[:, ::HEAD_DIM].T.reshape(1, SGU_HEADS * CHUNK)
    groups = len(POOL_WINDOWS)
    gdim = POOL_WIDTH // groups
    dpw4 = dpw.reshape(groups, gdim, groups, gdim)
    dpool = jnp.stack([dpw4[g, :, g, :] for g in range(groups)])
    misc = [dsgu_b[0], dpsc[0]] + ([] if loss_pieces is None else [loss_pieces])
    blocks = [_pad_cols(dmod), _rows8(list(g_rows)), _pad_cols(v384), _rows8(misc), _pad_cols(dcw),
              _pad_cols(cgrad[:, 0:4, :].reshape(4 * N_DEV, FF_SHARD)), _as_rows(dpool), _as_rows(dsguw)]
    return jnp.concatenate(blocks, axis=0)


def _small_grads_from_rows(total):
    per = {n: [] for n in SMALL_PACKED}
    for l in range(N_LAYERS):
        s = total[l]
        per["mod_b"].append(s[ROW_DMOD:ROW_DMOD + N_DEV, :MOD_SHARD].reshape(6 * D_MODEL))
        for j, name in enumerate(("mix_pre_g", "mix_post_g", "branch_g", "ffn_pre_g", "ffn_post_g")):
            per[name].append(s[ROW_G1024 + j])
        for j, name in enumerate(("sgu_norm_g", "sgu_norm_b", "conv_b", "conv_norm_g", "conv_norm_b")):
            per[name].append(s[ROW_V384 + j, :SGU_WIDTH])
        per["sgu_b"].append(s[ROW_SGU_B, :SGU_HEADS * CHUNK].reshape(SGU_HEADS, CHUNK))
        per["pool_scale"].append(s[ROW_POOL_SCALE, :POOL_WIDTH])
        per["conv_w"].append(s[ROW_CONV_W:ROW_CONV_W + CONV_K, :CONV_WIDTH])
        fc = s[ROW_FFN_CONV:ROW_FFN_CONV + 4 * N_DEV, :FF_SHARD].reshape(N_DEV, 4, FF_SHARD)
        per["ffn_conv_w"].append(fc[:, 0:3, :].transpose(1, 0, 2).reshape(FFN_CONV_K, 2 * D_FF))
        per["ffn_conv_b"].append(fc[:, 3, :].reshape(2 * D_FF))
        per["pool_w"].append(s[ROW_POOL_W:ROW_POOL_W + 16].reshape(len(POOL_WINDOWS), HEAD_DIM, HEAD_DIM))
        per["sgu_w"].append(s[ROW_SGU_W:ROW_SGU_W + 96].reshape(SGU_HEADS, CHUNK, CHUNK))
    return {n: jnp.stack(v) for n, v in per.items()}


def kernel(x, c, mod_w, mod_b, mix_pre_g, mix_post_g, w_in, sgu_norm_g, sgu_norm_b, sgu_w, sgu_b, conv_w, conv_b, conv_norm_g, conv_norm_b, pool_w, pool_scale, branch_g, w_out, ffn_pre_g, ffn_post_g, ffn_up, ffn_conv_w, ffn_conv_b, ffn_down, loss_target, m_mod_w, m_mod_b, m_mix_pre_g, m_mix_post_g, m_w_in, m_sgu_norm_g, m_sgu_norm_b, m_sgu_w, m_sgu_b, m_conv_w, m_conv_b, m_conv_norm_g, m_conv_norm_b, m_pool_w, m_pool_scale, m_branch_g, m_w_out, m_ffn_pre_g, m_ffn_post_g, m_ffn_up, m_ffn_conv_w, m_ffn_conv_b, m_ffn_down, v_mod_w, v_mod_b, v_mix_pre_g, v_mix_post_g, v_w_in, v_sgu_norm_g, v_sgu_norm_b, v_sgu_w, v_sgu_b, v_conv_w, v_conv_b, v_conv_norm_g, v_conv_norm_b, v_pool_w, v_pool_scale, v_branch_g, v_w_out, v_ffn_pre_g, v_ffn_post_g, v_ffn_up, v_ffn_conv_w, v_ffn_conv_b, v_ffn_down):
    w = dict(zip(WEIGHT_NAMES, (mod_w, mod_b, mix_pre_g, mix_post_g, w_in, sgu_norm_g, sgu_norm_b, sgu_w, sgu_b, conv_w,
                                conv_b, conv_norm_g, conv_norm_b, pool_w, pool_scale, branch_g, w_out, ffn_pre_g,
                                ffn_post_g, ffn_up, ffn_conv_w, ffn_conv_b, ffn_down)))
    m = dict(zip(WEIGHT_NAMES, (m_mod_w, m_mod_b, m_mix_pre_g, m_mix_post_g, m_w_in, m_sgu_norm_g, m_sgu_norm_b, m_sgu_w,
                                m_sgu_b, m_conv_w, m_conv_b, m_conv_norm_g, m_conv_norm_b, m_pool_w, m_pool_scale,
                                m_branch_g, m_w_out, m_ffn_pre_g, m_ffn_post_g, m_ffn_up, m_ffn_conv_w, m_ffn_conv_b,
                                m_ffn_down)))
    v = dict(zip(WEIGHT_NAMES, (v_mod_w, v_mod_b, v_mix_pre_g, v_mix_post_g, v_w_in, v_sgu_norm_g, v_sgu_norm_b, v_sgu_w,
                                v_sgu_b, v_conv_w, v_conv_b, v_conv_norm_g, v_conv_norm_b, v_pool_w, v_pool_scale,
                                v_branch_g, v_w_out, v_ffn_pre_g, v_ffn_post_g, v_ffn_up, v_ffn_conv_w, v_ffn_conv_b,
                                v_ffn_down)))
    me = _flat(_my_pos())
    xs = x[0]

    transposed = ("w_in", "ffn_up")
    wt = {n: jnp.swapaxes(w[n], 1, 2) if n in transposed else w[n] for n in SHARDED_BIG}
    mt = {n: jnp.swapaxes(m[n], 1, 2) if n in transposed else m[n] for n in SHARDED_BIG}
    vt = {n: jnp.swapaxes(v[n], 1, 2) if n in transposed else v[n] for n in SHARDED_BIG}
    bf16_shards = [[wt[n][l].astype(BF16) for n in SHARDED_BIG] for l in range(N_LAYERS)]

    def mixer_operands(l, win_g, wout_g):
        win = win_g.reshape(IN_WIDTH, D_MODEL)
        return _layer_consts(l, w, mod_rows, win, wout_g.reshape(D_MODEL, D_MODEL), conv_w_full)

    def ffn_operands(l, modv, g1024, wup_g, wdn_g):
        wdn = wdn_g.reshape(FF_PAIRS, FF_SHARD, D_MODEL)
        return modv, g1024, wup_g, wdn, ffn_cw_full[:, l], ffn_conv_b[l].reshape(N_DEV, 1, FF_SHARD)

    w0, w1 = bf16_shards
    c8 = jnp.broadcast_to(c, (8, D_MODEL))
    mod_b8 = jnp.pad(mod_b, ((0, 8 - N_LAYERS), (0, 0)))
    sc_all, mod_rows, win0_g, wout0_g, conv_w_g, ffn_cw_full = _gather_weights(
        c8, mod_w, mod_b8, [("gather2", w0[0]), ("gather2", w0[1]), ("gather", conv_w), ("gather", ffn_conv_w)])
    conv_w_full = conv_w_g.transpose(1, 2, 0, 3).reshape(N_LAYERS, CONV_K, CONV_WIDTH)

    modv0, g0, mix_consts0 = mixer_operands(0, win0_g, wout0_g)
    (x1, z, o, cv), (wup0_g, wdn0_g) = _mixer_fwd(xs, *mix_consts0, name="mixer_fwd_l0", natural_x=True,
                                                  job=[("gather2", w0[2]), ("gather2", w0[3])],
                                                  relay_steps_before_end=1)
    ffn_consts0 = ffn_operands(0, modv0, g0, wup0_g, wdn0_g)
    (x2, y2, p, u), (win1_g, wout1_g, wdn1_g) = _ffn_fwd(
        x1, *ffn_consts0, name="ffn_fwd_l0", job=[("gather2", w1[0]), ("gather2", w1[1]), ("gather2", w1[3])])
    saved = [(xs, z, o, cv, x1, y2, p, u)]
    modv1, g1, mix_consts1 = mixer_operands(1, win1_g, wout1_g)
    (x1, z, o, cv), (wup1_g,) = _mixer_fwd(x2, *mix_consts1, name="mixer_fwd_l1", job=[("gather2", w1[2])])
    ffn_consts1 = ffn_operands(1, modv1, g1, wup1_g, wdn1_g)
    (dh, y2, p, u, loss_tile), _ = _ffn_fwd(x1, *ffn_consts1, name="ffn_fwd_l1", loss_target=loss_target[0])
    saved.append((x2, z, o, cv, x1, y2, p, u))
    loss_local = jnp.sum(loss_tile)
    loss_hi = loss_local.astype(BF16).astype(F32)
    loss_mid = (loss_local - loss_hi).astype(BF16).astype(F32)
    loss_pieces = jnp.stack([loss_hi, loss_mid, loss_local - loss_hi - loss_mid])

    def ffn_weight_grads(l, ffn):
        dp, a, dy2, h2 = ffn[1:5]
        d_up = _wgrad(dp, h2, f"wgrad_ffn_up_l{l}", tk=WGRAD_TK_FFN)
        d_dn = _wgrad(a, dy2, f"wgrad_ffn_down_l{l}", tk=WGRAD_TK_FFN).reshape(N_DEV, D_FF // N_DEV, D_MODEL)
        return d_up, d_dn

    def mixer_weight_grads(l, mix):
        dz, do, ycat, h1 = mix[1:5]
        d_in = _wgrad(dz[None], h1, f"wgrad_w_in_l{l}").reshape(N_DEV, IN_WIDTH // N_DEV, D_MODEL)
        d_out = _wgrad(ycat, do[None], f"wgrad_w_out_l{l}").reshape(N_DEV, D_MODEL // N_DEV, D_MODEL)
        return d_in, d_out

    x_in, z, o, cv, x1, y2, p, u = saved[1]
    ffn1, _ = _ffn_bwd(dh, x1, y2, p, u, *ffn_consts1[:-1], name="ffn_bwd_l1")
    d_up1, d_dn1 = ffn_weight_grads(1, ffn1)
    mix1, (sib_up1, sib_dn1) = _mixer_bwd(ffn1[0], x_in, o, z, cv, *mix_consts1, name="mixer_bwd_l1",
                                          job=[("scatter_p1", d_up1), ("scatter_p1", d_dn1)])
    chip_up1 = _pair_add(d_up1, sib_up1, 176, "pair_add_ffn_up_l1")
    chip_dn1 = _pair_add(d_dn1, sib_dn1, 176, "pair_add_ffn_down_l1")
    d_in1, d_out1 = mixer_weight_grads(1, mix1)
    small1 = _small_grad_rows(mix1, ffn1).astype(BF16)

    x_in, z, o, cv, x1, y2, p, u = saved[0]
    ffn0, job_out = _ffn_bwd(mix1[0], x1, y2, p, u, *ffn_consts0[:-1], name="ffn_bwd_l0",
                             job=[("scatter", d_in1), ("scatter", d_out1), ("scatter_p2", chip_up1),
                                  ("scatter_p2", chip_dn1)])
    recv1 = job_out
    dp, a, dy2, h2 = ffn0[1:5]
    d_dn0, (small_all1,) = _wgrad(a, dy2, "wgrad_ffn_down_l0", tk=WGRAD_TK_FFN, job=[("gather2", small1)])
    d_dn0 = d_dn0.reshape(N_DEV, D_FF // N_DEV, D_MODEL)
    d_up0, (recv_dn0,) = _wgrad(dp, h2, "wgrad_ffn_up_l0", tk=WGRAD_TK_FFN, job=[("scatter", d_dn0)])
    mix0, (recv_up0,) = _mixer_bwd(ffn0[0], x_in, o, z, cv, *mix_consts0, name="mixer_bwd_l0", natural_x=True,
                                   job=[("scatter", d_up0)])
    recv_ffn0 = (recv_up0, recv_dn0)
    grad_x = mix0[0][None]
    dz, do, ycat, h1 = mix0[1:5]
    small0 = _small_grad_rows(mix0, ffn0, loss_pieces).astype(BF16)
    d_in0, (small_all0,) = _wgrad(dz[None], h1, "wgrad_w_in_l0", tk=WGRAD_TK // 2, job=[("gather2", small0)])
    d_in0 = d_in0.reshape(N_DEV, IN_WIDTH // N_DEV, D_MODEL)
    d_out0, (recv_in0,) = _wgrad(ycat, do[None], "wgrad_w_out_l0", tk=WGRAD_TK // 2, job=[("scatter", d_in0)])
    d_out0 = d_out0.reshape(N_DEV, D_MODEL // N_DEV, D_MODEL)
    small_total, g_mod_w, (recv_out0,) = _small_sums(sc_all, small_all0, small_all1, [("scatter", d_out0)])
    recv0 = [recv_in0, recv_out0, recv_ffn0[0], recv_ffn0[1]]
    loss = small_total[0, ROW_LOSS, 0] + small_total[0, ROW_LOSS, 1] + small_total[0, ROW_LOSS, 2]

    grads, deltas, new_m, new_v = {}, {}, {}, {}
    for j, (name, chunk) in enumerate((("w_in", 224), ("w_out", 128), ("ffn_up", 176), ("ffn_down", 176))):
        outs = _adam_sharded(recv0[j], recv1[j], wt[name], mt[name], vt[name], chunk, "adam_" + name)
        if name in transposed:
            outs = [jnp.swapaxes(t, 1, 2) for t in outs]
        grads[name], deltas[name], new_m[name], new_v[name] = outs
    grads["mod_w"], deltas["mod_w"], new_m["mod_w"], new_v["mod_w"] = _adam_dense(
        g_mod_w, mod_w, m_mod_w, v_mod_w, 256, "adam_mod_w")

    small_g = _small_grads_from_rows(small_total)
    small_g["conv_w"] = lax.dynamic_slice_in_dim(small_g["conv_w"], me * conv_w.shape[2], conv_w.shape[2], axis=2)
    small_g["ffn_conv_w"] = lax.dynamic_slice_in_dim(small_g["ffn_conv_w"], me * FF_SHARD, FF_SHARD, axis=2)
    shapes = [w[n].shape for n in SMALL_PACKED]
    packs = [_pack_rows([src[n] for n in SMALL_PACKED])[None] for src in (small_g, w, m, v)]
    _, d, m2, v2 = _adam_dense(*packs, packs[0].shape[1], "adam_small")
    for name, dd, mm, vv in zip(SMALL_PACKED, _unpack_rows(d[0], shapes), _unpack_rows(m2[0], shapes),
                                _unpack_rows(v2[0], shapes)):
        grads[name], deltas[name], new_m[name], new_v[name] = small_g[name], dd, mm, vv

    return (loss, grad_x, *[grads[n] for n in WEIGHT_NAMES], *[deltas[n] for n in WEIGHT_NAMES],
            *[new_m[n] for n in WEIGHT_NAMES], *[new_v[n] for n in WEIGHT_NAMES])
```

```python
import math

import jax
import jax.numpy as jnp
from jax import lax
from jax.experimental import pallas as pl
from jax.experimental.pallas import tpu as pltpu

F32 = jnp.float32
BF16 = jnp.bfloat16

D_MODEL = 1024
N_DEV = 8
SGU_WIDTH = 384
CONV_WIDTH = 384
POOL_WIDTH = 256
HEAD_DIM = 64
SGU_HEADS = 6
CHUNK = 128
CONV_K = 31
POOL_WINDOWS = (2, 4, 8, 16)
IN_WIDTH = 1792
D_FF = 2816
FF_SHARD = 2 * D_FF // N_DEV
FF_PAIRS = N_DEV // 2
FFN_CONV_K = 3
EPS = 1e-6
GELU_C0 = math.sqrt(2.0 / math.pi)
GELU_C1 = 0.044715

ADAM_LR = 0.001
ADAM_B1 = 0.9
ADAM_B2 = 0.999
ADAM_EPS = 1e-08
ADAM_WD = 0.01
ADAM_STEP = 10

VMEM_LIMIT_BYTES = 56 * 1024 * 1024
TILE = 256
MIX_TILE = TILE
FFN_TILE = TILE
FFN_HALO = 8 * (FFN_CONV_K - 1)
POOL_HALO = 8 * POOL_WINDOWS[-1]
WGRAD_TK = 2048
WGRAD_TK_FFN = 4096


def _cparams(n_axes):
    return pltpu.CompilerParams(dimension_semantics=("arbitrary",) * n_axes, vmem_limit_bytes=VMEM_LIMIT_BYTES)


def _whole(shape):
    nd = len(shape)
    return pl.BlockSpec(shape, lambda *_: (0,) * nd, pipeline_mode=pl.Buffered(1))


def _dot(a, b):
    return jnp.dot(a, b, preferred_element_type=F32)


def _dot_nt(a, b):
    return lax.dot_general(a, b, (((1,), (1,)), ((), ())), preferred_element_type=F32)


def _dot_tn(a, b):
    return lax.dot_general(a, b, (((0,), (0,)), ((), ())), preferred_element_type=F32)


def _gelu(x):
    t = jnp.tanh(GELU_C0 * (x + GELU_C1 * x * x * x))
    return 0.5 * x * (1.0 + t), t


def _gelu_grad(x, t):
    return 0.5 * (1.0 + t) + 0.5 * x * (1.0 - t * t) * (GELU_C0 * (1.0 + 3.0 * GELU_C1 * x * x))


def _rowmean(x):
    return jnp.mean(x, axis=-1, keepdims=True)


def _colsum(x):
    return jnp.sum(x, axis=0, keepdims=True)


def _rms_fwd(x):
    r = lax.rsqrt(_rowmean(x * x) + EPS)
    return x * r, r


def _rms_bwd(dxhat, xhat, r):
    return r * (dxhat - xhat * _rowmean(dxhat * xhat))


N_PEERS = N_DEV - 1
ANY = pl.BlockSpec(memory_space=pl.ANY)
VMEM = pl.BlockSpec(memory_space=pltpu.VMEM)


def _my_pos():
    return lax.axis_index("x"), lax.axis_index("y"), lax.axis_index("c")


def _peer(pos, k):
    x, y, c = pos
    return (1 - x if k & 4 else x, 1 - y if k & 2 else y, 1 - c if k & 1 else c)


def _flat(pos):
    return 4 * pos[0] + 2 * pos[1] + pos[2]


def _remote_copy(src, dst, send_sem, recv_sem, peer):
    return pltpu.make_async_remote_copy(src_ref=src, dst_ref=dst, send_sem=send_sem, recv_sem=recv_sem,
                                        device_id=peer, device_id_type=pl.DeviceIdType.MESH)


N_CHIPS = N_DEV // 2
SIBLING = 1
SAME_CORE_PEERS = (2, 4, 6)


def _exchange_out_shapes(job):
    def shape(kind, a):
        if kind in ("gather", "gather2"):
            return (N_DEV,) + a.shape
        if kind == "scatter_p1":
            return (N_CHIPS,) + a.shape[1:]
        return a.shape
    return [jax.ShapeDtypeStruct(shape(kind, a), a.dtype) for kind, a in job]


def _exchange_sems(n):
    return [pltpu.SemaphoreType.DMA((n, N_PEERS)), pltpu.SemaphoreType.DMA((n, N_PEERS)), pltpu.SemaphoreType.DMA((n,))]


def _exchange_copies(kinds, src_refs, dst_refs, send_sems, recv_sems, local_sems, phase):
    pos = _my_pos()
    me = _flat(pos)
    chip, core = 2 * pos[0] + pos[1], pos[2]
    copies = []

    def remote(a, src, dst, k, sem=None):
        sem = k - 1 if sem is None else sem
        copies.append(_remote_copy(src, dst, send_sems.at[a, sem], recv_sems.at[a, sem], _peer(pos, k)))

    for a, kind in enumerate(kinds):
        src, dst = src_refs[a], dst_refs[a]
        if phase == 1:
            if kind == "gather2":
                for k in SAME_CORE_PEERS:
                    remote(a, dst.at[me ^ k], dst.at[me ^ k], SIBLING, sem=k)
        elif kind in ("gather", "gather2"):
            copies.append(pltpu.make_async_copy(src, dst.at[me], local_sems.at[a]))
            for k in (range(1, N_DEV) if kind == "gather" else (SIBLING,) + SAME_CORE_PEERS):
                remote(a, src, dst.at[me], k)
        elif kind == "scatter":
            copies.append(pltpu.make_async_copy(src.at[me], dst.at[me], local_sems.at[a]))
            for k in range(1, N_DEV):
                remote(a, src.at[me ^ k], dst.at[me], k)
        elif kind == "scatter_p1":
            for q in range(N_CHIPS):
                remote(a, src.at[2 * q + 1 - core], dst.at[q], SIBLING, sem=q)
        elif kind == "scatter_p2":
            copies.append(pltpu.make_async_copy(src.at[chip], dst.at[chip], local_sems.at[a]))
            for k in SAME_CORE_PEERS:
                remote(a, src.at[chip ^ (k >> 1)], dst.at[chip], k)
    return copies


def _pallas_call_with_exchange(body, *, grid, in_specs, out_specs, out_shape, scratch_shapes, operands, name, job,
                               relay_steps_before_end=2):
    params = _cparams(len(grid))
    if not job:
        outs = pl.pallas_call(body, grid=grid, in_specs=in_specs, out_specs=out_specs, out_shape=out_shape,
                              scratch_shapes=scratch_shapes, compiler_params=params, name=name)(*operands)
        return outs, []
    kinds = [kind for kind, _ in job]
    relayed = [kind if kind == "gather2" else None for kind in kinds]
    unrelayed = [None if kind == "gather2" else kind for kind in kinds]
    n, n_in, n_out, n_scr = len(job), len(in_specs), len(out_specs), len(scratch_shapes)
    n_steps = math.prod(grid)
    relay_step = max(n_steps - relay_steps_before_end, 0)

    def wrapped(*refs):
        ins, jin = refs[:n_in], refs[n_in:n_in + n]
        outs, jout = refs[n_in + n:n_in + n + n_out], refs[n_in + n + n_out:n_in + 2 * n + n_out]
        scr = refs[n_in + 2 * n + n_out:n_in + 2 * n + n_out + n_scr]
        sems = refs[n_in + 2 * n + n_out + n_scr:]
        step = pl.program_id(0)
        for d in range(1, len(grid)):
            step = step * grid[d] + pl.program_id(d)

        def copies(which, phase):
            return _exchange_copies(which, jin, jout, *sems, phase=phase)

        @pl.when(step == 0)
        def _():
            for cp in copies(kinds, 0):
                cp.start()

        body(*ins, *outs, *scr)

        @pl.when(step == relay_step)
        def _():
            for cp in copies(relayed, 0):
                cp.wait()
            for cp in copies(relayed, 1):
                cp.start()

        @pl.when(step == n_steps - 1)
        def _():
            for cp in copies(unrelayed, 0) + copies(relayed, 1):
                cp.wait()

    res = pl.pallas_call(
        wrapped, grid=grid,
        in_specs=list(in_specs) + [ANY] * n,
        out_specs=list(out_specs) + [ANY] * n,
        out_shape=list(out_shape) + _exchange_out_shapes(job),
        scratch_shapes=list(scratch_shapes) + _exchange_sems(n),
        compiler_params=params, name=name,
    )(*operands, *[a for _, a in job])
    return res[:n_out], res[n_out:]


def _seg_mean(x, segp):
    hi = x.astype(BF16)
    lo = (x - hi.astype(F32)).astype(BF16)
    return _dot(hi, segp) + _dot(lo, segp)


def _rot_rows(x, shift):
    m, c = x.shape
    return pltpu.roll(x.reshape(m // 8, 8, c), shift, 1).reshape(m, c)


def _sublane_is(shape, s):
    return lax.broadcasted_iota(jnp.int32, shape, 0) % 8 == s


def _causal_tail(tail, prev_rot):
    rot = _rot_rows(tail, 1)
    return jnp.where(_sublane_is(tail.shape, 0), prev_rot, rot), rot


def _lookahead_head(head, next_rot):
    rot = _rot_rows(head, 7)
    return jnp.where(_sublane_is(head.shape, 7), next_rot, rot), rot


def _tile_token_index(t, tile_idx):
    r = lax.broadcasted_iota(jnp.int32, (t, 1), 0)
    return tile_idx * t + (r % 8) * (t // 8) + r // 8


def _interleave(x):
    t, c = x.shape
    return jnp.swapaxes(x.reshape(8, t // 8, c), 0, 1).reshape(t, c)


def _deinterleave(x):
    t, c = x.shape
    return jnp.swapaxes(x.reshape(t // 8, 8, c), 0, 1).reshape(t, c)


def _ffn_fwd(x1, modv, g1024, wup, wdn, cw, cb, name="ffn_fwd", job=None, loss_target=None):
    s_len = x1.shape[0]
    t = FFN_TILE
    n_tiles = s_len // t
    with_loss = loss_target is not None

    def body(x1_ref, *rest):
        if with_loss:
            tgt_ref, rest = rest[0], rest[1:]
            loss_ref, rest = rest[10], rest[:10] + rest[11:]
        mod_ref, g_ref, wup_ref, wdn_ref, cw_ref, cb_ref, x2_ref, y2_ref, p_ref, u_ref, ext_ref, carry_ref = rest
        i = pl.program_id(0)

        @pl.when(i == 0)
        def _():
            carry_ref[...] = jnp.zeros_like(carry_ref)
            if with_loss:
                loss_ref[...] = jnp.zeros_like(loss_ref)

        x1v = x1_ref[...]
        pre_g, post_g = g_ref[3:4, :], g_ref[4:5, :]
        sh2, sc2, g2 = mod_ref[3:4, :], mod_ref[4:5, :], mod_ref[5:6, :]
        xhat, _ = _rms_fwd(x1v)
        h2b = (xhat * pre_g * (1.0 + sc2) + sh2).astype(BF16)

        def conv_shard(s):
            p = _dot_nt(h2b, wup_ref[s])
            p_ref[s] = p.astype(BF16)
            ext_ref[0:FFN_HALO, :], carry_ref[s] = _causal_tail(p[t - FFN_HALO:t, :], carry_ref[s])
            ext_ref[FFN_HALO:FFN_HALO + t, :] = p
            w = cw_ref[s]
            u = w[0:1, :] * ext_ref[0:t, :] + w[1:2, :] * ext_ref[8:8 + t, :] + w[2:3, :] * p + cb_ref[s]
            u_ref[s] = u.astype(BF16)
            return u

        y2 = jnp.zeros((t, D_MODEL), F32)
        for j in range(FF_PAIRS):
            ug = conv_shard(j)
            uv = conv_shard(j + FF_PAIRS)
            ge, _ = _gelu(ug)
            y2 = y2 + _dot((ge * uv).astype(BF16), wdn_ref[j])
        y2_ref[...] = y2
        yhat, _ = _rms_fwd(y2)
        x2 = x1v + g2 * (yhat * post_g)
        if with_loss:
            diff = x2 - _interleave(tgt_ref[...])
            x2_ref[...] = diff * (1.0 / D_MODEL)
            sq = jnp.sum((diff * diff).reshape(t // 8, 8, D_MODEL), axis=0)
            loss_ref[...] += (0.5 / D_MODEL) * sum(sq[:, 128 * j:128 * (j + 1)] for j in range(D_MODEL // 128))
        else:
            x2_ref[...] = x2

    tile = pl.BlockSpec((t, D_MODEL), lambda i: (i, 0))
    consts = (modv, g1024, wup, wdn, cw, cb)
    shards = pl.BlockSpec((N_DEV, t, FF_SHARD), lambda i: (0, i, 0))
    out_specs = [tile, tile, shards, shards]
    out_shape = [jax.ShapeDtypeStruct((s_len, D_MODEL), F32), jax.ShapeDtypeStruct((s_len, D_MODEL), F32),
                 jax.ShapeDtypeStruct((N_DEV, s_len, FF_SHARD), BF16),
                 jax.ShapeDtypeStruct((N_DEV, s_len, FF_SHARD), BF16)]
    if with_loss:
        out_specs.append(pl.BlockSpec((8, 128), lambda i: (0, 0)))
        out_shape.append(jax.ShapeDtypeStruct((8, 128), F32))
    return _pallas_call_with_exchange(
        body,
        grid=(n_tiles,),
        in_specs=[tile] * (2 if with_loss else 1) + [_whole(c.shape) for c in consts],
        out_specs=out_specs,
        out_shape=out_shape,
        scratch_shapes=[pltpu.VMEM((FFN_HALO + t, FF_SHARD), F32), pltpu.VMEM((N_DEV, FFN_HALO, FF_SHARD), F32)],
        operands=(x1,) + ((loss_target,) if with_loss else ()) + consts,
        name=name, job=job)


def _ffn_bwd(dx2, x1, y2, p, u, modv, g1024, wup, wdn, cw, name="ffn_bwd", job=None):
    s_len = x1.shape[0]
    t = FFN_TILE
    n_tiles = s_len // t
    hb = FFN_HALO

    def body(dx2_ref, x1_ref, y2_ref, p_ref, u_ref, mod_ref, g_ref, wup_ref, wdn_ref, cw_ref,
             dx1_ref, dp_ref, a_ref, dy2_ref, h2_ref, vec_ref, cgrad_ref, dext_ref, dcarry_ref):
        i = pl.program_id(0)

        @pl.when(i == 0)
        def _():
            vec_ref[...] = jnp.zeros_like(vec_ref)
            cgrad_ref[...] = jnp.zeros_like(cgrad_ref)
            dcarry_ref[...] = jnp.zeros_like(dcarry_ref)

        dx2v, x1v, y2v = dx2_ref[...], x1_ref[...], y2_ref[...]
        pre_g, post_g = g_ref[3:4, :], g_ref[4:5, :]
        sh2, sc2, g2 = mod_ref[3:4, :], mod_ref[4:5, :], mod_ref[5:6, :]

        yhat, ry = _rms_fwd(y2v)
        vec_ref[1:2, :] += _colsum(dx2v * (yhat * post_g))
        dyn = dx2v * g2
        vec_ref[0:1, :] += _colsum(dyn * yhat)
        dy2b = _rms_bwd(dyn * post_g, yhat, ry).astype(BF16)
        dy2_ref[...] = dy2b

        xhat, rx = _rms_fwd(x1v)
        xn = xhat * pre_g
        h2_ref[...] = (xn * (1.0 + sc2) + sh2).astype(BF16)

        def conv_bwd(s, du):
            w = cw_ref[s]
            pf = p_ref[s].astype(F32)
            dext_ref[0:t, :] = du
            dext_ref[t:t + hb, :], dcarry_ref[s] = _lookahead_head(du[0:hb, :], dcarry_ref[s])
            du1, du2 = dext_ref[8:8 + t, :], dext_ref[16:16 + t, :]
            cgrad_ref[s, 0:1, :] += _colsum(du2 * pf)
            cgrad_ref[s, 1:2, :] += _colsum(du1 * pf)
            cgrad_ref[s, 2:3, :] += _colsum(du * pf)
            cgrad_ref[s, 3:4, :] += _colsum(du)
            dpb = (w[2:3, :] * du + w[1:2, :] * du1 + w[0:1, :] * du2).astype(BF16)
            dp_ref[s] = dpb
            return _dot(dpb, wup_ref[s])

        dh2 = jnp.zeros((t, D_MODEL), F32)
        for j in range(FF_PAIRS):
            ug = u_ref[j].astype(F32)
            uv = u_ref[j + FF_PAIRS].astype(F32)
            ge, th = _gelu(ug)
            a_ref[j] = (ge * uv).astype(BF16)
            da = _dot_nt(dy2b, wdn_ref[j])
            dh2 = dh2 + conv_bwd(j, da * uv * _gelu_grad(ug, th))
            dh2 = dh2 + conv_bwd(j + FF_PAIRS, da * ge)

        vec_ref[2:3, :] += _colsum(dh2)
        vec_ref[3:4, :] += _colsum(dh2 * xn)
        dxn = dh2 * (1.0 + sc2)
        vec_ref[4:5, :] += _colsum(dxn * xhat)
        dx1_ref[...] = dx2v + _rms_bwd(dxn * pre_g, xhat, rx)

    rev = lambda i: (n_tiles - 1 - i, 0)
    tile = pl.BlockSpec((t, D_MODEL), rev)
    return _pallas_call_with_exchange(
        body,
        grid=(n_tiles,),
        in_specs=[tile, tile, tile,
                  pl.BlockSpec((N_DEV, t, FF_SHARD), lambda i: (0, n_tiles - 1 - i, 0)),
                  pl.BlockSpec((N_DEV, t, FF_SHARD), lambda i: (0, n_tiles - 1 - i, 0)),
                  _whole(modv.shape), _whole(g1024.shape), _whole(wup.shape), _whole(wdn.shape),
                  _whole(cw.shape)],
        out_specs=[tile,
                   pl.BlockSpec((N_DEV, t, FF_SHARD), lambda i: (0, n_tiles - 1 - i, 0)),
                   pl.BlockSpec((FF_PAIRS, t, FF_SHARD), lambda i: (0, n_tiles - 1 - i, 0)),
                   tile, tile,
                   pl.BlockSpec((8, D_MODEL), lambda i: (0, 0)),
                   pl.BlockSpec((N_DEV, 8, FF_SHARD), lambda i: (0, 0, 0))],
        out_shape=[jax.ShapeDtypeStruct((s_len, D_MODEL), F32),
                   jax.ShapeDtypeStruct((N_DEV, s_len, FF_SHARD), BF16),
                   jax.ShapeDtypeStruct((FF_PAIRS, s_len, FF_SHARD), BF16),
                   jax.ShapeDtypeStruct((s_len, D_MODEL), BF16),
                   jax.ShapeDtypeStruct((s_len, D_MODEL), BF16),
                   jax.ShapeDtypeStruct((8, D_MODEL), F32),
                   jax.ShapeDtypeStruct((N_DEV, 8, FF_SHARD), F32)],
        scratch_shapes=[pltpu.VMEM((t + hb, FF_SHARD), F32), pltpu.VMEM((N_DEV, hb, FF_SHARD), F32)],
        operands=(dx2, x1, y2, p, u, modv, g1024, wup, wdn, cw),
        name=name, job=job)


def _wgrad(a, b, name, tk=WGRAD_TK, job=None):
    a_grouped, b_grouped = a.ndim == 3, b.ndim == 3
    groups = a.shape[0] if a_grouped else b.shape[0]
    s_len, m, n = a.shape[-2], a.shape[-1], b.shape[-1]
    tk = min(tk, s_len)
    n_k = s_len // tk

    def body(a_ref, b_ref, o_ref, acc_ref):
        k = pl.program_id(1)
        av = a_ref[0] if a_grouped else a_ref[...]
        bv = b_ref[0] if b_grouped else b_ref[...]
        part = _dot_tn(av, bv)
        if n_k == 1:
            o_ref[0] = part.astype(BF16)
            return

        @pl.when(k == 0)
        def _():
            acc_ref[...] = part

        @pl.when(jnp.logical_and(k > 0, k < n_k - 1))
        def _():
            acc_ref[...] += part

        @pl.when(k == n_k - 1)
        def _():
            o_ref[0] = (acc_ref[...] + part).astype(BF16)

    a_spec = pl.BlockSpec((1, tk, m), lambda g, k: (g, k, 0)) if a_grouped else pl.BlockSpec((tk, m), lambda g, k: (k, 0))
    b_spec = pl.BlockSpec((1, tk, n), lambda g, k: (g, k, 0)) if b_grouped else pl.BlockSpec((tk, n), lambda g, k: (k, 0))
    (out,), exchanged = _pallas_call_with_exchange(
        body,
        grid=(groups, n_k),
        in_specs=[a_spec, b_spec],
        out_specs=[pl.BlockSpec((1, m, n), lambda g, k: (g, 0, 0))],
        out_shape=[jax.ShapeDtypeStruct((groups, m, n), BF16)],
        scratch_shapes=[pltpu.VMEM((m, n), F32)],
        operands=(a, b),
        name=name, job=job)
    return (out, exchanged) if job else out


def _lane(shape):
    return lax.broadcasted_iota(jnp.int32, shape, 1)


def _by_pool_group(shape, vals):
    lane = _lane(shape)
    return jnp.where(lane < 64, vals[0], jnp.where(lane < 128, vals[1], jnp.where(lane < 192, vals[2], vals[3])))


def _pool_inv_counts(t, tile_idx):
    pos1 = _tile_token_index(t, tile_idx) + 1
    return [1.0 / jnp.minimum(pos1, w).astype(F32) for w in POOL_WINDOWS]


def _sgu_keep_mask(t):
    tok_r = _tile_token_index(t, 0)
    c = lax.broadcasted_iota(jnp.int32, (1, t), 1)
    tok_c = (c % 8) * (t // 8) + c // 8
    return jnp.logical_and(tok_r // CHUNK == tok_c // CHUNK, tok_r >= tok_c)


def _masked_sgu_w(sguw_ref):
    keep = _sgu_keep_mask(sguw_ref.shape[1])
    return [jnp.where(keep, sguw_ref[h], 0.0).astype(BF16) for h in range(SGU_HEADS)]


def _branches_fwd(z, tile_idx, p384_ref, cw_ref, wm, bmat_ref, pwbd_ref, psc_ref, segp_ref, g_ref, hext_ref, zext_ref,
                  h_prev_rot, z_prev_rot, conv_saved=None):
    t = z.shape[0]
    segp = segp_ref[...]
    r = {}
    u, _ = _gelu(z[:, 0:SGU_WIDTH])
    vraw, _ = _gelu(z[:, SGU_WIDTH:2 * SGU_WIDTH])
    xc = vraw - _seg_mean(vraw, segp)
    rstd_v = lax.rsqrt(_seg_mean(xc * xc, segp) + EPS)
    xh_v = xc * rstd_v
    vnb = (xh_v * p384_ref[0:1, :] + p384_ref[1:2, :]).astype(BF16)
    first_head = _lane((t, 128)) < HEAD_DIM
    f_pairs = []
    for pr in range(SGU_HEADS // 2):
        vp = vnb[:, pr * 128:(pr + 1) * 128]
        f_pairs.append(jnp.where(first_head, _dot(wm[2 * pr], vp), _dot(wm[2 * pr + 1], vp)))
    f = jnp.concatenate(f_pairs, axis=1) + bmat_ref[...]
    ya = u * f
    r.update(u=u, xh_v=xh_v, rstd_v=rstd_v, vnb=vnb, f=f)
    o_b = 2 * SGU_WIDTH
    a_in = z[:, o_b:o_b + CONV_WIDTH]
    sig_g = jax.nn.sigmoid(z[:, o_b + CONV_WIDTH:o_b + 2 * CONV_WIDTH])
    hh = a_in * sig_g
    if conv_saved is None:
        hext_ref[0:t, :], r["h_rot"] = _causal_tail(hh, h_prev_rot)
        hext_ref[t:2 * t, :] = hh
        conv = jnp.zeros((t, CONV_WIDTH), F32) + p384_ref[2:3, :]
        for k in range(CONV_K):
            conv = conv + cw_ref[k:k + 1, :] * hext_ref[pl.ds(t - 8 * (CONV_K - 1 - k), t), :]
        r["conv"] = conv
    else:
        conv = conv_saved
    cc = conv - _rowmean(conv)
    rstd_c = lax.rsqrt(_rowmean(cc * cc) + EPS)
    xh_c = cc * rstd_c
    cn = xh_c * p384_ref[3:4, :] + p384_ref[4:5, :]
    sig_c = jax.nn.sigmoid(cn)
    yb = cn * sig_c
    r.update(a_in=a_in, sig_g=sig_g, hh=hh, xh_c=xh_c, rstd_c=rstd_c, cn=cn, sig_c=sig_c)
    o_c = o_b + 2 * CONV_WIDTH
    zc = z[:, o_c:o_c + POOL_WIDTH]
    zext_ref[0:POOL_HALO, :], r["z_rot"] = _causal_tail(zc[t - POOL_HALO:t, :], z_prev_rot)
    zext_ref[POOL_HALO:POOL_HALO + t, :] = zc
    sums, acc = [], zc
    for j in range(1, POOL_WINDOWS[-1]):
        acc = acc + zext_ref[pl.ds(POOL_HALO - 8 * j, t), :]
        if j + 1 in POOL_WINDOWS:
            sums.append(acc)
    inv = _pool_inv_counts(t, tile_idx)
    dpool = _by_pool_group((t, POOL_WIDTH), [s * iv for s, iv in zip(sums, inv)]) - zc
    ycp = _dot(dpool.astype(BF16), pwbd_ref[...])
    yc = ycp * psc_ref[0:1, :]
    r.update(dpool=dpool, ycp=ycp)
    yha, ra = _rms_fwd(ya)
    yhb, rb = _rms_fwd(yb)
    yhc, rc = _rms_fwd(yc)
    bg = g_ref[2:3, :]
    ycat = jnp.concatenate([yha * bg[:, 0:384], yhb * bg[:, 384:768], yhc * bg[:, 768:1024]], axis=1)
    r.update(yha=yha, ra=ra, yhb=yhb, rb=rb, yhc=yhc, rc=rc, ycat=ycat)
    return r


def _mixer_fwd(x, modv, g1024, p384, cw, sguw, bmat, pwbd, psc, segp, win, wout, name="mixer_fwd", job=None,
               natural_x=False, relay_steps_before_end=2):
    s_len = x.shape[0]
    t = MIX_TILE

    def body(x_ref, mod_ref, g_ref, p384_ref, cw_ref, sguw_ref, bmat_ref, pwbd_ref, psc_ref, segp_ref, win_ref, wout_ref,
             x1_ref, z_ref, o_ref, conv_ref, hext_ref, zext_ref, hrot_ref, zrot_ref, wm_ref):
        i = pl.program_id(0)

        @pl.when(i == 0)
        def _():
            hrot_ref[...] = jnp.zeros_like(hrot_ref)
            zrot_ref[...] = jnp.zeros_like(zrot_ref)
            for h, wmh in enumerate(_masked_sgu_w(sguw_ref)):
                wm_ref[h] = wmh

        xv = _interleave(x_ref[...]) if natural_x else x_ref[...]
        sh1, sc1, g1 = mod_ref[0:1, :], mod_ref[1:2, :], mod_ref[2:3, :]
        xhat, _ = _rms_fwd(xv)
        h1 = xhat * g_ref[0:1, :] * (1.0 + sc1) + sh1
        z = _dot_nt(h1.astype(BF16), win_ref[...])
        z_ref[...] = z
        r = _branches_fwd(z, i, p384_ref, cw_ref, [wm_ref[h] for h in range(SGU_HEADS)], bmat_ref, pwbd_ref, psc_ref,
                          segp_ref, g_ref, hext_ref, zext_ref, hrot_ref[...], zrot_ref[...])
        hrot_ref[...] = r["h_rot"]
        zrot_ref[...] = r["z_rot"]
        conv_ref[...] = r["conv"]
        o = _dot(r["ycat"].astype(BF16), wout_ref[...])
        o_ref[...] = o
        ohat, _ = _rms_fwd(o)
        x1_ref[...] = xv + g1 * (ohat * g_ref[1:2, :])

    tile = pl.BlockSpec((t, D_MODEL), lambda i: (i, 0))
    consts = (modv, g1024, p384, cw, sguw, bmat, pwbd, psc, segp, win, wout)
    return _pallas_call_with_exchange(
        body,
        grid=(s_len // t,),
        in_specs=[tile] + [_whole(c.shape) for c in consts],
        out_specs=[tile, pl.BlockSpec((t, IN_WIDTH), lambda i: (i, 0)), tile,
                   pl.BlockSpec((t, CONV_WIDTH), lambda i: (i, 0))],
        out_shape=[jax.ShapeDtypeStruct((s_len, D_MODEL), F32), jax.ShapeDtypeStruct((s_len, IN_WIDTH), F32),
                   jax.ShapeDtypeStruct((s_len, D_MODEL), F32), jax.ShapeDtypeStruct((s_len, CONV_WIDTH), F32)],
        scratch_shapes=[pltpu.VMEM((2 * t, CONV_WIDTH), F32), pltpu.VMEM((POOL_HALO + t, POOL_WIDTH), F32),
                        pltpu.VMEM((t, CONV_WIDTH), F32), pltpu.VMEM((POOL_HALO, POOL_WIDTH), F32),
                        pltpu.VMEM((SGU_HEADS, t, t), BF16)],
        operands=(x, *consts),
        name=name, job=job, relay_steps_before_end=relay_steps_before_end)


def _mixer_bwd(dx1, x, o, z, conv, modv, g1024, p384, cw, sguw, bmat, pwbd, psc, segp, win, wout, name="mixer_bwd",
               job=None, natural_x=False):
    s_len = x.shape[0]
    t = MIX_TILE
    n_tiles = s_len // t

    def body(dx1_ref, x_ref, o_ref, z_ref, zh_ref, conv_ref, mod_ref, g_ref, p384_ref, cw_ref, sguw_ref, bmat_ref, pwbd_ref,
             psc_ref, segp_ref, win_ref, wout_ref,
             dx_ref, dz_ref, do_ref, ycat_ref, h1_ref, vec_ref, v384_ref, dcw_ref, dsguw_out_ref, dbmat_out_ref, dpw_ref,
             dpsc_ref, zext_ref, gext_ref, qext_ref, grot_ref, qrot_ref, wm_ref, dsguw_ref, dbmat_ref):
        i = pl.program_id(0)
        tile_idx = n_tiles - 1 - i

        @pl.when(i == 0)
        def _():
            for ref in (vec_ref, v384_ref, dcw_ref, dsguw_ref, dbmat_ref, dpw_ref, dpsc_ref, grot_ref, qrot_ref):
                ref[...] = jnp.zeros_like(ref)
            for h, wmh in enumerate(_masked_sgu_w(sguw_ref)):
                wm_ref[h] = wmh

        dx1v, ov, z = dx1_ref[...], o_ref[...], z_ref[...]
        xv = _interleave(x_ref[...]) if natural_x else x_ref[...]
        sh1, sc1, g1 = mod_ref[0:1, :], mod_ref[1:2, :], mod_ref[2:3, :]
        pre_g, post_g, bg = g_ref[0:1, :], g_ref[1:2, :], g_ref[2:3, :]
        segp = segp_ref[...]

        ohat, ro = _rms_fwd(ov)
        vec_ref[1:2, :] += _colsum(dx1v * (ohat * post_g))
        don = dx1v * g1
        vec_ref[0:1, :] += _colsum(don * ohat)
        dob = _rms_bwd(don * post_g, ohat, ro).astype(BF16)
        do_ref[...] = dob
        dycat = _dot_nt(dob, wout_ref[...])

        not_first = (tile_idx > 0).astype(F32)
        o_b = 2 * SGU_WIDTH
        z_prev_rot = _rot_rows(zh_ref[...], 1) * not_first
        wm = [wm_ref[h] for h in range(SGU_HEADS)]
        r = _branches_fwd(z, tile_idx, p384_ref, cw_ref, wm, bmat_ref, pwbd_ref, psc_ref, segp_ref, g_ref,
                          None, zext_ref, None, z_prev_rot, conv_saved=conv_ref[...])
        ycat_ref[...] = r["ycat"].astype(BF16)

        def branch_norm_bwd(dyn, yhat, rr, gain):
            return _colsum(dyn * yhat), _rms_bwd(dyn * gain, yhat, rr)

        dga, dya = branch_norm_bwd(dycat[:, 0:384], r["yha"], r["ra"], bg[:, 0:384])
        dgb, dyb = branch_norm_bwd(dycat[:, 384:768], r["yhb"], r["rb"], bg[:, 384:768])
        dgc, dyc = branch_norm_bwd(dycat[:, 768:1024], r["yhc"], r["rc"], bg[:, 768:1024])
        vec_ref[5:6, :] += jnp.concatenate([dga, dgb, dgc], axis=1)

        du_act = dya * r["f"]
        df = dya * r["u"]
        first_head = _lane((t, 128)) < HEAD_DIM
        dbmat_ref[...] += df
        dvn_pairs = []
        for pr in range(SGU_HEADS // 2):
            dfp = df[:, pr * 128:(pr + 1) * 128]
            df0 = jnp.where(first_head, dfp, 0.0).astype(BF16)
            df1 = jnp.where(first_head, 0.0, dfp).astype(BF16)
            vp = r["vnb"][:, pr * 128:(pr + 1) * 128]
            dvn_pairs.append(_dot_tn(wm[2 * pr], df0) + _dot_tn(wm[2 * pr + 1], df1))
            dsguw_ref[2 * pr] += _dot_nt(df0, vp)
            dsguw_ref[2 * pr + 1] += _dot_nt(df1, vp)
        dvn = jnp.concatenate(dvn_pairs, axis=1)
        v384_ref[0:1, :] += _colsum(dvn * r["xh_v"])
        v384_ref[1:2, :] += _colsum(dvn)
        dxh = dvn * p384_ref[0:1, :]
        dvraw = r["rstd_v"] * (dxh - _seg_mean(dxh, segp) - r["xh_v"] * _seg_mean(dxh * r["xh_v"], segp))
        zu, zv = z[:, 0:SGU_WIDTH], z[:, SGU_WIDTH:o_b]
        _, tu = _gelu(zu)
        _, tv = _gelu(zv)
        dz_u = du_act * _gelu_grad(zu, tu)
        dz_v = dvraw * _gelu_grad(zv, tv)

        cn, sig_c = r["cn"], r["sig_c"]
        dcn = dyb * (sig_c * (1.0 + cn * (1.0 - sig_c)))
        v384_ref[3:4, :] += _colsum(dcn * r["xh_c"])
        v384_ref[4:5, :] += _colsum(dcn)
        dxc = dcn * p384_ref[3:4, :]
        gconv = r["rstd_c"] * (dxc - _rowmean(dxc) - r["xh_c"] * _rowmean(dxc * r["xh_c"]))
        v384_ref[2:3, :] += _colsum(gconv)
        gext_ref[0:t, :] = gconv
        gext_ref[t:2 * t, :], grot_ref[...] = _lookahead_head(gconv, grot_ref[...])
        dhh = jnp.zeros((t, CONV_WIDTH), F32)
        hh = r["hh"]
        for k in range(CONV_K):
            shift = CONV_K - 1 - k
            g_ahead = gext_ref[pl.ds(8 * shift, t), :]
            dcw_ref[k:k + 1, :] += _colsum(g_ahead * hh)
            dhh = dhh + cw_ref[k:k + 1, :] * g_ahead
        sig_g = r["sig_g"]
        dz_a = dhh * sig_g
        dz_g = dhh * r["a_in"] * sig_g * (1.0 - sig_g)

        dpsc_ref[0:1, :] += _colsum(dyc * r["ycp"])
        dycp = (dyc * psc_ref[0:1, :]).astype(BF16)
        dpw_ref[...] += _dot_tn(r["dpool"].astype(BF16), dycp)
        ddp = _dot_nt(dycp, pwbd_ref[...])
        inv = _pool_inv_counts(t, tile_idx)
        q = ddp * _by_pool_group((t, POOL_WIDTH), inv)
        qext_ref[0:t, :] = q
        qext_ref[t:t + POOL_HALO, :], qrot_ref[...] = _lookahead_head(q[0:POOL_HALO, :], qrot_ref[...])
        sums, acc = [], q
        for j in range(1, POOL_WINDOWS[-1]):
            acc = acc + qext_ref[pl.ds(8 * j, t), :]
            if j + 1 in POOL_WINDOWS:
                sums.append(acc)
        dz_c = _by_pool_group((t, POOL_WIDTH), sums) - ddp

        dzb = jnp.concatenate([dz_u, dz_v, dz_a, dz_g, dz_c], axis=1).astype(BF16)
        dz_ref[...] = dzb
        dh1 = _dot(dzb, win_ref[...])

        xhat, rx = _rms_fwd(xv)
        xn = xhat * pre_g
        h1_ref[...] = (xn * (1.0 + sc1) + sh1).astype(BF16)
        vec_ref[2:3, :] += _colsum(dh1)
        vec_ref[3:4, :] += _colsum(dh1 * xn)
        dxn = dh1 * (1.0 + sc1)
        vec_ref[4:5, :] += _colsum(dxn * xhat)
        dx = dx1v + _rms_bwd(dxn * pre_g, xhat, rx)
        dx_ref[...] = _deinterleave(dx) if natural_x else dx

        @pl.when(i == n_tiles - 1)
        def _():
            keep = _sgu_keep_mask(t)
            for h in range(SGU_HEADS):
                rows_natural = _deinterleave(jnp.where(keep, dsguw_ref[h], 0.0))
                natural = _deinterleave(rows_natural.T).T
                dsguw_out_ref[h] = sum(natural[b * CHUNK:(b + 1) * CHUNK, b * CHUNK:(b + 1) * CHUNK]
                                       for b in range(t // CHUNK))
            dbmat = _deinterleave(float(HEAD_DIM) * _seg_mean(dbmat_ref[...], segp))
            dbmat_out_ref[...] = sum(dbmat[b * CHUNK:(b + 1) * CHUNK, :] for b in range(t // CHUNK))

    rev = lambda i: (n_tiles - 1 - i, 0)
    tile = pl.BlockSpec((t, D_MODEL), rev)
    ztile = pl.BlockSpec((t, IN_WIDTH), rev)
    zhalo = pl.BlockSpec((POOL_HALO, POOL_WIDTH),
                         lambda i: (jnp.maximum((n_tiles - 1 - i) * (t // POOL_HALO) - 1, 0),
                                    IN_WIDTH // POOL_WIDTH - 1))
    consts = (modv, g1024, p384, cw, sguw, bmat, pwbd, psc, segp, win, wout)
    acc = lambda shape: pl.BlockSpec(shape, lambda i: (0,) * len(shape))
    acc_shapes = [(8, D_MODEL), (8, SGU_WIDTH), (32, CONV_WIDTH), (SGU_HEADS, CHUNK, CHUNK), (CHUNK, SGU_WIDTH),
                  (POOL_WIDTH, POOL_WIDTH), (8, POOL_WIDTH)]
    return _pallas_call_with_exchange(
        body,
        grid=(n_tiles,),
        in_specs=[tile, tile, tile, ztile, zhalo, pl.BlockSpec((t, CONV_WIDTH), rev)] + [_whole(c.shape) for c in consts],
        out_specs=[tile, ztile, tile, tile, tile] + [acc(s) for s in acc_shapes],
        out_shape=[jax.ShapeDtypeStruct((s_len, D_MODEL), F32), jax.ShapeDtypeStruct((s_len, IN_WIDTH), BF16),
                   jax.ShapeDtypeStruct((s_len, D_MODEL), BF16), jax.ShapeDtypeStruct((s_len, D_MODEL), BF16),
                   jax.ShapeDtypeStruct((s_len, D_MODEL), BF16)] + [jax.ShapeDtypeStruct(s, F32) for s in acc_shapes],
        scratch_shapes=[pltpu.VMEM((POOL_HALO + t, POOL_WIDTH), F32),
                        pltpu.VMEM((2 * t, CONV_WIDTH), F32), pltpu.VMEM((t + POOL_HALO, POOL_WIDTH), F32),
                        pltpu.VMEM((t, CONV_WIDTH), F32), pltpu.VMEM((POOL_HALO, POOL_WIDTH), F32),
                        pltpu.VMEM((SGU_HEADS, t, t), BF16), pltpu.VMEM((SGU_HEADS, t, t), F32),
                        pltpu.VMEM((t, SGU_WIDTH), F32)],
        operands=(dx1, x, o, z, z, conv, *consts),
        name=name, job=job)


MOD_SHARD = 6 * D_MODEL // N_DEV

ROW_DMOD = 0
ROW_G1024 = 8
ROW_V384 = 16
ROW_SGU_B = 24
ROW_POOL_SCALE = 25
ROW_LOSS = 26
ROW_CONV_W = 32
ROW_FFN_CONV = 64
ROW_POOL_W = 96
ROW_SGU_W = 112
ROWS_PER_LAYER = 208
N_LAYERS = 2


def _gather_weights(c8, mod_w, mod_b8, job):
    kinds = [kind for kind, _ in job]
    shards = [a for _, a in job]
    n = len(shards)

    def body(c_ref, modw_ref, modb_ref, *rest):
        shard_refs = rest[:n]
        sc_all_ref, modrows_ref = rest[n], rest[n + 1]
        full_refs = rest[n + 2:2 * n + 2]
        send_buf, mod_recv, w_send, w_recv, w_local, sc_send, sc_recv, mod_send, mod_recv_sem = rest[2 * n + 2:]
        pos = _my_pos()
        me = _flat(pos)
        peers = [_peer(pos, k) for k in range(1, N_DEV)]

        w_copies = _exchange_copies(kinds, shard_refs, full_refs, w_send, w_recv, w_local, phase=0)
        for cp in w_copies:
            cp.start()

        cv = c_ref[...]
        sc_all_ref[me] = cv * jax.nn.sigmoid(cv)
        sc_copies = [_remote_copy(sc_all_ref.at[me], sc_all_ref.at[me], sc_send.at[k], sc_recv.at[k], peers[k])
                     for k in range(N_PEERS)]
        for cp in sc_copies:
            cp.start()
        for cp in sc_copies:
            cp.wait()

        sc = jnp.concatenate([sc_all_ref[j, 0:1, :] for j in range(N_DEV)], axis=0)
        send_buf[...] = jnp.zeros_like(send_buf)
        for l in range(N_LAYERS):
            part = jnp.dot(sc, modw_ref[l], precision=lax.Precision.HIGHEST, preferred_element_type=F32)
            for j in range(N_DEV):
                send_buf[j, l:l + 1, :] = part[j:j + 1, :]
        mod_recv[me] = send_buf[me]
        mod_copies = [_remote_copy(send_buf.at[_flat(peers[k])], mod_recv.at[me], mod_send.at[k], mod_recv_sem.at[k],
                                   peers[k]) for k in range(N_PEERS)]
        for cp in mod_copies:
            cp.start()
        for cp in mod_copies:
            cp.wait()
        modrows_ref[...] = jnp.zeros_like(modrows_ref)
        for l in range(N_LAYERS):
            row = jnp.concatenate([mod_recv[j, l:l + 1, :] for j in range(N_DEV)], axis=1)
            modrows_ref[l:l + 1, :] = row + modb_ref[l:l + 1, :]

        for cp in w_copies:
            cp.wait()
        relays = _exchange_copies(kinds, shard_refs, full_refs, w_send, w_recv, w_local, phase=1)
        for cp in relays:
            cp.start()
        for cp in relays:
            cp.wait()

    out_shape = ([jax.ShapeDtypeStruct((N_DEV, 8, D_MODEL), F32), jax.ShapeDtypeStruct((8, 6 * D_MODEL), F32)]
                 + [jax.ShapeDtypeStruct((N_DEV,) + s.shape, s.dtype) for s in shards])
    return pl.pallas_call(
        body,
        in_specs=[VMEM, VMEM, VMEM] + [ANY] * n,
        out_specs=[VMEM, VMEM] + [ANY] * n,
        out_shape=out_shape,
        scratch_shapes=[pltpu.VMEM((N_DEV, 8, MOD_SHARD), F32), pltpu.VMEM((N_DEV, 8, MOD_SHARD), F32),
                        pltpu.SemaphoreType.DMA((n, N_PEERS)), pltpu.SemaphoreType.DMA((n, N_PEERS)),
                        pltpu.SemaphoreType.DMA((n,)),
                        pltpu.SemaphoreType.DMA((N_PEERS,)), pltpu.SemaphoreType.DMA((N_PEERS,)),
                        pltpu.SemaphoreType.DMA((N_PEERS,)), pltpu.SemaphoreType.DMA((N_PEERS,))],
        compiler_params=pltpu.CompilerParams(vmem_limit_bytes=VMEM_LIMIT_BYTES),
        name="gather_weights",
    )(c8, mod_w, mod_b8, *shards)


def _small_sums(sc_all, small_all0, small_all1, job):
    kinds = [kind for kind, _ in job]
    n = len(job)

    def body(sc_all_ref, small_all0_ref, small_all1_ref, *rest):
        small_sum_ref, gmodw_ref = rest[n], rest[n + 1]
        copies = _exchange_copies(kinds, rest[:n], rest[n + 2:2 * n + 2], *rest[2 * n + 2:], phase=0)
        for cp in copies:
            cp.start()
        me = _flat(_my_pos())
        sc = jnp.concatenate([sc_all_ref[j, 0:1, :] for j in range(N_DEV)], axis=0)
        mine = lax.broadcasted_iota(jnp.int32, (2 * N_DEV, MOD_SHARD), 0) == me
        for l, parts in enumerate((small_all0_ref, small_all1_ref)):
            total = parts[0].astype(F32)
            for j in range(1, N_DEV):
                total = total + parts[j].astype(F32)
            small_sum_ref[l] = total
            dm = jnp.concatenate(
                [jnp.sum(jnp.where(mine, parts[j, ROW_DMOD:ROW_DMOD + 2 * N_DEV, 0:MOD_SHARD].astype(F32), 0.0),
                         axis=0, keepdims=True) for j in range(N_DEV)], axis=0)
            gmodw_ref[l] = lax.dot_general(sc, dm, (((0,), (0,)), ((), ())), precision=lax.Precision.HIGHEST,
                                           preferred_element_type=F32)
        for cp in copies:
            cp.wait()

    res = pl.pallas_call(
        body,
        in_specs=[VMEM, VMEM, VMEM] + [ANY] * n,
        out_specs=[VMEM, VMEM] + [ANY] * n,
        out_shape=[jax.ShapeDtypeStruct((N_LAYERS, ROWS_PER_LAYER, D_MODEL), F32),
                   jax.ShapeDtypeStruct((N_LAYERS, D_MODEL, MOD_SHARD), F32)] + _exchange_out_shapes(job),
        scratch_shapes=_exchange_sems(n),
        compiler_params=pltpu.CompilerParams(vmem_limit_bytes=VMEM_LIMIT_BYTES),
        name="small_sums",
    )(sc_all, small_all0, small_all1, *[a for _, a in job])
    return res[0], res[1], res[2:]


def _adam_update(g, w, m, v):
    m2 = ADAM_B1 * m + (1.0 - ADAM_B1) * g
    v2 = ADAM_B2 * v + (1.0 - ADAM_B2) * (g * g)
    m_hat = m2 / (1.0 - ADAM_B1 ** ADAM_STEP)
    v_hat = v2 / (1.0 - ADAM_B2 ** ADAM_STEP)
    delta = -ADAM_LR * (m_hat / (jnp.sqrt(v_hat) + ADAM_EPS) + ADAM_WD * w)
    return delta, m2, v2


def _pair_add(g, r1, row_chunk, name):
    _, rows, cols = g.shape
    core = lax.axis_index("c").astype(jnp.int32).reshape(1)

    def body(core_ref, g_ref, r_ref, o_ref):
        o_ref[0] = (g_ref[0, 0].astype(F32) + r_ref[0].astype(F32)).astype(BF16)

    blk = pl.BlockSpec((1, row_chunk, cols), lambda q, i, core_ref: (q, i, 0))
    grid_spec = pltpu.PrefetchScalarGridSpec(
        num_scalar_prefetch=1, grid=(N_CHIPS, rows // row_chunk),
        in_specs=[pl.BlockSpec((1, 1, row_chunk, cols), lambda q, i, core_ref: (q, core_ref[0], i, 0)), blk],
        out_specs=blk)
    return pl.pallas_call(
        body, grid_spec=grid_spec, out_shape=jax.ShapeDtypeStruct((N_CHIPS, rows, cols), BF16),
        compiler_params=_cparams(2), name=name,
    )(core, g.reshape(N_CHIPS, 2, rows, cols), r1)


def _adam_sharded(recv0, recv1, w, m, v, row_chunk, name):
    _, rows, cols = w.shape
    n_chunks = rows // row_chunk

    def body(r0_ref, r1_ref, w_ref, m_ref, v_ref, g_ref, d_ref, m2_ref, v2_ref):
        layer = pl.program_id(0)

        def run(r_ref):
            g = r_ref[0].astype(F32)
            for j in range(1, r_ref.shape[0]):
                g = g + r_ref[j].astype(F32)
            delta, m2, v2 = _adam_update(g, w_ref[0], m_ref[0], v_ref[0])
            g_ref[0], d_ref[0], m2_ref[0], v2_ref[0] = g, delta, m2, v2

        @pl.when(layer == 0)
        def _():
            run(r0_ref)

        @pl.when(layer == 1)
        def _():
            run(r1_ref)

    r0_spec = pl.BlockSpec((recv0.shape[0], row_chunk, cols), lambda l, i: (0, i * (1 - l) + (n_chunks - 1) * l, 0))
    r1_spec = pl.BlockSpec((recv1.shape[0], row_chunk, cols), lambda l, i: (0, i * l, 0))
    blk = pl.BlockSpec((1, row_chunk, cols), lambda l, i: (l, i, 0))
    out = jax.ShapeDtypeStruct(w.shape, F32)
    return pl.pallas_call(
        body,
        grid=(N_LAYERS, n_chunks),
        in_specs=[r0_spec, r1_spec, blk, blk, blk],
        out_specs=[blk] * 4,
        out_shape=[out] * 4,
        compiler_params=_cparams(2),
        name=name,
    )(recv0, recv1, w, m, v)


def _adam_dense(g, w, m, v, row_chunk, name):
    n_lead, rows, cols = w.shape

    def body(g_ref, w_ref, m_ref, v_ref, go_ref, d_ref, m2_ref, v2_ref):
        gv = g_ref[...]
        go_ref[...] = gv
        d_ref[...], m2_ref[...], v2_ref[...] = _adam_update(gv, w_ref[...], m_ref[...], v_ref[...])

    blk = pl.BlockSpec((1, row_chunk, cols), lambda l, i: (l, i, 0))
    out = jax.ShapeDtypeStruct(w.shape, F32)
    return pl.pallas_call(
        body,
        grid=(n_lead, rows // row_chunk),
        in_specs=[blk] * 4,
        out_specs=[blk] * 4,
        out_shape=[out] * 4,
        compiler_params=_cparams(2),
        name=name,
    )(g, w, m, v)


WEIGHT_NAMES = ("mod_w", "mod_b", "mix_pre_g", "mix_post_g", "w_in", "sgu_norm_g", "sgu_norm_b", "sgu_w", "sgu_b",
                "conv_w", "conv_b", "conv_norm_g", "conv_norm_b", "pool_w", "pool_scale", "branch_g", "w_out",
                "ffn_pre_g", "ffn_post_g", "ffn_up", "ffn_conv_w", "ffn_conv_b", "ffn_down")
SHARDED_BIG = ("w_in", "w_out", "ffn_up", "ffn_down")
SMALL_PACKED = tuple(n for n in WEIGHT_NAMES if n not in SHARDED_BIG + ("mod_w",))


def _rows8(rows, width=D_MODEL):
    out = [jnp.pad(r.astype(F32), (0, width - r.shape[0]))[None] for r in rows]
    out.append(jnp.zeros((8 - len(rows), width), F32))
    return jnp.concatenate(out, axis=0)


def _as_rows(a, width=D_MODEL):
    flat = a.astype(F32).reshape(-1)
    pad = (-flat.shape[0]) % width
    return jnp.pad(flat, (0, pad)).reshape(-1, width)


def _pad_cols(a, width=D_MODEL):
    return jnp.pad(a.astype(F32), ((0, 0), (0, width - a.shape[1])))


def _pack_rows(arrays):
    rows = jnp.concatenate([_as_rows(a) for a in arrays], axis=0)
    return jnp.pad(rows, ((0, (-rows.shape[0]) % 8), (0, 0)))


def _unpack_rows(packed, shapes):
    out, r = [], 0
    for shape in shapes:
        size = math.prod(shape)
        n_rows = -(-size // D_MODEL)
        out.append(packed[r:r + n_rows].reshape(-1)[:size].reshape(shape))
        r += n_rows
    return out


TILE_VREGS = MIX_TILE // 8
CHUNK_SUBLANES = CHUNK // TILE_VREGS
CHUNKS_PER_TILE = MIX_TILE // CHUNK


def _chunk_axis_to_tile(a, axis):
    shape = a.shape
    a = a.reshape(shape[:axis] + (CHUNK_SUBLANES, TILE_VREGS) + shape[axis + 1:])
    a = jnp.swapaxes(a, axis, axis + 1)
    a = jnp.tile(a, (1,) * (axis + 1) + (CHUNKS_PER_TILE,) + (1,) * (len(shape) - axis - 1))
    return a.reshape(shape[:axis] + (MIX_TILE,) + shape[axis + 1:])


def _layer_consts(l, w, mod_rows, win, wout, conv_w_full):
    modv = _rows8(list(mod_rows[l].reshape(6, D_MODEL)))
    g1024 = _rows8([w["mix_pre_g"][l], w["mix_post_g"][l], w["branch_g"][l], w["ffn_pre_g"][l], w["ffn_post_g"][l]])
    p384 = _rows8([w["sgu_norm_g"][l], w["sgu_norm_b"][l], w["conv_b"][l], w["conv_norm_g"][l], w["conv_norm_b"][l]],
                  SGU_WIDTH)
    cw = jnp.pad(conv_w_full[l], ((0, 32 - CONV_K), (0, 0)))
    sguw = _chunk_axis_to_tile(_chunk_axis_to_tile(w["sgu_w"][l], 2), 1)
    bmat = _chunk_axis_to_tile(jnp.repeat(w["sgu_b"][l].T, HEAD_DIM, axis=1), 0)
    groups = len(POOL_WINDOWS)
    eye = jnp.eye(groups, dtype=F32)
    pwbd = (eye[:, None, :, None] * w["pool_w"][l][:, :, None, :]).reshape(POOL_WIDTH, POOL_WIDTH).astype(BF16)
    psc = _rows8([w["pool_scale"][l]], POOL_WIDTH)
    seg = jnp.arange(SGU_WIDTH) // HEAD_DIM
    segp = jnp.where(seg[:, None] == seg[None, :], 1.0 / HEAD_DIM, 0.0).astype(BF16)
    return modv, g1024, (modv, g1024, p384, cw, sguw, bmat, pwbd, psc, segp, win, wout)


def _small_grad_rows(mix, ffn, loss_pieces=None):
    _, _, _, _, _, mvec, v384, dcw, dsguw, dbmat, dpw, dpsc = mix
    fvec, cgrad = ffn[5], ffn[6]
    dmod = jnp.stack([mvec[2], mvec[3], mvec[1], fvec[2], fvec[3], fvec[1]]).reshape(N_DEV, MOD_SHARD)
    g_rows = jnp.stack([mvec[4], mvec[0], mvec[5], fvec[4], fvec[0]])
    dsgu_b = dbmat[:, ::HEAD_DIM].T.reshape(1, SGU_HEADS * CHUNK)
    groups = len(POOL_WINDOWS)
    gdim = POOL_WIDTH // groups
    dpw4 = dpw.reshape(groups, gdim, groups, gdim)
    dpool = jnp.stack([dpw4[g, :, g, :] for g in range(groups)])
    misc = [dsgu_b[0], dpsc[0]] + ([] if loss_pieces is None else [loss_pieces])
    blocks = [_pad_cols(dmod), _rows8(list(g_rows)), _pad_cols(v384), _rows8(misc), _pad_cols(dcw),
              _pad_cols(cgrad[:, 0:4, :].reshape(4 * N_DEV, FF_SHARD)), _as_rows(dpool), _as_rows(dsguw)]
    return jnp.concatenate(blocks, axis=0)


def _small_grads_from_rows(total):
    per = {n: [] for n in SMALL_PACKED}
    for l in range(N_LAYERS):
        s = total[l]
        per["mod_b"].append(s[ROW_DMOD:ROW_DMOD + N_DEV, :MOD_SHARD].reshape(6 * D_MODEL))
        for j, name in enumerate(("mix_pre_g", "mix_post_g", "branch_g", "ffn_pre_g", "ffn_post_g")):
            per[name].append(s[ROW_G1024 + j])
        for j, name in enumerate(("sgu_norm_g", "sgu_norm_b", "conv_b", "conv_norm_g", "conv_norm_b")):
            per[name].append(s[ROW_V384 + j, :SGU_WIDTH])
        per["sgu_b"].append(s[ROW_SGU_B, :SGU_HEADS * CHUNK].reshape(SGU_HEADS, CHUNK))
        per["pool_scale"].append(s[ROW_POOL_SCALE, :POOL_WIDTH])
        per["conv_w"].append(s[ROW_CONV_W:ROW_CONV_W + CONV_K, :CONV_WIDTH])
        fc = s[ROW_FFN_CONV:ROW_FFN_CONV + 4 * N_DEV, :FF_SHARD].reshape(N_DEV, 4, FF_SHARD)
        per["ffn_conv_w"].append(fc[:, 0:3, :].transpose(1, 0, 2).reshape(FFN_CONV_K, 2 * D_FF))
        per["ffn_conv_b"].append(fc[:, 3, :].reshape(2 * D_FF))
        per["pool_w"].append(s[ROW_POOL_W:ROW_POOL_W + 16].reshape(len(POOL_WINDOWS), HEAD_DIM, HEAD_DIM))
        per["sgu_w"].append(s[ROW_SGU_W:ROW_SGU_W + 96].reshape(SGU_HEADS, CHUNK, CHUNK))
    return {n: jnp.stack(v) for n, v in per.items()}


def kernel(x, c, mod_w, mod_b, mix_pre_g, mix_post_g, w_in, sgu_norm_g, sgu_norm_b, sgu_w, sgu_b, conv_w, conv_b, conv_norm_g, conv_norm_b, pool_w, pool_scale, branch_g, w_out, ffn_pre_g, ffn_post_g, ffn_up, ffn_conv_w, ffn_conv_b, ffn_down, loss_target, m_mod_w, m_mod_b, m_mix_pre_g, m_mix_post_g, m_w_in, m_sgu_norm_g, m_sgu_norm_b, m_sgu_w, m_sgu_b, m_conv_w, m_conv_b, m_conv_norm_g, m_conv_norm_b, m_pool_w, m_pool_scale, m_branch_g, m_w_out, m_ffn_pre_g, m_ffn_post_g, m_ffn_up, m_ffn_conv_w, m_ffn_conv_b, m_ffn_down, v_mod_w, v_mod_b, v_mix_pre_g, v_mix_post_g, v_w_in, v_sgu_norm_g, v_sgu_norm_b, v_sgu_w, v_sgu_b, v_conv_w, v_conv_b, v_conv_norm_g, v_conv_norm_b, v_pool_w, v_pool_scale, v_branch_g, v_w_out, v_ffn_pre_g, v_ffn_post_g, v_ffn_up, v_ffn_conv_w, v_ffn_conv_b, v_ffn_down):
    w = dict(zip(WEIGHT_NAMES, (mod_w, mod_b, mix_pre_g, mix_post_g, w_in, sgu_norm_g, sgu_norm_b, sgu_w, sgu_b, conv_w,
                                conv_b, conv_norm_g, conv_norm_b, pool_w, pool_scale, branch_g, w_out, ffn_pre_g,
                                ffn_post_g, ffn_up, ffn_conv_w, ffn_conv_b, ffn_down)))
    m = dict(zip(WEIGHT_NAMES, (m_mod_w, m_mod_b, m_mix_pre_g, m_mix_post_g, m_w_in, m_sgu_norm_g, m_sgu_norm_b, m_sgu_w,
                                m_sgu_b, m_conv_w, m_conv_b, m_conv_norm_g, m_conv_norm_b, m_pool_w, m_pool_scale,
                                m_branch_g, m_w_out, m_ffn_pre_g, m_ffn_post_g, m_ffn_up, m_ffn_conv_w, m_ffn_conv_b,
                                m_ffn_down)))
    v = dict(zip(WEIGHT_NAMES, (v_mod_w, v_mod_b, v_mix_pre_g, v_mix_post_g, v_w_in, v_sgu_norm_g, v_sgu_norm_b, v_sgu_w,
                                v_sgu_b, v_conv_w, v_conv_b, v_conv_norm_g, v_conv_norm_b, v_pool_w, v_pool_scale,
                                v_branch_g, v_w_out, v_ffn_pre_g, v_ffn_post_g, v_ffn_up, v_ffn_conv_w, v_ffn_conv_b,
                                v_ffn_down)))
    me = _flat(_my_pos())
    xs = x[0]

    transposed = ("w_in", "ffn_up")
    wt = {n: jnp.swapaxes(w[n], 1, 2) if n in transposed else w[n] for n in SHARDED_BIG}
    mt = {n: jnp.swapaxes(m[n], 1, 2) if n in transposed else m[n] for n in SHARDED_BIG}
    vt = {n: jnp.swapaxes(v[n], 1, 2) if n in transposed else v[n] for n in SHARDED_BIG}
    bf16_shards = [[wt[n][l].astype(BF16) for n in SHARDED_BIG] for l in range(N_LAYERS)]

    def mixer_operands(l, win_g, wout_g):
        win = win_g.reshape(IN_WIDTH, D_MODEL)
        return _layer_consts(l, w, mod_rows, win, wout_g.reshape(D_MODEL, D_MODEL), conv_w_full)

    def ffn_operands(l, modv, g1024, wup_g, wdn_g):
        wdn = wdn_g.reshape(FF_PAIRS, FF_SHARD, D_MODEL)
        return modv, g1024, wup_g, wdn, ffn_cw_full[:, l], ffn_conv_b[l].reshape(N_DEV, 1, FF_SHARD)

    w0, w1 = bf16_shards
    c8 = jnp.broadcast_to(c, (8, D_MODEL))
    mod_b8 = jnp.pad(mod_b, ((0, 8 - N_LAYERS), (0, 0)))
    sc_all, mod_rows, win0_g, wout0_g, conv_w_g, ffn_cw_full = _gather_weights(
        c8, mod_w, mod_b8, [("gather2", w0[0]), ("gather2", w0[1]), ("gather", conv_w), ("gather", ffn_conv_w)])
    conv_w_full = conv_w_g.transpose(1, 2, 0, 3).reshape(N_LAYERS, CONV_K, CONV_WIDTH)

    modv0, g0, mix_consts0 = mixer_operands(0, win0_g, wout0_g)
    (x1, z, o, cv), (wup0_g, wdn0_g) = _mixer_fwd(xs, *mix_consts0, name="mixer_fwd_l0", natural_x=True,
                                                  job=[("gather2", w0[2]), ("gather2", w0[3])],
                                                  relay_steps_before_end=1)
    ffn_consts0 = ffn_operands(0, modv0, g0, wup0_g, wdn0_g)
    (x2, y2, p, u), (win1_g, wout1_g, wdn1_g) = _ffn_fwd(
        x1, *ffn_consts0, name="ffn_fwd_l0", job=[("gather2", w1[0]), ("gather2", w1[1]), ("gather2", w1[3])])
    saved = [(xs, z, o, cv, x1, y2, p, u)]
    modv1, g1, mix_consts1 = mixer_operands(1, win1_g, wout1_g)
    (x1, z, o, cv), (wup1_g,) = _mixer_fwd(x2, *mix_consts1, name="mixer_fwd_l1", job=[("gather2", w1[2])])
    ffn_consts1 = ffn_operands(1, modv1, g1, wup1_g, wdn1_g)
    (dh, y2, p, u, loss_tile), _ = _ffn_fwd(x1, *ffn_consts1, name="ffn_fwd_l1", loss_target=loss_target[0])
    saved.append((x2, z, o, cv, x1, y2, p, u))
    loss_local = jnp.sum(loss_tile)
    loss_hi = loss_local.astype(BF16).astype(F32)
    loss_mid = (loss_local - loss_hi).astype(BF16).astype(F32)
    loss_pieces = jnp.stack([loss_hi, loss_mid, loss_local - loss_hi - loss_mid])

    def ffn_weight_grads(l, ffn):
        dp, a, dy2, h2 = ffn[1:5]
        d_up = _wgrad(dp, h2, f"wgrad_ffn_up_l{l}", tk=WGRAD_TK_FFN)
        d_dn = _wgrad(a, dy2, f"wgrad_ffn_down_l{l}", tk=WGRAD_TK_FFN).reshape(N_DEV, D_FF // N_DEV, D_MODEL)
        return d_up, d_dn

    def mixer_weight_grads(l, mix):
        dz, do, ycat, h1 = mix[1:5]
        d_in = _wgrad(dz[None], h1, f"wgrad_w_in_l{l}").reshape(N_DEV, IN_WIDTH // N_DEV, D_MODEL)
        d_out = _wgrad(ycat, do[None], f"wgrad_w_out_l{l}").reshape(N_DEV, D_MODEL // N_DEV, D_MODEL)
        return d_in, d_out

    x_in, z, o, cv, x1, y2, p, u = saved[1]
    ffn1, _ = _ffn_bwd(dh, x1, y2, p, u, *ffn_consts1[:-1], name="ffn_bwd_l1")
    d_up1, d_dn1 = ffn_weight_grads(1, ffn1)
    mix1, (sib_up1, sib_dn1) = _mixer_bwd(ffn1[0], x_in, o, z, cv, *mix_consts1, name="mixer_bwd_l1",
                                          job=[("scatter_p1", d_up1), ("scatter_p1", d_dn1)])
    chip_up1 = _pair_add(d_up1, sib_up1, 176, "pair_add_ffn_up_l1")
    chip_dn1 = _pair_add(d_dn1, sib_dn1, 176, "pair_add_ffn_down_l1")
    d_in1, d_out1 = mixer_weight_grads(1, mix1)
    small1 = _small_grad_rows(mix1, ffn1).astype(BF16)

    x_in, z, o, cv, x1, y2, p, u = saved[0]
    ffn0, job_out = _ffn_bwd(mix1[0], x1, y2, p, u, *ffn_consts0[:-1], name="ffn_bwd_l0",
                             job=[("scatter", d_in1), ("scatter", d_out1), ("scatter_p2", chip_up1),
                                  ("scatter_p2", chip_dn1)])
    recv1 = job_out
    dp, a, dy2, h2 = ffn0[1:5]
    d_dn0, (small_all1,) = _wgrad(a, dy2, "wgrad_ffn_down_l0", tk=WGRAD_TK_FFN, job=[("gather2", small1)])
    d_dn0 = d_dn0.reshape(N_DEV, D_FF // N_DEV, D_MODEL)
    d_up0, (recv_dn0,) = _wgrad(dp, h2, "wgrad_ffn_up_l0", tk=WGRAD_TK_FFN, job=[("scatter", d_dn0)])
    mix0, (recv_up0,) = _mixer_bwd(ffn0[0], x_in, o, z, cv, *mix_consts0, name="mixer_bwd_l0", natural_x=True,
                                   job=[("scatter", d_up0)])
    recv_ffn0 = (recv_up0, recv_dn0)
    grad_x = mix0[0][None]
    dz, do, ycat, h1 = mix0[1:5]
    small0 = _small_grad_rows(mix0, ffn0, loss_pieces).astype(BF16)
    d_in0, (small_all0,) = _wgrad(dz[None], h1, "wgrad_w_in_l0", tk=WGRAD_TK // 2, job=[("gather2", small0)])
    d_in0 = d_in0.reshape(N_DEV, IN_WIDTH // N_DEV, D_MODEL)
    d_out0, (recv_in0,) = _wgrad(ycat, do[None], "wgrad_w_out_l0", tk=WGRAD_TK // 2, job=[("scatter", d_in0)])
    d_out0 = d_out0.reshape(N_DEV, D_MODEL // N_DEV, D_MODEL)
    small_total, g_mod_w, (recv_out0,) = _small_sums(sc_all, small_all0, small_all1, [("scatter", d_out0)])
    recv0 = [recv_in0, recv_out0, recv_ffn0[0], recv_ffn0[1]]
    loss = small_total[0, ROW_LOSS, 0] + small_total[0, ROW_LOSS, 1] + small_total[0, ROW_LOSS, 2]

    grads, deltas, new_m, new_v = {}, {}, {}, {}
    for j, (name, chunk) in enumerate((("w_in", 224), ("w_out", 128), ("ffn_up", 176), ("ffn_down", 176))):
        outs = _adam_sharded(recv0[j], recv1[j], wt[name], mt[name], vt[name], chunk, "adam_" + name)
        if name in transposed:
            outs = [jnp.swapaxes(t, 1, 2) for t in outs]
        grads[name], deltas[name], new_m[name], new_v[name] = outs
    grads["mod_w"], deltas["mod_w"], new_m["mod_w"], new_v["mod_w"] = _adam_dense(
        g_mod_w, mod_w, m_mod_w, v_mod_w, 256, "adam_mod_w")

    small_g = _small_grads_from_rows(small_total)
    small_g["conv_w"] = lax.dynamic_slice_in_dim(small_g["conv_w"], me * conv_w.shape[2], conv_w.shape[2], axis=2)
    small_g["ffn_conv_w"] = lax.dynamic_slice_in_dim(small_g["ffn_conv_w"], me * FF_SHARD, FF_SHARD, axis=2)
    shapes = [w[n].shape for n in SMALL_PACKED]
    packs = [_pack_rows([src[n] for n in SMALL_PACKED])[None] for src in (small_g, w, m, v)]
    _, d, m2, v2 = _adam_dense(*packs, packs[0].shape[1], "adam_small")
    for name, dd, mm, vv in zip(SMALL_PACKED, _unpack_rows(d[0], shapes), _unpack_rows(m2[0], shapes),
                                _unpack_rows(v2[0], shapes)):
        grads[name], deltas[name], new_m[name], new_v[name] = small_g[name], dd, mm, vv

    return (loss, grad_x, *[grads[n] for n in WEIGHT_NAMES], *[deltas[n] for n in WEIGHT_NAMES],
            *[new_m[n] for n in WEIGHT_NAMES], *[new_v[n] for n in WEIGHT_NAMES])
```

```python
import math

import jax
import jax.numpy as jnp
from jax import lax
from jax.experimental import pallas as pl
from jax.experimental.pallas import tpu as pltpu

F32 = jnp.float32
BF16 = jnp.bfloat16

D_MODEL = 1024
N_DEV = 8
SGU_WIDTH = 384
CONV_WIDTH = 384
POOL_WIDTH = 256
HEAD_DIM = 64
SGU_HEADS = 6
CHUNK = 128
CONV_K = 31
POOL_WINDOWS = (2, 4, 8, 16)
IN_WIDTH = 1792
D_FF = 2816
FF_SHARD = 2 * D_FF // N_DEV
FF_PAIRS = N_DEV // 2
FFN_CONV_K = 3
EPS = 1e-6
GELU_C0 = math.sqrt(2.0 / math.pi)
GELU_C1 = 0.044715

ADAM_LR = 0.001
ADAM_B1 = 0.9
ADAM_B2 = 0.999
ADAM_EPS = 1e-08
ADAM_WD = 0.01
ADAM_STEP = 10

VMEM_LIMIT_BYTES = 60 * 1024 * 1024
TILE = 256
MIX_TILE = TILE
FFN_TILE = TILE
FFN_HALO = 8 * (FFN_CONV_K - 1)
POOL_HALO = 8 * POOL_WINDOWS[-1]
WGRAD_TK = 2048
WGRAD_TK_FFN = 4096


def _cparams(n_axes):
    return pltpu.CompilerParams(dimension_semantics=("arbitrary",) * n_axes, vmem_limit_bytes=VMEM_LIMIT_BYTES)


def _whole(shape):
    nd = len(shape)
    return pl.BlockSpec(shape, lambda *_: (0,) * nd, pipeline_mode=pl.Buffered(1))


def _dot(a, b):
    return jnp.dot(a, b, preferred_element_type=F32)


def _dot_nt(a, b):
    return lax.dot_general(a, b, (((1,), (1,)), ((), ())), preferred_element_type=F32)


def _dot_tn(a, b):
    return lax.dot_general(a, b, (((0,), (0,)), ((), ())), preferred_element_type=F32)


def _gelu(x):
    t = jnp.tanh(GELU_C0 * (x + GELU_C1 * x * x * x))
    return 0.5 * x * (1.0 + t), t


def _gelu_grad(x, t):
    return 0.5 * (1.0 + t) + 0.5 * x * (1.0 - t * t) * (GELU_C0 * (1.0 + 3.0 * GELU_C1 * x * x))


def _rowmean(x):
    return jnp.mean(x, axis=-1, keepdims=True)


def _colsum(x):
    return jnp.sum(x, axis=0, keepdims=True)


def _rms_fwd(x):
    r = lax.rsqrt(_rowmean(x * x) + EPS)
    return x * r, r


def _rms_bwd(dxhat, xhat, r):
    return r * (dxhat - xhat * _rowmean(dxhat * xhat))


N_PEERS = N_DEV - 1
ANY = pl.BlockSpec(memory_space=pl.ANY)
VMEM = pl.BlockSpec(memory_space=pltpu.VMEM)


def _my_pos():
    return lax.axis_index("x"), lax.axis_index("y"), lax.axis_index("c")


def _peer(pos, k):
    x, y, c = pos
    return (1 - x if k & 4 else x, 1 - y if k & 2 else y, 1 - c if k & 1 else c)


def _flat(pos):
    return 4 * pos[0] + 2 * pos[1] + pos[2]


def _remote_copy(src, dst, send_sem, recv_sem, peer):
    return pltpu.make_async_remote_copy(src_ref=src, dst_ref=dst, send_sem=send_sem, recv_sem=recv_sem,
                                        device_id=peer, device_id_type=pl.DeviceIdType.MESH)


N_CHIPS = N_DEV // 2
SIBLING = 1
SAME_CORE_PEERS = (2, 4, 6)


def _exchange_out_shapes(job):
    def shape(kind, a):
        if kind in ("gather", "gather2"):
            return (N_DEV,) + a.shape
        if kind == "scatter_p1":
            return (N_CHIPS,) + a.shape[1:]
        return a.shape
    return [jax.ShapeDtypeStruct(shape(kind, a), a.dtype) for kind, a in job]


def _exchange_sems(n):
    return [pltpu.SemaphoreType.DMA((n, N_PEERS)), pltpu.SemaphoreType.DMA((n, N_PEERS)), pltpu.SemaphoreType.DMA((n,))]


def _exchange_copies(kinds, src_refs, dst_refs, send_sems, recv_sems, local_sems, phase):
    pos = _my_pos()
    me = _flat(pos)
    chip, core = 2 * pos[0] + pos[1], pos[2]
    copies = []

    def remote(a, src, dst, k, sem=None):
        sem = k - 1 if sem is None else sem
        copies.append(_remote_copy(src, dst, send_sems.at[a, sem], recv_sems.at[a, sem], _peer(pos, k)))

    for a, kind in enumerate(kinds):
        src, dst = src_refs[a], dst_refs[a]
        if phase == 1:
            if kind == "gather2":
                for k in SAME_CORE_PEERS:
                    remote(a, dst.at[me ^ k], dst.at[me ^ k], SIBLING, sem=k)
        elif kind in ("gather", "gather2"):
            copies.append(pltpu.make_async_copy(src, dst.at[me], local_sems.at[a]))
            for k in (range(1, N_DEV) if kind == "gather" else (SIBLING,) + SAME_CORE_PEERS):
                remote(a, src, dst.at[me], k)
        elif kind == "scatter":
            copies.append(pltpu.make_async_copy(src.at[me], dst.at[me], local_sems.at[a]))
            for k in range(1, N_DEV):
                remote(a, src.at[me ^ k], dst.at[me], k)
        elif kind == "scatter_p1":
            for q in range(N_CHIPS):
                remote(a, src.at[2 * q + 1 - core], dst.at[q], SIBLING, sem=q)
        elif kind == "scatter_p2":
            copies.append(pltpu.make_async_copy(src.at[chip], dst.at[chip], local_sems.at[a]))
            for k in SAME_CORE_PEERS:
                remote(a, src.at[chip ^ (k >> 1)], dst.at[chip], k)
    return copies


def _pallas_call_with_exchange(body, *, grid, in_specs, out_specs, out_shape, scratch_shapes, operands, name, job,
                               relay_steps_before_end=2):
    params = _cparams(len(grid))
    if not job:
        outs = pl.pallas_call(body, grid=grid, in_specs=in_specs, out_specs=out_specs, out_shape=out_shape,
                              scratch_shapes=scratch_shapes, compiler_params=params, name=name)(*operands)
        return outs, []
    kinds = [kind for kind, _ in job]
    relayed = [kind if kind == "gather2" else None for kind in kinds]
    unrelayed = [None if kind == "gather2" else kind for kind in kinds]
    n, n_in, n_out, n_scr = len(job), len(in_specs), len(out_specs), len(scratch_shapes)
    n_steps = math.prod(grid)
    relay_step = max(n_steps - relay_steps_before_end, 0)

    def wrapped(*refs):
        ins, jin = refs[:n_in], refs[n_in:n_in + n]
        outs, jout = refs[n_in + n:n_in + n + n_out], refs[n_in + n + n_out:n_in + 2 * n + n_out]
        scr = refs[n_in + 2 * n + n_out:n_in + 2 * n + n_out + n_scr]
        sems = refs[n_in + 2 * n + n_out + n_scr:]
        step = pl.program_id(0)
        for d in range(1, len(grid)):
            step = step * grid[d] + pl.program_id(d)

        def copies(which, phase):
            return _exchange_copies(which, jin, jout, *sems, phase=phase)

        @pl.when(step == 0)
        def _():
            for cp in copies(kinds, 0):
                cp.start()

        body(*ins, *outs, *scr)

        @pl.when(step == relay_step)
        def _():
            for cp in copies(relayed, 0):
                cp.wait()
            for cp in copies(relayed, 1):
                cp.start()

        @pl.when(step == n_steps - 1)
        def _():
            for cp in copies(unrelayed, 0) + copies(relayed, 1):
                cp.wait()

    res = pl.pallas_call(
        wrapped, grid=grid,
        in_specs=list(in_specs) + [ANY] * n,
        out_specs=list(out_specs) + [ANY] * n,
        out_shape=list(out_shape) + _exchange_out_shapes(job),
        scratch_shapes=list(scratch_shapes) + _exchange_sems(n),
        compiler_params=params, name=name,
    )(*operands, *[a for _, a in job])
    return res[:n_out], res[n_out:]


def _seg_mean(x, segp):
    hi = x.astype(BF16)
    lo = (x - hi.astype(F32)).astype(BF16)
    return _dot(hi, segp) + _dot(lo, segp)


def _rot_rows(x, shift):
    m, c = x.shape
    return pltpu.roll(x.reshape(m // 8, 8, c), shift, 1).reshape(m, c)


def _sublane_is(shape, s):
    return lax.broadcasted_iota(jnp.int32, shape, 0) % 8 == s


def _causal_tail(tail, prev_rot):
    rot = _rot_rows(tail, 1)
    return jnp.where(_sublane_is(tail.shape, 0), prev_rot, rot), rot


def _lookahead_head(head, next_rot):
    rot = _rot_rows(head, 7)
    return jnp.where(_sublane_is(head.shape, 7), next_rot, rot), rot


def _tile_token_index(t, tile_idx):
    r = lax.broadcasted_iota(jnp.int32, (t, 1), 0)
    return tile_idx * t + (r % 8) * (t // 8) + r // 8


def _interleave(x):
    t, c = x.shape
    return jnp.swapaxes(x.reshape(8, t // 8, c), 0, 1).reshape(t, c)


def _deinterleave(x):
    t, c = x.shape
    return jnp.swapaxes(x.reshape(t // 8, 8, c), 0, 1).reshape(t, c)


def _ffn_fwd(x1, modv, g1024, wup, wdn, cw, cb, name="ffn_fwd", job=None, loss_target=None):
    s_len = x1.shape[0]
    t = FFN_TILE
    n_tiles = s_len // t
    with_loss = loss_target is not None

    def body(x1_ref, *rest):
        if with_loss:
            tgt_ref, rest = rest[0], rest[1:]
            loss_ref, rest = rest[10], rest[:10] + rest[11:]
        mod_ref, g_ref, wup_ref, wdn_ref, cw_ref, cb_ref, x2_ref, y2_ref, p_ref, u_ref, ext_ref, carry_ref = rest
        i = pl.program_id(0)

        @pl.when(i == 0)
        def _():
            carry_ref[...] = jnp.zeros_like(carry_ref)
            if with_loss:
                loss_ref[...] = jnp.zeros_like(loss_ref)

        x1v = x1_ref[...]
        pre_g, post_g = g_ref[3:4, :], g_ref[4:5, :]
        sh2, sc2, g2 = mod_ref[3:4, :], mod_ref[4:5, :], mod_ref[5:6, :]
        xhat, _ = _rms_fwd(x1v)
        h2b = (xhat * pre_g * (1.0 + sc2) + sh2).astype(BF16)

        def conv_shard(s):
            p = _dot_nt(h2b, wup_ref[s])
            p_ref[s] = p.astype(BF16)
            ext_ref[0:FFN_HALO, :], carry_ref[s] = _causal_tail(p[t - FFN_HALO:t, :], carry_ref[s])
            ext_ref[FFN_HALO:FFN_HALO + t, :] = p
            w = cw_ref[s]
            u = w[0:1, :] * ext_ref[0:t, :] + w[1:2, :] * ext_ref[8:8 + t, :] + w[2:3, :] * p + cb_ref[s]
            u_ref[s] = u.astype(BF16)
            return u

        y2 = jnp.zeros((t, D_MODEL), F32)
        for j in range(FF_PAIRS):
            ug = conv_shard(j)
            uv = conv_shard(j + FF_PAIRS)
            ge, _ = _gelu(ug)
            y2 = y2 + _dot((ge * uv).astype(BF16), wdn_ref[j])
        y2_ref[...] = y2
        yhat, _ = _rms_fwd(y2)
        x2 = x1v + g2 * (yhat * post_g)
        if with_loss:
            diff = x2 - _interleave(tgt_ref[...])
            x2_ref[...] = diff * (1.0 / D_MODEL)
            sq = jnp.sum((diff * diff).reshape(t // 8, 8, D_MODEL), axis=0)
            loss_ref[...] += (0.5 / D_MODEL) * sum(sq[:, 128 * j:128 * (j + 1)] for j in range(D_MODEL // 128))
        else:
            x2_ref[...] = x2

    tile = pl.BlockSpec((t, D_MODEL), lambda i: (i, 0))
    consts = (modv, g1024, wup, wdn, cw, cb)
    shards = pl.BlockSpec((N_DEV, t, FF_SHARD), lambda i: (0, i, 0))
    out_specs = [tile, tile, shards, shards]
    out_shape = [jax.ShapeDtypeStruct((s_len, D_MODEL), F32), jax.ShapeDtypeStruct((s_len, D_MODEL), F32),
                 jax.ShapeDtypeStruct((N_DEV, s_len, FF_SHARD), BF16),
                 jax.ShapeDtypeStruct((N_DEV, s_len, FF_SHARD), BF16)]
    if with_loss:
        out_specs.append(pl.BlockSpec((8, 128), lambda i: (0, 0)))
        out_shape.append(jax.ShapeDtypeStruct((8, 128), F32))
    return _pallas_call_with_exchange(
        body,
        grid=(n_tiles,),
        in_specs=[tile] * (2 if with_loss else 1) + [_whole(c.shape) for c in consts],
        out_specs=out_specs,
        out_shape=out_shape,
        scratch_shapes=[pltpu.VMEM((FFN_HALO + t, FF_SHARD), F32), pltpu.VMEM((N_DEV, FFN_HALO, FF_SHARD), F32)],
        operands=(x1,) + ((loss_target,) if with_loss else ()) + consts,
        name=name, job=job)


def _ffn_bwd(dx2, x1, y2, p, u, modv, g1024, wup, wdn, cw, name="ffn_bwd", job=None):
    s_len = x1.shape[0]
    t = FFN_TILE
    n_tiles = s_len // t
    hb = FFN_HALO

    def body(dx2_ref, x1_ref, y2_ref, p_hbm, u_hbm, mod_ref, g_ref, wup_ref, wdn_ref, cw_ref,
             dx1_ref, dp_ref, a_ref, dy2_ref, h2_ref, vec_ref, cgrad_ref, dext_ref, dcarry_ref, p_ring, u_ring, ring_sem):
        i = pl.program_id(0)

        def fetch(k, slot):
            rows = pl.ds(pl.multiple_of((n_tiles - 1 - k) * t, t), t)
            return [pltpu.make_async_copy(p_hbm.at[:, rows, :], p_ring.at[slot], ring_sem.at[0, slot]),
                    pltpu.make_async_copy(u_hbm.at[:, rows, :], u_ring.at[slot], ring_sem.at[1, slot])]

        @pl.when(i == 0)
        def _():
            for cp in fetch(0, 0) + fetch(1, 1):
                cp.start()
            vec_ref[...] = jnp.zeros_like(vec_ref)
            cgrad_ref[...] = jnp.zeros_like(cgrad_ref)
            dcarry_ref[...] = jnp.zeros_like(dcarry_ref)

        @pl.when(i + 2 < n_tiles)
        def _():
            for cp in fetch(i + 2, lax.rem(i + 2, 3)):
                cp.start()

        slot = lax.rem(i, 3)
        for cp in fetch(i, slot):
            cp.wait()
        p_ref, u_ref = p_ring.at[slot], u_ring.at[slot]

        dx2v, x1v, y2v = dx2_ref[...], x1_ref[...], y2_ref[...]
        pre_g, post_g = g_ref[3:4, :], g_ref[4:5, :]
        sh2, sc2, g2 = mod_ref[3:4, :], mod_ref[4:5, :], mod_ref[5:6, :]

        yhat, ry = _rms_fwd(y2v)
        vec_ref[1:2, :] += _colsum(dx2v * (yhat * post_g))
        dyn = dx2v * g2
        vec_ref[0:1, :] += _colsum(dyn * yhat)
        dy2b = _rms_bwd(dyn * post_g, yhat, ry).astype(BF16)
        dy2_ref[...] = dy2b

        xhat, rx = _rms_fwd(x1v)
        xn = xhat * pre_g
        h2_ref[...] = (xn * (1.0 + sc2) + sh2).astype(BF16)

        def conv_bwd(s, du):
            w = cw_ref[s]
            pf = p_ref[s].astype(F32)
            dext_ref[0:t, :] = du
            dext_ref[t:t + hb, :], dcarry_ref[s] = _lookahead_head(du[0:hb, :], dcarry_ref[s])
            du1, du2 = dext_ref[8:8 + t, :], dext_ref[16:16 + t, :]
            cgrad_ref[s, 0:1, :] += _colsum(du2 * pf)
            cgrad_ref[s, 1:2, :] += _colsum(du1 * pf)
            cgrad_ref[s, 2:3, :] += _colsum(du * pf)
            cgrad_ref[s, 3:4, :] += _colsum(du)
            dpb = (w[2:3, :] * du + w[1:2, :] * du1 + w[0:1, :] * du2).astype(BF16)
            dp_ref[s] = dpb
            return _dot(dpb, wup_ref[s])

        dh2 = jnp.zeros((t, D_MODEL), F32)
        for j in range(FF_PAIRS):
            ug = u_ref[j].astype(F32)
            uv = u_ref[j + FF_PAIRS].astype(F32)
            ge, th = _gelu(ug)
            a_ref[j] = (ge * uv).astype(BF16)
            da = _dot_nt(dy2b, wdn_ref[j])
            dh2 = dh2 + conv_bwd(j, da * uv * _gelu_grad(ug, th))
            dh2 = dh2 + conv_bwd(j + FF_PAIRS, da * ge)

        vec_ref[2:3, :] += _colsum(dh2)
        vec_ref[3:4, :] += _colsum(dh2 * xn)
        dxn = dh2 * (1.0 + sc2)
        vec_ref[4:5, :] += _colsum(dxn * xhat)
        dx1_ref[...] = dx2v + _rms_bwd(dxn * pre_g, xhat, rx)

    rev = lambda i: (n_tiles - 1 - i, 0)
    tile = pl.BlockSpec((t, D_MODEL), rev)
    return _pallas_call_with_exchange(
        body,
        grid=(n_tiles,),
        in_specs=[tile, tile, tile, ANY, ANY,
                  _whole(modv.shape), _whole(g1024.shape), _whole(wup.shape), _whole(wdn.shape),
                  _whole(cw.shape)],
        out_specs=[tile,
                   pl.BlockSpec((N_DEV, t, FF_SHARD), lambda i: (0, n_tiles - 1 - i, 0)),
                   pl.BlockSpec((FF_PAIRS, t, FF_SHARD), lambda i: (0, n_tiles - 1 - i, 0)),
                   tile, tile,
                   pl.BlockSpec((8, D_MODEL), lambda i: (0, 0)),
                   pl.BlockSpec((N_DEV, 8, FF_SHARD), lambda i: (0, 0, 0))],
        out_shape=[jax.ShapeDtypeStruct((s_len, D_MODEL), F32),
                   jax.ShapeDtypeStruct((N_DEV, s_len, FF_SHARD), BF16),
                   jax.ShapeDtypeStruct((FF_PAIRS, s_len, FF_SHARD), BF16),
                   jax.ShapeDtypeStruct((s_len, D_MODEL), BF16),
                   jax.ShapeDtypeStruct((s_len, D_MODEL), BF16),
                   jax.ShapeDtypeStruct((8, D_MODEL), F32),
                   jax.ShapeDtypeStruct((N_DEV, 8, FF_SHARD), F32)],
        scratch_shapes=[pltpu.VMEM((t + hb, FF_SHARD), F32), pltpu.VMEM((N_DEV, hb, FF_SHARD), F32),
                        pltpu.VMEM((3, N_DEV, t, FF_SHARD), BF16), pltpu.VMEM((3, N_DEV, t, FF_SHARD), BF16),
                        pltpu.SemaphoreType.DMA((2, 3))],
        operands=(dx2, x1, y2, p, u, modv, g1024, wup, wdn, cw),
        name=name, job=job)


def _wgrad(a, b, name, tk=WGRAD_TK, job=None):
    a_grouped, b_grouped = a.ndim == 3, b.ndim == 3
    groups = a.shape[0] if a_grouped else b.shape[0]
    s_len, m, n = a.shape[-2], a.shape[-1], b.shape[-1]
    tk = min(tk, s_len)
    n_k = s_len // tk

    def body(a_ref, b_ref, o_ref, acc_ref):
        k = pl.program_id(1)
        av = a_ref[0] if a_grouped else a_ref[...]
        bv = b_ref[0] if b_grouped else b_ref[...]
        part = _dot_tn(av, bv)
        if n_k == 1:
            o_ref[0] = part.astype(BF16)
            return

        @pl.when(k == 0)
        def _():
            acc_ref[...] = part

        @pl.when(jnp.logical_and(k > 0, k < n_k - 1))
        def _():
            acc_ref[...] += part

        @pl.when(k == n_k - 1)
        def _():
            o_ref[0] = (acc_ref[...] + part).astype(BF16)

    a_spec = pl.BlockSpec((1, tk, m), lambda g, k: (g, k, 0)) if a_grouped else pl.BlockSpec((tk, m), lambda g, k: (k, 0))
    b_spec = pl.BlockSpec((1, tk, n), lambda g, k: (g, k, 0)) if b_grouped else pl.BlockSpec((tk, n), lambda g, k: (k, 0))
    (out,), exchanged = _pallas_call_with_exchange(
        body,
        grid=(groups, n_k),
        in_specs=[a_spec, b_spec],
        out_specs=[pl.BlockSpec((1, m, n), lambda g, k: (g, 0, 0))],
        out_shape=[jax.ShapeDtypeStruct((groups, m, n), BF16)],
        scratch_shapes=[pltpu.VMEM((m, n), F32)],
        operands=(a, b),
        name=name, job=job)
    return (out, exchanged) if job else out


def _lane(shape):
    return lax.broadcasted_iota(jnp.int32, shape, 1)


def _by_pool_group(shape, vals):
    lane = _lane(shape)
    return jnp.where(lane < 64, vals[0], jnp.where(lane < 128, vals[1], jnp.where(lane < 192, vals[2], vals[3])))


def _pool_inv_counts(t, tile_idx):
    pos1 = _tile_token_index(t, tile_idx) + 1
    return [1.0 / jnp.minimum(pos1, w).astype(F32) for w in POOL_WINDOWS]


def _sgu_keep_mask(t):
    tok_r = _tile_token_index(t, 0)
    c = lax.broadcasted_iota(jnp.int32, (1, t), 1)
    tok_c = (c % 8) * (t // 8) + c // 8
    return jnp.logical_and(tok_r // CHUNK == tok_c // CHUNK, tok_r >= tok_c)


def _masked_sgu_w(sguw_ref):
    keep = _sgu_keep_mask(sguw_ref.shape[1])
    return [jnp.where(keep, sguw_ref[h], 0.0).astype(BF16) for h in range(SGU_HEADS)]


def _branches_fwd(z, tile_idx, p384_ref, cw_ref, wm, bmat_ref, pwbd_ref, psc_ref, segp_ref, g_ref, hext_ref, zext_ref,
                  h_prev_rot, z_prev_rot, conv_saved=None):
    t = z.shape[0]
    segp = segp_ref[...]
    r = {}
    u, _ = _gelu(z[:, 0:SGU_WIDTH])
    vraw, _ = _gelu(z[:, SGU_WIDTH:2 * SGU_WIDTH])
    xc = vraw - _seg_mean(vraw, segp)
    rstd_v = lax.rsqrt(_seg_mean(xc * xc, segp) + EPS)
    xh_v = xc * rstd_v
    vnb = (xh_v * p384_ref[0:1, :] + p384_ref[1:2, :]).astype(BF16)
    first_head = _lane((t, 128)) < HEAD_DIM
    f_pairs = []
    for pr in range(SGU_HEADS // 2):
        vp = vnb[:, pr * 128:(pr + 1) * 128]
        f_pairs.append(jnp.where(first_head, _dot(wm[2 * pr], vp), _dot(wm[2 * pr + 1], vp)))
    f = jnp.concatenate(f_pairs, axis=1) + bmat_ref[...]
    ya = u * f
    r.update(u=u, xh_v=xh_v, rstd_v=rstd_v, vnb=vnb, f=f)
    o_b = 2 * SGU_WIDTH
    a_in = z[:, o_b:o_b + CONV_WIDTH]
    sig_g = jax.nn.sigmoid(z[:, o_b + CONV_WIDTH:o_b + 2 * CONV_WIDTH])
    hh = a_in * sig_g
    if conv_saved is None:
        hext_ref[0:t, :], r["h_rot"] = _causal_tail(hh, h_prev_rot)
        hext_ref[t:2 * t, :] = hh
        conv = jnp.zeros((t, CONV_WIDTH), F32) + p384_ref[2:3, :]
        for k in range(CONV_K):
            conv = conv + cw_ref[k:k + 1, :] * hext_ref[pl.ds(t - 8 * (CONV_K - 1 - k), t), :]
        r["conv"] = conv
    else:
        conv = conv_saved
    cc = conv - _rowmean(conv)
    rstd_c = lax.rsqrt(_rowmean(cc * cc) + EPS)
    xh_c = cc * rstd_c
    cn = xh_c * p384_ref[3:4, :] + p384_ref[4:5, :]
    sig_c = jax.nn.sigmoid(cn)
    yb = cn * sig_c
    r.update(a_in=a_in, sig_g=sig_g, hh=hh, xh_c=xh_c, rstd_c=rstd_c, cn=cn, sig_c=sig_c)
    o_c = o_b + 2 * CONV_WIDTH
    zc = z[:, o_c:o_c + POOL_WIDTH]
    zext_ref[0:POOL_HALO, :], r["z_rot"] = _causal_tail(zc[t - POOL_HALO:t, :], z_prev_rot)
    zext_ref[POOL_HALO:POOL_HALO + t, :] = zc
    sums, acc = [], zc
    for j in range(1, POOL_WINDOWS[-1]):
        acc = acc + zext_ref[pl.ds(POOL_HALO - 8 * j, t), :]
        if j + 1 in POOL_WINDOWS:
            sums.append(acc)
    inv = _pool_inv_counts(t, tile_idx)
    dpool = _by_pool_group((t, POOL_WIDTH), [s * iv for s, iv in zip(sums, inv)]) - zc
    ycp = _dot(dpool.astype(BF16), pwbd_ref[...])
    yc = ycp * psc_ref[0:1, :]
    r.update(dpool=dpool, ycp=ycp)
    yha, ra = _rms_fwd(ya)
    yhb, rb = _rms_fwd(yb)
    yhc, rc = _rms_fwd(yc)
    bg = g_ref[2:3, :]
    ycat = jnp.concatenate([yha * bg[:, 0:384], yhb * bg[:, 384:768], yhc * bg[:, 768:1024]], axis=1)
    r.update(yha=yha, ra=ra, yhb=yhb, rb=rb, yhc=yhc, rc=rc, ycat=ycat)
    return r


def _mixer_fwd(x, modv, g1024, p384, cw, sguw, bmat, pwbd, psc, segp, win, wout, name="mixer_fwd", job=None,
               natural_x=False, relay_steps_before_end=2):
    s_len = x.shape[0]
    t = MIX_TILE

    def body(x_ref, mod_ref, g_ref, p384_ref, cw_ref, sguw_ref, bmat_ref, pwbd_ref, psc_ref, segp_ref, win_ref, wout_ref,
             x1_ref, z_ref, o_ref, conv_ref, hext_ref, zext_ref, hrot_ref, zrot_ref, wm_ref):
        i = pl.program_id(0)

        @pl.when(i == 0)
        def _():
            hrot_ref[...] = jnp.zeros_like(hrot_ref)
            zrot_ref[...] = jnp.zeros_like(zrot_ref)
            for h, wmh in enumerate(_masked_sgu_w(sguw_ref)):
                wm_ref[h] = wmh

        xv = _interleave(x_ref[...]) if natural_x else x_ref[...]
        sh1, sc1, g1 = mod_ref[0:1, :], mod_ref[1:2, :], mod_ref[2:3, :]
        xhat, _ = _rms_fwd(xv)
        h1 = xhat * g_ref[0:1, :] * (1.0 + sc1) + sh1
        z = _dot_nt(h1.astype(BF16), win_ref[...])
        z_ref[...] = z
        r = _branches_fwd(z, i, p384_ref, cw_ref, [wm_ref[h] for h in range(SGU_HEADS)], bmat_ref, pwbd_ref, psc_ref,
                          segp_ref, g_ref, hext_ref, zext_ref, hrot_ref[...], zrot_ref[...])
        hrot_ref[...] = r["h_rot"]
        zrot_ref[...] = r["z_rot"]
        conv_ref[...] = r["conv"]
        o = _dot(r["ycat"].astype(BF16), wout_ref[...])
        o_ref[...] = o
        ohat, _ = _rms_fwd(o)
        x1_ref[...] = xv + g1 * (ohat * g_ref[1:2, :])

    tile = pl.BlockSpec((t, D_MODEL), lambda i: (i, 0))
    consts = (modv, g1024, p384, cw, sguw, bmat, pwbd, psc, segp, win, wout)
    return _pallas_call_with_exchange(
        body,
        grid=(s_len // t,),
        in_specs=[tile] + [_whole(c.shape) for c in consts],
        out_specs=[tile, pl.BlockSpec((t, IN_WIDTH), lambda i: (i, 0)), tile,
                   pl.BlockSpec((t, CONV_WIDTH), lambda i: (i, 0))],
        out_shape=[jax.ShapeDtypeStruct((s_len, D_MODEL), F32), jax.ShapeDtypeStruct((s_len, IN_WIDTH), F32),
                   jax.ShapeDtypeStruct((s_len, D_MODEL), F32), jax.ShapeDtypeStruct((s_len, CONV_WIDTH), F32)],
        scratch_shapes=[pltpu.VMEM((2 * t, CONV_WIDTH), F32), pltpu.VMEM((POOL_HALO + t, POOL_WIDTH), F32),
                        pltpu.VMEM((t, CONV_WIDTH), F32), pltpu.VMEM((POOL_HALO, POOL_WIDTH), F32),
                        pltpu.VMEM((SGU_HEADS, t, t), BF16)],
        operands=(x, *consts),
        name=name, job=job, relay_steps_before_end=relay_steps_before_end)


def _mixer_bwd(dx1, x, o, z, conv, modv, g1024, p384, cw, sguw, bmat, pwbd, psc, segp, win, wout, name="mixer_bwd",
               job=None, natural_x=False):
    s_len = x.shape[0]
    t = MIX_TILE
    n_tiles = s_len // t

    def body(dx1_ref, x_ref, o_ref, z_ref, zh_ref, conv_ref, mod_ref, g_ref, p384_ref, cw_ref, sguw_ref, bmat_ref, pwbd_ref,
             psc_ref, segp_ref, win_ref, wout_ref,
             dx_ref, dz_ref, do_ref, ycat_ref, h1_ref, vec_ref, v384_ref, dcw_ref, dsguw_out_ref, dbmat_out_ref, dpw_ref,
             dpsc_ref, zext_ref, gext_ref, qext_ref, grot_ref, qrot_ref, wm_ref, dsguw_ref, dbmat_ref):
        i = pl.program_id(0)
        tile_idx = n_tiles - 1 - i

        @pl.when(i == 0)
        def _():
            for ref in (vec_ref, v384_ref, dcw_ref, dsguw_ref, dbmat_ref, dpw_ref, dpsc_ref, grot_ref, qrot_ref):
                ref[...] = jnp.zeros_like(ref)
            for h, wmh in enumerate(_masked_sgu_w(sguw_ref)):
                wm_ref[h] = wmh

        dx1v, ov, z = dx1_ref[...], o_ref[...], z_ref[...]
        xv = _interleave(x_ref[...]) if natural_x else x_ref[...]
        sh1, sc1, g1 = mod_ref[0:1, :], mod_ref[1:2, :], mod_ref[2:3, :]
        pre_g, post_g, bg = g_ref[0:1, :], g_ref[1:2, :], g_ref[2:3, :]
        segp = segp_ref[...]

        ohat, ro = _rms_fwd(ov)
        vec_ref[1:2, :] += _colsum(dx1v * (ohat * post_g))
        don = dx1v * g1
        vec_ref[0:1, :] += _colsum(don * ohat)
        dob = _rms_bwd(don * post_g, ohat, ro).astype(BF16)
        do_ref[...] = dob
        dycat = _dot_nt(dob, wout_ref[...])

        not_first = (tile_idx > 0).astype(F32)
        o_b = 2 * SGU_WIDTH
        o_c = o_b + 2 * CONV_WIDTH
        z_prev_rot = _rot_rows(zh_ref[:, o_c:o_c + POOL_WIDTH], 1) * not_first
        wm = [wm_ref[h] for h in range(SGU_HEADS)]
        r = _branches_fwd(z, tile_idx, p384_ref, cw_ref, wm, bmat_ref, pwbd_ref, psc_ref, segp_ref, g_ref,
                          None, zext_ref, None, z_prev_rot, conv_saved=conv_ref[...])
        ycat_ref[...] = r["ycat"].astype(BF16)

        def branch_norm_bwd(dyn, yhat, rr, gain):
            return _colsum(dyn * yhat), _rms_bwd(dyn * gain, yhat, rr)

        dga, dya = branch_norm_bwd(dycat[:, 0:384], r["yha"], r["ra"], bg[:, 0:384])
        dgb, dyb = branch_norm_bwd(dycat[:, 384:768], r["yhb"], r["rb"], bg[:, 384:768])
        dgc, dyc = branch_norm_bwd(dycat[:, 768:1024], r["yhc"], r["rc"], bg[:, 768:1024])
        vec_ref[5:6, :] += jnp.concatenate([dga, dgb, dgc], axis=1)

        du_act = dya * r["f"]
        df = dya * r["u"]
        first_head = _lane((t, 128)) < HEAD_DIM
        dbmat_ref[...] += df
        dvn_pairs = []
        for pr in range(SGU_HEADS // 2):
            dfp = df[:, pr * 128:(pr + 1) * 128]
            df0 = jnp.where(first_head, dfp, 0.0).astype(BF16)
            df1 = jnp.where(first_head, 0.0, dfp).astype(BF16)
            vp = r["vnb"][:, pr * 128:(pr + 1) * 128]
            dvn_pairs.append(_dot_tn(wm[2 * pr], df0) + _dot_tn(wm[2 * pr + 1], df1))
            dsguw_ref[2 * pr] += _dot_nt(df0, vp)
            dsguw_ref[2 * pr + 1] += _dot_nt(df1, vp)
        dvn = jnp.concatenate(dvn_pairs, axis=1)
        v384_ref[0:1, :] += _colsum(dvn * r["xh_v"])
        v384_ref[1:2, :] += _colsum(dvn)
        dxh = dvn * p384_ref[0:1, :]
        dvraw = r["rstd_v"] * (dxh - _seg_mean(dxh, segp) - r["xh_v"] * _seg_mean(dxh * r["xh_v"], segp))
        zu, zv = z[:, 0:SGU_WIDTH], z[:, SGU_WIDTH:o_b]
        _, tu = _gelu(zu)
        _, tv = _gelu(zv)
        dz_u = du_act * _gelu_grad(zu, tu)
        dz_v = dvraw * _gelu_grad(zv, tv)

        cn, sig_c = r["cn"], r["sig_c"]
        dcn = dyb * (sig_c * (1.0 + cn * (1.0 - sig_c)))
        v384_ref[3:4, :] += _colsum(dcn * r["xh_c"])
        v384_ref[4:5, :] += _colsum(dcn)
        dxc = dcn * p384_ref[3:4, :]
        gconv = r["rstd_c"] * (dxc - _rowmean(dxc) - r["xh_c"] * _rowmean(dxc * r["xh_c"]))
        v384_ref[2:3, :] += _colsum(gconv)
        gext_ref[0:t, :] = gconv
        gext_ref[t:2 * t, :], grot_ref[...] = _lookahead_head(gconv, grot_ref[...])
        dhh = jnp.zeros((t, CONV_WIDTH), F32)
        hh = r["hh"]
        for k in range(CONV_K):
            shift = CONV_K - 1 - k
            g_ahead = gext_ref[pl.ds(8 * shift, t), :]
            dcw_ref[k:k + 1, :] += _colsum(g_ahead * hh)
            dhh = dhh + cw_ref[k:k + 1, :] * g_ahead
        sig_g = r["sig_g"]
        dz_a = dhh * sig_g
        dz_g = dhh * r["a_in"] * sig_g * (1.0 - sig_g)

        dpsc_ref[0:1, :] += _colsum(dyc * r["ycp"])
        dycp = (dyc * psc_ref[0:1, :]).astype(BF16)
        dpw_ref[...] += _dot_tn(r["dpool"].astype(BF16), dycp)
        ddp = _dot_nt(dycp, pwbd_ref[...])
        inv = _pool_inv_counts(t, tile_idx)
        q = ddp * _by_pool_group((t, POOL_WIDTH), inv)
        qext_ref[0:t, :] = q
        qext_ref[t:t + POOL_HALO, :], qrot_ref[...] = _lookahead_head(q[0:POOL_HALO, :], qrot_ref[...])
        sums, acc = [], q
        for j in range(1, POOL_WINDOWS[-1]):
            acc = acc + qext_ref[pl.ds(8 * j, t), :]
            if j + 1 in POOL_WINDOWS:
                sums.append(acc)
        dz_c = _by_pool_group((t, POOL_WIDTH), sums) - ddp

        dzb = jnp.concatenate([dz_u, dz_v, dz_a, dz_g, dz_c], axis=1).astype(BF16)
        dz_ref[...] = dzb
        dh1 = _dot(dzb, win_ref[...])

        xhat, rx = _rms_fwd(xv)
        xn = xhat * pre_g
        h1_ref[...] = (xn * (1.0 + sc1) + sh1).astype(BF16)
        vec_ref[2:3, :] += _colsum(dh1)
        vec_ref[3:4, :] += _colsum(dh1 * xn)
        dxn = dh1 * (1.0 + sc1)
        vec_ref[4:5, :] += _colsum(dxn * xhat)
        dx = dx1v + _rms_bwd(dxn * pre_g, xhat, rx)
        dx_ref[...] = _deinterleave(dx) if natural_x else dx

        @pl.when(i == n_tiles - 1)
        def _():
            keep = _sgu_keep_mask(t)
            for h in range(SGU_HEADS):
                rows_natural = _deinterleave(jnp.where(keep, dsguw_ref[h], 0.0))
                natural = _deinterleave(rows_natural.T).T
                dsguw_out_ref[h] = sum(natural[b * CHUNK:(b + 1) * CHUNK, b * CHUNK:(b + 1) * CHUNK]
                                       for b in range(t // CHUNK))
            dbmat = _deinterleave(float(HEAD_DIM) * _seg_mean(dbmat_ref[...], segp))
            dbmat_out_ref[...] = sum(dbmat[b * CHUNK:(b + 1) * CHUNK, :] for b in range(t // CHUNK))

    rev = lambda i: (n_tiles - 1 - i, 0)
    tile = pl.BlockSpec((t, D_MODEL), rev)
    ztile = pl.BlockSpec((t, IN_WIDTH), rev)
    zhalo = pl.BlockSpec((POOL_HALO, IN_WIDTH),
                         lambda i: (jnp.maximum((n_tiles - 1 - i) * (t // POOL_HALO) - 1, 0), 0))
    consts = (modv, g1024, p384, cw, sguw, bmat, pwbd, psc, segp, win, wout)
    acc = lambda shape: pl.BlockSpec(shape, lambda i: (0,) * len(shape))
    acc_shapes = [(8, D_MODEL), (8, SGU_WIDTH), (32, CONV_WIDTH), (SGU_HEADS, CHUNK, CHUNK), (CHUNK, SGU_WIDTH),
                  (POOL_WIDTH, POOL_WIDTH), (8, POOL_WIDTH)]
    return _pallas_call_with_exchange(
        body,
        grid=(n_tiles,),
        in_specs=[tile, tile, tile, ztile, zhalo, pl.BlockSpec((t, CONV_WIDTH), rev)] + [_whole(c.shape) for c in consts],
        out_specs=[tile, ztile, tile, tile, tile] + [acc(s) for s in acc_shapes],
        out_shape=[jax.ShapeDtypeStruct((s_len, D_MODEL), F32), jax.ShapeDtypeStruct((s_len, IN_WIDTH), BF16),
                   jax.ShapeDtypeStruct((s_len, D_MODEL), BF16), jax.ShapeDtypeStruct((s_len, D_MODEL), BF16),
                   jax.ShapeDtypeStruct((s_len, D_MODEL), BF16)] + [jax.ShapeDtypeStruct(s, F32) for s in acc_shapes],
        scratch_shapes=[pltpu.VMEM((POOL_HALO + t, POOL_WIDTH), F32),
                        pltpu.VMEM((2 * t, CONV_WIDTH), F32), pltpu.VMEM((t + POOL_HALO, POOL_WIDTH), F32),
                        pltpu.VMEM((t, CONV_WIDTH), F32), pltpu.VMEM((POOL_HALO, POOL_WIDTH), F32),
                        pltpu.VMEM((SGU_HEADS, t, t), BF16), pltpu.VMEM((SGU_HEADS, t, t), F32),
                        pltpu.VMEM((t, SGU_WIDTH), F32)],
        operands=(dx1, x, o, z, z, conv, *consts),
        name=name, job=job)


MOD_SHARD = 6 * D_MODEL // N_DEV

ROW_DMOD = 0
ROW_G1024 = 8
ROW_V384 = 16
ROW_SGU_B = 24
ROW_POOL_SCALE = 25
ROW_LOSS = 26
ROW_CONV_W = 32
ROW_FFN_CONV = 64
ROW_POOL_W = 96
ROW_SGU_W = 112
ROWS_PER_LAYER = 208
N_LAYERS = 2


def _gather_weights(c8, mod_w, mod_b8, job):
    kinds = [kind for kind, _ in job]
    shards = [a for _, a in job]
    n = len(shards)

    def body(c_ref, modw_ref, modb_ref, *rest):
        shard_refs = rest[:n]
        sc_all_ref, modrows_ref = rest[n], rest[n + 1]
        full_refs = rest[n + 2:2 * n + 2]
        send_buf, mod_recv, w_send, w_recv, w_local, sc_send, sc_recv, mod_send, mod_recv_sem = rest[2 * n + 2:]
        pos = _my_pos()
        me = _flat(pos)
        peers = [_peer(pos, k) for k in range(1, N_DEV)]

        w_copies = _exchange_copies(kinds, shard_refs, full_refs, w_send, w_recv, w_local, phase=0)
        for cp in w_copies:
            cp.start()

        cv = c_ref[...]
        sc_all_ref[me] = cv * jax.nn.sigmoid(cv)
        sc_copies = [_remote_copy(sc_all_ref.at[me], sc_all_ref.at[me], sc_send.at[k], sc_recv.at[k], peers[k])
                     for k in range(N_PEERS)]
        for cp in sc_copies:
            cp.start()
        for cp in sc_copies:
            cp.wait()

        sc = jnp.concatenate([sc_all_ref[j, 0:1, :] for j in range(N_DEV)], axis=0)
        send_buf[...] = jnp.zeros_like(send_buf)
        for l in range(N_LAYERS):
            part = jnp.dot(sc, modw_ref[l], precision=lax.Precision.HIGHEST, preferred_element_type=F32)
            for j in range(N_DEV):
                send_buf[j, l:l + 1, :] = part[j:j + 1, :]
        mod_recv[me] = send_buf[me]
        mod_copies = [_remote_copy(send_buf.at[_flat(peers[k])], mod_recv.at[me], mod_send.at[k], mod_recv_sem.at[k],
                                   peers[k]) for k in range(N_PEERS)]
        for cp in mod_copies:
            cp.start()
        for cp in mod_copies:
            cp.wait()
        modrows_ref[...] = jnp.zeros_like(modrows_ref)
        for l in range(N_LAYERS):
            row = jnp.concatenate([mod_recv[j, l:l + 1, :] for j in range(N_DEV)], axis=1)
            modrows_ref[l:l + 1, :] = row + modb_ref[l:l + 1, :]

        for cp in w_copies:
            cp.wait()
        relays = _exchange_copies(kinds, shard_refs, full_refs, w_send, w_recv, w_local, phase=1)
        for cp in relays:
            cp.start()
        for cp in relays:
            cp.wait()

    out_shape = ([jax.ShapeDtypeStruct((N_DEV, 8, D_MODEL), F32), jax.ShapeDtypeStruct((8, 6 * D_MODEL), F32)]
                 + [jax.ShapeDtypeStruct((N_DEV,) + s.shape, s.dtype) for s in shards])
    return pl.pallas_call(
        body,
        in_specs=[VMEM, VMEM, VMEM] + [ANY] * n,
        out_specs=[VMEM, VMEM] + [ANY] * n,
        out_shape=out_shape,
        scratch_shapes=[pltpu.VMEM((N_DEV, 8, MOD_SHARD), F32), pltpu.VMEM((N_DEV, 8, MOD_SHARD), F32),
                        pltpu.SemaphoreType.DMA((n, N_PEERS)), pltpu.SemaphoreType.DMA((n, N_PEERS)),
                        pltpu.SemaphoreType.DMA((n,)),
                        pltpu.SemaphoreType.DMA((N_PEERS,)), pltpu.SemaphoreType.DMA((N_PEERS,)),
                        pltpu.SemaphoreType.DMA((N_PEERS,)), pltpu.SemaphoreType.DMA((N_PEERS,))],
        compiler_params=pltpu.CompilerParams(vmem_limit_bytes=VMEM_LIMIT_BYTES),
        name="gather_weights",
    )(c8, mod_w, mod_b8, *shards)


def _small_sums(sc_all, small_all0, small_all1, job):
    kinds = [kind for kind, _ in job]
    n = len(job)

    def body(sc_all_ref, small_all0_ref, small_all1_ref, *rest):
        small_sum_ref, gmodw_ref = rest[n], rest[n + 1]
        copies = _exchange_copies(kinds, rest[:n], rest[n + 2:2 * n + 2], *rest[2 * n + 2:], phase=0)
        for cp in copies:
            cp.start()
        me = _flat(_my_pos())
        sc = jnp.concatenate([sc_all_ref[j, 0:1, :] for j in range(N_DEV)], axis=0)
        mine = lax.broadcasted_iota(jnp.int32, (2 * N_DEV, MOD_SHARD), 0) == me
        for l, parts in enumerate((small_all0_ref, small_all1_ref)):
            total = parts[0].astype(F32)
            for j in range(1, N_DEV):
                total = total + parts[j].astype(F32)
            small_sum_ref[l] = total
            dm = jnp.concatenate(
                [jnp.sum(jnp.where(mine, parts[j, ROW_DMOD:ROW_DMOD + 2 * N_DEV, 0:MOD_SHARD].astype(F32), 0.0),
                         axis=0, keepdims=True) for j in range(N_DEV)], axis=0)
            gmodw_ref[l] = lax.dot_general(sc, dm, (((0,), (0,)), ((), ())), precision=lax.Precision.HIGHEST,
                                           preferred_element_type=F32)
        for cp in copies:
            cp.wait()

    res = pl.pallas_call(
        body,
        in_specs=[VMEM, VMEM, VMEM] + [ANY] * n,
        out_specs=[VMEM, VMEM] + [ANY] * n,
        out_shape=[jax.ShapeDtypeStruct((N_LAYERS, ROWS_PER_LAYER, D_MODEL), F32),
                   jax.ShapeDtypeStruct((N_LAYERS, D_MODEL, MOD_SHARD), F32)] + _exchange_out_shapes(job),
        scratch_shapes=_exchange_sems(n),
        compiler_params=pltpu.CompilerParams(vmem_limit_bytes=VMEM_LIMIT_BYTES),
        name="small_sums",
    )(sc_all, small_all0, small_all1, *[a for _, a in job])
    return res[0], res[1], res[2:]


def _adam_update(g, w, m, v):
    m2 = ADAM_B1 * m + (1.0 - ADAM_B1) * g
    v2 = ADAM_B2 * v + (1.0 - ADAM_B2) * (g * g)
    m_hat = m2 / (1.0 - ADAM_B1 ** ADAM_STEP)
    v_hat = v2 / (1.0 - ADAM_B2 ** ADAM_STEP)
    delta = -ADAM_LR * (m_hat / (jnp.sqrt(v_hat) + ADAM_EPS) + ADAM_WD * w)
    return delta, m2, v2


def _pair_add(g, r1, row_chunk, name):
    _, rows, cols = g.shape
    core = lax.axis_index("c").astype(jnp.int32).reshape(1)

    def body(core_ref, g_ref, r_ref, o_ref):
        o_ref[0] = (g_ref[0, 0].astype(F32) + r_ref[0].astype(F32)).astype(BF16)

    blk = pl.BlockSpec((1, row_chunk, cols), lambda q, i, core_ref: (q, i, 0))
    grid_spec = pltpu.PrefetchScalarGridSpec(
        num_scalar_prefetch=1, grid=(N_CHIPS, rows // row_chunk),
        in_specs=[pl.BlockSpec((1, 1, row_chunk, cols), lambda q, i, core_ref: (q, core_ref[0], i, 0)), blk],
        out_specs=blk)
    return pl.pallas_call(
        body, grid_spec=grid_spec, out_shape=jax.ShapeDtypeStruct((N_CHIPS, rows, cols), BF16),
        compiler_params=_cparams(2), name=name,
    )(core, g.reshape(N_CHIPS, 2, rows, cols), r1)


def _adam_sharded(recv0, recv1, w, m, v, row_chunk, name):
    _, rows, cols = w.shape
    n_chunks = rows // row_chunk

    def body(r0_ref, r1_ref, w_ref, m_ref, v_ref, g_ref, d_ref, m2_ref, v2_ref):
        layer = pl.program_id(0)

        def run(r_ref):
            g = r_ref[0].astype(F32)
            for j in range(1, r_ref.shape[0]):
                g = g + r_ref[j].astype(F32)
            delta, m2, v2 = _adam_update(g, w_ref[0], m_ref[0], v_ref[0])
            g_ref[0], d_ref[0], m2_ref[0], v2_ref[0] = g, delta, m2, v2

        @pl.when(layer == 0)
        def _():
            run(r0_ref)

        @pl.when(layer == 1)
        def _():
            run(r1_ref)

    r0_spec = pl.BlockSpec((recv0.shape[0], row_chunk, cols), lambda l, i: (0, i * (1 - l) + (n_chunks - 1) * l, 0))
    r1_spec = pl.BlockSpec((recv1.shape[0], row_chunk, cols), lambda l, i: (0, i * l, 0))
    blk = pl.BlockSpec((1, row_chunk, cols), lambda l, i: (l, i, 0))
    out = jax.ShapeDtypeStruct(w.shape, F32)
    return pl.pallas_call(
        body,
        grid=(N_LAYERS, n_chunks),
        in_specs=[r0_spec, r1_spec, blk, blk, blk],
        out_specs=[blk] * 4,
        out_shape=[out] * 4,
        compiler_params=_cparams(2),
        name=name,
    )(recv0, recv1, w, m, v)


def _adam_dense(g, w, m, v, row_chunk, name):
    n_lead, rows, cols = w.shape

    def body(g_ref, w_ref, m_ref, v_ref, go_ref, d_ref, m2_ref, v2_ref):
        gv = g_ref[...]
        go_ref[...] = gv
        d_ref[...], m2_ref[...], v2_ref[...] = _adam_update(gv, w_ref[...], m_ref[...], v_ref[...])

    blk = pl.BlockSpec((1, row_chunk, cols), lambda l, i: (l, i, 0))
    out = jax.ShapeDtypeStruct(w.shape, F32)
    return pl.pallas_call(
        body,
        grid=(n_lead, rows // row_chunk),
        in_specs=[blk] * 4,
        out_specs=[blk] * 4,
        out_shape=[out] * 4,
        compiler_params=_cparams(2),
        name=name,
    )(g, w, m, v)


WEIGHT_NAMES = ("mod_w", "mod_b", "mix_pre_g", "mix_post_g", "w_in", "sgu_norm_g", "sgu_norm_b", "sgu_w", "sgu_b",
                "conv_w", "conv_b", "conv_norm_g", "conv_norm_b", "pool_w", "pool_scale", "branch_g", "w_out",
                "ffn_pre_g", "ffn_post_g", "ffn_up", "ffn_conv_w", "ffn_conv_b", "ffn_down")
SHARDED_BIG = ("w_in", "w_out", "ffn_up", "ffn_down")
SMALL_PACKED = tuple(n for n in WEIGHT_NAMES if n not in SHARDED_BIG + ("mod_w",))


def _rows8(rows, width=D_MODEL):
    out = [jnp.pad(r.astype(F32), (0, width - r.shape[0]))[None] for r in rows]
    out.append(jnp.zeros((8 - len(rows), width), F32))
    return jnp.concatenate(out, axis=0)


def _as_rows(a, width=D_MODEL):
    flat = a.astype(F32).reshape(-1)
    pad = (-flat.shape[0]) % width
    return jnp.pad(flat, (0, pad)).reshape(-1, width)


def _pad_cols(a, width=D_MODEL):
    return jnp.pad(a.astype(F32), ((0, 0), (0, width - a.shape[1])))


def _pack_rows(arrays):
    rows = jnp.concatenate([_as_rows(a) for a in arrays], axis=0)
    return jnp.pad(rows, ((0, (-rows.shape[0]) % 8), (0, 0)))


def _unpack_rows(packed, shapes):
    out, r = [], 0
    for shape in shapes:
        size = math.prod(shape)
        n_rows = -(-size // D_MODEL)
        out.append(packed[r:r + n_rows].reshape(-1)[:size].reshape(shape))
        r += n_rows
    return out


TILE_VREGS = MIX_TILE // 8
CHUNK_SUBLANES = CHUNK // TILE_VREGS
CHUNKS_PER_TILE = MIX_TILE // CHUNK


def _chunk_axis_to_tile(a, axis):
    shape = a.shape
    a = a.reshape(shape[:axis] + (CHUNK_SUBLANES, TILE_VREGS) + shape[axis + 1:])
    a = jnp.swapaxes(a, axis, axis + 1)
    a = jnp.tile(a, (1,) * (axis + 1) + (CHUNKS_PER_TILE,) + (1,) * (len(shape) - axis - 1))
    return a.reshape(shape[:axis] + (MIX_TILE,) + shape[axis + 1:])


def _layer_consts(l, w, mod_rows, win, wout, conv_w_full):
    modv = _rows8(list(mod_rows[l].reshape(6, D_MODEL)))
    g1024 = _rows8([w["mix_pre_g"][l], w["mix_post_g"][l], w["branch_g"][l], w["ffn_pre_g"][l], w["ffn_post_g"][l]])
    p384 = _rows8([w["sgu_norm_g"][l], w["sgu_norm_b"][l], w["conv_b"][l], w["conv_norm_g"][l], w["conv_norm_b"][l]],
                  SGU_WIDTH)
    cw = jnp.pad(conv_w_full[l], ((0, 32 - CONV_K), (0, 0)))
    sguw = _chunk_axis_to_tile(_chunk_axis_to_tile(w["sgu_w"][l], 2), 1)
    bmat = _chunk_axis_to_tile(jnp.repeat(w["sgu_b"][l].T, HEAD_DIM, axis=1), 0)
    groups = len(POOL_WINDOWS)
    eye = jnp.eye(groups, dtype=F32)
    pwbd = (eye[:, None, :, None] * w["pool_w"][l][:, :, None, :]).reshape(POOL_WIDTH, POOL_WIDTH).astype(BF16)
    psc = _rows8([w["pool_scale"][l]], POOL_WIDTH)
    seg = jnp.arange(SGU_WIDTH) // HEAD_DIM
    segp = jnp.where(seg[:, None] == seg[None, :], 1.0 / HEAD_DIM, 0.0).astype(BF16)
    return modv, g1024, (modv, g1024, p384, cw, sguw, bmat, pwbd, psc, segp, win, wout)


def _small_grad_rows(mix, ffn, loss_pieces=None):
    _, _, _, _, _, mvec, v384, dcw, dsguw, dbmat, dpw, dpsc = mix
    fvec, cgrad = ffn[5], ffn[6]
    dmod = jnp.stack([mvec[2], mvec[3], mvec[1], fvec[2], fvec[3], fvec[1]]).reshape(N_DEV, MOD_SHARD)
    g_rows = jnp.stack([mvec[4], mvec[0], mvec[5], fvec[4], fvec[0]])
    dsgu_b = dbmat[:, ::HEAD_DIM].T.reshape(1, SGU_HEADS * CHUNK)
    groups = len(POOL_WINDOWS)
    gdim = POOL_WIDTH // groups
    dpw4 = dpw.reshape(groups, gdim, groups, gdim)
    dpool = jnp.stack([dpw4[g, :, g, :] for g in range(groups)])
    misc = [dsgu_b[0], dpsc[0]] + ([] if loss_pieces is None else [loss_pieces])
    blocks = [_pad_cols(dmod), _rows8(list(g_rows)), _pad_cols(v384), _rows8(misc), _pad_cols(dcw),
              _pad_cols(cgrad[:, 0:4, :].reshape(4 * N_DEV, FF_SHARD)), _as_rows(dpool), _as_rows(dsguw)]
    return jnp.concatenate(blocks, axis=0)


def _small_grads_from_rows(total):
    per = {n: [] for n in SMALL_PACKED}
    for l in range(N_LAYERS):
        s = total[l]
        per["mod_b"].append(s[ROW_DMOD:ROW_DMOD + N_DEV, :MOD_SHARD].reshape(6 * D_MODEL))
        for j, name in enumerate(("mix_pre_g", "mix_post_g", "branch_g", "ffn_pre_g", "ffn_post_g")):
            per[name].append(s[ROW_G1024 + j])
        for j, name in enumerate(("sgu_norm_g", "sgu_norm_b", "conv_b", "conv_norm_g", "conv_norm_b")):
            per[name].append(s[ROW_V384 + j, :SGU_WIDTH])
        per["sgu_b"].append(s[ROW_SGU_B, :SGU_HEADS * CHUNK].reshape(SGU_HEADS, CHUNK))
        per["pool_scale"].append(s[ROW_POOL_SCALE, :POOL_WIDTH])
        per["conv_w"].append(s[ROW_CONV_W:ROW_CONV_W + CONV_K, :CONV_WIDTH])
        fc = s[ROW_FFN_CONV:ROW_FFN_CONV + 4 * N_DEV, :FF_SHARD].reshape(N_DEV, 4, FF_SHARD)
        per["ffn_conv_w"].append(fc[:, 0:3, :].transpose(1, 0, 2).reshape(FFN_CONV_K, 2 * D_FF))
        per["ffn_conv_b"].append(fc[:, 3, :].reshape(2 * D_FF))
        per["pool_w"].append(s[ROW_POOL_W:ROW_POOL_W + 16].reshape(len(POOL_WINDOWS), HEAD_DIM, HEAD_DIM))
        per["sgu_w"].append(s[ROW_SGU_W:ROW_SGU_W + 96].reshape(SGU_HEADS, CHUNK, CHUNK))
    return {n: jnp.stack(v) for n, v in per.items()}


def kernel(x, c, mod_w, mod_b, mix_pre_g, mix_post_g, w_in, sgu_norm_g, sgu_norm_b, sgu_w, sgu_b, conv_w, conv_b, conv_norm_g, conv_norm_b, pool_w, pool_scale, branch_g, w_out, ffn_pre_g, ffn_post_g, ffn_up, ffn_conv_w, ffn_conv_b, ffn_down, loss_target, m_mod_w, m_mod_b, m_mix_pre_g, m_mix_post_g, m_w_in, m_sgu_norm_g, m_sgu_norm_b, m_sgu_w, m_sgu_b, m_conv_w, m_conv_b, m_conv_norm_g, m_conv_norm_b, m_pool_w, m_pool_scale, m_branch_g, m_w_out, m_ffn_pre_g, m_ffn_post_g, m_ffn_up, m_ffn_conv_w, m_ffn_conv_b, m_ffn_down, v_mod_w, v_mod_b, v_mix_pre_g, v_mix_post_g, v_w_in, v_sgu_norm_g, v_sgu_norm_b, v_sgu_w, v_sgu_b, v_conv_w, v_conv_b, v_conv_norm_g, v_conv_norm_b, v_pool_w, v_pool_scale, v_branch_g, v_w_out, v_ffn_pre_g, v_ffn_post_g, v_ffn_up, v_ffn_conv_w, v_ffn_conv_b, v_ffn_down):
    w = dict(zip(WEIGHT_NAMES, (mod_w, mod_b, mix_pre_g, mix_post_g, w_in, sgu_norm_g, sgu_norm_b, sgu_w, sgu_b, conv_w,
                                conv_b, conv_norm_g, conv_norm_b, pool_w, pool_scale, branch_g, w_out, ffn_pre_g,
                                ffn_post_g, ffn_up, ffn_conv_w, ffn_conv_b, ffn_down)))
    m = dict(zip(WEIGHT_NAMES, (m_mod_w, m_mod_b, m_mix_pre_g, m_mix_post_g, m_w_in, m_sgu_norm_g, m_sgu_norm_b, m_sgu_w,
                                m_sgu_b, m_conv_w, m_conv_b, m_conv_norm_g, m_conv_norm_b, m_pool_w, m_pool_scale,
                                m_branch_g, m_w_out, m_ffn_pre_g, m_ffn_post_g, m_ffn_up, m_ffn_conv_w, m_ffn_conv_b,
                                m_ffn_down)))
    v = dict(zip(WEIGHT_NAMES, (v_mod_w, v_mod_b, v_mix_pre_g, v_mix_post_g, v_w_in, v_sgu_norm_g, v_sgu_norm_b, v_sgu_w,
                                v_sgu_b, v_conv_w, v_conv_b, v_conv_norm_g, v_conv_norm_b, v_pool_w, v_pool_scale,
                                v_branch_g, v_w_out, v_ffn_pre_g, v_ffn_post_g, v_ffn_up, v_ffn_conv_w, v_ffn_conv_b,
                                v_ffn_down)))
    me = _flat(_my_pos())
    xs = x[0]

    transposed = ("w_in", "ffn_up")
    wt = {n: jnp.swapaxes(w[n], 1, 2) if n in transposed else w[n] for n in SHARDED_BIG}
    mt = {n: jnp.swapaxes(m[n], 1, 2) if n in transposed else m[n] for n in SHARDED_BIG}
    vt = {n: jnp.swapaxes(v[n], 1, 2) if n in transposed else v[n] for n in SHARDED_BIG}
    bf16_shards = [[wt[n][l].astype(BF16) for n in SHARDED_BIG] for l in range(N_LAYERS)]

    def mixer_operands(l, win_g, wout_g):
        win = win_g.reshape(IN_WIDTH, D_MODEL)
        return _layer_consts(l, w, mod_rows, win, wout_g.reshape(D_MODEL, D_MODEL), conv_w_full)

    def ffn_operands(l, modv, g1024, wup_g, wdn_g):
        wdn = wdn_g.reshape(FF_PAIRS, FF_SHARD, D_MODEL)
        return modv, g1024, wup_g, wdn, ffn_cw_full[:, l], ffn_conv_b[l].reshape(N_DEV, 1, FF_SHARD)

    w0, w1 = bf16_shards
    c8 = jnp.broadcast_to(c, (8, D_MODEL))
    mod_b8 = jnp.pad(mod_b, ((0, 8 - N_LAYERS), (0, 0)))
    sc_all, mod_rows, win0_g, wout0_g, conv_w_g, ffn_cw_full = _gather_weights(
        c8, mod_w, mod_b8, [("gather2", w0[0]), ("gather2", w0[1]), ("gather", conv_w), ("gather", ffn_conv_w)])
    conv_w_full = conv_w_g.transpose(1, 2, 0, 3).reshape(N_LAYERS, CONV_K, CONV_WIDTH)

    modv0, g0, mix_consts0 = mixer_operands(0, win0_g, wout0_g)
    (x1, z, o, cv), (wup0_g, wdn0_g) = _mixer_fwd(xs, *mix_consts0, name="mixer_fwd_l0", natural_x=True,
                                                  job=[("gather2", w0[2]), ("gather2", w0[3])],
                                                  relay_steps_before_end=1)
    ffn_consts0 = ffn_operands(0, modv0, g0, wup0_g, wdn0_g)
    (x2, y2, p, u), (win1_g, wout1_g, wdn1_g) = _ffn_fwd(
        x1, *ffn_consts0, name="ffn_fwd_l0", job=[("gather2", w1[0]), ("gather2", w1[1]), ("gather2", w1[3])])
    saved = [(xs, z, o, cv, x1, y2, p, u)]
    modv1, g1, mix_consts1 = mixer_operands(1, win1_g, wout1_g)
    (x1, z, o, cv), (wup1_g,) = _mixer_fwd(x2, *mix_consts1, name="mixer_fwd_l1", job=[("gather2", w1[2])])
    ffn_consts1 = ffn_operands(1, modv1, g1, wup1_g, wdn1_g)
    (dh, y2, p, u, loss_tile), _ = _ffn_fwd(x1, *ffn_consts1, name="ffn_fwd_l1", loss_target=loss_target[0])
    saved.append((x2, z, o, cv, x1, y2, p, u))
    loss_local = jnp.sum(loss_tile)
    loss_hi = loss_local.astype(BF16).astype(F32)
    loss_mid = (loss_local - loss_hi).astype(BF16).astype(F32)
    loss_pieces = jnp.stack([loss_hi, loss_mid, loss_local - loss_hi - loss_mid])

    def ffn_weight_grads(l, ffn):
        dp, a, dy2, h2 = ffn[1:5]
        d_up = _wgrad(dp, h2, f"wgrad_ffn_up_l{l}", tk=WGRAD_TK_FFN)
        d_dn = _wgrad(a, dy2, f"wgrad_ffn_down_l{l}", tk=WGRAD_TK_FFN).reshape(N_DEV, D_FF // N_DEV, D_MODEL)
        return d_up, d_dn

    def mixer_weight_grads(l, mix):
        dz, do, ycat, h1 = mix[1:5]
        d_in = _wgrad(dz[None], h1, f"wgrad_w_in_l{l}").reshape(N_DEV, IN_WIDTH // N_DEV, D_MODEL)
        d_out = _wgrad(ycat, do[None], f"wgrad_w_out_l{l}").reshape(N_DEV, D_MODEL // N_DEV, D_MODEL)
        return d_in, d_out

    x_in, z, o, cv, x1, y2, p, u = saved[1]
    ffn1, _ = _ffn_bwd(dh, x1, y2, p, u, *ffn_consts1[:-1], name="ffn_bwd_l1")
    d_up1, d_dn1 = ffn_weight_grads(1, ffn1)
    mix1, (sib_up1, sib_dn1) = _mixer_bwd(ffn1[0], x_in, o, z, cv, *mix_consts1, name="mixer_bwd_l1",
                                          job=[("scatter_p1", d_up1), ("scatter_p1", d_dn1)])
    chip_up1 = _pair_add(d_up1, sib_up1, 176, "pair_add_ffn_up_l1")
    chip_dn1 = _pair_add(d_dn1, sib_dn1, 176, "pair_add_ffn_down_l1")
    d_in1, d_out1 = mixer_weight_grads(1, mix1)
    small1 = _small_grad_rows(mix1, ffn1).astype(BF16)

    x_in, z, o, cv, x1, y2, p, u = saved[0]
    ffn0, job_out = _ffn_bwd(mix1[0], x1, y2, p, u, *ffn_consts0[:-1], name="ffn_bwd_l0",
                             job=[("scatter", d_in1), ("scatter", d_out1), ("scatter_p2", chip_up1),
                                  ("scatter_p2", chip_dn1)])
    recv1 = job_out
    dp, a, dy2, h2 = ffn0[1:5]
    d_dn0, (small_all1,) = _wgrad(a, dy2, "wgrad_ffn_down_l0", tk=WGRAD_TK_FFN, job=[("gather2", small1)])
    d_dn0 = d_dn0.reshape(N_DEV, D_FF // N_DEV, D_MODEL)
    d_up0, (recv_dn0,) = _wgrad(dp, h2, "wgrad_ffn_up_l0", tk=WGRAD_TK_FFN, job=[("scatter", d_dn0)])
    mix0, (recv_up0,) = _mixer_bwd(ffn0[0], x_in, o, z, cv, *mix_consts0, name="mixer_bwd_l0", natural_x=True,
                                   job=[("scatter", d_up0)])
    recv_ffn0 = (recv_up0, recv_dn0)
    grad_x = mix0[0][None]
    dz, do, ycat, h1 = mix0[1:5]
    small0 = _small_grad_rows(mix0, ffn0, loss_pieces).astype(BF16)
    d_in0, (small_all0,) = _wgrad(dz[None], h1, "wgrad_w_in_l0", tk=WGRAD_TK // 2, job=[("gather2", small0)])
    d_in0 = d_in0.reshape(N_DEV, IN_WIDTH // N_DEV, D_MODEL)
    d_out0, (recv_in0,) = _wgrad(ycat, do[None], "wgrad_w_out_l0", tk=WGRAD_TK // 2, job=[("scatter", d_in0)])
    d_out0 = d_out0.reshape(N_DEV, D_MODEL // N_DEV, D_MODEL)
    small_total, g_mod_w, (recv_out0,) = _small_sums(sc_all, small_all0, small_all1, [("scatter", d_out0)])
    recv0 = [recv_in0, recv_out0, recv_ffn0[0], recv_ffn0[1]]
    loss = small_total[0, ROW_LOSS, 0] + small_total[0, ROW_LOSS, 1] + small_total[0, ROW_LOSS, 2]

    grads, deltas, new_m, new_v = {}, {}, {}, {}
    for j, (name, chunk) in enumerate((("w_in", 224), ("w_out", 128), ("ffn_up", 176), ("ffn_down", 176))):
        outs = _adam_sharded(recv0[j], recv1[j], wt[name], mt[name], vt[name], chunk, "adam_" + name)
        if name in transposed:
            outs = [jnp.swapaxes(t, 1, 2) for t in outs]
        grads[name], deltas[name], new_m[name], new_v[name] = outs
    grads["mod_w"], deltas["mod_w"], new_m["mod_w"], new_v["mod_w"] = _adam_dense(
        g_mod_w, mod_w, m_mod_w, v_mod_w, 256, "adam_mod_w")

    small_g = _small_grads_from_rows(small_total)
    small_g["conv_w"] = lax.dynamic_slice_in_dim(small_g["conv_w"], me * conv_w.shape[2], conv_w.shape[2], axis=2)
    small_g["ffn_conv_w"] = lax.dynamic_slice_in_dim(small_g["ffn_conv_w"], me * FF_SHARD, FF_SHARD, axis=2)
    shapes = [w[n].shape for n in SMALL_PACKED]
    packs = [_pack_rows([src[n] for n in SMALL_PACKED])[None] for src in (small_g, w, m, v)]
    _, d, m2, v2 = _adam_dense(*packs, packs[0].shape[1], "adam_small")
    for name, dd, mm, vv in zip(SMALL_PACKED, _unpack_rows(d[0], shapes), _unpack_rows(m2[0], shapes),
                                _unpack_rows(v2[0], shapes)):
        grads[name], deltas[name], new_m[name], new_v[name] = small_g[name], dd, mm, vv

    return (loss, grad_x, *[grads[n] for n in WEIGHT_NAMES], *[deltas[n] for n in WEIGHT_NAMES],
            *[new_m[n] for n in WEIGHT_NAMES], *[new_v[n] for n in WEIGHT_NAMES])
```
